```python
import jax, jax.numpy as jnp
from jax import lax
import numpy as np

D_MODEL = 2048
BATCH = 8
SEQ = 2048
DEPTH = 4

CHUNK = 64
D_MIX = D_MODEL
D_GLA = D_MIX // 2
D_ATT = D_MIX - D_GLA
GLA_HEADS = 4
GLA_DK = D_GLA // 2 // GLA_HEADS
GLA_DV = D_GLA // GLA_HEADS
GLA_KW = GLA_HEADS * GLA_DK
GLA_GATE_RANK = 16
GLA_TAU = 16.0
ATT_HEADS = 8
ATT_HD = D_ATT // ATT_HEADS
LEFT_CHUNKS = 8
BAND = (LEFT_CHUNKS + 1) * CHUNK
REL_CLIP = 128
N_REL = 2 * REL_CLIP + 1
EPS = 1e-6

SPLIT_SIZES = (GLA_KW, GLA_KW, D_GLA, D_GLA, GLA_GATE_RANK, D_ATT, D_ATT, D_ATT, D_ATT)
D_IN = GLA_KW * 2 + D_GLA * 2 + GLA_GATE_RANK + D_ATT * 4

kernel_name = "hymba_gla_chunkattn_sandwich"


def rmsnorm(x, g):
    xf = x.astype(jnp.float32)
    y = xf * lax.rsqrt(jnp.mean(xf * xf, axis=-1, keepdims=True) + EPS) * g.astype(jnp.float32)
    return y.astype(x.dtype)


def split_columns(z):
    points = []
    acc = 0
    for s in SPLIT_SIZES[:-1]:
        acc += s
        points.append(acc)
    return jnp.split(z, points, axis=-1)


def gla_chunk_causal(q, k, v, log_a):
    out_dtype = v.dtype
    B, S, H, DK = q.shape
    DV = v.shape[-1]
    nc = S // CHUNK
    qf = q.astype(jnp.float32).reshape(B, nc, CHUNK, H, DK) * (DK ** -0.5)
    kf = k.astype(jnp.float32).reshape(B, nc, CHUNK, H, DK)
    vf = v.astype(jnp.float32).reshape(B, nc, CHUNK, H, DV)
    L = jnp.cumsum(log_a.astype(jnp.float32).reshape(B, nc, CHUNK, H, DK), axis=2)
    L_end = L[:, :, -1]
    k_dec = kf * jnp.exp(L_end[:, :, None] - L)
    U = jnp.einsum('bnchk,bnchv->bnhkv', k_dec, vf)
    A = jnp.exp(L_end)

    def step(state, inp):
        a, u = inp
        new = a[..., None] * state + u
        return new, new

    init = jnp.zeros((B, H, DK, DV), jnp.float32)
    _, states = lax.scan(step, init, (jnp.swapaxes(A, 0, 1), jnp.swapaxes(U, 0, 1)))
    states = jnp.swapaxes(states, 0, 1)
    o = jnp.einsum('bnchk,bnhkv->bnchv', qf, states)
    return o.reshape(B, S, H, DV).astype(out_dtype)


def chunk_band_attention(q, k, v, rel_bias):
    B, S, H, D = q.shape
    nc = S // CHUNK
    qc = q.reshape(B, nc, CHUNK, H, D)
    pad = ((0, 0), (LEFT_CHUNKS * CHUNK, 0), (0, 0), (0, 0))
    kp = jnp.pad(k, pad).reshape(B, nc + LEFT_CHUNKS, CHUNK, H, D)
    vp = jnp.pad(v, pad).reshape(B, nc + LEFT_CHUNKS, CHUNK, H, D)
    band_idx = jnp.arange(nc)[:, None] + jnp.arange(LEFT_CHUNKS + 1)[None, :]
    kb = kp[:, band_idx].reshape(B, nc, BAND, H, D)
    vb = vp[:, band_idx].reshape(B, nc, BAND, H, D)
    scores = jnp.einsum('bnqhd,bnkhd->bnhqk', qc, kb,
                        preferred_element_type=jnp.float32) * (D ** -0.5)
    qi = jnp.arange(CHUNK)[:, None] + LEFT_CHUNKS * CHUNK
    kj = jnp.arange(BAND)[None, :]
    rel = jnp.clip(qi - kj, -REL_CLIP, REL_CLIP) + REL_CLIP
    bias = rel_bias.astype(jnp.float32)[:, rel]
    key_chunk = band_idx - LEFT_CHUNKS
    valid = jnp.repeat(key_chunk >= 0, CHUNK, axis=1)
    scores = jnp.where(valid[None, :, None, None, :], scores + bias[None, None], -jnp.inf)
    p = jax.nn.softmax(scores, axis=-1).astype(v.dtype)
    o = jnp.einsum('bnhqk,bnkhd->bnqhd', p, vb)
    return o.reshape(B, S, H, D)


def hybrid_layer(x, w_in, w_out, g_pre, g_post, w_alpha, b_alpha, g_gla, g_att, rel_bias):
    B, S, _ = x.shape
    h = rmsnorm(x, g_pre)
    z = h @ w_in
    gq, gk, gv, gg, ga, aq, ak, av, ag = split_columns(z)
    log_a = jax.nn.log_sigmoid((ga @ w_alpha + b_alpha).astype(jnp.float32)) / GLA_TAU
    o_gla = gla_chunk_causal(gq.reshape(B, S, GLA_HEADS, GLA_DK),
                             gk.reshape(B, S, GLA_HEADS, GLA_DK),
                             gv.reshape(B, S, GLA_HEADS, GLA_DV),
                             log_a.reshape(B, S, GLA_HEADS, GLA_DK))
    o_gla = rmsnorm(o_gla, g_gla.reshape(GLA_HEADS, GLA_DV)).reshape(B, S, D_GLA)
    o_gla = o_gla * jax.nn.silu(gg)
    o_att = chunk_band_attention(aq.reshape(B, S, ATT_HEADS, ATT_HD),
                                 ak.reshape(B, S, ATT_HEADS, ATT_HD),
                                 av.reshape(B, S, ATT_HEADS, ATT_HD), rel_bias)
    o_att = rmsnorm(o_att, g_att.reshape(ATT_HEADS, ATT_HD)).reshape(B, S, D_ATT)
    o_att = o_att * jax.nn.silu(ag)
    y = jnp.concatenate([o_gla, o_att], axis=-1) @ w_out
    return x + rmsnorm(y, g_post)


def _fwd_setup_inputs(seed: int = 0) -> dict:
    key = jax.random.key(seed)
    ks = jax.random.split(key, 10)
    x = jax.random.normal(ks[0], (BATCH, SEQ, D_MODEL), jnp.float32)
    w_in = jax.random.normal(ks[1], (DEPTH, D_MODEL, D_IN), jnp.float32) * (D_MODEL ** -0.5)
    w_out = jax.random.normal(ks[2], (DEPTH, D_MIX, D_MODEL), jnp.float32) * (D_MIX ** -0.5)
    g_pre = 1.0 + 0.02 * jax.random.normal(ks[3], (DEPTH, D_MODEL), jnp.float32)
    g_post = 1.0 + 0.02 * jax.random.normal(ks[4], (DEPTH, D_MODEL), jnp.float32)
    w_alpha = jax.random.normal(ks[5], (DEPTH, GLA_GATE_RANK, GLA_KW), jnp.float32) * (GLA_GATE_RANK ** -0.5)
    b_alpha = 0.1 * jax.random.normal(ks[6], (DEPTH, GLA_KW), jnp.float32)
    g_gla = 1.0 + 0.02 * jax.random.normal(ks[7], (DEPTH, D_GLA), jnp.float32)
    g_att = 1.0 + 0.02 * jax.random.normal(ks[8], (DEPTH, D_ATT), jnp.float32)
    rel_bias = 0.1 * jax.random.normal(ks[9], (DEPTH, ATT_HEADS, N_REL), jnp.float32)
    return {"x": x, "w_in": w_in, "w_out": w_out, "g_pre": g_pre, "g_post": g_post,
            "w_alpha": w_alpha, "b_alpha": b_alpha, "g_gla": g_gla, "g_att": g_att,
            "rel_bias": rel_bias}


def _fwd_reference(x, w_in, w_out, g_pre, g_post, w_alpha, b_alpha, g_gla, g_att, rel_bias):
    h = x
    for l in range(DEPTH):
        h = hybrid_layer(h, w_in[l], w_out[l], g_pre[l], g_post[l], w_alpha[l], b_alpha[l],
                         g_gla[l], g_att[l], rel_bias[l])
    return h


import jax as _jax
import jax.numpy as _jnp

TWIN_FORMAT = 'train_step'
FWD_PARAMS = ['x', 'w_in', 'w_out', 'g_pre', 'g_post', 'w_alpha', 'b_alpha', 'g_gla', 'g_att', 'rel_bias']
TWIN_WEIGHTS = ['w_in', 'w_out', 'g_pre', 'g_post', 'w_alpha', 'b_alpha', 'g_gla', 'g_att', 'rel_bias']
TWIN_DIFF_INPUT = 'x'
TWIN_INPUTS = ['x', 'w_in', 'w_out', 'g_pre', 'g_post', 'w_alpha', 'b_alpha', 'g_gla', 'g_att', 'rel_bias', 'loss_target', 'm_w_in', 'm_w_out', 'm_g_pre', 'm_g_post', 'm_w_alpha', 'm_b_alpha', 'm_g_gla', 'm_g_att', 'm_rel_bias', 'v_w_in', 'v_w_out', 'v_g_pre', 'v_g_post', 'v_w_alpha', 'v_b_alpha', 'v_g_gla', 'v_g_att', 'v_rel_bias']
TWIN_OUTPUTS = ['loss', 'grad_x', 'grad_w_in', 'grad_w_out', 'grad_g_pre', 'grad_g_post', 'grad_w_alpha', 'grad_b_alpha', 'grad_g_gla', 'grad_g_att', 'grad_rel_bias', 'delta_w_in', 'delta_w_out', 'delta_g_pre', 'delta_g_post', 'delta_w_alpha', 'delta_b_alpha', 'delta_g_gla', 'delta_g_att', 'delta_rel_bias', 'new_m_w_in', 'new_m_w_out', 'new_m_g_pre', 'new_m_g_post', 'new_m_w_alpha', 'new_m_b_alpha', 'new_m_g_gla', 'new_m_g_att', 'new_m_rel_bias', 'new_v_w_in', 'new_v_w_out', 'new_v_g_pre', 'new_v_g_post', 'new_v_w_alpha', 'new_v_b_alpha', 'new_v_g_gla', 'new_v_g_att', 'new_v_rel_bias']
TWIN_LEAF_KINDS = {'loss': 'loss', 'grad_x': 'grad_x', 'grad_w_in': 'grad_w', 'grad_w_out': 'grad_w', 'grad_g_pre': 'grad_w', 'grad_g_post': 'grad_w', 'grad_w_alpha': 'grad_w', 'grad_b_alpha': 'grad_w', 'grad_g_gla': 'grad_w', 'grad_g_att': 'grad_w', 'grad_rel_bias': 'grad_w', 'delta_w_in': 'delta_w', 'delta_w_out': 'delta_w', 'delta_g_pre': 'delta_w', 'delta_g_post': 'delta_w', 'delta_w_alpha': 'delta_w', 'delta_b_alpha': 'delta_w', 'delta_g_gla': 'delta_w', 'delta_g_att': 'delta_w', 'delta_rel_bias': 'delta_w', 'new_m_w_in': 'new_m', 'new_m_w_out': 'new_m', 'new_m_g_pre': 'new_m', 'new_m_g_post': 'new_m', 'new_m_w_alpha': 'new_m', 'new_m_b_alpha': 'new_m', 'new_m_g_gla': 'new_m', 'new_m_g_att': 'new_m', 'new_m_rel_bias': 'new_m', 'new_v_w_in': 'new_v', 'new_v_w_out': 'new_v', 'new_v_g_pre': 'new_v', 'new_v_g_post': 'new_v', 'new_v_w_alpha': 'new_v', 'new_v_b_alpha': 'new_v', 'new_v_g_gla': 'new_v', 'new_v_g_att': 'new_v', 'new_v_rel_bias': 'new_v'}


def _forward(args):
    return _fwd_reference(*[args[k] for k in FWD_PARAMS])


def _output_shape():
    out = _jax.eval_shape(lambda: _forward(_fwd_setup_inputs(0)))
    return out.shape, out.dtype

N_MICROBATCH = 1
ADAM_LR = 0.001
ADAM_B1 = 0.9
ADAM_B2 = 0.999
ADAM_EPS = 1e-08
ADAM_WD = 0.01
ADAM_STEP = 10
PER_EXAMPLE_BATCH_AXIS = {'x': 0, 'loss_target': 0}
SHARED_INPUTS = []
_WEIGHT_DTYPES = {'w_in': _jnp.float32, 'w_out': _jnp.float32, 'g_pre': _jnp.float32, 'g_post': _jnp.float32, 'w_alpha': _jnp.float32, 'b_alpha': _jnp.float32, 'g_gla': _jnp.float32, 'g_att': _jnp.float32, 'rel_bias': _jnp.float32}
MOMENT_SCALE = {'w_in': 3.214976e-01, 'w_out': 3.927122e-01, 'g_pre': 6.276037e-01, 'g_post': 7.962215e+00, 'w_alpha': 4.835146e-02, 'b_alpha': 1.853000e-01, 'g_gla': 2.906495e-01, 'g_att': 4.857443e-01, 'rel_bias': 1.765019e-01}


def _to_microbatches(a, axis):
    t = _jnp.moveaxis(a, axis, 0)
    t = t.reshape((N_MICROBATCH, t.shape[0] // N_MICROBATCH) + t.shape[1:])
    return _jnp.moveaxis(t, 1, axis + 1)


def setup_inputs(seed: int = 0) -> dict:
    inp = _fwd_setup_inputs(seed)
    key = _jax.random.fold_in(_jax.random.key(seed), 7919)
    shape, _ = _output_shape()
    out = dict(inp)
    out["loss_target"] = _jax.random.normal(_jax.random.fold_in(key, 0), shape, _jnp.float32)
    for i, name in enumerate(TWIN_WEIGHTS):
        w = inp[name].astype(_jnp.float32)
        if MOMENT_SCALE is None:
            s = _jnp.sqrt(_jnp.mean(_jnp.square(w)) + 1e-30)
        else:
            s = MOMENT_SCALE[name]
        km, kv = _jax.random.split(_jax.random.fold_in(key, i + 1))
        out[name] = w
        out["m_" + name] = s * _jax.random.normal(km, w.shape, _jnp.float32)
        out["v_" + name] = (s * s) * _jax.random.uniform(kv, w.shape, _jnp.float32, 0.5, 1.5)
    if N_MICROBATCH > 1:
        for name, axis in PER_EXAMPLE_BATCH_AXIS.items():
            out[name] = _to_microbatches(out[name], axis)
    return {'x': out['x'], 'w_in': out['w_in'], 'w_out': out['w_out'], 'g_pre': out['g_pre'], 'g_post': out['g_post'], 'w_alpha': out['w_alpha'], 'b_alpha': out['b_alpha'], 'g_gla': out['g_gla'], 'g_att': out['g_att'], 'rel_bias': out['rel_bias'], 'loss_target': out['loss_target'], 'm_w_in': out['m_w_in'], 'm_w_out': out['m_w_out'], 'm_g_pre': out['m_g_pre'], 'm_g_post': out['m_g_post'], 'm_w_alpha': out['m_w_alpha'], 'm_b_alpha': out['m_b_alpha'], 'm_g_gla': out['m_g_gla'], 'm_g_att': out['m_g_att'], 'm_rel_bias': out['m_rel_bias'], 'v_w_in': out['v_w_in'], 'v_w_out': out['v_w_out'], 'v_g_pre': out['v_g_pre'], 'v_g_post': out['v_g_post'], 'v_w_alpha': out['v_w_alpha'], 'v_b_alpha': out['v_b_alpha'], 'v_g_gla': out['v_g_gla'], 'v_g_att': out['v_g_att'], 'v_rel_bias': out['v_rel_bias']}


def _loss(weights, diff, rest, loss_target):
    with _jax.named_scope("forward"):
        args = {**rest, TWIN_DIFF_INPUT: diff, **{k: w.astype(_WEIGHT_DTYPES[k]) for k, w in weights.items()}}
        y = _forward(args)
    with _jax.named_scope("loss_head"):
        err = _jnp.square(y.astype(_jnp.float32) - loss_target)
        return 0.5 * _jnp.sum(_jnp.mean(err, axis=-1)) if err.ndim else 0.5 * err


def _adamw(w, g, m, v):
    m = ADAM_B1 * m + (1.0 - ADAM_B1) * g
    v = ADAM_B2 * v + (1.0 - ADAM_B2) * _jnp.square(g)
    m_hat = m / (1.0 - ADAM_B1 ** ADAM_STEP)
    v_hat = v / (1.0 - ADAM_B2 ** ADAM_STEP)
    delta = -ADAM_LR * (m_hat / (_jnp.sqrt(v_hat) + ADAM_EPS) + ADAM_WD * w)
    return delta, m, v


def reference(x, w_in, w_out, g_pre, g_post, w_alpha, b_alpha, g_gla, g_att, rel_bias, loss_target, m_w_in, m_w_out, m_g_pre, m_g_post, m_w_alpha, m_b_alpha, m_g_gla, m_g_att, m_rel_bias, v_w_in, v_w_out, v_g_pre, v_g_post, v_w_alpha, v_b_alpha, v_g_gla, v_g_att, v_rel_bias):
    given = dict(x=x, w_in=w_in, w_out=w_out, g_pre=g_pre, g_post=g_post, w_alpha=w_alpha, b_alpha=b_alpha, g_gla=g_gla, g_att=g_att, rel_bias=rel_bias, loss_target=loss_target, m_w_in=m_w_in, m_w_out=m_w_out, m_g_pre=m_g_pre, m_g_post=m_g_post, m_w_alpha=m_w_alpha, m_b_alpha=m_b_alpha, m_g_gla=m_g_gla, m_g_att=m_g_att, m_rel_bias=m_rel_bias, v_w_in=v_w_in, v_w_out=v_w_out, v_g_pre=v_g_pre, v_g_post=v_g_post, v_w_alpha=v_w_alpha, v_b_alpha=v_b_alpha, v_g_gla=v_g_gla, v_g_att=v_g_att, v_rel_bias=v_rel_bias)
    weights = {n: given[n] for n in TWIN_WEIGHTS}
    shared = {n: given[n] for n in SHARED_INPUTS}
    per_example = {n: given[n] for n in ['x']}
    grad_fn = _jax.value_and_grad(_loss, argnums=(0, 1))

    def one_microbatch(ex, loss_target):
        ex = dict(ex)
        diff = ex.pop(TWIN_DIFF_INPUT)
        return grad_fn(weights, diff, {**shared, **ex}, loss_target)

    if N_MICROBATCH == 1:
        loss, (grad_w, grad_x) = one_microbatch(per_example, given["loss_target"])
    else:
        def body(carry, xs):
            loss_sum, grad_sum = carry
            l_k, (gw_k, gx_k) = one_microbatch(xs[0], xs[1])
            with _jax.named_scope("update"):
                return (loss_sum + l_k, _jax.tree.map(_jnp.add, grad_sum, gw_k)), gx_k

        init = (_jnp.zeros((), _jnp.float32), _jax.tree.map(_jnp.zeros_like, weights))
        (loss, grad_w), grad_x = _jax.lax.scan(body, init, (per_example, given["loss_target"]))
    with _jax.named_scope("update"):
        delta_w, new_m, new_v = {}, {}, {}
        for n in TWIN_WEIGHTS:
            delta_w[n], new_m[n], new_v[n] = _adamw(weights[n], grad_w[n], given["m_" + n], given["v_" + n])
    return (loss, grad_x, *[grad_w[n] for n in TWIN_WEIGHTS], *[delta_w[n] for n in TWIN_WEIGHTS],
            *[new_m[n] for n in TWIN_WEIGHTS], *[new_v[n] for n in TWIN_WEIGHTS])
```

```python
import functools

import jax
import jax.numpy as jnp
from jax import lax
from jax.experimental import pallas as pl
from jax.experimental.pallas import tpu as pltpu

F32 = jnp.float32
BF16 = jnp.bfloat16
MESH = pl.DeviceIdType.MESH
ANY = pl.BlockSpec(memory_space=pl.ANY)

CHUNK = 64
GLA_HEADS = 4
GLA_DK = 128
GLA_DV = 256
GLA_KW = GLA_HEADS * GLA_DK
D_GLA = GLA_HEADS * GLA_DV
GLA_RANK = 16
GLA_TAU = 16.0
ATT_HEADS = 8
ATT_HD = 128
D_ATT = ATT_HEADS * ATT_HD
LEFT_CHUNKS = 8
REL_CLIP = 128
N_REL = 2 * REL_CLIP + 1
EPS = 1e-6
D_IN = 2 * GLA_KW + 2 * D_GLA + GLA_RANK + 4 * D_ATT
GLA_SCALE = GLA_DK ** -0.5
ATT_SCALE = ATT_HD ** -0.5

ADAM_LR = 0.001
ADAM_B1 = 0.9
ADAM_B2 = 0.999
ADAM_EPS = 1e-08
ADAM_WD = 0.01
ADAM_STEP = 10

N_DEV = 8
LANE = 128
GA_ORIG = 2 * GLA_KW + 2 * D_GLA
OFF_AQ = GA_ORIG
OFF_GA = GA_ORIG + 4 * D_ATT
D_ZP = OFF_GA + LANE
QB = 2 * CHUNK
WIN = (LEFT_CHUNKS + 2) * CHUNK
ET_ROWS = WIN + LEFT_CHUNKS * CHUNK
NEG = -1e30
VMEM_LIMIT = 48 * 1024 * 1024


def _cparams(sem):
    return pltpu.CompilerParams(dimension_semantics=sem, vmem_limit_bytes=VMEM_LIMIT)


def _dot(a, b):
    return jnp.dot(a, b, preferred_element_type=F32)


def _dot_nt(a, b):
    return lax.dot_general(a, b, (((1,), (1,)), ((), ())), preferred_element_type=F32)


def _dot_tn(a, b):
    return lax.dot_general(a, b, (((0,), (0,)), ((), ())), preferred_element_type=F32)


def _dot01(t, x):
    hi = x.astype(BF16)
    r = x - hi.astype(F32)
    mid = r.astype(BF16)
    lo = (r - mid.astype(F32)).astype(BF16)
    return _dot(t, hi) + _dot(t, mid) + _dot(t, lo)


def _sigmoid(x):
    return 1.0 / (1.0 + jnp.exp(-x))


def _log_sigmoid(x):
    return jnp.minimum(x, 0.0) - jnp.log(1.0 + jnp.exp(-jnp.abs(x)))


def _matmul(a, b, mode, out_dtype, tm, tn, tk, name):
    if mode == "nn":
        (m, k), n = a.shape, b.shape[1]
    elif mode == "nt":
        (m, k), n = a.shape, b.shape[0]
    else:
        (k, m), n = a.shape, b.shape[1]
    tm, tn, tk = min(tm, m), min(tn, n), min(tk, k)
    assert m % tm == 0 and n % tn == 0 and k % tk == 0, (name, m, n, k)
    nk = k // tk

    def body(a_ref, b_ref, o_ref, acc_ref):
        kk = pl.program_id(2)

        @pl.when(kk == 0)
        def _():
            acc_ref[...] = jnp.zeros_like(acc_ref)

        if mode == "nn":
            acc_ref[...] += _dot(a_ref[...], b_ref[...])
        elif mode == "nt":
            acc_ref[...] += _dot_nt(a_ref[...], b_ref[...])
        else:
            acc_ref[...] += _dot_tn(a_ref[...], b_ref[...])

        @pl.when(kk == nk - 1)
        def _():
            o_ref[...] = acc_ref[...].astype(out_dtype)

    if mode == "tn":
        a_spec = pl.BlockSpec((tk, tm), lambda i, j, kk: (kk, i))
    else:
        a_spec = pl.BlockSpec((tm, tk), lambda i, j, kk: (i, kk))
    if mode == "nt":
        b_spec = pl.BlockSpec((tn, tk), lambda i, j, kk: (j, kk))
    else:
        b_spec = pl.BlockSpec((tk, tn), lambda i, j, kk: (kk, j))
    return pl.pallas_call(
        body, name=name,
        grid=(m // tm, n // tn, nk),
        in_specs=[a_spec, b_spec],
        out_specs=pl.BlockSpec((tm, tn), lambda i, j, kk: (i, j)),
        out_shape=jax.ShapeDtypeStruct((m, n), out_dtype),
        scratch_shapes=[pltpu.VMEM((tm, tn), F32)],
        compiler_params=_cparams(("parallel", "parallel", "arbitrary")),
    )(a, b)


ROWS = 256


def _rms_fwd(x, g):
    s, d = x.shape

    def body(x_ref, g_ref, h_ref):
        xv = x_ref[...]
        r = lax.rsqrt(jnp.mean(xv * xv, axis=-1, keepdims=True) + EPS)
        h_ref[...] = (xv * r * g_ref[...]).astype(BF16)

    return pl.pallas_call(
        body, name="rms_fwd", grid=(s // ROWS,),
        in_specs=[pl.BlockSpec((ROWS, d), lambda i: (i, 0)), pl.BlockSpec((1, d), lambda i: (0, 0))],
        out_specs=pl.BlockSpec((ROWS, d), lambda i: (i, 0)),
        out_shape=jax.ShapeDtypeStruct((s, d), BF16),
        compiler_params=_cparams(("parallel",)),
    )(x, g)


def _post_fwd(x, y, g):
    s, d = x.shape

    def body(x_ref, y_ref, g_ref, o_ref):
        yv = y_ref[...]
        r = lax.rsqrt(jnp.mean(yv * yv, axis=-1, keepdims=True) + EPS)
        o_ref[...] = x_ref[...] + yv * r * g_ref[...]

    row = pl.BlockSpec((ROWS, d), lambda i: (i, 0))
    return pl.pallas_call(
        body, name="post_fwd", grid=(s // ROWS,),
        in_specs=[row, row, pl.BlockSpec((1, d), lambda i: (0, 0))],
        out_specs=row,
        out_shape=jax.ShapeDtypeStruct((s, d), F32),
        compiler_params=_cparams(("parallel",)),
    )(x, y, g)


def _loss_head(out, tgt):
    s, d = out.shape

    def body(o_ref, t_ref, dout_ref, sum_ref):
        @pl.when(pl.program_id(0) == 0)
        def _():
            sum_ref[...] = jnp.zeros_like(sum_ref)

        e = o_ref[...] - t_ref[...]
        dout_ref[...] = e * (1.0 / d)
        sum_ref[...] += jnp.sum(jnp.sum(e * e, axis=1, keepdims=True), axis=0, keepdims=True)

    row = pl.BlockSpec((ROWS, d), lambda i: (i, 0))
    return pl.pallas_call(
        body, name="loss_head", grid=(s // ROWS,),
        in_specs=[row, row],
        out_specs=[row, pl.BlockSpec((1, 1), lambda i: (0, 0))],
        out_shape=[jax.ShapeDtypeStruct((s, d), F32), jax.ShapeDtypeStruct((1, 1), F32)],
        compiler_params=_cparams(("arbitrary",)),
    )(out, tgt)


def _post_bwd(dout, y, g):
    s, d = y.shape

    def body(do_ref, y_ref, g_ref, dy_ref, dg_ref):
        @pl.when(pl.program_id(0) == 0)
        def _():
            dg_ref[...] = jnp.zeros_like(dg_ref)

        yv = y_ref[...]
        dv = do_ref[...]
        r = lax.rsqrt(jnp.mean(yv * yv, axis=-1, keepdims=True) + EPS)
        dg_ref[...] += jnp.sum(dv * yv * r, axis=0, keepdims=True)
        w = dv * g_ref[...]
        dy = r * (w - yv * (r * r) * jnp.mean(w * yv, axis=-1, keepdims=True))
        dy_ref[...] = dy.astype(BF16)

    row = pl.BlockSpec((ROWS, d), lambda i: (i, 0))
    vec = pl.BlockSpec((1, d), lambda i: (0, 0))
    return pl.pallas_call(
        body, name="post_bwd", grid=(s // ROWS,),
        in_specs=[row, row, vec],
        out_specs=[row, vec],
        out_shape=[jax.ShapeDtypeStruct((s, d), BF16), jax.ShapeDtypeStruct((1, d), F32)],
        compiler_params=_cparams(("arbitrary",)),
    )(dout, y, g)


def _pre_bwd(dh, x, g, dout):
    s, d = x.shape

    def body(dh_ref, x_ref, g_ref, do_ref, dx_ref, dg_ref):
        @pl.when(pl.program_id(0) == 0)
        def _():
            dg_ref[...] = jnp.zeros_like(dg_ref)

        xv = x_ref[...]
        dv = dh_ref[...]
        r = lax.rsqrt(jnp.mean(xv * xv, axis=-1, keepdims=True) + EPS)
        dg_ref[...] += jnp.sum(dv * xv * r, axis=0, keepdims=True)
        w = dv * g_ref[...]
        dx_ref[...] = do_ref[...] + r * (w - xv * (r * r) * jnp.mean(w * xv, axis=-1, keepdims=True))

    row = pl.BlockSpec((ROWS, d), lambda i: (i, 0))
    vec = pl.BlockSpec((1, d), lambda i: (0, 0))
    return pl.pallas_call(
        body, name="pre_bwd", grid=(s // ROWS,),
        in_specs=[row, row, vec, row],
        out_specs=[row, vec],
        out_shape=[jax.ShapeDtypeStruct((s, d), F32), jax.ShapeDtypeStruct((1, d), F32)],
        compiler_params=_cparams(("arbitrary",)),
    )(dh, x, g, dout)


def _gla_gate(ga_b, wa_ref, b_ref, cs, tri):
    pre = _dot(ga_b, wa_ref[:, cs].astype(BF16)) + b_ref[:, cs]
    la = _log_sigmoid(pre) * (1.0 / GLA_TAU)
    cum = _dot01(tri, la)
    last = lax.broadcasted_iota(jnp.int32, cum.shape, 0) == CHUNK - 1
    return pre, cum, jnp.sum(jnp.where(last, cum, 0.0), axis=0, keepdims=True)


def _z_specs_gla(rev=None):
    idx = (lambda n: n) if rev is None else rev
    return [
        pl.BlockSpec((CHUNK, GLA_KW), lambda n: (idx(n), 0)),
        pl.BlockSpec((CHUNK, GLA_KW), lambda n: (idx(n), 1)),
        pl.BlockSpec((CHUNK, D_GLA), lambda n: (idx(n), 1)),
        pl.BlockSpec((CHUNK, D_GLA), lambda n: (idx(n), 2)),
        pl.BlockSpec((CHUNK, LANE), lambda n: (idx(n), OFF_GA // LANE)),
    ]


def _gla_fwd(z, wa_pad, b_alpha, g_gla):
    s = z.shape[0]
    nchunk = s // CHUNK

    def body(q_ref, k_ref, v_ref, gg_ref, ga_ref, wa_ref, b_ref, g_ref, y_ref, o_ref, st_ref, state):
        @pl.when(pl.program_id(0) == 0)
        def _():
            state[...] = jnp.zeros_like(state)

        ga_b = ga_ref[...].astype(BF16)
        ri = lax.broadcasted_iota(jnp.int32, (CHUNK, CHUNK), 0)
        ci = lax.broadcasted_iota(jnp.int32, (CHUNK, CHUNK), 1)
        tri = jnp.where(ri >= ci, 1.0, 0.0).astype(BF16)
        for h in range(GLA_HEADS):
            cs = slice(h * GLA_DK, (h + 1) * GLA_DK)
            vs = slice(h * GLA_DV, (h + 1) * GLA_DV)
            _, cum, cend = _gla_gate(ga_b, wa_ref, b_ref, cs, tri)
            kd = k_ref[:, cs] * jnp.exp(cend - cum)
            st = state[h] * jnp.exp(cend) + _dot_tn(v_ref[:, vs].astype(BF16), kd.astype(BF16))
            state[h] = st
            st_ref[0, h] = st
            qs = (q_ref[:, cs] * GLA_SCALE).astype(BF16)
            o = _dot_nt(qs, st.astype(BF16))
            o_ref[:, vs] = o
            r = lax.rsqrt(jnp.mean(o * o, axis=-1, keepdims=True) + EPS)
            gg = gg_ref[:, vs]
            y_ref[:, vs] = (o * r * g_ref[:, vs] * (gg * _sigmoid(gg))).astype(BF16)

    full = lambda shape: pl.BlockSpec(shape, lambda n: tuple(0 for _ in shape))
    wide = pl.BlockSpec((CHUNK, D_GLA), lambda n: (n, 0))
    return pl.pallas_call(
        body, name="gla_fwd", grid=(nchunk,),
        in_specs=_z_specs_gla() + [full((LANE, GLA_KW)), full((1, GLA_KW)), full((1, D_GLA))],
        out_specs=[wide, wide, pl.BlockSpec((1, GLA_HEADS, GLA_DV, GLA_DK), lambda n: (n, 0, 0, 0))],
        out_shape=[jax.ShapeDtypeStruct((s, D_GLA), BF16), jax.ShapeDtypeStruct((s, D_GLA), F32),
                   jax.ShapeDtypeStruct((nchunk, GLA_HEADS, GLA_DV, GLA_DK), F32)],
        scratch_shapes=[pltpu.VMEM((GLA_HEADS, GLA_DV, GLA_DK), F32)],
        compiler_params=_cparams(("arbitrary",)),
    )(z, z, z, z, z, wa_pad, b_alpha, g_gla)


def _gla_bwd(dyc, o_gla, z, wa_pad, b_alpha, g_gla, states):
    s = z.shape[0]
    nchunk = s // CHUNK
    rev = lambda n: nchunk - 1 - n

    def body(dy_ref, o_ref, q_ref, k_ref, v_ref, gg_ref, ga_ref, wa_ref, b_ref, g_ref, st_ref, stp_ref,
             dq_ref, dk_ref, dv_ref, dgg_ref, dga_ref, dwa_ref, db_ref, dg_ref, carry):
        step = pl.program_id(0)

        @pl.when(step == 0)
        def _():
            carry[...] = jnp.zeros_like(carry)
            dwa_ref[...] = jnp.zeros_like(dwa_ref)
            db_ref[...] = jnp.zeros_like(db_ref)
            dg_ref[...] = jnp.zeros_like(dg_ref)

        has_prev = (step < nchunk - 1).astype(F32)
        ga_b = ga_ref[...].astype(BF16)
        ri = lax.broadcasted_iota(jnp.int32, (CHUNK, CHUNK), 0)
        ci = lax.broadcasted_iota(jnp.int32, (CHUNK, CHUNK), 1)
        tri = jnp.where(ri >= ci, 1.0, 0.0).astype(BF16)
        tri_up = jnp.where(ci >= ri, 1.0, 0.0).astype(BF16)
        dga = jnp.zeros((CHUNK, LANE), F32)
        for h in range(GLA_HEADS):
            cs = slice(h * GLA_DK, (h + 1) * GLA_DK)
            vs = slice(h * GLA_DV, (h + 1) * GLA_DV)
            pre, cum, cend = _gla_gate(ga_b, wa_ref, b_ref, cs, tri)
            e = jnp.exp(cend - cum)
            a = jnp.exp(cend)
            kf = k_ref[:, cs]
            kd_b = (kf * e).astype(BF16)
            v_b = v_ref[:, vs].astype(BF16)
            qs = (q_ref[:, cs] * GLA_SCALE).astype(BF16)
            o = o_ref[:, vs]
            gg = gg_ref[:, vs]
            g = g_ref[:, vs]
            dy = dy_ref[:, vs]
            r = lax.rsqrt(jnp.mean(o * o, axis=-1, keepdims=True) + EPS)
            sg = _sigmoid(gg)
            dogn = dy * (gg * sg)
            dgg_ref[:, vs] = (dy * (o * r * g) * (sg * (1.0 + gg * (1.0 - sg)))).astype(BF16)
            dg_ref[:, vs] += jnp.sum(dogn * o * r, axis=0, keepdims=True)
            w = dogn * g
            do_b = (r * (w - o * (r * r) * jnp.mean(w * o, axis=-1, keepdims=True))).astype(BF16)
            st = st_ref[0, h]
            dq_ref[:, cs] = (_dot(do_b, st.astype(BF16)) * GLA_SCALE).astype(BF16)
            gt = _dot_tn(do_b, qs) + carry[h]
            gt_b = gt.astype(BF16)
            dkd = _dot(v_b, gt_b)
            dv_ref[:, vs] = _dot_nt(kd_b, gt_b).astype(BF16)
            da = jnp.sum(gt * (stp_ref[0, h] * has_prev), axis=0, keepdims=True)
            carry[h] = gt * a
            dk_ref[:, cs] = (dkd * e).astype(BF16)
            dd = dkd * kf * e
            dcend = jnp.sum(dd, axis=0, keepdims=True) + da * a
            dla = dcend - _dot01(tri_up, dd)
            dpre = dla * (1.0 / GLA_TAU) * (1.0 - _sigmoid(pre))
            dpre_b = dpre.astype(BF16)
            dga = dga + _dot_nt(dpre_b, wa_ref[:, cs].astype(BF16))
            dwa_ref[:, cs] += _dot_tn(ga_b, dpre_b)
            db_ref[:, cs] += jnp.sum(dpre, axis=0, keepdims=True)
        dga_ref[...] = dga.astype(BF16)

    full = lambda shape: pl.BlockSpec(shape, lambda n: tuple(0 for _ in shape))
    wide = pl.BlockSpec((CHUNK, D_GLA), lambda n: (rev(n), 0))
    keyw = pl.BlockSpec((CHUNK, GLA_KW), lambda n: (rev(n), 0))
    st_spec = pl.BlockSpec((1, GLA_HEADS, GLA_DV, GLA_DK), lambda n: (rev(n), 0, 0, 0))
    stp_spec = pl.BlockSpec((1, GLA_HEADS, GLA_DV, GLA_DK), lambda n: (jnp.maximum(rev(n) - 1, 0), 0, 0, 0))
    return pl.pallas_call(
        body, name="gla_bwd", grid=(nchunk,),
        in_specs=[wide, wide] + _z_specs_gla(rev)
        + [full((LANE, GLA_KW)), full((1, GLA_KW)), full((1, D_GLA)), st_spec, stp_spec],
        out_specs=[keyw, keyw, wide, wide, pl.BlockSpec((CHUNK, LANE), lambda n: (rev(n), 0)),
                   full((LANE, GLA_KW)), full((1, GLA_KW)), full((1, D_GLA))],
        out_shape=[jax.ShapeDtypeStruct((s, GLA_KW), BF16), jax.ShapeDtypeStruct((s, GLA_KW), BF16),
                   jax.ShapeDtypeStruct((s, D_GLA), BF16), jax.ShapeDtypeStruct((s, D_GLA), BF16),
                   jax.ShapeDtypeStruct((s, LANE), BF16),
                   jax.ShapeDtypeStruct((LANE, GLA_KW), F32), jax.ShapeDtypeStruct((1, GLA_KW), F32),
                   jax.ShapeDtypeStruct((1, D_GLA), F32)],
        scratch_shapes=[pltpu.VMEM((GLA_HEADS, GLA_DV, GLA_DK), F32)],
        compiler_params=_cparams(("arbitrary",)),
    )(dyc, o_gla, z, z, z, z, z, wa_pad, b_alpha, g_gla, states, states)


def _build_bias_table(rb_row, et_ref):
    far = jnp.broadcast_to(rb_row[:, 2 * REL_CLIP:2 * REL_CLIP + 1], (1, LANE))
    near_hi = rb_row[:, REL_CLIP:2 * REL_CLIP]
    near_lo = rb_row[:, 0:REL_CLIP]
    past = jnp.broadcast_to(rb_row[:, 0:1], (1, LANE))
    seg = [far, far, far, far, near_hi, near_lo] + [past] * (ET_ROWS // LANE - 5)
    ri = lax.broadcasted_iota(jnp.int32, (LANE, LANE), 0)
    ci = lax.broadcasted_iota(jnp.int32, (LANE, LANE), 1)
    for kb in range(ET_ROWS // LANE):
        wmat = jnp.where(ri + ci < LANE, seg[kb], seg[kb + 1])
        blk = pltpu.roll(wmat, 0, 1, stride=1, stride_axis=0)
        lag = LEFT_CHUNKS + ci // CHUNK - (2 * kb + ri // CHUNK)
        et_ref[kb * LANE:(kb + 1) * LANE, :] = jnp.where((lag >= 0) & (lag <= LEFT_CHUNKS), blk, NEG)


def _reduce_bias_table(det_ref):
    lane = lax.broadcasted_iota(jnp.int32, (1, LANE), 1)

    def diag_sums(kb):
        def one(i, acc):
            plus, minus = acc
            rowv = det_ref[pl.ds(kb * LANE + i, 1), :]
            rolled = pltpu.roll(rowv, (LANE - i) % LANE, 1)
            keep = lane + i < LANE
            return plus + jnp.where(keep, rolled, 0.0), minus + jnp.where(keep, 0.0, rolled)

        zero = jnp.zeros((1, LANE), F32)
        return lax.fori_loop(0, LANE, one, (zero, zero))

    segs = []
    prev_minus = jnp.zeros((1, LANE), F32)
    for kb in range(6):
        plus, minus = diag_sums(kb)
        segs.append(plus + prev_minus)
        prev_minus = minus
    far = jnp.sum(segs[0] + segs[1] + segs[2] + segs[3], axis=1, keepdims=True)
    last = jnp.where(lane == 0, far, 0.0)
    return jnp.concatenate([segs[5], segs[4], last], axis=1)


def _att_window(b):
    c0 = 2 * b
    kstart = pl.multiple_of(jnp.maximum(c0 - LEFT_CHUNKS, 0) * CHUNK, CHUNK)
    eoff = pl.multiple_of(jnp.maximum(LEFT_CHUNKS - c0, 0) * CHUNK, CHUNK)
    return kstart, eoff


def _att_probs(q_b, kw_b, et):
    st = _dot_nt(kw_b, q_b) * ATT_SCALE + et
    m = jnp.max(st, axis=0, keepdims=True)
    ex = jnp.exp(st - m)
    return ex / jnp.sum(ex, axis=0, keepdims=True)


def _att_fwd(z, rb_pad, g_att):
    s = z.shape[0]
    nblk = s // QB
    c_aq, c_ak, c_av, c_ag = [(OFF_AQ + i * D_ATT) // ATT_HD for i in range(4)]

    def body(q_ref, k_ref, v_ref, ag_ref, rb_ref, g_ref, y_ref, o_ref, et_ref):
        h = pl.program_id(0)
        b = pl.program_id(1)

        @pl.when(b == 0)
        def _():
            _build_bias_table(rb_ref[pl.ds(h, 1), :], et_ref)

        kstart, eoff = _att_window(b)
        q_b = q_ref[...].astype(BF16)
        kw_b = k_ref[pl.ds(kstart, WIN), :].astype(BF16)
        vw_b = v_ref[pl.ds(kstart, WIN), :].astype(BF16)
        pt = _att_probs(q_b, kw_b, et_ref[pl.ds(eoff, WIN), :])
        o = _dot_tn(pt.astype(BF16), vw_b)
        o_ref[...] = o
        r = lax.rsqrt(jnp.mean(o * o, axis=-1, keepdims=True) + EPS)
        ag = ag_ref[...]
        y_ref[...] = (o * r * g_ref[...] * (ag * _sigmoid(ag))).astype(BF16)

    blk = lambda col: pl.BlockSpec((QB, ATT_HD), lambda h, b: (b, col + h))
    seq = lambda col: pl.BlockSpec((s, ATT_HD), lambda h, b: (0, col + h))
    out_blk = pl.BlockSpec((QB, ATT_HD), lambda h, b: (b, h))
    return pl.pallas_call(
        body, name="att_fwd", grid=(ATT_HEADS, nblk),
        in_specs=[blk(c_aq), seq(c_ak), seq(c_av), blk(c_ag),
                  pl.BlockSpec((ATT_HEADS, 3 * LANE), lambda h, b: (0, 0)),
                  pl.BlockSpec((1, ATT_HD), lambda h, b: (0, h))],
        out_specs=[out_blk, out_blk],
        out_shape=[jax.ShapeDtypeStruct((s, D_ATT), BF16), jax.ShapeDtypeStruct((s, D_ATT), F32)],
        scratch_shapes=[pltpu.VMEM((ET_ROWS, LANE), F32)],
        compiler_params=_cparams(("arbitrary", "arbitrary")),
    )(z, z, z, z, rb_pad, g_att)


def _att_bwd(dyc, o_att, z, rb_pad, g_att):
    s = z.shape[0]
    nblk = s // QB
    c_aq, c_ak, c_av, c_ag = [(OFF_AQ + i * D_ATT) // ATT_HD for i in range(4)]
    c_dy = D_GLA // ATT_HD

    def body(dy_ref, o_ref, q_ref, k_ref, v_ref, ag_ref, rb_ref, g_ref,
             dq_ref, dk_ref, dv_ref, dag_ref, drb_ref, dg_ref, et_ref, det_ref):
        h = pl.program_id(0)
        b = pl.program_id(1)

        @pl.when(b == 0)
        def _():
            _build_bias_table(rb_ref[pl.ds(h, 1), :], et_ref)
            det_ref[...] = jnp.zeros_like(det_ref)
            dk_ref[...] = jnp.zeros_like(dk_ref)
            dv_ref[...] = jnp.zeros_like(dv_ref)
            dg_ref[...] = jnp.zeros_like(dg_ref)

        kstart, eoff = _att_window(b)
        q_b = q_ref[...].astype(BF16)
        kw_b = k_ref[pl.ds(kstart, WIN), :].astype(BF16)
        vw_b = v_ref[pl.ds(kstart, WIN), :].astype(BF16)
        pt = _att_probs(q_b, kw_b, et_ref[pl.ds(eoff, WIN), :])
        o = o_ref[...]
        ag = ag_ref[...]
        g = g_ref[...]
        dy = dy_ref[...]
        r = lax.rsqrt(jnp.mean(o * o, axis=-1, keepdims=True) + EPS)
        sg = _sigmoid(ag)
        don = dy * (ag * sg)
        dag_ref[...] = (dy * (o * r * g) * (sg * (1.0 + ag * (1.0 - sg)))).astype(BF16)
        dg_ref[...] += jnp.sum(don * o * r, axis=0, keepdims=True)
        w = don * g
        do_b = (r * (w - o * (r * r) * jnp.mean(w * o, axis=-1, keepdims=True))).astype(BF16)
        pt_b = pt.astype(BF16)
        dpt = _dot_nt(vw_b, do_b)
        dst = pt * (dpt - jnp.sum(dpt * pt, axis=0, keepdims=True))
        det_ref[pl.ds(eoff, WIN), :] += dst
        ds_b = (dst * ATT_SCALE).astype(BF16)
        dq_ref[...] = _dot_tn(ds_b, kw_b).astype(BF16)
        dk_ref[pl.ds(kstart, WIN), :] += _dot(ds_b, q_b)
        dv_ref[pl.ds(kstart, WIN), :] += _dot(pt_b, do_b)

        @pl.when(b == nblk - 1)
        def _():
            drb_ref[0] = jnp.broadcast_to(_reduce_bias_table(det_ref), (8, 3 * LANE))

    blk = lambda col: pl.BlockSpec((QB, ATT_HD), lambda h, b: (b, col + h))
    seq = lambda col: pl.BlockSpec((s, ATT_HD), lambda h, b: (0, col + h))
    out_blk = pl.BlockSpec((QB, ATT_HD), lambda h, b: (b, h))
    out_seq = pl.BlockSpec((s, ATT_HD), lambda h, b: (0, h))
    return pl.pallas_call(
        body, name="att_bwd", grid=(ATT_HEADS, nblk),
        in_specs=[blk(c_dy), blk(0), blk(c_aq), seq(c_ak), seq(c_av), blk(c_ag),
                  pl.BlockSpec((ATT_HEADS, 3 * LANE), lambda h, b: (0, 0)),
                  pl.BlockSpec((1, ATT_HD), lambda h, b: (0, h))],
        out_specs=[out_blk, out_seq, out_seq, out_blk,
                   pl.BlockSpec((1, 8, 3 * LANE), lambda h, b: (h, 0, 0)),
                   pl.BlockSpec((1, ATT_HD), lambda h, b: (0, h))],
        out_shape=[jax.ShapeDtypeStruct((s, D_ATT), BF16), jax.ShapeDtypeStruct((s, D_ATT), F32),
                   jax.ShapeDtypeStruct((s, D_ATT), F32), jax.ShapeDtypeStruct((s, D_ATT), BF16),
                   jax.ShapeDtypeStruct((ATT_HEADS, 8, 3 * LANE), F32),
                   jax.ShapeDtypeStruct((1, D_ATT), F32)],
        scratch_shapes=[pltpu.VMEM((ET_ROWS, LANE), F32), pltpu.VMEM((ET_ROWS, LANE), F32)],
        compiler_params=_cparams(("arbitrary", "arbitrary")),
    )(dyc, o_att, z, z, z, z, rb_pad, g_att)


def _adam_math(w, g, m, v):
    m2 = ADAM_B1 * m + (1.0 - ADAM_B1) * g
    v2 = ADAM_B2 * v + (1.0 - ADAM_B2) * (g * g)
    m_hat = m2 / (1.0 - ADAM_B1 ** ADAM_STEP)
    v_hat = v2 / (1.0 - ADAM_B2 ** ADAM_STEP)
    delta = -ADAM_LR * (m_hat / (jnp.sqrt(v_hat) + ADAM_EPS) + ADAM_WD * w)
    return delta, m2, v2


def _adam_sharded(parts, w, m, v, rows, name):
    nl, nr, nc = w.shape

    def body(p_ref, w_ref, m_ref, v_ref, g_ref, d_ref, m2_ref, v2_ref):
        g = p_ref[0, 0].astype(F32)
        for dev in range(1, N_DEV):
            g = g + p_ref[dev, 0].astype(F32)
        delta, m2, v2 = _adam_math(w_ref[0], g, m_ref[0], v_ref[0])
        g_ref[0] = g
        d_ref[0] = delta
        m2_ref[0] = m2
        v2_ref[0] = v2

    blk = pl.BlockSpec((1, rows, nc), lambda l, i: (l, i, 0))
    shp = jax.ShapeDtypeStruct(w.shape, F32)
    return pl.pallas_call(
        body, name=name, grid=(nl, nr // rows),
        in_specs=[pl.BlockSpec((N_DEV, 1, rows, nc), lambda l, i: (0, l, i, 0)), blk, blk, blk],
        out_specs=[blk, blk, blk, blk],
        out_shape=[shp, shp, shp, shp],
        compiler_params=_cparams(("parallel", "parallel")),
    )(parts, w, m, v)


def _adam_small(w, g, m, v):
    def body(w_ref, g_ref, m_ref, v_ref, d_ref, m2_ref, v2_ref):
        delta, m2, v2 = _adam_math(w_ref[...], g_ref[...], m_ref[...], v_ref[...])
        d_ref[...] = delta
        m2_ref[...] = m2
        v2_ref[...] = v2

    shp = jax.ShapeDtypeStruct(w.shape, F32)
    return pl.pallas_call(body, name="adam_small", out_shape=[shp, shp, shp])(w, g, m, v)


def _position():
    return lax.axis_index("x"), lax.axis_index("y"), lax.axis_index("c")


def _slot(p):
    return 4 * p[0] + 2 * p[1] + p[2]


def _allgather(arrs, name):
    n = len(arrs)

    def body(*refs):
        ins, outs = refs[:n], refs[n:2 * n]
        send_sems, recv_sems, local_sems = refs[2 * n:]
        x, y, c = _position()
        me, sibling = (x, y, c), (x, y, 1 - c)
        chips = [(1 - x, y), (x, 1 - y), (1 - x, 1 - y)]

        def copy(a, k, block, to, src=None):
            dst = outs[a].at[_slot(block)]
            return pltpu.make_async_remote_copy(
                src_ref=dst if src is None else src, dst_ref=dst,
                send_sem=send_sems.at[a * 7 + k], recv_sem=recv_sems.at[a * 7 + k],
                device_id=to, device_id_type=MESH)

        mine = [pltpu.make_async_copy(ins[a], outs[a].at[_slot(me)], local_sems.at[a]) for a in range(n)]
        for cp in mine:
            cp.start()
        first = []
        for a in range(n):
            first.append(copy(a, 0, me, sibling, src=ins[a]))
            first += [copy(a, 1 + j, me, (*chip, c), src=ins[a]) for j, chip in enumerate(chips)]
        for cp in first:
            cp.start()
        passed = []
        for j, chip in enumerate(chips):
            for a in range(n):
                copy(a, 1 + j, (*chip, c), me).wait_recv()
                fwd = copy(a, 4 + j, (*chip, c), sibling)
                fwd.start()
                passed.append(fwd)
        for a in range(n):
            copy(a, 0, sibling, me).wait_recv()
            for j, chip in enumerate(chips):
                copy(a, 4 + j, (*chip, 1 - c), me).wait_recv()
        for cp in first + passed:
            cp.wait_send()
        for cp in mine:
            cp.wait()

    return pl.pallas_call(
        body, name=name,
        in_specs=[ANY] * n, out_specs=[ANY] * n,
        out_shape=[jax.ShapeDtypeStruct((N_DEV,) + a.shape, a.dtype) for a in arrs],
        scratch_shapes=[pltpu.SemaphoreType.DMA((7 * n,)), pltpu.SemaphoreType.DMA((7 * n,)),
                        pltpu.SemaphoreType.DMA((n,))],
    )(*arrs)


def _peer(pos, k):
    x, y, c = pos
    return (1 - x if k & 4 else x, 1 - y if k & 2 else y, 1 - c if k & 1 else c)


def _exchange(arrs, name):
    n = len(arrs)

    def body(*refs):
        ins, outs = refs[:n], refs[n:2 * n]
        send_sems, recv_sems, local_sems = refs[2 * n:]
        me = _position()

        def copy(a, k):
            peer = _peer(me, k)
            return pltpu.make_async_remote_copy(
                src_ref=ins[a].at[_slot(peer)], dst_ref=outs[a].at[_slot(me)],
                send_sem=send_sems.at[a * 7 + k - 1], recv_sem=recv_sems.at[a * 7 + k - 1],
                device_id=peer, device_id_type=MESH)

        def landed(a, k):
            peer = _peer(me, k)
            return pltpu.make_async_remote_copy(
                src_ref=ins[a].at[_slot(peer)], dst_ref=outs[a].at[_slot(peer)],
                send_sem=send_sems.at[a * 7 + k - 1], recv_sem=recv_sems.at[a * 7 + k - 1],
                device_id=peer, device_id_type=MESH)

        mine = [pltpu.make_async_copy(ins[a].at[_slot(me)], outs[a].at[_slot(me)], local_sems.at[a])
                for a in range(n)]
        for cp in mine:
            cp.start()
        sent = [copy(a, k) for k in range(1, N_DEV) for a in range(n)]
        for cp in sent:
            cp.start()
        for k in range(1, N_DEV):
            for a in range(n):
                landed(a, k).wait_recv()
        for cp in sent:
            cp.wait_send()
        for cp in mine:
            cp.wait()

    return pl.pallas_call(
        body, name=name,
        in_specs=[ANY] * n, out_specs=[ANY] * n,
        out_shape=[jax.ShapeDtypeStruct(a.shape, a.dtype) for a in arrs],
        scratch_shapes=[pltpu.SemaphoreType.DMA((7 * n,)), pltpu.SemaphoreType.DMA((7 * n,)),
                        pltpu.SemaphoreType.DMA((n,))],
    )(*arrs)


def _sum_slots(parts):
    def body(p_ref, o_ref):
        acc = p_ref[0]
        for dev in range(1, N_DEV):
            acc = acc + p_ref[dev]
        o_ref[...] = acc

    return pl.pallas_call(body, name="sum_slots",
                          out_shape=jax.ShapeDtypeStruct(parts.shape[1:], F32))(parts)


def _pack(arrs):
    flat = jnp.concatenate([a.reshape(-1) for a in arrs])
    pad = (-flat.shape[0]) % (8 * LANE)
    return jnp.pad(flat, (0, pad)).reshape(-1, LANE)


def _unpack(packed, shapes):
    flat = packed.reshape(-1)
    out, at = [], 0
    for shp in shapes:
        size = 1
        for dim in shp:
            size *= dim
        out.append(flat[at:at + size].reshape(shp))
        at += size
    return out


def _to_padded_cols(w):
    zeros = jnp.zeros(w.shape[:-1] + (LANE - GLA_RANK,), w.dtype)
    return jnp.concatenate([w[..., :GA_ORIG], w[..., GA_ORIG + GLA_RANK:],
                            w[..., GA_ORIG:GA_ORIG + GLA_RANK], zeros], axis=-1)


def _from_padded_cols(w):
    return jnp.concatenate([w[..., :GA_ORIG], w[..., OFF_GA:OFF_GA + GLA_RANK], w[..., GA_ORIG:OFF_GA]], axis=-1)


def _layer_fwd(x, win, wout, g_pre, g_post, wa_pad, b_alpha, g_gla, g_att, rb_pad):
    h = _rms_fwd(x, g_pre)
    z = _matmul(h, win, "nn", F32, 512, D_ZP // 3, 512, "in_proj")
    y_gla, o_gla, states = _gla_fwd(z, wa_pad, b_alpha, g_gla)
    y_att, o_att = _att_fwd(z, rb_pad, g_att)
    ycat = jnp.concatenate([y_gla, y_att], axis=1)
    y = _matmul(ycat, wout, "nn", F32, 512, 1024, 512, "out_proj")
    out = _post_fwd(x, y, g_post)
    return out, (x, h, z, o_gla, states, o_att, ycat, y)


def _layer_bwd(dout, saved, win, wout, g_pre, g_post, wa_pad, b_alpha, g_gla, g_att, rb_pad):
    x, h, z, o_gla, states, o_att, ycat, y = saved
    dy, dg_post = _post_bwd(dout, y, g_post)
    dycat = _matmul(dy, wout, "nt", F32, 512, 1024, 512, "out_proj_dx")
    dwout = _matmul(ycat, dy, "tn", BF16, 512, 1024, 512, "out_proj_dw")
    dq, dk, dv, dgg, dga, dwa, db, dg_gla = _gla_bwd(dycat, o_gla, z, wa_pad, b_alpha, g_gla, states)
    daq, dak, dav, dag, drb, dg_att = _att_bwd(dycat, o_att, z, rb_pad, g_att)
    dz = jnp.concatenate([dq, dk, dv, dgg, daq, dak.astype(BF16), dav.astype(BF16), dag, dga], axis=1)
    dh = _matmul(dz, win, "nt", F32, 512, 1024, D_ZP // 3, "in_proj_dx")
    dwin = _matmul(h, dz, "tn", BF16, 512, D_ZP // 3, 512, "in_proj_dw")
    dx, dg_pre = _pre_bwd(dh, x, g_pre, dout)
    small = (dg_pre[0], dg_post[0], dwa[:GLA_RANK], db[0], dg_gla[0], dg_att[0], drb[:, 0, :N_REL])
    return dx, dwin, dwout, small


def kernel(x, w_in, w_out, g_pre, g_post, w_alpha, b_alpha, g_gla, g_att, rel_bias, loss_target, m_w_in, m_w_out, m_g_pre, m_g_post, m_w_alpha, m_b_alpha, m_g_gla, m_g_att, m_rel_bias, v_w_in, v_w_out, v_g_pre, v_g_post, v_w_alpha, v_b_alpha, v_g_gla, v_g_att, v_rel_bias):
    nl, d, cols = w_in.shape
    s = x.shape[1]
    x0 = x.reshape(s, d)
    tgt = loss_target.reshape(s, d)

    win_g, wout_g = _allgather([w_in.astype(BF16), w_out.astype(BF16)], "gather_weights")
    wa_g = _exchange([jnp.broadcast_to(_pack([w_alpha])[None], (N_DEV,) + _pack([w_alpha]).shape)], "gather_alpha")[0]
    wa_cols = w_alpha.shape[2]
    wa_full = wa_g.reshape(N_DEV, -1)[:, :nl * GLA_RANK * wa_cols].reshape(N_DEV, nl, GLA_RANK, wa_cols)
    wa_full = jnp.transpose(wa_full, (1, 2, 0, 3)).reshape(nl, GLA_RANK, GLA_KW)
    wa_pad = jnp.pad(wa_full, ((0, 0), (0, LANE - GLA_RANK), (0, 0)))
    rb_pad = jnp.pad(rel_bias, ((0, 0), (0, 0), (0, 3 * LANE - N_REL)))
    wins, wouts = [], []
    for l in range(nl):
        full = jnp.transpose(win_g[:, l], (1, 0, 2)).reshape(d, N_DEV * cols)
        wins.append(_to_padded_cols(full))
        wouts.append(wout_g[:, l].reshape(N_DEV * w_out.shape[1], d))

    def layer_args(l):
        return (wins[l], wouts[l], g_pre[l:l + 1], g_post[l:l + 1], wa_pad[l], b_alpha[l:l + 1],
                g_gla[l:l + 1], g_att[l:l + 1], rb_pad[l])

    act, saved = x0, []
    for l in range(nl):
        act, sv = _layer_fwd(act, *layer_args(l))
        saved.append(sv)
    dout, sq = _loss_head(act, tgt)
    loss = lax.psum(sq[0, 0] * (0.5 / d), ("x", "y", "c"))

    dwins, dwouts, smalls = [None] * nl, [None] * nl, [None] * nl
    for l in reversed(range(nl)):
        dout, dwins[l], dwouts[l], smalls[l] = _layer_bwd(dout, saved[l], *layer_args(l))
    grad_x = dout.reshape(x.shape)

    gin = jnp.stack([jnp.transpose(_from_padded_cols(dw).reshape(d, N_DEV, cols), (1, 0, 2)) for dw in dwins], axis=1)
    gout = jnp.stack([dw.reshape(N_DEV, w_out.shape[1], d) for dw in dwouts], axis=1)
    pin, pout = _exchange([gin, gout], "scatter_grads")
    g_w_in, d_w_in, m2_w_in, v2_w_in = _adam_sharded(pin, w_in, m_w_in, v_w_in, 256, "adam_w_in")
    g_w_out, d_w_out, m2_w_out, v2_w_out = _adam_sharded(pout, w_out, m_w_out, v_w_out, 128, "adam_w_out")

    names = 7
    small_stacked = [jnp.stack([smalls[l][i] for l in range(nl)]) for i in range(names)]
    shapes = [a.shape for a in small_stacked]
    packed = _pack(small_stacked)
    gathered = _exchange([jnp.broadcast_to(packed[None], (N_DEV,) + packed.shape)], "gather_small_grads")[0]
    g_pre_g, g_post_g, wa_g_full, b_g, gla_g, att_g, rb_g = _unpack(_sum_slots(gathered), shapes)
    my = _slot(_position())
    wa_g_mine = lax.dynamic_slice_in_dim(wa_g_full, my * wa_cols, wa_cols, axis=2)
    grads = [g_pre_g, g_post_g, wa_g_mine, b_g, gla_g, att_g, rb_g]
    ws = [g_pre, g_post, w_alpha, b_alpha, g_gla, g_att, rel_bias]
    ms = [m_g_pre, m_g_post, m_w_alpha, m_b_alpha, m_g_gla, m_g_att, m_rel_bias]
    vs = [v_g_pre, v_g_post, v_w_alpha, v_b_alpha, v_g_gla, v_g_att, v_rel_bias]
    shapes2 = [a.shape for a in ws]
    d_s, m2_s, v2_s = _adam_small(_pack(ws), _pack(grads), _pack(ms), _pack(vs))
    d_s, m2_s, v2_s = _unpack(d_s, shapes2), _unpack(m2_s, shapes2), _unpack(v2_s, shapes2)

    def ordered(big_in, big_out, small):
        return [big_in, big_out] + list(small)

    return (loss, grad_x,
            *ordered(g_w_in, g_w_out, grads),
            *ordered(d_w_in, d_w_out, d_s),
            *ordered(m2_w_in, m2_w_out, m2_s),
            *ordered(v2_w_in, v2_w_out, v2_s))
```

```python
import functools

import jax
import jax.numpy as jnp
from jax import lax
from jax.experimental import pallas as pl
from jax.experimental.pallas import tpu as pltpu

F32 = jnp.float32
BF16 = jnp.bfloat16
MESH = pl.DeviceIdType.MESH
ANY = pl.BlockSpec(memory_space=pl.ANY)

CHUNK = 64
GLA_HEADS = 4
GLA_DK = 128
GLA_DV = 256
GLA_KW = GLA_HEADS * GLA_DK
D_GLA = GLA_HEADS * GLA_DV
GLA_RANK = 16
GLA_TAU = 16.0
ATT_HEADS = 8
ATT_HD = 128
D_ATT = ATT_HEADS * ATT_HD
LEFT_CHUNKS = 8
REL_CLIP = 128
N_REL = 2 * REL_CLIP + 1
EPS = 1e-6
D_IN = 2 * GLA_KW + 2 * D_GLA + GLA_RANK + 4 * D_ATT
GLA_SCALE = GLA_DK ** -0.5
ATT_SCALE = ATT_HD ** -0.5

ADAM_LR = 0.001
ADAM_B1 = 0.9
ADAM_B2 = 0.999
ADAM_EPS = 1e-08
ADAM_WD = 0.01
ADAM_STEP = 10

N_DEV = 8
LANE = 128
GA_ORIG = 2 * GLA_KW + 2 * D_GLA
OFF_AQ = GA_ORIG
OFF_GA = GA_ORIG + 4 * D_ATT
D_ZP = OFF_GA + LANE
QB = 2 * CHUNK
WIN = (LEFT_CHUNKS + 2) * CHUNK
ET_ROWS = WIN + LEFT_CHUNKS * CHUNK
NEG = -1e30
VMEM_LIMIT = 48 * 1024 * 1024


def _cparams(sem):
    return pltpu.CompilerParams(dimension_semantics=sem, vmem_limit_bytes=VMEM_LIMIT)


def _dot(a, b):
    return jnp.dot(a, b, preferred_element_type=F32)


def _dot_nt(a, b):
    return lax.dot_general(a, b, (((1,), (1,)), ((), ())), preferred_element_type=F32)


def _dot_tn(a, b):
    return lax.dot_general(a, b, (((0,), (0,)), ((), ())), preferred_element_type=F32)


def _dot01(t, x, left=True):
    if not left:
        t, x = x, t
    hi = x.astype(BF16)
    r = x - hi.astype(F32)
    mid = r.astype(BF16)
    lo = (r - mid.astype(F32)).astype(BF16)
    if left:
        return _dot(t, hi) + _dot(t, mid) + _dot(t, lo)
    return _dot(hi, t) + _dot(mid, t) + _dot(lo, t)


def _sigmoid(x):
    return 1.0 / (1.0 + jnp.exp(-x))


def _log_sigmoid(x):
    return jnp.minimum(x, 0.0) - jnp.log(1.0 + jnp.exp(-jnp.abs(x)))


def _matmul(a, b, mode, out_dtype, tm, tn, tk, name, b_layer=None):
    bshape = b.shape if b_layer is None else b.shape[1:]
    if mode == "nn":
        (m, k), n = a.shape, bshape[1]
    elif mode == "nt":
        (m, k), n = a.shape, bshape[0]
    else:
        (k, m), n = a.shape, bshape[1]
    tm, tn, tk = min(tm, m), min(tn, n), min(tk, k)
    assert m % tm == 0 and n % tn == 0 and k % tk == 0, (name, m, n, k)
    nk = k // tk

    def body(a_ref, b_ref, o_ref, acc_ref):
        kk = pl.program_id(2)

        @pl.when(kk == 0)
        def _():
            acc_ref[...] = jnp.zeros_like(acc_ref)

        if mode == "nn":
            acc_ref[...] += _dot(a_ref[...], b_ref[...])
        elif mode == "nt":
            acc_ref[...] += _dot_nt(a_ref[...], b_ref[...])
        else:
            acc_ref[...] += _dot_tn(a_ref[...], b_ref[...])

        @pl.when(kk == nk - 1)
        def _():
            o_ref[...] = acc_ref[...].astype(out_dtype)

    if mode == "tn":
        a_spec = pl.BlockSpec((tk, tm), lambda i, j, kk: (kk, i))
    else:
        a_spec = pl.BlockSpec((tm, tk), lambda i, j, kk: (i, kk))
    b_blk, b_idx = ((tn, tk), lambda i, j, kk: (j, kk)) if mode == "nt" else ((tk, tn), lambda i, j, kk: (kk, j))
    if b_layer is None:
        b_spec = pl.BlockSpec(b_blk, b_idx)
    else:
        b_spec = pl.BlockSpec((None,) + b_blk, lambda i, j, kk: (b_layer,) + b_idx(i, j, kk))
    return pl.pallas_call(
        body, name=name,
        grid=(m // tm, n // tn, nk),
        in_specs=[a_spec, b_spec],
        out_specs=pl.BlockSpec((tm, tn), lambda i, j, kk: (i, j)),
        out_shape=jax.ShapeDtypeStruct((m, n), out_dtype),
        scratch_shapes=[pltpu.VMEM((tm, tn), F32)],
        compiler_params=_cparams(("parallel", "parallel", "arbitrary")),
    )(a, b)


ROWS = 256


def _rms_fwd(x, g):
    s, d = x.shape

    def body(x_ref, g_ref, h_ref):
        xv = x_ref[...]
        r = lax.rsqrt(jnp.mean(xv * xv, axis=-1, keepdims=True) + EPS)
        h_ref[...] = (xv * r * g_ref[...]).astype(BF16)

    return pl.pallas_call(
        body, name="rms_fwd", grid=(s // ROWS,),
        in_specs=[pl.BlockSpec((ROWS, d), lambda i: (i, 0)), pl.BlockSpec((1, d), lambda i: (0, 0))],
        out_specs=pl.BlockSpec((ROWS, d), lambda i: (i, 0)),
        out_shape=jax.ShapeDtypeStruct((s, d), BF16),
        compiler_params=_cparams(("parallel",)),
    )(x, g)


def _post_fwd(x, y, g):
    s, d = x.shape

    def body(x_ref, y_ref, g_ref, o_ref):
        yv = y_ref[...]
        r = lax.rsqrt(jnp.mean(yv * yv, axis=-1, keepdims=True) + EPS)
        o_ref[...] = x_ref[...] + yv * r * g_ref[...]

    row = pl.BlockSpec((ROWS, d), lambda i: (i, 0))
    return pl.pallas_call(
        body, name="post_fwd", grid=(s // ROWS,),
        in_specs=[row, row, pl.BlockSpec((1, d), lambda i: (0, 0))],
        out_specs=row,
        out_shape=jax.ShapeDtypeStruct((s, d), F32),
        compiler_params=_cparams(("parallel",)),
    )(x, y, g)


def _loss_head(out, tgt):
    s, d = out.shape

    def body(o_ref, t_ref, dout_ref, sum_ref):
        @pl.when(pl.program_id(0) == 0)
        def _():
            sum_ref[...] = jnp.zeros_like(sum_ref)

        e = o_ref[...] - t_ref[...]
        dout_ref[...] = e * (1.0 / d)
        sum_ref[...] += jnp.sum(jnp.sum(e * e, axis=1, keepdims=True), axis=0, keepdims=True)

    row = pl.BlockSpec((ROWS, d), lambda i: (i, 0))
    return pl.pallas_call(
        body, name="loss_head", grid=(s // ROWS,),
        in_specs=[row, row],
        out_specs=[row, pl.BlockSpec((1, 1), lambda i: (0, 0))],
        out_shape=[jax.ShapeDtypeStruct((s, d), F32), jax.ShapeDtypeStruct((1, 1), F32)],
        compiler_params=_cparams(("arbitrary",)),
    )(out, tgt)


def _post_bwd(dout, y, g):
    s, d = y.shape

    def body(do_ref, y_ref, g_ref, dy_ref, dg_ref):
        @pl.when(pl.program_id(0) == 0)
        def _():
            dg_ref[...] = jnp.zeros_like(dg_ref)

        yv = y_ref[...]
        dv = do_ref[...]
        r = lax.rsqrt(jnp.mean(yv * yv, axis=-1, keepdims=True) + EPS)
        dg_ref[...] += jnp.sum(dv * yv * r, axis=0, keepdims=True)
        w = dv * g_ref[...]
        dy = r * (w - yv * (r * r) * jnp.mean(w * yv, axis=-1, keepdims=True))
        dy_ref[...] = dy.astype(BF16)

    row = pl.BlockSpec((ROWS, d), lambda i: (i, 0))
    vec = pl.BlockSpec((1, d), lambda i: (0, 0))
    return pl.pallas_call(
        body, name="post_bwd", grid=(s // ROWS,),
        in_specs=[row, row, vec],
        out_specs=[row, vec],
        out_shape=[jax.ShapeDtypeStruct((s, d), BF16), jax.ShapeDtypeStruct((1, d), F32)],
        compiler_params=_cparams(("arbitrary",)),
    )(dout, y, g)


def _pre_bwd(dh, x, g, dout):
    s, d = x.shape

    def body(dh_ref, x_ref, g_ref, do_ref, dx_ref, dg_ref):
        @pl.when(pl.program_id(0) == 0)
        def _():
            dg_ref[...] = jnp.zeros_like(dg_ref)

        xv = x_ref[...]
        dv = dh_ref[...]
        r = lax.rsqrt(jnp.mean(xv * xv, axis=-1, keepdims=True) + EPS)
        dg_ref[...] += jnp.sum(dv * xv * r, axis=0, keepdims=True)
        w = dv * g_ref[...]
        dx_ref[...] = do_ref[...] + r * (w - xv * (r * r) * jnp.mean(w * xv, axis=-1, keepdims=True))

    row = pl.BlockSpec((ROWS, d), lambda i: (i, 0))
    vec = pl.BlockSpec((1, d), lambda i: (0, 0))
    return pl.pallas_call(
        body, name="pre_bwd", grid=(s // ROWS,),
        in_specs=[row, row, vec, row],
        out_specs=[row, vec],
        out_shape=[jax.ShapeDtypeStruct((s, d), F32), jax.ShapeDtypeStruct((1, d), F32)],
        compiler_params=_cparams(("arbitrary",)),
    )(dh, x, g, dout)


def _gla_gate(ga_b, wa_ref, b_ref, cs, tri):
    pre = _dot(ga_b, wa_ref[:, cs].astype(BF16)) + b_ref[:, cs]
    la = _log_sigmoid(pre) * (1.0 / GLA_TAU)
    cum = _dot01(tri, la)
    last = lax.broadcasted_iota(jnp.int32, cum.shape, 0) == CHUNK - 1
    return pre, cum, jnp.sum(jnp.where(last, cum, 0.0), axis=0, keepdims=True)


def _z_specs_gla(rev=None):
    idx = (lambda n: n) if rev is None else rev
    return [
        pl.BlockSpec((CHUNK, GLA_KW), lambda n: (idx(n), 0)),
        pl.BlockSpec((CHUNK, GLA_KW), lambda n: (idx(n), 1)),
        pl.BlockSpec((CHUNK, D_GLA), lambda n: (idx(n), 1)),
        pl.BlockSpec((CHUNK, D_GLA), lambda n: (idx(n), 2)),
        pl.BlockSpec((CHUNK, LANE), lambda n: (idx(n), OFF_GA // LANE)),
    ]


def _gla_fwd(z, wa_pad, b_alpha, g_gla):
    s = z.shape[0]
    nchunk = s // CHUNK

    def body(q_ref, k_ref, v_ref, gg_ref, ga_ref, wa_ref, b_ref, g_ref, y_ref, o_ref, st_ref, state):
        @pl.when(pl.program_id(0) == 0)
        def _():
            state[...] = jnp.zeros_like(state)

        ga_b = ga_ref[...].astype(BF16)
        ri = lax.broadcasted_iota(jnp.int32, (CHUNK, CHUNK), 0)
        ci = lax.broadcasted_iota(jnp.int32, (CHUNK, CHUNK), 1)
        tri = jnp.where(ri >= ci, 1.0, 0.0).astype(BF16)
        for h in range(GLA_HEADS):
            cs = slice(h * GLA_DK, (h + 1) * GLA_DK)
            vs = slice(h * GLA_DV, (h + 1) * GLA_DV)
            _, cum, cend = _gla_gate(ga_b, wa_ref, b_ref, cs, tri)
            kd = k_ref[:, cs] * jnp.exp(cend - cum)
            st = state[h] * jnp.exp(cend) + _dot_tn(v_ref[:, vs].astype(BF16), kd.astype(BF16))
            state[h] = st
            st_ref[0, h] = st
            qs = (q_ref[:, cs] * GLA_SCALE).astype(BF16)
            o = _dot_nt(qs, st.astype(BF16))
            o_ref[:, vs] = o
            r = lax.rsqrt(jnp.mean(o * o, axis=-1, keepdims=True) + EPS)
            gg = gg_ref[:, vs]
            y_ref[:, vs] = (o * r * g_ref[:, vs] * (gg * _sigmoid(gg))).astype(BF16)

    full = lambda shape: pl.BlockSpec(shape, lambda n: tuple(0 for _ in shape))
    wide = pl.BlockSpec((CHUNK, D_GLA), lambda n: (n, 0))
    return pl.pallas_call(
        body, name="gla_fwd", grid=(nchunk,),
        in_specs=_z_specs_gla() + [full((LANE, GLA_KW)), full((1, GLA_KW)), full((1, D_GLA))],
        out_specs=[wide, wide, pl.BlockSpec((1, GLA_HEADS, GLA_DV, GLA_DK), lambda n: (n, 0, 0, 0))],
        out_shape=[jax.ShapeDtypeStruct((s, D_GLA), BF16), jax.ShapeDtypeStruct((s, D_GLA), F32),
                   jax.ShapeDtypeStruct((nchunk, GLA_HEADS, GLA_DV, GLA_DK), F32)],
        scratch_shapes=[pltpu.VMEM((GLA_HEADS, GLA_DV, GLA_DK), F32)],
        compiler_params=_cparams(("arbitrary",)),
    )(z, z, z, z, z, wa_pad, b_alpha, g_gla)


def _gla_bwd(dyc, o_gla, z, wa_pad, b_alpha, g_gla, states):
    s = z.shape[0]
    nchunk = s // CHUNK
    rev = lambda n: nchunk - 1 - n

    def body(dy_ref, o_ref, q_ref, k_ref, v_ref, gg_ref, ga_ref, wa_ref, b_ref, g_ref, st_ref, stp_ref,
             dq_ref, dk_ref, dv_ref, dgg_ref, dga_ref, dwa_ref, db_ref, dg_ref, carry):
        step = pl.program_id(0)

        @pl.when(step == 0)
        def _():
            carry[...] = jnp.zeros_like(carry)
            dwa_ref[...] = jnp.zeros_like(dwa_ref)
            db_ref[...] = jnp.zeros_like(db_ref)
            dg_ref[...] = jnp.zeros_like(dg_ref)

        has_prev = (step < nchunk - 1).astype(F32)
        ga_b = ga_ref[...].astype(BF16)
        ri = lax.broadcasted_iota(jnp.int32, (CHUNK, CHUNK), 0)
        ci = lax.broadcasted_iota(jnp.int32, (CHUNK, CHUNK), 1)
        tri = jnp.where(ri >= ci, 1.0, 0.0).astype(BF16)
        tri_up = jnp.where(ci >= ri, 1.0, 0.0).astype(BF16)
        dga = jnp.zeros((CHUNK, LANE), F32)
        for h in range(GLA_HEADS):
            cs = slice(h * GLA_DK, (h + 1) * GLA_DK)
            vs = slice(h * GLA_DV, (h + 1) * GLA_DV)
            pre, cum, cend = _gla_gate(ga_b, wa_ref, b_ref, cs, tri)
            e = jnp.exp(cend - cum)
            a = jnp.exp(cend)
            kf = k_ref[:, cs]
            kd_b = (kf * e).astype(BF16)
            v_b = v_ref[:, vs].astype(BF16)
            qs = (q_ref[:, cs] * GLA_SCALE).astype(BF16)
            o = o_ref[:, vs]
            gg = gg_ref[:, vs]
            g = g_ref[:, vs]
            dy = dy_ref[:, vs]
            r = lax.rsqrt(jnp.mean(o * o, axis=-1, keepdims=True) + EPS)
            sg = _sigmoid(gg)
            dogn = dy * (gg * sg)
            dgg_ref[:, vs] = (dy * (o * r * g) * (sg * (1.0 + gg * (1.0 - sg)))).astype(BF16)
            dg_ref[:, vs] += jnp.sum(dogn * o * r, axis=0, keepdims=True)
            w = dogn * g
            do_b = (r * (w - o * (r * r) * jnp.mean(w * o, axis=-1, keepdims=True))).astype(BF16)
            st = st_ref[0, h]
            dq_ref[:, cs] = (_dot(do_b, st.astype(BF16)) * GLA_SCALE).astype(BF16)
            gt = _dot_tn(do_b, qs) + carry[h]
            gt_b = gt.astype(BF16)
            dkd = _dot(v_b, gt_b)
            dv_ref[:, vs] = _dot_nt(kd_b, gt_b).astype(BF16)
            da = jnp.sum(gt * (stp_ref[0, h] * has_prev), axis=0, keepdims=True)
            carry[h] = gt * a
            dk_ref[:, cs] = (dkd * e).astype(BF16)
            dd = dkd * kf * e
            dcend = jnp.sum(dd, axis=0, keepdims=True) + da * a
            dla = dcend - _dot01(tri_up, dd)
            dpre = dla * (1.0 / GLA_TAU) * (1.0 - _sigmoid(pre))
            dpre_b = dpre.astype(BF16)
            dga = dga + _dot_nt(dpre_b, wa_ref[:, cs].astype(BF16))
            dwa_ref[:, cs] += _dot_tn(ga_b, dpre_b)
            db_ref[:, cs] += jnp.sum(dpre, axis=0, keepdims=True)
        dga_ref[...] = dga.astype(BF16)

    full = lambda shape: pl.BlockSpec(shape, lambda n: tuple(0 for _ in shape))
    wide = pl.BlockSpec((CHUNK, D_GLA), lambda n: (rev(n), 0))
    keyw = pl.BlockSpec((CHUNK, GLA_KW), lambda n: (rev(n), 0))
    st_spec = pl.BlockSpec((1, GLA_HEADS, GLA_DV, GLA_DK), lambda n: (rev(n), 0, 0, 0))
    stp_spec = pl.BlockSpec((1, GLA_HEADS, GLA_DV, GLA_DK), lambda n: (jnp.maximum(rev(n) - 1, 0), 0, 0, 0))
    return pl.pallas_call(
        body, name="gla_bwd", grid=(nchunk,),
        in_specs=[wide, wide] + _z_specs_gla(rev)
        + [full((LANE, GLA_KW)), full((1, GLA_KW)), full((1, D_GLA)), st_spec, stp_spec],
        out_specs=[keyw, keyw, wide, wide, pl.BlockSpec((CHUNK, LANE), lambda n: (rev(n), 0)),
                   full((LANE, GLA_KW)), full((1, GLA_KW)), full((1, D_GLA))],
        out_shape=[jax.ShapeDtypeStruct((s, GLA_KW), BF16), jax.ShapeDtypeStruct((s, GLA_KW), BF16),
                   jax.ShapeDtypeStruct((s, D_GLA), BF16), jax.ShapeDtypeStruct((s, D_GLA), BF16),
                   jax.ShapeDtypeStruct((s, LANE), BF16),
                   jax.ShapeDtypeStruct((LANE, GLA_KW), F32), jax.ShapeDtypeStruct((1, GLA_KW), F32),
                   jax.ShapeDtypeStruct((1, D_GLA), F32)],
        scratch_shapes=[pltpu.VMEM((GLA_HEADS, GLA_DV, GLA_DK), F32)],
        compiler_params=_cparams(("arbitrary",)),
    )(dyc, o_gla, z, z, z, z, z, wa_pad, b_alpha, g_gla, states, states)


def _build_bias_table(rb_row, et_ref):
    far = jnp.broadcast_to(rb_row[:, 2 * REL_CLIP:2 * REL_CLIP + 1], (1, LANE))
    near_hi = rb_row[:, REL_CLIP:2 * REL_CLIP]
    near_lo = rb_row[:, 0:REL_CLIP]
    past = jnp.broadcast_to(rb_row[:, 0:1], (1, LANE))
    seg = [far, far, far, far, near_hi, near_lo] + [past] * (ET_ROWS // LANE - 5)
    ri = lax.broadcasted_iota(jnp.int32, (LANE, LANE), 0)
    ci = lax.broadcasted_iota(jnp.int32, (LANE, LANE), 1)
    for kb in range(ET_ROWS // LANE):
        wmat = jnp.where(ri + ci < LANE, seg[kb], seg[kb + 1])
        blk = pltpu.roll(wmat, 0, 1, stride=1, stride_axis=0)
        lag = LEFT_CHUNKS + ci // CHUNK - (2 * kb + ri // CHUNK)
        et_ref[kb * LANE:(kb + 1) * LANE, :] = jnp.where((lag >= 0) & (lag <= LEFT_CHUNKS), blk, NEG)


def _reduce_bias_table(det_ref):
    lane = lax.broadcasted_iota(jnp.int32, (1, LANE), 1)
    ri = lax.broadcasted_iota(jnp.int32, (LANE, LANE), 0)
    ci = lax.broadcasted_iota(jnp.int32, (LANE, LANE), 1)
    flip = jnp.where(ri + ci == LANE - 1, 1.0, 0.0).astype(BF16)
    segs = jnp.zeros((8, LANE), F32)
    seg_row = lax.broadcasted_iota(jnp.int32, (8, LANE), 0)
    prev_minus = jnp.zeros((1, LANE), F32)
    for kb in range(6):
        rolled = pltpu.roll(_dot01(det_ref[kb * LANE:(kb + 1) * LANE, :], flip, left=False), 0, 1,
                            stride=1, stride_axis=0)
        plus = jnp.sum(jnp.where(ci >= ri, rolled, 0.0), axis=0, keepdims=True)
        minus = jnp.sum(jnp.where(ci < ri, rolled, 0.0), axis=0, keepdims=True)
        segs = segs + jnp.where(seg_row == kb, plus + prev_minus, 0.0)
        prev_minus = minus
    segs = _dot01(segs, flip, left=False)
    pick = lambda kb: jnp.sum(jnp.where(seg_row == kb, segs, 0.0), axis=0, keepdims=True)
    far = jnp.sum(pick(0) + pick(1) + pick(2) + pick(3), axis=1, keepdims=True)
    last = jnp.where(lane == 0, far, 0.0)
    return jnp.concatenate([pick(5), pick(4), last], axis=1)


def _att_window(b):
    c0 = 2 * b
    kstart = pl.multiple_of(jnp.maximum(c0 - LEFT_CHUNKS, 0) * CHUNK, CHUNK)
    eoff = pl.multiple_of(jnp.maximum(LEFT_CHUNKS - c0, 0) * CHUNK, CHUNK)
    return kstart, eoff


def _att_probs(q_b, kw_b, et):
    st = _dot_nt(kw_b, q_b) * ATT_SCALE + et
    m = jnp.max(st, axis=0, keepdims=True)
    ex = jnp.exp(st - m)
    return ex / jnp.sum(ex, axis=0, keepdims=True)


def _att_fwd(z, rb_pad, g_att):
    s = z.shape[0]
    nblk = s // QB
    c_aq, c_ak, c_av, c_ag = [(OFF_AQ + i * D_ATT) // ATT_HD for i in range(4)]

    def body(q_ref, k_ref, v_ref, ag_ref, rb_ref, g_ref, y_ref, o_ref, et_ref):
        h = pl.program_id(0)
        b = pl.program_id(1)

        @pl.when(b == 0)
        def _():
            _build_bias_table(rb_ref[pl.ds(h, 1), :], et_ref)

        kstart, eoff = _att_window(b)
        q_b = q_ref[...].astype(BF16)
        kw_b = k_ref[pl.ds(kstart, WIN), :].astype(BF16)
        vw_b = v_ref[pl.ds(kstart, WIN), :].astype(BF16)
        pt = _att_probs(q_b, kw_b, et_ref[pl.ds(eoff, WIN), :])
        o = _dot_tn(pt.astype(BF16), vw_b)
        o_ref[...] = o
        r = lax.rsqrt(jnp.mean(o * o, axis=-1, keepdims=True) + EPS)
        ag = ag_ref[...]
        y_ref[...] = (o * r * g_ref[...] * (ag * _sigmoid(ag))).astype(BF16)

    blk = lambda col: pl.BlockSpec((QB, ATT_HD), lambda h, b: (b, col + h))
    seq = lambda col: pl.BlockSpec((s, ATT_HD), lambda h, b: (0, col + h))
    out_blk = pl.BlockSpec((QB, ATT_HD), lambda h, b: (b, h))
    return pl.pallas_call(
        body, name="att_fwd", grid=(ATT_HEADS, nblk),
        in_specs=[blk(c_aq), seq(c_ak), seq(c_av), blk(c_ag),
                  pl.BlockSpec((ATT_HEADS, 3 * LANE), lambda h, b: (0, 0)),
                  pl.BlockSpec((1, ATT_HD), lambda h, b: (0, h))],
        out_specs=[out_blk, out_blk],
        out_shape=[jax.ShapeDtypeStruct((s, D_ATT), BF16), jax.ShapeDtypeStruct((s, D_ATT), F32)],
        scratch_shapes=[pltpu.VMEM((ET_ROWS, LANE), F32)],
        compiler_params=_cparams(("arbitrary", "arbitrary")),
    )(z, z, z, z, rb_pad, g_att)


def _att_bwd(dyc, o_att, z, rb_pad, g_att):
    s = z.shape[0]
    nblk = s // QB
    c_aq, c_ak, c_av, c_ag = [(OFF_AQ + i * D_ATT) // ATT_HD for i in range(4)]
    c_dy = D_GLA // ATT_HD

    def body(dy_ref, o_ref, q_ref, k_ref, v_ref, ag_ref, rb_ref, g_ref,
             dq_ref, dk_ref, dv_ref, dag_ref, drb_ref, dg_ref, et_ref, det_ref):
        h = pl.program_id(0)
        b = pl.program_id(1)

        @pl.when(b == 0)
        def _():
            _build_bias_table(rb_ref[pl.ds(h, 1), :], et_ref)
            det_ref[...] = jnp.zeros_like(det_ref)
            dk_ref[...] = jnp.zeros_like(dk_ref)
            dv_ref[...] = jnp.zeros_like(dv_ref)
            dg_ref[...] = jnp.zeros_like(dg_ref)

        kstart, eoff = _att_window(b)
        q_b = q_ref[...].astype(BF16)
        kw_b = k_ref[pl.ds(kstart, WIN), :].astype(BF16)
        vw_b = v_ref[pl.ds(kstart, WIN), :].astype(BF16)
        pt = _att_probs(q_b, kw_b, et_ref[pl.ds(eoff, WIN), :])
        o = o_ref[...]
        ag = ag_ref[...]
        g = g_ref[...]
        dy = dy_ref[...]
        r = lax.rsqrt(jnp.mean(o * o, axis=-1, keepdims=True) + EPS)
        sg = _sigmoid(ag)
        don = dy * (ag * sg)
        dag_ref[...] = (dy * (o * r * g) * (sg * (1.0 + ag * (1.0 - sg)))).astype(BF16)
        dg_ref[...] += jnp.sum(don * o * r, axis=0, keepdims=True)
        w = don * g
        do_b = (r * (w - o * (r * r) * jnp.mean(w * o, axis=-1, keepdims=True))).astype(BF16)
        pt_b = pt.astype(BF16)
        dpt = _dot_nt(vw_b, do_b)
        dst = pt * (dpt - jnp.sum(dpt * pt, axis=0, keepdims=True))
        det_ref[pl.ds(eoff, WIN), :] += dst
        ds_b = (dst * ATT_SCALE).astype(BF16)
        dq_ref[...] = _dot_tn(ds_b, kw_b).astype(BF16)
        dk_ref[pl.ds(kstart, WIN), :] += _dot(ds_b, q_b)
        dv_ref[pl.ds(kstart, WIN), :] += _dot(pt_b, do_b)

        @pl.when(b == nblk - 1)
        def _():
            drb_ref[0] = jnp.broadcast_to(_reduce_bias_table(det_ref), (8, 3 * LANE))

    blk = lambda col: pl.BlockSpec((QB, ATT_HD), lambda h, b: (b, col + h))
    seq = lambda col: pl.BlockSpec((s, ATT_HD), lambda h, b: (0, col + h))
    out_blk = pl.BlockSpec((QB, ATT_HD), lambda h, b: (b, h))
    out_seq = pl.BlockSpec((s, ATT_HD), lambda h, b: (0, h))
    return pl.pallas_call(
        body, name="att_bwd", grid=(ATT_HEADS, nblk),
        in_specs=[blk(c_dy), blk(0), blk(c_aq), seq(c_ak), seq(c_av), blk(c_ag),
                  pl.BlockSpec((ATT_HEADS, 3 * LANE), lambda h, b: (0, 0)),
                  pl.BlockSpec((1, ATT_HD), lambda h, b: (0, h))],
        out_specs=[out_blk, out_seq, out_seq, out_blk,
                   pl.BlockSpec((1, 8, 3 * LANE), lambda h, b: (h, 0, 0)),
                   pl.BlockSpec((1, ATT_HD), lambda h, b: (0, h))],
        out_shape=[jax.ShapeDtypeStruct((s, D_ATT), BF16), jax.ShapeDtypeStruct((s, D_ATT), F32),
                   jax.ShapeDtypeStruct((s, D_ATT), F32), jax.ShapeDtypeStruct((s, D_ATT), BF16),
                   jax.ShapeDtypeStruct((ATT_HEADS, 8, 3 * LANE), F32),
                   jax.ShapeDtypeStruct((1, D_ATT), F32)],
        scratch_shapes=[pltpu.VMEM((ET_ROWS, LANE), F32), pltpu.VMEM((ET_ROWS, LANE), F32)],
        compiler_params=_cparams(("arbitrary", "arbitrary")),
    )(dyc, o_att, z, z, z, z, rb_pad, g_att)


def _adam_math(w, g, m, v):
    m2 = ADAM_B1 * m + (1.0 - ADAM_B1) * g
    v2 = ADAM_B2 * v + (1.0 - ADAM_B2) * (g * g)
    m_hat = m2 / (1.0 - ADAM_B1 ** ADAM_STEP)
    v_hat = v2 / (1.0 - ADAM_B2 ** ADAM_STEP)
    delta = -ADAM_LR * (m_hat / (jnp.sqrt(v_hat) + ADAM_EPS) + ADAM_WD * w)
    return delta, m2, v2


def _adam_sharded(parts, w, m, v, rows, name):
    nl, nr, nc = w.shape

    def body(p_ref, w_ref, m_ref, v_ref, g_ref, d_ref, m2_ref, v2_ref):
        g = p_ref[0, 0].astype(F32)
        for dev in range(1, N_DEV):
            g = g + p_ref[dev, 0].astype(F32)
        delta, m2, v2 = _adam_math(w_ref[0], g, m_ref[0], v_ref[0])
        g_ref[0] = g
        d_ref[0] = delta
        m2_ref[0] = m2
        v2_ref[0] = v2

    blk = pl.BlockSpec((1, rows, nc), lambda l, i: (l, i, 0))
    shp = jax.ShapeDtypeStruct(w.shape, F32)
    return pl.pallas_call(
        body, name=name, grid=(nl, pl.cdiv(nr, rows)),
        in_specs=[pl.BlockSpec((N_DEV, 1, rows, nc), lambda l, i: (0, l, i, 0)), blk, blk, blk],
        out_specs=[blk, blk, blk, blk],
        out_shape=[shp, shp, shp, shp],
        compiler_params=_cparams(("parallel", "parallel")),
    )(parts, w, m, v)


def _adam_small(w, g, m, v):
    def body(w_ref, g_ref, m_ref, v_ref, d_ref, m2_ref, v2_ref):
        delta, m2, v2 = _adam_math(w_ref[...], g_ref[...], m_ref[...], v_ref[...])
        d_ref[...] = delta
        m2_ref[...] = m2
        v2_ref[...] = v2

    shp = jax.ShapeDtypeStruct(w.shape, F32)
    return pl.pallas_call(body, name="adam_small", out_shape=[shp, shp, shp])(w, g, m, v)


def _position():
    return lax.axis_index("x"), lax.axis_index("y"), lax.axis_index("c")


def _slot(p):
    return 4 * p[0] + 2 * p[1] + p[2]


def _to_aligned_rows(w):
    zeros = jnp.zeros((D_ZP - D_IN,) + w.shape[1:], w.dtype)
    return jnp.concatenate([w[:GA_ORIG], w[GA_ORIG + GLA_RANK:], w[GA_ORIG:GA_ORIG + GLA_RANK], zeros], axis=0)


def _from_aligned_rows(w):
    return jnp.concatenate([w[:GA_ORIG], w[OFF_GA:OFF_GA + GLA_RANK], w[GA_ORIG:OFF_GA]], axis=0)


def _gather_weights(win_t, wout):
    nl, cols, d = win_t.shape
    rows = wout.shape[1]
    per = 2

    def body(win_ref, wout_ref, wt_ref, wo_ref, send_sems, recv_sems, local_sems):
        x, y, c = _position()
        me, sibling = (x, y, c), (x, y, 1 - c)
        chips = [(1 - x, y), (x, 1 - y), (1 - x, 1 - y)]

        def places(block):
            dev = _slot(block)
            return [wt_ref.at[:, dev], wo_ref.at[:, pl.ds(pl.multiple_of(dev * rows, rows), rows), :]]

        own = [win_ref, wout_ref]

        def copies(k, block, to, from_input=False):
            dsts = places(block)
            srcs = own if from_input else dsts
            return [pltpu.make_async_remote_copy(
                src_ref=srcs[i], dst_ref=dsts[i], send_sem=send_sems.at[k * per + i],
                recv_sem=recv_sems.at[k * per + i], device_id=to, device_id_type=MESH) for i in range(per)]

        mine = [pltpu.make_async_copy(own[i], dst, local_sems.at[i]) for i, dst in enumerate(places(me))]
        for cp in mine:
            cp.start()
        first = copies(0, me, sibling, from_input=True)
        for j, chip in enumerate(chips):
            first += copies(1 + j, me, (*chip, c), from_input=True)
        for cp in first:
            cp.start()
        passed = []
        for j, chip in enumerate(chips):
            for cp in copies(1 + j, (*chip, c), me):
                cp.wait_recv()
            fwd = copies(4 + j, (*chip, c), sibling)
            for cp in fwd:
                cp.start()
            passed += fwd
        for cp in copies(0, sibling, me):
            cp.wait_recv()
        for j, chip in enumerate(chips):
            for cp in copies(4 + j, (*chip, 1 - c), me):
                cp.wait_recv()
        for cp in first + passed:
            cp.wait_send()
        for cp in mine:
            cp.wait()

    return pl.pallas_call(
        body, name="gather_weights",
        in_specs=[ANY] * 2, out_specs=[ANY] * 2,
        out_shape=[jax.ShapeDtypeStruct((nl, N_DEV, cols, d), win_t.dtype),
                   jax.ShapeDtypeStruct((nl, N_DEV * rows, d), wout.dtype)],
        scratch_shapes=[pltpu.SemaphoreType.DMA((7 * per,)), pltpu.SemaphoreType.DMA((7 * per,)),
                        pltpu.SemaphoreType.DMA((per,))],
    )(win_t, wout)


def _scatter_grads(dwts, dwos, rows):
    nl = len(dwts)
    _, cols, d = dwts[0].shape

    def body(*refs):
        dwt_refs, dwo_refs = refs[:nl], refs[nl:2 * nl]
        pin_ref, pout_ref, send_sems, recv_sems, local_sems = refs[2 * nl:]
        me = _position()
        mine = _slot(me)

        def send(to_slot, make):
            out = []
            for l in range(nl):
                out.append(make(0, dwt_refs[l].at[to_slot], pin_ref.at[mine, l]))
                out.append(make(1, dwo_refs[l].at[pl.ds(pl.multiple_of(to_slot * rows, rows), rows), :],
                                pout_ref.at[mine, l]))
            return out

        def whole(k):
            peer = _peer(me, k)
            slot = _slot(peer)
            return [pltpu.make_async_remote_copy(
                src_ref=ref.at[slot], dst_ref=ref.at[slot], send_sem=send_sems.at[2 * (k - 1) + i],
                recv_sem=recv_sems.at[2 * (k - 1) + i], device_id=peer, device_id_type=MESH)
                for i, ref in enumerate((pin_ref, pout_ref))]

        local = send(mine, lambda i, src, dst: pltpu.make_async_copy(src, dst, local_sems.at[i]))
        for cp in local:
            cp.start()
        for k in range(1, N_DEV):
            peer = _peer(me, k)
            for cp in send(_slot(peer), lambda i, src, dst: pltpu.make_async_remote_copy(
                    src_ref=src, dst_ref=dst, send_sem=send_sems.at[2 * (k - 1) + i],
                    recv_sem=recv_sems.at[2 * (k - 1) + i], device_id=peer, device_id_type=MESH)):
                cp.start()
        for k in range(1, N_DEV):
            for cp in whole(k):
                cp.wait_recv()
        for k in range(1, N_DEV):
            for cp in whole(k):
                cp.wait_send()
        for i, ref in enumerate((pin_ref, pout_ref)):
            pltpu.make_async_copy(ref.at[mine], ref.at[mine], local_sems.at[i]).wait()

    return pl.pallas_call(
        body, name="scatter_grads",
        in_specs=[ANY] * (2 * nl), out_specs=[ANY] * 2,
        out_shape=[jax.ShapeDtypeStruct((N_DEV, nl, cols, d), dwts[0].dtype),
                   jax.ShapeDtypeStruct((N_DEV, nl, rows, d), dwos[0].dtype)],
        scratch_shapes=[pltpu.SemaphoreType.DMA((14,)), pltpu.SemaphoreType.DMA((14,)),
                        pltpu.SemaphoreType.DMA((2,))],
    )(*dwts, *dwos)


def _peer(pos, k):
    x, y, c = pos
    return (1 - x if k & 4 else x, 1 - y if k & 2 else y, 1 - c if k & 1 else c)


def _exchange(arrs, name):
    n = len(arrs)

    def body(*refs):
        ins, outs = refs[:n], refs[n:2 * n]
        send_sems, recv_sems, local_sems = refs[2 * n:]
        me = _position()

        def copy(a, k):
            peer = _peer(me, k)
            return pltpu.make_async_remote_copy(
                src_ref=ins[a].at[_slot(peer)], dst_ref=outs[a].at[_slot(me)],
                send_sem=send_sems.at[a * 7 + k - 1], recv_sem=recv_sems.at[a * 7 + k - 1],
                device_id=peer, device_id_type=MESH)

        def landed(a, k):
            peer = _peer(me, k)
            return pltpu.make_async_remote_copy(
                src_ref=ins[a].at[_slot(peer)], dst_ref=outs[a].at[_slot(peer)],
                send_sem=send_sems.at[a * 7 + k - 1], recv_sem=recv_sems.at[a * 7 + k - 1],
                device_id=peer, device_id_type=MESH)

        mine = [pltpu.make_async_copy(ins[a].at[_slot(me)], outs[a].at[_slot(me)], local_sems.at[a])
                for a in range(n)]
        for cp in mine:
            cp.start()
        sent = [copy(a, k) for k in range(1, N_DEV) for a in range(n)]
        for cp in sent:
            cp.start()
        for k in range(1, N_DEV):
            for a in range(n):
                landed(a, k).wait_recv()
        for cp in sent:
            cp.wait_send()
        for cp in mine:
            cp.wait()

    return pl.pallas_call(
        body, name=name,
        in_specs=[ANY] * n, out_specs=[ANY] * n,
        out_shape=[jax.ShapeDtypeStruct(a.shape, a.dtype) for a in arrs],
        scratch_shapes=[pltpu.SemaphoreType.DMA((7 * n,)), pltpu.SemaphoreType.DMA((7 * n,)),
                        pltpu.SemaphoreType.DMA((n,))],
    )(*arrs)


def _sum_slots(parts):
    def body(p_ref, o_ref):
        acc = p_ref[0]
        for dev in range(1, N_DEV):
            acc = acc + p_ref[dev]
        o_ref[...] = acc

    return pl.pallas_call(body, name="sum_slots",
                          out_shape=jax.ShapeDtypeStruct(parts.shape[1:], F32))(parts)


def _pack(arrs):
    flat = jnp.concatenate([a.reshape(-1) for a in arrs])
    pad = (-flat.shape[0]) % (8 * LANE)
    return jnp.pad(flat, (0, pad)).reshape(-1, LANE)


def _unpack(packed, shapes):
    flat = packed.reshape(-1)
    out, at = [], 0
    for shp in shapes:
        size = 1
        for dim in shp:
            size *= dim
        out.append(flat[at:at + size].reshape(shp))
        at += size
    return out


def _layer_fwd(x, l, wt, wo, g_pre, g_post, wa_pad, b_alpha, g_gla, g_att, rb_pad):
    h = _rms_fwd(x, g_pre)
    z = _matmul(h, wt, "nt", F32, 512, D_ZP // 3, 512, "in_proj")
    y_gla, o_gla, states = _gla_fwd(z, wa_pad, b_alpha, g_gla)
    y_att, o_att = _att_fwd(z, rb_pad, g_att)
    ycat = jnp.concatenate([y_gla, y_att], axis=1)
    y = _matmul(ycat, wo, "nn", F32, 512, 1024, 512, "out_proj", b_layer=l)
    out = _post_fwd(x, y, g_post)
    return out, (x, h, z, o_gla, states, o_att, ycat, y)


def _layer_bwd(dout, saved, l, wt, wo, g_pre, g_post, wa_pad, b_alpha, g_gla, g_att, rb_pad):
    x, h, z, o_gla, states, o_att, ycat, y = saved
    dy, dg_post = _post_bwd(dout, y, g_post)
    dycat = _matmul(dy, wo, "nt", F32, 512, 1024, 512, "out_proj_dx", b_layer=l)
    dwo = _matmul(ycat, dy, "tn", BF16, 512, 1024, 512, "out_proj_dw")
    dq, dk, dv, dgg, dga, dwa, db, dg_gla = _gla_bwd(dycat, o_gla, z, wa_pad, b_alpha, g_gla, states)
    daq, dak, dav, dag, drb, dg_att = _att_bwd(dycat, o_att, z, rb_pad, g_att)
    dz = jnp.concatenate([dq, dk, dv, dgg, daq, dak.astype(BF16), dav.astype(BF16), dag, dga], axis=1)
    dh = _matmul(dz, wt, "nn", F32, 512, 1024, D_ZP // 3, "in_proj_dx")
    dwt = _matmul(dz, h, "tn", BF16, D_ZP // 3, 512, 512, "in_proj_dw")
    dx, dg_pre = _pre_bwd(dh, x, g_pre, dout)
    small = (dg_pre[0], dg_post[0], dwa[:GLA_RANK], db[0], dg_gla[0], dg_att[0], drb[:, 0, :N_REL])
    return dx, dwt, dwo, small


def kernel(x, w_in, w_out, g_pre, g_post, w_alpha, b_alpha, g_gla, g_att, rel_bias, loss_target, m_w_in, m_w_out, m_g_pre, m_g_post, m_w_alpha, m_b_alpha, m_g_gla, m_g_att, m_rel_bias, v_w_in, v_w_out, v_g_pre, v_g_post, v_w_alpha, v_b_alpha, v_g_gla, v_g_att, v_rel_bias):
    nl, d, cols = w_in.shape
    rows = w_out.shape[1]
    s = x.shape[1]
    x0 = x.reshape(s, d)
    tgt = loss_target.reshape(s, d)
    swap = lambda a: jnp.transpose(a, (0, 2, 1))

    w_t = swap(w_in)
    wt_g, wo = _gather_weights(w_t.astype(BF16), w_out.astype(BF16))
    wts = [_to_aligned_rows(wt_g[l].reshape(N_DEV * cols, d)) for l in range(nl)]
    wa_g = _exchange([jnp.broadcast_to(_pack([w_alpha])[None], (N_DEV,) + _pack([w_alpha]).shape)], "gather_alpha")[0]
    wa_cols = w_alpha.shape[2]
    wa_full = wa_g.reshape(N_DEV, -1)[:, :nl * GLA_RANK * wa_cols].reshape(N_DEV, nl, GLA_RANK, wa_cols)
    wa_full = jnp.transpose(wa_full, (1, 2, 0, 3)).reshape(nl, GLA_RANK, GLA_KW)
    wa_pad = jnp.pad(wa_full, ((0, 0), (0, LANE - GLA_RANK), (0, 0)))
    rb_pad = jnp.pad(rel_bias, ((0, 0), (0, 0), (0, 3 * LANE - N_REL)))

    def layer_args(l):
        return (l, wts[l], wo, g_pre[l:l + 1], g_post[l:l + 1], wa_pad[l], b_alpha[l:l + 1],
                g_gla[l:l + 1], g_att[l:l + 1], rb_pad[l])

    act, saved = x0, []
    for l in range(nl):
        act, sv = _layer_fwd(act, *layer_args(l))
        saved.append(sv)
    dout, sq = _loss_head(act, tgt)
    loss = lax.psum(sq[0, 0] * (0.5 / d), ("x", "y", "c"))

    dwts, dwos, smalls = [None] * nl, [None] * nl, [None] * nl
    for l in reversed(range(nl)):
        dout, dwts[l], dwos[l], smalls[l] = _layer_bwd(dout, saved[l], *layer_args(l))
    grad_x = dout.reshape(x.shape)

    pin, pout = _scatter_grads([_from_aligned_rows(dw).reshape(N_DEV, cols, d) for dw in dwts], dwos, rows)
    g_w_in, d_w_in, m2_w_in, v2_w_in = [
        swap(a) for a in _adam_sharded(pin, w_t, swap(m_w_in), swap(v_w_in), 128, "adam_w_in")]
    g_w_out, d_w_out, m2_w_out, v2_w_out = _adam_sharded(pout, w_out, m_w_out, v_w_out, 128, "adam_w_out")

    names = 7
    small_stacked = [jnp.stack([smalls[l][i] for l in range(nl)]) for i in range(names)]
    shapes = [a.shape for a in small_stacked]
    packed = _pack(small_stacked)
    gathered = _exchange([jnp.broadcast_to(packed[None], (N_DEV,) + packed.shape)], "gather_small_grads")[0]
    g_pre_g, g_post_g, wa_g_full, b_g, gla_g, att_g, rb_g = _unpack(_sum_slots(gathered), shapes)
    my = _slot(_position())
    wa_g_mine = lax.dynamic_slice_in_dim(wa_g_full, my * wa_cols, wa_cols, axis=2)
    grads = [g_pre_g, g_post_g, wa_g_mine, b_g, gla_g, att_g, rb_g]
    ws = [g_pre, g_post, w_alpha, b_alpha, g_gla, g_att, rel_bias]
    ms = [m_g_pre, m_g_post, m_w_alpha, m_b_alpha, m_g_gla, m_g_att, m_rel_bias]
    vs = [v_g_pre, v_g_post, v_w_alpha, v_b_alpha, v_g_gla, v_g_att, v_rel_bias]
    shapes2 = [a.shape for a in ws]
    d_s, m2_s, v2_s = _adam_small(_pack(ws), _pack(grads), _pack(ms), _pack(vs))
    d_s, m2_s, v2_s = _unpack(d_s, shapes2), _unpack(m2_s, shapes2), _unpack(v2_s, shapes2)

    def ordered(big_in, big_out, small):
        return [big_in, big_out] + list(small)

    return (loss, grad_x,
            *ordered(g_w_in, g_w_out, grads),
            *ordered(d_w_in, d_w_out, d_s),
            *ordered(m2_w_in, m2_w_out, m2_s),
            *ordered(v2_w_in, v2_w_out, v2_s))
```

```python
import functools

import jax
import jax.numpy as jnp
from jax import lax
from jax.experimental import pallas as pl
from jax.experimental.pallas import tpu as pltpu

F32 = jnp.float32
BF16 = jnp.bfloat16
MESH = pl.DeviceIdType.MESH
ANY = pl.BlockSpec(memory_space=pl.ANY)

CHUNK = 64
GLA_HEADS = 4
GLA_DK = 128
GLA_DV = 256
GLA_KW = GLA_HEADS * GLA_DK
D_GLA = GLA_HEADS * GLA_DV
GLA_RANK = 16
GLA_TAU = 16.0
ATT_HEADS = 8
ATT_HD = 128
D_ATT = ATT_HEADS * ATT_HD
LEFT_CHUNKS = 8
REL_CLIP = 128
N_REL = 2 * REL_CLIP + 1
EPS = 1e-6
D_IN = 2 * GLA_KW + 2 * D_GLA + GLA_RANK + 4 * D_ATT
GLA_SCALE = GLA_DK ** -0.5
ATT_SCALE = ATT_HD ** -0.5

ADAM_LR = 0.001
ADAM_B1 = 0.9
ADAM_B2 = 0.999
ADAM_EPS = 1e-08
ADAM_WD = 0.01
ADAM_STEP = 10

N_DEV = 8
LANE = 128
GA_ORIG = 2 * GLA_KW + 2 * D_GLA
OFF_AQ = GA_ORIG
OFF_GA = GA_ORIG + 4 * D_ATT
D_ZP = OFF_GA + LANE
QB = 2 * CHUNK
WIN = (LEFT_CHUNKS + 2) * CHUNK
ET_ROWS = WIN + LEFT_CHUNKS * CHUNK
NEG = -1e30
VMEM_LIMIT = 48 * 1024 * 1024


def _cparams(sem):
    return pltpu.CompilerParams(dimension_semantics=sem, vmem_limit_bytes=VMEM_LIMIT)


def _dot(a, b):
    return jnp.dot(a, b, preferred_element_type=F32)


def _dot_nt(a, b):
    return lax.dot_general(a, b, (((1,), (1,)), ((), ())), preferred_element_type=F32)


def _dot_tn(a, b):
    return lax.dot_general(a, b, (((0,), (0,)), ((), ())), preferred_element_type=F32)


def _dot01(t, x, left=True):
    if not left:
        t, x = x, t
    hi = x.astype(BF16)
    r = x - hi.astype(F32)
    mid = r.astype(BF16)
    lo = (r - mid.astype(F32)).astype(BF16)
    if left:
        return _dot(t, hi) + _dot(t, mid) + _dot(t, lo)
    return _dot(hi, t) + _dot(mid, t) + _dot(lo, t)


def _sigmoid(x):
    return 1.0 / (1.0 + jnp.exp(-x))


def _log_sigmoid(x):
    return jnp.minimum(x, 0.0) - jnp.log(1.0 + jnp.exp(-jnp.abs(x)))


def _matmul(a, b, mode, out_dtype, tm, tn, tk, name, b_layer=None):
    bshape = b.shape if b_layer is None else b.shape[1:]
    if mode == "nn":
        (m, k), n = a.shape, bshape[1]
    elif mode == "nt":
        (m, k), n = a.shape, bshape[0]
    else:
        (k, m), n = a.shape, bshape[1]
    tm, tn, tk = min(tm, m), min(tn, n), min(tk, k)
    assert m % tm == 0 and n % tn == 0 and k % tk == 0, (name, m, n, k)
    nk = k // tk

    def body(a_ref, b_ref, o_ref, acc_ref):
        kk = pl.program_id(2)

        @pl.when(kk == 0)
        def _():
            acc_ref[...] = jnp.zeros_like(acc_ref)

        if mode == "nn":
            acc_ref[...] += _dot(a_ref[...], b_ref[...])
        elif mode == "nt":
            acc_ref[...] += _dot_nt(a_ref[...], b_ref[...])
        else:
            acc_ref[...] += _dot_tn(a_ref[...], b_ref[...])

        @pl.when(kk == nk - 1)
        def _():
            o_ref[...] = acc_ref[...].astype(out_dtype)

    if mode == "tn":
        a_spec = pl.BlockSpec((tk, tm), lambda i, j, kk: (kk, i))
    else:
        a_spec = pl.BlockSpec((tm, tk), lambda i, j, kk: (i, kk))
    b_blk, b_idx = ((tn, tk), lambda i, j, kk: (j, kk)) if mode == "nt" else ((tk, tn), lambda i, j, kk: (kk, j))
    if b_layer is None:
        b_spec = pl.BlockSpec(b_blk, b_idx)
    else:
        b_spec = pl.BlockSpec((None,) + b_blk, lambda i, j, kk: (b_layer,) + b_idx(i, j, kk))
    return pl.pallas_call(
        body, name=name,
        grid=(m // tm, n // tn, nk),
        in_specs=[a_spec, b_spec],
        out_specs=pl.BlockSpec((tm, tn), lambda i, j, kk: (i, j)),
        out_shape=jax.ShapeDtypeStruct((m, n), out_dtype),
        scratch_shapes=[pltpu.VMEM((tm, tn), F32)],
        compiler_params=_cparams(("parallel", "parallel", "arbitrary")),
    )(a, b)


ROWS = 256


def _rms_fwd(x, g):
    s, d = x.shape

    def body(x_ref, g_ref, h_ref):
        xv = x_ref[...]
        r = lax.rsqrt(jnp.mean(xv * xv, axis=-1, keepdims=True) + EPS)
        h_ref[...] = (xv * r * g_ref[...]).astype(BF16)

    return pl.pallas_call(
        body, name="rms_fwd", grid=(s // ROWS,),
        in_specs=[pl.BlockSpec((ROWS, d), lambda i: (i, 0)), pl.BlockSpec((1, d), lambda i: (0, 0))],
        out_specs=pl.BlockSpec((ROWS, d), lambda i: (i, 0)),
        out_shape=jax.ShapeDtypeStruct((s, d), BF16),
        compiler_params=_cparams(("parallel",)),
    )(x, g)


def _post_fwd(x, y, g):
    s, d = x.shape

    def body(x_ref, y_ref, g_ref, o_ref):
        yv = y_ref[...]
        r = lax.rsqrt(jnp.mean(yv * yv, axis=-1, keepdims=True) + EPS)
        o_ref[...] = x_ref[...] + yv * r * g_ref[...]

    row = pl.BlockSpec((ROWS, d), lambda i: (i, 0))
    return pl.pallas_call(
        body, name="post_fwd", grid=(s // ROWS,),
        in_specs=[row, row, pl.BlockSpec((1, d), lambda i: (0, 0))],
        out_specs=row,
        out_shape=jax.ShapeDtypeStruct((s, d), F32),
        compiler_params=_cparams(("parallel",)),
    )(x, y, g)


def _loss_head(out, tgt):
    s, d = out.shape

    def body(o_ref, t_ref, dout_ref, sum_ref):
        @pl.when(pl.program_id(0) == 0)
        def _():
            sum_ref[...] = jnp.zeros_like(sum_ref)

        e = o_ref[...] - t_ref[...]
        dout_ref[...] = e * (1.0 / d)
        sum_ref[...] += jnp.sum(jnp.sum(e * e, axis=1, keepdims=True), axis=0, keepdims=True)

    row = pl.BlockSpec((ROWS, d), lambda i: (i, 0))
    return pl.pallas_call(
        body, name="loss_head", grid=(s // ROWS,),
        in_specs=[row, row],
        out_specs=[row, pl.BlockSpec((1, 1), lambda i: (0, 0))],
        out_shape=[jax.ShapeDtypeStruct((s, d), F32), jax.ShapeDtypeStruct((1, 1), F32)],
        compiler_params=_cparams(("arbitrary",)),
    )(out, tgt)


def _post_bwd(dout, y, g):
    s, d = y.shape

    def body(do_ref, y_ref, g_ref, dy_ref, dg_ref):
        @pl.when(pl.program_id(0) == 0)
        def _():
            dg_ref[...] = jnp.zeros_like(dg_ref)

        yv = y_ref[...]
        dv = do_ref[...]
        r = lax.rsqrt(jnp.mean(yv * yv, axis=-1, keepdims=True) + EPS)
        dg_ref[...] += jnp.sum(dv * yv * r, axis=0, keepdims=True)
        w = dv * g_ref[...]
        dy = r * (w - yv * (r * r) * jnp.mean(w * yv, axis=-1, keepdims=True))
        dy_ref[...] = dy.astype(BF16)

    row = pl.BlockSpec((ROWS, d), lambda i: (i, 0))
    vec = pl.BlockSpec((1, d), lambda i: (0, 0))
    return pl.pallas_call(
        body, name="post_bwd", grid=(s // ROWS,),
        in_specs=[row, row, vec],
        out_specs=[row, vec],
        out_shape=[jax.ShapeDtypeStruct((s, d), BF16), jax.ShapeDtypeStruct((1, d), F32)],
        compiler_params=_cparams(("arbitrary",)),
    )(dout, y, g)


def _pre_bwd(dh, x, g, dout):
    s, d = x.shape

    def body(dh_ref, x_ref, g_ref, do_ref, dx_ref, dg_ref):
        @pl.when(pl.program_id(0) == 0)
        def _():
            dg_ref[...] = jnp.zeros_like(dg_ref)

        xv = x_ref[...]
        dv = dh_ref[...]
        r = lax.rsqrt(jnp.mean(xv * xv, axis=-1, keepdims=True) + EPS)
        dg_ref[...] += jnp.sum(dv * xv * r, axis=0, keepdims=True)
        w = dv * g_ref[...]
        dx_ref[...] = do_ref[...] + r * (w - xv * (r * r) * jnp.mean(w * xv, axis=-1, keepdims=True))

    row = pl.BlockSpec((ROWS, d), lambda i: (i, 0))
    vec = pl.BlockSpec((1, d), lambda i: (0, 0))
    return pl.pallas_call(
        body, name="pre_bwd", grid=(s // ROWS,),
        in_specs=[row, row, vec, row],
        out_specs=[row, vec],
        out_shape=[jax.ShapeDtypeStruct((s, d), F32), jax.ShapeDtypeStruct((1, d), F32)],
        compiler_params=_cparams(("arbitrary",)),
    )(dh, x, g, dout)


def _gla_gate(ga_b, wa_ref, b_ref, cs, tri):
    pre = _dot(ga_b, wa_ref[:, cs].astype(BF16)) + b_ref[:, cs]
    la = _log_sigmoid(pre) * (1.0 / GLA_TAU)
    cum = _dot01(tri, la)
    last = lax.broadcasted_iota(jnp.int32, cum.shape, 0) == CHUNK - 1
    return pre, cum, jnp.sum(jnp.where(last, cum, 0.0), axis=0, keepdims=True)


def _z_specs_gla(rev=None):
    idx = (lambda n: n) if rev is None else rev
    return [
        pl.BlockSpec((CHUNK, GLA_KW), lambda n: (idx(n), 0)),
        pl.BlockSpec((CHUNK, GLA_KW), lambda n: (idx(n), 1)),
        pl.BlockSpec((CHUNK, D_GLA), lambda n: (idx(n), 1)),
        pl.BlockSpec((CHUNK, D_GLA), lambda n: (idx(n), 2)),
        pl.BlockSpec((CHUNK, LANE), lambda n: (idx(n), OFF_GA // LANE)),
    ]


def _gla_fwd(z, wa_pad, b_alpha, g_gla):
    s = z.shape[0]
    nchunk = s // CHUNK

    def body(q_ref, k_ref, v_ref, gg_ref, ga_ref, wa_ref, b_ref, g_ref, y_ref, o_ref, st_ref, state):
        @pl.when(pl.program_id(0) == 0)
        def _():
            state[...] = jnp.zeros_like(state)

        ga_b = ga_ref[...].astype(BF16)
        ri = lax.broadcasted_iota(jnp.int32, (CHUNK, CHUNK), 0)
        ci = lax.broadcasted_iota(jnp.int32, (CHUNK, CHUNK), 1)
        tri = jnp.where(ri >= ci, 1.0, 0.0).astype(BF16)
        for h in range(GLA_HEADS):
            cs = slice(h * GLA_DK, (h + 1) * GLA_DK)
            vs = slice(h * GLA_DV, (h + 1) * GLA_DV)
            _, cum, cend = _gla_gate(ga_b, wa_ref, b_ref, cs, tri)
            kd = k_ref[:, cs] * jnp.exp(cend - cum)
            st = state[h] * jnp.exp(cend) + _dot_tn(v_ref[:, vs].astype(BF16), kd.astype(BF16))
            state[h] = st
            st_ref[0, h] = st
            qs = (q_ref[:, cs] * GLA_SCALE).astype(BF16)
            o = _dot_nt(qs, st.astype(BF16))
            o_ref[:, vs] = o
            r = lax.rsqrt(jnp.mean(o * o, axis=-1, keepdims=True) + EPS)
            gg = gg_ref[:, vs]
            y_ref[:, vs] = (o * r * g_ref[:, vs] * (gg * _sigmoid(gg))).astype(BF16)

    full = lambda shape: pl.BlockSpec(shape, lambda n: tuple(0 for _ in shape))
    wide = pl.BlockSpec((CHUNK, D_GLA), lambda n: (n, 0))
    return pl.pallas_call(
        body, name="gla_fwd", grid=(nchunk,),
        in_specs=_z_specs_gla() + [full((LANE, GLA_KW)), full((1, GLA_KW)), full((1, D_GLA))],
        out_specs=[wide, wide, pl.BlockSpec((1, GLA_HEADS, GLA_DV, GLA_DK), lambda n: (n, 0, 0, 0))],
        out_shape=[jax.ShapeDtypeStruct((s, D_GLA), BF16), jax.ShapeDtypeStruct((s, D_GLA), F32),
                   jax.ShapeDtypeStruct((nchunk, GLA_HEADS, GLA_DV, GLA_DK), F32)],
        scratch_shapes=[pltpu.VMEM((GLA_HEADS, GLA_DV, GLA_DK), F32)],
        compiler_params=_cparams(("arbitrary",)),
    )(z, z, z, z, z, wa_pad, b_alpha, g_gla)


def _gla_bwd(dyc, o_gla, z, wa_pad, b_alpha, g_gla, states):
    s = z.shape[0]
    nchunk = s // CHUNK
    rev = lambda n: nchunk - 1 - n

    def body(dy_ref, o_ref, q_ref, k_ref, v_ref, gg_ref, ga_ref, wa_ref, b_ref, g_ref, st_ref, stp_ref,
             dq_ref, dk_ref, dv_ref, dgg_ref, dga_ref, dwa_ref, db_ref, dg_ref, carry):
        step = pl.program_id(0)

        @pl.when(step == 0)
        def _():
            carry[...] = jnp.zeros_like(carry)
            dwa_ref[...] = jnp.zeros_like(dwa_ref)
            db_ref[...] = jnp.zeros_like(db_ref)
            dg_ref[...] = jnp.zeros_like(dg_ref)

        has_prev = (step < nchunk - 1).astype(F32)
        ga_b = ga_ref[...].astype(BF16)
        ri = lax.broadcasted_iota(jnp.int32, (CHUNK, CHUNK), 0)
        ci = lax.broadcasted_iota(jnp.int32, (CHUNK, CHUNK), 1)
        tri = jnp.where(ri >= ci, 1.0, 0.0).astype(BF16)
        tri_up = jnp.where(ci >= ri, 1.0, 0.0).astype(BF16)
        dga = jnp.zeros((CHUNK, LANE), F32)
        for h in range(GLA_HEADS):
            cs = slice(h * GLA_DK, (h + 1) * GLA_DK)
            vs = slice(h * GLA_DV, (h + 1) * GLA_DV)
            pre, cum, cend = _gla_gate(ga_b, wa_ref, b_ref, cs, tri)
            e = jnp.exp(cend - cum)
            a = jnp.exp(cend)
            kf = k_ref[:, cs]
            kd_b = (kf * e).astype(BF16)
            v_b = v_ref[:, vs].astype(BF16)
            qs = (q_ref[:, cs] * GLA_SCALE).astype(BF16)
            o = o_ref[:, vs]
            gg = gg_ref[:, vs]
            g = g_ref[:, vs]
            dy = dy_ref[:, vs]
            r = lax.rsqrt(jnp.mean(o * o, axis=-1, keepdims=True) + EPS)
            sg = _sigmoid(gg)
            dogn = dy * (gg * sg)
            dgg_ref[:, vs] = (dy * (o * r * g) * (sg * (1.0 + gg * (1.0 - sg)))).astype(BF16)
            dg_ref[:, vs] += jnp.sum(dogn * o * r, axis=0, keepdims=True)
            w = dogn * g
            do_b = (r * (w - o * (r * r) * jnp.mean(w * o, axis=-1, keepdims=True))).astype(BF16)
            st = st_ref[0, h]
            dq_ref[:, cs] = (_dot(do_b, st.astype(BF16)) * GLA_SCALE).astype(BF16)
            gt = _dot_tn(do_b, qs) + carry[h]
            gt_b = gt.astype(BF16)
            dkd = _dot(v_b, gt_b)
            dv_ref[:, vs] = _dot_nt(kd_b, gt_b).astype(BF16)
            da = jnp.sum(gt * (stp_ref[0, h] * has_prev), axis=0, keepdims=True)
            carry[h] = gt * a
            dk_ref[:, cs] = (dkd * e).astype(BF16)
            dd = dkd * kf * e
            dcend = jnp.sum(dd, axis=0, keepdims=True) + da * a
            dla = dcend - _dot01(tri_up, dd)
            dpre = dla * (1.0 / GLA_TAU) * (1.0 - _sigmoid(pre))
            dpre_b = dpre.astype(BF16)
            dga = dga + _dot_nt(dpre_b, wa_ref[:, cs].astype(BF16))
            dwa_ref[:, cs] += _dot_tn(ga_b, dpre_b)
            db_ref[:, cs] += jnp.sum(dpre, axis=0, keepdims=True)
        dga_ref[...] = dga.astype(BF16)

    full = lambda shape: pl.BlockSpec(shape, lambda n: tuple(0 for _ in shape))
    wide = pl.BlockSpec((CHUNK, D_GLA), lambda n: (rev(n), 0))
    keyw = pl.BlockSpec((CHUNK, GLA_KW), lambda n: (rev(n), 0))
    st_spec = pl.BlockSpec((1, GLA_HEADS, GLA_DV, GLA_DK), lambda n: (rev(n), 0, 0, 0))
    stp_spec = pl.BlockSpec((1, GLA_HEADS, GLA_DV, GLA_DK), lambda n: (jnp.maximum(rev(n) - 1, 0), 0, 0, 0))
    return pl.pallas_call(
        body, name="gla_bwd", grid=(nchunk,),
        in_specs=[wide, wide] + _z_specs_gla(rev)
        + [full((LANE, GLA_KW)), full((1, GLA_KW)), full((1, D_GLA)), st_spec, stp_spec],
        out_specs=[keyw, keyw, wide, wide, pl.BlockSpec((CHUNK, LANE), lambda n: (rev(n), 0)),
                   full((LANE, GLA_KW)), full((1, GLA_KW)), full((1, D_GLA))],
        out_shape=[jax.ShapeDtypeStruct((s, GLA_KW), BF16), jax.ShapeDtypeStruct((s, GLA_KW), BF16),
                   jax.ShapeDtypeStruct((s, D_GLA), BF16), jax.ShapeDtypeStruct((s, D_GLA), BF16),
                   jax.ShapeDtypeStruct((s, LANE), BF16),
                   jax.ShapeDtypeStruct((LANE, GLA_KW), F32), jax.ShapeDtypeStruct((1, GLA_KW), F32),
                   jax.ShapeDtypeStruct((1, D_GLA), F32)],
        scratch_shapes=[pltpu.VMEM((GLA_HEADS, GLA_DV, GLA_DK), F32)],
        compiler_params=_cparams(("arbitrary",)),
    )(dyc, o_gla, z, z, z, z, z, wa_pad, b_alpha, g_gla, states, states)


def _build_bias_table(rb_row, et_ref):
    far = jnp.broadcast_to(rb_row[:, 2 * REL_CLIP:2 * REL_CLIP + 1], (1, LANE))
    near_hi = rb_row[:, REL_CLIP:2 * REL_CLIP]
    near_lo = rb_row[:, 0:REL_CLIP]
    past = jnp.broadcast_to(rb_row[:, 0:1], (1, LANE))
    seg = [far, far, far, far, near_hi, near_lo] + [past] * (ET_ROWS // LANE - 5)
    ri = lax.broadcasted_iota(jnp.int32, (LANE, LANE), 0)
    ci = lax.broadcasted_iota(jnp.int32, (LANE, LANE), 1)
    for kb in range(ET_ROWS // LANE):
        wmat = jnp.where(ri + ci < LANE, seg[kb], seg[kb + 1])
        blk = pltpu.roll(wmat, 0, 1, stride=1, stride_axis=0)
        lag = LEFT_CHUNKS + ci // CHUNK - (2 * kb + ri // CHUNK)
        et_ref[kb * LANE:(kb + 1) * LANE, :] = jnp.where((lag >= 0) & (lag <= LEFT_CHUNKS), blk, NEG)


def _reduce_bias_table(det_ref):
    lane = lax.broadcasted_iota(jnp.int32, (1, LANE), 1)
    ri = lax.broadcasted_iota(jnp.int32, (LANE, LANE), 0)
    ci = lax.broadcasted_iota(jnp.int32, (LANE, LANE), 1)
    flip = jnp.where(ri + ci == LANE - 1, 1.0, 0.0).astype(BF16)
    segs = jnp.zeros((8, LANE), F32)
    seg_row = lax.broadcasted_iota(jnp.int32, (8, LANE), 0)
    prev_minus = jnp.zeros((1, LANE), F32)
    for kb in range(6):
        rolled = pltpu.roll(_dot01(det_ref[kb * LANE:(kb + 1) * LANE, :], flip, left=False), 0, 1,
                            stride=1, stride_axis=0)
        plus = jnp.sum(jnp.where(ci >= ri, rolled, 0.0), axis=0, keepdims=True)
        minus = jnp.sum(jnp.where(ci < ri, rolled, 0.0), axis=0, keepdims=True)
        segs = segs + jnp.where(seg_row == kb, plus + prev_minus, 0.0)
        prev_minus = minus
    segs = _dot01(segs, flip, left=False)
    pick = lambda kb: jnp.sum(jnp.where(seg_row == kb, segs, 0.0), axis=0, keepdims=True)
    far = jnp.sum(pick(0) + pick(1) + pick(2) + pick(3), axis=1, keepdims=True)
    last = jnp.where(lane == 0, far, 0.0)
    return jnp.concatenate([pick(5), pick(4), last], axis=1)


def _att_window(b):
    c0 = 2 * b
    kstart = pl.multiple_of(jnp.maximum(c0 - LEFT_CHUNKS, 0) * CHUNK, CHUNK)
    eoff = pl.multiple_of(jnp.maximum(LEFT_CHUNKS - c0, 0) * CHUNK, CHUNK)
    return kstart, eoff


def _att_probs(q_b, kw_b, et):
    st = _dot_nt(kw_b, q_b) * ATT_SCALE + et
    m = jnp.max(st, axis=0, keepdims=True)
    ex = jnp.exp(st - m)
    return ex / jnp.sum(ex, axis=0, keepdims=True)


def _att_fwd(z, rb_pad, g_att):
    s = z.shape[0]
    nblk = s // QB
    c_aq, c_ak, c_av, c_ag = [(OFF_AQ + i * D_ATT) // ATT_HD for i in range(4)]

    def body(q_ref, k_ref, v_ref, ag_ref, rb_ref, g_ref, y_ref, o_ref, et_ref):
        h = pl.program_id(0)
        b = pl.program_id(1)

        @pl.when(b == 0)
        def _():
            _build_bias_table(rb_ref[pl.ds(h, 1), :], et_ref)

        kstart, eoff = _att_window(b)
        q_b = q_ref[...].astype(BF16)
        kw_b = k_ref[pl.ds(kstart, WIN), :].astype(BF16)
        vw_b = v_ref[pl.ds(kstart, WIN), :].astype(BF16)
        pt = _att_probs(q_b, kw_b, et_ref[pl.ds(eoff, WIN), :])
        o = _dot_tn(pt.astype(BF16), vw_b)
        o_ref[...] = o
        r = lax.rsqrt(jnp.mean(o * o, axis=-1, keepdims=True) + EPS)
        ag = ag_ref[...]
        y_ref[...] = (o * r * g_ref[...] * (ag * _sigmoid(ag))).astype(BF16)

    blk = lambda col: pl.BlockSpec((QB, ATT_HD), lambda h, b: (b, col + h))
    seq = lambda col: pl.BlockSpec((s, ATT_HD), lambda h, b: (0, col + h))
    out_blk = pl.BlockSpec((QB, ATT_HD), lambda h, b: (b, h))
    return pl.pallas_call(
        body, name="att_fwd", grid=(ATT_HEADS, nblk),
        in_specs=[blk(c_aq), seq(c_ak), seq(c_av), blk(c_ag),
                  pl.BlockSpec((ATT_HEADS, 3 * LANE), lambda h, b: (0, 0)),
                  pl.BlockSpec((1, ATT_HD), lambda h, b: (0, h))],
        out_specs=[out_blk, out_blk],
        out_shape=[jax.ShapeDtypeStruct((s, D_ATT), BF16), jax.ShapeDtypeStruct((s, D_ATT), F32)],
        scratch_shapes=[pltpu.VMEM((ET_ROWS, LANE), F32)],
        compiler_params=_cparams(("arbitrary", "arbitrary")),
    )(z, z, z, z, rb_pad, g_att)


def _att_bwd(dyc, o_att, z, rb_pad, g_att):
    s = z.shape[0]
    nblk = s // QB
    c_aq, c_ak, c_av, c_ag = [(OFF_AQ + i * D_ATT) // ATT_HD for i in range(4)]
    c_dy = D_GLA // ATT_HD

    def body(dy_ref, o_ref, q_ref, k_ref, v_ref, ag_ref, rb_ref, g_ref,
             dq_ref, dk_ref, dv_ref, dag_ref, drb_ref, dg_ref, et_ref, det_ref):
        h = pl.program_id(0)
        b = pl.program_id(1)

        @pl.when(b == 0)
        def _():
            _build_bias_table(rb_ref[pl.ds(h, 1), :], et_ref)
            det_ref[...] = jnp.zeros_like(det_ref)
            dk_ref[...] = jnp.zeros_like(dk_ref)
            dv_ref[...] = jnp.zeros_like(dv_ref)
            dg_ref[...] = jnp.zeros_like(dg_ref)

        kstart, eoff = _att_window(b)
        q_b = q_ref[...].astype(BF16)
        kw_b = k_ref[pl.ds(kstart, WIN), :].astype(BF16)
        vw_b = v_ref[pl.ds(kstart, WIN), :].astype(BF16)
        pt = _att_probs(q_b, kw_b, et_ref[pl.ds(eoff, WIN), :])
        o = o_ref[...]
        ag = ag_ref[...]
        g = g_ref[...]
        dy = dy_ref[...]
        r = lax.rsqrt(jnp.mean(o * o, axis=-1, keepdims=True) + EPS)
        sg = _sigmoid(ag)
        don = dy * (ag * sg)
        dag_ref[...] = (dy * (o * r * g) * (sg * (1.0 + ag * (1.0 - sg)))).astype(BF16)
        dg_ref[...] += jnp.sum(don * o * r, axis=0, keepdims=True)
        w = don * g
        do_b = (r * (w - o * (r * r) * jnp.mean(w * o, axis=-1, keepdims=True))).astype(BF16)
        pt_b = pt.astype(BF16)
        dpt = _dot_nt(vw_b, do_b)
        dst = pt * (dpt - jnp.sum(dpt * pt, axis=0, keepdims=True))
        det_ref[pl.ds(eoff, WIN), :] += dst
        ds_b = (dst * ATT_SCALE).astype(BF16)
        dq_ref[...] = _dot_tn(ds_b, kw_b).astype(BF16)
        dk_ref[pl.ds(kstart, WIN), :] += _dot(ds_b, q_b)
        dv_ref[pl.ds(kstart, WIN), :] += _dot(pt_b, do_b)

        @pl.when(b == nblk - 1)
        def _():
            drb_ref[0] = jnp.broadcast_to(_reduce_bias_table(det_ref), (8, 3 * LANE))

    blk = lambda col: pl.BlockSpec((QB, ATT_HD), lambda h, b: (b, col + h))
    seq = lambda col: pl.BlockSpec((s, ATT_HD), lambda h, b: (0, col + h))
    out_blk = pl.BlockSpec((QB, ATT_HD), lambda h, b: (b, h))
    out_seq = pl.BlockSpec((s, ATT_HD), lambda h, b: (0, h))
    return pl.pallas_call(
        body, name="att_bwd", grid=(ATT_HEADS, nblk),
        in_specs=[blk(c_dy), blk(0), blk(c_aq), seq(c_ak), seq(c_av), blk(c_ag),
                  pl.BlockSpec((ATT_HEADS, 3 * LANE), lambda h, b: (0, 0)),
                  pl.BlockSpec((1, ATT_HD), lambda h, b: (0, h))],
        out_specs=[out_blk, out_seq, out_seq, out_blk,
                   pl.BlockSpec((1, 8, 3 * LANE), lambda h, b: (h, 0, 0)),
                   pl.BlockSpec((1, ATT_HD), lambda h, b: (0, h))],
        out_shape=[jax.ShapeDtypeStruct((s, D_ATT), BF16), jax.ShapeDtypeStruct((s, D_ATT), F32),
                   jax.ShapeDtypeStruct((s, D_ATT), F32), jax.ShapeDtypeStruct((s, D_ATT), BF16),
                   jax.ShapeDtypeStruct((ATT_HEADS, 8, 3 * LANE), F32),
                   jax.ShapeDtypeStruct((1, D_ATT), F32)],
        scratch_shapes=[pltpu.VMEM((ET_ROWS, LANE), F32), pltpu.VMEM((ET_ROWS, LANE), F32)],
        compiler_params=_cparams(("arbitrary", "arbitrary")),
    )(dyc, o_att, z, z, z, z, rb_pad, g_att)


ADAM_ROWS = 64


def _adam_math(w, g, m, v):
    m2 = ADAM_B1 * m + (1.0 - ADAM_B1) * g
    v2 = ADAM_B2 * v + (1.0 - ADAM_B2) * (g * g)
    m_hat = m2 / (1.0 - ADAM_B1 ** ADAM_STEP)
    v_hat = v2 / (1.0 - ADAM_B2 ** ADAM_STEP)
    delta = -ADAM_LR * (m_hat / (jnp.sqrt(v_hat) + ADAM_EPS) + ADAM_WD * w)
    return delta, m2, v2


def _adam_sharded(parts, first, w, m, v, name):
    nl, nr, nc = w.shape

    def body(*refs):
        p_refs = refs[:nl]
        w_ref, m_ref, v_ref, g_ref, d_ref, m2_ref, v2_ref = refs[nl:]
        for k in range(nl):
            @pl.when(pl.program_id(0) == k)
            def _(p_ref=p_refs[k]):
                g = p_ref[0].astype(F32)
                for dev in range(1, N_DEV):
                    g = g + p_ref[dev].astype(F32)
                delta, m2, v2 = _adam_math(w_ref[0], g, m_ref[0], v_ref[0])
                g_ref[0] = g
                d_ref[0] = delta
                m2_ref[0] = m2
                v2_ref[0] = v2

    def part_spec(k):
        return pl.BlockSpec((N_DEV, ADAM_ROWS, nc), lambda l, i: (0, first + jnp.where(l == k, i, 0), 0))

    blk = pl.BlockSpec((1, ADAM_ROWS, nc), lambda l, i: (l, i, 0))
    shp = jax.ShapeDtypeStruct(w.shape, F32)
    return pl.pallas_call(
        body, name=name, grid=(nl, pl.cdiv(nr, ADAM_ROWS)),
        in_specs=[part_spec(k) for k in range(nl)] + [blk, blk, blk],
        out_specs=[blk, blk, blk, blk],
        out_shape=[shp, shp, shp, shp],
        compiler_params=_cparams(("arbitrary", "arbitrary")),
    )(*parts, w, m, v)


def _adam_small(w, g, m, v):
    def body(w_ref, g_ref, m_ref, v_ref, d_ref, m2_ref, v2_ref):
        delta, m2, v2 = _adam_math(w_ref[...], g_ref[...], m_ref[...], v_ref[...])
        d_ref[...] = delta
        m2_ref[...] = m2
        v2_ref[...] = v2

    shp = jax.ShapeDtypeStruct(w.shape, F32)
    return pl.pallas_call(body, name="adam_small", out_shape=[shp, shp, shp])(w, g, m, v)


def _position():
    return lax.axis_index("x"), lax.axis_index("y"), lax.axis_index("c")


def _slot(p):
    return 4 * p[0] + 2 * p[1] + p[2]


def _to_aligned_rows(w):
    zeros = jnp.zeros((D_ZP - D_IN,) + w.shape[1:], w.dtype)
    return jnp.concatenate([w[:GA_ORIG], w[GA_ORIG + GLA_RANK:], w[GA_ORIG:GA_ORIG + GLA_RANK], zeros], axis=0)


def _from_aligned_rows(w):
    return jnp.concatenate([w[:GA_ORIG], w[OFF_GA:OFF_GA + GLA_RANK], w[GA_ORIG:OFF_GA]], axis=0)


def _peer(pos, k):
    x, y, c = pos
    return (1 - x if k & 4 else x, 1 - y if k & 2 else y, 1 - c if k & 1 else c)


HBM_SPEC = pl.BlockSpec(memory_space=pltpu.HBM)
SEM_SPEC = pl.BlockSpec(memory_space=pltpu.SEMAPHORE)
GATHER_PEERS = (1, 4, 2, 6)
ALL_PEERS = (1, 2, 3, 4, 5, 6, 7)


def _hbm(a):
    return pltpu.with_memory_space_constraint(a, pltpu.HBM)


def _split_copies(src_ref, land_ref, send_sems, recv_sems, ks, per_peer, landed):
    me = _position()
    out = []
    for i, k in enumerate(ks):
        peer = _peer(me, k)
        src = src_ref.at[_slot(peer)] if per_peer else src_ref
        dst = land_ref.at[_slot(peer) if landed else _slot(me)]
        out.append(pltpu.make_async_remote_copy(
            src_ref=src, dst_ref=dst, send_sem=send_sems.at[i], recv_sem=recv_sems.at[i],
            device_id=peer, device_id_type=MESH))
    return out


def _exchange_start(src, after, ks, per_peer, name):
    slab = src.shape[1:] if per_peer else src.shape
    land_shape = (N_DEV,) + tuple(slab)
    n = len(ks)

    def body(src_ref, land_ref, after_ref, send_sems, recv_sems, src_thru, land_thru, token):
        for cp in _split_copies(src_ref, land_ref, send_sems, recv_sems, ks, per_peer, landed=False):
            cp.start()
        token[...] = jnp.zeros_like(token)

    return pl.pallas_call(
        body, name=name,
        out_shape=(pltpu.SemaphoreType.DMA((n,)), pltpu.SemaphoreType.DMA((n,)),
                   pltpu.HBM(src.shape, src.dtype), pltpu.HBM(land_shape, src.dtype),
                   jax.ShapeDtypeStruct((8, LANE), F32)),
        in_specs=(HBM_SPEC, HBM_SPEC, ANY),
        out_specs=(SEM_SPEC, SEM_SPEC, HBM_SPEC, HBM_SPEC, pl.BlockSpec(memory_space=pltpu.VMEM)),
        input_output_aliases={0: 2, 1: 3},
        compiler_params=pltpu.CompilerParams(has_side_effects=pltpu.SideEffectType.DATAFLOW_SIDE_EFFECTING),
    )(_hbm(src), _hbm(lax.empty(land_shape, src.dtype)), after)


def _exchange_wait(started, after, ks, per_peer, name):
    send_sems, recv_sems, src_thru, land_thru = started

    def body(src_ref, land_ref, send_sems, recv_sems, after_ref, src_dead, land_out):
        for cp in _split_copies(src_ref, land_ref, send_sems, recv_sems, ks, per_peer, landed=True):
            cp.wait_send()
            cp.wait_recv()

    return pl.pallas_call(
        body, name=name,
        out_shape=(pltpu.HBM(src_thru.shape, src_thru.dtype), pltpu.HBM(land_thru.shape, land_thru.dtype)),
        in_specs=(HBM_SPEC, HBM_SPEC, SEM_SPEC, SEM_SPEC, ANY), out_specs=(HBM_SPEC, HBM_SPEC),
        input_output_aliases={0: 0, 1: 1},
        compiler_params=pltpu.CompilerParams(has_side_effects=pltpu.SideEffectType.DATAFLOW_SIDE_EFFECTING),
    )(src_thru, land_thru, send_sems, recv_sems, after)


def _finish_gather(land, own):
    def body(land_in, own_ref, land_ref, send_sems, recv_sems, local_sem):
        me = _position()
        sibling = _peer(me, 1)
        mine = pltpu.make_async_copy(own_ref, land_ref.at[_slot(me)], local_sem)
        mine.start()
        sent = []
        for i, k in enumerate(GATHER_PEERS[1:]):
            have = land_ref.at[_slot(_peer(me, k))]
            sent.append(pltpu.make_async_remote_copy(
                src_ref=have, dst_ref=have, send_sem=send_sems.at[i], recv_sem=recv_sems.at[i],
                device_id=sibling, device_id_type=MESH))
        for cp in sent:
            cp.start()
        for i, k in enumerate(GATHER_PEERS[1:]):
            lack = land_ref.at[_slot(_peer(sibling, k))]
            pltpu.make_async_remote_copy(
                src_ref=lack, dst_ref=lack, send_sem=send_sems.at[i], recv_sem=recv_sems.at[i],
                device_id=sibling, device_id_type=MESH).wait_recv()
        for cp in sent:
            cp.wait_send()
        mine.wait()

    return pl.pallas_call(
        body, name="finish_gather",
        in_specs=[ANY, ANY], out_specs=ANY,
        out_shape=jax.ShapeDtypeStruct(land.shape, land.dtype),
        input_output_aliases={0: 0},
        scratch_shapes=[pltpu.SemaphoreType.DMA((3,)), pltpu.SemaphoreType.DMA((3,)), pltpu.SemaphoreType.DMA],
    )(land, own)


def _exchange(arrs, name):
    n = len(arrs)

    def body(*refs):
        ins, outs = refs[:n], refs[n:2 * n]
        send_sems, recv_sems, local_sems = refs[2 * n:]
        me = _position()

        def copy(a, k):
            peer = _peer(me, k)
            return pltpu.make_async_remote_copy(
                src_ref=ins[a].at[_slot(peer)], dst_ref=outs[a].at[_slot(me)],
                send_sem=send_sems.at[a * 7 + k - 1], recv_sem=recv_sems.at[a * 7 + k - 1],
                device_id=peer, device_id_type=MESH)

        def landed(a, k):
            peer = _peer(me, k)
            return pltpu.make_async_remote_copy(
                src_ref=ins[a].at[_slot(peer)], dst_ref=outs[a].at[_slot(peer)],
                send_sem=send_sems.at[a * 7 + k - 1], recv_sem=recv_sems.at[a * 7 + k - 1],
                device_id=peer, device_id_type=MESH)

        mine = [pltpu.make_async_copy(ins[a].at[_slot(me)], outs[a].at[_slot(me)], local_sems.at[a])
                for a in range(n)]
        for cp in mine:
            cp.start()
        sent = [copy(a, k) for k in range(1, N_DEV) for a in range(n)]
        for cp in sent:
            cp.start()
        for k in range(1, N_DEV):
            for a in range(n):
                landed(a, k).wait_recv()
        for cp in sent:
            cp.wait_send()
        for cp in mine:
            cp.wait()

    return pl.pallas_call(
        body, name=name,
        in_specs=[ANY] * n, out_specs=[ANY] * n,
        out_shape=[jax.ShapeDtypeStruct(a.shape, a.dtype) for a in arrs],
        scratch_shapes=[pltpu.SemaphoreType.DMA((7 * n,)), pltpu.SemaphoreType.DMA((7 * n,)),
                        pltpu.SemaphoreType.DMA((n,))],
    )(*arrs)


def _sum_slots(parts):
    def body(p_ref, o_ref):
        acc = p_ref[0]
        for dev in range(1, N_DEV):
            acc = acc + p_ref[dev]
        o_ref[...] = acc

    return pl.pallas_call(body, name="sum_slots",
                          out_shape=jax.ShapeDtypeStruct(parts.shape[1:], F32))(parts)


def _pack(arrs):
    flat = jnp.concatenate([a.reshape(-1) for a in arrs])
    pad = (-flat.shape[0]) % (8 * LANE)
    return jnp.pad(flat, (0, pad)).reshape(-1, LANE)


def _unpack(packed, shapes):
    flat = packed.reshape(-1)
    out, at = [], 0
    for shp in shapes:
        size = 1
        for dim in shp:
            size *= dim
        out.append(flat[at:at + size].reshape(shp))
        at += size
    return out


def _layer_fwd(x, wt, wo, g_pre, g_post, wa_pad, b_alpha, g_gla, g_att, rb_pad):
    h = _rms_fwd(x, g_pre)
    z = _matmul(h, wt, "nt", F32, 512, D_ZP // 3, 512, "in_proj")
    y_gla, o_gla, states = _gla_fwd(z, wa_pad, b_alpha, g_gla)
    y_att, o_att = _att_fwd(z, rb_pad, g_att)
    ycat = jnp.concatenate([y_gla, y_att], axis=1)
    y = _matmul(ycat, wo, "nn", F32, 512, 1024, 512, "out_proj")
    out = _post_fwd(x, y, g_post)
    return out, (x, h, z, o_gla, states, o_att, ycat, y)


def _layer_bwd(dout, saved, wt, wo, g_pre, g_post, wa_pad, b_alpha, g_gla, g_att, rb_pad):
    x, h, z, o_gla, states, o_att, ycat, y = saved
    dy, dg_post = _post_bwd(dout, y, g_post)
    dycat = _matmul(dy, wo, "nt", F32, 512, 1024, 512, "out_proj_dx")
    dwo = _matmul(ycat, dy, "tn", BF16, 512, 1024, 512, "out_proj_dw")
    dq, dk, dv, dgg, dga, dwa, db, dg_gla = _gla_bwd(dycat, o_gla, z, wa_pad, b_alpha, g_gla, states)
    daq, dak, dav, dag, drb, dg_att = _att_bwd(dycat, o_att, z, rb_pad, g_att)
    dz = jnp.concatenate([dq, dk, dv, dgg, daq, dak.astype(BF16), dav.astype(BF16), dag, dga], axis=1)
    dh = _matmul(dz, wt, "nn", F32, 512, 1024, D_ZP // 3, "in_proj_dx")
    dwt = _matmul(dz, h, "tn", BF16, D_ZP // 3, 512, 512, "in_proj_dw")
    dx, dg_pre = _pre_bwd(dh, x, g_pre, dout)
    small = (dg_pre[0], dg_post[0], dwa[:GLA_RANK], db[0], dg_gla[0], dg_att[0], drb[:, 0, :N_REL])
    return dx, dwt, dwo, small


def kernel(x, w_in, w_out, g_pre, g_post, w_alpha, b_alpha, g_gla, g_att, rel_bias, loss_target, m_w_in, m_w_out, m_g_pre, m_g_post, m_w_alpha, m_b_alpha, m_g_gla, m_g_att, m_rel_bias, v_w_in, v_w_out, v_g_pre, v_g_post, v_w_alpha, v_b_alpha, v_g_gla, v_g_att, v_rel_bias):
    nl, d, cols = w_in.shape
    rows = w_out.shape[1]
    s = x.shape[1]
    x0 = x.reshape(s, d)
    tgt = loss_target.reshape(s, d)
    swap = lambda a: jnp.transpose(a, (0, 2, 1))

    w_t = swap(w_in)
    shards = [jnp.concatenate([w_out[l].astype(BF16), w_t[l].astype(BF16)], axis=0) for l in range(nl)]
    wa_g = _exchange([jnp.broadcast_to(_pack([w_alpha])[None], (N_DEV,) + _pack([w_alpha]).shape)], "gather_alpha")[0]
    wa_cols = w_alpha.shape[2]
    wa_full = wa_g.reshape(N_DEV, -1)[:, :nl * GLA_RANK * wa_cols].reshape(N_DEV, nl, GLA_RANK, wa_cols)
    wa_full = jnp.transpose(wa_full, (1, 2, 0, 3)).reshape(nl, GLA_RANK, GLA_KW)
    wa_pad = jnp.pad(wa_full, ((0, 0), (0, LANE - GLA_RANK), (0, 0)))
    rb_pad = jnp.pad(rel_bias, ((0, 0), (0, 0), (0, 3 * LANE - N_REL)))

    def layer_args(l, follows_pre=None, follows_post=None):
        gp = g_pre[l:l + 1] if follows_pre is None else g_pre[l:l + 1] + follows_pre[:1, :1]
        gq = g_post[l:l + 1] if follows_post is None else g_post[l:l + 1] + follows_post[:1, :1]
        return (wts[l], wos[l], gp, gq, wa_pad[l], b_alpha[l:l + 1], g_gla[l:l + 1], g_att[l:l + 1], rb_pad[l])

    act, saved, wts, wos = x0, [], [], []
    started = _exchange_start(shards[0], x0, GATHER_PEERS, False, "gather_start_0")
    for l in range(nl):
        _, land = _exchange_wait(started[:4], act, GATHER_PEERS, False, f"gather_wait_{l}")
        token = None
        if l + 1 < nl:
            started = _exchange_start(shards[l + 1], land, GATHER_PEERS, False, f"gather_start_{l + 1}")
            token = started[4]
        land = _finish_gather(land, shards[l])
        wos.append(land[:, :rows].reshape(N_DEV * rows, d))
        wts.append(_to_aligned_rows(land[:, rows:].reshape(N_DEV * cols, d)))
        act, sv = _layer_fwd(act, *layer_args(l, follows_pre=token))
        saved.append(sv)
    dout, sq = _loss_head(act, tgt)
    loss = lax.psum(sq[0, 0] * (0.5 / d), ("x", "y", "c"))

    smalls, pending, token = [None] * nl, [None] * nl, None
    for l in reversed(range(nl)):
        dout, dwt, dwo, smalls[l] = _layer_bwd(dout, saved[l], *layer_args(l, follows_post=token))
        partial = jnp.concatenate([dwo.reshape(N_DEV, rows, d), _from_aligned_rows(dwt).reshape(N_DEV, cols, d)],
                                  axis=1)
        pending[l] = _exchange_start(partial, dout, ALL_PEERS, True, f"scatter_start_{l}")
        token = pending[l][4]
    grad_x = dout.reshape(x.shape)

    my = _slot(_position())
    parts = []
    for l in range(nl):
        partial, land = _exchange_wait(pending[l][:4], dout, ALL_PEERS, True, f"scatter_wait_{l}")
        parts.append(lax.dynamic_update_slice_in_dim(land, lax.dynamic_slice_in_dim(partial, my, 1, 0), my, 0))
    g_w_in, d_w_in, m2_w_in, v2_w_in = [
        swap(a) for a in _adam_sharded(parts, rows // ADAM_ROWS, w_t, swap(m_w_in), swap(v_w_in), "adam_w_in")]
    g_w_out, d_w_out, m2_w_out, v2_w_out = _adam_sharded(parts, 0, w_out, m_w_out, v_w_out, "adam_w_out")

    names = 7
    small_stacked = [jnp.stack([smalls[l][i] for l in range(nl)]) for i in range(names)]
    shapes = [a.shape for a in small_stacked]
    packed = _pack(small_stacked)
    gathered = _exchange([jnp.broadcast_to(packed[None], (N_DEV,) + packed.shape)], "gather_small_grads")[0]
    g_pre_g, g_post_g, wa_g_full, b_g, gla_g, att_g, rb_g = _unpack(_sum_slots(gathered), shapes)
    wa_g_mine = lax.dynamic_slice_in_dim(wa_g_full, my * wa_cols, wa_cols, axis=2)
    grads = [g_pre_g, g_post_g, wa_g_mine, b_g, gla_g, att_g, rb_g]
    ws = [g_pre, g_post, w_alpha, b_alpha, g_gla, g_att, rel_bias]
    ms = [m_g_pre, m_g_post, m_w_alpha, m_b_alpha, m_g_gla, m_g_att, m_rel_bias]
    vs = [v_g_pre, v_g_post, v_w_alpha, v_b_alpha, v_g_gla, v_g_att, v_rel_bias]
    shapes2 = [a.shape for a in ws]
    d_s, m2_s, v2_s = _adam_small(_pack(ws), _pack(grads), _pack(ms), _pack(vs))
    d_s, m2_s, v2_s = _unpack(d_s, shapes2), _unpack(m2_s, shapes2), _unpack(v2_s, shapes2)

    def ordered(big_in, big_out, small):
        return [big_in, big_out] + list(small)

    return (loss, grad_x,
            *ordered(g_w_in, g_w_out, grads),
            *ordered(d_w_in, d_w_out, d_s),
            *ordered(m2_w_in, m2_w_out, m2_s),
            *ordered(v2_w_in, v2_w_out, v2_s))
```

```python
import functools

import jax
import jax.numpy as jnp
from jax import lax
from jax.experimental import pallas as pl
from jax.experimental.pallas import tpu as pltpu

F32 = jnp.float32
BF16 = jnp.bfloat16
MESH = pl.DeviceIdType.MESH
ANY = pl.BlockSpec(memory_space=pl.ANY)

CHUNK = 64
GLA_HEADS = 4
GLA_DK = 128
GLA_DV = 256
GLA_KW = GLA_HEADS * GLA_DK
D_GLA = GLA_HEADS * GLA_DV
GLA_RANK = 16
GLA_TAU = 16.0
ATT_HEADS = 8
ATT_HD = 128
D_ATT = ATT_HEADS * ATT_HD
LEFT_CHUNKS = 8
REL_CLIP = 128
N_REL = 2 * REL_CLIP + 1
EPS = 1e-6
D_IN = 2 * GLA_KW + 2 * D_GLA + GLA_RANK + 4 * D_ATT
GLA_SCALE = GLA_DK ** -0.5
ATT_SCALE = ATT_HD ** -0.5

ADAM_LR = 0.001
ADAM_B1 = 0.9
ADAM_B2 = 0.999
ADAM_EPS = 1e-08
ADAM_WD = 0.01
ADAM_STEP = 10

N_DEV = 8
LANE = 128
GA_ORIG = 2 * GLA_KW + 2 * D_GLA
OFF_AQ = GA_ORIG
OFF_GA = GA_ORIG + 4 * D_ATT
D_ZP = OFF_GA + LANE
QB = 2 * CHUNK
WIN = (LEFT_CHUNKS + 2) * CHUNK
ET_ROWS = WIN + LEFT_CHUNKS * CHUNK
NEG = -1e30
VMEM_LIMIT = 48 * 1024 * 1024


def _cparams(sem):
    return pltpu.CompilerParams(dimension_semantics=sem, vmem_limit_bytes=VMEM_LIMIT)


def _dot(a, b):
    return jnp.dot(a, b, preferred_element_type=F32)


def _dot_nt(a, b):
    return lax.dot_general(a, b, (((1,), (1,)), ((), ())), preferred_element_type=F32)


def _dot_tn(a, b):
    return lax.dot_general(a, b, (((0,), (0,)), ((), ())), preferred_element_type=F32)


def _dot01(t, x, left=True):
    if not left:
        t, x = x, t
    hi = x.astype(BF16)
    r = x - hi.astype(F32)
    mid = r.astype(BF16)
    lo = (r - mid.astype(F32)).astype(BF16)
    if left:
        return _dot(t, hi) + _dot(t, mid) + _dot(t, lo)
    return _dot(hi, t) + _dot(mid, t) + _dot(lo, t)


def _sigmoid(x):
    return 1.0 / (1.0 + jnp.exp(-x))


def _log_sigmoid(x):
    return jnp.minimum(x, 0.0) - jnp.log(1.0 + jnp.exp(-jnp.abs(x)))


def _matmul(a, b, mode, out_dtype, tm, tn, tk, name, b_layer=None):
    bshape = b.shape if b_layer is None else b.shape[1:]
    if mode == "nn":
        (m, k), n = a.shape, bshape[1]
    elif mode == "nt":
        (m, k), n = a.shape, bshape[0]
    else:
        (k, m), n = a.shape, bshape[1]
    tm, tn, tk = min(tm, m), min(tn, n), min(tk, k)
    assert m % tm == 0 and n % tn == 0 and k % tk == 0, (name, m, n, k)
    nk = k // tk

    def body(a_ref, b_ref, o_ref, acc_ref):
        kk = pl.program_id(2)

        @pl.when(kk == 0)
        def _():
            acc_ref[...] = jnp.zeros_like(acc_ref)

        if mode == "nn":
            acc_ref[...] += _dot(a_ref[...], b_ref[...])
        elif mode == "nt":
            acc_ref[...] += _dot_nt(a_ref[...], b_ref[...])
        else:
            acc_ref[...] += _dot_tn(a_ref[...], b_ref[...])

        @pl.when(kk == nk - 1)
        def _():
            o_ref[...] = acc_ref[...].astype(out_dtype)

    if mode == "tn":
        a_spec = pl.BlockSpec((tk, tm), lambda i, j, kk: (kk, i))
    else:
        a_spec = pl.BlockSpec((tm, tk), lambda i, j, kk: (i, kk))
    b_blk, b_idx = ((tn, tk), lambda i, j, kk: (j, kk)) if mode == "nt" else ((tk, tn), lambda i, j, kk: (kk, j))
    if b_layer is None:
        b_spec = pl.BlockSpec(b_blk, b_idx)
    else:
        b_spec = pl.BlockSpec((None,) + b_blk, lambda i, j, kk: (b_layer,) + b_idx(i, j, kk))
    return pl.pallas_call(
        body, name=name,
        grid=(m // tm, n // tn, nk),
        in_specs=[a_spec, b_spec],
        out_specs=pl.BlockSpec((tm, tn), lambda i, j, kk: (i, j)),
        out_shape=jax.ShapeDtypeStruct((m, n), out_dtype),
        scratch_shapes=[pltpu.VMEM((tm, tn), F32)],
        compiler_params=_cparams(("parallel", "parallel", "arbitrary")),
    )(a, b)


ROWS = 256


def _rms_fwd(x, g):
    s, d = x.shape

    def body(x_ref, g_ref, h_ref):
        xv = x_ref[...]
        r = lax.rsqrt(jnp.mean(xv * xv, axis=-1, keepdims=True) + EPS)
        h_ref[...] = (xv * r * g_ref[...]).astype(BF16)

    return pl.pallas_call(
        body, name="rms_fwd", grid=(s // ROWS,),
        in_specs=[pl.BlockSpec((ROWS, d), lambda i: (i, 0)), pl.BlockSpec((1, d), lambda i: (0, 0))],
        out_specs=pl.BlockSpec((ROWS, d), lambda i: (i, 0)),
        out_shape=jax.ShapeDtypeStruct((s, d), BF16),
        compiler_params=_cparams(("parallel",)),
    )(x, g)


def _post_fwd(x, y, g):
    s, d = x.shape

    def body(x_ref, y_ref, g_ref, o_ref):
        yv = y_ref[...]
        r = lax.rsqrt(jnp.mean(yv * yv, axis=-1, keepdims=True) + EPS)
        o_ref[...] = x_ref[...] + yv * r * g_ref[...]

    row = pl.BlockSpec((ROWS, d), lambda i: (i, 0))
    return pl.pallas_call(
        body, name="post_fwd", grid=(s // ROWS,),
        in_specs=[row, row, pl.BlockSpec((1, d), lambda i: (0, 0))],
        out_specs=row,
        out_shape=jax.ShapeDtypeStruct((s, d), F32),
        compiler_params=_cparams(("parallel",)),
    )(x, y, g)


def _loss_head(out, tgt):
    s, d = out.shape

    def body(o_ref, t_ref, dout_ref, sum_ref):
        @pl.when(pl.program_id(0) == 0)
        def _():
            sum_ref[...] = jnp.zeros_like(sum_ref)

        e = o_ref[...] - t_ref[...]
        dout_ref[...] = e * (1.0 / d)
        sum_ref[...] += jnp.sum(jnp.sum(e * e, axis=1, keepdims=True), axis=0, keepdims=True)

    row = pl.BlockSpec((ROWS, d), lambda i: (i, 0))
    return pl.pallas_call(
        body, name="loss_head", grid=(s // ROWS,),
        in_specs=[row, row],
        out_specs=[row, pl.BlockSpec((1, 1), lambda i: (0, 0))],
        out_shape=[jax.ShapeDtypeStruct((s, d), F32), jax.ShapeDtypeStruct((1, 1), F32)],
        compiler_params=_cparams(("arbitrary",)),
    )(out, tgt)


def _post_bwd(dout, y, g):
    s, d = y.shape

    def body(do_ref, y_ref, g_ref, dy_ref, dg_ref):
        @pl.when(pl.program_id(0) == 0)
        def _():
            dg_ref[...] = jnp.zeros_like(dg_ref)

        yv = y_ref[...]
        dv = do_ref[...]
        r = lax.rsqrt(jnp.mean(yv * yv, axis=-1, keepdims=True) + EPS)
        dg_ref[...] += jnp.sum(dv * yv * r, axis=0, keepdims=True)
        w = dv * g_ref[...]
        dy = r * (w - yv * (r * r) * jnp.mean(w * yv, axis=-1, keepdims=True))
        dy_ref[...] = dy.astype(BF16)

    row = pl.BlockSpec((ROWS, d), lambda i: (i, 0))
    vec = pl.BlockSpec((1, d), lambda i: (0, 0))
    return pl.pallas_call(
        body, name="post_bwd", grid=(s // ROWS,),
        in_specs=[row, row, vec],
        out_specs=[row, vec],
        out_shape=[jax.ShapeDtypeStruct((s, d), BF16), jax.ShapeDtypeStruct((1, d), F32)],
        compiler_params=_cparams(("arbitrary",)),
    )(dout, y, g)


def _pre_bwd(dh, x, g, dout):
    s, d = x.shape

    def body(dh_ref, x_ref, g_ref, do_ref, dx_ref, dg_ref):
        @pl.when(pl.program_id(0) == 0)
        def _():
            dg_ref[...] = jnp.zeros_like(dg_ref)

        xv = x_ref[...]
        dv = dh_ref[...]
        r = lax.rsqrt(jnp.mean(xv * xv, axis=-1, keepdims=True) + EPS)
        dg_ref[...] += jnp.sum(dv * xv * r, axis=0, keepdims=True)
        w = dv * g_ref[...]
        dx_ref[...] = do_ref[...] + r * (w - xv * (r * r) * jnp.mean(w * xv, axis=-1, keepdims=True))

    row = pl.BlockSpec((ROWS, d), lambda i: (i, 0))
    vec = pl.BlockSpec((1, d), lambda i: (0, 0))
    return pl.pallas_call(
        body, name="pre_bwd", grid=(s // ROWS,),
        in_specs=[row, row, vec, row],
        out_specs=[row, vec],
        out_shape=[jax.ShapeDtypeStruct((s, d), F32), jax.ShapeDtypeStruct((1, d), F32)],
        compiler_params=_cparams(("arbitrary",)),
    )(dh, x, g, dout)


def _gla_gate(ga_b, wa_ref, b_ref, cs, tri):
    pre = _dot(ga_b, wa_ref[:, cs].astype(BF16)) + b_ref[:, cs]
    la = _log_sigmoid(pre) * (1.0 / GLA_TAU)
    cum = _dot01(tri, la)
    last = lax.broadcasted_iota(jnp.int32, cum.shape, 0) == CHUNK - 1
    return pre, cum, jnp.sum(jnp.where(last, cum, 0.0), axis=0, keepdims=True)


def _z_specs_gla(rev=None):
    idx = (lambda n: n) if rev is None else rev
    return [
        pl.BlockSpec((CHUNK, GLA_KW), lambda n: (idx(n), 0)),
        pl.BlockSpec((CHUNK, GLA_KW), lambda n: (idx(n), 1)),
        pl.BlockSpec((CHUNK, D_GLA), lambda n: (idx(n), 1)),
        pl.BlockSpec((CHUNK, D_GLA), lambda n: (idx(n), 2)),
        pl.BlockSpec((CHUNK, LANE), lambda n: (idx(n), OFF_GA // LANE)),
    ]


def _gla_fwd(z, wa_pad, b_alpha, g_gla):
    s = z.shape[0]
    nchunk = s // CHUNK

    def body(q_ref, k_ref, v_ref, gg_ref, ga_ref, wa_ref, b_ref, g_ref, y_ref, o_ref, st_ref, state):
        @pl.when(pl.program_id(0) == 0)
        def _():
            state[...] = jnp.zeros_like(state)

        ga_b = ga_ref[...].astype(BF16)
        ri = lax.broadcasted_iota(jnp.int32, (CHUNK, CHUNK), 0)
        ci = lax.broadcasted_iota(jnp.int32, (CHUNK, CHUNK), 1)
        tri = jnp.where(ri >= ci, 1.0, 0.0).astype(BF16)
        for h in range(GLA_HEADS):
            cs = slice(h * GLA_DK, (h + 1) * GLA_DK)
            vs = slice(h * GLA_DV, (h + 1) * GLA_DV)
            _, cum, cend = _gla_gate(ga_b, wa_ref, b_ref, cs, tri)
            kd = k_ref[:, cs] * jnp.exp(cend - cum)
            st = state[h] * jnp.exp(cend) + _dot_tn(v_ref[:, vs].astype(BF16), kd.astype(BF16))
            state[h] = st
            st_ref[0, h] = st
            qs = (q_ref[:, cs] * GLA_SCALE).astype(BF16)
            o = _dot_nt(qs, st.astype(BF16))
            o_ref[:, vs] = o
            r = lax.rsqrt(jnp.mean(o * o, axis=-1, keepdims=True) + EPS)
            gg = gg_ref[:, vs]
            y_ref[:, vs] = (o * r * g_ref[:, vs] * (gg * _sigmoid(gg))).astype(BF16)

    full = lambda shape: pl.BlockSpec(shape, lambda n: tuple(0 for _ in shape))
    wide = pl.BlockSpec((CHUNK, D_GLA), lambda n: (n, 0))
    return pl.pallas_call(
        body, name="gla_fwd", grid=(nchunk,),
        in_specs=_z_specs_gla() + [full((LANE, GLA_KW)), full((1, GLA_KW)), full((1, D_GLA))],
        out_specs=[wide, wide, pl.BlockSpec((1, GLA_HEADS, GLA_DV, GLA_DK), lambda n: (n, 0, 0, 0))],
        out_shape=[jax.ShapeDtypeStruct((s, D_GLA), BF16), jax.ShapeDtypeStruct((s, D_GLA), F32),
                   jax.ShapeDtypeStruct((nchunk, GLA_HEADS, GLA_DV, GLA_DK), F32)],
        scratch_shapes=[pltpu.VMEM((GLA_HEADS, GLA_DV, GLA_DK), F32)],
        compiler_params=_cparams(("arbitrary",)),
    )(z, z, z, z, z, wa_pad, b_alpha, g_gla)


def _gla_bwd(dyc, o_gla, z, wa_pad, b_alpha, g_gla, states):
    s = z.shape[0]
    nchunk = s // CHUNK
    rev = lambda n: nchunk - 1 - n

    def body(dy_ref, o_ref, q_ref, k_ref, v_ref, gg_ref, ga_ref, wa_ref, b_ref, g_ref, st_ref, stp_ref,
             dq_ref, dk_ref, dv_ref, dgg_ref, dga_ref, dwa_ref, db_ref, dg_ref, carry):
        step = pl.program_id(0)

        @pl.when(step == 0)
        def _():
            carry[...] = jnp.zeros_like(carry)
            dwa_ref[...] = jnp.zeros_like(dwa_ref)
            db_ref[...] = jnp.zeros_like(db_ref)
            dg_ref[...] = jnp.zeros_like(dg_ref)

        has_prev = (step < nchunk - 1).astype(F32)
        ga_b = ga_ref[...].astype(BF16)
        ri = lax.broadcasted_iota(jnp.int32, (CHUNK, CHUNK), 0)
        ci = lax.broadcasted_iota(jnp.int32, (CHUNK, CHUNK), 1)
        tri = jnp.where(ri >= ci, 1.0, 0.0).astype(BF16)
        tri_up = jnp.where(ci >= ri, 1.0, 0.0).astype(BF16)
        dga = jnp.zeros((CHUNK, LANE), F32)
        for h in range(GLA_HEADS):
            cs = slice(h * GLA_DK, (h + 1) * GLA_DK)
            vs = slice(h * GLA_DV, (h + 1) * GLA_DV)
            pre, cum, cend = _gla_gate(ga_b, wa_ref, b_ref, cs, tri)
            e = jnp.exp(cend - cum)
            a = jnp.exp(cend)
            kf = k_ref[:, cs]
            kd_b = (kf * e).astype(BF16)
            v_b = v_ref[:, vs].astype(BF16)
            qs = (q_ref[:, cs] * GLA_SCALE).astype(BF16)
            o = o_ref[:, vs]
            gg = gg_ref[:, vs]
            g = g_ref[:, vs]
            dy = dy_ref[:, vs]
            r = lax.rsqrt(jnp.mean(o * o, axis=-1, keepdims=True) + EPS)
            sg = _sigmoid(gg)
            dogn = dy * (gg * sg)
            dgg_ref[:, vs] = (dy * (o * r * g) * (sg * (1.0 + gg * (1.0 - sg)))).astype(BF16)
            dg_ref[:, vs] += jnp.sum(dogn * o * r, axis=0, keepdims=True)
            w = dogn * g
            do_b = (r * (w - o * (r * r) * jnp.mean(w * o, axis=-1, keepdims=True))).astype(BF16)
            st = st_ref[0, h]
            dq_ref[:, cs] = (_dot(do_b, st.astype(BF16)) * GLA_SCALE).astype(BF16)
            gt = _dot_tn(do_b, qs) + carry[h]
            gt_b = gt.astype(BF16)
            dkd = _dot(v_b, gt_b)
            dv_ref[:, vs] = _dot_nt(kd_b, gt_b).astype(BF16)
            da = jnp.sum(gt * (stp_ref[0, h] * has_prev), axis=0, keepdims=True)
            carry[h] = gt * a
            dk_ref[:, cs] = (dkd * e).astype(BF16)
            dd = dkd * kf * e
            dcend = jnp.sum(dd, axis=0, keepdims=True) + da * a
            dla = dcend - _dot01(tri_up, dd)
            dpre = dla * (1.0 / GLA_TAU) * (1.0 - _sigmoid(pre))
            dpre_b = dpre.astype(BF16)
            dga = dga + _dot_nt(dpre_b, wa_ref[:, cs].astype(BF16))
            dwa_ref[:, cs] += _dot_tn(ga_b, dpre_b)
            db_ref[:, cs] += jnp.sum(dpre, axis=0, keepdims=True)
        dga_ref[...] = dga.astype(BF16)

    full = lambda shape: pl.BlockSpec(shape, lambda n: tuple(0 for _ in shape))
    wide = pl.BlockSpec((CHUNK, D_GLA), lambda n: (rev(n), 0))
    keyw = pl.BlockSpec((CHUNK, GLA_KW), lambda n: (rev(n), 0))
    st_spec = pl.BlockSpec((1, GLA_HEADS, GLA_DV, GLA_DK), lambda n: (rev(n), 0, 0, 0))
    stp_spec = pl.BlockSpec((1, GLA_HEADS, GLA_DV, GLA_DK), lambda n: (jnp.maximum(rev(n) - 1, 0), 0, 0, 0))
    return pl.pallas_call(
        body, name="gla_bwd", grid=(nchunk,),
        in_specs=[wide, wide] + _z_specs_gla(rev)
        + [full((LANE, GLA_KW)), full((1, GLA_KW)), full((1, D_GLA)), st_spec, stp_spec],
        out_specs=[keyw, keyw, wide, wide, pl.BlockSpec((CHUNK, LANE), lambda n: (rev(n), 0)),
                   full((LANE, GLA_KW)), full((1, GLA_KW)), full((1, D_GLA))],
        out_shape=[jax.ShapeDtypeStruct((s, GLA_KW), BF16), jax.ShapeDtypeStruct((s, GLA_KW), BF16),
                   jax.ShapeDtypeStruct((s, D_GLA), BF16), jax.ShapeDtypeStruct((s, D_GLA), BF16),
                   jax.ShapeDtypeStruct((s, LANE), BF16),
                   jax.ShapeDtypeStruct((LANE, GLA_KW), F32), jax.ShapeDtypeStruct((1, GLA_KW), F32),
                   jax.ShapeDtypeStruct((1, D_GLA), F32)],
        scratch_shapes=[pltpu.VMEM((GLA_HEADS, GLA_DV, GLA_DK), F32)],
        compiler_params=_cparams(("arbitrary",)),
    )(dyc, o_gla, z, z, z, z, z, wa_pad, b_alpha, g_gla, states, states)


def _build_bias_table(rb_row, et_ref):
    far = jnp.broadcast_to(rb_row[:, 2 * REL_CLIP:2 * REL_CLIP + 1], (1, LANE))
    near_hi = rb_row[:, REL_CLIP:2 * REL_CLIP]
    near_lo = rb_row[:, 0:REL_CLIP]
    past = jnp.broadcast_to(rb_row[:, 0:1], (1, LANE))
    seg = [far, far, far, far, near_hi, near_lo] + [past] * (ET_ROWS // LANE - 5)
    ri = lax.broadcasted_iota(jnp.int32, (LANE, LANE), 0)
    ci = lax.broadcasted_iota(jnp.int32, (LANE, LANE), 1)
    for kb in range(ET_ROWS // LANE):
        wmat = jnp.where(ri + ci < LANE, seg[kb], seg[kb + 1])
        blk = pltpu.roll(wmat, 0, 1, stride=1, stride_axis=0)
        lag = LEFT_CHUNKS + ci // CHUNK - (2 * kb + ri // CHUNK)
        et_ref[kb * LANE:(kb + 1) * LANE, :] = jnp.where((lag >= 0) & (lag <= LEFT_CHUNKS), blk, NEG)


def _reduce_bias_table(det_ref):
    lane = lax.broadcasted_iota(jnp.int32, (1, LANE), 1)
    ri = lax.broadcasted_iota(jnp.int32, (LANE, LANE), 0)
    ci = lax.broadcasted_iota(jnp.int32, (LANE, LANE), 1)
    flip = jnp.where(ri + ci == LANE - 1, 1.0, 0.0).astype(BF16)
    segs = jnp.zeros((8, LANE), F32)
    seg_row = lax.broadcasted_iota(jnp.int32, (8, LANE), 0)
    prev_minus = jnp.zeros((1, LANE), F32)
    for kb in range(6):
        rolled = pltpu.roll(_dot01(det_ref[kb * LANE:(kb + 1) * LANE, :], flip, left=False), 0, 1,
                            stride=1, stride_axis=0)
        plus = jnp.sum(jnp.where(ci >= ri, rolled, 0.0), axis=0, keepdims=True)
        minus = jnp.sum(jnp.where(ci < ri, rolled, 0.0), axis=0, keepdims=True)
        segs = segs + jnp.where(seg_row == kb, plus + prev_minus, 0.0)
        prev_minus = minus
    segs = _dot01(segs, flip, left=False)
    pick = lambda kb: jnp.sum(jnp.where(seg_row == kb, segs, 0.0), axis=0, keepdims=True)
    far = jnp.sum(pick(0) + pick(1) + pick(2) + pick(3), axis=1, keepdims=True)
    last = jnp.where(lane == 0, far, 0.0)
    return jnp.concatenate([pick(5), pick(4), last], axis=1)


def _att_window(b):
    c0 = 2 * b
    kstart = pl.multiple_of(jnp.maximum(c0 - LEFT_CHUNKS, 0) * CHUNK, CHUNK)
    eoff = pl.multiple_of(jnp.maximum(LEFT_CHUNKS - c0, 0) * CHUNK, CHUNK)
    return kstart, eoff


def _att_probs(q_b, kw_b, et):
    st = _dot_nt(kw_b, q_b) * ATT_SCALE + et
    m = jnp.max(st, axis=0, keepdims=True)
    ex = jnp.exp(st - m)
    return ex / jnp.sum(ex, axis=0, keepdims=True)


def _att_fwd(z, rb_pad, g_att):
    s = z.shape[0]
    nblk = s // QB
    c_aq, c_ak, c_av, c_ag = [(OFF_AQ + i * D_ATT) // ATT_HD for i in range(4)]

    def body(q_ref, k_ref, v_ref, ag_ref, rb_ref, g_ref, y_ref, o_ref, et_ref):
        h = pl.program_id(0)
        b = pl.program_id(1)

        @pl.when(b == 0)
        def _():
            _build_bias_table(rb_ref[pl.ds(h, 1), :], et_ref)

        kstart, eoff = _att_window(b)
        q_b = q_ref[...].astype(BF16)
        kw_b = k_ref[pl.ds(kstart, WIN), :].astype(BF16)
        vw_b = v_ref[pl.ds(kstart, WIN), :].astype(BF16)
        pt = _att_probs(q_b, kw_b, et_ref[pl.ds(eoff, WIN), :])
        o = _dot_tn(pt.astype(BF16), vw_b)
        o_ref[...] = o
        r = lax.rsqrt(jnp.mean(o * o, axis=-1, keepdims=True) + EPS)
        ag = ag_ref[...]
        y_ref[...] = (o * r * g_ref[...] * (ag * _sigmoid(ag))).astype(BF16)

    blk = lambda col: pl.BlockSpec((QB, ATT_HD), lambda h, b: (b, col + h))
    seq = lambda col: pl.BlockSpec((s, ATT_HD), lambda h, b: (0, col + h))
    out_blk = pl.BlockSpec((QB, ATT_HD), lambda h, b: (b, h))
    return pl.pallas_call(
        body, name="att_fwd", grid=(ATT_HEADS, nblk),
        in_specs=[blk(c_aq), seq(c_ak), seq(c_av), blk(c_ag),
                  pl.BlockSpec((ATT_HEADS, 3 * LANE), lambda h, b: (0, 0)),
                  pl.BlockSpec((1, ATT_HD), lambda h, b: (0, h))],
        out_specs=[out_blk, out_blk],
        out_shape=[jax.ShapeDtypeStruct((s, D_ATT), BF16), jax.ShapeDtypeStruct((s, D_ATT), F32)],
        scratch_shapes=[pltpu.VMEM((ET_ROWS, LANE), F32)],
        compiler_params=_cparams(("arbitrary", "arbitrary")),
    )(z, z, z, z, rb_pad, g_att)


def _att_bwd(dyc, o_att, z, rb_pad, g_att):
    s = z.shape[0]
    nblk = s // QB
    c_aq, c_ak, c_av, c_ag = [(OFF_AQ + i * D_ATT) // ATT_HD for i in range(4)]
    c_dy = D_GLA // ATT_HD

    def body(dy_ref, o_ref, q_ref, k_ref, v_ref, ag_ref, rb_ref, g_ref,
             dq_ref, dk_ref, dv_ref, dag_ref, drb_ref, dg_ref, et_ref, det_ref):
        h = pl.program_id(0)
        b = pl.program_id(1)

        @pl.when(b == 0)
        def _():
            _build_bias_table(rb_ref[pl.ds(h, 1), :], et_ref)
            det_ref[...] = jnp.zeros_like(det_ref)
            dk_ref[...] = jnp.zeros_like(dk_ref)
            dv_ref[...] = jnp.zeros_like(dv_ref)
            dg_ref[...] = jnp.zeros_like(dg_ref)

        kstart, eoff = _att_window(b)
        q_b = q_ref[...].astype(BF16)
        kw_b = k_ref[pl.ds(kstart, WIN), :].astype(BF16)
        vw_b = v_ref[pl.ds(kstart, WIN), :].astype(BF16)
        pt = _att_probs(q_b, kw_b, et_ref[pl.ds(eoff, WIN), :])
        o = o_ref[...]
        ag = ag_ref[...]
        g = g_ref[...]
        dy = dy_ref[...]
        r = lax.rsqrt(jnp.mean(o * o, axis=-1, keepdims=True) + EPS)
        sg = _sigmoid(ag)
        don = dy * (ag * sg)
        dag_ref[...] = (dy * (o * r * g) * (sg * (1.0 + ag * (1.0 - sg)))).astype(BF16)
        dg_ref[...] += jnp.sum(don * o * r, axis=0, keepdims=True)
        w = don * g
        do_b = (r * (w - o * (r * r) * jnp.mean(w * o, axis=-1, keepdims=True))).astype(BF16)
        pt_b = pt.astype(BF16)
        dpt = _dot_nt(vw_b, do_b)
        dst = pt * (dpt - jnp.sum(dpt * pt, axis=0, keepdims=True))
        det_ref[pl.ds(eoff, WIN), :] += dst
        ds_b = (dst * ATT_SCALE).astype(BF16)
        dq_ref[...] = _dot_tn(ds_b, kw_b).astype(BF16)
        dk_ref[pl.ds(kstart, WIN), :] += _dot(ds_b, q_b)
        dv_ref[pl.ds(kstart, WIN), :] += _dot(pt_b, do_b)

        @pl.when(b == nblk - 1)
        def _():
            drb_ref[0] = jnp.broadcast_to(_reduce_bias_table(det_ref), (8, 3 * LANE))

    blk = lambda col: pl.BlockSpec((QB, ATT_HD), lambda h, b: (b, col + h))
    seq = lambda col: pl.BlockSpec((s, ATT_HD), lambda h, b: (0, col + h))
    out_blk = pl.BlockSpec((QB, ATT_HD), lambda h, b: (b, h))
    out_seq = pl.BlockSpec((s, ATT_HD), lambda h, b: (0, h))
    return pl.pallas_call(
        body, name="att_bwd", grid=(ATT_HEADS, nblk),
        in_specs=[blk(c_dy), blk(0), blk(c_aq), seq(c_ak), seq(c_av), blk(c_ag),
                  pl.BlockSpec((ATT_HEADS, 3 * LANE), lambda h, b: (0, 0)),
                  pl.BlockSpec((1, ATT_HD), lambda h, b: (0, h))],
        out_specs=[out_blk, out_seq, out_seq, out_blk,
                   pl.BlockSpec((1, 8, 3 * LANE), lambda h, b: (h, 0, 0)),
                   pl.BlockSpec((1, ATT_HD), lambda h, b: (0, h))],
        out_shape=[jax.ShapeDtypeStruct((s, D_ATT), BF16), jax.ShapeDtypeStruct((s, D_ATT), F32),
                   jax.ShapeDtypeStruct((s, D_ATT), F32), jax.ShapeDtypeStruct((s, D_ATT), BF16),
                   jax.ShapeDtypeStruct((ATT_HEADS, 8, 3 * LANE), F32),
                   jax.ShapeDtypeStruct((1, D_ATT), F32)],
        scratch_shapes=[pltpu.VMEM((ET_ROWS, LANE), F32), pltpu.VMEM((ET_ROWS, LANE), F32)],
        compiler_params=_cparams(("arbitrary", "arbitrary")),
    )(dyc, o_att, z, z, z, z, rb_pad, g_att)


ADAM_ROWS = 64
ADAM_COL_ROWS = 32


def _adam_math(w, g, m, v):
    m2 = ADAM_B1 * m + (1.0 - ADAM_B1) * g
    v2 = ADAM_B2 * v + (1.0 - ADAM_B2) * (g * g)
    m_hat = m2 / (1.0 - ADAM_B1 ** ADAM_STEP)
    v_hat = v2 / (1.0 - ADAM_B2 ** ADAM_STEP)
    delta = -ADAM_LR * (m_hat / (jnp.sqrt(v_hat) + ADAM_EPS) + ADAM_WD * w)
    return delta, m2, v2


def _adam_sharded(parts, first, w, m, v, name):
    nl, nr, nc = w.shape

    def body(*refs):
        p_refs = refs[:nl]
        w_ref, m_ref, v_ref, g_ref, d_ref, m2_ref, v2_ref = refs[nl:]
        for k in range(nl):
            @pl.when(pl.program_id(0) == k)
            def _(p_ref=p_refs[k]):
                g = p_ref[0].astype(F32)
                for dev in range(1, N_DEV):
                    g = g + p_ref[dev].astype(F32)
                delta, m2, v2 = _adam_math(w_ref[0], g, m_ref[0], v_ref[0])
                g_ref[0] = g
                d_ref[0] = delta
                m2_ref[0] = m2
                v2_ref[0] = v2

    def part_spec(k):
        return pl.BlockSpec((N_DEV, ADAM_ROWS, nc), lambda l, i: (0, first + jnp.where(l == k, i, 0), 0))

    blk = pl.BlockSpec((1, ADAM_ROWS, nc), lambda l, i: (l, i, 0))
    shp = jax.ShapeDtypeStruct(w.shape, F32)
    return pl.pallas_call(
        body, name=name, grid=(nl, pl.cdiv(nr, ADAM_ROWS)),
        in_specs=[part_spec(k) for k in range(nl)] + [blk, blk, blk],
        out_specs=[blk, blk, blk, blk],
        out_shape=[shp, shp, shp, shp],
        compiler_params=_cparams(("arbitrary", "arbitrary")),
    )(*parts, w, m, v)


def _adam_columns(parts, first, w, m, v):
    nc, nl, d = w.shape

    def body(*refs):
        p_refs = refs[:nl]
        w_ref, m_ref, v_ref, g_ref, d_ref, m2_ref, v2_ref = refs[nl:]
        for l in range(nl):
            g = p_refs[l][0].astype(F32)
            for dev in range(1, N_DEV):
                g = g + p_refs[l][dev].astype(F32)
            delta, m2, v2 = _adam_math(w_ref[:, l, :], g, m_ref[:, l, :], v_ref[:, l, :])
            g_ref[:, l, :] = g
            d_ref[:, l, :] = delta
            m2_ref[:, l, :] = m2
            v2_ref[:, l, :] = v2

    blk = pl.BlockSpec((ADAM_COL_ROWS, nl, d), lambda i: (i, 0, 0))
    part = pl.BlockSpec((N_DEV, ADAM_COL_ROWS, d), lambda i: (0, first + i, 0))
    shp = jax.ShapeDtypeStruct(w.shape, F32)
    return pl.pallas_call(
        body, name="adam_w_in", grid=(pl.cdiv(nc, ADAM_COL_ROWS),),
        in_specs=[part] * nl + [blk, blk, blk],
        out_specs=[blk, blk, blk, blk],
        out_shape=[shp, shp, shp, shp],
        compiler_params=_cparams(("parallel",)),
    )(*parts, w, m, v)


def _adam_small(w, g, m, v):
    def body(w_ref, g_ref, m_ref, v_ref, d_ref, m2_ref, v2_ref):
        delta, m2, v2 = _adam_math(w_ref[...], g_ref[...], m_ref[...], v_ref[...])
        d_ref[...] = delta
        m2_ref[...] = m2
        v2_ref[...] = v2

    shp = jax.ShapeDtypeStruct(w.shape, F32)
    return pl.pallas_call(body, name="adam_small", out_shape=[shp, shp, shp])(w, g, m, v)


def _position():
    return lax.axis_index("x"), lax.axis_index("y"), lax.axis_index("c")


def _slot(p):
    return 4 * p[0] + 2 * p[1] + p[2]


def _to_aligned_rows(w):
    zeros = jnp.zeros((D_ZP - D_IN,) + w.shape[1:], w.dtype)
    return jnp.concatenate([w[:GA_ORIG], w[GA_ORIG + GLA_RANK:], w[GA_ORIG:GA_ORIG + GLA_RANK], zeros], axis=0)


def _from_aligned_rows(w):
    return jnp.concatenate([w[:GA_ORIG], w[OFF_GA:OFF_GA + GLA_RANK], w[GA_ORIG:OFF_GA]], axis=0)


def _peer(pos, k):
    x, y, c = pos
    return (1 - x if k & 4 else x, 1 - y if k & 2 else y, 1 - c if k & 1 else c)


HBM_SPEC = pl.BlockSpec(memory_space=pltpu.HBM)
SEM_SPEC = pl.BlockSpec(memory_space=pltpu.SEMAPHORE)
GATHER_PEERS = (1, 4, 2, 6)
ALL_PEERS = (1, 2, 3, 4, 5, 6, 7)


def _hbm(a):
    return pltpu.with_memory_space_constraint(a, pltpu.HBM)


def _split_copies(src_ref, land_ref, send_sems, recv_sems, ks, per_peer, landed):
    me = _position()
    out = []
    for i, k in enumerate(ks):
        peer = _peer(me, k)
        src = src_ref.at[_slot(peer)] if per_peer else src_ref
        dst = land_ref.at[_slot(peer) if landed else _slot(me)]
        out.append(pltpu.make_async_remote_copy(
            src_ref=src, dst_ref=dst, send_sem=send_sems.at[i], recv_sem=recv_sems.at[i],
            device_id=peer, device_id_type=MESH))
    return out


def _exchange_start(src, after, ks, per_peer, name):
    slab = src.shape[1:] if per_peer else src.shape
    land_shape = (N_DEV,) + tuple(slab)
    n = len(ks)

    def body(src_ref, land_ref, after_ref, send_sems, recv_sems, src_thru, land_thru, token):
        for cp in _split_copies(src_ref, land_ref, send_sems, recv_sems, ks, per_peer, landed=False):
            cp.start()
        token[...] = jnp.zeros_like(token)

    return pl.pallas_call(
        body, name=name,
        out_shape=(pltpu.SemaphoreType.DMA((n,)), pltpu.SemaphoreType.DMA((n,)),
                   pltpu.HBM(src.shape, src.dtype), pltpu.HBM(land_shape, src.dtype),
                   jax.ShapeDtypeStruct((8, LANE), F32)),
        in_specs=(HBM_SPEC, HBM_SPEC, ANY),
        out_specs=(SEM_SPEC, SEM_SPEC, HBM_SPEC, HBM_SPEC, pl.BlockSpec(memory_space=pltpu.VMEM)),
        input_output_aliases={0: 2, 1: 3},
        compiler_params=pltpu.CompilerParams(has_side_effects=pltpu.SideEffectType.DATAFLOW_SIDE_EFFECTING),
    )(_hbm(src), _hbm(lax.empty(land_shape, src.dtype)), after)


def _exchange_wait(started, after, ks, per_peer, name):
    send_sems, recv_sems, src_thru, land_thru = started

    def body(src_ref, land_ref, send_sems, recv_sems, after_ref, src_dead, land_out):
        for cp in _split_copies(src_ref, land_ref, send_sems, recv_sems, ks, per_peer, landed=True):
            cp.wait_send()
            cp.wait_recv()

    return pl.pallas_call(
        body, name=name,
        out_shape=(pltpu.HBM(src_thru.shape, src_thru.dtype), pltpu.HBM(land_thru.shape, land_thru.dtype)),
        in_specs=(HBM_SPEC, HBM_SPEC, SEM_SPEC, SEM_SPEC, ANY), out_specs=(HBM_SPEC, HBM_SPEC),
        input_output_aliases={0: 0, 1: 1},
        compiler_params=pltpu.CompilerParams(has_side_effects=pltpu.SideEffectType.DATAFLOW_SIDE_EFFECTING),
    )(src_thru, land_thru, send_sems, recv_sems, after)


def _relay_copies(land_ref, send_sems, recv_sems, landed):
    me = _position()
    sibling = _peer(me, 1)
    out = []
    for i, k in enumerate(GATHER_PEERS[1:]):
        blk = land_ref.at[_slot(_peer(sibling if landed else me, k))]
        out.append(pltpu.make_async_remote_copy(
            src_ref=blk, dst_ref=blk, send_sem=send_sems.at[i], recv_sem=recv_sems.at[i],
            device_id=sibling, device_id_type=MESH))
    return out


def _relay_start(land, name):
    n = len(GATHER_PEERS) - 1

    def body(land_ref, send_sems, recv_sems, land_thru, token):
        for cp in _relay_copies(land_ref, send_sems, recv_sems, landed=False):
            cp.start()
        token[...] = jnp.zeros_like(token)

    return pl.pallas_call(
        body, name=name,
        out_shape=(pltpu.SemaphoreType.DMA((n,)), pltpu.SemaphoreType.DMA((n,)),
                   pltpu.HBM(land.shape, land.dtype), jax.ShapeDtypeStruct((8, LANE), F32)),
        in_specs=(HBM_SPEC,),
        out_specs=(SEM_SPEC, SEM_SPEC, HBM_SPEC, pl.BlockSpec(memory_space=pltpu.VMEM)),
        input_output_aliases={0: 2},
        compiler_params=pltpu.CompilerParams(has_side_effects=pltpu.SideEffectType.DATAFLOW_SIDE_EFFECTING),
    )(_hbm(land))


def _relay_wait(started, after, name):
    send_sems, recv_sems, land_thru = started

    def body(land_ref, send_sems, recv_sems, after_ref, land_out):
        for cp in _relay_copies(land_ref, send_sems, recv_sems, landed=True):
            cp.wait_send()
            cp.wait_recv()

    return pl.pallas_call(
        body, name=name,
        out_shape=pltpu.HBM(land_thru.shape, land_thru.dtype),
        in_specs=(HBM_SPEC, SEM_SPEC, SEM_SPEC, ANY), out_specs=HBM_SPEC,
        input_output_aliases={0: 0},
        compiler_params=pltpu.CompilerParams(has_side_effects=pltpu.SideEffectType.DATAFLOW_SIDE_EFFECTING),
    )(land_thru, send_sems, recv_sems, after)


def _exchange(arrs, name):
    n = len(arrs)

    def body(*refs):
        ins, outs = refs[:n], refs[n:2 * n]
        send_sems, recv_sems, local_sems = refs[2 * n:]
        me = _position()

        def copy(a, k):
            peer = _peer(me, k)
            return pltpu.make_async_remote_copy(
                src_ref=ins[a].at[_slot(peer)], dst_ref=outs[a].at[_slot(me)],
                send_sem=send_sems.at[a * 7 + k - 1], recv_sem=recv_sems.at[a * 7 + k - 1],
                device_id=peer, device_id_type=MESH)

        def landed(a, k):
            peer = _peer(me, k)
            return pltpu.make_async_remote_copy(
                src_ref=ins[a].at[_slot(peer)], dst_ref=outs[a].at[_slot(peer)],
                send_sem=send_sems.at[a * 7 + k - 1], recv_sem=recv_sems.at[a * 7 + k - 1],
                device_id=peer, device_id_type=MESH)

        mine = [pltpu.make_async_copy(ins[a].at[_slot(me)], outs[a].at[_slot(me)], local_sems.at[a])
                for a in range(n)]
        for cp in mine:
            cp.start()
        sent = [copy(a, k) for k in range(1, N_DEV) for a in range(n)]
        for cp in sent:
            cp.start()
        for k in range(1, N_DEV):
            for a in range(n):
                landed(a, k).wait_recv()
        for cp in sent:
            cp.wait_send()
        for cp in mine:
            cp.wait()

    return pl.pallas_call(
        body, name=name,
        in_specs=[ANY] * n, out_specs=[ANY] * n,
        out_shape=[jax.ShapeDtypeStruct(a.shape, a.dtype) for a in arrs],
        scratch_shapes=[pltpu.SemaphoreType.DMA((7 * n,)), pltpu.SemaphoreType.DMA((7 * n,)),
                        pltpu.SemaphoreType.DMA((n,))],
    )(*arrs)


def _sum_slots(parts):
    def body(p_ref, o_ref):
        acc = p_ref[0]
        for dev in range(1, N_DEV):
            acc = acc + p_ref[dev]
        o_ref[...] = acc

    return pl.pallas_call(body, name="sum_slots",
                          out_shape=jax.ShapeDtypeStruct(parts.shape[1:], F32))(parts)


def _pack(arrs):
    flat = jnp.concatenate([a.reshape(-1) for a in arrs])
    pad = (-flat.shape[0]) % (8 * LANE)
    return jnp.pad(flat, (0, pad)).reshape(-1, LANE)


def _unpack(packed, shapes):
    flat = packed.reshape(-1)
    out, at = [], 0
    for shp in shapes:
        size = 1
        for dim in shp:
            size *= dim
        out.append(flat[at:at + size].reshape(shp))
        at += size
    return out


def _layer_fwd(x, wt, wo, g_pre, g_post, wa_pad, b_alpha, g_gla, g_att, rb_pad, midway=None):
    h = _rms_fwd(x, g_pre)
    z = _matmul(h, wt, "nt", F32, 512, D_ZP // 3, 512, "in_proj")
    y_gla, o_gla, states = _gla_fwd(z, wa_pad, b_alpha, g_gla)
    if midway is not None:
        g_att = g_att + midway(y_gla)[:1, :1]
    y_att, o_att = _att_fwd(z, rb_pad, g_att)
    ycat = jnp.concatenate([y_gla, y_att], axis=1)
    y = _matmul(ycat, wo, "nn", F32, 512, 1024, 512, "out_proj")
    out = _post_fwd(x, y, g_post)
    return out, (x, h, z, o_gla, states, o_att, ycat, y)


def _layer_bwd(dout, saved, wt, wo, g_pre, g_post, wa_pad, b_alpha, g_gla, g_att, rb_pad):
    x, h, z, o_gla, states, o_att, ycat, y = saved
    dy, dg_post = _post_bwd(dout, y, g_post)
    dycat = _matmul(dy, wo, "nt", F32, 512, 1024, 512, "out_proj_dx")
    dwo = _matmul(ycat, dy, "tn", BF16, 512, 1024, 512, "out_proj_dw")
    dq, dk, dv, dgg, dga, dwa, db, dg_gla = _gla_bwd(dycat, o_gla, z, wa_pad, b_alpha, g_gla, states)
    daq, dak, dav, dag, drb, dg_att = _att_bwd(dycat, o_att, z, rb_pad, g_att)
    dz = jnp.concatenate([dq, dk, dv, dgg, daq, dak.astype(BF16), dav.astype(BF16), dag, dga], axis=1)
    dh = _matmul(dz, wt, "nn", F32, 512, 1024, D_ZP // 3, "in_proj_dx")
    dwt = _matmul(dz, h, "tn", BF16, D_ZP // 3, 512, 512, "in_proj_dw")
    dx, dg_pre = _pre_bwd(dh, x, g_pre, dout)
    small = (dg_pre[0], dg_post[0], dwa[:GLA_RANK], db[0], dg_gla[0], dg_att[0], drb[:, 0, :N_REL])
    return dx, dwt, dwo, small


def kernel(x, w_in, w_out, g_pre, g_post, w_alpha, b_alpha, g_gla, g_att, rel_bias, loss_target, m_w_in, m_w_out, m_g_pre, m_g_post, m_w_alpha, m_b_alpha, m_g_gla, m_g_att, m_rel_bias, v_w_in, v_w_out, v_g_pre, v_g_post, v_w_alpha, v_b_alpha, v_g_gla, v_g_att, v_rel_bias):
    nl, d, cols = w_in.shape
    rows = w_out.shape[1]
    s = x.shape[1]
    x0 = x.reshape(s, d)
    tgt = loss_target.reshape(s, d)

    cols_first = lambda a: jnp.transpose(a, (2, 0, 1))
    w_c = cols_first(w_in)
    shards = [jnp.concatenate([w_out[l].astype(BF16), w_c[:, l].astype(BF16)], axis=0) for l in range(nl)]
    wa_g = _exchange([jnp.broadcast_to(_pack([w_alpha])[None], (N_DEV,) + _pack([w_alpha]).shape)], "gather_alpha")[0]
    wa_cols = w_alpha.shape[2]
    wa_full = wa_g.reshape(N_DEV, -1)[:, :nl * GLA_RANK * wa_cols].reshape(N_DEV, nl, GLA_RANK, wa_cols)
    wa_full = jnp.transpose(wa_full, (1, 2, 0, 3)).reshape(nl, GLA_RANK, GLA_KW)
    wa_pad = jnp.pad(wa_full, ((0, 0), (0, LANE - GLA_RANK), (0, 0)))
    rb_pad = jnp.pad(rel_bias, ((0, 0), (0, 0), (0, 3 * LANE - N_REL)))

    def layer_args(l, follows_pre=None, follows_post=None):
        gp = g_pre[l:l + 1] if follows_pre is None else g_pre[l:l + 1] + follows_pre[:1, :1]
        gq = g_post[l:l + 1] if follows_post is None else g_post[l:l + 1] + follows_post[:1, :1]
        return (wts[l], wos[l], gp, gq, wa_pad[l], b_alpha[l:l + 1], g_gla[l:l + 1], g_att[l:l + 1], rb_pad[l])

    my = _slot(_position())

    def fetch(l, after):
        return _exchange_start(shards[l], after, GATHER_PEERS, False, f"gather_start_{l}")

    def relay(l, first_hop, after):
        _, land = _exchange_wait(first_hop[:4], after, GATHER_PEERS, False, f"gather_wait_{l}")
        return _relay_start(land, f"relay_start_{l}")

    def midway(l, y):
        flight["relay"] = relay(l + 1, flight["fetch"], y)
        if l + 2 >= nl:
            return flight["relay"][3]
        flight["fetch"] = fetch(l + 2, flight["relay"][2])
        return flight["fetch"][4]

    act, saved, wts, wos, flight = x0, [], [], [], {}
    flight["fetch"] = fetch(0, x0)
    flight["relay"] = relay(0, flight["fetch"], x0)
    if nl > 1:
        flight["fetch"] = fetch(1, flight["relay"][2])
    for l in range(nl):
        land = _relay_wait(flight["relay"][:3], act, f"relay_wait_{l}")
        land = lax.dynamic_update_slice_in_dim(land, shards[l][None], my, 0)
        wos.append(land[:, :rows].reshape(N_DEV * rows, d))
        wts.append(_to_aligned_rows(land[:, rows:].reshape(N_DEV * cols, d)))
        act, sv = _layer_fwd(act, *layer_args(l),
                             midway=functools.partial(midway, l) if l + 1 < nl else None)
        saved.append(sv)
    dout, sq = _loss_head(act, tgt)
    loss = lax.psum(sq[0, 0] * (0.5 / d), ("x", "y", "c"))

    smalls, pending, token = [None] * nl, [None] * nl, None
    for l in reversed(range(nl)):
        dout, dwt, dwo, smalls[l] = _layer_bwd(dout, saved[l], *layer_args(l, follows_post=token))
        partial = jnp.concatenate([dwo.reshape(N_DEV, rows, d), _from_aligned_rows(dwt).reshape(N_DEV, cols, d)],
                                  axis=1)
        pending[l] = _exchange_start(partial, dout, ALL_PEERS, True, f"scatter_start_{l}")
        token = pending[l][4]
    grad_x = dout.reshape(x.shape)

    parts = []
    for l in range(nl):
        partial, land = _exchange_wait(pending[l][:4], dout, ALL_PEERS, True, f"scatter_wait_{l}")
        parts.append(lax.dynamic_update_slice_in_dim(land, lax.dynamic_slice_in_dim(partial, my, 1, 0), my, 0))
    g_w_in, d_w_in, m2_w_in, v2_w_in = [
        jnp.transpose(a, (1, 2, 0))
        for a in _adam_columns(parts, rows // ADAM_COL_ROWS, w_c, cols_first(m_w_in), cols_first(v_w_in))]
    g_w_out, d_w_out, m2_w_out, v2_w_out = _adam_sharded(parts, 0, w_out, m_w_out, v_w_out, "adam_w_out")

    names = 7
    small_stacked = [jnp.stack([smalls[l][i] for l in range(nl)]) for i in range(names)]
    shapes = [a.shape for a in small_stacked]
    packed = _pack(small_stacked)
    gathered = _exchange([jnp.broadcast_to(packed[None], (N_DEV,) + packed.shape)], "gather_small_grads")[0]
    g_pre_g, g_post_g, wa_g_full, b_g, gla_g, att_g, rb_g = _unpack(_sum_slots(gathered), shapes)
    wa_g_mine = lax.dynamic_slice_in_dim(wa_g_full, my * wa_cols, wa_cols, axis=2)
    grads = [g_pre_g, g_post_g, wa_g_mine, b_g, gla_g, att_g, rb_g]
    ws = [g_pre, g_post, w_alpha, b_alpha, g_gla, g_att, rel_bias]
    ms = [m_g_pre, m_g_post, m_w_alpha, m_b_alpha, m_g_gla, m_g_att, m_rel_bias]
    vs = [v_g_pre, v_g_post, v_w_alpha, v_b_alpha, v_g_gla, v_g_att, v_rel_bias]
    shapes2 = [a.shape for a in ws]
    d_s, m2_s, v2_s = _adam_small(_pack(ws), _pack(grads), _pack(ms), _pack(vs))
    d_s, m2_s, v2_s = _unpack(d_s, shapes2), _unpack(m2_s, shapes2), _unpack(v2_s, shapes2)

    def ordered(big_in, big_out, small):
        return [big_in, big_out] + list(small)

    return (loss, grad_x,
            *ordered(g_w_in, g_w_out, grads),
            *ordered(d_w_in, d_w_out, d_s),
            *ordered(m2_w_in, m2_w_out, m2_s),
            *ordered(v2_w_in, v2_w_out, v2_s))
```

```python
import functools

import jax
import jax.numpy as jnp
from jax import lax
from jax.experimental import pallas as pl
from jax.experimental.pallas import tpu as pltpu

F32 = jnp.float32
BF16 = jnp.bfloat16
MESH = pl.DeviceIdType.MESH
ANY = pl.BlockSpec(memory_space=pl.ANY)

CHUNK = 64
GLA_HEADS = 4
GLA_DK = 128
GLA_DV = 256
GLA_KW = GLA_HEADS * GLA_DK
D_GLA = GLA_HEADS * GLA_DV
GLA_RANK = 16
GLA_TAU = 16.0
ATT_HEADS = 8
ATT_HD = 128
D_ATT = ATT_HEADS * ATT_HD
LEFT_CHUNKS = 8
REL_CLIP = 128
N_REL = 2 * REL_CLIP + 1
EPS = 1e-6
D_IN = 2 * GLA_KW + 2 * D_GLA + GLA_RANK + 4 * D_ATT
GLA_SCALE = GLA_DK ** -0.5
ATT_SCALE = ATT_HD ** -0.5

ADAM_LR = 0.001
ADAM_B1 = 0.9
ADAM_B2 = 0.999
ADAM_EPS = 1e-08
ADAM_WD = 0.01
ADAM_STEP = 10

N_DEV = 8
LANE = 128
GA_ORIG = 2 * GLA_KW + 2 * D_GLA
OFF_AQ = GA_ORIG
OFF_GA = GA_ORIG + 4 * D_ATT
D_ZP = OFF_GA + LANE
QB = 2 * CHUNK
WIN = (LEFT_CHUNKS + 2) * CHUNK
ET_ROWS = WIN + LEFT_CHUNKS * CHUNK
NEG = -1e30
VMEM_LIMIT = 48 * 1024 * 1024


def _cparams(sem):
    return pltpu.CompilerParams(dimension_semantics=sem, vmem_limit_bytes=VMEM_LIMIT)


def _dot(a, b):
    return jnp.dot(a, b, preferred_element_type=F32)


def _dot_nt(a, b):
    return lax.dot_general(a, b, (((1,), (1,)), ((), ())), preferred_element_type=F32)


def _dot_tn(a, b):
    return lax.dot_general(a, b, (((0,), (0,)), ((), ())), preferred_element_type=F32)


def _dot01(t, x, left=True):
    if not left:
        t, x = x, t
    hi = x.astype(BF16)
    r = x - hi.astype(F32)
    mid = r.astype(BF16)
    lo = (r - mid.astype(F32)).astype(BF16)
    if left:
        return _dot(t, hi) + _dot(t, mid) + _dot(t, lo)
    return _dot(hi, t) + _dot(mid, t) + _dot(lo, t)


def _sigmoid(x):
    return 1.0 / (1.0 + jnp.exp(-x))


def _log_sigmoid(x):
    return jnp.minimum(x, 0.0) - jnp.log(1.0 + jnp.exp(-jnp.abs(x)))


def _matmul(a, b, mode, out_dtype, tm, tn, tk, name, b_layer=None):
    bshape = b.shape if b_layer is None else b.shape[1:]
    if mode == "nn":
        (m, k), n = a.shape, bshape[1]
    elif mode == "nt":
        (m, k), n = a.shape, bshape[0]
    else:
        (k, m), n = a.shape, bshape[1]
    tm, tn, tk = min(tm, m), min(tn, n), min(tk, k)
    assert m % tm == 0 and n % tn == 0 and k % tk == 0, (name, m, n, k)
    nk = k // tk

    def body(a_ref, b_ref, o_ref, acc_ref):
        kk = pl.program_id(2)

        @pl.when(kk == 0)
        def _():
            acc_ref[...] = jnp.zeros_like(acc_ref)

        if mode == "nn":
            acc_ref[...] += _dot(a_ref[...], b_ref[...])
        elif mode == "nt":
            acc_ref[...] += _dot_nt(a_ref[...], b_ref[...])
        else:
            acc_ref[...] += _dot_tn(a_ref[...], b_ref[...])

        @pl.when(kk == nk - 1)
        def _():
            o_ref[...] = acc_ref[...].astype(out_dtype)

    if mode == "tn":
        a_spec = pl.BlockSpec((tk, tm), lambda i, j, kk: (kk, i))
    else:
        a_spec = pl.BlockSpec((tm, tk), lambda i, j, kk: (i, kk))
    b_blk, b_idx = ((tn, tk), lambda i, j, kk: (j, kk)) if mode == "nt" else ((tk, tn), lambda i, j, kk: (kk, j))
    if b_layer is None:
        b_spec = pl.BlockSpec(b_blk, b_idx)
    else:
        b_spec = pl.BlockSpec((None,) + b_blk, lambda i, j, kk: (b_layer,) + b_idx(i, j, kk))
    return pl.pallas_call(
        body, name=name,
        grid=(m // tm, n // tn, nk),
        in_specs=[a_spec, b_spec],
        out_specs=pl.BlockSpec((tm, tn), lambda i, j, kk: (i, j)),
        out_shape=jax.ShapeDtypeStruct((m, n), out_dtype),
        scratch_shapes=[pltpu.VMEM((tm, tn), F32)],
        compiler_params=_cparams(("parallel", "parallel", "arbitrary")),
    )(a, b)


ROWS = 256


def _rms_fwd(x, g):
    s, d = x.shape

    def body(x_ref, g_ref, h_ref):
        xv = x_ref[...]
        r = lax.rsqrt(jnp.mean(xv * xv, axis=-1, keepdims=True) + EPS)
        h_ref[...] = (xv * r * g_ref[...]).astype(BF16)

    return pl.pallas_call(
        body, name="rms_fwd", grid=(s // ROWS,),
        in_specs=[pl.BlockSpec((ROWS, d), lambda i: (i, 0)), pl.BlockSpec((1, d), lambda i: (0, 0))],
        out_specs=pl.BlockSpec((ROWS, d), lambda i: (i, 0)),
        out_shape=jax.ShapeDtypeStruct((s, d), BF16),
        compiler_params=_cparams(("parallel",)),
    )(x, g)


def _post_fwd(x, y, g):
    s, d = x.shape

    def body(x_ref, y_ref, g_ref, o_ref):
        yv = y_ref[...]
        r = lax.rsqrt(jnp.mean(yv * yv, axis=-1, keepdims=True) + EPS)
        o_ref[...] = x_ref[...] + yv * r * g_ref[...]

    row = pl.BlockSpec((ROWS, d), lambda i: (i, 0))
    return pl.pallas_call(
        body, name="post_fwd", grid=(s // ROWS,),
        in_specs=[row, row, pl.BlockSpec((1, d), lambda i: (0, 0))],
        out_specs=row,
        out_shape=jax.ShapeDtypeStruct((s, d), F32),
        compiler_params=_cparams(("parallel",)),
    )(x, y, g)


def _loss_head(out, tgt):
    s, d = out.shape

    def body(o_ref, t_ref, dout_ref, sum_ref):
        @pl.when(pl.program_id(0) == 0)
        def _():
            sum_ref[...] = jnp.zeros_like(sum_ref)

        e = o_ref[...] - t_ref[...]
        dout_ref[...] = e * (1.0 / d)
        sum_ref[...] += jnp.sum(jnp.sum(e * e, axis=1, keepdims=True), axis=0, keepdims=True)

    row = pl.BlockSpec((ROWS, d), lambda i: (i, 0))
    return pl.pallas_call(
        body, name="loss_head", grid=(s // ROWS,),
        in_specs=[row, row],
        out_specs=[row, pl.BlockSpec((1, 1), lambda i: (0, 0))],
        out_shape=[jax.ShapeDtypeStruct((s, d), F32), jax.ShapeDtypeStruct((1, 1), F32)],
        compiler_params=_cparams(("arbitrary",)),
    )(out, tgt)


def _post_bwd(dout, y, g):
    s, d = y.shape

    def body(do_ref, y_ref, g_ref, dy_ref, dg_ref):
        @pl.when(pl.program_id(0) == 0)
        def _():
            dg_ref[...] = jnp.zeros_like(dg_ref)

        yv = y_ref[...]
        dv = do_ref[...]
        r = lax.rsqrt(jnp.mean(yv * yv, axis=-1, keepdims=True) + EPS)
        dg_ref[...] += jnp.sum(dv * yv * r, axis=0, keepdims=True)
        w = dv * g_ref[...]
        dy = r * (w - yv * (r * r) * jnp.mean(w * yv, axis=-1, keepdims=True))
        dy_ref[...] = dy.astype(BF16)

    row = pl.BlockSpec((ROWS, d), lambda i: (i, 0))
    vec = pl.BlockSpec((1, d), lambda i: (0, 0))
    return pl.pallas_call(
        body, name="post_bwd", grid=(s // ROWS,),
        in_specs=[row, row, vec],
        out_specs=[row, vec],
        out_shape=[jax.ShapeDtypeStruct((s, d), BF16), jax.ShapeDtypeStruct((1, d), F32)],
        compiler_params=_cparams(("arbitrary",)),
    )(dout, y, g)


def _pre_bwd(dh, x, g, dout):
    s, d = x.shape

    def body(dh_ref, x_ref, g_ref, do_ref, dx_ref, dg_ref):
        @pl.when(pl.program_id(0) == 0)
        def _():
            dg_ref[...] = jnp.zeros_like(dg_ref)

        xv = x_ref[...]
        dv = dh_ref[...]
        r = lax.rsqrt(jnp.mean(xv * xv, axis=-1, keepdims=True) + EPS)
        dg_ref[...] += jnp.sum(dv * xv * r, axis=0, keepdims=True)
        w = dv * g_ref[...]
        dx_ref[...] = do_ref[...] + r * (w - xv * (r * r) * jnp.mean(w * xv, axis=-1, keepdims=True))

    row = pl.BlockSpec((ROWS, d), lambda i: (i, 0))
    vec = pl.BlockSpec((1, d), lambda i: (0, 0))
    return pl.pallas_call(
        body, name="pre_bwd", grid=(s // ROWS,),
        in_specs=[row, row, vec, row],
        out_specs=[row, vec],
        out_shape=[jax.ShapeDtypeStruct((s, d), F32), jax.ShapeDtypeStruct((1, d), F32)],
        compiler_params=_cparams(("arbitrary",)),
    )(dh, x, g, dout)


def _gla_gate(ga_b, wa_ref, b_ref, cs, tri):
    pre = _dot(ga_b, wa_ref[:, cs].astype(BF16)) + b_ref[:, cs]
    la = _log_sigmoid(pre) * (1.0 / GLA_TAU)
    cum = _dot01(tri, la)
    last = lax.broadcasted_iota(jnp.int32, cum.shape, 0) == CHUNK - 1
    return pre, cum, jnp.sum(jnp.where(last, cum, 0.0), axis=0, keepdims=True)


def _z_specs_gla(rev=None):
    idx = (lambda n: n) if rev is None else rev
    return [
        pl.BlockSpec((CHUNK, GLA_KW), lambda n: (idx(n), 0)),
        pl.BlockSpec((CHUNK, GLA_KW), lambda n: (idx(n), 1)),
        pl.BlockSpec((CHUNK, D_GLA), lambda n: (idx(n), 1)),
        pl.BlockSpec((CHUNK, D_GLA), lambda n: (idx(n), 2)),
        pl.BlockSpec((CHUNK, LANE), lambda n: (idx(n), OFF_GA // LANE)),
    ]


def _gla_fwd(z, wa_pad, b_alpha, g_gla):
    s = z.shape[0]
    nchunk = s // CHUNK

    def body(q_ref, k_ref, v_ref, gg_ref, ga_ref, wa_ref, b_ref, g_ref, y_ref, o_ref, st_ref, state):
        @pl.when(pl.program_id(0) == 0)
        def _():
            state[...] = jnp.zeros_like(state)

        ga_b = ga_ref[...].astype(BF16)
        ri = lax.broadcasted_iota(jnp.int32, (CHUNK, CHUNK), 0)
        ci = lax.broadcasted_iota(jnp.int32, (CHUNK, CHUNK), 1)
        tri = jnp.where(ri >= ci, 1.0, 0.0).astype(BF16)
        for h in range(GLA_HEADS):
            cs = slice(h * GLA_DK, (h + 1) * GLA_DK)
            vs = slice(h * GLA_DV, (h + 1) * GLA_DV)
            _, cum, cend = _gla_gate(ga_b, wa_ref, b_ref, cs, tri)
            kd = k_ref[:, cs] * jnp.exp(cend - cum)
            st = state[h] * jnp.exp(cend) + _dot_tn(v_ref[:, vs].astype(BF16), kd.astype(BF16))
            state[h] = st
            st_ref[0, h] = st
            qs = (q_ref[:, cs] * GLA_SCALE).astype(BF16)
            o = _dot_nt(qs, st.astype(BF16))
            o_ref[:, vs] = o
            r = lax.rsqrt(jnp.mean(o * o, axis=-1, keepdims=True) + EPS)
            gg = gg_ref[:, vs]
            y_ref[:, vs] = (o * r * g_ref[:, vs] * (gg * _sigmoid(gg))).astype(BF16)

    full = lambda shape: pl.BlockSpec(shape, lambda n: tuple(0 for _ in shape))
    wide = pl.BlockSpec((CHUNK, D_GLA), lambda n: (n, 0))
    return pl.pallas_call(
        body, name="gla_fwd", grid=(nchunk,),
        in_specs=_z_specs_gla() + [full((LANE, GLA_KW)), full((1, GLA_KW)), full((1, D_GLA))],
        out_specs=[wide, wide, pl.BlockSpec((1, GLA_HEADS, GLA_DV, GLA_DK), lambda n: (n, 0, 0, 0))],
        out_shape=[jax.ShapeDtypeStruct((s, D_GLA), BF16), jax.ShapeDtypeStruct((s, D_GLA), F32),
                   jax.ShapeDtypeStruct((nchunk, GLA_HEADS, GLA_DV, GLA_DK), F32)],
        scratch_shapes=[pltpu.VMEM((GLA_HEADS, GLA_DV, GLA_DK), F32)],
        compiler_params=_cparams(("arbitrary",)),
    )(z, z, z, z, z, wa_pad, b_alpha, g_gla)


def _gla_bwd(dyc, o_gla, z, wa_pad, b_alpha, g_gla, states):
    s = z.shape[0]
    nchunk = s // CHUNK
    rev = lambda n: nchunk - 1 - n

    def body(dy_ref, o_ref, q_ref, k_ref, v_ref, gg_ref, ga_ref, wa_ref, b_ref, g_ref, st_ref, stp_ref,
             dq_ref, dk_ref, dv_ref, dgg_ref, dga_ref, dwa_ref, db_ref, dg_ref, carry):
        step = pl.program_id(0)

        @pl.when(step == 0)
        def _():
            carry[...] = jnp.zeros_like(carry)
            dwa_ref[...] = jnp.zeros_like(dwa_ref)
            db_ref[...] = jnp.zeros_like(db_ref)
            dg_ref[...] = jnp.zeros_like(dg_ref)

        has_prev = (step < nchunk - 1).astype(F32)
        ga_b = ga_ref[...].astype(BF16)
        ri = lax.broadcasted_iota(jnp.int32, (CHUNK, CHUNK), 0)
        ci = lax.broadcasted_iota(jnp.int32, (CHUNK, CHUNK), 1)
        tri = jnp.where(ri >= ci, 1.0, 0.0).astype(BF16)
        tri_up = jnp.where(ci >= ri, 1.0, 0.0).astype(BF16)
        dga = jnp.zeros((CHUNK, LANE), F32)
        for h in range(GLA_HEADS):
            cs = slice(h * GLA_DK, (h + 1) * GLA_DK)
            vs = slice(h * GLA_DV, (h + 1) * GLA_DV)
            pre, cum, cend = _gla_gate(ga_b, wa_ref, b_ref, cs, tri)
            e = jnp.exp(cend - cum)
            a = jnp.exp(cend)
            kf = k_ref[:, cs]
            kd_b = (kf * e).astype(BF16)
            v_b = v_ref[:, vs].astype(BF16)
            qs = (q_ref[:, cs] * GLA_SCALE).astype(BF16)
            o = o_ref[:, vs]
            gg = gg_ref[:, vs]
            g = g_ref[:, vs]
            dy = dy_ref[:, vs]
            r = lax.rsqrt(jnp.mean(o * o, axis=-1, keepdims=True) + EPS)
            sg = _sigmoid(gg)
            dogn = dy * (gg * sg)
            dgg_ref[:, vs] = (dy * (o * r * g) * (sg * (1.0 + gg * (1.0 - sg)))).astype(BF16)
            dg_ref[:, vs] += jnp.sum(dogn * o * r, axis=0, keepdims=True)
            w = dogn * g
            do_b = (r * (w - o * (r * r) * jnp.mean(w * o, axis=-1, keepdims=True))).astype(BF16)
            st = st_ref[0, h]
            dq_ref[:, cs] = (_dot(do_b, st.astype(BF16)) * GLA_SCALE).astype(BF16)
            gt = _dot_tn(do_b, qs) + carry[h]
            gt_b = gt.astype(BF16)
            dkd = _dot(v_b, gt_b)
            dv_ref[:, vs] = _dot_nt(kd_b, gt_b).astype(BF16)
            da = jnp.sum(gt * (stp_ref[0, h] * has_prev), axis=0, keepdims=True)
            carry[h] = gt * a
            dk_ref[:, cs] = (dkd * e).astype(BF16)
            dd = dkd * kf * e
            dcend = jnp.sum(dd, axis=0, keepdims=True) + da * a
            dla = dcend - _dot01(tri_up, dd)
            dpre = dla * (1.0 / GLA_TAU) * (1.0 - _sigmoid(pre))
            dpre_b = dpre.astype(BF16)
            dga = dga + _dot_nt(dpre_b, wa_ref[:, cs].astype(BF16))
            dwa_ref[:, cs] += _dot_tn(ga_b, dpre_b)
            db_ref[:, cs] += jnp.sum(dpre, axis=0, keepdims=True)
        dga_ref[...] = dga.astype(BF16)

    full = lambda shape: pl.BlockSpec(shape, lambda n: tuple(0 for _ in shape))
    wide = pl.BlockSpec((CHUNK, D_GLA), lambda n: (rev(n), 0))
    keyw = pl.BlockSpec((CHUNK, GLA_KW), lambda n: (rev(n), 0))
    st_spec = pl.BlockSpec((1, GLA_HEADS, GLA_DV, GLA_DK), lambda n: (rev(n), 0, 0, 0))
    stp_spec = pl.BlockSpec((1, GLA_HEADS, GLA_DV, GLA_DK), lambda n: (jnp.maximum(rev(n) - 1, 0), 0, 0, 0))
    return pl.pallas_call(
        body, name="gla_bwd", grid=(nchunk,),
        in_specs=[wide, wide] + _z_specs_gla(rev)
        + [full((LANE, GLA_KW)), full((1, GLA_KW)), full((1, D_GLA)), st_spec, stp_spec],
        out_specs=[keyw, keyw, wide, wide, pl.BlockSpec((CHUNK, LANE), lambda n: (rev(n), 0)),
                   full((LANE, GLA_KW)), full((1, GLA_KW)), full((1, D_GLA))],
        out_shape=[jax.ShapeDtypeStruct((s, GLA_KW), BF16), jax.ShapeDtypeStruct((s, GLA_KW), BF16),
                   jax.ShapeDtypeStruct((s, D_GLA), BF16), jax.ShapeDtypeStruct((s, D_GLA), BF16),
                   jax.ShapeDtypeStruct((s, LANE), BF16),
                   jax.ShapeDtypeStruct((LANE, GLA_KW), F32), jax.ShapeDtypeStruct((1, GLA_KW), F32),
                   jax.ShapeDtypeStruct((1, D_GLA), F32)],
        scratch_shapes=[pltpu.VMEM((GLA_HEADS, GLA_DV, GLA_DK), F32)],
        compiler_params=_cparams(("arbitrary",)),
    )(dyc, o_gla, z, z, z, z, z, wa_pad, b_alpha, g_gla, states, states)


def _build_bias_table(rb_row, et_ref):
    far = jnp.broadcast_to(rb_row[:, 2 * REL_CLIP:2 * REL_CLIP + 1], (1, LANE))
    near_hi = rb_row[:, REL_CLIP:2 * REL_CLIP]
    near_lo = rb_row[:, 0:REL_CLIP]
    past = jnp.broadcast_to(rb_row[:, 0:1], (1, LANE))
    seg = [far, far, far, far, near_hi, near_lo] + [past] * (ET_ROWS // LANE - 5)
    ri = lax.broadcasted_iota(jnp.int32, (LANE, LANE), 0)
    ci = lax.broadcasted_iota(jnp.int32, (LANE, LANE), 1)
    for kb in range(ET_ROWS // LANE):
        wmat = jnp.where(ri + ci < LANE, seg[kb], seg[kb + 1])
        blk = pltpu.roll(wmat, 0, 1, stride=1, stride_axis=0)
        lag = LEFT_CHUNKS + ci // CHUNK - (2 * kb + ri // CHUNK)
        et_ref[kb * LANE:(kb + 1) * LANE, :] = jnp.where((lag >= 0) & (lag <= LEFT_CHUNKS), blk, NEG)


def _reduce_bias_table(det_ref):
    lane = lax.broadcasted_iota(jnp.int32, (1, LANE), 1)
    ri = lax.broadcasted_iota(jnp.int32, (LANE, LANE), 0)
    ci = lax.broadcasted_iota(jnp.int32, (LANE, LANE), 1)
    flip = jnp.where(ri + ci == LANE - 1, 1.0, 0.0).astype(BF16)
    segs = jnp.zeros((8, LANE), F32)
    seg_row = lax.broadcasted_iota(jnp.int32, (8, LANE), 0)
    prev_minus = jnp.zeros((1, LANE), F32)
    for kb in range(6):
        rolled = pltpu.roll(_dot01(det_ref[kb * LANE:(kb + 1) * LANE, :], flip, left=False), 0, 1,
                            stride=1, stride_axis=0)
        plus = jnp.sum(jnp.where(ci >= ri, rolled, 0.0), axis=0, keepdims=True)
        minus = jnp.sum(jnp.where(ci < ri, rolled, 0.0), axis=0, keepdims=True)
        segs = segs + jnp.where(seg_row == kb, plus + prev_minus, 0.0)
        prev_minus = minus
    segs = _dot01(segs, flip, left=False)
    pick = lambda kb: jnp.sum(jnp.where(seg_row == kb, segs, 0.0), axis=0, keepdims=True)
    far = jnp.sum(pick(0) + pick(1) + pick(2) + pick(3), axis=1, keepdims=True)
    last = jnp.where(lane == 0, far, 0.0)
    return jnp.concatenate([pick(5), pick(4), last], axis=1)


def _att_window(b):
    c0 = 2 * b
    kstart = pl.multiple_of(jnp.maximum(c0 - LEFT_CHUNKS, 0) * CHUNK, CHUNK)
    eoff = pl.multiple_of(jnp.maximum(LEFT_CHUNKS - c0, 0) * CHUNK, CHUNK)
    return kstart, eoff


def _att_probs(q_b, kw_b, et):
    st = _dot_nt(kw_b, q_b) * ATT_SCALE + et
    m = jnp.max(st, axis=0, keepdims=True)
    ex = jnp.exp(st - m)
    return ex / jnp.sum(ex, axis=0, keepdims=True)


def _att_fwd(z, rb_pad, g_att):
    s = z.shape[0]
    nblk = s // QB
    c_aq, c_ak, c_av, c_ag = [(OFF_AQ + i * D_ATT) // ATT_HD for i in range(4)]

    def body(q_ref, k_ref, v_ref, ag_ref, rb_ref, g_ref, y_ref, o_ref, et_ref):
        h = pl.program_id(0)
        b = pl.program_id(1)

        @pl.when(b == 0)
        def _():
            _build_bias_table(rb_ref[pl.ds(h, 1), :], et_ref)

        kstart, eoff = _att_window(b)
        q_b = q_ref[...].astype(BF16)
        kw_b = k_ref[pl.ds(kstart, WIN), :].astype(BF16)
        vw_b = v_ref[pl.ds(kstart, WIN), :].astype(BF16)
        pt = _att_probs(q_b, kw_b, et_ref[pl.ds(eoff, WIN), :])
        o = _dot_tn(pt.astype(BF16), vw_b)
        o_ref[...] = o
        r = lax.rsqrt(jnp.mean(o * o, axis=-1, keepdims=True) + EPS)
        ag = ag_ref[...]
        y_ref[...] = (o * r * g_ref[...] * (ag * _sigmoid(ag))).astype(BF16)

    blk = lambda col: pl.BlockSpec((QB, ATT_HD), lambda h, b: (b, col + h))
    seq = lambda col: pl.BlockSpec((s, ATT_HD), lambda h, b: (0, col + h))
    out_blk = pl.BlockSpec((QB, ATT_HD), lambda h, b: (b, h))
    return pl.pallas_call(
        body, name="att_fwd", grid=(ATT_HEADS, nblk),
        in_specs=[blk(c_aq), seq(c_ak), seq(c_av), blk(c_ag),
                  pl.BlockSpec((ATT_HEADS, 3 * LANE), lambda h, b: (0, 0)),
                  pl.BlockSpec((1, ATT_HD), lambda h, b: (0, h))],
        out_specs=[out_blk, out_blk],
        out_shape=[jax.ShapeDtypeStruct((s, D_ATT), BF16), jax.ShapeDtypeStruct((s, D_ATT), F32)],
        scratch_shapes=[pltpu.VMEM((ET_ROWS, LANE), F32)],
        compiler_params=_cparams(("arbitrary", "arbitrary")),
    )(z, z, z, z, rb_pad, g_att)


def _att_bwd(dyc, o_att, z, rb_pad, g_att):
    s = z.shape[0]
    nblk = s // QB
    c_aq, c_ak, c_av, c_ag = [(OFF_AQ + i * D_ATT) // ATT_HD for i in range(4)]
    c_dy = D_GLA // ATT_HD

    def body(dy_ref, o_ref, q_ref, k_ref, v_ref, ag_ref, rb_ref, g_ref,
             dq_ref, dk_ref, dv_ref, dag_ref, drb_ref, dg_ref, et_ref, det_ref):
        h = pl.program_id(0)
        b = pl.program_id(1)

        @pl.when(b == 0)
        def _():
            _build_bias_table(rb_ref[pl.ds(h, 1), :], et_ref)
            det_ref[...] = jnp.zeros_like(det_ref)
            dk_ref[...] = jnp.zeros_like(dk_ref)
            dv_ref[...] = jnp.zeros_like(dv_ref)
            dg_ref[...] = jnp.zeros_like(dg_ref)

        kstart, eoff = _att_window(b)
        q_b = q_ref[...].astype(BF16)
        kw_b = k_ref[pl.ds(kstart, WIN), :].astype(BF16)
        vw_b = v_ref[pl.ds(kstart, WIN), :].astype(BF16)
        pt = _att_probs(q_b, kw_b, et_ref[pl.ds(eoff, WIN), :])
        o = o_ref[...]
        ag = ag_ref[...]
        g = g_ref[...]
        dy = dy_ref[...]
        r = lax.rsqrt(jnp.mean(o * o, axis=-1, keepdims=True) + EPS)
        sg = _sigmoid(ag)
        don = dy * (ag * sg)
        dag_ref[...] = (dy * (o * r * g) * (sg * (1.0 + ag * (1.0 - sg)))).astype(BF16)
        dg_ref[...] += jnp.sum(don * o * r, axis=0, keepdims=True)
        w = don * g
        do_b = (r * (w - o * (r * r) * jnp.mean(w * o, axis=-1, keepdims=True))).astype(BF16)
        pt_b = pt.astype(BF16)
        dpt = _dot_nt(vw_b, do_b)
        dst = pt * (dpt - jnp.sum(dpt * pt, axis=0, keepdims=True))
        det_ref[pl.ds(eoff, WIN), :] += dst
        ds_b = (dst * ATT_SCALE).astype(BF16)
        dq_ref[...] = _dot_tn(ds_b, kw_b).astype(BF16)
        dk_ref[pl.ds(kstart, WIN), :] += _dot(ds_b, q_b)
        dv_ref[pl.ds(kstart, WIN), :] += _dot(pt_b, do_b)

        @pl.when(b == nblk - 1)
        def _():
            drb_ref[0] = jnp.broadcast_to(_reduce_bias_table(det_ref), (8, 3 * LANE))

    blk = lambda col: pl.BlockSpec((QB, ATT_HD), lambda h, b: (b, col + h))
    seq = lambda col: pl.BlockSpec((s, ATT_HD), lambda h, b: (0, col + h))
    out_blk = pl.BlockSpec((QB, ATT_HD), lambda h, b: (b, h))
    out_seq = pl.BlockSpec((s, ATT_HD), lambda h, b: (0, h))
    return pl.pallas_call(
        body, name="att_bwd", grid=(ATT_HEADS, nblk),
        in_specs=[blk(c_dy), blk(0), blk(c_aq), seq(c_ak), seq(c_av), blk(c_ag),
                  pl.BlockSpec((ATT_HEADS, 3 * LANE), lambda h, b: (0, 0)),
                  pl.BlockSpec((1, ATT_HD), lambda h, b: (0, h))],
        out_specs=[out_blk, out_seq, out_seq, out_blk,
                   pl.BlockSpec((1, 8, 3 * LANE), lambda h, b: (h, 0, 0)),
                   pl.BlockSpec((1, ATT_HD), lambda h, b: (0, h))],
        out_shape=[jax.ShapeDtypeStruct((s, D_ATT), BF16), jax.ShapeDtypeStruct((s, D_ATT), F32),
                   jax.ShapeDtypeStruct((s, D_ATT), F32), jax.ShapeDtypeStruct((s, D_ATT), BF16),
                   jax.ShapeDtypeStruct((ATT_HEADS, 8, 3 * LANE), F32),
                   jax.ShapeDtypeStruct((1, D_ATT), F32)],
        scratch_shapes=[pltpu.VMEM((ET_ROWS, LANE), F32), pltpu.VMEM((ET_ROWS, LANE), F32)],
        compiler_params=_cparams(("arbitrary", "arbitrary")),
    )(dyc, o_att, z, z, z, z, rb_pad, g_att)


ADAM_ROWS = 64
ADAM_COL_ROWS = 32


def _adam_math(w, g, m, v):
    m2 = ADAM_B1 * m + (1.0 - ADAM_B1) * g
    v2 = ADAM_B2 * v + (1.0 - ADAM_B2) * (g * g)
    m_hat = m2 / (1.0 - ADAM_B1 ** ADAM_STEP)
    v_hat = v2 / (1.0 - ADAM_B2 ** ADAM_STEP)
    delta = -ADAM_LR * (m_hat / (jnp.sqrt(v_hat) + ADAM_EPS) + ADAM_WD * w)
    return delta, m2, v2


def _adam_sharded(parts, first, w, m, v, name):
    nl, nr, nc = w.shape

    def body(*refs):
        p_refs = refs[:nl]
        w_ref, m_ref, v_ref, g_ref, d_ref, m2_ref, v2_ref = refs[nl:]
        for k in range(nl):
            @pl.when(pl.program_id(0) == k)
            def _(p_ref=p_refs[k]):
                g = p_ref[0].astype(F32)
                for dev in range(1, N_DEV):
                    g = g + p_ref[dev].astype(F32)
                delta, m2, v2 = _adam_math(w_ref[0], g, m_ref[0], v_ref[0])
                g_ref[0] = g
                d_ref[0] = delta
                m2_ref[0] = m2
                v2_ref[0] = v2

    def part_spec(k):
        return pl.BlockSpec((N_DEV, ADAM_ROWS, nc), lambda l, i: (0, first + jnp.where(l == k, i, 0), 0))

    blk = pl.BlockSpec((1, ADAM_ROWS, nc), lambda l, i: (l, i, 0))
    shp = jax.ShapeDtypeStruct(w.shape, F32)
    return pl.pallas_call(
        body, name=name, grid=(nl, pl.cdiv(nr, ADAM_ROWS)),
        in_specs=[part_spec(k) for k in range(nl)] + [blk, blk, blk],
        out_specs=[blk, blk, blk, blk],
        out_shape=[shp, shp, shp, shp],
        compiler_params=_cparams(("arbitrary", "arbitrary")),
    )(*parts, w, m, v)


def _adam_columns(parts, first, w, m, v):
    nc, nl, d = w.shape

    def body(*refs):
        p_refs = refs[:nl]
        w_ref, m_ref, v_ref, g_ref, d_ref, m2_ref, v2_ref = refs[nl:]
        for l in range(nl):
            g = p_refs[l][0].astype(F32)
            for dev in range(1, N_DEV):
                g = g + p_refs[l][dev].astype(F32)
            delta, m2, v2 = _adam_math(w_ref[:, l, :], g, m_ref[:, l, :], v_ref[:, l, :])
            g_ref[:, l, :] = g
            d_ref[:, l, :] = delta
            m2_ref[:, l, :] = m2
            v2_ref[:, l, :] = v2

    blk = pl.BlockSpec((ADAM_COL_ROWS, nl, d), lambda i: (i, 0, 0))
    part = pl.BlockSpec((N_DEV, ADAM_COL_ROWS, d), lambda i: (0, first + i, 0))
    shp = jax.ShapeDtypeStruct(w.shape, F32)
    return pl.pallas_call(
        body, name="adam_w_in", grid=(pl.cdiv(nc, ADAM_COL_ROWS),),
        in_specs=[part] * nl + [blk, blk, blk],
        out_specs=[blk, blk, blk, blk],
        out_shape=[shp, shp, shp, shp],
        compiler_params=_cparams(("parallel",)),
    )(*parts, w, m, v)


def _adam_small(w, g, m, v):
    def body(w_ref, g_ref, m_ref, v_ref, d_ref, m2_ref, v2_ref):
        delta, m2, v2 = _adam_math(w_ref[...], g_ref[...], m_ref[...], v_ref[...])
        d_ref[...] = delta
        m2_ref[...] = m2
        v2_ref[...] = v2

    shp = jax.ShapeDtypeStruct(w.shape, F32)
    return pl.pallas_call(body, name="adam_small", out_shape=[shp, shp, shp])(w, g, m, v)


def _position():
    return lax.axis_index("x"), lax.axis_index("y"), lax.axis_index("c")


def _slot(p):
    return 4 * p[0] + 2 * p[1] + p[2]


BF16_TILE_ROWS = 16


def _slab_rows(rows, cols):
    return -(-(rows + cols) // BF16_TILE_ROWS) * BF16_TILE_ROWS


RELAYOUT_COLS = 512
RELAYOUT_CHUNK = 64


def _shard_pieces(dev, rows, cols):
    moved = ((0, GA_ORIG, 0), (GA_ORIG, GA_ORIG + GLA_RANK, OFF_GA - GA_ORIG), (GA_ORIG + GLA_RANK, D_IN, -GLA_RANK))
    c0, c1 = dev * cols, (dev + 1) * cols
    return [(rows + max(c0, lo) - c0, max(c0, lo) + off, min(c1, hi) - max(c0, lo))
            for lo, hi, off in moved if max(c0, lo) < min(c1, hi)]


def _move_rows(src, src_row, dst, dst_row, n):
    assert src_row % 2 == 0 and dst_row % 2 == 0 and n % 2 == 0
    for r in range(0, n // 2, RELAYOUT_CHUNK):
        m = min(RELAYOUT_CHUNK, n // 2 - r)
        dst[dst_row // 2 + r:dst_row // 2 + r + m, :] = src[src_row // 2 + r:src_row // 2 + r + m, :]


def _aligned_weight(land, rows, cols):
    _, slab, d = land.shape
    ct = min(RELAYOUT_COLS, d)

    def body(land_ref, wt_ref):
        dev = pl.program_id(1)
        src = land_ref.bitcast(jnp.uint32)
        dst = wt_ref.bitcast(jnp.uint32)

        @pl.when(dev == 0)
        def _():
            dst[D_IN // 2:D_ZP // 2, :] = jnp.zeros(((D_ZP - D_IN) // 2, ct), jnp.uint32)

        for k in range(N_DEV):
            @pl.when(dev == k)
            def _(k=k):
                for at, to, n in _shard_pieces(k, rows, cols):
                    _move_rows(src, at, dst, to, n)

    return pl.pallas_call(
        body, name="aligned_weight", grid=(d // ct, N_DEV),
        in_specs=[pl.BlockSpec((slab, ct), lambda c, dev: (dev, c))],
        out_specs=pl.BlockSpec((D_ZP, ct), lambda c, dev: (0, c)),
        out_shape=jax.ShapeDtypeStruct((D_ZP, d), land.dtype),
        compiler_params=_cparams(("parallel", "arbitrary")),
    )(land.reshape(N_DEV * slab, d))


def _partial_slabs(dwo, dwt, rows, cols):
    d = dwt.shape[1]
    slab = _slab_rows(rows, cols)
    ct = min(RELAYOUT_COLS, d)

    def body(dwo_ref, dwt_ref, out_ref):
        dev = pl.program_id(1)
        src = dwt_ref.bitcast(jnp.uint32)
        dst = out_ref.bitcast(jnp.uint32)
        out_ref[0:rows, :] = dwo_ref[...]
        dst[(rows + cols) // 2:slab // 2, :] = jnp.zeros(((slab - rows - cols) // 2, ct), jnp.uint32)
        for k in range(N_DEV):
            @pl.when(dev == k)
            def _(k=k):
                for to, at, n in _shard_pieces(k, rows, cols):
                    _move_rows(src, at, dst, to, n)

    return pl.pallas_call(
        body, name="partial_slabs", grid=(d // ct, N_DEV),
        in_specs=[pl.BlockSpec((rows, ct), lambda c, dev: (dev, c)),
                  pl.BlockSpec((D_ZP, ct), lambda c, dev: (0, c))],
        out_specs=pl.BlockSpec((slab, ct), lambda c, dev: (dev, c)),
        out_shape=jax.ShapeDtypeStruct((N_DEV * slab, d), dwt.dtype),
        compiler_params=_cparams(("parallel", "arbitrary")),
    )(dwo, dwt).reshape(N_DEV, slab, d)


def _peer(pos, k):
    x, y, c = pos
    return (1 - x if k & 4 else x, 1 - y if k & 2 else y, 1 - c if k & 1 else c)


HBM_SPEC = pl.BlockSpec(memory_space=pltpu.HBM)
SEM_SPEC = pl.BlockSpec(memory_space=pltpu.SEMAPHORE)
GATHER_PEERS = (1, 4, 2, 6)
ALL_PEERS = (1, 2, 3, 4, 5, 6, 7)


def _hbm(a):
    return pltpu.with_memory_space_constraint(a, pltpu.HBM)


def _split_copies(src_ref, land_ref, send_sems, recv_sems, ks, per_peer, landed):
    me = _position()
    out = []
    for i, k in enumerate(ks):
        peer = _peer(me, k)
        src = src_ref.at[_slot(peer)] if per_peer else src_ref
        dst = land_ref.at[_slot(peer) if landed else _slot(me)]
        out.append(pltpu.make_async_remote_copy(
            src_ref=src, dst_ref=dst, send_sem=send_sems.at[i], recv_sem=recv_sems.at[i],
            device_id=peer, device_id_type=MESH))
    return out


def _exchange_start(src, after, ks, per_peer, name):
    slab = src.shape[1:] if per_peer else src.shape
    land_shape = (N_DEV,) + tuple(slab)
    n = len(ks)

    def body(src_ref, land_ref, after_ref, send_sems, recv_sems, src_thru, land_thru, token):
        for cp in _split_copies(src_ref, land_ref, send_sems, recv_sems, ks, per_peer, landed=False):
            cp.start()
        token[...] = jnp.zeros_like(token)

    return pl.pallas_call(
        body, name=name,
        out_shape=(pltpu.SemaphoreType.DMA((n,)), pltpu.SemaphoreType.DMA((n,)),
                   pltpu.HBM(src.shape, src.dtype), pltpu.HBM(land_shape, src.dtype),
                   jax.ShapeDtypeStruct((8, LANE), F32)),
        in_specs=(HBM_SPEC, HBM_SPEC, ANY),
        out_specs=(SEM_SPEC, SEM_SPEC, HBM_SPEC, HBM_SPEC, pl.BlockSpec(memory_space=pltpu.VMEM)),
        input_output_aliases={0: 2, 1: 3},
        compiler_params=pltpu.CompilerParams(has_side_effects=pltpu.SideEffectType.DATAFLOW_SIDE_EFFECTING),
    )(_hbm(src), _hbm(lax.empty(land_shape, src.dtype)), after)


def _exchange_wait(started, after, ks, per_peer, name):
    send_sems, recv_sems, src_thru, land_thru = started

    def body(src_ref, land_ref, send_sems, recv_sems, after_ref, src_dead, land_out):
        for cp in _split_copies(src_ref, land_ref, send_sems, recv_sems, ks, per_peer, landed=True):
            cp.wait_send()
            cp.wait_recv()

    return pl.pallas_call(
        body, name=name,
        out_shape=(pltpu.HBM(src_thru.shape, src_thru.dtype), pltpu.HBM(land_thru.shape, land_thru.dtype)),
        in_specs=(HBM_SPEC, HBM_SPEC, SEM_SPEC, SEM_SPEC, ANY), out_specs=(HBM_SPEC, HBM_SPEC),
        input_output_aliases={0: 0, 1: 1},
        compiler_params=pltpu.CompilerParams(has_side_effects=pltpu.SideEffectType.DATAFLOW_SIDE_EFFECTING),
    )(src_thru, land_thru, send_sems, recv_sems, after)


def _relay_copies(land_ref, send_sems, recv_sems, landed):
    me = _position()
    sibling = _peer(me, 1)
    out = []
    for i, k in enumerate(GATHER_PEERS[1:]):
        blk = land_ref.at[_slot(_peer(sibling if landed else me, k))]
        out.append(pltpu.make_async_remote_copy(
            src_ref=blk, dst_ref=blk, send_sem=send_sems.at[i], recv_sem=recv_sems.at[i],
            device_id=sibling, device_id_type=MESH))
    return out


def _relay_start(land, name):
    n = len(GATHER_PEERS) - 1

    def body(land_ref, send_sems, recv_sems, land_thru, token):
        for cp in _relay_copies(land_ref, send_sems, recv_sems, landed=False):
            cp.start()
        token[...] = jnp.zeros_like(token)

    return pl.pallas_call(
        body, name=name,
        out_shape=(pltpu.SemaphoreType.DMA((n,)), pltpu.SemaphoreType.DMA((n,)),
                   pltpu.HBM(land.shape, land.dtype), jax.ShapeDtypeStruct((8, LANE), F32)),
        in_specs=(HBM_SPEC,),
        out_specs=(SEM_SPEC, SEM_SPEC, HBM_SPEC, pl.BlockSpec(memory_space=pltpu.VMEM)),
        input_output_aliases={0: 2},
        compiler_params=pltpu.CompilerParams(has_side_effects=pltpu.SideEffectType.DATAFLOW_SIDE_EFFECTING),
    )(_hbm(land))


def _relay_wait(started, after, name):
    send_sems, recv_sems, land_thru = started

    def body(land_ref, send_sems, recv_sems, after_ref, land_out):
        for cp in _relay_copies(land_ref, send_sems, recv_sems, landed=True):
            cp.wait_send()
            cp.wait_recv()

    return pl.pallas_call(
        body, name=name,
        out_shape=pltpu.HBM(land_thru.shape, land_thru.dtype),
        in_specs=(HBM_SPEC, SEM_SPEC, SEM_SPEC, ANY), out_specs=HBM_SPEC,
        input_output_aliases={0: 0},
        compiler_params=pltpu.CompilerParams(has_side_effects=pltpu.SideEffectType.DATAFLOW_SIDE_EFFECTING),
    )(land_thru, send_sems, recv_sems, after)


def _exchange(arrs, name):
    n = len(arrs)

    def body(*refs):
        ins, outs = refs[:n], refs[n:2 * n]
        send_sems, recv_sems, local_sems = refs[2 * n:]
        me = _position()

        def copy(a, k):
            peer = _peer(me, k)
            return pltpu.make_async_remote_copy(
                src_ref=ins[a].at[_slot(peer)], dst_ref=outs[a].at[_slot(me)],
                send_sem=send_sems.at[a * 7 + k - 1], recv_sem=recv_sems.at[a * 7 + k - 1],
                device_id=peer, device_id_type=MESH)

        def landed(a, k):
            peer = _peer(me, k)
            return pltpu.make_async_remote_copy(
                src_ref=ins[a].at[_slot(peer)], dst_ref=outs[a].at[_slot(peer)],
                send_sem=send_sems.at[a * 7 + k - 1], recv_sem=recv_sems.at[a * 7 + k - 1],
                device_id=peer, device_id_type=MESH)

        mine = [pltpu.make_async_copy(ins[a].at[_slot(me)], outs[a].at[_slot(me)], local_sems.at[a])
                for a in range(n)]
        for cp in mine:
            cp.start()
        sent = [copy(a, k) for k in range(1, N_DEV) for a in range(n)]
        for cp in sent:
            cp.start()
        for k in range(1, N_DEV):
            for a in range(n):
                landed(a, k).wait_recv()
        for cp in sent:
            cp.wait_send()
        for cp in mine:
            cp.wait()

    return pl.pallas_call(
        body, name=name,
        in_specs=[ANY] * n, out_specs=[ANY] * n,
        out_shape=[jax.ShapeDtypeStruct(a.shape, a.dtype) for a in arrs],
        scratch_shapes=[pltpu.SemaphoreType.DMA((7 * n,)), pltpu.SemaphoreType.DMA((7 * n,)),
                        pltpu.SemaphoreType.DMA((n,))],
    )(*arrs)


def _sum_slots(parts):
    def body(p_ref, o_ref):
        acc = p_ref[0]
        for dev in range(1, N_DEV):
            acc = acc + p_ref[dev]
        o_ref[...] = acc

    return pl.pallas_call(body, name="sum_slots",
                          out_shape=jax.ShapeDtypeStruct(parts.shape[1:], F32))(parts)


def _pack(arrs):
    flat = jnp.concatenate([a.reshape(-1) for a in arrs])
    pad = (-flat.shape[0]) % (8 * LANE)
    return jnp.pad(flat, (0, pad)).reshape(-1, LANE)


def _unpack(packed, shapes):
    flat = packed.reshape(-1)
    out, at = [], 0
    for shp in shapes:
        size = 1
        for dim in shp:
            size *= dim
        out.append(flat[at:at + size].reshape(shp))
        at += size
    return out


def _layer_fwd(x, wt, wo, g_pre, g_post, wa_pad, b_alpha, g_gla, g_att, rb_pad, midway=None):
    h = _rms_fwd(x, g_pre)
    z = _matmul(h, wt, "nt", F32, 512, D_ZP // 3, 512, "in_proj")
    y_gla, o_gla, states = _gla_fwd(z, wa_pad, b_alpha, g_gla)
    if midway is not None:
        g_att = g_att + midway(y_gla)[:1, :1]
    y_att, o_att = _att_fwd(z, rb_pad, g_att)
    ycat = jnp.concatenate([y_gla, y_att], axis=1)
    y = _matmul(ycat, wo, "nn", F32, 512, 1024, 512, "out_proj")
    out = _post_fwd(x, y, g_post)
    return out, (x, h, z, o_gla, states, o_att, ycat, y)


def _layer_bwd(dout, saved, wt, wo, g_pre, g_post, wa_pad, b_alpha, g_gla, g_att, rb_pad):
    x, h, z, o_gla, states, o_att, ycat, y = saved
    dy, dg_post = _post_bwd(dout, y, g_post)
    dycat = _matmul(dy, wo, "nt", F32, 512, 1024, 512, "out_proj_dx")
    dwo = _matmul(ycat, dy, "tn", BF16, 512, 1024, 512, "out_proj_dw")
    dq, dk, dv, dgg, dga, dwa, db, dg_gla = _gla_bwd(dycat, o_gla, z, wa_pad, b_alpha, g_gla, states)
    daq, dak, dav, dag, drb, dg_att = _att_bwd(dycat, o_att, z, rb_pad, g_att)
    dz = jnp.concatenate([dq, dk, dv, dgg, daq, dak.astype(BF16), dav.astype(BF16), dag, dga], axis=1)
    dh = _matmul(dz, wt, "nn", F32, 512, 1024, D_ZP // 3, "in_proj_dx")
    dwt = _matmul(dz, h, "tn", BF16, D_ZP // 3, 512, 512, "in_proj_dw")
    dx, dg_pre = _pre_bwd(dh, x, g_pre, dout)
    small = (dg_pre[0], dg_post[0], dwa[:GLA_RANK], db[0], dg_gla[0], dg_att[0], drb[:, 0, :N_REL])
    return dx, dwt, dwo, small


def kernel(x, w_in, w_out, g_pre, g_post, w_alpha, b_alpha, g_gla, g_att, rel_bias, loss_target, m_w_in, m_w_out, m_g_pre, m_g_post, m_w_alpha, m_b_alpha, m_g_gla, m_g_att, m_rel_bias, v_w_in, v_w_out, v_g_pre, v_g_post, v_w_alpha, v_b_alpha, v_g_gla, v_g_att, v_rel_bias):
    nl, d, cols = w_in.shape
    rows = w_out.shape[1]
    s = x.shape[1]
    x0 = x.reshape(s, d)
    tgt = loss_target.reshape(s, d)

    cols_first = lambda a: jnp.transpose(a, (2, 0, 1))
    w_c = cols_first(w_in)
    pad = jnp.zeros((_slab_rows(rows, cols) - rows - cols, d), BF16)
    shards = [jnp.concatenate([w_out[l].astype(BF16), w_c[:, l].astype(BF16), pad], axis=0) for l in range(nl)]
    wa_g = _exchange([jnp.broadcast_to(_pack([w_alpha])[None], (N_DEV,) + _pack([w_alpha]).shape)], "gather_alpha")[0]
    wa_cols = w_alpha.shape[2]
    wa_full = wa_g.reshape(N_DEV, -1)[:, :nl * GLA_RANK * wa_cols].reshape(N_DEV, nl, GLA_RANK, wa_cols)
    wa_full = jnp.transpose(wa_full, (1, 2, 0, 3)).reshape(nl, GLA_RANK, GLA_KW)
    wa_pad = jnp.pad(wa_full, ((0, 0), (0, LANE - GLA_RANK), (0, 0)))
    rb_pad = jnp.pad(rel_bias, ((0, 0), (0, 0), (0, 3 * LANE - N_REL)))

    def layer_args(l, follows_pre=None, follows_post=None):
        gp = g_pre[l:l + 1] if follows_pre is None else g_pre[l:l + 1] + follows_pre[:1, :1]
        gq = g_post[l:l + 1] if follows_post is None else g_post[l:l + 1] + follows_post[:1, :1]
        return (wts[l], wos[l], gp, gq, wa_pad[l], b_alpha[l:l + 1], g_gla[l:l + 1], g_att[l:l + 1], rb_pad[l])

    my = _slot(_position())

    def fetch(l, after):
        return _exchange_start(shards[l], after, GATHER_PEERS, False, f"gather_start_{l}")

    def relay(l, first_hop, after):
        _, land = _exchange_wait(first_hop[:4], after, GATHER_PEERS, False, f"gather_wait_{l}")
        return _relay_start(land, f"relay_start_{l}")

    def midway(l, y):
        flight["relay"] = relay(l + 1, flight["fetch"], y)
        if l + 2 >= nl:
            return flight["relay"][3]
        flight["fetch"] = fetch(l + 2, flight["relay"][2])
        return flight["fetch"][4]

    act, saved, wts, wos, flight = x0, [], [], [], {}
    flight["fetch"] = fetch(0, x0)
    flight["relay"] = relay(0, flight["fetch"], x0)
    if nl > 1:
        flight["fetch"] = fetch(1, flight["relay"][2])
    for l in range(nl):
        land = _relay_wait(flight["relay"][:3], act, f"relay_wait_{l}")
        land = lax.dynamic_update_slice_in_dim(land, shards[l][None], my, 0)
        wos.append(land[:, :rows].reshape(N_DEV * rows, d))
        wts.append(_aligned_weight(land, rows, cols))
        act, sv = _layer_fwd(act, *layer_args(l),
                             midway=functools.partial(midway, l) if l + 1 < nl else None)
        saved.append(sv)
    dout, sq = _loss_head(act, tgt)
    loss = lax.psum(sq[0, 0] * (0.5 / d), ("x", "y", "c"))

    smalls, pending, token = [None] * nl, [None] * nl, None
    for l in reversed(range(nl)):
        dout, dwt, dwo, smalls[l] = _layer_bwd(dout, saved[l], *layer_args(l, follows_post=token))
        pending[l] = _exchange_start(_partial_slabs(dwo, dwt, rows, cols), dout, ALL_PEERS, True,
                                     f"scatter_start_{l}")
        token = pending[l][4]
    grad_x = dout.reshape(x.shape)

    parts = []
    for l in range(nl):
        partial, land = _exchange_wait(pending[l][:4], dout, ALL_PEERS, True, f"scatter_wait_{l}")
        parts.append(lax.dynamic_update_slice_in_dim(land, lax.dynamic_slice_in_dim(partial, my, 1, 0), my, 0))
    g_w_in, d_w_in, m2_w_in, v2_w_in = [
        jnp.transpose(a, (1, 2, 0))
        for a in _adam_columns(parts, rows // ADAM_COL_ROWS, w_c, cols_first(m_w_in), cols_first(v_w_in))]
    g_w_out, d_w_out, m2_w_out, v2_w_out = _adam_sharded(parts, 0, w_out, m_w_out, v_w_out, "adam_w_out")

    names = 7
    small_stacked = [jnp.stack([smalls[l][i] for l in range(nl)]) for i in range(names)]
    shapes = [a.shape for a in small_stacked]
    packed = _pack(small_stacked)
    gathered = _exchange([jnp.broadcast_to(packed[None], (N_DEV,) + packed.shape)], "gather_small_grads")[0]
    g_pre_g, g_post_g, wa_g_full, b_g, gla_g, att_g, rb_g = _unpack(_sum_slots(gathered), shapes)
    wa_g_mine = lax.dynamic_slice_in_dim(wa_g_full, my * wa_cols, wa_cols, axis=2)
    grads = [g_pre_g, g_post_g, wa_g_mine, b_g, gla_g, att_g, rb_g]
    ws = [g_pre, g_post, w_alpha, b_alpha, g_gla, g_att, rel_bias]
    ms = [m_g_pre, m_g_post, m_w_alpha, m_b_alpha, m_g_gla, m_g_att, m_rel_bias]
    vs = [v_g_pre, v_g_post, v_w_alpha, v_b_alpha, v_g_gla, v_g_att, v_rel_bias]
    shapes2 = [a.shape for a in ws]
    d_s, m2_s, v2_s = _adam_small(_pack(ws), _pack(grads), _pack(ms), _pack(vs))
    d_s, m2_s, v2_s = _unpack(d_s, shapes2), _unpack(m2_s, shapes2), _unpack(v2_s, shapes2)

    def ordered(big_in, big_out, small):
        return [big_in, big_out] + list(small)

    return (loss, grad_x,
            *ordered(g_w_in, g_w_out, grads),
            *ordered(d_w_in, d_w_out, d_s),
            *ordered(m2_w_in, m2_w_out, m2_s),
            *ordered(v2_w_in, v2_w_out, v2_s))
```

```python
import functools

import jax
import jax.numpy as jnp
from jax import lax
from jax.experimental import pallas as pl
from jax.experimental.pallas import tpu as pltpu

F32 = jnp.float32
BF16 = jnp.bfloat16
MESH = pl.DeviceIdType.MESH
ANY = pl.BlockSpec(memory_space=pl.ANY)

CHUNK = 64
GLA_HEADS = 4
GLA_DK = 128
GLA_DV = 256
GLA_KW = GLA_HEADS * GLA_DK
D_GLA = GLA_HEADS * GLA_DV
GLA_RANK = 16
GLA_TAU = 16.0
ATT_HEADS = 8
ATT_HD = 128
D_ATT = ATT_HEADS * ATT_HD
LEFT_CHUNKS = 8
REL_CLIP = 128
N_REL = 2 * REL_CLIP + 1
EPS = 1e-6
D_IN = 2 * GLA_KW + 2 * D_GLA + GLA_RANK + 4 * D_ATT
GLA_SCALE = GLA_DK ** -0.5
ATT_SCALE = ATT_HD ** -0.5

ADAM_LR = 0.001
ADAM_B1 = 0.9
ADAM_B2 = 0.999
ADAM_EPS = 1e-08
ADAM_WD = 0.01
ADAM_STEP = 10

N_DEV = 8
LANE = 128
GA_ORIG = 2 * GLA_KW + 2 * D_GLA
OFF_AQ = GA_ORIG
OFF_GA = GA_ORIG + 4 * D_ATT
D_ZP = OFF_GA + LANE
QB = 2 * CHUNK
ATT_UNROLL = 4
WIN = (LEFT_CHUNKS + 2) * CHUNK
ET_ROWS = WIN + LEFT_CHUNKS * CHUNK
NEG = -1e30
VMEM_LIMIT = 48 * 1024 * 1024


def _cparams(sem):
    return pltpu.CompilerParams(dimension_semantics=sem, vmem_limit_bytes=VMEM_LIMIT)


def _dot(a, b):
    return jnp.dot(a, b, preferred_element_type=F32)


def _dot_nt(a, b):
    return lax.dot_general(a, b, (((1,), (1,)), ((), ())), preferred_element_type=F32)


def _dot_tn(a, b):
    return lax.dot_general(a, b, (((0,), (0,)), ((), ())), preferred_element_type=F32)


def _dot01(t, x, left=True):
    if not left:
        t, x = x, t
    hi = x.astype(BF16)
    r = x - hi.astype(F32)
    mid = r.astype(BF16)
    lo = (r - mid.astype(F32)).astype(BF16)
    if left:
        return _dot(t, hi) + _dot(t, mid) + _dot(t, lo)
    return _dot(hi, t) + _dot(mid, t) + _dot(lo, t)


def _sigmoid(x):
    return 1.0 / (1.0 + jnp.exp(-x))


def _log_sigmoid(x):
    return jnp.minimum(x, 0.0) - jnp.log(1.0 + jnp.exp(-jnp.abs(x)))


TILES = {
    "in_proj": (512, D_ZP // 3, None),
    "in_proj_dx": (512, 512, None),
    "in_proj_dw": (D_ZP // 3, 512, None),
    "out_proj": (512, 1024, None),
    "out_proj_dx": (512, 1024, None),
    "out_proj_dw": (1024, 1024, None),
}


def _matmul(a, b, mode, out_dtype, tm, tn, tk, name, n_outer=False):
    if mode == "nn":
        (m, k), n = a.shape, b.shape[1]
    elif mode == "nt":
        (m, k), n = a.shape, b.shape[0]
    else:
        (k, m), n = a.shape, b.shape[1]
    tm, tn, tk = min(tm, m), min(tn, n), k if tk is None else min(tk, k)
    assert m % tm == 0 and n % tn == 0 and k % tk == 0, (name, m, n, k)
    nk = k // tk
    dot = {"nn": _dot, "nt": _dot_nt, "tn": _dot_tn}[mode]

    def body_whole_k(a_ref, b_ref, o_ref):
        o_ref[...] = dot(a_ref[...], b_ref[...]).astype(out_dtype)

    def body(a_ref, b_ref, o_ref, acc_ref):
        kk = pl.program_id(2)

        @pl.when(kk == 0)
        def _():
            acc_ref[...] = jnp.zeros_like(acc_ref)

        acc_ref[...] += dot(a_ref[...], b_ref[...])

        @pl.when(kk == nk - 1)
        def _():
            o_ref[...] = acc_ref[...].astype(out_dtype)

    def at(index):
        return (lambda j, i, kk: index(i, j, kk)) if n_outer else index

    if mode == "tn":
        a_spec = pl.BlockSpec((tk, tm), at(lambda i, j, kk: (kk, i)))
    else:
        a_spec = pl.BlockSpec((tm, tk), at(lambda i, j, kk: (i, kk)))
    if mode == "nt":
        b_spec = pl.BlockSpec((tn, tk), at(lambda i, j, kk: (j, kk)))
    else:
        b_spec = pl.BlockSpec((tk, tn), at(lambda i, j, kk: (kk, j)))
    return pl.pallas_call(
        body_whole_k if nk == 1 else body, name=name,
        grid=(n // tn, m // tm, nk) if n_outer else (m // tm, n // tn, nk),
        in_specs=[a_spec, b_spec],
        out_specs=pl.BlockSpec((tm, tn), at(lambda i, j, kk: (i, j))),
        out_shape=jax.ShapeDtypeStruct((m, n), out_dtype),
        scratch_shapes=[] if nk == 1 else [pltpu.VMEM((tm, tn), F32)],
        compiler_params=_cparams(("parallel", "parallel", "arbitrary")),
    )(a, b)


ROWS = 256


def _rms_fwd(x, g):
    s, d = x.shape

    def body(x_ref, g_ref, h_ref):
        xv = x_ref[...]
        r = lax.rsqrt(jnp.mean(xv * xv, axis=-1, keepdims=True) + EPS)
        h_ref[...] = (xv * r * g_ref[...]).astype(BF16)

    return pl.pallas_call(
        body, name="rms_fwd", grid=(s // ROWS,),
        in_specs=[pl.BlockSpec((ROWS, d), lambda i: (i, 0)), pl.BlockSpec((1, d), lambda i: (0, 0))],
        out_specs=pl.BlockSpec((ROWS, d), lambda i: (i, 0)),
        out_shape=jax.ShapeDtypeStruct((s, d), BF16),
        compiler_params=_cparams(("parallel",)),
    )(x, g)


def _post_fwd(x, y, g):
    s, d = x.shape

    def body(x_ref, y_ref, g_ref, o_ref):
        yv = y_ref[...]
        r = lax.rsqrt(jnp.mean(yv * yv, axis=-1, keepdims=True) + EPS)
        o_ref[...] = x_ref[...] + yv * r * g_ref[...]

    row = pl.BlockSpec((ROWS, d), lambda i: (i, 0))
    return pl.pallas_call(
        body, name="post_fwd", grid=(s // ROWS,),
        in_specs=[row, row, pl.BlockSpec((1, d), lambda i: (0, 0))],
        out_specs=row,
        out_shape=jax.ShapeDtypeStruct((s, d), F32),
        compiler_params=_cparams(("parallel",)),
    )(x, y, g)


def _loss_head(out, tgt):
    s, d = out.shape

    def body(o_ref, t_ref, dout_ref, sum_ref):
        @pl.when(pl.program_id(0) == 0)
        def _():
            sum_ref[...] = jnp.zeros_like(sum_ref)

        e = o_ref[...] - t_ref[...]
        dout_ref[...] = e * (1.0 / d)
        sum_ref[...] += jnp.sum(jnp.sum(e * e, axis=1, keepdims=True), axis=0, keepdims=True)

    row = pl.BlockSpec((ROWS, d), lambda i: (i, 0))
    return pl.pallas_call(
        body, name="loss_head", grid=(s // ROWS,),
        in_specs=[row, row],
        out_specs=[row, pl.BlockSpec((1, 1), lambda i: (0, 0))],
        out_shape=[jax.ShapeDtypeStruct((s, d), F32), jax.ShapeDtypeStruct((1, 1), F32)],
        compiler_params=_cparams(("arbitrary",)),
    )(out, tgt)


def _post_bwd(dout, y, g):
    s, d = y.shape

    def body(do_ref, y_ref, g_ref, dy_ref, dg_ref):
        @pl.when(pl.program_id(0) == 0)
        def _():
            dg_ref[...] = jnp.zeros_like(dg_ref)

        yv = y_ref[...]
        dv = do_ref[...]
        r = lax.rsqrt(jnp.mean(yv * yv, axis=-1, keepdims=True) + EPS)
        dg_ref[...] += jnp.sum(dv * yv * r, axis=0, keepdims=True)
        w = dv * g_ref[...]
        dy = r * (w - yv * (r * r) * jnp.mean(w * yv, axis=-1, keepdims=True))
        dy_ref[...] = dy.astype(BF16)

    row = pl.BlockSpec((ROWS, d), lambda i: (i, 0))
    vec = pl.BlockSpec((1, d), lambda i: (0, 0))
    return pl.pallas_call(
        body, name="post_bwd", grid=(s // ROWS,),
        in_specs=[row, row, vec],
        out_specs=[row, vec],
        out_shape=[jax.ShapeDtypeStruct((s, d), BF16), jax.ShapeDtypeStruct((1, d), F32)],
        compiler_params=_cparams(("arbitrary",)),
    )(dout, y, g)


def _pre_bwd(dh, x, g, dout):
    s, d = x.shape

    def body(dh_ref, x_ref, g_ref, do_ref, dx_ref, dg_ref):
        @pl.when(pl.program_id(0) == 0)
        def _():
            dg_ref[...] = jnp.zeros_like(dg_ref)

        xv = x_ref[...]
        dv = dh_ref[...]
        r = lax.rsqrt(jnp.mean(xv * xv, axis=-1, keepdims=True) + EPS)
        dg_ref[...] += jnp.sum(dv * xv * r, axis=0, keepdims=True)
        w = dv * g_ref[...]
        dx_ref[...] = do_ref[...] + r * (w - xv * (r * r) * jnp.mean(w * xv, axis=-1, keepdims=True))

    row = pl.BlockSpec((ROWS, d), lambda i: (i, 0))
    vec = pl.BlockSpec((1, d), lambda i: (0, 0))
    return pl.pallas_call(
        body, name="pre_bwd", grid=(s // ROWS,),
        in_specs=[row, row, vec, row],
        out_specs=[row, vec],
        out_shape=[jax.ShapeDtypeStruct((s, d), F32), jax.ShapeDtypeStruct((1, d), F32)],
        compiler_params=_cparams(("arbitrary",)),
    )(dh, x, g, dout)


def _gla_gate(ga_b, wa_b, b_ref, tri):
    pre = _dot(ga_b, wa_b) + b_ref[...]
    la = _log_sigmoid(pre) * (1.0 / GLA_TAU)
    cum = _dot01(tri, la)
    last = lax.broadcasted_iota(jnp.int32, cum.shape, 0) == CHUNK - 1
    return pre, cum, jnp.sum(jnp.where(last, cum, 0.0), axis=0, keepdims=True)


def _heads(width):
    return [slice(h * width, (h + 1) * width) for h in range(GLA_HEADS)]


def _z_specs_gla(rev=None):
    idx = (lambda n: n) if rev is None else rev
    return [
        pl.BlockSpec((CHUNK, GLA_KW), lambda n: (idx(n), 0)),
        pl.BlockSpec((CHUNK, GLA_KW), lambda n: (idx(n), 1)),
        pl.BlockSpec((CHUNK, D_GLA), lambda n: (idx(n), 1)),
        pl.BlockSpec((CHUNK, D_GLA), lambda n: (idx(n), 2)),
        pl.BlockSpec((CHUNK, LANE), lambda n: (idx(n), OFF_GA // LANE)),
    ]


def _gla_fwd(z, wa_pad, b_alpha, g_gla):
    s = z.shape[0]
    nchunk = s // CHUNK

    def body(q_ref, k_ref, v_ref, gg_ref, ga_ref, wa_ref, b_ref, g_ref, y_ref, o_ref, st_ref, state):
        @pl.when(pl.program_id(0) == 0)
        def _():
            state[...] = jnp.zeros_like(state)

        ga_b = ga_ref[...].astype(BF16)
        ri = lax.broadcasted_iota(jnp.int32, (CHUNK, CHUNK), 0)
        ci = lax.broadcasted_iota(jnp.int32, (CHUNK, CHUNK), 1)
        tri = jnp.where(ri >= ci, 1.0, 0.0).astype(BF16)
        _, cum, cend = _gla_gate(ga_b, wa_ref[...].astype(BF16), b_ref, tri)
        kd_b = (k_ref[...] * jnp.exp(cend - cum)).astype(BF16)
        a = jnp.exp(cend)
        qs = (q_ref[...] * GLA_SCALE).astype(BF16)
        v_b = v_ref[...].astype(BF16)
        keys, vals = _heads(GLA_DK), _heads(GLA_DV)
        sts = [state[h] * a[:, keys[h]] + _dot_tn(v_b[:, vals[h]], kd_b[:, keys[h]]) for h in range(GLA_HEADS)]
        for h in range(GLA_HEADS):
            state[h] = sts[h]
            st_ref[0, h] = sts[h]
        outs = [_dot_nt(qs[:, keys[h]], sts[h].astype(BF16)) for h in range(GLA_HEADS)]
        for h in range(GLA_HEADS):
            o, vs = outs[h], vals[h]
            o_ref[:, vs] = o
            r = lax.rsqrt(jnp.mean(o * o, axis=-1, keepdims=True) + EPS)
            gg = gg_ref[:, vs]
            y_ref[:, vs] = (o * r * g_ref[:, vs] * (gg * _sigmoid(gg))).astype(BF16)

    full = lambda shape: pl.BlockSpec(shape, lambda n: tuple(0 for _ in shape))
    wide = pl.BlockSpec((CHUNK, D_GLA), lambda n: (n, 0))
    return pl.pallas_call(
        body, name="gla_fwd", grid=(nchunk,),
        in_specs=_z_specs_gla() + [full((LANE, GLA_KW)), full((1, GLA_KW)), full((1, D_GLA))],
        out_specs=[wide, wide, pl.BlockSpec((1, GLA_HEADS, GLA_DV, GLA_DK), lambda n: (n, 0, 0, 0))],
        out_shape=[jax.ShapeDtypeStruct((s, D_GLA), BF16), jax.ShapeDtypeStruct((s, D_GLA), F32),
                   jax.ShapeDtypeStruct((nchunk, GLA_HEADS, GLA_DV, GLA_DK), F32)],
        scratch_shapes=[pltpu.VMEM((GLA_HEADS, GLA_DV, GLA_DK), F32)],
        compiler_params=_cparams(("arbitrary",)),
    )(z, z, z, z, z, wa_pad, b_alpha, g_gla)


def _gla_bwd(dyc, o_gla, z, wa_pad, b_alpha, g_gla, states):
    s = z.shape[0]
    nchunk = s // CHUNK
    rev = lambda n: nchunk - 1 - n

    def body(dy_ref, o_ref, q_ref, k_ref, v_ref, gg_ref, ga_ref, wa_ref, b_ref, g_ref, st_ref, stp_ref,
             dq_ref, dk_ref, dv_ref, dgg_ref, dga_ref, dwa_ref, db_ref, dg_ref, carry):
        step = pl.program_id(0)

        @pl.when(step == 0)
        def _():
            carry[...] = jnp.zeros_like(carry)
            dwa_ref[...] = jnp.zeros_like(dwa_ref)
            db_ref[...] = jnp.zeros_like(db_ref)
            dg_ref[...] = jnp.zeros_like(dg_ref)

        has_prev = (step < nchunk - 1).astype(F32)
        ga_b = ga_ref[...].astype(BF16)
        ri = lax.broadcasted_iota(jnp.int32, (CHUNK, CHUNK), 0)
        ci = lax.broadcasted_iota(jnp.int32, (CHUNK, CHUNK), 1)
        tri = jnp.where(ri >= ci, 1.0, 0.0).astype(BF16)
        tri_up = jnp.where(ci >= ri, 1.0, 0.0).astype(BF16)
        nh = range(GLA_HEADS)
        keys, vals = _heads(GLA_DK), _heads(GLA_DV)
        wa_b = wa_ref[...].astype(BF16)
        pre, cum, cend = _gla_gate(ga_b, wa_b, b_ref, tri)
        e = jnp.exp(cend - cum)
        a = jnp.exp(cend)
        kf = k_ref[...]
        kd_b = (kf * e).astype(BF16)
        v_b = v_ref[...].astype(BF16)
        qs = (q_ref[...] * GLA_SCALE).astype(BF16)
        do_b = []
        for h in nh:
            vs = vals[h]
            o = o_ref[:, vs]
            gg = gg_ref[:, vs]
            g = g_ref[:, vs]
            dy = dy_ref[:, vs]
            r = lax.rsqrt(jnp.mean(o * o, axis=-1, keepdims=True) + EPS)
            sg = _sigmoid(gg)
            dogn = dy * (gg * sg)
            dgg_ref[:, vs] = (dy * (o * r * g) * (sg * (1.0 + gg * (1.0 - sg)))).astype(BF16)
            dg_ref[:, vs] += jnp.sum(dogn * o * r, axis=0, keepdims=True)
            w = dogn * g
            do_b.append((r * (w - o * (r * r) * jnp.mean(w * o, axis=-1, keepdims=True))).astype(BF16))
        dqs = [_dot(do_b[h], st_ref[0, h].astype(BF16)) for h in nh]
        gts = [_dot_tn(do_b[h], qs[:, keys[h]]) + carry[h] for h in nh]
        dq_ref[...] = (jnp.concatenate(dqs, axis=1) * GLA_SCALE).astype(BF16)
        gt_b = [gt.astype(BF16) for gt in gts]
        dkd = jnp.concatenate([_dot(v_b[:, vals[h]], gt_b[h]) for h in nh], axis=1)
        dvs = [_dot_nt(kd_b[:, keys[h]], gt_b[h]) for h in nh]
        da = jnp.concatenate([jnp.sum(gts[h] * (stp_ref[0, h] * has_prev), axis=0, keepdims=True) for h in nh],
                             axis=1)
        for h in nh:
            dv_ref[:, vals[h]] = dvs[h].astype(BF16)
            carry[h] = gts[h] * a[:, keys[h]]
        dk_ref[...] = (dkd * e).astype(BF16)
        dd = dkd * kf * e
        dcend = jnp.sum(dd, axis=0, keepdims=True) + da * a
        dla = dcend - _dot01(tri_up, dd)
        dpre = dla * (1.0 / GLA_TAU) * (1.0 - _sigmoid(pre))
        dpre_b = dpre.astype(BF16)
        dga_ref[...] = _dot_nt(dpre_b, wa_b).astype(BF16)
        dwa_ref[...] += _dot_tn(ga_b, dpre_b)
        db_ref[...] += jnp.sum(dpre, axis=0, keepdims=True)

    full = lambda shape: pl.BlockSpec(shape, lambda n: tuple(0 for _ in shape))
    wide = pl.BlockSpec((CHUNK, D_GLA), lambda n: (rev(n), 0))
    keyw = pl.BlockSpec((CHUNK, GLA_KW), lambda n: (rev(n), 0))
    st_spec = pl.BlockSpec((1, GLA_HEADS, GLA_DV, GLA_DK), lambda n: (rev(n), 0, 0, 0))
    stp_spec = pl.BlockSpec((1, GLA_HEADS, GLA_DV, GLA_DK), lambda n: (jnp.maximum(rev(n) - 1, 0), 0, 0, 0))
    return pl.pallas_call(
        body, name="gla_bwd", grid=(nchunk,),
        in_specs=[wide, wide] + _z_specs_gla(rev)
        + [full((LANE, GLA_KW)), full((1, GLA_KW)), full((1, D_GLA)), st_spec, stp_spec],
        out_specs=[keyw, keyw, wide, wide, pl.BlockSpec((CHUNK, LANE), lambda n: (rev(n), 0)),
                   full((LANE, GLA_KW)), full((1, GLA_KW)), full((1, D_GLA))],
        out_shape=[jax.ShapeDtypeStruct((s, GLA_KW), BF16), jax.ShapeDtypeStruct((s, GLA_KW), BF16),
                   jax.ShapeDtypeStruct((s, D_GLA), BF16), jax.ShapeDtypeStruct((s, D_GLA), BF16),
                   jax.ShapeDtypeStruct((s, LANE), BF16),
                   jax.ShapeDtypeStruct((LANE, GLA_KW), F32), jax.ShapeDtypeStruct((1, GLA_KW), F32),
                   jax.ShapeDtypeStruct((1, D_GLA), F32)],
        scratch_shapes=[pltpu.VMEM((GLA_HEADS, GLA_DV, GLA_DK), F32)],
        compiler_params=_cparams(("arbitrary",)),
    )(dyc, o_gla, z, z, z, z, z, wa_pad, b_alpha, g_gla, states, states)


def _build_bias_table(rb_row, et_ref):
    far = jnp.broadcast_to(rb_row[:, 2 * REL_CLIP:2 * REL_CLIP + 1], (1, LANE))
    near_hi = rb_row[:, REL_CLIP:2 * REL_CLIP]
    near_lo = rb_row[:, 0:REL_CLIP]
    past = jnp.broadcast_to(rb_row[:, 0:1], (1, LANE))
    seg = [far, far, far, far, near_hi, near_lo] + [past] * (ET_ROWS // LANE - 5)
    ri = lax.broadcasted_iota(jnp.int32, (LANE, LANE), 0)
    ci = lax.broadcasted_iota(jnp.int32, (LANE, LANE), 1)
    for kb in range(ET_ROWS // LANE):
        wmat = jnp.where(ri + ci < LANE, seg[kb], seg[kb + 1])
        blk = pltpu.roll(wmat, 0, 1, stride=1, stride_axis=0)
        lag = LEFT_CHUNKS + ci // CHUNK - (2 * kb + ri // CHUNK)
        et_ref[kb * LANE:(kb + 1) * LANE, :] = jnp.where((lag >= 0) & (lag <= LEFT_CHUNKS), blk, NEG)


def _reduce_bias_table(det_ref):
    lane = lax.broadcasted_iota(jnp.int32, (1, LANE), 1)
    ri = lax.broadcasted_iota(jnp.int32, (LANE, LANE), 0)
    ci = lax.broadcasted_iota(jnp.int32, (LANE, LANE), 1)
    flip = jnp.where(ri + ci == LANE - 1, 1.0, 0.0).astype(BF16)
    segs = jnp.zeros((8, LANE), F32)
    seg_row = lax.broadcasted_iota(jnp.int32, (8, LANE), 0)
    prev_minus = jnp.zeros((1, LANE), F32)
    for kb in range(6):
        rolled = pltpu.roll(_dot01(det_ref[kb * LANE:(kb + 1) * LANE, :], flip, left=False), 0, 1,
                            stride=1, stride_axis=0)
        plus = jnp.sum(jnp.where(ci >= ri, rolled, 0.0), axis=0, keepdims=True)
        minus = jnp.sum(jnp.where(ci < ri, rolled, 0.0), axis=0, keepdims=True)
        segs = segs + jnp.where(seg_row == kb, plus + prev_minus, 0.0)
        prev_minus = minus
    segs = _dot01(segs, flip, left=False)
    pick = lambda kb: jnp.sum(jnp.where(seg_row == kb, segs, 0.0), axis=0, keepdims=True)
    far = jnp.sum(pick(0) + pick(1) + pick(2) + pick(3), axis=1, keepdims=True)
    last = jnp.where(lane == 0, far, 0.0)
    return jnp.concatenate([pick(5), pick(4), last], axis=1)


def _att_window(b):
    c0 = 2 * b
    kstart = pl.multiple_of(jnp.maximum(c0 - LEFT_CHUNKS, 0) * CHUNK, CHUNK)
    eoff = pl.multiple_of(jnp.maximum(LEFT_CHUNKS - c0, 0) * CHUNK, CHUNK)
    return kstart, eoff


def _att_probs(q_b, kw_b, et):
    st = _dot_nt(kw_b, q_b) * ATT_SCALE + et
    m = jnp.max(st, axis=0, keepdims=True)
    ex = jnp.exp(st - m)
    return ex / jnp.sum(ex, axis=0, keepdims=True)


def _att_fwd(z, rb_pad, g_att):
    s = z.shape[0]
    nblk = s // QB
    c_aq, c_ak, c_av, c_ag = [(OFF_AQ + i * D_ATT) // ATT_HD for i in range(4)]

    def body(q_ref, k_ref, v_ref, ag_ref, rb_ref, g_ref, y_ref, o_ref, et_ref):
        h = pl.program_id(0)
        b = pl.program_id(1)

        @pl.when(b == 0)
        def _():
            _build_bias_table(rb_ref[pl.ds(h, 1), :], et_ref)

        for j in range(ATT_UNROLL):
            rs = slice(j * QB, (j + 1) * QB)
            kstart, eoff = _att_window(b * ATT_UNROLL + j)
            q_b = q_ref[rs, :].astype(BF16)
            kw_b = k_ref[pl.ds(kstart, WIN), :].astype(BF16)
            vw_b = v_ref[pl.ds(kstart, WIN), :].astype(BF16)
            pt = _att_probs(q_b, kw_b, et_ref[pl.ds(eoff, WIN), :])
            o = _dot_tn(pt.astype(BF16), vw_b)
            o_ref[rs, :] = o
            r = lax.rsqrt(jnp.mean(o * o, axis=-1, keepdims=True) + EPS)
            ag = ag_ref[rs, :]
            y_ref[rs, :] = (o * r * g_ref[...] * (ag * _sigmoid(ag))).astype(BF16)

    blk = lambda col: pl.BlockSpec((ATT_UNROLL * QB, ATT_HD), lambda h, b: (b, col + h))
    seq = lambda col: pl.BlockSpec((s, ATT_HD), lambda h, b: (0, col + h))
    out_blk = pl.BlockSpec((ATT_UNROLL * QB, ATT_HD), lambda h, b: (b, h))
    return pl.pallas_call(
        body, name="att_fwd", grid=(ATT_HEADS, nblk // ATT_UNROLL),
        in_specs=[blk(c_aq), seq(c_ak), seq(c_av), blk(c_ag),
                  pl.BlockSpec((ATT_HEADS, 3 * LANE), lambda h, b: (0, 0)),
                  pl.BlockSpec((1, ATT_HD), lambda h, b: (0, h))],
        out_specs=[out_blk, out_blk],
        out_shape=[jax.ShapeDtypeStruct((s, D_ATT), BF16), jax.ShapeDtypeStruct((s, D_ATT), F32)],
        scratch_shapes=[pltpu.VMEM((ET_ROWS, LANE), F32)],
        compiler_params=_cparams(("arbitrary", "arbitrary")),
    )(z, z, z, z, rb_pad, g_att)


def _att_bwd(dyc, o_att, z, rb_pad, g_att):
    s = z.shape[0]
    nblk = s // QB
    c_aq, c_ak, c_av, c_ag = [(OFF_AQ + i * D_ATT) // ATT_HD for i in range(4)]
    c_dy = D_GLA // ATT_HD

    def body(dy_ref, o_ref, q_ref, k_ref, v_ref, ag_ref, rb_ref, g_ref,
             dq_ref, dk_ref, dv_ref, dag_ref, drb_ref, dg_ref, et_ref, det_ref):
        h = pl.program_id(0)
        b = pl.program_id(1)

        @pl.when(b == 0)
        def _():
            _build_bias_table(rb_ref[pl.ds(h, 1), :], et_ref)
            det_ref[...] = jnp.zeros_like(det_ref)
            dk_ref[...] = jnp.zeros_like(dk_ref)
            dv_ref[...] = jnp.zeros_like(dv_ref)
            dg_ref[...] = jnp.zeros_like(dg_ref)

        g = g_ref[...]
        dg = jnp.zeros((1, ATT_HD), F32)
        for j in range(ATT_UNROLL):
            rs = slice(j * QB, (j + 1) * QB)
            kstart, eoff = _att_window(b * ATT_UNROLL + j)
            q_b = q_ref[rs, :].astype(BF16)
            kw_b = k_ref[pl.ds(kstart, WIN), :].astype(BF16)
            vw_b = v_ref[pl.ds(kstart, WIN), :].astype(BF16)
            pt = _att_probs(q_b, kw_b, et_ref[pl.ds(eoff, WIN), :])
            o = o_ref[rs, :]
            ag = ag_ref[rs, :]
            dy = dy_ref[rs, :]
            r = lax.rsqrt(jnp.mean(o * o, axis=-1, keepdims=True) + EPS)
            sg = _sigmoid(ag)
            don = dy * (ag * sg)
            dag_ref[rs, :] = (dy * (o * r * g) * (sg * (1.0 + ag * (1.0 - sg)))).astype(BF16)
            dg = dg + jnp.sum(don * o * r, axis=0, keepdims=True)
            w = don * g
            do_b = (r * (w - o * (r * r) * jnp.mean(w * o, axis=-1, keepdims=True))).astype(BF16)
            pt_b = pt.astype(BF16)
            dpt = _dot_nt(vw_b, do_b)
            dst = pt * (dpt - jnp.sum(dpt * pt, axis=0, keepdims=True))
            det_ref[pl.ds(eoff, WIN), :] += dst
            ds_b = (dst * ATT_SCALE).astype(BF16)
            dq_ref[rs, :] = _dot_tn(ds_b, kw_b).astype(BF16)
            dk_ref[pl.ds(kstart, WIN), :] += _dot(ds_b, q_b)
            dv_ref[pl.ds(kstart, WIN), :] += _dot(pt_b, do_b)
        dg_ref[...] += dg

        @pl.when(b == nblk // ATT_UNROLL - 1)
        def _():
            drb_ref[0] = jnp.broadcast_to(_reduce_bias_table(det_ref), (8, 3 * LANE))

    blk = lambda col: pl.BlockSpec((ATT_UNROLL * QB, ATT_HD), lambda h, b: (b, col + h))
    seq = lambda col: pl.BlockSpec((s, ATT_HD), lambda h, b: (0, col + h))
    out_blk = pl.BlockSpec((ATT_UNROLL * QB, ATT_HD), lambda h, b: (b, h))
    out_seq = pl.BlockSpec((s, ATT_HD), lambda h, b: (0, h))
    return pl.pallas_call(
        body, name="att_bwd", grid=(ATT_HEADS, nblk // ATT_UNROLL),
        in_specs=[blk(c_dy), blk(0), blk(c_aq), seq(c_ak), seq(c_av), blk(c_ag),
                  pl.BlockSpec((ATT_HEADS, 3 * LANE), lambda h, b: (0, 0)),
                  pl.BlockSpec((1, ATT_HD), lambda h, b: (0, h))],
        out_specs=[out_blk, out_seq, out_seq, out_blk,
                   pl.BlockSpec((1, 8, 3 * LANE), lambda h, b: (h, 0, 0)),
                   pl.BlockSpec((1, ATT_HD), lambda h, b: (0, h))],
        out_shape=[jax.ShapeDtypeStruct((s, D_ATT), BF16), jax.ShapeDtypeStruct((s, D_ATT), F32),
                   jax.ShapeDtypeStruct((s, D_ATT), F32), jax.ShapeDtypeStruct((s, D_ATT), BF16),
                   jax.ShapeDtypeStruct((ATT_HEADS, 8, 3 * LANE), F32),
                   jax.ShapeDtypeStruct((1, D_ATT), F32)],
        scratch_shapes=[pltpu.VMEM((ET_ROWS, LANE), F32), pltpu.VMEM((ET_ROWS, LANE), F32)],
        compiler_params=_cparams(("arbitrary", "arbitrary")),
    )(dyc, o_att, z, z, z, z, rb_pad, g_att)


ADAM_ROWS = 64
ADAM_COL_ROWS = 32


def _adam_math(w, g, m, v):
    m2 = ADAM_B1 * m + (1.0 - ADAM_B1) * g
    v2 = ADAM_B2 * v + (1.0 - ADAM_B2) * (g * g)
    m_hat = m2 / (1.0 - ADAM_B1 ** ADAM_STEP)
    v_hat = v2 / (1.0 - ADAM_B2 ** ADAM_STEP)
    delta = -ADAM_LR * (m_hat / (jnp.sqrt(v_hat) + ADAM_EPS) + ADAM_WD * w)
    return delta, m2, v2


def _adam_sharded(parts, first, w, m, v, name):
    nl, nr, nc = w.shape

    def body(*refs):
        p_refs = refs[:nl]
        w_ref, m_ref, v_ref, g_ref, d_ref, m2_ref, v2_ref = refs[nl:]
        for k in range(nl):
            @pl.when(pl.program_id(0) == k)
            def _(p_ref=p_refs[k]):
                g = p_ref[0].astype(F32)
                for dev in range(1, N_DEV):
                    g = g + p_ref[dev].astype(F32)
                delta, m2, v2 = _adam_math(w_ref[0], g, m_ref[0], v_ref[0])
                g_ref[0] = g
                d_ref[0] = delta
                m2_ref[0] = m2
                v2_ref[0] = v2

    def part_spec(k):
        return pl.BlockSpec((N_DEV, ADAM_ROWS, nc), lambda l, i: (0, first + jnp.where(l == k, i, 0), 0))

    blk = pl.BlockSpec((1, ADAM_ROWS, nc), lambda l, i: (l, i, 0))
    shp = jax.ShapeDtypeStruct(w.shape, F32)
    return pl.pallas_call(
        body, name=name, grid=(nl, pl.cdiv(nr, ADAM_ROWS)),
        in_specs=[part_spec(k) for k in range(nl)] + [blk, blk, blk],
        out_specs=[blk, blk, blk, blk],
        out_shape=[shp, shp, shp, shp],
        compiler_params=_cparams(("arbitrary", "arbitrary")),
    )(*parts, w, m, v)


def _adam_columns(parts, first, w, m, v):
    nc, nl, d = w.shape

    def body(*refs):
        p_refs = refs[:nl]
        w_ref, m_ref, v_ref, g_ref, d_ref, m2_ref, v2_ref = refs[nl:]
        for l in range(nl):
            g = p_refs[l][0].astype(F32)
            for dev in range(1, N_DEV):
                g = g + p_refs[l][dev].astype(F32)
            delta, m2, v2 = _adam_math(w_ref[:, l, :], g, m_ref[:, l, :], v_ref[:, l, :])
            g_ref[:, l, :] = g
            d_ref[:, l, :] = delta
            m2_ref[:, l, :] = m2
            v2_ref[:, l, :] = v2

    blk = pl.BlockSpec((ADAM_COL_ROWS, nl, d), lambda i: (i, 0, 0))
    part = pl.BlockSpec((N_DEV, ADAM_COL_ROWS, d), lambda i: (0, first + i, 0))
    shp = jax.ShapeDtypeStruct(w.shape, F32)
    return pl.pallas_call(
        body, name="adam_w_in", grid=(pl.cdiv(nc, ADAM_COL_ROWS),),
        in_specs=[part] * nl + [blk, blk, blk],
        out_specs=[blk, blk, blk, blk],
        out_shape=[shp, shp, shp, shp],
        compiler_params=_cparams(("parallel",)),
    )(*parts, w, m, v)


def _adam_small(w, g, m, v):
    def body(w_ref, g_ref, m_ref, v_ref, d_ref, m2_ref, v2_ref):
        delta, m2, v2 = _adam_math(w_ref[...], g_ref[...], m_ref[...], v_ref[...])
        d_ref[...] = delta
        m2_ref[...] = m2
        v2_ref[...] = v2

    shp = jax.ShapeDtypeStruct(w.shape, F32)
    return pl.pallas_call(body, name="adam_small", out_shape=[shp, shp, shp])(w, g, m, v)


def _position():
    return lax.axis_index("x"), lax.axis_index("y"), lax.axis_index("c")


def _slot(p):
    return 4 * p[0] + 2 * p[1] + p[2]


BF16_TILE_ROWS = 16


def _slab_rows(rows, cols):
    return -(-(rows + cols) // BF16_TILE_ROWS) * BF16_TILE_ROWS


RELAYOUT_COLS = 512
RELAYOUT_CHUNK = 64


def _shard_pieces(dev, rows, cols):
    moved = ((0, GA_ORIG, 0), (GA_ORIG, GA_ORIG + GLA_RANK, OFF_GA - GA_ORIG), (GA_ORIG + GLA_RANK, D_IN, -GLA_RANK))
    c0, c1 = dev * cols, (dev + 1) * cols
    return [(rows + max(c0, lo) - c0, max(c0, lo) + off, min(c1, hi) - max(c0, lo))
            for lo, hi, off in moved if max(c0, lo) < min(c1, hi)]


def _move_rows(src, src_row, dst, dst_row, n):
    assert src_row % 2 == 0 and dst_row % 2 == 0 and n % 2 == 0
    for r in range(0, n // 2, RELAYOUT_CHUNK):
        m = min(RELAYOUT_CHUNK, n // 2 - r)
        dst[dst_row // 2 + r:dst_row // 2 + r + m, :] = src[src_row // 2 + r:src_row // 2 + r + m, :]


def _aligned_weight(land, rows, cols):
    _, slab, d = land.shape
    ct = min(RELAYOUT_COLS, d)

    def body(land_ref, wt_ref):
        dev = pl.program_id(1)
        src = land_ref.bitcast(jnp.uint32)
        dst = wt_ref.bitcast(jnp.uint32)

        @pl.when(dev == 0)
        def _():
            dst[D_IN // 2:D_ZP // 2, :] = jnp.zeros(((D_ZP - D_IN) // 2, ct), jnp.uint32)

        for k in range(N_DEV):
            @pl.when(dev == k)
            def _(k=k):
                for at, to, n in _shard_pieces(k, rows, cols):
                    _move_rows(src, at, dst, to, n)

    return pl.pallas_call(
        body, name="aligned_weight", grid=(d // ct, N_DEV),
        in_specs=[pl.BlockSpec((slab, ct), lambda c, dev: (dev, c))],
        out_specs=pl.BlockSpec((D_ZP, ct), lambda c, dev: (0, c)),
        out_shape=jax.ShapeDtypeStruct((D_ZP, d), land.dtype),
        compiler_params=_cparams(("parallel", "arbitrary")),
    )(land.reshape(N_DEV * slab, d))


def _partial_slabs(dwo, dwt, rows, cols):
    d = dwt.shape[1]
    slab = _slab_rows(rows, cols)
    ct = min(RELAYOUT_COLS, d)

    def body(dwo_ref, dwt_ref, out_ref):
        dev = pl.program_id(1)
        src = dwt_ref.bitcast(jnp.uint32)
        dst = out_ref.bitcast(jnp.uint32)
        out_ref[0:rows, :] = dwo_ref[...]
        dst[(rows + cols) // 2:slab // 2, :] = jnp.zeros(((slab - rows - cols) // 2, ct), jnp.uint32)
        for k in range(N_DEV):
            @pl.when(dev == k)
            def _(k=k):
                for to, at, n in _shard_pieces(k, rows, cols):
                    _move_rows(src, at, dst, to, n)

    return pl.pallas_call(
        body, name="partial_slabs", grid=(d // ct, N_DEV),
        in_specs=[pl.BlockSpec((rows, ct), lambda c, dev: (dev, c)),
                  pl.BlockSpec((D_ZP, ct), lambda c, dev: (0, c))],
        out_specs=pl.BlockSpec((slab, ct), lambda c, dev: (dev, c)),
        out_shape=jax.ShapeDtypeStruct((N_DEV * slab, d), dwt.dtype),
        compiler_params=_cparams(("parallel", "arbitrary")),
    )(dwo, dwt).reshape(N_DEV, slab, d)


def _peer(pos, k):
    x, y, c = pos
    return (1 - x if k & 4 else x, 1 - y if k & 2 else y, 1 - c if k & 1 else c)


HBM_SPEC = pl.BlockSpec(memory_space=pltpu.HBM)
SEM_SPEC = pl.BlockSpec(memory_space=pltpu.SEMAPHORE)
GATHER_PEERS = (1, 4, 2, 6)
ALL_PEERS = (1, 2, 3, 4, 5, 6, 7)


def _hbm(a):
    return pltpu.with_memory_space_constraint(a, pltpu.HBM)


def _split_copies(src_ref, land_ref, send_sems, recv_sems, ks, per_peer, landed):
    me = _position()
    out = []
    for i, k in enumerate(ks):
        peer = _peer(me, k)
        src = src_ref.at[_slot(peer)] if per_peer else src_ref
        dst = land_ref.at[_slot(peer) if landed else _slot(me)]
        out.append(pltpu.make_async_remote_copy(
            src_ref=src, dst_ref=dst, send_sem=send_sems.at[i], recv_sem=recv_sems.at[i],
            device_id=peer, device_id_type=MESH))
    return out


def _exchange_start(src, after, ks, per_peer, name):
    slab = src.shape[1:] if per_peer else src.shape
    land_shape = (N_DEV,) + tuple(slab)
    n = len(ks)

    def body(src_ref, land_ref, after_ref, send_sems, recv_sems, src_thru, land_thru, token):
        for cp in _split_copies(src_ref, land_ref, send_sems, recv_sems, ks, per_peer, landed=False):
            cp.start()
        token[...] = jnp.zeros_like(token)

    return pl.pallas_call(
        body, name=name,
        out_shape=(pltpu.SemaphoreType.DMA((n,)), pltpu.SemaphoreType.DMA((n,)),
                   pltpu.HBM(src.shape, src.dtype), pltpu.HBM(land_shape, src.dtype),
                   jax.ShapeDtypeStruct((8, LANE), F32)),
        in_specs=(HBM_SPEC, HBM_SPEC, ANY),
        out_specs=(SEM_SPEC, SEM_SPEC, HBM_SPEC, HBM_SPEC, pl.BlockSpec(memory_space=pltpu.VMEM)),
        input_output_aliases={0: 2, 1: 3},
        compiler_params=pltpu.CompilerParams(has_side_effects=pltpu.SideEffectType.DATAFLOW_SIDE_EFFECTING),
    )(_hbm(src), _hbm(lax.empty(land_shape, src.dtype)), after)


def _exchange_wait(started, after, ks, per_peer, name):
    send_sems, recv_sems, src_thru, land_thru = started

    def body(src_ref, land_ref, send_sems, recv_sems, after_ref, src_dead, land_out):
        for cp in _split_copies(src_ref, land_ref, send_sems, recv_sems, ks, per_peer, landed=True):
            cp.wait_send()
            cp.wait_recv()

    return pl.pallas_call(
        body, name=name,
        out_shape=(pltpu.HBM(src_thru.shape, src_thru.dtype), pltpu.HBM(land_thru.shape, land_thru.dtype)),
        in_specs=(HBM_SPEC, HBM_SPEC, SEM_SPEC, SEM_SPEC, ANY), out_specs=(HBM_SPEC, HBM_SPEC),
        input_output_aliases={0: 0, 1: 1},
        compiler_params=pltpu.CompilerParams(has_side_effects=pltpu.SideEffectType.DATAFLOW_SIDE_EFFECTING),
    )(src_thru, land_thru, send_sems, recv_sems, after)


def _relay_copies(land_ref, send_sems, recv_sems, landed):
    me = _position()
    sibling = _peer(me, 1)
    out = []
    for i, k in enumerate(GATHER_PEERS[1:]):
        blk = land_ref.at[_slot(_peer(sibling if landed else me, k))]
        out.append(pltpu.make_async_remote_copy(
            src_ref=blk, dst_ref=blk, send_sem=send_sems.at[i], recv_sem=recv_sems.at[i],
            device_id=sibling, device_id_type=MESH))
    return out


def _relay_start(land, name):
    n = len(GATHER_PEERS) - 1

    def body(land_ref, send_sems, recv_sems, land_thru, token):
        for cp in _relay_copies(land_ref, send_sems, recv_sems, landed=False):
            cp.start()
        token[...] = jnp.zeros_like(token)

    return pl.pallas_call(
        body, name=name,
        out_shape=(pltpu.SemaphoreType.DMA((n,)), pltpu.SemaphoreType.DMA((n,)),
                   pltpu.HBM(land.shape, land.dtype), jax.ShapeDtypeStruct((8, LANE), F32)),
        in_specs=(HBM_SPEC,),
        out_specs=(SEM_SPEC, SEM_SPEC, HBM_SPEC, pl.BlockSpec(memory_space=pltpu.VMEM)),
        input_output_aliases={0: 2},
        compiler_params=pltpu.CompilerParams(has_side_effects=pltpu.SideEffectType.DATAFLOW_SIDE_EFFECTING),
    )(_hbm(land))


def _relay_wait(started, after, name):
    send_sems, recv_sems, land_thru = started

    def body(land_ref, send_sems, recv_sems, after_ref, land_out):
        for cp in _relay_copies(land_ref, send_sems, recv_sems, landed=True):
            cp.wait_send()
            cp.wait_recv()

    return pl.pallas_call(
        body, name=name,
        out_shape=pltpu.HBM(land_thru.shape, land_thru.dtype),
        in_specs=(HBM_SPEC, SEM_SPEC, SEM_SPEC, ANY), out_specs=HBM_SPEC,
        input_output_aliases={0: 0},
        compiler_params=pltpu.CompilerParams(has_side_effects=pltpu.SideEffectType.DATAFLOW_SIDE_EFFECTING),
    )(land_thru, send_sems, recv_sems, after)


def _exchange(arrs, name):
    n = len(arrs)

    def body(*refs):
        ins, outs = refs[:n], refs[n:2 * n]
        send_sems, recv_sems, local_sems = refs[2 * n:]
        me = _position()

        def copy(a, k):
            peer = _peer(me, k)
            return pltpu.make_async_remote_copy(
                src_ref=ins[a].at[_slot(peer)], dst_ref=outs[a].at[_slot(me)],
                send_sem=send_sems.at[a * 7 + k - 1], recv_sem=recv_sems.at[a * 7 + k - 1],
                device_id=peer, device_id_type=MESH)

        def landed(a, k):
            peer = _peer(me, k)
            return pltpu.make_async_remote_copy(
                src_ref=ins[a].at[_slot(peer)], dst_ref=outs[a].at[_slot(peer)],
                send_sem=send_sems.at[a * 7 + k - 1], recv_sem=recv_sems.at[a * 7 + k - 1],
                device_id=peer, device_id_type=MESH)

        mine = [pltpu.make_async_copy(ins[a].at[_slot(me)], outs[a].at[_slot(me)], local_sems.at[a])
                for a in range(n)]
        for cp in mine:
            cp.start()
        sent = [copy(a, k) for k in range(1, N_DEV) for a in range(n)]
        for cp in sent:
            cp.start()
        for k in range(1, N_DEV):
            for a in range(n):
                landed(a, k).wait_recv()
        for cp in sent:
            cp.wait_send()
        for cp in mine:
            cp.wait()

    return pl.pallas_call(
        body, name=name,
        in_specs=[ANY] * n, out_specs=[ANY] * n,
        out_shape=[jax.ShapeDtypeStruct(a.shape, a.dtype) for a in arrs],
        scratch_shapes=[pltpu.SemaphoreType.DMA((7 * n,)), pltpu.SemaphoreType.DMA((7 * n,)),
                        pltpu.SemaphoreType.DMA((n,))],
    )(*arrs)


def _sum_slots(parts):
    def body(p_ref, o_ref):
        acc = p_ref[0]
        for dev in range(1, N_DEV):
            acc = acc + p_ref[dev]
        o_ref[...] = acc

    return pl.pallas_call(body, name="sum_slots",
                          out_shape=jax.ShapeDtypeStruct(parts.shape[1:], F32))(parts)


def _pack(arrs):
    flat = jnp.concatenate([a.reshape(-1) for a in arrs])
    pad = (-flat.shape[0]) % (8 * LANE)
    return jnp.pad(flat, (0, pad)).reshape(-1, LANE)


def _unpack(packed, shapes):
    flat = packed.reshape(-1)
    out, at = [], 0
    for shp in shapes:
        size = 1
        for dim in shp:
            size *= dim
        out.append(flat[at:at + size].reshape(shp))
        at += size
    return out


def _layer_fwd(x, wt, wo, g_pre, g_post, wa_pad, b_alpha, g_gla, g_att, rb_pad, midway=None):
    h = _rms_fwd(x, g_pre)
    z = _matmul(h, wt, "nt", F32, *TILES["in_proj"], "in_proj", n_outer=True)
    y_gla, o_gla, states = _gla_fwd(z, wa_pad, b_alpha, g_gla)
    if midway is not None:
        g_att = g_att + midway(y_gla)[:1, :1]
    y_att, o_att = _att_fwd(z, rb_pad, g_att)
    ycat = jnp.concatenate([y_gla, y_att], axis=1)
    y = _matmul(ycat, wo, "nn", F32, *TILES["out_proj"], "out_proj", n_outer=True)
    out = _post_fwd(x, y, g_post)
    return out, (x, h, z, o_gla, states, o_att, ycat, y)


def _layer_bwd(dout, saved, wt, wo, g_pre, g_post, wa_pad, b_alpha, g_gla, g_att, rb_pad):
    x, h, z, o_gla, states, o_att, ycat, y = saved
    dy, dg_post = _post_bwd(dout, y, g_post)
    dycat = _matmul(dy, wo, "nt", F32, *TILES["out_proj_dx"], "out_proj_dx", n_outer=True)
    dwo = _matmul(ycat, dy, "tn", BF16, *TILES["out_proj_dw"], "out_proj_dw")
    dq, dk, dv, dgg, dga, dwa, db, dg_gla = _gla_bwd(dycat, o_gla, z, wa_pad, b_alpha, g_gla, states)
    daq, dak, dav, dag, drb, dg_att = _att_bwd(dycat, o_att, z, rb_pad, g_att)
    dz = jnp.concatenate([dq, dk, dv, dgg, daq, dak.astype(BF16), dav.astype(BF16), dag, dga], axis=1)
    dh = _matmul(dz, wt, "nn", F32, *TILES["in_proj_dx"], "in_proj_dx", n_outer=True)
    dwt = _matmul(dz, h, "tn", BF16, *TILES["in_proj_dw"], "in_proj_dw")
    dx, dg_pre = _pre_bwd(dh, x, g_pre, dout)
    small = (dg_pre[0], dg_post[0], dwa[:GLA_RANK], db[0], dg_gla[0], dg_att[0], drb[:, 0, :N_REL])
    return dx, dwt, dwo, small


def kernel(x, w_in, w_out, g_pre, g_post, w_alpha, b_alpha, g_gla, g_att, rel_bias, loss_target, m_w_in, m_w_out, m_g_pre, m_g_post, m_w_alpha, m_b_alpha, m_g_gla, m_g_att, m_rel_bias, v_w_in, v_w_out, v_g_pre, v_g_post, v_w_alpha, v_b_alpha, v_g_gla, v_g_att, v_rel_bias):
    nl, d, cols = w_in.shape
    rows = w_out.shape[1]
    s = x.shape[1]
    x0 = x.reshape(s, d)
    tgt = loss_target.reshape(s, d)

    cols_first = lambda a: jnp.transpose(a, (2, 0, 1))
    w_c = cols_first(w_in)
    pad = jnp.zeros((_slab_rows(rows, cols) - rows - cols, d), BF16)
    shards = [jnp.concatenate([w_out[l].astype(BF16), w_c[:, l].astype(BF16), pad], axis=0) for l in range(nl)]
    wa_g = _exchange([jnp.broadcast_to(_pack([w_alpha])[None], (N_DEV,) + _pack([w_alpha]).shape)], "gather_alpha")[0]
    wa_cols = w_alpha.shape[2]
    wa_full = wa_g.reshape(N_DEV, -1)[:, :nl * GLA_RANK * wa_cols].reshape(N_DEV, nl, GLA_RANK, wa_cols)
    wa_full = jnp.transpose(wa_full, (1, 2, 0, 3)).reshape(nl, GLA_RANK, GLA_KW)
    wa_pad = jnp.pad(wa_full, ((0, 0), (0, LANE - GLA_RANK), (0, 0)))
    rb_pad = jnp.pad(rel_bias, ((0, 0), (0, 0), (0, 3 * LANE - N_REL)))

    def layer_args(l, follows_pre=None, follows_post=None):
        gp = g_pre[l:l + 1] if follows_pre is None else g_pre[l:l + 1] + follows_pre[:1, :1]
        gq = g_post[l:l + 1] if follows_post is None else g_post[l:l + 1] + follows_post[:1, :1]
        return (wts[l], wos[l], gp, gq, wa_pad[l], b_alpha[l:l + 1], g_gla[l:l + 1], g_att[l:l + 1], rb_pad[l])

    my = _slot(_position())

    def fetch(l, after):
        return _exchange_start(shards[l], after, GATHER_PEERS, False, f"gather_start_{l}")

    def relay(l, first_hop, after):
        _, land = _exchange_wait(first_hop[:4], after, GATHER_PEERS, False, f"gather_wait_{l}")
        return _relay_start(land, f"relay_start_{l}")

    def midway(l, y):
        flight["relay"] = relay(l + 1, flight["fetch"], y)
        if l + 2 >= nl:
            return flight["relay"][3]
        flight["fetch"] = fetch(l + 2, flight["relay"][2])
        return flight["fetch"][4]

    act, saved, wts, wos, flight = x0, [], [], [], {}
    flight["fetch"] = fetch(0, x0)
    flight["relay"] = relay(0, flight["fetch"], x0)
    if nl > 1:
        flight["fetch"] = fetch(1, flight["relay"][2])
    for l in range(nl):
        land = _relay_wait(flight["relay"][:3], act, f"relay_wait_{l}")
        land = lax.dynamic_update_slice_in_dim(land, shards[l][None], my, 0)
        wos.append(land[:, :rows].reshape(N_DEV * rows, d))
        wts.append(_aligned_weight(land, rows, cols))
        act, sv = _layer_fwd(act, *layer_args(l),
                             midway=functools.partial(midway, l) if l + 1 < nl else None)
        saved.append(sv)
    dout, sq = _loss_head(act, tgt)
    loss = lax.psum(sq[0, 0] * (0.5 / d), ("x", "y", "c"))

    smalls, pending, token = [None] * nl, [None] * nl, None
    for l in reversed(range(nl)):
        dout, dwt, dwo, smalls[l] = _layer_bwd(dout, saved[l], *layer_args(l, follows_post=token))
        pending[l] = _exchange_start(_partial_slabs(dwo, dwt, rows, cols), dout, ALL_PEERS, True,
                                     f"scatter_start_{l}")
        token = pending[l][4]
    grad_x = dout.reshape(x.shape)

    parts = []
    for l in range(nl):
        partial, land = _exchange_wait(pending[l][:4], dout, ALL_PEERS, True, f"scatter_wait_{l}")
        parts.append(lax.dynamic_update_slice_in_dim(land, lax.dynamic_slice_in_dim(partial, my, 1, 0), my, 0))
    g_w_in, d_w_in, m2_w_in, v2_w_in = [
        jnp.transpose(a, (1, 2, 0))
        for a in _adam_columns(parts, rows // ADAM_COL_ROWS, w_c, cols_first(m_w_in), cols_first(v_w_in))]
    g_w_out, d_w_out, m2_w_out, v2_w_out = _adam_sharded(parts, 0, w_out, m_w_out, v_w_out, "adam_w_out")

    names = 7
    small_stacked = [jnp.stack([smalls[l][i] for l in range(nl)]) for i in range(names)]
    shapes = [a.shape for a in small_stacked]
    packed = _pack(small_stacked)
    gathered = _exchange([jnp.broadcast_to(packed[None], (N_DEV,) + packed.shape)], "gather_small_grads")[0]
    g_pre_g, g_post_g, wa_g_full, b_g, gla_g, att_g, rb_g = _unpack(_sum_slots(gathered), shapes)
    wa_g_mine = lax.dynamic_slice_in_dim(wa_g_full, my * wa_cols, wa_cols, axis=2)
    grads = [g_pre_g, g_post_g, wa_g_mine, b_g, gla_g, att_g, rb_g]
    ws = [g_pre, g_post, w_alpha, b_alpha, g_gla, g_att, rel_bias]
    ms = [m_g_pre, m_g_post, m_w_alpha, m_b_alpha, m_g_gla, m_g_att, m_rel_bias]
    vs = [v_g_pre, v_g_post, v_w_alpha, v_b_alpha, v_g_gla, v_g_att, v_rel_bias]
    shapes2 = [a.shape for a in ws]
    d_s, m2_s, v2_s = _adam_small(_pack(ws), _pack(grads), _pack(ms), _pack(vs))
    d_s, m2_s, v2_s = _unpack(d_s, shapes2), _unpack(m2_s, shapes2), _unpack(v2_s, shapes2)

    def ordered(big_in, big_out, small):
        return [big_in, big_out] + list(small)

    return (loss, grad_x,
            *ordered(g_w_in, g_w_out, grads),
            *ordered(d_w_in, d_w_out, d_s),
            *ordered(m2_w_in, m2_w_out, m2_s),
            *ordered(v2_w_in, v2_w_out, v2_s))
```

```python
import functools

import jax
import jax.numpy as jnp
from jax import lax
from jax.experimental import pallas as pl
from jax.experimental.pallas import tpu as pltpu

F32 = jnp.float32
BF16 = jnp.bfloat16
MESH = pl.DeviceIdType.MESH
ANY = pl.BlockSpec(memory_space=pl.ANY)

CHUNK = 64
GLA_HEADS = 4
GLA_DK = 128
GLA_DV = 256
GLA_KW = GLA_HEADS * GLA_DK
D_GLA = GLA_HEADS * GLA_DV
GLA_RANK = 16
GLA_TAU = 16.0
ATT_HEADS = 8
ATT_HD = 128
D_ATT = ATT_HEADS * ATT_HD
LEFT_CHUNKS = 8
REL_CLIP = 128
N_REL = 2 * REL_CLIP + 1
EPS = 1e-6
D_IN = 2 * GLA_KW + 2 * D_GLA + GLA_RANK + 4 * D_ATT
GLA_SCALE = GLA_DK ** -0.5
ATT_SCALE = ATT_HD ** -0.5

ADAM_LR = 0.001
ADAM_B1 = 0.9
ADAM_B2 = 0.999
ADAM_EPS = 1e-08
ADAM_WD = 0.01
ADAM_STEP = 10

N_DEV = 8
LANE = 128
GA_ORIG = 2 * GLA_KW + 2 * D_GLA
OFF_AQ = GA_ORIG
OFF_GA = GA_ORIG + 4 * D_ATT
D_ZP = OFF_GA + LANE
QB = 2 * CHUNK
ATT_UNROLL = 4
WIN = (LEFT_CHUNKS + 2) * CHUNK
ET_ROWS = WIN + LEFT_CHUNKS * CHUNK
NEG = -1e30
VMEM_LIMIT = 48 * 1024 * 1024


def _cparams(sem):
    return pltpu.CompilerParams(dimension_semantics=sem, vmem_limit_bytes=VMEM_LIMIT)


def _dot(a, b):
    return jnp.dot(a, b, preferred_element_type=F32)


def _dot_nt(a, b):
    return lax.dot_general(a, b, (((1,), (1,)), ((), ())), preferred_element_type=F32)


def _dot_tn(a, b):
    return lax.dot_general(a, b, (((0,), (0,)), ((), ())), preferred_element_type=F32)


def _dot01(t, x, left=True):
    if not left:
        t, x = x, t
    hi = x.astype(BF16)
    r = x - hi.astype(F32)
    mid = r.astype(BF16)
    lo = (r - mid.astype(F32)).astype(BF16)
    if left:
        return _dot(t, hi) + _dot(t, mid) + _dot(t, lo)
    return _dot(hi, t) + _dot(mid, t) + _dot(lo, t)


def _sigmoid(x):
    return 1.0 / (1.0 + jnp.exp(-x))


def _log_sigmoid(x):
    return jnp.minimum(x, 0.0) - jnp.log(1.0 + jnp.exp(-jnp.abs(x)))


TILES = {
    "in_proj": (512, D_ZP // 3, None),
    "in_proj_dx": (512, 512, None),
    "in_proj_dw": (D_ZP // 3, 512, None),
    "out_proj": (512, 1024, None),
    "out_proj_dx": (512, 1024, None),
    "out_proj_dw": (1024, 1024, None),
}


def _matmul(a, b, mode, out_dtype, tm, tn, tk, name, n_outer=False, after=None):
    if mode == "nn":
        (m, k), n = a.shape, b.shape[1]
    elif mode == "nt":
        (m, k), n = a.shape, b.shape[0]
    else:
        (k, m), n = a.shape, b.shape[1]
    tm, tn, tk = min(tm, m), min(tn, n), k if tk is None else min(tk, k)
    assert m % tm == 0 and n % tn == 0 and k % tk == 0, (name, m, n, k)
    nk = k // tk
    dot = {"nn": _dot, "nt": _dot_nt, "tn": _dot_tn}[mode]

    follows = [] if after is None else [after]

    def body_whole_k(a_ref, b_ref, *rest):
        o_ref = rest[-1]
        o_ref[...] = dot(a_ref[...], b_ref[...]).astype(out_dtype)

    def body(a_ref, b_ref, *rest):
        o_ref, acc_ref = rest[-2:]
        kk = pl.program_id(2)

        @pl.when(kk == 0)
        def _():
            acc_ref[...] = jnp.zeros_like(acc_ref)

        acc_ref[...] += dot(a_ref[...], b_ref[...])

        @pl.when(kk == nk - 1)
        def _():
            o_ref[...] = acc_ref[...].astype(out_dtype)

    def at(index):
        return (lambda j, i, kk: index(i, j, kk)) if n_outer else index

    if mode == "tn":
        a_spec = pl.BlockSpec((tk, tm), at(lambda i, j, kk: (kk, i)))
    else:
        a_spec = pl.BlockSpec((tm, tk), at(lambda i, j, kk: (i, kk)))
    if mode == "nt":
        b_spec = pl.BlockSpec((tn, tk), at(lambda i, j, kk: (j, kk)))
    else:
        b_spec = pl.BlockSpec((tk, tn), at(lambda i, j, kk: (kk, j)))
    return pl.pallas_call(
        body_whole_k if nk == 1 else body, name=name,
        grid=(n // tn, m // tm, nk) if n_outer else (m // tm, n // tn, nk),
        in_specs=[a_spec, b_spec] + [ANY] * len(follows),
        out_specs=pl.BlockSpec((tm, tn), at(lambda i, j, kk: (i, j))),
        out_shape=jax.ShapeDtypeStruct((m, n), out_dtype),
        scratch_shapes=[] if nk == 1 else [pltpu.VMEM((tm, tn), F32)],
        compiler_params=_cparams(("parallel", "parallel", "arbitrary")),
    )(a, b, *follows)


ROWS = 256


def _rms_fwd(x, g):
    s, d = x.shape

    def body(x_ref, g_ref, h_ref):
        xv = x_ref[...]
        r = lax.rsqrt(jnp.mean(xv * xv, axis=-1, keepdims=True) + EPS)
        h_ref[...] = (xv * r * g_ref[...]).astype(BF16)

    return pl.pallas_call(
        body, name="rms_fwd", grid=(s // ROWS,),
        in_specs=[pl.BlockSpec((ROWS, d), lambda i: (i, 0)), pl.BlockSpec((1, d), lambda i: (0, 0))],
        out_specs=pl.BlockSpec((ROWS, d), lambda i: (i, 0)),
        out_shape=jax.ShapeDtypeStruct((s, d), BF16),
        compiler_params=_cparams(("parallel",)),
    )(x, g)


def _post_fwd(x, y, g):
    s, d = x.shape

    def body(x_ref, y_ref, g_ref, o_ref):
        yv = y_ref[...]
        r = lax.rsqrt(jnp.mean(yv * yv, axis=-1, keepdims=True) + EPS)
        o_ref[...] = x_ref[...] + yv * r * g_ref[...]

    row = pl.BlockSpec((ROWS, d), lambda i: (i, 0))
    return pl.pallas_call(
        body, name="post_fwd", grid=(s // ROWS,),
        in_specs=[row, row, pl.BlockSpec((1, d), lambda i: (0, 0))],
        out_specs=row,
        out_shape=jax.ShapeDtypeStruct((s, d), F32),
        compiler_params=_cparams(("parallel",)),
    )(x, y, g)


def _loss_head(out, tgt):
    s, d = out.shape

    def body(o_ref, t_ref, dout_ref, sum_ref):
        @pl.when(pl.program_id(0) == 0)
        def _():
            sum_ref[...] = jnp.zeros_like(sum_ref)

        e = o_ref[...] - t_ref[...]
        dout_ref[...] = e * (1.0 / d)
        sum_ref[...] += jnp.sum(jnp.sum(e * e, axis=1, keepdims=True), axis=0, keepdims=True)

    row = pl.BlockSpec((ROWS, d), lambda i: (i, 0))
    return pl.pallas_call(
        body, name="loss_head", grid=(s // ROWS,),
        in_specs=[row, row],
        out_specs=[row, pl.BlockSpec((1, 1), lambda i: (0, 0))],
        out_shape=[jax.ShapeDtypeStruct((s, d), F32), jax.ShapeDtypeStruct((1, 1), F32)],
        compiler_params=_cparams(("arbitrary",)),
    )(out, tgt)


def _post_bwd(dout, y, g):
    s, d = y.shape

    def body(do_ref, y_ref, g_ref, dy_ref, dg_ref):
        @pl.when(pl.program_id(0) == 0)
        def _():
            dg_ref[...] = jnp.zeros_like(dg_ref)

        yv = y_ref[...]
        dv = do_ref[...]
        r = lax.rsqrt(jnp.mean(yv * yv, axis=-1, keepdims=True) + EPS)
        dg_ref[...] += jnp.sum(dv * yv * r, axis=0, keepdims=True)
        w = dv * g_ref[...]
        dy = r * (w - yv * (r * r) * jnp.mean(w * yv, axis=-1, keepdims=True))
        dy_ref[...] = dy.astype(BF16)

    row = pl.BlockSpec((ROWS, d), lambda i: (i, 0))
    vec = pl.BlockSpec((1, d), lambda i: (0, 0))
    return pl.pallas_call(
        body, name="post_bwd", grid=(s // ROWS,),
        in_specs=[row, row, vec],
        out_specs=[row, vec],
        out_shape=[jax.ShapeDtypeStruct((s, d), BF16), jax.ShapeDtypeStruct((1, d), F32)],
        compiler_params=_cparams(("arbitrary",)),
    )(dout, y, g)


def _pre_bwd(dh, x, g, dout):
    s, d = x.shape

    def body(dh_ref, x_ref, g_ref, do_ref, dx_ref, dg_ref):
        @pl.when(pl.program_id(0) == 0)
        def _():
            dg_ref[...] = jnp.zeros_like(dg_ref)

        xv = x_ref[...]
        dv = dh_ref[...]
        r = lax.rsqrt(jnp.mean(xv * xv, axis=-1, keepdims=True) + EPS)
        dg_ref[...] += jnp.sum(dv * xv * r, axis=0, keepdims=True)
        w = dv * g_ref[...]
        dx_ref[...] = do_ref[...] + r * (w - xv * (r * r) * jnp.mean(w * xv, axis=-1, keepdims=True))

    row = pl.BlockSpec((ROWS, d), lambda i: (i, 0))
    vec = pl.BlockSpec((1, d), lambda i: (0, 0))
    return pl.pallas_call(
        body, name="pre_bwd", grid=(s // ROWS,),
        in_specs=[row, row, vec, row],
        out_specs=[row, vec],
        out_shape=[jax.ShapeDtypeStruct((s, d), F32), jax.ShapeDtypeStruct((1, d), F32)],
        compiler_params=_cparams(("arbitrary",)),
    )(dh, x, g, dout)


def _gla_gate(ga_b, wa_b, b_ref, tri):
    pre = _dot(ga_b, wa_b) + b_ref[...]
    la = _log_sigmoid(pre) * (1.0 / GLA_TAU)
    cum = _dot01(tri, la)
    last = lax.broadcasted_iota(jnp.int32, cum.shape, 0) == CHUNK - 1
    return pre, cum, jnp.sum(jnp.where(last, cum, 0.0), axis=0, keepdims=True)


def _heads(width):
    return [slice(h * width, (h + 1) * width) for h in range(GLA_HEADS)]


def _z_specs_gla(rev=None):
    idx = (lambda n: n) if rev is None else rev
    return [
        pl.BlockSpec((CHUNK, GLA_KW), lambda n: (idx(n), 0)),
        pl.BlockSpec((CHUNK, GLA_KW), lambda n: (idx(n), 1)),
        pl.BlockSpec((CHUNK, D_GLA), lambda n: (idx(n), 1)),
        pl.BlockSpec((CHUNK, D_GLA), lambda n: (idx(n), 2)),
        pl.BlockSpec((CHUNK, LANE), lambda n: (idx(n), OFF_GA // LANE)),
    ]


def _gla_fwd(z, wa_pad, b_alpha, g_gla):
    s = z.shape[0]
    nchunk = s // CHUNK

    def body(q_ref, k_ref, v_ref, gg_ref, ga_ref, wa_ref, b_ref, g_ref, y_ref, o_ref, st_ref, state):
        @pl.when(pl.program_id(0) == 0)
        def _():
            state[...] = jnp.zeros_like(state)

        ga_b = ga_ref[...].astype(BF16)
        ri = lax.broadcasted_iota(jnp.int32, (CHUNK, CHUNK), 0)
        ci = lax.broadcasted_iota(jnp.int32, (CHUNK, CHUNK), 1)
        tri = jnp.where(ri >= ci, 1.0, 0.0).astype(BF16)
        _, cum, cend = _gla_gate(ga_b, wa_ref[...].astype(BF16), b_ref, tri)
        kd_b = (k_ref[...] * jnp.exp(cend - cum)).astype(BF16)
        a = jnp.exp(cend)
        qs = (q_ref[...] * GLA_SCALE).astype(BF16)
        v_b = v_ref[...].astype(BF16)
        keys, vals = _heads(GLA_DK), _heads(GLA_DV)
        sts = [state[h] * a[:, keys[h]] + _dot_tn(v_b[:, vals[h]], kd_b[:, keys[h]]) for h in range(GLA_HEADS)]
        for h in range(GLA_HEADS):
            state[h] = sts[h]
            st_ref[0, h] = sts[h]
        outs = [_dot_nt(qs[:, keys[h]], sts[h].astype(BF16)) for h in range(GLA_HEADS)]
        for h in range(GLA_HEADS):
            o, vs = outs[h], vals[h]
            o_ref[:, vs] = o
            r = lax.rsqrt(jnp.mean(o * o, axis=-1, keepdims=True) + EPS)
            gg = gg_ref[:, vs]
            y_ref[:, vs] = (o * r * g_ref[:, vs] * (gg * _sigmoid(gg))).astype(BF16)

    full = lambda shape: pl.BlockSpec(shape, lambda n: tuple(0 for _ in shape))
    wide = pl.BlockSpec((CHUNK, D_GLA), lambda n: (n, 0))
    return pl.pallas_call(
        body, name="gla_fwd", grid=(nchunk,),
        in_specs=_z_specs_gla() + [full((LANE, GLA_KW)), full((1, GLA_KW)), full((1, D_GLA))],
        out_specs=[wide, wide, pl.BlockSpec((1, GLA_HEADS, GLA_DV, GLA_DK), lambda n: (n, 0, 0, 0))],
        out_shape=[jax.ShapeDtypeStruct((s, D_GLA), BF16), jax.ShapeDtypeStruct((s, D_GLA), F32),
                   jax.ShapeDtypeStruct((nchunk, GLA_HEADS, GLA_DV, GLA_DK), F32)],
        scratch_shapes=[pltpu.VMEM((GLA_HEADS, GLA_DV, GLA_DK), F32)],
        compiler_params=_cparams(("arbitrary",)),
    )(z, z, z, z, z, wa_pad, b_alpha, g_gla)


def _gla_bwd(dyc, o_gla, z, wa_pad, b_alpha, g_gla, states):
    s = z.shape[0]
    nchunk = s // CHUNK
    rev = lambda n: nchunk - 1 - n

    def body(dy_ref, o_ref, q_ref, k_ref, v_ref, gg_ref, ga_ref, wa_ref, b_ref, g_ref, st_ref, stp_ref,
             dq_ref, dk_ref, dv_ref, dgg_ref, dga_ref, dwa_ref, db_ref, dg_ref, carry):
        step = pl.program_id(0)

        @pl.when(step == 0)
        def _():
            carry[...] = jnp.zeros_like(carry)
            dwa_ref[...] = jnp.zeros_like(dwa_ref)
            db_ref[...] = jnp.zeros_like(db_ref)
            dg_ref[...] = jnp.zeros_like(dg_ref)

        has_prev = (step < nchunk - 1).astype(F32)
        ga_b = ga_ref[...].astype(BF16)
        ri = lax.broadcasted_iota(jnp.int32, (CHUNK, CHUNK), 0)
        ci = lax.broadcasted_iota(jnp.int32, (CHUNK, CHUNK), 1)
        tri = jnp.where(ri >= ci, 1.0, 0.0).astype(BF16)
        tri_up = jnp.where(ci >= ri, 1.0, 0.0).astype(BF16)
        nh = range(GLA_HEADS)
        keys, vals = _heads(GLA_DK), _heads(GLA_DV)
        wa_b = wa_ref[...].astype(BF16)
        pre, cum, cend = _gla_gate(ga_b, wa_b, b_ref, tri)
        e = jnp.exp(cend - cum)
        a = jnp.exp(cend)
        kf = k_ref[...]
        kd_b = (kf * e).astype(BF16)
        v_b = v_ref[...].astype(BF16)
        qs = (q_ref[...] * GLA_SCALE).astype(BF16)
        do_b = []
        for h in nh:
            vs = vals[h]
            o = o_ref[:, vs]
            gg = gg_ref[:, vs]
            g = g_ref[:, vs]
            dy = dy_ref[:, vs]
            r = lax.rsqrt(jnp.mean(o * o, axis=-1, keepdims=True) + EPS)
            sg = _sigmoid(gg)
            dogn = dy * (gg * sg)
            dgg_ref[:, vs] = (dy * (o * r * g) * (sg * (1.0 + gg * (1.0 - sg)))).astype(BF16)
            dg_ref[:, vs] += jnp.sum(dogn * o * r, axis=0, keepdims=True)
            w = dogn * g
            do_b.append((r * (w - o * (r * r) * jnp.mean(w * o, axis=-1, keepdims=True))).astype(BF16))
        dqs = [_dot(do_b[h], st_ref[0, h].astype(BF16)) for h in nh]
        gts = [_dot_tn(do_b[h], qs[:, keys[h]]) + carry[h] for h in nh]
        dq_ref[...] = (jnp.concatenate(dqs, axis=1) * GLA_SCALE).astype(BF16)
        gt_b = [gt.astype(BF16) for gt in gts]
        dkd = jnp.concatenate([_dot(v_b[:, vals[h]], gt_b[h]) for h in nh], axis=1)
        dvs = [_dot_nt(kd_b[:, keys[h]], gt_b[h]) for h in nh]
        da = jnp.concatenate([jnp.sum(gts[h] * (stp_ref[0, h] * has_prev), axis=0, keepdims=True) for h in nh],
                             axis=1)
        for h in nh:
            dv_ref[:, vals[h]] = dvs[h].astype(BF16)
            carry[h] = gts[h] * a[:, keys[h]]
        dk_ref[...] = (dkd * e).astype(BF16)
        dd = dkd * kf * e
        dcend = jnp.sum(dd, axis=0, keepdims=True) + da * a
        dla = dcend - _dot01(tri_up, dd)
        dpre = dla * (1.0 / GLA_TAU) * (1.0 - _sigmoid(pre))
        dpre_b = dpre.astype(BF16)
        dga_ref[...] = _dot_nt(dpre_b, wa_b).astype(BF16)
        dwa_ref[...] += _dot_tn(ga_b, dpre_b)
        db_ref[...] += jnp.sum(dpre, axis=0, keepdims=True)

    full = lambda shape: pl.BlockSpec(shape, lambda n: tuple(0 for _ in shape))
    wide = pl.BlockSpec((CHUNK, D_GLA), lambda n: (rev(n), 0))
    keyw = pl.BlockSpec((CHUNK, GLA_KW), lambda n: (rev(n), 0))
    st_spec = pl.BlockSpec((1, GLA_HEADS, GLA_DV, GLA_DK), lambda n: (rev(n), 0, 0, 0))
    stp_spec = pl.BlockSpec((1, GLA_HEADS, GLA_DV, GLA_DK), lambda n: (jnp.maximum(rev(n) - 1, 0), 0, 0, 0))
    return pl.pallas_call(
        body, name="gla_bwd", grid=(nchunk,),
        in_specs=[wide, wide] + _z_specs_gla(rev)
        + [full((LANE, GLA_KW)), full((1, GLA_KW)), full((1, D_GLA)), st_spec, stp_spec],
        out_specs=[keyw, keyw, wide, wide, pl.BlockSpec((CHUNK, LANE), lambda n: (rev(n), 0)),
                   full((LANE, GLA_KW)), full((1, GLA_KW)), full((1, D_GLA))],
        out_shape=[jax.ShapeDtypeStruct((s, GLA_KW), BF16), jax.ShapeDtypeStruct((s, GLA_KW), BF16),
                   jax.ShapeDtypeStruct((s, D_GLA), BF16), jax.ShapeDtypeStruct((s, D_GLA), BF16),
                   jax.ShapeDtypeStruct((s, LANE), BF16),
                   jax.ShapeDtypeStruct((LANE, GLA_KW), F32), jax.ShapeDtypeStruct((1, GLA_KW), F32),
                   jax.ShapeDtypeStruct((1, D_GLA), F32)],
        scratch_shapes=[pltpu.VMEM((GLA_HEADS, GLA_DV, GLA_DK), F32)],
        compiler_params=_cparams(("arbitrary",)),
    )(dyc, o_gla, z, z, z, z, z, wa_pad, b_alpha, g_gla, states, states)


def _build_bias_table(rb_row, et_ref):
    far = jnp.broadcast_to(rb_row[:, 2 * REL_CLIP:2 * REL_CLIP + 1], (1, LANE))
    near_hi = rb_row[:, REL_CLIP:2 * REL_CLIP]
    near_lo = rb_row[:, 0:REL_CLIP]
    past = jnp.broadcast_to(rb_row[:, 0:1], (1, LANE))
    seg = [far, far, far, far, near_hi, near_lo] + [past] * (ET_ROWS // LANE - 5)
    ri = lax.broadcasted_iota(jnp.int32, (LANE, LANE), 0)
    ci = lax.broadcasted_iota(jnp.int32, (LANE, LANE), 1)
    for kb in range(ET_ROWS // LANE):
        wmat = jnp.where(ri + ci < LANE, seg[kb], seg[kb + 1])
        blk = pltpu.roll(wmat, 0, 1, stride=1, stride_axis=0)
        lag = LEFT_CHUNKS + ci // CHUNK - (2 * kb + ri // CHUNK)
        et_ref[kb * LANE:(kb + 1) * LANE, :] = jnp.where((lag >= 0) & (lag <= LEFT_CHUNKS), blk, NEG)


def _reduce_bias_table(det_ref):
    lane = lax.broadcasted_iota(jnp.int32, (1, LANE), 1)
    ri = lax.broadcasted_iota(jnp.int32, (LANE, LANE), 0)
    ci = lax.broadcasted_iota(jnp.int32, (LANE, LANE), 1)
    flip = jnp.where(ri + ci == LANE - 1, 1.0, 0.0).astype(BF16)
    segs = jnp.zeros((8, LANE), F32)
    seg_row = lax.broadcasted_iota(jnp.int32, (8, LANE), 0)
    prev_minus = jnp.zeros((1, LANE), F32)
    for kb in range(6):
        rolled = pltpu.roll(_dot01(det_ref[kb * LANE:(kb + 1) * LANE, :], flip, left=False), 0, 1,
                            stride=1, stride_axis=0)
        plus = jnp.sum(jnp.where(ci >= ri, rolled, 0.0), axis=0, keepdims=True)
        minus = jnp.sum(jnp.where(ci < ri, rolled, 0.0), axis=0, keepdims=True)
        segs = segs + jnp.where(seg_row == kb, plus + prev_minus, 0.0)
        prev_minus = minus
    segs = _dot01(segs, flip, left=False)
    pick = lambda kb: jnp.sum(jnp.where(seg_row == kb, segs, 0.0), axis=0, keepdims=True)
    far = jnp.sum(pick(0) + pick(1) + pick(2) + pick(3), axis=1, keepdims=True)
    last = jnp.where(lane == 0, far, 0.0)
    return jnp.concatenate([pick(5), pick(4), last], axis=1)


def _att_window(b):
    c0 = 2 * b
    kstart = pl.multiple_of(jnp.maximum(c0 - LEFT_CHUNKS, 0) * CHUNK, CHUNK)
    eoff = pl.multiple_of(jnp.maximum(LEFT_CHUNKS - c0, 0) * CHUNK, CHUNK)
    return kstart, eoff


def _att_probs(q_b, kw_b, et):
    st = _dot_nt(kw_b, q_b) * ATT_SCALE + et
    m = jnp.max(st, axis=0, keepdims=True)
    ex = jnp.exp(st - m)
    return ex * (1.0 / jnp.sum(ex, axis=0, keepdims=True))


def _att_fwd(z, rb_pad, g_att):
    s = z.shape[0]
    nblk = s // QB
    c_aq, c_ak, c_av, c_ag = [(OFF_AQ + i * D_ATT) // ATT_HD for i in range(4)]

    def body(q_ref, k_ref, v_ref, ag_ref, rb_ref, g_ref, y_ref, o_ref, et_ref):
        h = pl.program_id(0)
        b = pl.program_id(1)

        @pl.when(b == 0)
        def _():
            _build_bias_table(rb_ref[pl.ds(h, 1), :], et_ref)

        for j in range(ATT_UNROLL):
            rs = slice(j * QB, (j + 1) * QB)
            kstart, eoff = _att_window(b * ATT_UNROLL + j)
            q_b = q_ref[rs, :].astype(BF16)
            kw_b = k_ref[pl.ds(kstart, WIN), :].astype(BF16)
            vw_b = v_ref[pl.ds(kstart, WIN), :].astype(BF16)
            pt = _att_probs(q_b, kw_b, et_ref[pl.ds(eoff, WIN), :])
            o = _dot_tn(pt.astype(BF16), vw_b)
            o_ref[rs, :] = o
            r = lax.rsqrt(jnp.mean(o * o, axis=-1, keepdims=True) + EPS)
            ag = ag_ref[rs, :]
            y_ref[rs, :] = (o * r * g_ref[...] * (ag * _sigmoid(ag))).astype(BF16)

    blk = lambda col: pl.BlockSpec((ATT_UNROLL * QB, ATT_HD), lambda h, b: (b, col + h))
    seq = lambda col: pl.BlockSpec((s, ATT_HD), lambda h, b: (0, col + h))
    out_blk = pl.BlockSpec((ATT_UNROLL * QB, ATT_HD), lambda h, b: (b, h))
    return pl.pallas_call(
        body, name="att_fwd", grid=(ATT_HEADS, nblk // ATT_UNROLL),
        in_specs=[blk(c_aq), seq(c_ak), seq(c_av), blk(c_ag),
                  pl.BlockSpec((ATT_HEADS, 3 * LANE), lambda h, b: (0, 0)),
                  pl.BlockSpec((1, ATT_HD), lambda h, b: (0, h))],
        out_specs=[out_blk, out_blk],
        out_shape=[jax.ShapeDtypeStruct((s, D_ATT), BF16), jax.ShapeDtypeStruct((s, D_ATT), F32)],
        scratch_shapes=[pltpu.VMEM((ET_ROWS, LANE), F32)],
        compiler_params=_cparams(("arbitrary", "arbitrary")),
    )(z, z, z, z, rb_pad, g_att)


def _att_bwd(dyc, o_att, z, rb_pad, g_att):
    s = z.shape[0]
    nblk = s // QB
    c_aq, c_ak, c_av, c_ag = [(OFF_AQ + i * D_ATT) // ATT_HD for i in range(4)]
    c_dy = D_GLA // ATT_HD

    def body(dy_ref, o_ref, q_ref, k_ref, v_ref, ag_ref, rb_ref, g_ref,
             dq_ref, dk_ref, dv_ref, dag_ref, drb_ref, dg_ref, et_ref, det_ref):
        h = pl.program_id(0)
        b = pl.program_id(1)

        @pl.when(b == 0)
        def _():
            _build_bias_table(rb_ref[pl.ds(h, 1), :], et_ref)
            det_ref[...] = jnp.zeros_like(det_ref)
            dk_ref[...] = jnp.zeros_like(dk_ref)
            dv_ref[...] = jnp.zeros_like(dv_ref)
            dg_ref[...] = jnp.zeros_like(dg_ref)

        g = g_ref[...]
        dg = jnp.zeros((1, ATT_HD), F32)
        for j in range(ATT_UNROLL):
            rs = slice(j * QB, (j + 1) * QB)
            kstart, eoff = _att_window(b * ATT_UNROLL + j)
            q_b = q_ref[rs, :].astype(BF16)
            kw_b = k_ref[pl.ds(kstart, WIN), :].astype(BF16)
            vw_b = v_ref[pl.ds(kstart, WIN), :].astype(BF16)
            pt = _att_probs(q_b, kw_b, et_ref[pl.ds(eoff, WIN), :])
            o = o_ref[rs, :]
            ag = ag_ref[rs, :]
            dy = dy_ref[rs, :]
            r = lax.rsqrt(jnp.mean(o * o, axis=-1, keepdims=True) + EPS)
            sg = _sigmoid(ag)
            don = dy * (ag * sg)
            dag_ref[rs, :] = (dy * (o * r * g) * (sg * (1.0 + ag * (1.0 - sg)))).astype(BF16)
            dg = dg + jnp.sum(don * o * r, axis=0, keepdims=True)
            w = don * g
            do_b = (r * (w - o * (r * r) * jnp.mean(w * o, axis=-1, keepdims=True))).astype(BF16)
            pt_b = pt.astype(BF16)
            dpt = _dot_nt(vw_b, do_b)
            dst = pt * (dpt - jnp.sum(dpt * pt, axis=0, keepdims=True))
            det_ref[pl.ds(eoff, WIN), :] += dst
            ds_b = (dst * ATT_SCALE).astype(BF16)
            dq_ref[rs, :] = _dot_tn(ds_b, kw_b).astype(BF16)
            dk_ref[pl.ds(kstart, WIN), :] += _dot(ds_b, q_b)
            dv_ref[pl.ds(kstart, WIN), :] += _dot(pt_b, do_b)
        dg_ref[...] += dg

        @pl.when(b == nblk // ATT_UNROLL - 1)
        def _():
            drb_ref[0] = jnp.broadcast_to(_reduce_bias_table(det_ref), (8, 3 * LANE))

    blk = lambda col: pl.BlockSpec((ATT_UNROLL * QB, ATT_HD), lambda h, b: (b, col + h))
    seq = lambda col: pl.BlockSpec((s, ATT_HD), lambda h, b: (0, col + h))
    out_blk = pl.BlockSpec((ATT_UNROLL * QB, ATT_HD), lambda h, b: (b, h))
    out_seq = pl.BlockSpec((s, ATT_HD), lambda h, b: (0, h))
    return pl.pallas_call(
        body, name="att_bwd", grid=(ATT_HEADS, nblk // ATT_UNROLL),
        in_specs=[blk(c_dy), blk(0), blk(c_aq), seq(c_ak), seq(c_av), blk(c_ag),
                  pl.BlockSpec((ATT_HEADS, 3 * LANE), lambda h, b: (0, 0)),
                  pl.BlockSpec((1, ATT_HD), lambda h, b: (0, h))],
        out_specs=[out_blk, out_seq, out_seq, out_blk,
                   pl.BlockSpec((1, 8, 3 * LANE), lambda h, b: (h, 0, 0)),
                   pl.BlockSpec((1, ATT_HD), lambda h, b: (0, h))],
        out_shape=[jax.ShapeDtypeStruct((s, D_ATT), BF16), jax.ShapeDtypeStruct((s, D_ATT), F32),
                   jax.ShapeDtypeStruct((s, D_ATT), F32), jax.ShapeDtypeStruct((s, D_ATT), BF16),
                   jax.ShapeDtypeStruct((ATT_HEADS, 8, 3 * LANE), F32),
                   jax.ShapeDtypeStruct((1, D_ATT), F32)],
        scratch_shapes=[pltpu.VMEM((ET_ROWS, LANE), F32), pltpu.VMEM((ET_ROWS, LANE), F32)],
        compiler_params=_cparams(("arbitrary", "arbitrary")),
    )(dyc, o_att, z, z, z, z, rb_pad, g_att)


ADAM_ROWS = 64
ADAM_COL_ROWS = 32


def _adam_math(w, g, m, v):
    m2 = ADAM_B1 * m + (1.0 - ADAM_B1) * g
    v2 = ADAM_B2 * v + (1.0 - ADAM_B2) * (g * g)
    m_hat = m2 / (1.0 - ADAM_B1 ** ADAM_STEP)
    v_hat = v2 / (1.0 - ADAM_B2 ** ADAM_STEP)
    delta = -ADAM_LR * (m_hat / (jnp.sqrt(v_hat) + ADAM_EPS) + ADAM_WD * w)
    return delta, m2, v2


def _adam_sharded(parts, first, w, m, v, name):
    nl, nr, nc = w.shape

    def body(*refs):
        p_refs = refs[:nl]
        w_ref, m_ref, v_ref, g_ref, d_ref, m2_ref, v2_ref = refs[nl:]
        for k in range(nl):
            @pl.when(pl.program_id(0) == k)
            def _(p_ref=p_refs[k]):
                g = p_ref[0].astype(F32)
                for dev in range(1, N_DEV):
                    g = g + p_ref[dev].astype(F32)
                delta, m2, v2 = _adam_math(w_ref[0], g, m_ref[0], v_ref[0])
                g_ref[0] = g
                d_ref[0] = delta
                m2_ref[0] = m2
                v2_ref[0] = v2

    def part_spec(k):
        return pl.BlockSpec((N_DEV, ADAM_ROWS, nc), lambda l, i: (0, first + jnp.where(l == k, i, 0), 0))

    blk = pl.BlockSpec((1, ADAM_ROWS, nc), lambda l, i: (l, i, 0))
    shp = jax.ShapeDtypeStruct(w.shape, F32)
    return pl.pallas_call(
        body, name=name, grid=(nl, pl.cdiv(nr, ADAM_ROWS)),
        in_specs=[part_spec(k) for k in range(nl)] + [blk, blk, blk],
        out_specs=[blk, blk, blk, blk],
        out_shape=[shp, shp, shp, shp],
        compiler_params=_cparams(("arbitrary", "arbitrary")),
    )(*parts, w, m, v)


def _adam_columns(parts, first, w, m, v):
    nc, nl, d = w.shape

    def body(*refs):
        p_refs = refs[:nl]
        w_ref, m_ref, v_ref, g_ref, d_ref, m2_ref, v2_ref = refs[nl:]
        for l in range(nl):
            g = p_refs[l][0].astype(F32)
            for dev in range(1, N_DEV):
                g = g + p_refs[l][dev].astype(F32)
            delta, m2, v2 = _adam_math(w_ref[:, l, :], g, m_ref[:, l, :], v_ref[:, l, :])
            g_ref[:, l, :] = g
            d_ref[:, l, :] = delta
            m2_ref[:, l, :] = m2
            v2_ref[:, l, :] = v2

    blk = pl.BlockSpec((ADAM_COL_ROWS, nl, d), lambda i: (i, 0, 0))
    part = pl.BlockSpec((N_DEV, ADAM_COL_ROWS, d), lambda i: (0, first + i, 0))
    shp = jax.ShapeDtypeStruct(w.shape, F32)
    return pl.pallas_call(
        body, name="adam_w_in", grid=(pl.cdiv(nc, ADAM_COL_ROWS),),
        in_specs=[part] * nl + [blk, blk, blk],
        out_specs=[blk, blk, blk, blk],
        out_shape=[shp, shp, shp, shp],
        compiler_params=_cparams(("parallel",)),
    )(*parts, w, m, v)


def _adam_small(w, g, m, v):
    def body(w_ref, g_ref, m_ref, v_ref, d_ref, m2_ref, v2_ref):
        delta, m2, v2 = _adam_math(w_ref[...], g_ref[...], m_ref[...], v_ref[...])
        d_ref[...] = delta
        m2_ref[...] = m2
        v2_ref[...] = v2

    shp = jax.ShapeDtypeStruct(w.shape, F32)
    return pl.pallas_call(body, name="adam_small", out_shape=[shp, shp, shp])(w, g, m, v)


def _position():
    return lax.axis_index("x"), lax.axis_index("y"), lax.axis_index("c")


def _slot(p):
    return 4 * p[0] + 2 * p[1] + p[2]


BF16_TILE_ROWS = 16


def _slab_rows(rows, cols):
    return -(-(rows + cols) // BF16_TILE_ROWS) * BF16_TILE_ROWS


RELAYOUT_COLS = 512
RELAYOUT_CHUNK = 64


def _shard_pieces(dev, rows, cols):
    moved = ((0, GA_ORIG, 0), (GA_ORIG, GA_ORIG + GLA_RANK, OFF_GA - GA_ORIG), (GA_ORIG + GLA_RANK, D_IN, -GLA_RANK))
    c0, c1 = dev * cols, (dev + 1) * cols
    return [(rows + max(c0, lo) - c0, max(c0, lo) + off, min(c1, hi) - max(c0, lo))
            for lo, hi, off in moved if max(c0, lo) < min(c1, hi)]


def _move_rows(src, src_row, dst, dst_row, n):
    assert src_row % 2 == 0 and dst_row % 2 == 0 and n % 2 == 0
    for r in range(0, n // 2, RELAYOUT_CHUNK):
        m = min(RELAYOUT_CHUNK, n // 2 - r)
        dst[dst_row // 2 + r:dst_row // 2 + r + m, :] = src[src_row // 2 + r:src_row // 2 + r + m, :]


def _aligned_weight(land, rows, cols):
    _, slab, d = land.shape
    ct = min(RELAYOUT_COLS, d)

    def body(land_ref, wt_ref):
        dev = pl.program_id(1)
        src = land_ref.bitcast(jnp.uint32)
        dst = wt_ref.bitcast(jnp.uint32)

        @pl.when(dev == 0)
        def _():
            dst[D_IN // 2:D_ZP // 2, :] = jnp.zeros(((D_ZP - D_IN) // 2, ct), jnp.uint32)

        for k in range(N_DEV):
            @pl.when(dev == k)
            def _(k=k):
                for at, to, n in _shard_pieces(k, rows, cols):
                    _move_rows(src, at, dst, to, n)

    return pl.pallas_call(
        body, name="aligned_weight", grid=(d // ct, N_DEV),
        in_specs=[pl.BlockSpec((slab, ct), lambda c, dev: (dev, c))],
        out_specs=pl.BlockSpec((D_ZP, ct), lambda c, dev: (0, c)),
        out_shape=jax.ShapeDtypeStruct((D_ZP, d), land.dtype),
        compiler_params=_cparams(("parallel", "arbitrary")),
    )(land.reshape(N_DEV * slab, d))


def _partial_slabs(dwo, dwt, rows, cols):
    d = dwt.shape[1]
    slab = _slab_rows(rows, cols)
    ct = min(RELAYOUT_COLS, d)

    def body(dwo_ref, dwt_ref, out_ref):
        dev = pl.program_id(1)
        src = dwt_ref.bitcast(jnp.uint32)
        dst = out_ref.bitcast(jnp.uint32)
        out_ref[0:rows, :] = dwo_ref[...]
        dst[(rows + cols) // 2:slab // 2, :] = jnp.zeros(((slab - rows - cols) // 2, ct), jnp.uint32)
        for k in range(N_DEV):
            @pl.when(dev == k)
            def _(k=k):
                for to, at, n in _shard_pieces(k, rows, cols):
                    _move_rows(src, at, dst, to, n)

    return pl.pallas_call(
        body, name="partial_slabs", grid=(d // ct, N_DEV),
        in_specs=[pl.BlockSpec((rows, ct), lambda c, dev: (dev, c)),
                  pl.BlockSpec((D_ZP, ct), lambda c, dev: (0, c))],
        out_specs=pl.BlockSpec((slab, ct), lambda c, dev: (dev, c)),
        out_shape=jax.ShapeDtypeStruct((N_DEV * slab, d), dwt.dtype),
        compiler_params=_cparams(("parallel", "arbitrary")),
    )(dwo, dwt).reshape(N_DEV, slab, d)


def _peer(pos, k):
    x, y, c = pos
    return (1 - x if k & 4 else x, 1 - y if k & 2 else y, 1 - c if k & 1 else c)


HBM_SPEC = pl.BlockSpec(memory_space=pltpu.HBM)
SEM_SPEC = pl.BlockSpec(memory_space=pltpu.SEMAPHORE)
GATHER_PEERS = (1, 4, 2, 6)
ALL_PEERS = (1, 2, 3, 4, 5, 6, 7)


def _hbm(a):
    return pltpu.with_memory_space_constraint(a, pltpu.HBM)


def _split_copies(src_ref, land_ref, send_sems, recv_sems, ks, per_peer, landed):
    me = _position()
    out = []
    for i, k in enumerate(ks):
        peer = _peer(me, k)
        src = src_ref.at[_slot(peer)] if per_peer else src_ref
        dst = land_ref.at[_slot(peer) if landed else _slot(me)]
        out.append(pltpu.make_async_remote_copy(
            src_ref=src, dst_ref=dst, send_sem=send_sems.at[i], recv_sem=recv_sems.at[i],
            device_id=peer, device_id_type=MESH))
    return out


def _exchange_start(src, after, ks, per_peer, name):
    slab = src.shape[1:] if per_peer else src.shape
    land_shape = (N_DEV,) + tuple(slab)
    n = len(ks)

    def body(src_ref, land_ref, after_ref, send_sems, recv_sems, src_thru, land_thru, token):
        for cp in _split_copies(src_ref, land_ref, send_sems, recv_sems, ks, per_peer, landed=False):
            cp.start()
        token[...] = jnp.zeros_like(token)

    return pl.pallas_call(
        body, name=name,
        out_shape=(pltpu.SemaphoreType.DMA((n,)), pltpu.SemaphoreType.DMA((n,)),
                   pltpu.HBM(src.shape, src.dtype), pltpu.HBM(land_shape, src.dtype),
                   jax.ShapeDtypeStruct((8, LANE), F32)),
        in_specs=(HBM_SPEC, HBM_SPEC, ANY),
        out_specs=(SEM_SPEC, SEM_SPEC, HBM_SPEC, HBM_SPEC, pl.BlockSpec(memory_space=pltpu.VMEM)),
        input_output_aliases={0: 2, 1: 3},
        compiler_params=pltpu.CompilerParams(has_side_effects=pltpu.SideEffectType.DATAFLOW_SIDE_EFFECTING),
    )(_hbm(src), _hbm(lax.empty(land_shape, src.dtype)), after)


def _exchange_wait(started, after, ks, per_peer, name):
    send_sems, recv_sems, src_thru, land_thru = started

    def body(src_ref, land_ref, send_sems, recv_sems, after_ref, src_dead, land_out):
        for cp in _split_copies(src_ref, land_ref, send_sems, recv_sems, ks, per_peer, landed=True):
            cp.wait_send()
            cp.wait_recv()

    return pl.pallas_call(
        body, name=name,
        out_shape=(pltpu.HBM(src_thru.shape, src_thru.dtype), pltpu.HBM(land_thru.shape, land_thru.dtype)),
        in_specs=(HBM_SPEC, HBM_SPEC, SEM_SPEC, SEM_SPEC, ANY), out_specs=(HBM_SPEC, HBM_SPEC),
        input_output_aliases={0: 0, 1: 1},
        compiler_params=pltpu.CompilerParams(has_side_effects=pltpu.SideEffectType.DATAFLOW_SIDE_EFFECTING),
    )(src_thru, land_thru, send_sems, recv_sems, after)


def _relay_copies(land_ref, send_sems, recv_sems, landed):
    me = _position()
    sibling = _peer(me, 1)
    out = []
    for i, k in enumerate(GATHER_PEERS[1:]):
        blk = land_ref.at[_slot(_peer(sibling if landed else me, k))]
        out.append(pltpu.make_async_remote_copy(
            src_ref=blk, dst_ref=blk, send_sem=send_sems.at[i], recv_sem=recv_sems.at[i],
            device_id=sibling, device_id_type=MESH))
    return out


def _relay_start(land, name):
    n = len(GATHER_PEERS) - 1

    def body(land_ref, send_sems, recv_sems, land_thru, token):
        for cp in _relay_copies(land_ref, send_sems, recv_sems, landed=False):
            cp.start()
        token[...] = jnp.zeros_like(token)

    return pl.pallas_call(
        body, name=name,
        out_shape=(pltpu.SemaphoreType.DMA((n,)), pltpu.SemaphoreType.DMA((n,)),
                   pltpu.HBM(land.shape, land.dtype), jax.ShapeDtypeStruct((8, LANE), F32)),
        in_specs=(HBM_SPEC,),
        out_specs=(SEM_SPEC, SEM_SPEC, HBM_SPEC, pl.BlockSpec(memory_space=pltpu.VMEM)),
        input_output_aliases={0: 2},
        compiler_params=pltpu.CompilerParams(has_side_effects=pltpu.SideEffectType.DATAFLOW_SIDE_EFFECTING),
    )(_hbm(land))


def _relay_wait(started, after, name):
    send_sems, recv_sems, land_thru = started

    def body(land_ref, send_sems, recv_sems, after_ref, land_out):
        for cp in _relay_copies(land_ref, send_sems, recv_sems, landed=True):
            cp.wait_send()
            cp.wait_recv()

    return pl.pallas_call(
        body, name=name,
        out_shape=pltpu.HBM(land_thru.shape, land_thru.dtype),
        in_specs=(HBM_SPEC, SEM_SPEC, SEM_SPEC, ANY), out_specs=HBM_SPEC,
        input_output_aliases={0: 0},
        compiler_params=pltpu.CompilerParams(has_side_effects=pltpu.SideEffectType.DATAFLOW_SIDE_EFFECTING),
    )(land_thru, send_sems, recv_sems, after)


def _exchange(arrs, name):
    n = len(arrs)

    def body(*refs):
        ins, outs = refs[:n], refs[n:2 * n]
        send_sems, recv_sems, local_sems = refs[2 * n:]
        me = _position()

        def copy(a, k):
            peer = _peer(me, k)
            return pltpu.make_async_remote_copy(
                src_ref=ins[a].at[_slot(peer)], dst_ref=outs[a].at[_slot(me)],
                send_sem=send_sems.at[a * 7 + k - 1], recv_sem=recv_sems.at[a * 7 + k - 1],
                device_id=peer, device_id_type=MESH)

        def landed(a, k):
            peer = _peer(me, k)
            return pltpu.make_async_remote_copy(
                src_ref=ins[a].at[_slot(peer)], dst_ref=outs[a].at[_slot(peer)],
                send_sem=send_sems.at[a * 7 + k - 1], recv_sem=recv_sems.at[a * 7 + k - 1],
                device_id=peer, device_id_type=MESH)

        mine = [pltpu.make_async_copy(ins[a].at[_slot(me)], outs[a].at[_slot(me)], local_sems.at[a])
                for a in range(n)]
        for cp in mine:
            cp.start()
        sent = [copy(a, k) for k in range(1, N_DEV) for a in range(n)]
        for cp in sent:
            cp.start()
        for k in range(1, N_DEV):
            for a in range(n):
                landed(a, k).wait_recv()
        for cp in sent:
            cp.wait_send()
        for cp in mine:
            cp.wait()

    return pl.pallas_call(
        body, name=name,
        in_specs=[ANY] * n, out_specs=[ANY] * n,
        out_shape=[jax.ShapeDtypeStruct(a.shape, a.dtype) for a in arrs],
        scratch_shapes=[pltpu.SemaphoreType.DMA((7 * n,)), pltpu.SemaphoreType.DMA((7 * n,)),
                        pltpu.SemaphoreType.DMA((n,))],
    )(*arrs)


def _sum_slots(parts):
    def body(p_ref, o_ref):
        acc = p_ref[0]
        for dev in range(1, N_DEV):
            acc = acc + p_ref[dev]
        o_ref[...] = acc

    return pl.pallas_call(body, name="sum_slots",
                          out_shape=jax.ShapeDtypeStruct(parts.shape[1:], F32))(parts)


def _pack(arrs):
    flat = jnp.concatenate([a.reshape(-1) for a in arrs])
    pad = (-flat.shape[0]) % (8 * LANE)
    return jnp.pad(flat, (0, pad)).reshape(-1, LANE)


def _unpack(packed, shapes):
    flat = packed.reshape(-1)
    out, at = [], 0
    for shp in shapes:
        size = 1
        for dim in shp:
            size *= dim
        out.append(flat[at:at + size].reshape(shp))
        at += size
    return out


def _layer_fwd(x, wt, wo, g_pre, g_post, wa_pad, b_alpha, g_gla, g_att, rb_pad, midway=None):
    h = _rms_fwd(x, g_pre)
    z = _matmul(h, wt, "nt", F32, *TILES["in_proj"], "in_proj", n_outer=True)
    y_gla, o_gla, states = _gla_fwd(z, wa_pad, b_alpha, g_gla)
    if midway is not None:
        g_att = g_att + midway(y_gla)[:1, :1]
    y_att, o_att = _att_fwd(z, rb_pad, g_att)
    ycat = jnp.concatenate([y_gla, y_att], axis=1)
    y = _matmul(ycat, wo, "nn", F32, *TILES["out_proj"], "out_proj", n_outer=True)
    out = _post_fwd(x, y, g_post)
    return out, (x, h, z, o_gla, states, o_att, ycat, y)


def _layer_bwd(dout, saved, wt, wo, g_pre, g_post, wa_pad, b_alpha, g_gla, g_att, rb_pad, on_dw):
    x, h, z, o_gla, states, o_att, ycat, y = saved
    dy, dg_post = _post_bwd(dout, y, g_post)
    dycat = _matmul(dy, wo, "nt", F32, *TILES["out_proj_dx"], "out_proj_dx", n_outer=True)
    dwo = _matmul(ycat, dy, "tn", BF16, *TILES["out_proj_dw"], "out_proj_dw")
    dq, dk, dv, dgg, dga, dwa, db, dg_gla = _gla_bwd(dycat, o_gla, z, wa_pad, b_alpha, g_gla, states)
    daq, dak, dav, dag, drb, dg_att = _att_bwd(dycat, o_att, z, rb_pad, g_att)
    dz = jnp.concatenate([dq, dk, dv, dgg, daq, dak.astype(BF16), dav.astype(BF16), dag, dga], axis=1)
    dwt = _matmul(dz, h, "tn", BF16, *TILES["in_proj_dw"], "in_proj_dw")
    token = on_dw(dwo, dwt)
    dh = _matmul(dz, wt, "nn", F32, *TILES["in_proj_dx"], "in_proj_dx", n_outer=True, after=token)
    dx, dg_pre = _pre_bwd(dh, x, g_pre, dout)
    small = (dg_pre[0], dg_post[0], dwa[:GLA_RANK], db[0], dg_gla[0], dg_att[0], drb[:, 0, :N_REL])
    return dx, small


def kernel(x, w_in, w_out, g_pre, g_post, w_alpha, b_alpha, g_gla, g_att, rel_bias, loss_target, m_w_in, m_w_out, m_g_pre, m_g_post, m_w_alpha, m_b_alpha, m_g_gla, m_g_att, m_rel_bias, v_w_in, v_w_out, v_g_pre, v_g_post, v_w_alpha, v_b_alpha, v_g_gla, v_g_att, v_rel_bias):
    nl, d, cols = w_in.shape
    rows = w_out.shape[1]
    s = x.shape[1]
    x0 = x.reshape(s, d)
    tgt = loss_target.reshape(s, d)

    cols_first = lambda a: jnp.transpose(a, (2, 0, 1))
    w_c = cols_first(w_in)
    slab = _slab_rows(rows, cols)
    is_out = lax.broadcasted_iota(jnp.int32, (slab, d), 0) < rows

    def shard(l, zero=0.0):
        top = jnp.pad((w_out[l] + zero).astype(BF16), ((0, slab - rows), (0, 0)))
        rest = jnp.pad(w_c[:, l].astype(BF16), ((rows, slab - rows - cols), (0, 0)))
        return jnp.where(is_out, top, rest)

    first_fetch = _exchange_start(shard(0), x, GATHER_PEERS, False, "gather_start_0")
    began = first_fetch[4][0, 0]
    shards = [None] + [shard(l, began) for l in range(1, nl)]
    alpha = _pack([w_alpha]) + began
    wa_g = _exchange([jnp.broadcast_to(alpha[None], (N_DEV,) + alpha.shape)], "gather_alpha")[0]
    wa_cols = w_alpha.shape[2]
    wa_full = wa_g.reshape(N_DEV, -1)[:, :nl * GLA_RANK * wa_cols].reshape(N_DEV, nl, GLA_RANK, wa_cols)
    wa_full = jnp.transpose(wa_full, (1, 2, 0, 3)).reshape(nl, GLA_RANK, GLA_KW)
    wa_pad = jnp.pad(wa_full, ((0, 0), (0, LANE - GLA_RANK), (0, 0)))
    rb_pad = jnp.pad(rel_bias, ((0, 0), (0, 0), (0, 3 * LANE - N_REL)))

    def layer_args(l, follows_pre=None, follows_post=None):
        gp = g_pre[l:l + 1] if follows_pre is None else g_pre[l:l + 1] + follows_pre[:1, :1]
        gq = g_post[l:l + 1] if follows_post is None else g_post[l:l + 1] + follows_post[:1, :1]
        return (wts[l], wos[l], gp, gq, wa_pad[l], b_alpha[l:l + 1], g_gla[l:l + 1], g_att[l:l + 1], rb_pad[l])

    my = _slot(_position())

    def fetch(l, after):
        return _exchange_start(shards[l], after, GATHER_PEERS, False, f"gather_start_{l}")

    def relay(l, first_hop, after):
        own[l], land = _exchange_wait(first_hop[:4], after, GATHER_PEERS, False, f"gather_wait_{l}")
        return _relay_start(land, f"relay_start_{l}")

    def midway(l, y):
        flight["relay"] = relay(l + 1, flight["fetch"], y)
        if l + 2 >= nl:
            return flight["relay"][3]
        flight["fetch"] = fetch(l + 2, flight["relay"][2])
        return flight["fetch"][4]

    act, saved, wts, wos, flight, own = x0, [], [], [], {}, [None] * nl
    prepared = (wa_pad[0, :1, :1] + sum(sh[:1, :1].astype(F32) for sh in shards[1:]))
    flight["relay"] = relay(0, first_fetch, prepared)
    if nl > 1:
        flight["fetch"] = fetch(1, flight["relay"][2])
    for l in range(nl):
        land = _relay_wait(flight["relay"][:3], act, f"relay_wait_{l}")
        land = lax.dynamic_update_slice_in_dim(land, own[l][None], my, 0)
        wos.append(land[:, :rows].reshape(N_DEV * rows, d))
        wts.append(_aligned_weight(land, rows, cols))
        act, sv = _layer_fwd(act, *layer_args(l),
                             midway=functools.partial(midway, l) if l + 1 < nl else None)
        saved.append(sv)
    dout, sq = _loss_head(act, tgt)
    loss = lax.psum(sq[0, 0] * (0.5 / d), ("x", "y", "c"))

    smalls, pending = [None] * nl, [None] * nl

    def send_partials(l, dwo, dwt):
        pending[l] = _exchange_start(_partial_slabs(dwo, dwt, rows, cols), dwt, ALL_PEERS, True,
                                     f"scatter_start_{l}")
        return pending[l][4]

    for l in reversed(range(nl)):
        dout, smalls[l] = _layer_bwd(dout, saved[l], *layer_args(l), on_dw=functools.partial(send_partials, l))
    grad_x = dout.reshape(x.shape)

    names = 7
    small_stacked = [jnp.stack([smalls[l][i] for l in range(nl)]) for i in range(names)]
    shapes = [a.shape for a in small_stacked]
    packed = _pack(small_stacked)
    gathered = _exchange([jnp.broadcast_to(packed[None], (N_DEV,) + packed.shape)], "gather_small_grads")[0]
    g_pre_g, g_post_g, wa_g_full, b_g, gla_g, att_g, rb_g = _unpack(_sum_slots(gathered), shapes)
    wa_g_mine = lax.dynamic_slice_in_dim(wa_g_full, my * wa_cols, wa_cols, axis=2)
    grads = [g_pre_g, g_post_g, wa_g_mine, b_g, gla_g, att_g, rb_g]
    ws = [g_pre, g_post, w_alpha, b_alpha, g_gla, g_att, rel_bias]
    ms = [m_g_pre, m_g_post, m_w_alpha, m_b_alpha, m_g_gla, m_g_att, m_rel_bias]
    vs = [v_g_pre, v_g_post, v_w_alpha, v_b_alpha, v_g_gla, v_g_att, v_rel_bias]
    shapes2 = [a.shape for a in ws]
    d_p, m2_p, v2_p = _adam_small(_pack(ws), _pack(grads), _pack(ms), _pack(vs))
    d_s, m2_s, v2_s = _unpack(d_p, shapes2), _unpack(m2_p, shapes2), _unpack(v2_p, shapes2)

    parts = []
    for l in range(nl):
        partial, land = _exchange_wait(pending[l][:4], d_p, ALL_PEERS, True, f"scatter_wait_{l}")
        parts.append(lax.dynamic_update_slice_in_dim(land, lax.dynamic_slice_in_dim(partial, my, 1, 0), my, 0))
    g_w_in, d_w_in, m2_w_in, v2_w_in = [
        jnp.transpose(a, (1, 2, 0))
        for a in _adam_columns(parts, rows // ADAM_COL_ROWS, w_c, cols_first(m_w_in), cols_first(v_w_in))]
    g_w_out, d_w_out, m2_w_out, v2_w_out = _adam_sharded(parts, 0, w_out, m_w_out, v_w_out, "adam_w_out")

    def ordered(big_in, big_out, small):
        return [big_in, big_out] + list(small)

    return (loss, grad_x,
            *ordered(g_w_in, g_w_out, grads),
            *ordered(d_w_in, d_w_out, d_s),
            *ordered(m2_w_in, m2_w_out, m2_s),
            *ordered(v2_w_in, v2_w_out, v2_s))
```

```python
import functools

import jax
import jax.numpy as jnp
from jax import lax
from jax.experimental import pallas as pl
from jax.experimental.pallas import tpu as pltpu

F32 = jnp.float32
BF16 = jnp.bfloat16
MESH = pl.DeviceIdType.MESH
ANY = pl.BlockSpec(memory_space=pl.ANY)

CHUNK = 64
GLA_HEADS = 4
GLA_DK = 128
GLA_DV = 256
GLA_KW = GLA_HEADS * GLA_DK
D_GLA = GLA_HEADS * GLA_DV
GLA_RANK = 16
GLA_TAU = 16.0
ATT_HEADS = 8
ATT_HD = 128
D_ATT = ATT_HEADS * ATT_HD
LEFT_CHUNKS = 8
REL_CLIP = 128
N_REL = 2 * REL_CLIP + 1
EPS = 1e-6
D_IN = 2 * GLA_KW + 2 * D_GLA + GLA_RANK + 4 * D_ATT
GLA_SCALE = GLA_DK ** -0.5
ATT_SCALE = ATT_HD ** -0.5

ADAM_LR = 0.001
ADAM_B1 = 0.9
ADAM_B2 = 0.999
ADAM_EPS = 1e-08
ADAM_WD = 0.01
ADAM_STEP = 10

N_DEV = 8
LANE = 128
GA_ORIG = 2 * GLA_KW + 2 * D_GLA
OFF_AQ = GA_ORIG
OFF_GA = GA_ORIG + 4 * D_ATT
D_ZP = OFF_GA + LANE
QB = 2 * CHUNK
ATT_UNROLL = 8
WIN = (LEFT_CHUNKS + 2) * CHUNK
ET_ROWS = WIN + LEFT_CHUNKS * CHUNK
NEG = -1e30
VMEM_LIMIT = 48 * 1024 * 1024


def _cparams(sem):
    return pltpu.CompilerParams(dimension_semantics=sem, vmem_limit_bytes=VMEM_LIMIT)


def _dot(a, b):
    return jnp.dot(a, b, preferred_element_type=F32)


def _dot_nt(a, b):
    return lax.dot_general(a, b, (((1,), (1,)), ((), ())), preferred_element_type=F32)


def _dot_tn(a, b):
    return lax.dot_general(a, b, (((0,), (0,)), ((), ())), preferred_element_type=F32)


def _dot01(t, x, left=True):
    if not left:
        t, x = x, t
    hi = x.astype(BF16)
    r = x - hi.astype(F32)
    mid = r.astype(BF16)
    lo = (r - mid.astype(F32)).astype(BF16)
    if left:
        return _dot(t, hi) + _dot(t, mid) + _dot(t, lo)
    return _dot(hi, t) + _dot(mid, t) + _dot(lo, t)


def _sigmoid(x):
    return 1.0 / (1.0 + jnp.exp(-x))


def _log_sigmoid(x):
    return jnp.minimum(x, 0.0) - jnp.log(1.0 + jnp.exp(-jnp.abs(x)))


TILES = {
    "in_proj": (512, D_ZP // 3, None),
    "in_proj_dx": (512, 512, None),
    "in_proj_dw": (D_ZP // 3, 512, None),
    "out_proj": (512, 1024, None),
    "out_proj_dx": (512, 1024, None),
    "out_proj_dw": (1024, 1024, None),
}


def _matmul(a, b, mode, out_dtype, tm, tn, tk, name, n_outer=False, after=None):
    if mode == "nn":
        (m, k), n = a.shape, b.shape[1]
    elif mode == "nt":
        (m, k), n = a.shape, b.shape[0]
    else:
        (k, m), n = a.shape, b.shape[1]
    tm, tn, tk = min(tm, m), min(tn, n), k if tk is None else min(tk, k)
    assert m % tm == 0 and n % tn == 0 and k % tk == 0, (name, m, n, k)
    nk = k // tk
    dot = {"nn": _dot, "nt": _dot_nt, "tn": _dot_tn}[mode]

    follows = [] if after is None else [after]

    def body_whole_k(a_ref, b_ref, *rest):
        o_ref = rest[-1]
        o_ref[...] = dot(a_ref[...], b_ref[...]).astype(out_dtype)

    def body(a_ref, b_ref, *rest):
        o_ref, acc_ref = rest[-2:]
        kk = pl.program_id(2)

        @pl.when(kk == 0)
        def _():
            acc_ref[...] = jnp.zeros_like(acc_ref)

        acc_ref[...] += dot(a_ref[...], b_ref[...])

        @pl.when(kk == nk - 1)
        def _():
            o_ref[...] = acc_ref[...].astype(out_dtype)

    def at(index):
        return (lambda j, i, kk: index(i, j, kk)) if n_outer else index

    if mode == "tn":
        a_spec = pl.BlockSpec((tk, tm), at(lambda i, j, kk: (kk, i)))
    else:
        a_spec = pl.BlockSpec((tm, tk), at(lambda i, j, kk: (i, kk)))
    if mode == "nt":
        b_spec = pl.BlockSpec((tn, tk), at(lambda i, j, kk: (j, kk)))
    else:
        b_spec = pl.BlockSpec((tk, tn), at(lambda i, j, kk: (kk, j)))
    return pl.pallas_call(
        body_whole_k if nk == 1 else body, name=name,
        grid=(n // tn, m // tm, nk) if n_outer else (m // tm, n // tn, nk),
        in_specs=[a_spec, b_spec] + [ANY] * len(follows),
        out_specs=pl.BlockSpec((tm, tn), at(lambda i, j, kk: (i, j))),
        out_shape=jax.ShapeDtypeStruct((m, n), out_dtype),
        scratch_shapes=[] if nk == 1 else [pltpu.VMEM((tm, tn), F32)],
        compiler_params=_cparams(("parallel", "parallel", "arbitrary")),
    )(a, b, *follows)


ROWS = 256


def _rms_fwd(x, g):
    s, d = x.shape

    def body(x_ref, g_ref, h_ref):
        xv = x_ref[...]
        r = lax.rsqrt(jnp.mean(xv * xv, axis=-1, keepdims=True) + EPS)
        h_ref[...] = (xv * r * g_ref[...]).astype(BF16)

    return pl.pallas_call(
        body, name="rms_fwd", grid=(s // ROWS,),
        in_specs=[pl.BlockSpec((ROWS, d), lambda i: (i, 0)), pl.BlockSpec((1, d), lambda i: (0, 0))],
        out_specs=pl.BlockSpec((ROWS, d), lambda i: (i, 0)),
        out_shape=jax.ShapeDtypeStruct((s, d), BF16),
        compiler_params=_cparams(("parallel",)),
    )(x, g)


def _post_fwd(x, y, g):
    s, d = x.shape

    def body(x_ref, y_ref, g_ref, o_ref):
        yv = y_ref[...]
        r = lax.rsqrt(jnp.mean(yv * yv, axis=-1, keepdims=True) + EPS)
        o_ref[...] = x_ref[...] + yv * r * g_ref[...]

    row = pl.BlockSpec((ROWS, d), lambda i: (i, 0))
    return pl.pallas_call(
        body, name="post_fwd", grid=(s // ROWS,),
        in_specs=[row, row, pl.BlockSpec((1, d), lambda i: (0, 0))],
        out_specs=row,
        out_shape=jax.ShapeDtypeStruct((s, d), F32),
        compiler_params=_cparams(("parallel",)),
    )(x, y, g)


def _loss_head(out, tgt):
    s, d = out.shape

    def body(o_ref, t_ref, dout_ref, sum_ref):
        @pl.when(pl.program_id(0) == 0)
        def _():
            sum_ref[...] = jnp.zeros_like(sum_ref)

        e = o_ref[...] - t_ref[...]
        dout_ref[...] = e * (1.0 / d)
        sum_ref[...] += jnp.sum(jnp.sum(e * e, axis=1, keepdims=True), axis=0, keepdims=True)

    row = pl.BlockSpec((ROWS, d), lambda i: (i, 0))
    return pl.pallas_call(
        body, name="loss_head", grid=(s // ROWS,),
        in_specs=[row, row],
        out_specs=[row, pl.BlockSpec((1, 1), lambda i: (0, 0))],
        out_shape=[jax.ShapeDtypeStruct((s, d), F32), jax.ShapeDtypeStruct((1, 1), F32)],
        compiler_params=_cparams(("arbitrary",)),
    )(out, tgt)


def _post_bwd(dout, y, g):
    s, d = y.shape

    def body(do_ref, y_ref, g_ref, dy_ref, dg_ref):
        @pl.when(pl.program_id(0) == 0)
        def _():
            dg_ref[...] = jnp.zeros_like(dg_ref)

        yv = y_ref[...]
        dv = do_ref[...]
        r = lax.rsqrt(jnp.mean(yv * yv, axis=-1, keepdims=True) + EPS)
        dg_ref[...] += jnp.sum(dv * yv * r, axis=0, keepdims=True)
        w = dv * g_ref[...]
        dy = r * (w - yv * (r * r) * jnp.mean(w * yv, axis=-1, keepdims=True))
        dy_ref[...] = dy.astype(BF16)

    row = pl.BlockSpec((ROWS, d), lambda i: (i, 0))
    vec = pl.BlockSpec((1, d), lambda i: (0, 0))
    return pl.pallas_call(
        body, name="post_bwd", grid=(s // ROWS,),
        in_specs=[row, row, vec],
        out_specs=[row, vec],
        out_shape=[jax.ShapeDtypeStruct((s, d), BF16), jax.ShapeDtypeStruct((1, d), F32)],
        compiler_params=_cparams(("arbitrary",)),
    )(dout, y, g)


def _pre_bwd(dh, x, g, dout):
    s, d = x.shape

    def body(dh_ref, x_ref, g_ref, do_ref, dx_ref, dg_ref):
        @pl.when(pl.program_id(0) == 0)
        def _():
            dg_ref[...] = jnp.zeros_like(dg_ref)

        xv = x_ref[...]
        dv = dh_ref[...]
        r = lax.rsqrt(jnp.mean(xv * xv, axis=-1, keepdims=True) + EPS)
        dg_ref[...] += jnp.sum(dv * xv * r, axis=0, keepdims=True)
        w = dv * g_ref[...]
        dx_ref[...] = do_ref[...] + r * (w - xv * (r * r) * jnp.mean(w * xv, axis=-1, keepdims=True))

    row = pl.BlockSpec((ROWS, d), lambda i: (i, 0))
    vec = pl.BlockSpec((1, d), lambda i: (0, 0))
    return pl.pallas_call(
        body, name="pre_bwd", grid=(s // ROWS,),
        in_specs=[row, row, vec, row],
        out_specs=[row, vec],
        out_shape=[jax.ShapeDtypeStruct((s, d), F32), jax.ShapeDtypeStruct((1, d), F32)],
        compiler_params=_cparams(("arbitrary",)),
    )(dh, x, g, dout)


GLA_STEP = 4
GLA_ROWS = GLA_STEP * CHUNK
GLA_CHUNKS = [slice(c * CHUNK, (c + 1) * CHUNK) for c in range(GLA_STEP)]


def _chunk_triangles():
    ri = lax.broadcasted_iota(jnp.int32, (GLA_ROWS, GLA_ROWS), 0)
    ci = lax.broadcasted_iota(jnp.int32, (GLA_ROWS, GLA_ROWS), 1)
    same = (ri // CHUNK) == (ci // CHUNK)
    return (jnp.where(same & (ri >= ci), 1.0, 0.0).astype(BF16), jnp.where(same & (ci >= ri), 1.0, 0.0).astype(BF16))


def _per_chunk(fn, like):
    row = lax.broadcasted_iota(jnp.int32, like.shape, 0)
    return [fn((row >= c * CHUNK) & (row < (c + 1) * CHUNK)) for c in range(GLA_STEP)]


def _spread(per_chunk, like):
    row = lax.broadcasted_iota(jnp.int32, like.shape, 0)
    out = per_chunk[-1]
    for c in reversed(range(GLA_STEP - 1)):
        out = jnp.where(row < (c + 1) * CHUNK, per_chunk[c], out)
    return out


def _gla_gate(ga_b, wa_b, b_ref, tri):
    pre = _dot(ga_b, wa_b) + b_ref[...]
    la = _log_sigmoid(pre) * (1.0 / GLA_TAU)
    cum = _dot01(tri, la)
    row = lax.broadcasted_iota(jnp.int32, cum.shape, 0)
    cends = [jnp.sum(jnp.where(row == (c + 1) * CHUNK - 1, cum, 0.0), axis=0, keepdims=True)
             for c in range(GLA_STEP)]
    return pre, cum, cends


def _heads(width):
    return [slice(h * width, (h + 1) * width) for h in range(GLA_HEADS)]


def _z_specs_gla(rev=None):
    idx = (lambda n: n) if rev is None else rev
    return [
        pl.BlockSpec((GLA_ROWS, GLA_KW), lambda n: (idx(n), 0)),
        pl.BlockSpec((GLA_ROWS, GLA_KW), lambda n: (idx(n), 1)),
        pl.BlockSpec((GLA_ROWS, D_GLA), lambda n: (idx(n), 1)),
        pl.BlockSpec((GLA_ROWS, D_GLA), lambda n: (idx(n), 2)),
        pl.BlockSpec((GLA_ROWS, LANE), lambda n: (idx(n), OFF_GA // LANE)),
    ]


def _gla_fwd(z, wa_pad, b_alpha, g_gla):
    s = z.shape[0]
    nchunk = s // CHUNK

    def body(q_ref, k_ref, v_ref, gg_ref, ga_ref, wa_ref, b_ref, g_ref, y_ref, o_ref, st_ref, state):
        @pl.when(pl.program_id(0) == 0)
        def _():
            state[...] = jnp.zeros_like(state)

        ga_b = ga_ref[...].astype(BF16)
        tri, _ = _chunk_triangles()
        nh = range(GLA_HEADS)
        keys, vals = _heads(GLA_DK), _heads(GLA_DV)
        _, cum, cends = _gla_gate(ga_b, wa_ref[...].astype(BF16), b_ref, tri)
        kd_b = (k_ref[...] * jnp.exp(_spread(cends, cum) - cum)).astype(BF16)
        qs = (q_ref[...] * GLA_SCALE).astype(BF16)
        v_b = v_ref[...].astype(BF16)
        uts = [[_dot_tn(v_b[rs, vals[h]], kd_b[rs, keys[h]]) for h in nh] for rs in GLA_CHUNKS]
        sts, prev = [], [state[h] for h in nh]
        for c in range(GLA_STEP):
            a = jnp.exp(cends[c])
            prev = [prev[h] * a[:, keys[h]] + uts[c][h] for h in nh]
            sts.append(prev)
        for h in nh:
            state[h] = prev[h]
            for c in range(GLA_STEP):
                st_ref[c, h] = sts[c][h]
        outs = [[_dot_nt(qs[rs, keys[h]], sts[c][h].astype(BF16)) for h in nh] for c, rs in enumerate(GLA_CHUNKS)]
        for h in nh:
            o, vs = jnp.concatenate([outs[c][h] for c in range(GLA_STEP)], axis=0), vals[h]
            o_ref[:, vs] = o
            r = lax.rsqrt(jnp.mean(o * o, axis=-1, keepdims=True) + EPS)
            gg = gg_ref[:, vs]
            y_ref[:, vs] = (o * r * g_ref[:, vs] * (gg * _sigmoid(gg))).astype(BF16)

    full = lambda shape: pl.BlockSpec(shape, lambda n: tuple(0 for _ in shape))
    wide = pl.BlockSpec((GLA_ROWS, D_GLA), lambda n: (n, 0))
    return pl.pallas_call(
        body, name="gla_fwd", grid=(nchunk // GLA_STEP,),
        in_specs=_z_specs_gla() + [full((LANE, GLA_KW)), full((1, GLA_KW)), full((1, D_GLA))],
        out_specs=[wide, wide, pl.BlockSpec((GLA_STEP, GLA_HEADS, GLA_DV, GLA_DK), lambda n: (n, 0, 0, 0))],
        out_shape=[jax.ShapeDtypeStruct((s, D_GLA), BF16), jax.ShapeDtypeStruct((s, D_GLA), F32),
                   jax.ShapeDtypeStruct((nchunk, GLA_HEADS, GLA_DV, GLA_DK), F32)],
        scratch_shapes=[pltpu.VMEM((GLA_HEADS, GLA_DV, GLA_DK), F32)],
        compiler_params=_cparams(("arbitrary",)),
    )(z, z, z, z, z, wa_pad, b_alpha, g_gla)


def _gla_bwd(dyc, o_gla, z, wa_pad, b_alpha, g_gla, states):
    s = z.shape[0]
    nsteps = s // GLA_ROWS
    rev = lambda n: nsteps - 1 - n

    def body(dy_ref, o_ref, q_ref, k_ref, v_ref, gg_ref, ga_ref, wa_ref, b_ref, g_ref, st_ref, stp_ref,
             dq_ref, dk_ref, dv_ref, dgg_ref, dga_ref, dwa_ref, db_ref, dg_ref, carry):
        step = pl.program_id(0)

        @pl.when(step == 0)
        def _():
            carry[...] = jnp.zeros_like(carry)
            dwa_ref[...] = jnp.zeros_like(dwa_ref)
            db_ref[...] = jnp.zeros_like(db_ref)
            dg_ref[...] = jnp.zeros_like(dg_ref)

        has_prev = (step < nsteps - 1).astype(F32)
        ga_b = ga_ref[...].astype(BF16)
        tri, tri_up = _chunk_triangles()
        nh, nc = range(GLA_HEADS), range(GLA_STEP)
        keys, vals = _heads(GLA_DK), _heads(GLA_DV)
        wa_b = wa_ref[...].astype(BF16)
        pre, cum, cends = _gla_gate(ga_b, wa_b, b_ref, tri)
        e = jnp.exp(_spread(cends, cum) - cum)
        a = [jnp.exp(cends[c]) for c in nc]
        kf = k_ref[...]
        kd_b = (kf * e).astype(BF16)
        v_b = v_ref[...].astype(BF16)
        qs = (q_ref[...] * GLA_SCALE).astype(BF16)
        do_b = []
        for h in nh:
            vs = vals[h]
            o = o_ref[:, vs]
            gg = gg_ref[:, vs]
            g = g_ref[:, vs]
            dy = dy_ref[:, vs]
            r = lax.rsqrt(jnp.mean(o * o, axis=-1, keepdims=True) + EPS)
            sg = _sigmoid(gg)
            dogn = dy * (gg * sg)
            dgg_ref[:, vs] = (dy * (o * r * g) * (sg * (1.0 + gg * (1.0 - sg)))).astype(BF16)
            dg_ref[:, vs] += jnp.sum(dogn * o * r, axis=0, keepdims=True)
            w = dogn * g
            do_b.append((r * (w - o * (r * r) * jnp.mean(w * o, axis=-1, keepdims=True))).astype(BF16))
        dqs = [jnp.concatenate([_dot(do_b[h][rs], st_ref[c, h].astype(BF16)) for c, rs in enumerate(GLA_CHUNKS)],
                               axis=0) for h in nh]
        dq_ref[...] = (jnp.concatenate(dqs, axis=1) * GLA_SCALE).astype(BF16)
        own = [[_dot_tn(do_b[h][rs], qs[rs, keys[h]]) for h in nh] for rs in GLA_CHUNKS]
        gts, later = [None] * GLA_STEP, [carry[h] for h in nh]
        for c in reversed(nc):
            gts[c] = [own[c][h] + later[h] for h in nh]
            later = [gts[c][h] * a[c][:, keys[h]] for h in nh]
        for h in nh:
            carry[h] = later[h]
        gt_b = [[gts[c][h].astype(BF16) for h in nh] for c in nc]
        dkd = jnp.concatenate([jnp.concatenate([_dot(v_b[rs, vals[h]], gt_b[c][h]) for h in nh], axis=1)
                               for c, rs in enumerate(GLA_CHUNKS)], axis=0)
        dvs = [[_dot_nt(kd_b[rs, keys[h]], gt_b[c][h]) for h in nh] for c, rs in enumerate(GLA_CHUNKS)]
        before = lambda c, h: st_ref[c - 1, h] if c > 0 else stp_ref[0, h] * has_prev
        da = [jnp.concatenate([jnp.sum(gts[c][h] * before(c, h), axis=0, keepdims=True) for h in nh], axis=1)
              for c in nc]
        for h in nh:
            dv_ref[:, vals[h]] = jnp.concatenate([dvs[c][h] for c in nc], axis=0).astype(BF16)
        dk_ref[...] = (dkd * e).astype(BF16)
        dd = dkd * kf * e
        dsum = _per_chunk(lambda mine: jnp.sum(jnp.where(mine, dd, 0.0), axis=0, keepdims=True), dd)
        dcend = _spread([dsum[c] + da[c] * a[c] for c in nc], dd)
        dla = dcend - _dot01(tri_up, dd)
        dpre = dla * (1.0 / GLA_TAU) * (1.0 - _sigmoid(pre))
        dpre_b = dpre.astype(BF16)
        dga_ref[...] = _dot_nt(dpre_b, wa_b).astype(BF16)
        dwa_ref[...] += _dot_tn(ga_b, dpre_b)
        db_ref[...] += jnp.sum(dpre, axis=0, keepdims=True)

    full = lambda shape: pl.BlockSpec(shape, lambda n: tuple(0 for _ in shape))
    wide = pl.BlockSpec((GLA_ROWS, D_GLA), lambda n: (rev(n), 0))
    keyw = pl.BlockSpec((GLA_ROWS, GLA_KW), lambda n: (rev(n), 0))
    st_spec = pl.BlockSpec((GLA_STEP, GLA_HEADS, GLA_DV, GLA_DK), lambda n: (rev(n), 0, 0, 0))
    stp_spec = pl.BlockSpec((1, GLA_HEADS, GLA_DV, GLA_DK),
                            lambda n: (jnp.maximum(GLA_STEP * rev(n) - 1, 0), 0, 0, 0))
    return pl.pallas_call(
        body, name="gla_bwd", grid=(nsteps,),
        in_specs=[wide, wide] + _z_specs_gla(rev)
        + [full((LANE, GLA_KW)), full((1, GLA_KW)), full((1, D_GLA)), st_spec, stp_spec],
        out_specs=[keyw, keyw, wide, wide, pl.BlockSpec((GLA_ROWS, LANE), lambda n: (rev(n), 0)),
                   full((LANE, GLA_KW)), full((1, GLA_KW)), full((1, D_GLA))],
        out_shape=[jax.ShapeDtypeStruct((s, GLA_KW), BF16), jax.ShapeDtypeStruct((s, GLA_KW), BF16),
                   jax.ShapeDtypeStruct((s, D_GLA), BF16), jax.ShapeDtypeStruct((s, D_GLA), BF16),
                   jax.ShapeDtypeStruct((s, LANE), BF16),
                   jax.ShapeDtypeStruct((LANE, GLA_KW), F32), jax.ShapeDtypeStruct((1, GLA_KW), F32),
                   jax.ShapeDtypeStruct((1, D_GLA), F32)],
        scratch_shapes=[pltpu.VMEM((GLA_HEADS, GLA_DV, GLA_DK), F32)],
        compiler_params=_cparams(("arbitrary",)),
    )(dyc, o_gla, z, z, z, z, z, wa_pad, b_alpha, g_gla, states, states)


def _build_bias_table(rb_row, et_ref):
    far = jnp.broadcast_to(rb_row[:, 2 * REL_CLIP:2 * REL_CLIP + 1], (1, LANE))
    near_hi = rb_row[:, REL_CLIP:2 * REL_CLIP]
    near_lo = rb_row[:, 0:REL_CLIP]
    past = jnp.broadcast_to(rb_row[:, 0:1], (1, LANE))
    seg = [far, far, far, far, near_hi, near_lo] + [past] * (ET_ROWS // LANE - 5)
    ri = lax.broadcasted_iota(jnp.int32, (LANE, LANE), 0)
    ci = lax.broadcasted_iota(jnp.int32, (LANE, LANE), 1)
    for kb in range(ET_ROWS // LANE):
        wmat = jnp.where(ri + ci < LANE, seg[kb], seg[kb + 1])
        blk = pltpu.roll(wmat, 0, 1, stride=1, stride_axis=0)
        lag = LEFT_CHUNKS + ci // CHUNK - (2 * kb + ri // CHUNK)
        et_ref[kb * LANE:(kb + 1) * LANE, :] = jnp.where((lag >= 0) & (lag <= LEFT_CHUNKS), blk, NEG)


def _reduce_bias_table(det_ref):
    lane = lax.broadcasted_iota(jnp.int32, (1, LANE), 1)
    ri = lax.broadcasted_iota(jnp.int32, (LANE, LANE), 0)
    ci = lax.broadcasted_iota(jnp.int32, (LANE, LANE), 1)
    flip = jnp.where(ri + ci == LANE - 1, 1.0, 0.0).astype(BF16)
    segs = jnp.zeros((8, LANE), F32)
    seg_row = lax.broadcasted_iota(jnp.int32, (8, LANE), 0)
    prev_minus = jnp.zeros((1, LANE), F32)
    for kb in range(6):
        rolled = pltpu.roll(_dot01(det_ref[kb * LANE:(kb + 1) * LANE, :], flip, left=False), 0, 1,
                            stride=1, stride_axis=0)
        plus = jnp.sum(jnp.where(ci >= ri, rolled, 0.0), axis=0, keepdims=True)
        minus = jnp.sum(jnp.where(ci < ri, rolled, 0.0), axis=0, keepdims=True)
        segs = segs + jnp.where(seg_row == kb, plus + prev_minus, 0.0)
        prev_minus = minus
    segs = _dot01(segs, flip, left=False)
    pick = lambda kb: jnp.sum(jnp.where(seg_row == kb, segs, 0.0), axis=0, keepdims=True)
    far = jnp.sum(pick(0) + pick(1) + pick(2) + pick(3), axis=1, keepdims=True)
    last = jnp.where(lane == 0, far, 0.0)
    return jnp.concatenate([pick(5), pick(4), last], axis=1)


def _att_window(b):
    c0 = 2 * b
    kstart = pl.multiple_of(jnp.maximum(c0 - LEFT_CHUNKS, 0) * CHUNK, CHUNK)
    eoff = pl.multiple_of(jnp.maximum(LEFT_CHUNKS - c0, 0) * CHUNK, CHUNK)
    return kstart, eoff


def _att_probs(q_b, kw_b, et):
    st = _dot_nt(kw_b, q_b) * ATT_SCALE + et
    m = jnp.max(st, axis=0, keepdims=True)
    ex = jnp.exp(st - m)
    return ex * (1.0 / jnp.sum(ex, axis=0, keepdims=True))


def _att_fwd(z, rb_pad, g_att):
    s = z.shape[0]
    nblk = s // QB
    c_aq, c_ak, c_av, c_ag = [(OFF_AQ + i * D_ATT) // ATT_HD for i in range(4)]

    def body(q_ref, k_ref, v_ref, ag_ref, rb_ref, g_ref, y_ref, o_ref, et_ref):
        h = pl.program_id(0)
        b = pl.program_id(1)

        @pl.when(b == 0)
        def _():
            _build_bias_table(rb_ref[pl.ds(h, 1), :], et_ref)

        for j in range(ATT_UNROLL):
            rs = slice(j * QB, (j + 1) * QB)
            kstart, eoff = _att_window(b * ATT_UNROLL + j)
            q_b = q_ref[rs, :].astype(BF16)
            kw_b = k_ref[pl.ds(kstart, WIN), :].astype(BF16)
            vw_b = v_ref[pl.ds(kstart, WIN), :].astype(BF16)
            pt = _att_probs(q_b, kw_b, et_ref[pl.ds(eoff, WIN), :])
            o = _dot_tn(pt.astype(BF16), vw_b)
            o_ref[rs, :] = o
            r = lax.rsqrt(jnp.mean(o * o, axis=-1, keepdims=True) + EPS)
            ag = ag_ref[rs, :]
            y_ref[rs, :] = (o * r * g_ref[...] * (ag * _sigmoid(ag))).astype(BF16)

    blk = lambda col: pl.BlockSpec((ATT_UNROLL * QB, ATT_HD), lambda h, b: (b, col + h))
    seq = lambda col: pl.BlockSpec((s, ATT_HD), lambda h, b: (0, col + h))
    out_blk = pl.BlockSpec((ATT_UNROLL * QB, ATT_HD), lambda h, b: (b, h))
    return pl.pallas_call(
        body, name="att_fwd", grid=(ATT_HEADS, nblk // ATT_UNROLL),
        in_specs=[blk(c_aq), seq(c_ak), seq(c_av), blk(c_ag),
                  pl.BlockSpec((ATT_HEADS, 3 * LANE), lambda h, b: (0, 0)),
                  pl.BlockSpec((1, ATT_HD), lambda h, b: (0, h))],
        out_specs=[out_blk, out_blk],
        out_shape=[jax.ShapeDtypeStruct((s, D_ATT), BF16), jax.ShapeDtypeStruct((s, D_ATT), F32)],
        scratch_shapes=[pltpu.VMEM((ET_ROWS, LANE), F32)],
        compiler_params=_cparams(("arbitrary", "arbitrary")),
    )(z, z, z, z, rb_pad, g_att)


def _att_bwd(dyc, o_att, z, rb_pad, g_att):
    s = z.shape[0]
    nblk = s // QB
    c_aq, c_ak, c_av, c_ag = [(OFF_AQ + i * D_ATT) // ATT_HD for i in range(4)]
    c_dy = D_GLA // ATT_HD

    def body(dy_ref, o_ref, q_ref, k_ref, v_ref, ag_ref, rb_ref, g_ref,
             dq_ref, dk_ref, dv_ref, dag_ref, drb_ref, dg_ref, et_ref, det_ref):
        h = pl.program_id(0)
        b = pl.program_id(1)

        @pl.when(b == 0)
        def _():
            _build_bias_table(rb_ref[pl.ds(h, 1), :], et_ref)
            det_ref[...] = jnp.zeros_like(det_ref)
            dk_ref[...] = jnp.zeros_like(dk_ref)
            dv_ref[...] = jnp.zeros_like(dv_ref)
            dg_ref[...] = jnp.zeros_like(dg_ref)

        g = g_ref[...]
        dg = jnp.zeros((1, ATT_HD), F32)
        for j in range(ATT_UNROLL):
            rs = slice(j * QB, (j + 1) * QB)
            kstart, eoff = _att_window(b * ATT_UNROLL + j)
            q_b = q_ref[rs, :].astype(BF16)
            kw_b = k_ref[pl.ds(kstart, WIN), :].astype(BF16)
            vw_b = v_ref[pl.ds(kstart, WIN), :].astype(BF16)
            pt = _att_probs(q_b, kw_b, et_ref[pl.ds(eoff, WIN), :])
            o = o_ref[rs, :]
            ag = ag_ref[rs, :]
            dy = dy_ref[rs, :]
            r = lax.rsqrt(jnp.mean(o * o, axis=-1, keepdims=True) + EPS)
            sg = _sigmoid(ag)
            don = dy * (ag * sg)
            dag_ref[rs, :] = (dy * (o * r * g) * (sg * (1.0 + ag * (1.0 - sg)))).astype(BF16)
            dg = dg + jnp.sum(don * o * r, axis=0, keepdims=True)
            w = don * g
            do_b = (r * (w - o * (r * r) * jnp.mean(w * o, axis=-1, keepdims=True))).astype(BF16)
            pt_b = pt.astype(BF16)
            dpt = _dot_nt(vw_b, do_b)
            dst = pt * (dpt - jnp.sum(dpt * pt, axis=0, keepdims=True))
            det_ref[pl.ds(eoff, WIN), :] += dst
            ds_b = (dst * ATT_SCALE).astype(BF16)
            dq_ref[rs, :] = _dot_tn(ds_b, kw_b).astype(BF16)
            dk_ref[pl.ds(kstart, WIN), :] += _dot(ds_b, q_b)
            dv_ref[pl.ds(kstart, WIN), :] += _dot(pt_b, do_b)
        dg_ref[...] += dg

        @pl.when(b == nblk // ATT_UNROLL - 1)
        def _():
            drb_ref[0] = jnp.broadcast_to(_reduce_bias_table(det_ref), (8, 3 * LANE))

    blk = lambda col: pl.BlockSpec((ATT_UNROLL * QB, ATT_HD), lambda h, b: (b, col + h))
    seq = lambda col: pl.BlockSpec((s, ATT_HD), lambda h, b: (0, col + h))
    out_blk = pl.BlockSpec((ATT_UNROLL * QB, ATT_HD), lambda h, b: (b, h))
    out_seq = pl.BlockSpec((s, ATT_HD), lambda h, b: (0, h))
    return pl.pallas_call(
        body, name="att_bwd", grid=(ATT_HEADS, nblk // ATT_UNROLL),
        in_specs=[blk(c_dy), blk(0), blk(c_aq), seq(c_ak), seq(c_av), blk(c_ag),
                  pl.BlockSpec((ATT_HEADS, 3 * LANE), lambda h, b: (0, 0)),
                  pl.BlockSpec((1, ATT_HD), lambda h, b: (0, h))],
        out_specs=[out_blk, out_seq, out_seq, out_blk,
                   pl.BlockSpec((1, 8, 3 * LANE), lambda h, b: (h, 0, 0)),
                   pl.BlockSpec((1, ATT_HD), lambda h, b: (0, h))],
        out_shape=[jax.ShapeDtypeStruct((s, D_ATT), BF16), jax.ShapeDtypeStruct((s, D_ATT), F32),
                   jax.ShapeDtypeStruct((s, D_ATT), F32), jax.ShapeDtypeStruct((s, D_ATT), BF16),
                   jax.ShapeDtypeStruct((ATT_HEADS, 8, 3 * LANE), F32),
                   jax.ShapeDtypeStruct((1, D_ATT), F32)],
        scratch_shapes=[pltpu.VMEM((ET_ROWS, LANE), F32), pltpu.VMEM((ET_ROWS, LANE), F32)],
        compiler_params=_cparams(("arbitrary", "arbitrary")),
    )(dyc, o_att, z, z, z, z, rb_pad, g_att)


ADAM_ROWS = 64
ADAM_COL_ROWS = 32


def _adam_math(w, g, m, v):
    m2 = ADAM_B1 * m + (1.0 - ADAM_B1) * g
    v2 = ADAM_B2 * v + (1.0 - ADAM_B2) * (g * g)
    m_hat = m2 / (1.0 - ADAM_B1 ** ADAM_STEP)
    v_hat = v2 / (1.0 - ADAM_B2 ** ADAM_STEP)
    delta = -ADAM_LR * (m_hat / (jnp.sqrt(v_hat) + ADAM_EPS) + ADAM_WD * w)
    return delta, m2, v2


def _adam_sharded(parts, first, w, m, v, name):
    nl, nr, nc = w.shape

    def body(*refs):
        p_refs = refs[:nl]
        w_ref, m_ref, v_ref, g_ref, d_ref, m2_ref, v2_ref = refs[nl:]
        for k in range(nl):
            @pl.when(pl.program_id(0) == k)
            def _(p_ref=p_refs[k]):
                g = p_ref[0].astype(F32)
                for dev in range(1, N_DEV):
                    g = g + p_ref[dev].astype(F32)
                delta, m2, v2 = _adam_math(w_ref[0], g, m_ref[0], v_ref[0])
                g_ref[0] = g
                d_ref[0] = delta
                m2_ref[0] = m2
                v2_ref[0] = v2

    def part_spec(k):
        return pl.BlockSpec((N_DEV, ADAM_ROWS, nc), lambda l, i: (0, first + jnp.where(l == k, i, 0), 0))

    blk = pl.BlockSpec((1, ADAM_ROWS, nc), lambda l, i: (l, i, 0))
    shp = jax.ShapeDtypeStruct(w.shape, F32)
    return pl.pallas_call(
        body, name=name, grid=(nl, pl.cdiv(nr, ADAM_ROWS)),
        in_specs=[part_spec(k) for k in range(nl)] + [blk, blk, blk],
        out_specs=[blk, blk, blk, blk],
        out_shape=[shp, shp, shp, shp],
        compiler_params=_cparams(("arbitrary", "arbitrary")),
    )(*parts, w, m, v)


def _adam_columns(parts, first, w, m, v):
    nc, nl, d = w.shape

    def body(*refs):
        p_refs = refs[:nl]
        w_ref, m_ref, v_ref, g_ref, d_ref, m2_ref, v2_ref = refs[nl:]
        for l in range(nl):
            g = p_refs[l][0].astype(F32)
            for dev in range(1, N_DEV):
                g = g + p_refs[l][dev].astype(F32)
            delta, m2, v2 = _adam_math(w_ref[:, l, :], g, m_ref[:, l, :], v_ref[:, l, :])
            g_ref[:, l, :] = g
            d_ref[:, l, :] = delta
            m2_ref[:, l, :] = m2
            v2_ref[:, l, :] = v2

    blk = pl.BlockSpec((ADAM_COL_ROWS, nl, d), lambda i: (i, 0, 0))
    part = pl.BlockSpec((N_DEV, ADAM_COL_ROWS, d), lambda i: (0, first + i, 0))
    shp = jax.ShapeDtypeStruct(w.shape, F32)
    return pl.pallas_call(
        body, name="adam_w_in", grid=(pl.cdiv(nc, ADAM_COL_ROWS),),
        in_specs=[part] * nl + [blk, blk, blk],
        out_specs=[blk, blk, blk, blk],
        out_shape=[shp, shp, shp, shp],
        compiler_params=_cparams(("parallel",)),
    )(*parts, w, m, v)


def _adam_small(w, g, m, v):
    def body(w_ref, g_ref, m_ref, v_ref, d_ref, m2_ref, v2_ref):
        delta, m2, v2 = _adam_math(w_ref[...], g_ref[...], m_ref[...], v_ref[...])
        d_ref[...] = delta
        m2_ref[...] = m2
        v2_ref[...] = v2

    shp = jax.ShapeDtypeStruct(w.shape, F32)
    return pl.pallas_call(body, name="adam_small", out_shape=[shp, shp, shp])(w, g, m, v)


def _position():
    return lax.axis_index("x"), lax.axis_index("y"), lax.axis_index("c")


def _slot(p):
    return 4 * p[0] + 2 * p[1] + p[2]


BF16_TILE_ROWS = 16


def _slab_rows(rows, cols):
    return -(-(rows + cols) // BF16_TILE_ROWS) * BF16_TILE_ROWS


RELAYOUT_COLS = 512
RELAYOUT_CHUNK = 64


def _shard_pieces(dev, rows, cols):
    moved = ((0, GA_ORIG, 0), (GA_ORIG, GA_ORIG + GLA_RANK, OFF_GA - GA_ORIG), (GA_ORIG + GLA_RANK, D_IN, -GLA_RANK))
    c0, c1 = dev * cols, (dev + 1) * cols
    return [(rows + max(c0, lo) - c0, max(c0, lo) + off, min(c1, hi) - max(c0, lo))
            for lo, hi, off in moved if max(c0, lo) < min(c1, hi)]


def _move_rows(src, src_row, dst, dst_row, n):
    assert src_row % 2 == 0 and dst_row % 2 == 0 and n % 2 == 0
    for r in range(0, n // 2, RELAYOUT_CHUNK):
        m = min(RELAYOUT_CHUNK, n // 2 - r)
        dst[dst_row // 2 + r:dst_row // 2 + r + m, :] = src[src_row // 2 + r:src_row // 2 + r + m, :]


def _aligned_weight(land, rows, cols):
    _, slab, d = land.shape
    ct = min(RELAYOUT_COLS, d)

    def body(land_ref, wt_ref):
        dev = pl.program_id(1)
        src = land_ref.bitcast(jnp.uint32)
        dst = wt_ref.bitcast(jnp.uint32)

        @pl.when(dev == 0)
        def _():
            dst[D_IN // 2:D_ZP // 2, :] = jnp.zeros(((D_ZP - D_IN) // 2, ct), jnp.uint32)

        for k in range(N_DEV):
            @pl.when(dev == k)
            def _(k=k):
                for at, to, n in _shard_pieces(k, rows, cols):
                    _move_rows(src, at, dst, to, n)

    return pl.pallas_call(
        body, name="aligned_weight", grid=(d // ct, N_DEV),
        in_specs=[pl.BlockSpec((slab, ct), lambda c, dev: (dev, c))],
        out_specs=pl.BlockSpec((D_ZP, ct), lambda c, dev: (0, c)),
        out_shape=jax.ShapeDtypeStruct((D_ZP, d), land.dtype),
        compiler_params=_cparams(("parallel", "arbitrary")),
    )(land.reshape(N_DEV * slab, d))


def _partial_slabs(dwo, dwt, rows, cols):
    d = dwt.shape[1]
    slab = _slab_rows(rows, cols)
    ct = min(RELAYOUT_COLS, d)

    def body(dwo_ref, dwt_ref, out_ref):
        dev = pl.program_id(1)
        src = dwt_ref.bitcast(jnp.uint32)
        dst = out_ref.bitcast(jnp.uint32)
        out_ref[0:rows, :] = dwo_ref[...]
        dst[(rows + cols) // 2:slab // 2, :] = jnp.zeros(((slab - rows - cols) // 2, ct), jnp.uint32)
        for k in range(N_DEV):
            @pl.when(dev == k)
            def _(k=k):
                for to, at, n in _shard_pieces(k, rows, cols):
                    _move_rows(src, at, dst, to, n)

    return pl.pallas_call(
        body, name="partial_slabs", grid=(d // ct, N_DEV),
        in_specs=[pl.BlockSpec((rows, ct), lambda c, dev: (dev, c)),
                  pl.BlockSpec((D_ZP, ct), lambda c, dev: (0, c))],
        out_specs=pl.BlockSpec((slab, ct), lambda c, dev: (dev, c)),
        out_shape=jax.ShapeDtypeStruct((N_DEV * slab, d), dwt.dtype),
        compiler_params=_cparams(("parallel", "arbitrary")),
    )(dwo, dwt).reshape(N_DEV, slab, d)


def _peer(pos, k):
    x, y, c = pos
    return (1 - x if k & 4 else x, 1 - y if k & 2 else y, 1 - c if k & 1 else c)


HBM_SPEC = pl.BlockSpec(memory_space=pltpu.HBM)
SEM_SPEC = pl.BlockSpec(memory_space=pltpu.SEMAPHORE)
GATHER_PEERS = (1, 4, 2, 6)
ALL_PEERS = (1, 2, 3, 4, 5, 6, 7)


def _hbm(a):
    return pltpu.with_memory_space_constraint(a, pltpu.HBM)


def _split_copies(src_ref, land_ref, send_sems, recv_sems, ks, per_peer, landed):
    me = _position()
    out = []
    for i, k in enumerate(ks):
        peer = _peer(me, k)
        src = src_ref.at[_slot(peer)] if per_peer else src_ref
        dst = land_ref.at[_slot(peer) if landed else _slot(me)]
        out.append(pltpu.make_async_remote_copy(
            src_ref=src, dst_ref=dst, send_sem=send_sems.at[i], recv_sem=recv_sems.at[i],
            device_id=peer, device_id_type=MESH))
    return out


def _exchange_start(src, after, ks, per_peer, name):
    slab = src.shape[1:] if per_peer else src.shape
    land_shape = (N_DEV,) + tuple(slab)
    n = len(ks)

    def body(src_ref, land_ref, after_ref, send_sems, recv_sems, src_thru, land_thru, token):
        for cp in _split_copies(src_ref, land_ref, send_sems, recv_sems, ks, per_peer, landed=False):
            cp.start()
        token[...] = jnp.zeros_like(token)

    return pl.pallas_call(
        body, name=name,
        out_shape=(pltpu.SemaphoreType.DMA((n,)), pltpu.SemaphoreType.DMA((n,)),
                   pltpu.HBM(src.shape, src.dtype), pltpu.HBM(land_shape, src.dtype),
                   jax.ShapeDtypeStruct((8, LANE), F32)),
        in_specs=(HBM_SPEC, HBM_SPEC, ANY),
        out_specs=(SEM_SPEC, SEM_SPEC, HBM_SPEC, HBM_SPEC, pl.BlockSpec(memory_space=pltpu.VMEM)),
        input_output_aliases={0: 2, 1: 3},
        compiler_params=pltpu.CompilerParams(has_side_effects=pltpu.SideEffectType.DATAFLOW_SIDE_EFFECTING),
    )(_hbm(src), _hbm(lax.empty(land_shape, src.dtype)), after)


def _exchange_wait(started, after, ks, per_peer, name):
    send_sems, recv_sems, src_thru, land_thru = started

    def body(src_ref, land_ref, send_sems, recv_sems, after_ref, src_dead, land_out):
        for cp in _split_copies(src_ref, land_ref, send_sems, recv_sems, ks, per_peer, landed=True):
            cp.wait_send()
            cp.wait_recv()

    return pl.pallas_call(
        body, name=name,
        out_shape=(pltpu.HBM(src_thru.shape, src_thru.dtype), pltpu.HBM(land_thru.shape, land_thru.dtype)),
        in_specs=(HBM_SPEC, HBM_SPEC, SEM_SPEC, SEM_SPEC, ANY), out_specs=(HBM_SPEC, HBM_SPEC),
        input_output_aliases={0: 0, 1: 1},
        compiler_params=pltpu.CompilerParams(has_side_effects=pltpu.SideEffectType.DATAFLOW_SIDE_EFFECTING),
    )(src_thru, land_thru, send_sems, recv_sems, after)


def _relay_copies(land_ref, send_sems, recv_sems, landed):
    me = _position()
    sibling = _peer(me, 1)
    out = []
    for i, k in enumerate(GATHER_PEERS[1:]):
        blk = land_ref.at[_slot(_peer(sibling if landed else me, k))]
        out.append(pltpu.make_async_remote_copy(
            src_ref=blk, dst_ref=blk, send_sem=send_sems.at[i], recv_sem=recv_sems.at[i],
            device_id=sibling, device_id_type=MESH))
    return out


def _relay_start(land, name):
    n = len(GATHER_PEERS) - 1

    def body(land_ref, send_sems, recv_sems, land_thru, token):
        for cp in _relay_copies(land_ref, send_sems, recv_sems, landed=False):
            cp.start()
        token[...] = jnp.zeros_like(token)

    return pl.pallas_call(
        body, name=name,
        out_shape=(pltpu.SemaphoreType.DMA((n,)), pltpu.SemaphoreType.DMA((n,)),
                   pltpu.HBM(land.shape, land.dtype), jax.ShapeDtypeStruct((8, LANE), F32)),
        in_specs=(HBM_SPEC,),
        out_specs=(SEM_SPEC, SEM_SPEC, HBM_SPEC, pl.BlockSpec(memory_space=pltpu.VMEM)),
        input_output_aliases={0: 2},
        compiler_params=pltpu.CompilerParams(has_side_effects=pltpu.SideEffectType.DATAFLOW_SIDE_EFFECTING),
    )(_hbm(land))


def _relay_wait(started, after, name):
    send_sems, recv_sems, land_thru = started

    def body(land_ref, send_sems, recv_sems, after_ref, land_out):
        for cp in _relay_copies(land_ref, send_sems, recv_sems, landed=True):
            cp.wait_send()
            cp.wait_recv()

    return pl.pallas_call(
        body, name=name,
        out_shape=pltpu.HBM(land_thru.shape, land_thru.dtype),
        in_specs=(HBM_SPEC, SEM_SPEC, SEM_SPEC, ANY), out_specs=HBM_SPEC,
        input_output_aliases={0: 0},
        compiler_params=pltpu.CompilerParams(has_side_effects=pltpu.SideEffectType.DATAFLOW_SIDE_EFFECTING),
    )(land_thru, send_sems, recv_sems, after)


def _exchange(arrs, name):
    n = len(arrs)

    def body(*refs):
        ins, outs = refs[:n], refs[n:2 * n]
        send_sems, recv_sems, local_sems = refs[2 * n:]
        me = _position()

        def copy(a, k):
            peer = _peer(me, k)
            return pltpu.make_async_remote_copy(
                src_ref=ins[a].at[_slot(peer)], dst_ref=outs[a].at[_slot(me)],
                send_sem=send_sems.at[a * 7 + k - 1], recv_sem=recv_sems.at[a * 7 + k - 1],
                device_id=peer, device_id_type=MESH)

        def landed(a, k):
            peer = _peer(me, k)
            return pltpu.make_async_remote_copy(
                src_ref=ins[a].at[_slot(peer)], dst_ref=outs[a].at[_slot(peer)],
                send_sem=send_sems.at[a * 7 + k - 1], recv_sem=recv_sems.at[a * 7 + k - 1],
                device_id=peer, device_id_type=MESH)

        mine = [pltpu.make_async_copy(ins[a].at[_slot(me)], outs[a].at[_slot(me)], local_sems.at[a])
                for a in range(n)]
        for cp in mine:
            cp.start()
        sent = [copy(a, k) for k in range(1, N_DEV) for a in range(n)]
        for cp in sent:
            cp.start()
        for k in range(1, N_DEV):
            for a in range(n):
                landed(a, k).wait_recv()
        for cp in sent:
            cp.wait_send()
        for cp in mine:
            cp.wait()

    return pl.pallas_call(
        body, name=name,
        in_specs=[ANY] * n, out_specs=[ANY] * n,
        out_shape=[jax.ShapeDtypeStruct(a.shape, a.dtype) for a in arrs],
        scratch_shapes=[pltpu.SemaphoreType.DMA((7 * n,)), pltpu.SemaphoreType.DMA((7 * n,)),
                        pltpu.SemaphoreType.DMA((n,))],
    )(*arrs)


def _sum_slots(parts):
    def body(p_ref, o_ref):
        acc = p_ref[0]
        for dev in range(1, N_DEV):
            acc = acc + p_ref[dev]
        o_ref[...] = acc

    return pl.pallas_call(body, name="sum_slots",
                          out_shape=jax.ShapeDtypeStruct(parts.shape[1:], F32))(parts)


PACK_ROWS = 8


def _packed_rows(size):
    return -(-size // (PACK_ROWS * LANE)) * PACK_ROWS


def _pack(arrs):
    def rows(a):
        flat = a.reshape(-1)
        return jnp.pad(flat, (0, _packed_rows(flat.shape[0]) * LANE - flat.shape[0])).reshape(-1, LANE)

    return jnp.concatenate([rows(a) for a in arrs], axis=0)


def _unpack(packed, shapes):
    out, at = [], 0
    for shp in shapes:
        size = 1
        for dim in shp:
            size *= dim
        nrows = _packed_rows(size)
        out.append(packed[at:at + nrows].reshape(-1)[:size].reshape(shp))
        at += nrows
    return out


def _layer_fwd(x, wt, wo, g_pre, g_post, wa_pad, b_alpha, g_gla, g_att, rb_pad, midway=None):
    h = _rms_fwd(x, g_pre)
    z = _matmul(h, wt, "nt", F32, *TILES["in_proj"], "in_proj", n_outer=True)
    y_gla, o_gla, states = _gla_fwd(z, wa_pad, b_alpha, g_gla)
    if midway is not None:
        g_att = g_att + midway(y_gla)[:1, :1]
    y_att, o_att = _att_fwd(z, rb_pad, g_att)
    ycat = jnp.concatenate([y_gla, y_att], axis=1)
    y = _matmul(ycat, wo, "nn", F32, *TILES["out_proj"], "out_proj", n_outer=True)
    out = _post_fwd(x, y, g_post)
    return out, (x, h, z, o_gla, states, o_att, ycat, y)


def _layer_bwd(dout, saved, wt, wo, g_pre, g_post, wa_pad, b_alpha, g_gla, g_att, rb_pad, on_dw):
    x, h, z, o_gla, states, o_att, ycat, y = saved
    dy, dg_post = _post_bwd(dout, y, g_post)
    dycat = _matmul(dy, wo, "nt", F32, *TILES["out_proj_dx"], "out_proj_dx", n_outer=True)
    dwo = _matmul(ycat, dy, "tn", BF16, *TILES["out_proj_dw"], "out_proj_dw")
    dq, dk, dv, dgg, dga, dwa, db, dg_gla = _gla_bwd(dycat, o_gla, z, wa_pad, b_alpha, g_gla, states)
    daq, dak, dav, dag, drb, dg_att = _att_bwd(dycat, o_att, z, rb_pad, g_att)
    dz = jnp.concatenate([dq, dk, dv, dgg, daq, dak.astype(BF16), dav.astype(BF16), dag, dga], axis=1)
    dwt = _matmul(dz, h, "tn", BF16, *TILES["in_proj_dw"], "in_proj_dw")
    token = on_dw(dwo, dwt)
    dh = _matmul(dz, wt, "nn", F32, *TILES["in_proj_dx"], "in_proj_dx", n_outer=True, after=token)
    dx, dg_pre = _pre_bwd(dh, x, g_pre, dout)
    small = (dg_pre[0], dg_post[0], dwa[:GLA_RANK], db[0], dg_gla[0], dg_att[0], drb[:, 0, :N_REL])
    return dx, small


def kernel(x, w_in, w_out, g_pre, g_post, w_alpha, b_alpha, g_gla, g_att, rel_bias, loss_target, m_w_in, m_w_out, m_g_pre, m_g_post, m_w_alpha, m_b_alpha, m_g_gla, m_g_att, m_rel_bias, v_w_in, v_w_out, v_g_pre, v_g_post, v_w_alpha, v_b_alpha, v_g_gla, v_g_att, v_rel_bias):
    nl, d, cols = w_in.shape
    rows = w_out.shape[1]
    s = x.shape[1]
    x0 = x.reshape(s, d)
    tgt = loss_target.reshape(s, d)

    cols_first = lambda a: jnp.transpose(a, (2, 0, 1))
    w_c = cols_first(w_in)
    slab = _slab_rows(rows, cols)
    is_out = lax.broadcasted_iota(jnp.int32, (slab, d), 0) < rows

    def shard(l, zero=0.0):
        top = jnp.pad((w_out[l] + zero).astype(BF16), ((0, slab - rows), (0, 0)))
        rest = jnp.pad(w_c[:, l].astype(BF16), ((rows, slab - rows - cols), (0, 0)))
        return jnp.where(is_out, top, rest)

    first_fetch = _exchange_start(shard(0), x, GATHER_PEERS, False, "gather_start_0")
    began = first_fetch[4][0, 0]
    shards = [None] + [shard(l, began) for l in range(1, nl)]
    alpha = _pack([w_alpha]) + began
    wa_g = _exchange([jnp.broadcast_to(alpha[None], (N_DEV,) + alpha.shape)], "gather_alpha")[0]
    wa_cols = w_alpha.shape[2]
    wa_full = wa_g.reshape(N_DEV, -1)[:, :nl * GLA_RANK * wa_cols].reshape(N_DEV, nl, GLA_RANK, wa_cols)
    wa_full = jnp.transpose(wa_full, (1, 2, 0, 3)).reshape(nl, GLA_RANK, GLA_KW)
    wa_pad = jnp.pad(wa_full, ((0, 0), (0, LANE - GLA_RANK), (0, 0)))
    rb_pad = jnp.pad(rel_bias, ((0, 0), (0, 0), (0, 3 * LANE - N_REL)))

    def layer_args(l, follows_pre=None, follows_post=None):
        gp = g_pre[l:l + 1] if follows_pre is None else g_pre[l:l + 1] + follows_pre[:1, :1]
        gq = g_post[l:l + 1] if follows_post is None else g_post[l:l + 1] + follows_post[:1, :1]
        return (wts[l], wos[l], gp, gq, wa_pad[l], b_alpha[l:l + 1], g_gla[l:l + 1], g_att[l:l + 1], rb_pad[l])

    my = _slot(_position())

    def fetch(l, after):
        return _exchange_start(shards[l], after, GATHER_PEERS, False, f"gather_start_{l}")

    def relay(l, first_hop, after):
        own[l], land = _exchange_wait(first_hop[:4], after, GATHER_PEERS, False, f"gather_wait_{l}")
        return _relay_start(land, f"relay_start_{l}")

    def midway(l, y):
        flight["relay"] = relay(l + 1, flight["fetch"], y)
        if l + 2 >= nl:
            return flight["relay"][3]
        flight["fetch"] = fetch(l + 2, flight["relay"][2])
        return flight["fetch"][4]

    act, saved, wts, wos, flight, own = x0, [], [], [], {}, [None] * nl
    prepared = (wa_pad[0, :1, :1] + sum(sh[:1, :1].astype(F32) for sh in shards[1:]))
    flight["relay"] = relay(0, first_fetch, prepared)
    if nl > 1:
        flight["fetch"] = fetch(1, flight["relay"][2])
    for l in range(nl):
        land = _relay_wait(flight["relay"][:3], act, f"relay_wait_{l}")
        land = lax.dynamic_update_slice_in_dim(land, own[l][None], my, 0)
        wos.append(land[:, :rows].reshape(N_DEV * rows, d))
        wts.append(_aligned_weight(land, rows, cols))
        act, sv = _layer_fwd(act, *layer_args(l),
                             midway=functools.partial(midway, l) if l + 1 < nl else None)
        saved.append(sv)
    dout, sq = _loss_head(act, tgt)
    loss = lax.psum(sq[0, 0] * (0.5 / d), ("x", "y", "c"))

    smalls, pending = [None] * nl, [None] * nl

    def send_partials(l, dwo, dwt):
        pending[l] = _exchange_start(_partial_slabs(dwo, dwt, rows, cols), dwt, ALL_PEERS, True,
                                     f"scatter_start_{l}")
        return pending[l][4]

    for l in reversed(range(nl)):
        dout, smalls[l] = _layer_bwd(dout, saved[l], *layer_args(l), on_dw=functools.partial(send_partials, l))
    grad_x = dout.reshape(x.shape)

    names = 7
    small_stacked = [jnp.stack([smalls[l][i] for l in range(nl)]) for i in range(names)]
    shapes = [a.shape for a in small_stacked]
    packed = _pack(small_stacked)
    gathered = _exchange([jnp.broadcast_to(packed[None], (N_DEV,) + packed.shape)], "gather_small_grads")[0]
    g_pre_g, g_post_g, wa_g_full, b_g, gla_g, att_g, rb_g = _unpack(_sum_slots(gathered), shapes)
    wa_g_mine = lax.dynamic_slice_in_dim(wa_g_full, my * wa_cols, wa_cols, axis=2)
    grads = [g_pre_g, g_post_g, wa_g_mine, b_g, gla_g, att_g, rb_g]
    ws = [g_pre, g_post, w_alpha, b_alpha, g_gla, g_att, rel_bias]
    ms = [m_g_pre, m_g_post, m_w_alpha, m_b_alpha, m_g_gla, m_g_att, m_rel_bias]
    vs = [v_g_pre, v_g_post, v_w_alpha, v_b_alpha, v_g_gla, v_g_att, v_rel_bias]
    shapes2 = [a.shape for a in ws]
    d_p, m2_p, v2_p = _adam_small(_pack(ws), _pack(grads), _pack(ms), _pack(vs))
    d_s, m2_s, v2_s = _unpack(d_p, shapes2), _unpack(m2_p, shapes2), _unpack(v2_p, shapes2)

    parts = []
    for l in range(nl):
        partial, land = _exchange_wait(pending[l][:4], d_p, ALL_PEERS, True, f"scatter_wait_{l}")
        parts.append(lax.dynamic_update_slice_in_dim(land, lax.dynamic_slice_in_dim(partial, my, 1, 0), my, 0))
    g_w_in, d_w_in, m2_w_in, v2_w_in = [
        jnp.transpose(a, (1, 2, 0))
        for a in _adam_columns(parts, rows // ADAM_COL_ROWS, w_c, cols_first(m_w_in), cols_first(v_w_in))]
    g_w_out, d_w_out, m2_w_out, v2_w_out = _adam_sharded(parts, 0, w_out, m_w_out, v_w_out, "adam_w_out")

    def ordered(big_in, big_out, small):
        return [big_in, big_out] + list(small)

    return (loss, grad_x,
            *ordered(g_w_in, g_w_out, grads),
            *ordered(d_w_in, d_w_out, d_s),
            *ordered(m2_w_in, m2_w_out, m2_s),
            *ordered(v2_w_in, v2_w_out, v2_s))
```

```python
import functools

import jax
import jax.numpy as jnp
from jax import lax
from jax.experimental import pallas as pl
from jax.experimental.pallas import tpu as pltpu

F32 = jnp.float32
BF16 = jnp.bfloat16
MESH = pl.DeviceIdType.MESH
ANY = pl.BlockSpec(memory_space=pl.ANY)

CHUNK = 64
GLA_HEADS = 4
GLA_DK = 128
GLA_DV = 256
GLA_KW = GLA_HEADS * GLA_DK
D_GLA = GLA_HEADS * GLA_DV
GLA_RANK = 16
GLA_TAU = 16.0
ATT_HEADS = 8
ATT_HD = 128
D_ATT = ATT_HEADS * ATT_HD
LEFT_CHUNKS = 8
REL_CLIP = 128
N_REL = 2 * REL_CLIP + 1
EPS = 1e-6
D_IN = 2 * GLA_KW + 2 * D_GLA + GLA_RANK + 4 * D_ATT
GLA_SCALE = GLA_DK ** -0.5
ATT_SCALE = ATT_HD ** -0.5

ADAM_LR = 0.001
ADAM_B1 = 0.9
ADAM_B2 = 0.999
ADAM_EPS = 1e-08
ADAM_WD = 0.01
ADAM_STEP = 10

N_DEV = 8
LANE = 128
GA_ORIG = 2 * GLA_KW + 2 * D_GLA
OFF_AQ = GA_ORIG
OFF_GA = GA_ORIG + 4 * D_ATT
D_ZP = OFF_GA + LANE
QB = 2 * CHUNK
ATT_UNROLL = 8
WIN = (LEFT_CHUNKS + 2) * CHUNK
ET_ROWS = WIN + LEFT_CHUNKS * CHUNK
NEG = -1e30
VMEM_LIMIT = 48 * 1024 * 1024


def _cparams(sem):
    return pltpu.CompilerParams(dimension_semantics=sem, vmem_limit_bytes=VMEM_LIMIT)


def _dot(a, b):
    return jnp.dot(a, b, preferred_element_type=F32)


def _dot_nt(a, b):
    return lax.dot_general(a, b, (((1,), (1,)), ((), ())), preferred_element_type=F32)


def _dot_tn(a, b):
    return lax.dot_general(a, b, (((0,), (0,)), ((), ())), preferred_element_type=F32)


def _dot01(t, x, left=True):
    if not left:
        t, x = x, t
    hi = x.astype(BF16)
    r = x - hi.astype(F32)
    mid = r.astype(BF16)
    lo = (r - mid.astype(F32)).astype(BF16)
    if left:
        return _dot(t, hi) + _dot(t, mid) + _dot(t, lo)
    return _dot(hi, t) + _dot(mid, t) + _dot(lo, t)


def _sigmoid(x):
    return 1.0 / (1.0 + jnp.exp(-x))


def _log_sigmoid(x):
    return jnp.minimum(x, 0.0) - jnp.log(1.0 + jnp.exp(-jnp.abs(x)))


TILES = {
    "in_proj": (512, D_ZP // 3, None),
    "in_proj_dx": (512, 512, None),
    "in_proj_dw": (D_ZP // 3, 512, None),
    "out_proj": (512, 1024, None),
    "out_proj_dx": (512, 1024, None),
    "out_proj_dw": (1024, 1024, None),
}


def _matmul(a, b, mode, out_dtype, tm, tn, tk, name, n_outer=False, after=None):
    if mode == "nn":
        (m, k), n = a.shape, b.shape[1]
    elif mode == "nt":
        (m, k), n = a.shape, b.shape[0]
    else:
        (k, m), n = a.shape, b.shape[1]
    tm, tn, tk = min(tm, m), min(tn, n), k if tk is None else min(tk, k)
    assert m % tm == 0 and n % tn == 0 and k % tk == 0, (name, m, n, k)
    nk = k // tk
    dot = {"nn": _dot, "nt": _dot_nt, "tn": _dot_tn}[mode]

    follows = [] if after is None else [after]

    def body_whole_k(a_ref, b_ref, *rest):
        o_ref = rest[-1]
        o_ref[...] = dot(a_ref[...], b_ref[...]).astype(out_dtype)

    def body(a_ref, b_ref, *rest):
        o_ref, acc_ref = rest[-2:]
        kk = pl.program_id(2)

        @pl.when(kk == 0)
        def _():
            acc_ref[...] = jnp.zeros_like(acc_ref)

        acc_ref[...] += dot(a_ref[...], b_ref[...])

        @pl.when(kk == nk - 1)
        def _():
            o_ref[...] = acc_ref[...].astype(out_dtype)

    def at(index):
        return (lambda j, i, kk: index(i, j, kk)) if n_outer else index

    if mode == "tn":
        a_spec = pl.BlockSpec((tk, tm), at(lambda i, j, kk: (kk, i)))
    else:
        a_spec = pl.BlockSpec((tm, tk), at(lambda i, j, kk: (i, kk)))
    if mode == "nt":
        b_spec = pl.BlockSpec((tn, tk), at(lambda i, j, kk: (j, kk)))
    else:
        b_spec = pl.BlockSpec((tk, tn), at(lambda i, j, kk: (kk, j)))
    return pl.pallas_call(
        body_whole_k if nk == 1 else body, name=name,
        grid=(n // tn, m // tm, nk) if n_outer else (m // tm, n // tn, nk),
        in_specs=[a_spec, b_spec] + [ANY] * len(follows),
        out_specs=pl.BlockSpec((tm, tn), at(lambda i, j, kk: (i, j))),
        out_shape=jax.ShapeDtypeStruct((m, n), out_dtype),
        scratch_shapes=[] if nk == 1 else [pltpu.VMEM((tm, tn), F32)],
        compiler_params=_cparams(("parallel", "parallel", "arbitrary")),
    )(a, b, *follows)


ROWS = 256


def _rms_fwd(x, g):
    s, d = x.shape

    def body(x_ref, g_ref, h_ref):
        xv = x_ref[...]
        r = lax.rsqrt(jnp.mean(xv * xv, axis=-1, keepdims=True) + EPS)
        h_ref[...] = (xv * r * g_ref[...]).astype(BF16)

    return pl.pallas_call(
        body, name="rms_fwd", grid=(s // ROWS,),
        in_specs=[pl.BlockSpec((ROWS, d), lambda i: (i, 0)), pl.BlockSpec((1, d), lambda i: (0, 0))],
        out_specs=pl.BlockSpec((ROWS, d), lambda i: (i, 0)),
        out_shape=jax.ShapeDtypeStruct((s, d), BF16),
        compiler_params=_cparams(("parallel",)),
    )(x, g)


def _post_fwd(x, y, g):
    s, d = x.shape

    def body(x_ref, y_ref, g_ref, o_ref):
        yv = y_ref[...]
        r = lax.rsqrt(jnp.mean(yv * yv, axis=-1, keepdims=True) + EPS)
        o_ref[...] = x_ref[...] + yv * r * g_ref[...]

    row = pl.BlockSpec((ROWS, d), lambda i: (i, 0))
    return pl.pallas_call(
        body, name="post_fwd", grid=(s // ROWS,),
        in_specs=[row, row, pl.BlockSpec((1, d), lambda i: (0, 0))],
        out_specs=row,
        out_shape=jax.ShapeDtypeStruct((s, d), F32),
        compiler_params=_cparams(("parallel",)),
    )(x, y, g)


def _loss_head(out, tgt):
    s, d = out.shape

    def body(o_ref, t_ref, dout_ref, sum_ref):
        @pl.when(pl.program_id(0) == 0)
        def _():
            sum_ref[...] = jnp.zeros_like(sum_ref)

        e = o_ref[...] - t_ref[...]
        dout_ref[...] = e * (1.0 / d)
        sum_ref[...] += jnp.sum(jnp.sum(e * e, axis=1, keepdims=True), axis=0, keepdims=True)

    row = pl.BlockSpec((ROWS, d), lambda i: (i, 0))
    return pl.pallas_call(
        body, name="loss_head", grid=(s // ROWS,),
        in_specs=[row, row],
        out_specs=[row, pl.BlockSpec((1, 1), lambda i: (0, 0))],
        out_shape=[jax.ShapeDtypeStruct((s, d), F32), jax.ShapeDtypeStruct((1, 1), F32)],
        compiler_params=_cparams(("arbitrary",)),
    )(out, tgt)


def _post_bwd(dout, y, g):
    s, d = y.shape

    def body(do_ref, y_ref, g_ref, dy_ref, dg_ref):
        @pl.when(pl.program_id(0) == 0)
        def _():
            dg_ref[...] = jnp.zeros_like(dg_ref)

        yv = y_ref[...]
        dv = do_ref[...]
        r = lax.rsqrt(jnp.mean(yv * yv, axis=-1, keepdims=True) + EPS)
        dg_ref[...] += jnp.sum(dv * yv * r, axis=0, keepdims=True)
        w = dv * g_ref[...]
        dy = r * (w - yv * (r * r) * jnp.mean(w * yv, axis=-1, keepdims=True))
        dy_ref[...] = dy.astype(BF16)

    row = pl.BlockSpec((ROWS, d), lambda i: (i, 0))
    vec = pl.BlockSpec((1, d), lambda i: (0, 0))
    return pl.pallas_call(
        body, name="post_bwd", grid=(s // ROWS,),
        in_specs=[row, row, vec],
        out_specs=[row, vec],
        out_shape=[jax.ShapeDtypeStruct((s, d), BF16), jax.ShapeDtypeStruct((1, d), F32)],
        compiler_params=_cparams(("arbitrary",)),
    )(dout, y, g)


def _pre_bwd(dh, x, g, dout):
    s, d = x.shape

    def body(dh_ref, x_ref, g_ref, do_ref, dx_ref, dg_ref):
        @pl.when(pl.program_id(0) == 0)
        def _():
            dg_ref[...] = jnp.zeros_like(dg_ref)

        xv = x_ref[...]
        dv = dh_ref[...]
        r = lax.rsqrt(jnp.mean(xv * xv, axis=-1, keepdims=True) + EPS)
        dg_ref[...] += jnp.sum(dv * xv * r, axis=0, keepdims=True)
        w = dv * g_ref[...]
        dx_ref[...] = do_ref[...] + r * (w - xv * (r * r) * jnp.mean(w * xv, axis=-1, keepdims=True))

    row = pl.BlockSpec((ROWS, d), lambda i: (i, 0))
    vec = pl.BlockSpec((1, d), lambda i: (0, 0))
    return pl.pallas_call(
        body, name="pre_bwd", grid=(s // ROWS,),
        in_specs=[row, row, vec, row],
        out_specs=[row, vec],
        out_shape=[jax.ShapeDtypeStruct((s, d), F32), jax.ShapeDtypeStruct((1, d), F32)],
        compiler_params=_cparams(("arbitrary",)),
    )(dh, x, g, dout)


GLA_STEP = 4
GLA_ROWS = GLA_STEP * CHUNK
GLA_CHUNKS = [slice(c * CHUNK, (c + 1) * CHUNK) for c in range(GLA_STEP)]


def _chunk_triangles():
    ri = lax.broadcasted_iota(jnp.int32, (GLA_ROWS, GLA_ROWS), 0)
    ci = lax.broadcasted_iota(jnp.int32, (GLA_ROWS, GLA_ROWS), 1)
    same = (ri // CHUNK) == (ci // CHUNK)
    return (jnp.where(same & (ri >= ci), 1.0, 0.0).astype(BF16), jnp.where(same & (ci >= ri), 1.0, 0.0).astype(BF16))


def _per_chunk(fn, like):
    row = lax.broadcasted_iota(jnp.int32, like.shape, 0)
    return [fn((row >= c * CHUNK) & (row < (c + 1) * CHUNK)) for c in range(GLA_STEP)]


def _spread(per_chunk, like):
    row = lax.broadcasted_iota(jnp.int32, like.shape, 0)
    out = per_chunk[-1]
    for c in reversed(range(GLA_STEP - 1)):
        out = jnp.where(row < (c + 1) * CHUNK, per_chunk[c], out)
    return out


def _gla_gate(ga_b, wa_b, b_ref, tri):
    pre = _dot(ga_b, wa_b) + b_ref[...]
    la = _log_sigmoid(pre) * (1.0 / GLA_TAU)
    cum = _dot01(tri, la)
    row = lax.broadcasted_iota(jnp.int32, cum.shape, 0)
    cends = [jnp.sum(jnp.where(row == (c + 1) * CHUNK - 1, cum, 0.0), axis=0, keepdims=True)
             for c in range(GLA_STEP)]
    return pre, cum, cends


def _heads(width):
    return [slice(h * width, (h + 1) * width) for h in range(GLA_HEADS)]


def _z_specs_gla(rev=None):
    idx = (lambda n: n) if rev is None else rev
    return [
        pl.BlockSpec((GLA_ROWS, GLA_KW), lambda n: (idx(n), 0)),
        pl.BlockSpec((GLA_ROWS, GLA_KW), lambda n: (idx(n), 1)),
        pl.BlockSpec((GLA_ROWS, D_GLA), lambda n: (idx(n), 1)),
        pl.BlockSpec((GLA_ROWS, D_GLA), lambda n: (idx(n), 2)),
        pl.BlockSpec((GLA_ROWS, LANE), lambda n: (idx(n), OFF_GA // LANE)),
    ]


def _gla_fwd(z, wa_pad, b_alpha, g_gla):
    s = z.shape[0]
    nchunk = s // CHUNK

    def body(q_ref, k_ref, v_ref, gg_ref, ga_ref, wa_ref, b_ref, g_ref, y_ref, o_ref, st_ref, state):
        @pl.when(pl.program_id(0) == 0)
        def _():
            state[...] = jnp.zeros_like(state)

        ga_b = ga_ref[...].astype(BF16)
        tri, _ = _chunk_triangles()
        nh = range(GLA_HEADS)
        keys, vals = _heads(GLA_DK), _heads(GLA_DV)
        _, cum, cends = _gla_gate(ga_b, wa_ref[...].astype(BF16), b_ref, tri)
        kd_b = (k_ref[...] * jnp.exp(_spread(cends, cum) - cum)).astype(BF16)
        qs = (q_ref[...] * GLA_SCALE).astype(BF16)
        v_b = v_ref[...].astype(BF16)
        uts = [[_dot_tn(v_b[rs, vals[h]], kd_b[rs, keys[h]]) for h in nh] for rs in GLA_CHUNKS]
        sts, prev = [], [state[h] for h in nh]
        for c in range(GLA_STEP):
            a = jnp.exp(cends[c])
            prev = [prev[h] * a[:, keys[h]] + uts[c][h] for h in nh]
            sts.append(prev)
        for h in nh:
            state[h] = prev[h]
            for c in range(GLA_STEP):
                st_ref[c, h] = sts[c][h]
        outs = [[_dot_nt(qs[rs, keys[h]], sts[c][h].astype(BF16)) for h in nh] for c, rs in enumerate(GLA_CHUNKS)]
        for h in nh:
            o, vs = jnp.concatenate([outs[c][h] for c in range(GLA_STEP)], axis=0), vals[h]
            o_ref[:, vs] = o
            r = lax.rsqrt(jnp.mean(o * o, axis=-1, keepdims=True) + EPS)
            gg = gg_ref[:, vs]
            y_ref[:, vs] = (o * r * g_ref[:, vs] * (gg * _sigmoid(gg))).astype(BF16)

    full = lambda shape: pl.BlockSpec(shape, lambda n: tuple(0 for _ in shape))
    wide = pl.BlockSpec((GLA_ROWS, D_GLA), lambda n: (n, 0))
    return pl.pallas_call(
        body, name="gla_fwd", grid=(nchunk // GLA_STEP,),
        in_specs=_z_specs_gla() + [full((LANE, GLA_KW)), full((1, GLA_KW)), full((1, D_GLA))],
        out_specs=[wide, wide, pl.BlockSpec((GLA_STEP, GLA_HEADS, GLA_DV, GLA_DK), lambda n: (n, 0, 0, 0))],
        out_shape=[jax.ShapeDtypeStruct((s, D_GLA), BF16), jax.ShapeDtypeStruct((s, D_GLA), F32),
                   jax.ShapeDtypeStruct((nchunk, GLA_HEADS, GLA_DV, GLA_DK), F32)],
        scratch_shapes=[pltpu.VMEM((GLA_HEADS, GLA_DV, GLA_DK), F32)],
        compiler_params=_cparams(("arbitrary",)),
    )(z, z, z, z, z, wa_pad, b_alpha, g_gla)


def _gla_bwd(dyc, o_gla, z, wa_pad, b_alpha, g_gla, states):
    s = z.shape[0]
    nsteps = s // GLA_ROWS
    rev = lambda n: nsteps - 1 - n

    def body(dy_ref, o_ref, q_ref, k_ref, v_ref, gg_ref, ga_ref, wa_ref, b_ref, g_ref, st_ref, stp_ref,
             dq_ref, dk_ref, dv_ref, dgg_ref, dga_ref, dwa_ref, db_ref, dg_ref, carry):
        step = pl.program_id(0)

        @pl.when(step == 0)
        def _():
            carry[...] = jnp.zeros_like(carry)
            dwa_ref[...] = jnp.zeros_like(dwa_ref)
            db_ref[...] = jnp.zeros_like(db_ref)
            dg_ref[...] = jnp.zeros_like(dg_ref)

        has_prev = (step < nsteps - 1).astype(F32)
        ga_b = ga_ref[...].astype(BF16)
        tri, tri_up = _chunk_triangles()
        nh, nc = range(GLA_HEADS), range(GLA_STEP)
        keys, vals = _heads(GLA_DK), _heads(GLA_DV)
        wa_b = wa_ref[...].astype(BF16)
        pre, cum, cends = _gla_gate(ga_b, wa_b, b_ref, tri)
        e = jnp.exp(_spread(cends, cum) - cum)
        a = [jnp.exp(cends[c]) for c in nc]
        kf = k_ref[...]
        kd_b = (kf * e).astype(BF16)
        v_b = v_ref[...].astype(BF16)
        qs = (q_ref[...] * GLA_SCALE).astype(BF16)
        do_b = []
        for h in nh:
            vs = vals[h]
            o = o_ref[:, vs]
            gg = gg_ref[:, vs]
            g = g_ref[:, vs]
            dy = dy_ref[:, vs]
            r = lax.rsqrt(jnp.mean(o * o, axis=-1, keepdims=True) + EPS)
            sg = _sigmoid(gg)
            dogn = dy * (gg * sg)
            dgg_ref[:, vs] = (dy * (o * r * g) * (sg * (1.0 + gg * (1.0 - sg)))).astype(BF16)
            dg_ref[:, vs] += jnp.sum(dogn * o * r, axis=0, keepdims=True)
            w = dogn * g
            do_b.append((r * (w - o * (r * r) * jnp.mean(w * o, axis=-1, keepdims=True))).astype(BF16))
        dqs = [jnp.concatenate([_dot(do_b[h][rs], st_ref[c, h].astype(BF16)) for c, rs in enumerate(GLA_CHUNKS)],
                               axis=0) for h in nh]
        dq_ref[...] = (jnp.concatenate(dqs, axis=1) * GLA_SCALE).astype(BF16)
        own = [[_dot_tn(do_b[h][rs], qs[rs, keys[h]]) for h in nh] for rs in GLA_CHUNKS]
        gts, later = [None] * GLA_STEP, [carry[h] for h in nh]
        for c in reversed(nc):
            gts[c] = [own[c][h] + later[h] for h in nh]
            later = [gts[c][h] * a[c][:, keys[h]] for h in nh]
        for h in nh:
            carry[h] = later[h]
        gt_b = [[gts[c][h].astype(BF16) for h in nh] for c in nc]
        dkd = jnp.concatenate([jnp.concatenate([_dot(v_b[rs, vals[h]], gt_b[c][h]) for h in nh], axis=1)
                               for c, rs in enumerate(GLA_CHUNKS)], axis=0)
        dvs = [[_dot_nt(kd_b[rs, keys[h]], gt_b[c][h]) for h in nh] for c, rs in enumerate(GLA_CHUNKS)]
        before = lambda c, h: st_ref[c - 1, h] if c > 0 else stp_ref[0, h] * has_prev
        da = [jnp.concatenate([jnp.sum(gts[c][h] * before(c, h), axis=0, keepdims=True) for h in nh], axis=1)
              for c in nc]
        for h in nh:
            dv_ref[:, vals[h]] = jnp.concatenate([dvs[c][h] for c in nc], axis=0).astype(BF16)
        dk_ref[...] = (dkd * e).astype(BF16)
        dd = dkd * kf * e
        dsum = _per_chunk(lambda mine: jnp.sum(jnp.where(mine, dd, 0.0), axis=0, keepdims=True), dd)
        dcend = _spread([dsum[c] + da[c] * a[c] for c in nc], dd)
        dla = dcend - _dot01(tri_up, dd)
        dpre = dla * (1.0 / GLA_TAU) * (1.0 - _sigmoid(pre))
        dpre_b = dpre.astype(BF16)
        dga_ref[...] = _dot_nt(dpre_b, wa_b).astype(BF16)
        dwa_ref[...] += _dot_tn(ga_b, dpre_b)
        db_ref[...] += jnp.sum(dpre, axis=0, keepdims=True)

    full = lambda shape: pl.BlockSpec(shape, lambda n: tuple(0 for _ in shape))
    wide = pl.BlockSpec((GLA_ROWS, D_GLA), lambda n: (rev(n), 0))
    keyw = pl.BlockSpec((GLA_ROWS, GLA_KW), lambda n: (rev(n), 0))
    st_spec = pl.BlockSpec((GLA_STEP, GLA_HEADS, GLA_DV, GLA_DK), lambda n: (rev(n), 0, 0, 0))
    stp_spec = pl.BlockSpec((1, GLA_HEADS, GLA_DV, GLA_DK),
                            lambda n: (jnp.maximum(GLA_STEP * rev(n) - 1, 0), 0, 0, 0))
    return pl.pallas_call(
        body, name="gla_bwd", grid=(nsteps,),
        in_specs=[wide, wide] + _z_specs_gla(rev)
        + [full((LANE, GLA_KW)), full((1, GLA_KW)), full((1, D_GLA)), st_spec, stp_spec],
        out_specs=[keyw, keyw, wide, wide, pl.BlockSpec((GLA_ROWS, LANE), lambda n: (rev(n), 0)),
                   full((LANE, GLA_KW)), full((1, GLA_KW)), full((1, D_GLA))],
        out_shape=[jax.ShapeDtypeStruct((s, GLA_KW), BF16), jax.ShapeDtypeStruct((s, GLA_KW), BF16),
                   jax.ShapeDtypeStruct((s, D_GLA), BF16), jax.ShapeDtypeStruct((s, D_GLA), BF16),
                   jax.ShapeDtypeStruct((s, LANE), BF16),
                   jax.ShapeDtypeStruct((LANE, GLA_KW), F32), jax.ShapeDtypeStruct((1, GLA_KW), F32),
                   jax.ShapeDtypeStruct((1, D_GLA), F32)],
        scratch_shapes=[pltpu.VMEM((GLA_HEADS, GLA_DV, GLA_DK), F32)],
        compiler_params=_cparams(("arbitrary",)),
    )(dyc, o_gla, z, z, z, z, z, wa_pad, b_alpha, g_gla, states, states)


def _build_bias_table(rb_row, et_ref):
    far = jnp.broadcast_to(rb_row[:, 2 * REL_CLIP:2 * REL_CLIP + 1], (1, LANE))
    near_hi = rb_row[:, REL_CLIP:2 * REL_CLIP]
    near_lo = rb_row[:, 0:REL_CLIP]
    past = jnp.broadcast_to(rb_row[:, 0:1], (1, LANE))
    seg = [far, far, far, far, near_hi, near_lo] + [past] * (ET_ROWS // LANE - 5)
    ri = lax.broadcasted_iota(jnp.int32, (LANE, LANE), 0)
    ci = lax.broadcasted_iota(jnp.int32, (LANE, LANE), 1)
    for kb in range(ET_ROWS // LANE):
        wmat = jnp.where(ri + ci < LANE, seg[kb], seg[kb + 1])
        blk = pltpu.roll(wmat, 0, 1, stride=1, stride_axis=0)
        lag = LEFT_CHUNKS + ci // CHUNK - (2 * kb + ri // CHUNK)
        et_ref[kb * LANE:(kb + 1) * LANE, :] = jnp.where((lag >= 0) & (lag <= LEFT_CHUNKS), blk, NEG)


def _reduce_bias_table(det_ref):
    lane = lax.broadcasted_iota(jnp.int32, (1, LANE), 1)
    ri = lax.broadcasted_iota(jnp.int32, (LANE, LANE), 0)
    ci = lax.broadcasted_iota(jnp.int32, (LANE, LANE), 1)
    flip = jnp.where(ri + ci == LANE - 1, 1.0, 0.0).astype(BF16)
    segs = jnp.zeros((8, LANE), F32)
    seg_row = lax.broadcasted_iota(jnp.int32, (8, LANE), 0)
    prev_minus = jnp.zeros((1, LANE), F32)
    for kb in range(6):
        rolled = pltpu.roll(_dot01(det_ref[kb * LANE:(kb + 1) * LANE, :], flip, left=False), 0, 1,
                            stride=1, stride_axis=0)
        plus = jnp.sum(jnp.where(ci >= ri, rolled, 0.0), axis=0, keepdims=True)
        minus = jnp.sum(jnp.where(ci < ri, rolled, 0.0), axis=0, keepdims=True)
        segs = segs + jnp.where(seg_row == kb, plus + prev_minus, 0.0)
        prev_minus = minus
    segs = _dot01(segs, flip, left=False)
    pick = lambda kb: jnp.sum(jnp.where(seg_row == kb, segs, 0.0), axis=0, keepdims=True)
    far = jnp.sum(pick(0) + pick(1) + pick(2) + pick(3), axis=1, keepdims=True)
    last = jnp.where(lane == 0, far, 0.0)
    return jnp.concatenate([pick(5), pick(4), last], axis=1)


def _att_window(b):
    c0 = 2 * b
    kstart = pl.multiple_of(jnp.maximum(c0 - LEFT_CHUNKS, 0) * CHUNK, CHUNK)
    eoff = pl.multiple_of(jnp.maximum(LEFT_CHUNKS - c0, 0) * CHUNK, CHUNK)
    return kstart, eoff


def _att_probs(q_b, kw_b, et):
    st = _dot_nt(kw_b, q_b) * ATT_SCALE + et
    m = jnp.max(st, axis=0, keepdims=True)
    ex = jnp.exp(st - m)
    return ex * (1.0 / jnp.sum(ex, axis=0, keepdims=True))


def _att_fwd(z, rb_pad, g_att):
    s = z.shape[0]
    nblk = s // QB
    c_aq, c_ak, c_av, c_ag = [(OFF_AQ + i * D_ATT) // ATT_HD for i in range(4)]

    def body(q_ref, k_ref, v_ref, ag_ref, rb_ref, g_ref, y_ref, o_ref, et_ref):
        h = pl.program_id(0)
        b = pl.program_id(1)

        @pl.when(b == 0)
        def _():
            _build_bias_table(rb_ref[pl.ds(h, 1), :], et_ref)

        for j in range(ATT_UNROLL):
            rs = slice(j * QB, (j + 1) * QB)
            kstart, eoff = _att_window(b * ATT_UNROLL + j)
            q_b = q_ref[rs, :].astype(BF16)
            kw_b = k_ref[pl.ds(kstart, WIN), :].astype(BF16)
            vw_b = v_ref[pl.ds(kstart, WIN), :].astype(BF16)
            pt = _att_probs(q_b, kw_b, et_ref[pl.ds(eoff, WIN), :])
            o = _dot_tn(pt.astype(BF16), vw_b)
            o_ref[rs, :] = o
            r = lax.rsqrt(jnp.mean(o * o, axis=-1, keepdims=True) + EPS)
            ag = ag_ref[rs, :]
            y_ref[rs, :] = (o * r * g_ref[...] * (ag * _sigmoid(ag))).astype(BF16)

    blk = lambda col: pl.BlockSpec((ATT_UNROLL * QB, ATT_HD), lambda h, b: (b, col + h))
    seq = lambda col: pl.BlockSpec((s, ATT_HD), lambda h, b: (0, col + h))
    out_blk = pl.BlockSpec((ATT_UNROLL * QB, ATT_HD), lambda h, b: (b, h))
    return pl.pallas_call(
        body, name="att_fwd", grid=(ATT_HEADS, nblk // ATT_UNROLL),
        in_specs=[blk(c_aq), seq(c_ak), seq(c_av), blk(c_ag),
                  pl.BlockSpec((ATT_HEADS, 3 * LANE), lambda h, b: (0, 0)),
                  pl.BlockSpec((1, ATT_HD), lambda h, b: (0, h))],
        out_specs=[out_blk, out_blk],
        out_shape=[jax.ShapeDtypeStruct((s, D_ATT), BF16), jax.ShapeDtypeStruct((s, D_ATT), F32)],
        scratch_shapes=[pltpu.VMEM((ET_ROWS, LANE), F32)],
        compiler_params=_cparams(("arbitrary", "arbitrary")),
    )(z, z, z, z, rb_pad, g_att)


def _att_bwd(dyc, o_att, z, rb_pad, g_att):
    s = z.shape[0]
    nblk = s // QB
    c_aq, c_ak, c_av, c_ag = [(OFF_AQ + i * D_ATT) // ATT_HD for i in range(4)]
    c_dy = D_GLA // ATT_HD

    def body(dy_ref, o_ref, q_ref, k_ref, v_ref, ag_ref, rb_ref, g_ref,
             dq_ref, dk_ref, dv_ref, dag_ref, drb_ref, dg_ref, et_ref, det_ref):
        h = pl.program_id(0)
        b = pl.program_id(1)

        @pl.when(b == 0)
        def _():
            _build_bias_table(rb_ref[pl.ds(h, 1), :], et_ref)
            det_ref[...] = jnp.zeros_like(det_ref)
            dk_ref[...] = jnp.zeros_like(dk_ref)
            dv_ref[...] = jnp.zeros_like(dv_ref)
            dg_ref[...] = jnp.zeros_like(dg_ref)

        g = g_ref[...]
        dg = jnp.zeros((1, ATT_HD), F32)
        for j in range(ATT_UNROLL):
            rs = slice(j * QB, (j + 1) * QB)
            kstart, eoff = _att_window(b * ATT_UNROLL + j)
            q_b = q_ref[rs, :].astype(BF16)
            kw_b = k_ref[pl.ds(kstart, WIN), :].astype(BF16)
            vw_b = v_ref[pl.ds(kstart, WIN), :].astype(BF16)
            pt = _att_probs(q_b, kw_b, et_ref[pl.ds(eoff, WIN), :])
            o = o_ref[rs, :]
            ag = ag_ref[rs, :]
            dy = dy_ref[rs, :]
            r = lax.rsqrt(jnp.mean(o * o, axis=-1, keepdims=True) + EPS)
            sg = _sigmoid(ag)
            don = dy * (ag * sg)
            dag_ref[rs, :] = (dy * (o * r * g) * (sg * (1.0 + ag * (1.0 - sg)))).astype(BF16)
            dg = dg + jnp.sum(don * o * r, axis=0, keepdims=True)
            w = don * g
            do_b = (r * (w - o * (r * r) * jnp.mean(w * o, axis=-1, keepdims=True))).astype(BF16)
            pt_b = pt.astype(BF16)
            dpt = _dot_nt(vw_b, do_b)
            dst = pt * (dpt - jnp.sum(dpt * pt, axis=0, keepdims=True))
            det_ref[pl.ds(eoff, WIN), :] += dst
            ds_b = (dst * ATT_SCALE).astype(BF16)
            dq_ref[rs, :] = _dot_tn(ds_b, kw_b).astype(BF16)
            dk_ref[pl.ds(kstart, WIN), :] += _dot(ds_b, q_b)
            dv_ref[pl.ds(kstart, WIN), :] += _dot(pt_b, do_b)
        dg_ref[...] += dg

        @pl.when(b == nblk // ATT_UNROLL - 1)
        def _():
            drb_ref[0] = jnp.broadcast_to(_reduce_bias_table(det_ref), (8, 3 * LANE))

    blk = lambda col: pl.BlockSpec((ATT_UNROLL * QB, ATT_HD), lambda h, b: (b, col + h))
    seq = lambda col: pl.BlockSpec((s, ATT_HD), lambda h, b: (0, col + h))
    out_blk = pl.BlockSpec((ATT_UNROLL * QB, ATT_HD), lambda h, b: (b, h))
    out_seq = pl.BlockSpec((s, ATT_HD), lambda h, b: (0, h))
    return pl.pallas_call(
        body, name="att_bwd", grid=(ATT_HEADS, nblk // ATT_UNROLL),
        in_specs=[blk(c_dy), blk(0), blk(c_aq), seq(c_ak), seq(c_av), blk(c_ag),
                  pl.BlockSpec((ATT_HEADS, 3 * LANE), lambda h, b: (0, 0)),
                  pl.BlockSpec((1, ATT_HD), lambda h, b: (0, h))],
        out_specs=[out_blk, out_seq, out_seq, out_blk,
                   pl.BlockSpec((1, 8, 3 * LANE), lambda h, b: (h, 0, 0)),
                   pl.BlockSpec((1, ATT_HD), lambda h, b: (0, h))],
        out_shape=[jax.ShapeDtypeStruct((s, D_ATT), BF16), jax.ShapeDtypeStruct((s, D_ATT), F32),
                   jax.ShapeDtypeStruct((s, D_ATT), F32), jax.ShapeDtypeStruct((s, D_ATT), BF16),
                   jax.ShapeDtypeStruct((ATT_HEADS, 8, 3 * LANE), F32),
                   jax.ShapeDtypeStruct((1, D_ATT), F32)],
        scratch_shapes=[pltpu.VMEM((ET_ROWS, LANE), F32), pltpu.VMEM((ET_ROWS, LANE), F32)],
        compiler_params=_cparams(("arbitrary", "arbitrary")),
    )(dyc, o_att, z, z, z, z, rb_pad, g_att)


ADAM_ROWS = 64
ADAM_COL_ROWS = 32


def _adam_math(w, g, m, v):
    m2 = ADAM_B1 * m + (1.0 - ADAM_B1) * g
    v2 = ADAM_B2 * v + (1.0 - ADAM_B2) * (g * g)
    m_hat = m2 / (1.0 - ADAM_B1 ** ADAM_STEP)
    v_hat = v2 / (1.0 - ADAM_B2 ** ADAM_STEP)
    delta = -ADAM_LR * (m_hat / (jnp.sqrt(v_hat) + ADAM_EPS) + ADAM_WD * w)
    return delta, m2, v2


def _adam_sharded(parts, first, w, m, v, name):
    nl, nr, nc = w.shape

    def body(*refs):
        p_refs = refs[:nl]
        w_ref, m_ref, v_ref, g_ref, d_ref, m2_ref, v2_ref = refs[nl:]
        for k in range(nl):
            @pl.when(pl.program_id(0) == k)
            def _(p_ref=p_refs[k]):
                g = p_ref[0].astype(F32)
                for dev in range(1, N_DEV):
                    g = g + p_ref[dev].astype(F32)
                delta, m2, v2 = _adam_math(w_ref[0], g, m_ref[0], v_ref[0])
                g_ref[0] = g
                d_ref[0] = delta
                m2_ref[0] = m2
                v2_ref[0] = v2

    def part_spec(k):
        return pl.BlockSpec((N_DEV, ADAM_ROWS, nc), lambda l, i: (0, first + jnp.where(l == k, i, 0), 0))

    blk = pl.BlockSpec((1, ADAM_ROWS, nc), lambda l, i: (l, i, 0))
    shp = jax.ShapeDtypeStruct(w.shape, F32)
    return pl.pallas_call(
        body, name=name, grid=(nl, pl.cdiv(nr, ADAM_ROWS)),
        in_specs=[part_spec(k) for k in range(nl)] + [blk, blk, blk],
        out_specs=[blk, blk, blk, blk],
        out_shape=[shp, shp, shp, shp],
        compiler_params=_cparams(("arbitrary", "arbitrary")),
    )(*parts, w, m, v)


def _adam_columns(parts, first, w, m, v):
    nc, nl, d = w.shape

    def body(*refs):
        p_refs = refs[:nl]
        w_ref, m_ref, v_ref, g_ref, d_ref, m2_ref, v2_ref = refs[nl:]
        for l in range(nl):
            g = p_refs[l][0].astype(F32)
            for dev in range(1, N_DEV):
                g = g + p_refs[l][dev].astype(F32)
            delta, m2, v2 = _adam_math(w_ref[:, l, :], g, m_ref[:, l, :], v_ref[:, l, :])
            g_ref[:, l, :] = g
            d_ref[:, l, :] = delta
            m2_ref[:, l, :] = m2
            v2_ref[:, l, :] = v2

    blk = pl.BlockSpec((ADAM_COL_ROWS, nl, d), lambda i: (i, 0, 0))
    part = pl.BlockSpec((N_DEV, ADAM_COL_ROWS, d), lambda i: (0, first + i, 0))
    shp = jax.ShapeDtypeStruct(w.shape, F32)
    return pl.pallas_call(
        body, name="adam_w_in", grid=(pl.cdiv(nc, ADAM_COL_ROWS),),
        in_specs=[part] * nl + [blk, blk, blk],
        out_specs=[blk, blk, blk, blk],
        out_shape=[shp, shp, shp, shp],
        compiler_params=_cparams(("parallel",)),
    )(*parts, w, m, v)


def _adam_small(w, g, m, v):
    def body(w_ref, g_ref, m_ref, v_ref, d_ref, m2_ref, v2_ref):
        delta, m2, v2 = _adam_math(w_ref[...], g_ref[...], m_ref[...], v_ref[...])
        d_ref[...] = delta
        m2_ref[...] = m2
        v2_ref[...] = v2

    shp = jax.ShapeDtypeStruct(w.shape, F32)
    return pl.pallas_call(body, name="adam_small", out_shape=[shp, shp, shp])(w, g, m, v)


def _position():
    return lax.axis_index("x"), lax.axis_index("y"), lax.axis_index("c")


def _slot(p):
    return 4 * p[0] + 2 * p[1] + p[2]


BF16_TILE_ROWS = 16


def _slab_rows(rows, cols):
    return -(-(rows + cols) // BF16_TILE_ROWS) * BF16_TILE_ROWS


RELAYOUT_COLS = 512
RELAYOUT_CHUNK = 64


def _shard_pieces(dev, rows, cols):
    moved = ((0, GA_ORIG, 0), (GA_ORIG, GA_ORIG + GLA_RANK, OFF_GA - GA_ORIG), (GA_ORIG + GLA_RANK, D_IN, -GLA_RANK))
    c0, c1 = dev * cols, (dev + 1) * cols
    return [(rows + max(c0, lo) - c0, max(c0, lo) + off, min(c1, hi) - max(c0, lo))
            for lo, hi, off in moved if max(c0, lo) < min(c1, hi)]


def _move_rows(src, src_row, dst, dst_row, n):
    assert src_row % 2 == 0 and dst_row % 2 == 0 and n % 2 == 0
    for r in range(0, n // 2, RELAYOUT_CHUNK):
        m = min(RELAYOUT_CHUNK, n // 2 - r)
        dst[dst_row // 2 + r:dst_row // 2 + r + m, :] = src[src_row // 2 + r:src_row // 2 + r + m, :]


def _aligned_weight(land, rows, cols):
    _, slab, d = land.shape
    ct = min(RELAYOUT_COLS, d)

    def body(land_ref, wt_ref):
        dev = pl.program_id(1)
        src = land_ref.bitcast(jnp.uint32)
        dst = wt_ref.bitcast(jnp.uint32)

        @pl.when(dev == 0)
        def _():
            dst[D_IN // 2:D_ZP // 2, :] = jnp.zeros(((D_ZP - D_IN) // 2, ct), jnp.uint32)

        for k in range(N_DEV):
            @pl.when(dev == k)
            def _(k=k):
                for at, to, n in _shard_pieces(k, rows, cols):
                    _move_rows(src, at, dst, to, n)

    return pl.pallas_call(
        body, name="aligned_weight", grid=(d // ct, N_DEV),
        in_specs=[pl.BlockSpec((slab, ct), lambda c, dev: (dev, c))],
        out_specs=pl.BlockSpec((D_ZP, ct), lambda c, dev: (0, c)),
        out_shape=jax.ShapeDtypeStruct((D_ZP, d), land.dtype),
        compiler_params=_cparams(("parallel", "arbitrary")),
    )(land.reshape(N_DEV * slab, d))


def _partial_slabs(dwt, cols):
    d = dwt.shape[1]
    slab = _slab_rows(0, cols)
    ct = min(RELAYOUT_COLS, d)

    def body(dwt_ref, out_ref):
        dev = pl.program_id(1)
        src = dwt_ref.bitcast(jnp.uint32)
        dst = out_ref.bitcast(jnp.uint32)
        dst[cols // 2:slab // 2, :] = jnp.zeros(((slab - cols) // 2, ct), jnp.uint32)
        for k in range(N_DEV):
            @pl.when(dev == k)
            def _(k=k):
                for to, at, n in _shard_pieces(k, 0, cols):
                    _move_rows(src, at, dst, to, n)

    return pl.pallas_call(
        body, name="partial_slabs", grid=(d // ct, N_DEV),
        in_specs=[pl.BlockSpec((D_ZP, ct), lambda c, dev: (0, c))],
        out_specs=pl.BlockSpec((slab, ct), lambda c, dev: (dev, c)),
        out_shape=jax.ShapeDtypeStruct((N_DEV * slab, d), dwt.dtype),
        compiler_params=_cparams(("parallel", "arbitrary")),
    )(dwt).reshape(N_DEV, slab, d)


def _peer(pos, k):
    x, y, c = pos
    return (1 - x if k & 4 else x, 1 - y if k & 2 else y, 1 - c if k & 1 else c)


HBM_SPEC = pl.BlockSpec(memory_space=pltpu.HBM)
SEM_SPEC = pl.BlockSpec(memory_space=pltpu.SEMAPHORE)
GATHER_PEERS = (1, 4, 2, 6)
ALL_PEERS = (1, 2, 3, 4, 5, 6, 7)


def _hbm(a):
    return pltpu.with_memory_space_constraint(a, pltpu.HBM)


def _split_copies(src_ref, land_ref, send_sems, recv_sems, ks, per_peer, landed):
    me = _position()
    out = []
    for i, k in enumerate(ks):
        peer = _peer(me, k)
        src = src_ref.at[_slot(peer)] if per_peer else src_ref
        dst = land_ref.at[_slot(peer) if landed else _slot(me)]
        out.append(pltpu.make_async_remote_copy(
            src_ref=src, dst_ref=dst, send_sem=send_sems.at[i], recv_sem=recv_sems.at[i],
            device_id=peer, device_id_type=MESH))
    return out


def _exchange_start(src, after, ks, per_peer, name):
    slab = src.shape[1:] if per_peer else src.shape
    land_shape = (N_DEV,) + tuple(slab)
    n = len(ks)

    def body(src_ref, land_ref, after_ref, send_sems, recv_sems, src_thru, land_thru, token):
        for cp in _split_copies(src_ref, land_ref, send_sems, recv_sems, ks, per_peer, landed=False):
            cp.start()
        token[...] = jnp.zeros_like(token)

    return pl.pallas_call(
        body, name=name,
        out_shape=(pltpu.SemaphoreType.DMA((n,)), pltpu.SemaphoreType.DMA((n,)),
                   pltpu.HBM(src.shape, src.dtype), pltpu.HBM(land_shape, src.dtype),
                   jax.ShapeDtypeStruct((8, LANE), F32)),
        in_specs=(HBM_SPEC, HBM_SPEC, ANY),
        out_specs=(SEM_SPEC, SEM_SPEC, HBM_SPEC, HBM_SPEC, pl.BlockSpec(memory_space=pltpu.VMEM)),
        input_output_aliases={0: 2, 1: 3},
        compiler_params=pltpu.CompilerParams(has_side_effects=pltpu.SideEffectType.DATAFLOW_SIDE_EFFECTING),
    )(_hbm(src), _hbm(lax.empty(land_shape, src.dtype)), after)


def _exchange_wait(started, after, ks, per_peer, name):
    send_sems, recv_sems, src_thru, land_thru = started

    def body(src_ref, land_ref, send_sems, recv_sems, after_ref, src_dead, land_out):
        for cp in _split_copies(src_ref, land_ref, send_sems, recv_sems, ks, per_peer, landed=True):
            cp.wait_send()
            cp.wait_recv()

    return pl.pallas_call(
        body, name=name,
        out_shape=(pltpu.HBM(src_thru.shape, src_thru.dtype), pltpu.HBM(land_thru.shape, land_thru.dtype)),
        in_specs=(HBM_SPEC, HBM_SPEC, SEM_SPEC, SEM_SPEC, ANY), out_specs=(HBM_SPEC, HBM_SPEC),
        input_output_aliases={0: 0, 1: 1},
        compiler_params=pltpu.CompilerParams(has_side_effects=pltpu.SideEffectType.DATAFLOW_SIDE_EFFECTING),
    )(src_thru, land_thru, send_sems, recv_sems, after)


def _relay_copies(land_ref, send_sems, recv_sems, landed):
    me = _position()
    sibling = _peer(me, 1)
    out = []
    for i, k in enumerate(GATHER_PEERS[1:]):
        blk = land_ref.at[_slot(_peer(sibling if landed else me, k))]
        out.append(pltpu.make_async_remote_copy(
            src_ref=blk, dst_ref=blk, send_sem=send_sems.at[i], recv_sem=recv_sems.at[i],
            device_id=sibling, device_id_type=MESH))
    return out


def _relay_start(land, name):
    n = len(GATHER_PEERS) - 1

    def body(land_ref, send_sems, recv_sems, land_thru, token):
        for cp in _relay_copies(land_ref, send_sems, recv_sems, landed=False):
            cp.start()
        token[...] = jnp.zeros_like(token)

    return pl.pallas_call(
        body, name=name,
        out_shape=(pltpu.SemaphoreType.DMA((n,)), pltpu.SemaphoreType.DMA((n,)),
                   pltpu.HBM(land.shape, land.dtype), jax.ShapeDtypeStruct((8, LANE), F32)),
        in_specs=(HBM_SPEC,),
        out_specs=(SEM_SPEC, SEM_SPEC, HBM_SPEC, pl.BlockSpec(memory_space=pltpu.VMEM)),
        input_output_aliases={0: 2},
        compiler_params=pltpu.CompilerParams(has_side_effects=pltpu.SideEffectType.DATAFLOW_SIDE_EFFECTING),
    )(_hbm(land))


def _relay_wait(started, after, name):
    send_sems, recv_sems, land_thru = started

    def body(land_ref, send_sems, recv_sems, after_ref, land_out):
        for cp in _relay_copies(land_ref, send_sems, recv_sems, landed=True):
            cp.wait_send()
            cp.wait_recv()

    return pl.pallas_call(
        body, name=name,
        out_shape=pltpu.HBM(land_thru.shape, land_thru.dtype),
        in_specs=(HBM_SPEC, SEM_SPEC, SEM_SPEC, ANY), out_specs=HBM_SPEC,
        input_output_aliases={0: 0},
        compiler_params=pltpu.CompilerParams(has_side_effects=pltpu.SideEffectType.DATAFLOW_SIDE_EFFECTING),
    )(land_thru, send_sems, recv_sems, after)


def _exchange(arrs, name):
    n = len(arrs)

    def body(*refs):
        ins, outs = refs[:n], refs[n:2 * n]
        send_sems, recv_sems, local_sems = refs[2 * n:]
        me = _position()

        def copy(a, k):
            peer = _peer(me, k)
            return pltpu.make_async_remote_copy(
                src_ref=ins[a].at[_slot(peer)], dst_ref=outs[a].at[_slot(me)],
                send_sem=send_sems.at[a * 7 + k - 1], recv_sem=recv_sems.at[a * 7 + k - 1],
                device_id=peer, device_id_type=MESH)

        def landed(a, k):
            peer = _peer(me, k)
            return pltpu.make_async_remote_copy(
                src_ref=ins[a].at[_slot(peer)], dst_ref=outs[a].at[_slot(peer)],
                send_sem=send_sems.at[a * 7 + k - 1], recv_sem=recv_sems.at[a * 7 + k - 1],
                device_id=peer, device_id_type=MESH)

        mine = [pltpu.make_async_copy(ins[a].at[_slot(me)], outs[a].at[_slot(me)], local_sems.at[a])
                for a in range(n)]
        for cp in mine:
            cp.start()
        sent = [copy(a, k) for k in range(1, N_DEV) for a in range(n)]
        for cp in sent:
            cp.start()
        for k in range(1, N_DEV):
            for a in range(n):
                landed(a, k).wait_recv()
        for cp in sent:
            cp.wait_send()
        for cp in mine:
            cp.wait()

    return pl.pallas_call(
        body, name=name,
        in_specs=[ANY] * n, out_specs=[ANY] * n,
        out_shape=[jax.ShapeDtypeStruct(a.shape, a.dtype) for a in arrs],
        scratch_shapes=[pltpu.SemaphoreType.DMA((7 * n,)), pltpu.SemaphoreType.DMA((7 * n,)),
                        pltpu.SemaphoreType.DMA((n,))],
    )(*arrs)


def _sum_slots(parts):
    def body(p_ref, o_ref):
        acc = p_ref[0]
        for dev in range(1, N_DEV):
            acc = acc + p_ref[dev]
        o_ref[...] = acc

    return pl.pallas_call(body, name="sum_slots",
                          out_shape=jax.ShapeDtypeStruct(parts.shape[1:], F32))(parts)


PACK_ROWS = 8


def _packed_rows(size):
    return -(-size // (PACK_ROWS * LANE)) * PACK_ROWS


def _pack(arrs):
    def rows(a):
        flat = a.reshape(-1)
        return jnp.pad(flat, (0, _packed_rows(flat.shape[0]) * LANE - flat.shape[0])).reshape(-1, LANE)

    return jnp.concatenate([rows(a) for a in arrs], axis=0)


def _unpack(packed, shapes):
    out, at = [], 0
    for shp in shapes:
        size = 1
        for dim in shp:
            size *= dim
        nrows = _packed_rows(size)
        out.append(packed[at:at + nrows].reshape(-1)[:size].reshape(shp))
        at += nrows
    return out


def _layer_fwd(x, wt, wo, g_pre, g_post, wa_pad, b_alpha, g_gla, g_att, rb_pad, midway=None):
    h = _rms_fwd(x, g_pre)
    z = _matmul(h, wt, "nt", F32, *TILES["in_proj"], "in_proj", n_outer=True)
    y_gla, o_gla, states = _gla_fwd(z, wa_pad, b_alpha, g_gla)
    if midway is not None:
        g_att = g_att + midway(y_gla)[:1, :1]
    y_att, o_att = _att_fwd(z, rb_pad, g_att)
    ycat = jnp.concatenate([y_gla, y_att], axis=1)
    y = _matmul(ycat, wo, "nn", F32, *TILES["out_proj"], "out_proj", n_outer=True)
    out = _post_fwd(x, y, g_post)
    return out, (x, h, z, o_gla, states, o_att, ycat, y)


def _layer_bwd(dout, saved, wt, wo, g_pre, g_post, wa_pad, b_alpha, g_gla, g_att, rb_pad, on_dwo, on_dwt):
    x, h, z, o_gla, states, o_att, ycat, y = saved
    dy, dg_post = _post_bwd(dout, y, g_post)
    dwo = _matmul(ycat, dy, "tn", BF16, *TILES["out_proj_dw"], "out_proj_dw")
    token = on_dwo(dwo)
    dycat = _matmul(dy, wo, "nt", F32, *TILES["out_proj_dx"], "out_proj_dx", n_outer=True, after=token)
    dq, dk, dv, dgg, dga, dwa, db, dg_gla = _gla_bwd(dycat, o_gla, z, wa_pad, b_alpha, g_gla, states)
    daq, dak, dav, dag, drb, dg_att = _att_bwd(dycat, o_att, z, rb_pad, g_att)
    dz = jnp.concatenate([dq, dk, dv, dgg, daq, dak.astype(BF16), dav.astype(BF16), dag, dga], axis=1)
    dwt = _matmul(dz, h, "tn", BF16, *TILES["in_proj_dw"], "in_proj_dw")
    token = on_dwt(dwt)
    dh = _matmul(dz, wt, "nn", F32, *TILES["in_proj_dx"], "in_proj_dx", n_outer=True, after=token)
    dx, dg_pre = _pre_bwd(dh, x, g_pre, dout)
    small = (dg_pre[0], dg_post[0], dwa[:GLA_RANK], db[0], dg_gla[0], dg_att[0], drb[:, 0, :N_REL])
    return dx, small


def kernel(x, w_in, w_out, g_pre, g_post, w_alpha, b_alpha, g_gla, g_att, rel_bias, loss_target, m_w_in, m_w_out, m_g_pre, m_g_post, m_w_alpha, m_b_alpha, m_g_gla, m_g_att, m_rel_bias, v_w_in, v_w_out, v_g_pre, v_g_post, v_w_alpha, v_b_alpha, v_g_gla, v_g_att, v_rel_bias):
    nl, d, cols = w_in.shape
    rows = w_out.shape[1]
    s = x.shape[1]
    x0 = x.reshape(s, d)
    tgt = loss_target.reshape(s, d)

    cols_first = lambda a: jnp.transpose(a, (2, 0, 1))
    w_c = cols_first(w_in)
    slab = _slab_rows(rows, cols)
    is_out = lax.broadcasted_iota(jnp.int32, (slab, d), 0) < rows

    def shard(l, zero=0.0):
        top = jnp.pad((w_out[l] + zero).astype(BF16), ((0, slab - rows), (0, 0)))
        rest = jnp.pad((w_c[:, l] + zero).astype(BF16), ((rows, slab - rows - cols), (0, 0)))
        return jnp.where(is_out, top, rest)

    first_fetch = _exchange_start(shard(0), x, GATHER_PEERS, False, "gather_start_0")
    began = first_fetch[4][0, 0]
    shards = [None] + [shard(l, began) for l in range(1, nl)]
    alpha = _pack([w_alpha]) + began
    wa_g = _exchange([jnp.broadcast_to(alpha[None], (N_DEV,) + alpha.shape)], "gather_alpha")[0]
    wa_cols = w_alpha.shape[2]
    wa_full = wa_g.reshape(N_DEV, -1)[:, :nl * GLA_RANK * wa_cols].reshape(N_DEV, nl, GLA_RANK, wa_cols)
    wa_full = jnp.transpose(wa_full, (1, 2, 0, 3)).reshape(nl, GLA_RANK, GLA_KW)
    wa_pad = jnp.pad(wa_full, ((0, 0), (0, LANE - GLA_RANK), (0, 0)))
    rb_pad = jnp.pad(rel_bias, ((0, 0), (0, 0), (0, 3 * LANE - N_REL)))

    def layer_args(l, follows_pre=None, follows_post=None):
        gp = g_pre[l:l + 1] if follows_pre is None else g_pre[l:l + 1] + follows_pre[:1, :1]
        gq = g_post[l:l + 1] if follows_post is None else g_post[l:l + 1] + follows_post[:1, :1]
        return (wts[l], wos[l], gp, gq, wa_pad[l], b_alpha[l:l + 1], g_gla[l:l + 1], g_att[l:l + 1], rb_pad[l])

    my = _slot(_position())

    def fetch(l, after):
        return _exchange_start(shards[l], after, GATHER_PEERS, False, f"gather_start_{l}")

    def relay(l, first_hop, after):
        own[l], land = _exchange_wait(first_hop[:4], after, GATHER_PEERS, False, f"gather_wait_{l}")
        return _relay_start(land, f"relay_start_{l}")

    def midway(l, y):
        flight["relay"] = relay(l + 1, flight["fetch"], y)
        if l + 2 >= nl:
            return flight["relay"][3]
        flight["fetch"] = fetch(l + 2, flight["relay"][2])
        return flight["fetch"][4]

    act, saved, wts, wos, flight, own = x0, [], [], [], {}, [None] * nl
    prepared = (wa_pad[0, :1, :1] + sum(sh[:1, :1].astype(F32) for sh in shards[1:]))
    flight["relay"] = relay(0, first_fetch, prepared)
    if nl > 1:
        flight["fetch"] = fetch(1, flight["relay"][2])
    for l in range(nl):
        land = _relay_wait(flight["relay"][:3], act, f"relay_wait_{l}")
        land = lax.dynamic_update_slice_in_dim(land, own[l][None], my, 0)
        wos.append(land[:, :rows].reshape(N_DEV * rows, d))
        wts.append(_aligned_weight(land, rows, cols))
        act, sv = _layer_fwd(act, *layer_args(l, follows_pre=first_fetch[4] if l == 0 else None),
                             midway=functools.partial(midway, l) if l + 1 < nl else None)
        saved.append(sv)
    dout, sq = _loss_head(act, tgt)
    loss = lax.psum(sq[0, 0] * (0.5 / d), ("x", "y", "c"))

    smalls, pending_out, pending_in = [None] * nl, [None] * nl, [None] * nl

    def send_out(l, dwo):
        pending_out[l] = _exchange_start(dwo.reshape(N_DEV, rows, d), dwo, ALL_PEERS, True, f"scatter_out_start_{l}")
        return pending_out[l][4]

    def send_in(l, dwt):
        pending_in[l] = _exchange_start(_partial_slabs(dwt, cols), dwt, ALL_PEERS, True, f"scatter_in_start_{l}")
        return pending_in[l][4]

    for l in reversed(range(nl)):
        dout, smalls[l] = _layer_bwd(dout, saved[l], *layer_args(l), on_dwo=functools.partial(send_out, l),
                                     on_dwt=functools.partial(send_in, l))
    grad_x = dout.reshape(x.shape)

    names = 7
    small_stacked = [jnp.stack([smalls[l][i] for l in range(nl)]) for i in range(names)]
    shapes = [a.shape for a in small_stacked]
    packed = _pack(small_stacked)
    gathered = _exchange([jnp.broadcast_to(packed[None], (N_DEV,) + packed.shape)], "gather_small_grads")[0]
    g_pre_g, g_post_g, wa_g_full, b_g, gla_g, att_g, rb_g = _unpack(_sum_slots(gathered), shapes)
    wa_g_mine = lax.dynamic_slice_in_dim(wa_g_full, my * wa_cols, wa_cols, axis=2)
    grads = [g_pre_g, g_post_g, wa_g_mine, b_g, gla_g, att_g, rb_g]
    ws = [g_pre, g_post, w_alpha, b_alpha, g_gla, g_att, rel_bias]
    ms = [m_g_pre, m_g_post, m_w_alpha, m_b_alpha, m_g_gla, m_g_att, m_rel_bias]
    vs = [v_g_pre, v_g_post, v_w_alpha, v_b_alpha, v_g_gla, v_g_att, v_rel_bias]
    shapes2 = [a.shape for a in ws]
    d_p, m2_p, v2_p = _adam_small(_pack(ws), _pack(grads), _pack(ms), _pack(vs))
    d_s, m2_s, v2_s = _unpack(d_p, shapes2), _unpack(m2_p, shapes2), _unpack(v2_p, shapes2)

    def landed(started, name):
        partial, land = _exchange_wait(started[:4], d_p, ALL_PEERS, True, name)
        return lax.dynamic_update_slice_in_dim(land, lax.dynamic_slice_in_dim(partial, my, 1, 0), my, 0)

    parts_out = [landed(pending_out[l], f"scatter_out_wait_{l}") for l in range(nl)]
    parts_in = [landed(pending_in[l], f"scatter_in_wait_{l}") for l in range(nl)]
    g_w_in, d_w_in, m2_w_in, v2_w_in = [
        jnp.transpose(a, (1, 2, 0))
        for a in _adam_columns(parts_in, 0, w_c, cols_first(m_w_in), cols_first(v_w_in))]
    g_w_out, d_w_out, m2_w_out, v2_w_out = _adam_sharded(parts_out, 0, w_out, m_w_out, v_w_out, "adam_w_out")

    def ordered(big_in, big_out, small):
        return [big_in, big_out] + list(small)

    return (loss, grad_x,
            *ordered(g_w_in, g_w_out, grads),
            *ordered(d_w_in, d_w_out, d_s),
            *ordered(m2_w_in, m2_w_out, m2_s),
            *ordered(v2_w_in, v2_w_out, v2_s))
```

```python
import functools

import jax
import jax.numpy as jnp
from jax import lax
from jax.experimental import pallas as pl
from jax.experimental.pallas import tpu as pltpu

F32 = jnp.float32
BF16 = jnp.bfloat16
MESH = pl.DeviceIdType.MESH
ANY = pl.BlockSpec(memory_space=pl.ANY)

CHUNK = 64
GLA_HEADS = 4
GLA_DK = 128
GLA_DV = 256
GLA_KW = GLA_HEADS * GLA_DK
D_GLA = GLA_HEADS * GLA_DV
GLA_RANK = 16
GLA_TAU = 16.0
ATT_HEADS = 8
ATT_HD = 128
D_ATT = ATT_HEADS * ATT_HD
LEFT_CHUNKS = 8
REL_CLIP = 128
N_REL = 2 * REL_CLIP + 1
EPS = 1e-6
D_IN = 2 * GLA_KW + 2 * D_GLA + GLA_RANK + 4 * D_ATT
GLA_SCALE = GLA_DK ** -0.5
ATT_SCALE = ATT_HD ** -0.5

ADAM_LR = 0.001
ADAM_B1 = 0.9
ADAM_B2 = 0.999
ADAM_EPS = 1e-08
ADAM_WD = 0.01
ADAM_STEP = 10

N_DEV = 8
LANE = 128
GA_ORIG = 2 * GLA_KW + 2 * D_GLA
OFF_AQ = GA_ORIG
OFF_GA = GA_ORIG + 4 * D_ATT
D_ZP = OFF_GA + LANE
QB = 2 * CHUNK
ATT_UNROLL = 8
WIN = (LEFT_CHUNKS + 2) * CHUNK
ET_ROWS = WIN + LEFT_CHUNKS * CHUNK
NEG = -1e30
VMEM_LIMIT = 48 * 1024 * 1024


def _cparams(sem):
    return pltpu.CompilerParams(dimension_semantics=sem, vmem_limit_bytes=VMEM_LIMIT)


def _dot(a, b):
    return jnp.dot(a, b, preferred_element_type=F32)


def _dot_nt(a, b):
    return lax.dot_general(a, b, (((1,), (1,)), ((), ())), preferred_element_type=F32)


def _dot_tn(a, b):
    return lax.dot_general(a, b, (((0,), (0,)), ((), ())), preferred_element_type=F32)


def _dot01(t, x, left=True):
    if not left:
        t, x = x, t
    hi = x.astype(BF16)
    r = x - hi.astype(F32)
    mid = r.astype(BF16)
    lo = (r - mid.astype(F32)).astype(BF16)
    if left:
        return _dot(t, hi) + _dot(t, mid) + _dot(t, lo)
    return _dot(hi, t) + _dot(mid, t) + _dot(lo, t)


def _sigmoid(x):
    return 1.0 / (1.0 + jnp.exp(-x))


def _log_sigmoid(x):
    return jnp.minimum(x, 0.0) - jnp.log(1.0 + jnp.exp(-jnp.abs(x)))


TILES = {
    "in_proj": (512, D_ZP // 3, None),
    "in_proj_dx": (512, 512, None),
    "in_proj_dw": (D_ZP // 3, 512, None),
    "out_proj": (512, 1024, None),
    "out_proj_dx": (512, 1024, None),
    "out_proj_dw": (1024, 1024, None),
}


def _matmul(a, b, mode, out_dtype, tm, tn, tk, name, n_outer=False, after=None):
    if mode == "nn":
        (m, k), n = a.shape, b.shape[1]
    elif mode == "nt":
        (m, k), n = a.shape, b.shape[0]
    else:
        (k, m), n = a.shape, b.shape[1]
    tm, tn, tk = min(tm, m), min(tn, n), k if tk is None else min(tk, k)
    assert m % tm == 0 and n % tn == 0 and k % tk == 0, (name, m, n, k)
    nk = k // tk
    dot = {"nn": _dot, "nt": _dot_nt, "tn": _dot_tn}[mode]

    follows = [] if after is None else [after]

    def body_whole_k(a_ref, b_ref, *rest):
        o_ref = rest[-1]
        o_ref[...] = dot(a_ref[...], b_ref[...]).astype(out_dtype)

    def body(a_ref, b_ref, *rest):
        o_ref, acc_ref = rest[-2:]
        kk = pl.program_id(2)

        @pl.when(kk == 0)
        def _():
            acc_ref[...] = jnp.zeros_like(acc_ref)

        acc_ref[...] += dot(a_ref[...], b_ref[...])

        @pl.when(kk == nk - 1)
        def _():
            o_ref[...] = acc_ref[...].astype(out_dtype)

    def at(index):
        return (lambda j, i, kk: index(i, j, kk)) if n_outer else index

    if mode == "tn":
        a_spec = pl.BlockSpec((tk, tm), at(lambda i, j, kk: (kk, i)))
    else:
        a_spec = pl.BlockSpec((tm, tk), at(lambda i, j, kk: (i, kk)))
    if mode == "nt":
        b_spec = pl.BlockSpec((tn, tk), at(lambda i, j, kk: (j, kk)))
    else:
        b_spec = pl.BlockSpec((tk, tn), at(lambda i, j, kk: (kk, j)))
    return pl.pallas_call(
        body_whole_k if nk == 1 else body, name=name,
        grid=(n // tn, m // tm, nk) if n_outer else (m // tm, n // tn, nk),
        in_specs=[a_spec, b_spec] + [ANY] * len(follows),
        out_specs=pl.BlockSpec((tm, tn), at(lambda i, j, kk: (i, j))),
        out_shape=jax.ShapeDtypeStruct((m, n), out_dtype),
        scratch_shapes=[] if nk == 1 else [pltpu.VMEM((tm, tn), F32)],
        compiler_params=_cparams(("parallel", "parallel", "arbitrary")),
    )(a, b, *follows)


ROWS = 256


def _rms_fwd(x, g):
    s, d = x.shape

    def body(x_ref, g_ref, h_ref):
        xv = x_ref[...]
        r = lax.rsqrt(jnp.mean(xv * xv, axis=-1, keepdims=True) + EPS)
        h_ref[...] = (xv * r * g_ref[...]).astype(BF16)

    return pl.pallas_call(
        body, name="rms_fwd", grid=(s // ROWS,),
        in_specs=[pl.BlockSpec((ROWS, d), lambda i: (i, 0)), pl.BlockSpec((1, d), lambda i: (0, 0))],
        out_specs=pl.BlockSpec((ROWS, d), lambda i: (i, 0)),
        out_shape=jax.ShapeDtypeStruct((s, d), BF16),
        compiler_params=_cparams(("parallel",)),
    )(x, g)


def _post_fwd(x, y, g):
    s, d = x.shape

    def body(x_ref, y_ref, g_ref, o_ref):
        yv = y_ref[...]
        r = lax.rsqrt(jnp.mean(yv * yv, axis=-1, keepdims=True) + EPS)
        o_ref[...] = x_ref[...] + yv * r * g_ref[...]

    row = pl.BlockSpec((ROWS, d), lambda i: (i, 0))
    return pl.pallas_call(
        body, name="post_fwd", grid=(s // ROWS,),
        in_specs=[row, row, pl.BlockSpec((1, d), lambda i: (0, 0))],
        out_specs=row,
        out_shape=jax.ShapeDtypeStruct((s, d), F32),
        compiler_params=_cparams(("parallel",)),
    )(x, y, g)


def _loss_head(out, tgt):
    s, d = out.shape

    def body(o_ref, t_ref, dout_ref, sum_ref):
        @pl.when(pl.program_id(0) == 0)
        def _():
            sum_ref[...] = jnp.zeros_like(sum_ref)

        e = o_ref[...] - t_ref[...]
        dout_ref[...] = e * (1.0 / d)
        sum_ref[...] += jnp.sum(jnp.sum(e * e, axis=1, keepdims=True), axis=0, keepdims=True)

    row = pl.BlockSpec((ROWS, d), lambda i: (i, 0))
    return pl.pallas_call(
        body, name="loss_head", grid=(s // ROWS,),
        in_specs=[row, row],
        out_specs=[row, pl.BlockSpec((1, 1), lambda i: (0, 0))],
        out_shape=[jax.ShapeDtypeStruct((s, d), F32), jax.ShapeDtypeStruct((1, 1), F32)],
        compiler_params=_cparams(("arbitrary",)),
    )(out, tgt)


def _post_bwd(dout, y, g):
    s, d = y.shape

    def body(do_ref, y_ref, g_ref, dy_ref, dg_ref):
        @pl.when(pl.program_id(0) == 0)
        def _():
            dg_ref[...] = jnp.zeros_like(dg_ref)

        yv = y_ref[...]
        dv = do_ref[...]
        r = lax.rsqrt(jnp.mean(yv * yv, axis=-1, keepdims=True) + EPS)
        dg_ref[...] += jnp.sum(dv * yv * r, axis=0, keepdims=True)
        w = dv * g_ref[...]
        dy = r * (w - yv * (r * r) * jnp.mean(w * yv, axis=-1, keepdims=True))
        dy_ref[...] = dy.astype(BF16)

    row = pl.BlockSpec((ROWS, d), lambda i: (i, 0))
    vec = pl.BlockSpec((1, d), lambda i: (0, 0))
    return pl.pallas_call(
        body, name="post_bwd", grid=(s // ROWS,),
        in_specs=[row, row, vec],
        out_specs=[row, vec],
        out_shape=[jax.ShapeDtypeStruct((s, d), BF16), jax.ShapeDtypeStruct((1, d), F32)],
        compiler_params=_cparams(("arbitrary",)),
    )(dout, y, g)


def _pre_bwd(dh, x, g, dout):
    s, d = x.shape

    def body(dh_ref, x_ref, g_ref, do_ref, dx_ref, dg_ref):
        @pl.when(pl.program_id(0) == 0)
        def _():
            dg_ref[...] = jnp.zeros_like(dg_ref)

        xv = x_ref[...]
        dv = dh_ref[...]
        r = lax.rsqrt(jnp.mean(xv * xv, axis=-1, keepdims=True) + EPS)
        dg_ref[...] += jnp.sum(dv * xv * r, axis=0, keepdims=True)
        w = dv * g_ref[...]
        dx_ref[...] = do_ref[...] + r * (w - xv * (r * r) * jnp.mean(w * xv, axis=-1, keepdims=True))

    row = pl.BlockSpec((ROWS, d), lambda i: (i, 0))
    vec = pl.BlockSpec((1, d), lambda i: (0, 0))
    return pl.pallas_call(
        body, name="pre_bwd", grid=(s // ROWS,),
        in_specs=[row, row, vec, row],
        out_specs=[row, vec],
        out_shape=[jax.ShapeDtypeStruct((s, d), F32), jax.ShapeDtypeStruct((1, d), F32)],
        compiler_params=_cparams(("arbitrary",)),
    )(dh, x, g, dout)


GLA_STEP = 4
GLA_ROWS = GLA_STEP * CHUNK
GLA_CHUNKS = [slice(c * CHUNK, (c + 1) * CHUNK) for c in range(GLA_STEP)]


def _chunk_triangles():
    ri = lax.broadcasted_iota(jnp.int32, (GLA_ROWS, GLA_ROWS), 0)
    ci = lax.broadcasted_iota(jnp.int32, (GLA_ROWS, GLA_ROWS), 1)
    same = (ri // CHUNK) == (ci // CHUNK)
    return (jnp.where(same & (ri >= ci), 1.0, 0.0).astype(BF16), jnp.where(same & (ci >= ri), 1.0, 0.0).astype(BF16))


def _per_chunk(fn, like):
    row = lax.broadcasted_iota(jnp.int32, like.shape, 0)
    return [fn((row >= c * CHUNK) & (row < (c + 1) * CHUNK)) for c in range(GLA_STEP)]


def _spread(per_chunk, like):
    row = lax.broadcasted_iota(jnp.int32, like.shape, 0)
    out = per_chunk[-1]
    for c in reversed(range(GLA_STEP - 1)):
        out = jnp.where(row < (c + 1) * CHUNK, per_chunk[c], out)
    return out


def _gla_gate(ga_b, wa_b, b_ref, tri):
    pre = _dot(ga_b, wa_b) + b_ref[...]
    la = _log_sigmoid(pre) * (1.0 / GLA_TAU)
    cum = _dot01(tri, la)
    row = lax.broadcasted_iota(jnp.int32, cum.shape, 0)
    cends = [jnp.sum(jnp.where(row == (c + 1) * CHUNK - 1, cum, 0.0), axis=0, keepdims=True)
             for c in range(GLA_STEP)]
    return pre, cum, cends


def _heads(width):
    return [slice(h * width, (h + 1) * width) for h in range(GLA_HEADS)]


def _z_specs_gla(rev=None):
    idx = (lambda n: n) if rev is None else rev
    return [
        pl.BlockSpec((GLA_ROWS, GLA_KW), lambda n: (idx(n), 0)),
        pl.BlockSpec((GLA_ROWS, GLA_KW), lambda n: (idx(n), 1)),
        pl.BlockSpec((GLA_ROWS, D_GLA), lambda n: (idx(n), 1)),
        pl.BlockSpec((GLA_ROWS, D_GLA), lambda n: (idx(n), 2)),
        pl.BlockSpec((GLA_ROWS, LANE), lambda n: (idx(n), OFF_GA // LANE)),
    ]


def _gla_fwd(z, wa_pad, b_alpha, g_gla):
    s = z.shape[0]
    nchunk = s // CHUNK

    def body(q_ref, k_ref, v_ref, gg_ref, ga_ref, wa_ref, b_ref, g_ref, y_ref, o_ref, st_ref, state):
        @pl.when(pl.program_id(0) == 0)
        def _():
            state[...] = jnp.zeros_like(state)

        ga_b = ga_ref[...].astype(BF16)
        tri, _ = _chunk_triangles()
        nh = range(GLA_HEADS)
        keys, vals = _heads(GLA_DK), _heads(GLA_DV)
        _, cum, cends = _gla_gate(ga_b, wa_ref[...].astype(BF16), b_ref, tri)
        kd_b = (k_ref[...] * jnp.exp(_spread(cends, cum) - cum)).astype(BF16)
        qs = (q_ref[...] * GLA_SCALE).astype(BF16)
        v_b = v_ref[...].astype(BF16)
        uts = [[_dot_tn(v_b[rs, vals[h]], kd_b[rs, keys[h]]) for h in nh] for rs in GLA_CHUNKS]
        sts, prev = [], [state[h] for h in nh]
        for c in range(GLA_STEP):
            a = jnp.exp(cends[c])
            prev = [prev[h] * a[:, keys[h]] + uts[c][h] for h in nh]
            sts.append(prev)
        for h in nh:
            state[h] = prev[h]
            for c in range(GLA_STEP):
                st_ref[c, h] = sts[c][h]
        outs = [[_dot_nt(qs[rs, keys[h]], sts[c][h].astype(BF16)) for h in nh] for c, rs in enumerate(GLA_CHUNKS)]
        for h in nh:
            o, vs = jnp.concatenate([outs[c][h] for c in range(GLA_STEP)], axis=0), vals[h]
            o_ref[:, vs] = o
            r = lax.rsqrt(jnp.mean(o * o, axis=-1, keepdims=True) + EPS)
            gg = gg_ref[:, vs]
            y_ref[:, vs] = (o * r * g_ref[:, vs] * (gg * _sigmoid(gg))).astype(BF16)

    full = lambda shape: pl.BlockSpec(shape, lambda n: tuple(0 for _ in shape))
    wide = pl.BlockSpec((GLA_ROWS, D_GLA), lambda n: (n, 0))
    return pl.pallas_call(
        body, name="gla_fwd", grid=(nchunk // GLA_STEP,),
        in_specs=_z_specs_gla() + [full((LANE, GLA_KW)), full((1, GLA_KW)), full((1, D_GLA))],
        out_specs=[wide, wide, pl.BlockSpec((GLA_STEP, GLA_HEADS, GLA_DV, GLA_DK), lambda n: (n, 0, 0, 0))],
        out_shape=[jax.ShapeDtypeStruct((s, D_GLA), BF16), jax.ShapeDtypeStruct((s, D_GLA), F32),
                   jax.ShapeDtypeStruct((nchunk, GLA_HEADS, GLA_DV, GLA_DK), F32)],
        scratch_shapes=[pltpu.VMEM((GLA_HEADS, GLA_DV, GLA_DK), F32)],
        compiler_params=_cparams(("arbitrary",)),
    )(z, z, z, z, z, wa_pad, b_alpha, g_gla)


def _gla_bwd(dyc, o_gla, z, wa_pad, b_alpha, g_gla, states):
    s = z.shape[0]
    nsteps = s // GLA_ROWS
    rev = lambda n: nsteps - 1 - n

    def body(dy_ref, o_ref, q_ref, k_ref, v_ref, gg_ref, ga_ref, wa_ref, b_ref, g_ref, st_ref, stp_ref,
             dq_ref, dk_ref, dv_ref, dgg_ref, dga_ref, dwa_ref, db_ref, dg_ref, carry):
        step = pl.program_id(0)

        @pl.when(step == 0)
        def _():
            carry[...] = jnp.zeros_like(carry)
            dwa_ref[...] = jnp.zeros_like(dwa_ref)
            db_ref[...] = jnp.zeros_like(db_ref)
            dg_ref[...] = jnp.zeros_like(dg_ref)

        has_prev = (step < nsteps - 1).astype(F32)
        ga_b = ga_ref[...].astype(BF16)
        tri, tri_up = _chunk_triangles()
        nh, nc = range(GLA_HEADS), range(GLA_STEP)
        keys, vals = _heads(GLA_DK), _heads(GLA_DV)
        wa_b = wa_ref[...].astype(BF16)
        pre, cum, cends = _gla_gate(ga_b, wa_b, b_ref, tri)
        e = jnp.exp(_spread(cends, cum) - cum)
        a = [jnp.exp(cends[c]) for c in nc]
        kf = k_ref[...]
        kd_b = (kf * e).astype(BF16)
        v_b = v_ref[...].astype(BF16)
        qs = (q_ref[...] * GLA_SCALE).astype(BF16)
        do_b = []
        for h in nh:
            vs = vals[h]
            o = o_ref[:, vs]
            gg = gg_ref[:, vs]
            g = g_ref[:, vs]
            dy = dy_ref[:, vs]
            r = lax.rsqrt(jnp.mean(o * o, axis=-1, keepdims=True) + EPS)
            sg = _sigmoid(gg)
            dogn = dy * (gg * sg)
            dgg_ref[:, vs] = (dy * (o * r * g) * (sg * (1.0 + gg * (1.0 - sg)))).astype(BF16)
            dg_ref[:, vs] += jnp.sum(dogn * o * r, axis=0, keepdims=True)
            w = dogn * g
            do_b.append((r * (w - o * (r * r) * jnp.mean(w * o, axis=-1, keepdims=True))).astype(BF16))
        dqs = [jnp.concatenate([_dot(do_b[h][rs], st_ref[c, h].astype(BF16)) for c, rs in enumerate(GLA_CHUNKS)],
                               axis=0) for h in nh]
        dq_ref[...] = (jnp.concatenate(dqs, axis=1) * GLA_SCALE).astype(BF16)
        own = [[_dot_tn(do_b[h][rs], qs[rs, keys[h]]) for h in nh] for rs in GLA_CHUNKS]
        gts, later = [None] * GLA_STEP, [carry[h] for h in nh]
        for c in reversed(nc):
            gts[c] = [own[c][h] + later[h] for h in nh]
            later = [gts[c][h] * a[c][:, keys[h]] for h in nh]
        for h in nh:
            carry[h] = later[h]
        gt_b = [[gts[c][h].astype(BF16) for h in nh] for c in nc]
        dkd = jnp.concatenate([jnp.concatenate([_dot(v_b[rs, vals[h]], gt_b[c][h]) for h in nh], axis=1)
                               for c, rs in enumerate(GLA_CHUNKS)], axis=0)
        dvs = [[_dot_nt(kd_b[rs, keys[h]], gt_b[c][h]) for h in nh] for c, rs in enumerate(GLA_CHUNKS)]
        before = lambda c, h: st_ref[c - 1, h] if c > 0 else stp_ref[0, h] * has_prev
        da = [jnp.concatenate([jnp.sum(gts[c][h] * before(c, h), axis=0, keepdims=True) for h in nh], axis=1)
              for c in nc]
        for h in nh:
            dv_ref[:, vals[h]] = jnp.concatenate([dvs[c][h] for c in nc], axis=0).astype(BF16)
        dk_ref[...] = (dkd * e).astype(BF16)
        dd = dkd * kf * e
        dsum = _per_chunk(lambda mine: jnp.sum(jnp.where(mine, dd, 0.0), axis=0, keepdims=True), dd)
        dcend = _spread([dsum[c] + da[c] * a[c] for c in nc], dd)
        dla = dcend - _dot01(tri_up, dd)
        dpre = dla * (1.0 / GLA_TAU) * (1.0 - _sigmoid(pre))
        dpre_b = dpre.astype(BF16)
        dga_ref[...] = _dot_nt(dpre_b, wa_b).astype(BF16)
        dwa_ref[...] += _dot_tn(ga_b, dpre_b)
        db_ref[...] += jnp.sum(dpre, axis=0, keepdims=True)

    full = lambda shape: pl.BlockSpec(shape, lambda n: tuple(0 for _ in shape))
    wide = pl.BlockSpec((GLA_ROWS, D_GLA), lambda n: (rev(n), 0))
    keyw = pl.BlockSpec((GLA_ROWS, GLA_KW), lambda n: (rev(n), 0))
    st_spec = pl.BlockSpec((GLA_STEP, GLA_HEADS, GLA_DV, GLA_DK), lambda n: (rev(n), 0, 0, 0))
    stp_spec = pl.BlockSpec((1, GLA_HEADS, GLA_DV, GLA_DK),
                            lambda n: (jnp.maximum(GLA_STEP * rev(n) - 1, 0), 0, 0, 0))
    return pl.pallas_call(
        body, name="gla_bwd", grid=(nsteps,),
        in_specs=[wide, wide] + _z_specs_gla(rev)
        + [full((LANE, GLA_KW)), full((1, GLA_KW)), full((1, D_GLA)), st_spec, stp_spec],
        out_specs=[keyw, keyw, wide, wide, pl.BlockSpec((GLA_ROWS, LANE), lambda n: (rev(n), 0)),
                   full((LANE, GLA_KW)), full((1, GLA_KW)), full((1, D_GLA))],
        out_shape=[jax.ShapeDtypeStruct((s, GLA_KW), BF16), jax.ShapeDtypeStruct((s, GLA_KW), BF16),
                   jax.ShapeDtypeStruct((s, D_GLA), BF16), jax.ShapeDtypeStruct((s, D_GLA), BF16),
                   jax.ShapeDtypeStruct((s, LANE), BF16),
                   jax.ShapeDtypeStruct((LANE, GLA_KW), F32), jax.ShapeDtypeStruct((1, GLA_KW), F32),
                   jax.ShapeDtypeStruct((1, D_GLA), F32)],
        scratch_shapes=[pltpu.VMEM((GLA_HEADS, GLA_DV, GLA_DK), F32)],
        compiler_params=_cparams(("arbitrary",)),
    )(dyc, o_gla, z, z, z, z, z, wa_pad, b_alpha, g_gla, states, states)


def _build_bias_table(rb_row, et_ref):
    far = jnp.broadcast_to(rb_row[:, 2 * REL_CLIP:2 * REL_CLIP + 1], (1, LANE))
    near_hi = rb_row[:, REL_CLIP:2 * REL_CLIP]
    near_lo = rb_row[:, 0:REL_CLIP]
    past = jnp.broadcast_to(rb_row[:, 0:1], (1, LANE))
    seg = [far, far, far, far, near_hi, near_lo] + [past] * (ET_ROWS // LANE - 5)
    ri = lax.broadcasted_iota(jnp.int32, (LANE, LANE), 0)
    ci = lax.broadcasted_iota(jnp.int32, (LANE, LANE), 1)
    for kb in range(ET_ROWS // LANE):
        wmat = jnp.where(ri + ci < LANE, seg[kb], seg[kb + 1])
        blk = pltpu.roll(wmat, 0, 1, stride=1, stride_axis=0)
        lag = LEFT_CHUNKS + ci // CHUNK - (2 * kb + ri // CHUNK)
        et_ref[kb * LANE:(kb + 1) * LANE, :] = jnp.where((lag >= 0) & (lag <= LEFT_CHUNKS), blk, NEG)


def _reduce_bias_table(det_ref):
    lane = lax.broadcasted_iota(jnp.int32, (1, LANE), 1)
    ri = lax.broadcasted_iota(jnp.int32, (LANE, LANE), 0)
    ci = lax.broadcasted_iota(jnp.int32, (LANE, LANE), 1)
    flip = jnp.where(ri + ci == LANE - 1, 1.0, 0.0).astype(BF16)
    segs = jnp.zeros((8, LANE), F32)
    seg_row = lax.broadcasted_iota(jnp.int32, (8, LANE), 0)
    prev_minus = jnp.zeros((1, LANE), F32)
    for kb in range(6):
        rolled = pltpu.roll(_dot01(det_ref[kb * LANE:(kb + 1) * LANE, :], flip, left=False), 0, 1,
                            stride=1, stride_axis=0)
        plus = jnp.sum(jnp.where(ci >= ri, rolled, 0.0), axis=0, keepdims=True)
        minus = jnp.sum(jnp.where(ci < ri, rolled, 0.0), axis=0, keepdims=True)
        segs = segs + jnp.where(seg_row == kb, plus + prev_minus, 0.0)
        prev_minus = minus
    segs = _dot01(segs, flip, left=False)
    pick = lambda kb: jnp.sum(jnp.where(seg_row == kb, segs, 0.0), axis=0, keepdims=True)
    far = jnp.sum(pick(0) + pick(1) + pick(2) + pick(3), axis=1, keepdims=True)
    last = jnp.where(lane == 0, far, 0.0)
    return jnp.concatenate([pick(5), pick(4), last], axis=1)


def _att_window(b):
    c0 = 2 * b
    kstart = pl.multiple_of(jnp.maximum(c0 - LEFT_CHUNKS, 0) * CHUNK, CHUNK)
    eoff = pl.multiple_of(jnp.maximum(LEFT_CHUNKS - c0, 0) * CHUNK, CHUNK)
    return kstart, eoff


def _att_probs(q_b, kw_b, et):
    st = _dot_nt(kw_b, q_b) * ATT_SCALE + et
    m = jnp.max(st, axis=0, keepdims=True)
    ex = jnp.exp(st - m)
    return ex * (1.0 / jnp.sum(ex, axis=0, keepdims=True))


def _att_fwd(z, rb_pad, g_att):
    s = z.shape[0]
    nblk = s // QB
    c_aq, c_ak, c_av, c_ag = [(OFF_AQ + i * D_ATT) // ATT_HD for i in range(4)]

    def body(q_ref, k_ref, v_ref, ag_ref, rb_ref, g_ref, y_ref, o_ref, et_ref, kb_ref, vb_ref):
        h = pl.program_id(0)
        b = pl.program_id(1)

        @pl.when(b == 0)
        def _():
            _build_bias_table(rb_ref[pl.ds(h, 1), :], et_ref)
            kb_ref[...] = k_ref[...].astype(BF16)
            vb_ref[...] = v_ref[...].astype(BF16)

        for j in range(ATT_UNROLL):
            rs = slice(j * QB, (j + 1) * QB)
            kstart, eoff = _att_window(b * ATT_UNROLL + j)
            q_b = q_ref[rs, :].astype(BF16)
            kw_b = kb_ref[pl.ds(kstart, WIN), :]
            vw_b = vb_ref[pl.ds(kstart, WIN), :]
            pt = _att_probs(q_b, kw_b, et_ref[pl.ds(eoff, WIN), :])
            o = _dot_tn(pt.astype(BF16), vw_b)
            o_ref[rs, :] = o
            r = lax.rsqrt(jnp.mean(o * o, axis=-1, keepdims=True) + EPS)
            ag = ag_ref[rs, :]
            y_ref[rs, :] = (o * r * g_ref[...] * (ag * _sigmoid(ag))).astype(BF16)

    blk = lambda col: pl.BlockSpec((ATT_UNROLL * QB, ATT_HD), lambda h, b: (b, col + h))
    seq = lambda col: pl.BlockSpec((s, ATT_HD), lambda h, b: (0, col + h))
    out_blk = pl.BlockSpec((ATT_UNROLL * QB, ATT_HD), lambda h, b: (b, h))
    return pl.pallas_call(
        body, name="att_fwd", grid=(ATT_HEADS, nblk // ATT_UNROLL),
        in_specs=[blk(c_aq), seq(c_ak), seq(c_av), blk(c_ag),
                  pl.BlockSpec((ATT_HEADS, 3 * LANE), lambda h, b: (0, 0)),
                  pl.BlockSpec((1, ATT_HD), lambda h, b: (0, h))],
        out_specs=[out_blk, out_blk],
        out_shape=[jax.ShapeDtypeStruct((s, D_ATT), BF16), jax.ShapeDtypeStruct((s, D_ATT), F32)],
        scratch_shapes=[pltpu.VMEM((ET_ROWS, LANE), F32), pltpu.VMEM((s, ATT_HD), BF16),
                        pltpu.VMEM((s, ATT_HD), BF16)],
        compiler_params=_cparams(("arbitrary", "arbitrary")),
    )(z, z, z, z, rb_pad, g_att)


def _att_bwd(dyc, o_att, z, rb_pad, g_att):
    s = z.shape[0]
    nblk = s // QB
    c_aq, c_ak, c_av, c_ag = [(OFF_AQ + i * D_ATT) // ATT_HD for i in range(4)]
    c_dy = D_GLA // ATT_HD

    def body(dy_ref, o_ref, q_ref, k_ref, v_ref, ag_ref, rb_ref, g_ref,
             dq_ref, dk_ref, dv_ref, dag_ref, drb_ref, dg_ref, et_ref, det_ref, kb_ref, vb_ref):
        h = pl.program_id(0)
        b = pl.program_id(1)

        @pl.when(b == 0)
        def _():
            _build_bias_table(rb_ref[pl.ds(h, 1), :], et_ref)
            kb_ref[...] = k_ref[...].astype(BF16)
            vb_ref[...] = v_ref[...].astype(BF16)
            det_ref[...] = jnp.zeros_like(det_ref)
            dk_ref[...] = jnp.zeros_like(dk_ref)
            dv_ref[...] = jnp.zeros_like(dv_ref)
            dg_ref[...] = jnp.zeros_like(dg_ref)

        g = g_ref[...]
        dg = jnp.zeros((1, ATT_HD), F32)
        for j in range(ATT_UNROLL):
            rs = slice(j * QB, (j + 1) * QB)
            kstart, eoff = _att_window(b * ATT_UNROLL + j)
            q_b = q_ref[rs, :].astype(BF16)
            kw_b = kb_ref[pl.ds(kstart, WIN), :]
            vw_b = vb_ref[pl.ds(kstart, WIN), :]
            pt = _att_probs(q_b, kw_b, et_ref[pl.ds(eoff, WIN), :])
            o = o_ref[rs, :]
            ag = ag_ref[rs, :]
            dy = dy_ref[rs, :]
            r = lax.rsqrt(jnp.mean(o * o, axis=-1, keepdims=True) + EPS)
            sg = _sigmoid(ag)
            don = dy * (ag * sg)
            dag_ref[rs, :] = (dy * (o * r * g) * (sg * (1.0 + ag * (1.0 - sg)))).astype(BF16)
            dg = dg + jnp.sum(don * o * r, axis=0, keepdims=True)
            w = don * g
            do_b = (r * (w - o * (r * r) * jnp.mean(w * o, axis=-1, keepdims=True))).astype(BF16)
            pt_b = pt.astype(BF16)
            dpt = _dot_nt(vw_b, do_b)
            dst = pt * (dpt - jnp.sum(dpt * pt, axis=0, keepdims=True))
            det_ref[pl.ds(eoff, WIN), :] += dst
            ds_b = (dst * ATT_SCALE).astype(BF16)
            dq_ref[rs, :] = _dot_tn(ds_b, kw_b).astype(BF16)
            dk_ref[pl.ds(kstart, WIN), :] += _dot(ds_b, q_b)
            dv_ref[pl.ds(kstart, WIN), :] += _dot(pt_b, do_b)
        dg_ref[...] += dg

        @pl.when(b == nblk // ATT_UNROLL - 1)
        def _():
            drb_ref[0] = jnp.broadcast_to(_reduce_bias_table(det_ref), (8, 3 * LANE))

    blk = lambda col: pl.BlockSpec((ATT_UNROLL * QB, ATT_HD), lambda h, b: (b, col + h))
    seq = lambda col: pl.BlockSpec((s, ATT_HD), lambda h, b: (0, col + h))
    out_blk = pl.BlockSpec((ATT_UNROLL * QB, ATT_HD), lambda h, b: (b, h))
    out_seq = pl.BlockSpec((s, ATT_HD), lambda h, b: (0, h))
    return pl.pallas_call(
        body, name="att_bwd", grid=(ATT_HEADS, nblk // ATT_UNROLL),
        in_specs=[blk(c_dy), blk(0), blk(c_aq), seq(c_ak), seq(c_av), blk(c_ag),
                  pl.BlockSpec((ATT_HEADS, 3 * LANE), lambda h, b: (0, 0)),
                  pl.BlockSpec((1, ATT_HD), lambda h, b: (0, h))],
        out_specs=[out_blk, out_seq, out_seq, out_blk,
                   pl.BlockSpec((1, 8, 3 * LANE), lambda h, b: (h, 0, 0)),
                   pl.BlockSpec((1, ATT_HD), lambda h, b: (0, h))],
        out_shape=[jax.ShapeDtypeStruct((s, D_ATT), BF16), jax.ShapeDtypeStruct((s, D_ATT), F32),
                   jax.ShapeDtypeStruct((s, D_ATT), F32), jax.ShapeDtypeStruct((s, D_ATT), BF16),
                   jax.ShapeDtypeStruct((ATT_HEADS, 8, 3 * LANE), F32),
                   jax.ShapeDtypeStruct((1, D_ATT), F32)],
        scratch_shapes=[pltpu.VMEM((ET_ROWS, LANE), F32), pltpu.VMEM((ET_ROWS, LANE), F32),
                        pltpu.VMEM((s, ATT_HD), BF16), pltpu.VMEM((s, ATT_HD), BF16)],
        compiler_params=_cparams(("arbitrary", "arbitrary")),
    )(dyc, o_att, z, z, z, z, rb_pad, g_att)


ADAM_ROWS = 64
ADAM_COL_ROWS = 32


def _adam_math(w, g, m, v):
    m2 = ADAM_B1 * m + (1.0 - ADAM_B1) * g
    v2 = ADAM_B2 * v + (1.0 - ADAM_B2) * (g * g)
    m_hat = m2 / (1.0 - ADAM_B1 ** ADAM_STEP)
    v_hat = v2 / (1.0 - ADAM_B2 ** ADAM_STEP)
    delta = -ADAM_LR * (m_hat / (jnp.sqrt(v_hat) + ADAM_EPS) + ADAM_WD * w)
    return delta, m2, v2


def _adam_sharded(parts, first, w, m, v, name):
    nl, nr, nc = w.shape

    def body(*refs):
        p_refs = refs[:nl]
        w_ref, m_ref, v_ref, g_ref, d_ref, m2_ref, v2_ref = refs[nl:]
        for k in range(nl):
            @pl.when(pl.program_id(0) == k)
            def _(p_ref=p_refs[k]):
                g = p_ref[0].astype(F32)
                for dev in range(1, N_DEV):
                    g = g + p_ref[dev].astype(F32)
                delta, m2, v2 = _adam_math(w_ref[0], g, m_ref[0], v_ref[0])
                g_ref[0] = g
                d_ref[0] = delta
                m2_ref[0] = m2
                v2_ref[0] = v2

    def part_spec(k):
        return pl.BlockSpec((N_DEV, ADAM_ROWS, nc), lambda l, i: (0, first + jnp.where(l == k, i, 0), 0))

    blk = pl.BlockSpec((1, ADAM_ROWS, nc), lambda l, i: (l, i, 0))
    shp = jax.ShapeDtypeStruct(w.shape, F32)
    return pl.pallas_call(
        body, name=name, grid=(nl, pl.cdiv(nr, ADAM_ROWS)),
        in_specs=[part_spec(k) for k in range(nl)] + [blk, blk, blk],
        out_specs=[blk, blk, blk, blk],
        out_shape=[shp, shp, shp, shp],
        compiler_params=_cparams(("arbitrary", "arbitrary")),
    )(*parts, w, m, v)


def _adam_columns(parts, first, w, m, v):
    nc, nl, d = w.shape

    def body(*refs):
        p_refs = refs[:nl]
        w_ref, m_ref, v_ref, g_ref, d_ref, m2_ref, v2_ref = refs[nl:]
        for l in range(nl):
            g = p_refs[l][0].astype(F32)
            for dev in range(1, N_DEV):
                g = g + p_refs[l][dev].astype(F32)
            delta, m2, v2 = _adam_math(w_ref[:, l, :], g, m_ref[:, l, :], v_ref[:, l, :])
            g_ref[:, l, :] = g
            d_ref[:, l, :] = delta
            m2_ref[:, l, :] = m2
            v2_ref[:, l, :] = v2

    blk = pl.BlockSpec((ADAM_COL_ROWS, nl, d), lambda i: (i, 0, 0))
    part = pl.BlockSpec((N_DEV, ADAM_COL_ROWS, d), lambda i: (0, first + i, 0))
    shp = jax.ShapeDtypeStruct(w.shape, F32)
    return pl.pallas_call(
        body, name="adam_w_in", grid=(pl.cdiv(nc, ADAM_COL_ROWS),),
        in_specs=[part] * nl + [blk, blk, blk],
        out_specs=[blk, blk, blk, blk],
        out_shape=[shp, shp, shp, shp],
        compiler_params=_cparams(("parallel",)),
    )(*parts, w, m, v)


def _adam_small(w, g, m, v):
    def body(w_ref, g_ref, m_ref, v_ref, d_ref, m2_ref, v2_ref):
        delta, m2, v2 = _adam_math(w_ref[...], g_ref[...], m_ref[...], v_ref[...])
        d_ref[...] = delta
        m2_ref[...] = m2
        v2_ref[...] = v2

    shp = jax.ShapeDtypeStruct(w.shape, F32)
    return pl.pallas_call(body, name="adam_small", out_shape=[shp, shp, shp])(w, g, m, v)


def _position():
    return lax.axis_index("x"), lax.axis_index("y"), lax.axis_index("c")


def _slot(p):
    return 4 * p[0] + 2 * p[1] + p[2]


BF16_TILE_ROWS = 16


def _slab_rows(rows, cols):
    return -(-(rows + cols) // BF16_TILE_ROWS) * BF16_TILE_ROWS


RELAYOUT_COLS = 512
RELAYOUT_CHUNK = 64


def _shard_pieces(dev, rows, cols):
    moved = ((0, GA_ORIG, 0), (GA_ORIG, GA_ORIG + GLA_RANK, OFF_GA - GA_ORIG), (GA_ORIG + GLA_RANK, D_IN, -GLA_RANK))
    c0, c1 = dev * cols, (dev + 1) * cols
    return [(rows + max(c0, lo) - c0, max(c0, lo) + off, min(c1, hi) - max(c0, lo))
            for lo, hi, off in moved if max(c0, lo) < min(c1, hi)]


def _move_rows(src, src_row, dst, dst_row, n):
    assert src_row % 2 == 0 and dst_row % 2 == 0 and n % 2 == 0
    for r in range(0, n // 2, RELAYOUT_CHUNK):
        m = min(RELAYOUT_CHUNK, n // 2 - r)
        dst[dst_row // 2 + r:dst_row // 2 + r + m, :] = src[src_row // 2 + r:src_row // 2 + r + m, :]


def _aligned_weight(land, rows, cols):
    _, slab, d = land.shape
    ct = min(RELAYOUT_COLS, d)

    def body(land_ref, wt_ref, wo_ref):
        dev = pl.program_id(1)
        src = land_ref.bitcast(jnp.uint32)
        dst = wt_ref.bitcast(jnp.uint32)
        wo_ref[...] = land_ref[0:rows, :]

        @pl.when(dev == 0)
        def _():
            dst[D_IN // 2:D_ZP // 2, :] = jnp.zeros(((D_ZP - D_IN) // 2, ct), jnp.uint32)

        for k in range(N_DEV):
            @pl.when(dev == k)
            def _(k=k):
                for at, to, n in _shard_pieces(k, rows, cols):
                    _move_rows(src, at, dst, to, n)

    return pl.pallas_call(
        body, name="aligned_weight", grid=(d // ct, N_DEV),
        in_specs=[pl.BlockSpec((slab, ct), lambda c, dev: (dev, c))],
        out_specs=[pl.BlockSpec((D_ZP, ct), lambda c, dev: (0, c)),
                   pl.BlockSpec((rows, ct), lambda c, dev: (dev, c))],
        out_shape=[jax.ShapeDtypeStruct((D_ZP, d), land.dtype),
                   jax.ShapeDtypeStruct((N_DEV * rows, d), land.dtype)],
        compiler_params=_cparams(("parallel", "arbitrary")),
    )(land.reshape(N_DEV * slab, d))


def _partial_slabs(dwt, cols):
    d = dwt.shape[1]
    slab = _slab_rows(0, cols)
    ct = min(RELAYOUT_COLS, d)

    def body(dwt_ref, out_ref):
        dev = pl.program_id(1)
        src = dwt_ref.bitcast(jnp.uint32)
        dst = out_ref.bitcast(jnp.uint32)
        dst[cols // 2:slab // 2, :] = jnp.zeros(((slab - cols) // 2, ct), jnp.uint32)
        for k in range(N_DEV):
            @pl.when(dev == k)
            def _(k=k):
                for to, at, n in _shard_pieces(k, 0, cols):
                    _move_rows(src, at, dst, to, n)

    return pl.pallas_call(
        body, name="partial_slabs", grid=(d // ct, N_DEV),
        in_specs=[pl.BlockSpec((D_ZP, ct), lambda c, dev: (0, c))],
        out_specs=pl.BlockSpec((slab, ct), lambda c, dev: (dev, c)),
        out_shape=jax.ShapeDtypeStruct((N_DEV * slab, d), dwt.dtype),
        compiler_params=_cparams(("parallel", "arbitrary")),
    )(dwt).reshape(N_DEV, slab, d)


def _peer(pos, k):
    x, y, c = pos
    return (1 - x if k & 4 else x, 1 - y if k & 2 else y, 1 - c if k & 1 else c)


HBM_SPEC = pl.BlockSpec(memory_space=pltpu.HBM)
SEM_SPEC = pl.BlockSpec(memory_space=pltpu.SEMAPHORE)
GATHER_PEERS = (1, 4, 2, 6)
ALL_PEERS = (1, 2, 3, 4, 5, 6, 7)


def _hbm(a):
    return pltpu.with_memory_space_constraint(a, pltpu.HBM)


def _split_copies(src_ref, land_ref, send_sems, recv_sems, ks, per_peer, landed):
    me = _position()
    out = []
    for i, k in enumerate(ks):
        peer = _peer(me, k)
        src = src_ref.at[_slot(peer)] if per_peer else src_ref
        dst = land_ref.at[_slot(peer) if landed else _slot(me)]
        out.append(pltpu.make_async_remote_copy(
            src_ref=src, dst_ref=dst, send_sem=send_sems.at[i], recv_sem=recv_sems.at[i],
            device_id=peer, device_id_type=MESH))
    return out


def _exchange_start(src, after, ks, per_peer, name):
    slab = src.shape[1:] if per_peer else src.shape
    land_shape = (N_DEV,) + tuple(slab)
    n = len(ks)

    def body(src_ref, land_ref, after_ref, send_sems, recv_sems, src_thru, land_thru, token):
        for cp in _split_copies(src_ref, land_ref, send_sems, recv_sems, ks, per_peer, landed=False):
            cp.start()
        token[...] = jnp.zeros_like(token)

    return pl.pallas_call(
        body, name=name,
        out_shape=(pltpu.SemaphoreType.DMA((n,)), pltpu.SemaphoreType.DMA((n,)),
                   pltpu.HBM(src.shape, src.dtype), pltpu.HBM(land_shape, src.dtype),
                   jax.ShapeDtypeStruct((8, LANE), F32)),
        in_specs=(HBM_SPEC, HBM_SPEC, ANY),
        out_specs=(SEM_SPEC, SEM_SPEC, HBM_SPEC, HBM_SPEC, pl.BlockSpec(memory_space=pltpu.VMEM)),
        input_output_aliases={0: 2, 1: 3},
        compiler_params=pltpu.CompilerParams(has_side_effects=pltpu.SideEffectType.DATAFLOW_SIDE_EFFECTING),
    )(_hbm(src), _hbm(lax.empty(land_shape, src.dtype)), after)


def _exchange_wait(started, after, ks, per_peer, name):
    send_sems, recv_sems, src_thru, land_thru = started

    def body(src_ref, land_ref, send_sems, recv_sems, after_ref, src_dead, land_out):
        for cp in _split_copies(src_ref, land_ref, send_sems, recv_sems, ks, per_peer, landed=True):
            cp.wait_send()
            cp.wait_recv()

    return pl.pallas_call(
        body, name=name,
        out_shape=(pltpu.HBM(src_thru.shape, src_thru.dtype), pltpu.HBM(land_thru.shape, land_thru.dtype)),
        in_specs=(HBM_SPEC, HBM_SPEC, SEM_SPEC, SEM_SPEC, ANY), out_specs=(HBM_SPEC, HBM_SPEC),
        input_output_aliases={0: 0, 1: 1},
        compiler_params=pltpu.CompilerParams(has_side_effects=pltpu.SideEffectType.DATAFLOW_SIDE_EFFECTING),
    )(src_thru, land_thru, send_sems, recv_sems, after)


def _relay_copies(land_ref, send_sems, recv_sems, landed):
    me = _position()
    sibling = _peer(me, 1)
    out = []
    for i, k in enumerate(GATHER_PEERS[1:]):
        blk = land_ref.at[_slot(_peer(sibling if landed else me, k))]
        out.append(pltpu.make_async_remote_copy(
            src_ref=blk, dst_ref=blk, send_sem=send_sems.at[i], recv_sem=recv_sems.at[i],
            device_id=sibling, device_id_type=MESH))
    return out


def _relay_start(land, name):
    n = len(GATHER_PEERS) - 1

    def body(land_ref, send_sems, recv_sems, land_thru, token):
        for cp in _relay_copies(land_ref, send_sems, recv_sems, landed=False):
            cp.start()
        token[...] = jnp.zeros_like(token)

    return pl.pallas_call(
        body, name=name,
        out_shape=(pltpu.SemaphoreType.DMA((n,)), pltpu.SemaphoreType.DMA((n,)),
                   pltpu.HBM(land.shape, land.dtype), jax.ShapeDtypeStruct((8, LANE), F32)),
        in_specs=(HBM_SPEC,),
        out_specs=(SEM_SPEC, SEM_SPEC, HBM_SPEC, pl.BlockSpec(memory_space=pltpu.VMEM)),
        input_output_aliases={0: 2},
        compiler_params=pltpu.CompilerParams(has_side_effects=pltpu.SideEffectType.DATAFLOW_SIDE_EFFECTING),
    )(_hbm(land))


def _relay_wait(started, after, name):
    send_sems, recv_sems, land_thru = started

    def body(land_ref, send_sems, recv_sems, after_ref, land_out):
        for cp in _relay_copies(land_ref, send_sems, recv_sems, landed=True):
            cp.wait_send()
            cp.wait_recv()

    return pl.pallas_call(
        body, name=name,
        out_shape=pltpu.HBM(land_thru.shape, land_thru.dtype),
        in_specs=(HBM_SPEC, SEM_SPEC, SEM_SPEC, ANY), out_specs=HBM_SPEC,
        input_output_aliases={0: 0},
        compiler_params=pltpu.CompilerParams(has_side_effects=pltpu.SideEffectType.DATAFLOW_SIDE_EFFECTING),
    )(land_thru, send_sems, recv_sems, after)


def _exchange(arrs, name):
    n = len(arrs)

    def body(*refs):
        ins, outs = refs[:n], refs[n:2 * n]
        send_sems, recv_sems, local_sems = refs[2 * n:]
        me = _position()

        def copy(a, k):
            peer = _peer(me, k)
            return pltpu.make_async_remote_copy(
                src_ref=ins[a].at[_slot(peer)], dst_ref=outs[a].at[_slot(me)],
                send_sem=send_sems.at[a * 7 + k - 1], recv_sem=recv_sems.at[a * 7 + k - 1],
                device_id=peer, device_id_type=MESH)

        def landed(a, k):
            peer = _peer(me, k)
            return pltpu.make_async_remote_copy(
                src_ref=ins[a].at[_slot(peer)], dst_ref=outs[a].at[_slot(peer)],
                send_sem=send_sems.at[a * 7 + k - 1], recv_sem=recv_sems.at[a * 7 + k - 1],
                device_id=peer, device_id_type=MESH)

        mine = [pltpu.make_async_copy(ins[a].at[_slot(me)], outs[a].at[_slot(me)], local_sems.at[a])
                for a in range(n)]
        for cp in mine:
            cp.start()
        sent = [copy(a, k) for k in range(1, N_DEV) for a in range(n)]
        for cp in sent:
            cp.start()
        for k in range(1, N_DEV):
            for a in range(n):
                landed(a, k).wait_recv()
        for cp in sent:
            cp.wait_send()
        for cp in mine:
            cp.wait()

    return pl.pallas_call(
        body, name=name,
        in_specs=[ANY] * n, out_specs=[ANY] * n,
        out_shape=[jax.ShapeDtypeStruct(a.shape, a.dtype) for a in arrs],
        scratch_shapes=[pltpu.SemaphoreType.DMA((7 * n,)), pltpu.SemaphoreType.DMA((7 * n,)),
                        pltpu.SemaphoreType.DMA((n,))],
    )(*arrs)


def _sum_slots(parts):
    def body(p_ref, o_ref):
        acc = p_ref[0]
        for dev in range(1, N_DEV):
            acc = acc + p_ref[dev]
        o_ref[...] = acc

    return pl.pallas_call(body, name="sum_slots",
                          out_shape=jax.ShapeDtypeStruct(parts.shape[1:], F32))(parts)


PACK_ROWS = 8


def _packed_rows(size):
    return -(-size // (PACK_ROWS * LANE)) * PACK_ROWS


def _pack(arrs):
    def rows(a):
        flat = a.reshape(-1)
        return jnp.pad(flat, (0, _packed_rows(flat.shape[0]) * LANE - flat.shape[0])).reshape(-1, LANE)

    return jnp.concatenate([rows(a) for a in arrs], axis=0)


def _unpack(packed, shapes):
    out, at = [], 0
    for shp in shapes:
        size = 1
        for dim in shp:
            size *= dim
        nrows = _packed_rows(size)
        out.append(packed[at:at + nrows].reshape(-1)[:size].reshape(shp))
        at += nrows
    return out


def _layer_fwd(x, wt, wo, g_pre, g_post, wa_pad, b_alpha, g_gla, g_att, rb_pad, midway=None):
    h = _rms_fwd(x, g_pre)
    z = _matmul(h, wt, "nt", F32, *TILES["in_proj"], "in_proj", n_outer=True)
    y_gla, o_gla, states = _gla_fwd(z, wa_pad, b_alpha, g_gla)
    if midway is not None:
        g_att = g_att + midway(y_gla)[:1, :1]
    y_att, o_att = _att_fwd(z, rb_pad, g_att)
    ycat = jnp.concatenate([y_gla, y_att], axis=1)
    y = _matmul(ycat, wo, "nn", F32, *TILES["out_proj"], "out_proj", n_outer=True)
    out = _post_fwd(x, y, g_post)
    return out, (x, h, z, o_gla, states, o_att, ycat, y)


def _layer_bwd(dout, saved, wt, wo, g_pre, g_post, wa_pad, b_alpha, g_gla, g_att, rb_pad, on_dwo, on_dwt):
    x, h, z, o_gla, states, o_att, ycat, y = saved
    dy, dg_post = _post_bwd(dout, y, g_post)
    dwo = _matmul(ycat, dy, "tn", BF16, *TILES["out_proj_dw"], "out_proj_dw")
    token = on_dwo(dwo)
    dycat = _matmul(dy, wo, "nt", F32, *TILES["out_proj_dx"], "out_proj_dx", n_outer=True, after=token)
    dq, dk, dv, dgg, dga, dwa, db, dg_gla = _gla_bwd(dycat, o_gla, z, wa_pad, b_alpha, g_gla, states)
    daq, dak, dav, dag, drb, dg_att = _att_bwd(dycat, o_att, z, rb_pad, g_att)
    dz = jnp.concatenate([dq, dk, dv, dgg, daq, dak.astype(BF16), dav.astype(BF16), dag, dga], axis=1)
    dwt = _matmul(dz, h, "tn", BF16, *TILES["in_proj_dw"], "in_proj_dw")
    token = on_dwt(dwt)
    dh = _matmul(dz, wt, "nn", F32, *TILES["in_proj_dx"], "in_proj_dx", n_outer=True, after=token)
    dx, dg_pre = _pre_bwd(dh, x, g_pre, dout)
    small = (dg_pre[0], dg_post[0], dwa[:GLA_RANK], db[0], dg_gla[0], dg_att[0], drb[:, 0, :N_REL])
    return dx, small


def kernel(x, w_in, w_out, g_pre, g_post, w_alpha, b_alpha, g_gla, g_att, rel_bias, loss_target, m_w_in, m_w_out, m_g_pre, m_g_post, m_w_alpha, m_b_alpha, m_g_gla, m_g_att, m_rel_bias, v_w_in, v_w_out, v_g_pre, v_g_post, v_w_alpha, v_b_alpha, v_g_gla, v_g_att, v_rel_bias):
    nl, d, cols = w_in.shape
    rows = w_out.shape[1]
    s = x.shape[1]
    x0 = x.reshape(s, d)
    tgt = loss_target.reshape(s, d)

    cols_first = lambda a: jnp.transpose(a, (2, 0, 1))
    w_c = cols_first(w_in)
    slab = _slab_rows(rows, cols)
    is_out = lax.broadcasted_iota(jnp.int32, (slab, d), 0) < rows

    def shard(l, zero=0.0):
        top = jnp.pad((w_out[l] + zero).astype(BF16), ((0, slab - rows), (0, 0)))
        rest = jnp.pad((w_c[:, l] + zero).astype(BF16), ((rows, slab - rows - cols), (0, 0)))
        return jnp.where(is_out, top, rest)

    first_fetch = _exchange_start(shard(0), x, GATHER_PEERS, False, "gather_start_0")
    began = first_fetch[4][0, 0]
    shards = [None] + [shard(l, began) for l in range(1, nl)]
    alpha = _pack([w_alpha]) + began
    wa_g = _exchange([jnp.broadcast_to(alpha[None], (N_DEV,) + alpha.shape)], "gather_alpha")[0]
    wa_cols = w_alpha.shape[2]
    wa_full = wa_g.reshape(N_DEV, -1)[:, :nl * GLA_RANK * wa_cols].reshape(N_DEV, nl, GLA_RANK, wa_cols)
    wa_full = jnp.transpose(wa_full, (1, 2, 0, 3)).reshape(nl, GLA_RANK, GLA_KW)
    wa_pad = jnp.pad(wa_full, ((0, 0), (0, LANE - GLA_RANK), (0, 0)))
    rb_pad = jnp.pad(rel_bias, ((0, 0), (0, 0), (0, 3 * LANE - N_REL)))

    def layer_args(l, follows_pre=None, follows_post=None):
        gp = g_pre[l:l + 1] if follows_pre is None else g_pre[l:l + 1] + follows_pre[:1, :1]
        gq = g_post[l:l + 1] if follows_post is None else g_post[l:l + 1] + follows_post[:1, :1]
        return (wts[l], wos[l], gp, gq, wa_pad[l], b_alpha[l:l + 1], g_gla[l:l + 1], g_att[l:l + 1], rb_pad[l])

    my = _slot(_position())

    def fetch(l, after):
        return _exchange_start(shards[l], after, GATHER_PEERS, False, f"gather_start_{l}")

    def relay(l, first_hop, after):
        own[l], land = _exchange_wait(first_hop[:4], after, GATHER_PEERS, False, f"gather_wait_{l}")
        return _relay_start(land, f"relay_start_{l}")

    def midway(l, y):
        flight["relay"] = relay(l + 1, flight["fetch"], y)
        if l + 2 >= nl:
            return flight["relay"][3]
        flight["fetch"] = fetch(l + 2, flight["relay"][2])
        return flight["fetch"][4]

    act, saved, wts, wos, flight, own = x0, [], [], [], {}, [None] * nl
    prepared = (wa_pad[0, :1, :1] + sum(sh[:1, :1].astype(F32) for sh in shards[1:]))
    flight["relay"] = relay(0, first_fetch, prepared)
    if nl > 1:
        flight["fetch"] = fetch(1, flight["relay"][2])
    for l in range(nl):
        land = _relay_wait(flight["relay"][:3], act, f"relay_wait_{l}")
        land = lax.dynamic_update_slice_in_dim(land, own[l][None], my, 0)
        wt_l, wo_l = _aligned_weight(land, rows, cols)
        wts.append(wt_l)
        wos.append(wo_l)
        act, sv = _layer_fwd(act, *layer_args(l, follows_pre=first_fetch[4] if l == 0 else None),
                             midway=functools.partial(midway, l) if l + 1 < nl else None)
        saved.append(sv)
    dout, sq = _loss_head(act, tgt)
    loss = lax.psum(sq[0, 0] * (0.5 / d), ("x", "y", "c"))

    smalls, pending_out, pending_in = [None] * nl, [None] * nl, [None] * nl

    def send_out(l, dwo):
        pending_out[l] = _exchange_start(dwo.reshape(N_DEV, rows, d), dwo, ALL_PEERS, True, f"scatter_out_start_{l}")
        return pending_out[l][4]

    def send_in(l, dwt):
        pending_in[l] = _exchange_start(_partial_slabs(dwt, cols), dwt, ALL_PEERS, True, f"scatter_in_start_{l}")
        return pending_in[l][4]

    for l in reversed(range(nl)):
        dout, smalls[l] = _layer_bwd(dout, saved[l], *layer_args(l), on_dwo=functools.partial(send_out, l),
                                     on_dwt=functools.partial(send_in, l))
    grad_x = dout.reshape(x.shape)

    names = 7
    small_stacked = [jnp.stack([smalls[l][i] for l in range(nl)]) for i in range(names)]
    shapes = [a.shape for a in small_stacked]
    packed = _pack(small_stacked)
    gathered = _exchange([jnp.broadcast_to(packed[None], (N_DEV,) + packed.shape)], "gather_small_grads")[0]
    g_pre_g, g_post_g, wa_g_full, b_g, gla_g, att_g, rb_g = _unpack(_sum_slots(gathered), shapes)
    wa_g_mine = lax.dynamic_slice_in_dim(wa_g_full, my * wa_cols, wa_cols, axis=2)
    grads = [g_pre_g, g_post_g, wa_g_mine, b_g, gla_g, att_g, rb_g]
    ws = [g_pre, g_post, w_alpha, b_alpha, g_gla, g_att, rel_bias]
    ms = [m_g_pre, m_g_post, m_w_alpha, m_b_alpha, m_g_gla, m_g_att, m_rel_bias]
    vs = [v_g_pre, v_g_post, v_w_alpha, v_b_alpha, v_g_gla, v_g_att, v_rel_bias]
    shapes2 = [a.shape for a in ws]
    d_p, m2_p, v2_p = _adam_small(_pack(ws), _pack(grads), _pack(ms), _pack(vs))
    d_s, m2_s, v2_s = _unpack(d_p, shapes2), _unpack(m2_p, shapes2), _unpack(v2_p, shapes2)

    def landed(started, name):
        partial, land = _exchange_wait(started[:4], d_p, ALL_PEERS, True, name)
        return lax.dynamic_update_slice_in_dim(land, lax.dynamic_slice_in_dim(partial, my, 1, 0), my, 0)

    parts_out = [landed(pending_out[l], f"scatter_out_wait_{l}") for l in range(nl)]
    parts_in = [landed(pending_in[l], f"scatter_in_wait_{l}") for l in range(nl)]
    g_w_in, d_w_in, m2_w_in, v2_w_in = [
        jnp.transpose(a, (1, 2, 0))
        for a in _adam_columns(parts_in, 0, w_c, cols_first(m_w_in), cols_first(v_w_in))]
    g_w_out, d_w_out, m2_w_out, v2_w_out = _adam_sharded(parts_out, 0, w_out, m_w_out, v_w_out, "adam_w_out")

    def ordered(big_in, big_out, small):
        return [big_in, big_out] + list(small)

    return (loss, grad_x,
            *ordered(g_w_in, g_w_out, grads),
            *ordered(d_w_in, d_w_out, d_s),
            *ordered(m2_w_in, m2_w_out, m2_s),
            *ordered(v2_w_in, v2_w_out, v2_s))
```

```python
import functools

import jax
import jax.numpy as jnp
from jax import lax
from jax.experimental import pallas as pl
from jax.experimental.pallas import tpu as pltpu

F32 = jnp.float32
BF16 = jnp.bfloat16
MESH = pl.DeviceIdType.MESH
ANY = pl.BlockSpec(memory_space=pl.ANY)

CHUNK = 64
GLA_HEADS = 4
GLA_DK = 128
GLA_DV = 256
GLA_KW = GLA_HEADS * GLA_DK
D_GLA = GLA_HEADS * GLA_DV
GLA_RANK = 16
GLA_TAU = 16.0
ATT_HEADS = 8
ATT_HD = 128
D_ATT = ATT_HEADS * ATT_HD
LEFT_CHUNKS = 8
REL_CLIP = 128
N_REL = 2 * REL_CLIP + 1
EPS = 1e-6
D_IN = 2 * GLA_KW + 2 * D_GLA + GLA_RANK + 4 * D_ATT
GLA_SCALE = GLA_DK ** -0.5
ATT_SCALE = ATT_HD ** -0.5

ADAM_LR = 0.001
ADAM_B1 = 0.9
ADAM_B2 = 0.999
ADAM_EPS = 1e-08
ADAM_WD = 0.01
ADAM_STEP = 10

N_DEV = 8
LANE = 128
GA_ORIG = 2 * GLA_KW + 2 * D_GLA
OFF_AQ = GA_ORIG
OFF_GA = GA_ORIG + 4 * D_ATT
D_ZP = OFF_GA + LANE
QB = 2 * CHUNK
ATT_UNROLL = 8
WIN = (LEFT_CHUNKS + 2) * CHUNK
ET_ROWS = WIN + LEFT_CHUNKS * CHUNK
NEG = -1e30
VMEM_LIMIT = 48 * 1024 * 1024


def _cparams(sem):
    return pltpu.CompilerParams(dimension_semantics=sem, vmem_limit_bytes=VMEM_LIMIT)


def _dot(a, b):
    return jnp.dot(a, b, preferred_element_type=F32)


def _dot_nt(a, b):
    return lax.dot_general(a, b, (((1,), (1,)), ((), ())), preferred_element_type=F32)


def _dot_tn(a, b):
    return lax.dot_general(a, b, (((0,), (0,)), ((), ())), preferred_element_type=F32)


def _dot01(t, x, left=True):
    if not left:
        t, x = x, t
    hi = x.astype(BF16)
    r = x - hi.astype(F32)
    mid = r.astype(BF16)
    lo = (r - mid.astype(F32)).astype(BF16)
    if left:
        return _dot(t, hi) + _dot(t, mid) + _dot(t, lo)
    return _dot(hi, t) + _dot(mid, t) + _dot(lo, t)


def _sigmoid(x):
    return 1.0 / (1.0 + jnp.exp(-x))


def _log_sigmoid(x):
    return jnp.minimum(x, 0.0) - jnp.log(1.0 + jnp.exp(-jnp.abs(x)))


TILES = {
    "in_proj": (512, D_ZP // 3, None),
    "in_proj_dx": (512, 512, None),
    "in_proj_dw": (D_ZP // 3, 512, None),
    "out_proj": (512, 1024, None),
    "out_proj_dx": (512, 1024, None),
    "out_proj_dw": (1024, 1024, None),
}


def _matmul(a, b, mode, out_dtype, tm, tn, tk, name, n_outer=False, after=None):
    if mode == "nn":
        (m, k), n = a.shape, b.shape[1]
    elif mode == "nt":
        (m, k), n = a.shape, b.shape[0]
    else:
        (k, m), n = a.shape, b.shape[1]
    tm, tn, tk = min(tm, m), min(tn, n), k if tk is None else min(tk, k)
    assert m % tm == 0 and n % tn == 0 and k % tk == 0, (name, m, n, k)
    nk = k // tk
    dot = {"nn": _dot, "nt": _dot_nt, "tn": _dot_tn}[mode]

    follows = [] if after is None else [after]

    def body_whole_k(a_ref, b_ref, *rest):
        o_ref = rest[-1]
        o_ref[...] = dot(a_ref[...], b_ref[...]).astype(out_dtype)

    def body(a_ref, b_ref, *rest):
        o_ref, acc_ref = rest[-2:]
        kk = pl.program_id(2)

        @pl.when(kk == 0)
        def _():
            acc_ref[...] = jnp.zeros_like(acc_ref)

        acc_ref[...] += dot(a_ref[...], b_ref[...])

        @pl.when(kk == nk - 1)
        def _():
            o_ref[...] = acc_ref[...].astype(out_dtype)

    def at(index):
        return (lambda j, i, kk: index(i, j, kk)) if n_outer else index

    if mode == "tn":
        a_spec = pl.BlockSpec((tk, tm), at(lambda i, j, kk: (kk, i)))
    else:
        a_spec = pl.BlockSpec((tm, tk), at(lambda i, j, kk: (i, kk)))
    if mode == "nt":
        b_spec = pl.BlockSpec((tn, tk), at(lambda i, j, kk: (j, kk)))
    else:
        b_spec = pl.BlockSpec((tk, tn), at(lambda i, j, kk: (kk, j)))
    return pl.pallas_call(
        body_whole_k if nk == 1 else body, name=name,
        grid=(n // tn, m // tm, nk) if n_outer else (m // tm, n // tn, nk),
        in_specs=[a_spec, b_spec] + [ANY] * len(follows),
        out_specs=pl.BlockSpec((tm, tn), at(lambda i, j, kk: (i, j))),
        out_shape=jax.ShapeDtypeStruct((m, n), out_dtype),
        scratch_shapes=[] if nk == 1 else [pltpu.VMEM((tm, tn), F32)],
        compiler_params=_cparams(("parallel", "parallel", "arbitrary")),
    )(a, b, *follows)


ROWS = 256


def _rms_fwd(x, g):
    s, d = x.shape

    def body(x_ref, g_ref, h_ref):
        xv = x_ref[...]
        r = lax.rsqrt(jnp.mean(xv * xv, axis=-1, keepdims=True) + EPS)
        h_ref[...] = (xv * r * g_ref[...]).astype(BF16)

    return pl.pallas_call(
        body, name="rms_fwd", grid=(s // ROWS,),
        in_specs=[pl.BlockSpec((ROWS, d), lambda i: (i, 0)), pl.BlockSpec((1, d), lambda i: (0, 0))],
        out_specs=pl.BlockSpec((ROWS, d), lambda i: (i, 0)),
        out_shape=jax.ShapeDtypeStruct((s, d), BF16),
        compiler_params=_cparams(("parallel",)),
    )(x, g)


def _post_fwd(x, y, g):
    s, d = x.shape

    def body(x_ref, y_ref, g_ref, o_ref):
        yv = y_ref[...]
        r = lax.rsqrt(jnp.mean(yv * yv, axis=-1, keepdims=True) + EPS)
        o_ref[...] = x_ref[...] + yv * r * g_ref[...]

    row = pl.BlockSpec((ROWS, d), lambda i: (i, 0))
    return pl.pallas_call(
        body, name="post_fwd", grid=(s // ROWS,),
        in_specs=[row, row, pl.BlockSpec((1, d), lambda i: (0, 0))],
        out_specs=row,
        out_shape=jax.ShapeDtypeStruct((s, d), F32),
        compiler_params=_cparams(("parallel",)),
    )(x, y, g)


def _loss_head(out, tgt):
    s, d = out.shape

    def body(o_ref, t_ref, dout_ref, sum_ref):
        @pl.when(pl.program_id(0) == 0)
        def _():
            sum_ref[...] = jnp.zeros_like(sum_ref)

        e = o_ref[...] - t_ref[...]
        dout_ref[...] = e * (1.0 / d)
        sum_ref[...] += jnp.sum(jnp.sum(e * e, axis=1, keepdims=True), axis=0, keepdims=True)

    row = pl.BlockSpec((ROWS, d), lambda i: (i, 0))
    return pl.pallas_call(
        body, name="loss_head", grid=(s // ROWS,),
        in_specs=[row, row],
        out_specs=[row, pl.BlockSpec((1, 1), lambda i: (0, 0))],
        out_shape=[jax.ShapeDtypeStruct((s, d), F32), jax.ShapeDtypeStruct((1, 1), F32)],
        compiler_params=_cparams(("arbitrary",)),
    )(out, tgt)


def _post_bwd(dout, y, g):
    s, d = y.shape

    def body(do_ref, y_ref, g_ref, dy_ref, dg_ref):
        @pl.when(pl.program_id(0) == 0)
        def _():
            dg_ref[...] = jnp.zeros_like(dg_ref)

        yv = y_ref[...]
        dv = do_ref[...]
        r = lax.rsqrt(jnp.mean(yv * yv, axis=-1, keepdims=True) + EPS)
        dg_ref[...] += jnp.sum(dv * yv * r, axis=0, keepdims=True)
        w = dv * g_ref[...]
        dy = r * (w - yv * (r * r) * jnp.mean(w * yv, axis=-1, keepdims=True))
        dy_ref[...] = dy.astype(BF16)

    row = pl.BlockSpec((ROWS, d), lambda i: (i, 0))
    vec = pl.BlockSpec((1, d), lambda i: (0, 0))
    return pl.pallas_call(
        body, name="post_bwd", grid=(s // ROWS,),
        in_specs=[row, row, vec],
        out_specs=[row, vec],
        out_shape=[jax.ShapeDtypeStruct((s, d), BF16), jax.ShapeDtypeStruct((1, d), F32)],
        compiler_params=_cparams(("arbitrary",)),
    )(dout, y, g)


def _pre_bwd(dh, x, g, dout):
    s, d = x.shape

    def body(dh_ref, x_ref, g_ref, do_ref, dx_ref, dg_ref):
        @pl.when(pl.program_id(0) == 0)
        def _():
            dg_ref[...] = jnp.zeros_like(dg_ref)

        xv = x_ref[...]
        dv = dh_ref[...]
        r = lax.rsqrt(jnp.mean(xv * xv, axis=-1, keepdims=True) + EPS)
        dg_ref[...] += jnp.sum(dv * xv * r, axis=0, keepdims=True)
        w = dv * g_ref[...]
        dx_ref[...] = do_ref[...] + r * (w - xv * (r * r) * jnp.mean(w * xv, axis=-1, keepdims=True))

    row = pl.BlockSpec((ROWS, d), lambda i: (i, 0))
    vec = pl.BlockSpec((1, d), lambda i: (0, 0))
    return pl.pallas_call(
        body, name="pre_bwd", grid=(s // ROWS,),
        in_specs=[row, row, vec, row],
        out_specs=[row, vec],
        out_shape=[jax.ShapeDtypeStruct((s, d), F32), jax.ShapeDtypeStruct((1, d), F32)],
        compiler_params=_cparams(("arbitrary",)),
    )(dh, x, g, dout)


GLA_STEP = 4
GLA_ROWS = GLA_STEP * CHUNK
GLA_CHUNKS = [slice(c * CHUNK, (c + 1) * CHUNK) for c in range(GLA_STEP)]


def _chunk_triangles():
    ri = lax.broadcasted_iota(jnp.int32, (GLA_ROWS, GLA_ROWS), 0)
    ci = lax.broadcasted_iota(jnp.int32, (GLA_ROWS, GLA_ROWS), 1)
    same = (ri // CHUNK) == (ci // CHUNK)
    return (jnp.where(same & (ri >= ci), 1.0, 0.0).astype(BF16), jnp.where(same & (ci >= ri), 1.0, 0.0).astype(BF16))


def _per_chunk(fn, like):
    row = lax.broadcasted_iota(jnp.int32, like.shape, 0)
    return [fn((row >= c * CHUNK) & (row < (c + 1) * CHUNK)) for c in range(GLA_STEP)]


def _spread(per_chunk, like):
    row = lax.broadcasted_iota(jnp.int32, like.shape, 0)
    out = per_chunk[-1]
    for c in reversed(range(GLA_STEP - 1)):
        out = jnp.where(row < (c + 1) * CHUNK, per_chunk[c], out)
    return out


def _gla_gate(ga_b, wa_b, b_ref, tri):
    pre = _dot(ga_b, wa_b) + b_ref[...]
    la = _log_sigmoid(pre) * (1.0 / GLA_TAU)
    cum = _dot01(tri, la)
    row = lax.broadcasted_iota(jnp.int32, cum.shape, 0)
    cends = [jnp.sum(jnp.where(row == (c + 1) * CHUNK - 1, cum, 0.0), axis=0, keepdims=True)
             for c in range(GLA_STEP)]
    return pre, cum, cends


def _heads(width):
    return [slice(h * width, (h + 1) * width) for h in range(GLA_HEADS)]


def _z_specs_gla(rev=None):
    idx = (lambda n: n) if rev is None else rev
    return [
        pl.BlockSpec((GLA_ROWS, GLA_KW), lambda n: (idx(n), 0)),
        pl.BlockSpec((GLA_ROWS, GLA_KW), lambda n: (idx(n), 1)),
        pl.BlockSpec((GLA_ROWS, D_GLA), lambda n: (idx(n), 1)),
        pl.BlockSpec((GLA_ROWS, D_GLA), lambda n: (idx(n), 2)),
        pl.BlockSpec((GLA_ROWS, LANE), lambda n: (idx(n), OFF_GA // LANE)),
    ]


def _gla_fwd(z, wa_pad, b_alpha, g_gla):
    s = z.shape[0]
    nchunk = s // CHUNK

    def body(q_ref, k_ref, v_ref, gg_ref, ga_ref, wa_ref, b_ref, g_ref, y_ref, o_ref, st_ref, state):
        @pl.when(pl.program_id(0) == 0)
        def _():
            state[...] = jnp.zeros_like(state)

        ga_b = ga_ref[...].astype(BF16)
        tri, _ = _chunk_triangles()
        nh = range(GLA_HEADS)
        keys, vals = _heads(GLA_DK), _heads(GLA_DV)
        _, cum, cends = _gla_gate(ga_b, wa_ref[...].astype(BF16), b_ref, tri)
        kd_b = (k_ref[...] * jnp.exp(_spread(cends, cum) - cum)).astype(BF16)
        qs = (q_ref[...] * GLA_SCALE).astype(BF16)
        v_b = v_ref[...].astype(BF16)
        uts = [[_dot_tn(v_b[rs, vals[h]], kd_b[rs, keys[h]]) for h in nh] for rs in GLA_CHUNKS]
        sts, prev = [], [state[h] for h in nh]
        for c in range(GLA_STEP):
            a = jnp.exp(cends[c])
            prev = [prev[h] * a[:, keys[h]] + uts[c][h] for h in nh]
            sts.append(prev)
        for h in nh:
            state[h] = prev[h]
            for c in range(GLA_STEP):
                st_ref[c, h] = sts[c][h]
        outs = [[_dot_nt(qs[rs, keys[h]], sts[c][h].astype(BF16)) for h in nh] for c, rs in enumerate(GLA_CHUNKS)]
        for h in nh:
            o, vs = jnp.concatenate([outs[c][h] for c in range(GLA_STEP)], axis=0), vals[h]
            o_ref[:, vs] = o
            r = lax.rsqrt(jnp.mean(o * o, axis=-1, keepdims=True) + EPS)
            gg = gg_ref[:, vs]
            y_ref[:, vs] = (o * r * g_ref[:, vs] * (gg * _sigmoid(gg))).astype(BF16)

    full = lambda shape: pl.BlockSpec(shape, lambda n: tuple(0 for _ in shape))
    wide = pl.BlockSpec((GLA_ROWS, D_GLA), lambda n: (n, 0))
    return pl.pallas_call(
        body, name="gla_fwd", grid=(nchunk // GLA_STEP,),
        in_specs=_z_specs_gla() + [full((LANE, GLA_KW)), full((1, GLA_KW)), full((1, D_GLA))],
        out_specs=[wide, wide, pl.BlockSpec((GLA_STEP, GLA_HEADS, GLA_DV, GLA_DK), lambda n: (n, 0, 0, 0))],
        out_shape=[jax.ShapeDtypeStruct((s, D_GLA), BF16), jax.ShapeDtypeStruct((s, D_GLA), F32),
                   jax.ShapeDtypeStruct((nchunk, GLA_HEADS, GLA_DV, GLA_DK), F32)],
        scratch_shapes=[pltpu.VMEM((GLA_HEADS, GLA_DV, GLA_DK), F32)],
        compiler_params=_cparams(("arbitrary",)),
    )(z, z, z, z, z, wa_pad, b_alpha, g_gla)


def _gla_bwd(dyc, o_gla, z, wa_pad, b_alpha, g_gla, states):
    s = z.shape[0]
    nsteps = s // GLA_ROWS
    rev = lambda n: nsteps - 1 - n

    def body(dy_ref, o_ref, q_ref, k_ref, v_ref, gg_ref, ga_ref, wa_ref, b_ref, g_ref, st_ref, stp_ref,
             dq_ref, dk_ref, dv_ref, dgg_ref, dga_ref, dwa_ref, db_ref, dg_ref, carry):
        step = pl.program_id(0)

        @pl.when(step == 0)
        def _():
            carry[...] = jnp.zeros_like(carry)
            dwa_ref[...] = jnp.zeros_like(dwa_ref)
            db_ref[...] = jnp.zeros_like(db_ref)
            dg_ref[...] = jnp.zeros_like(dg_ref)

        has_prev = (step < nsteps - 1).astype(F32)
        ga_b = ga_ref[...].astype(BF16)
        tri, tri_up = _chunk_triangles()
        nh, nc = range(GLA_HEADS), range(GLA_STEP)
        keys, vals = _heads(GLA_DK), _heads(GLA_DV)
        wa_b = wa_ref[...].astype(BF16)
        pre, cum, cends = _gla_gate(ga_b, wa_b, b_ref, tri)
        e = jnp.exp(_spread(cends, cum) - cum)
        a = [jnp.exp(cends[c]) for c in nc]
        kf = k_ref[...]
        kd_b = (kf * e).astype(BF16)
        v_b = v_ref[...].astype(BF16)
        qs = (q_ref[...] * GLA_SCALE).astype(BF16)
        do_b = []
        for h in nh:
            vs = vals[h]
            o = o_ref[:, vs]
            gg = gg_ref[:, vs]
            g = g_ref[:, vs]
            dy = dy_ref[:, vs]
            r = lax.rsqrt(jnp.mean(o * o, axis=-1, keepdims=True) + EPS)
            sg = _sigmoid(gg)
            dogn = dy * (gg * sg)
            dgg_ref[:, vs] = (dy * (o * r * g) * (sg * (1.0 + gg * (1.0 - sg)))).astype(BF16)
            dg_ref[:, vs] += jnp.sum(dogn * o * r, axis=0, keepdims=True)
            w = dogn * g
            do_b.append((r * (w - o * (r * r) * jnp.mean(w * o, axis=-1, keepdims=True))).astype(BF16))
        dqs = [jnp.concatenate([_dot(do_b[h][rs], st_ref[c, h].astype(BF16)) for c, rs in enumerate(GLA_CHUNKS)],
                               axis=0) for h in nh]
        dq_ref[...] = (jnp.concatenate(dqs, axis=1) * GLA_SCALE).astype(BF16)
        own = [[_dot_tn(do_b[h][rs], qs[rs, keys[h]]) for h in nh] for rs in GLA_CHUNKS]
        gts, later = [None] * GLA_STEP, [carry[h] for h in nh]
        for c in reversed(nc):
            gts[c] = [own[c][h] + later[h] for h in nh]
            later = [gts[c][h] * a[c][:, keys[h]] for h in nh]
        for h in nh:
            carry[h] = later[h]
        gt_b = [[gts[c][h].astype(BF16) for h in nh] for c in nc]
        dkd = jnp.concatenate([jnp.concatenate([_dot(v_b[rs, vals[h]], gt_b[c][h]) for h in nh], axis=1)
                               for c, rs in enumerate(GLA_CHUNKS)], axis=0)
        dvs = [[_dot_nt(kd_b[rs, keys[h]], gt_b[c][h]) for h in nh] for c, rs in enumerate(GLA_CHUNKS)]
        before = lambda c, h: st_ref[c - 1, h] if c > 0 else stp_ref[0, h] * has_prev
        da = [jnp.concatenate([jnp.sum(gts[c][h] * before(c, h), axis=0, keepdims=True) for h in nh], axis=1)
              for c in nc]
        for h in nh:
            dv_ref[:, vals[h]] = jnp.concatenate([dvs[c][h] for c in nc], axis=0).astype(BF16)
        dk_ref[...] = (dkd * e).astype(BF16)
        dd = dkd * kf * e
        dsum = _per_chunk(lambda mine: jnp.sum(jnp.where(mine, dd, 0.0), axis=0, keepdims=True), dd)
        dcend = _spread([dsum[c] + da[c] * a[c] for c in nc], dd)
        dla = dcend - _dot01(tri_up, dd)
        dpre = dla * (1.0 / GLA_TAU) * (1.0 - _sigmoid(pre))
        dpre_b = dpre.astype(BF16)
        dga_ref[...] = _dot_nt(dpre_b, wa_b).astype(BF16)
        dwa_ref[...] += _dot_tn(ga_b, dpre_b)
        db_ref[...] += jnp.sum(dpre, axis=0, keepdims=True)

    full = lambda shape: pl.BlockSpec(shape, lambda n: tuple(0 for _ in shape))
    wide = pl.BlockSpec((GLA_ROWS, D_GLA), lambda n: (rev(n), 0))
    keyw = pl.BlockSpec((GLA_ROWS, GLA_KW), lambda n: (rev(n), 0))
    st_spec = pl.BlockSpec((GLA_STEP, GLA_HEADS, GLA_DV, GLA_DK), lambda n: (rev(n), 0, 0, 0))
    stp_spec = pl.BlockSpec((1, GLA_HEADS, GLA_DV, GLA_DK),
                            lambda n: (jnp.maximum(GLA_STEP * rev(n) - 1, 0), 0, 0, 0))
    return pl.pallas_call(
        body, name="gla_bwd", grid=(nsteps,),
        in_specs=[wide, wide] + _z_specs_gla(rev)
        + [full((LANE, GLA_KW)), full((1, GLA_KW)), full((1, D_GLA)), st_spec, stp_spec],
        out_specs=[keyw, keyw, wide, wide, pl.BlockSpec((GLA_ROWS, LANE), lambda n: (rev(n), 0)),
                   full((LANE, GLA_KW)), full((1, GLA_KW)), full((1, D_GLA))],
        out_shape=[jax.ShapeDtypeStruct((s, GLA_KW), BF16), jax.ShapeDtypeStruct((s, GLA_KW), BF16),
                   jax.ShapeDtypeStruct((s, D_GLA), BF16), jax.ShapeDtypeStruct((s, D_GLA), BF16),
                   jax.ShapeDtypeStruct((s, LANE), BF16),
                   jax.ShapeDtypeStruct((LANE, GLA_KW), F32), jax.ShapeDtypeStruct((1, GLA_KW), F32),
                   jax.ShapeDtypeStruct((1, D_GLA), F32)],
        scratch_shapes=[pltpu.VMEM((GLA_HEADS, GLA_DV, GLA_DK), F32)],
        compiler_params=_cparams(("arbitrary",)),
    )(dyc, o_gla, z, z, z, z, z, wa_pad, b_alpha, g_gla, states, states)


def _build_bias_table(rb_row, et_ref):
    far = jnp.broadcast_to(rb_row[:, 2 * REL_CLIP:2 * REL_CLIP + 1], (1, LANE))
    near_hi = rb_row[:, REL_CLIP:2 * REL_CLIP]
    near_lo = rb_row[:, 0:REL_CLIP]
    past = jnp.broadcast_to(rb_row[:, 0:1], (1, LANE))
    seg = [far, far, far, far, near_hi, near_lo] + [past] * (ET_ROWS // LANE - 5)
    ri = lax.broadcasted_iota(jnp.int32, (LANE, LANE), 0)
    ci = lax.broadcasted_iota(jnp.int32, (LANE, LANE), 1)
    for kb in range(ET_ROWS // LANE):
        wmat = jnp.where(ri + ci < LANE, seg[kb], seg[kb + 1])
        blk = pltpu.roll(wmat, 0, 1, stride=1, stride_axis=0)
        lag = LEFT_CHUNKS + ci // CHUNK - (2 * kb + ri // CHUNK)
        et_ref[kb * LANE:(kb + 1) * LANE, :] = jnp.where((lag >= 0) & (lag <= LEFT_CHUNKS), blk, NEG)


def _reduce_bias_table(det_ref):
    lane = lax.broadcasted_iota(jnp.int32, (1, LANE), 1)
    ri = lax.broadcasted_iota(jnp.int32, (LANE, LANE), 0)
    ci = lax.broadcasted_iota(jnp.int32, (LANE, LANE), 1)
    flip = jnp.where(ri + ci == LANE - 1, 1.0, 0.0).astype(BF16)
    segs = jnp.zeros((8, LANE), F32)
    seg_row = lax.broadcasted_iota(jnp.int32, (8, LANE), 0)
    prev_minus = jnp.zeros((1, LANE), F32)
    for kb in range(6):
        rolled = pltpu.roll(_dot01(det_ref[kb * LANE:(kb + 1) * LANE, :], flip, left=False), 0, 1,
                            stride=1, stride_axis=0)
        plus = jnp.sum(jnp.where(ci >= ri, rolled, 0.0), axis=0, keepdims=True)
        minus = jnp.sum(jnp.where(ci < ri, rolled, 0.0), axis=0, keepdims=True)
        segs = segs + jnp.where(seg_row == kb, plus + prev_minus, 0.0)
        prev_minus = minus
    segs = _dot01(segs, flip, left=False)
    pick = lambda kb: jnp.sum(jnp.where(seg_row == kb, segs, 0.0), axis=0, keepdims=True)
    far = jnp.sum(pick(0) + pick(1) + pick(2) + pick(3), axis=1, keepdims=True)
    last = jnp.where(lane == 0, far, 0.0)
    return jnp.concatenate([pick(5), pick(4), last], axis=1)


def _att_window(b):
    c0 = 2 * b
    kstart = pl.multiple_of(jnp.maximum(c0 - LEFT_CHUNKS, 0) * CHUNK, CHUNK)
    eoff = pl.multiple_of(jnp.maximum(LEFT_CHUNKS - c0, 0) * CHUNK, CHUNK)
    return kstart, eoff


def _att_probs(q_b, kw_b, et):
    st = _dot_nt(kw_b, q_b) * ATT_SCALE + et
    m = jnp.max(st, axis=0, keepdims=True)
    ex = jnp.exp(st - m)
    return ex * (1.0 / jnp.sum(ex, axis=0, keepdims=True))


def _att_fwd(z, rb_pad, g_att):
    s = z.shape[0]
    nblk = s // QB
    c_aq, c_ak, c_av, c_ag = [(OFF_AQ + i * D_ATT) // ATT_HD for i in range(4)]

    def body(q_ref, k_ref, v_ref, ag_ref, rb_ref, g_ref, y_ref, o_ref, et_ref, kb_ref, vb_ref):
        h = pl.program_id(0)
        b = pl.program_id(1)

        @pl.when(b == 0)
        def _():
            _build_bias_table(rb_ref[pl.ds(h, 1), :], et_ref)
            kb_ref[...] = k_ref[...].astype(BF16)
            vb_ref[...] = v_ref[...].astype(BF16)

        for j in range(ATT_UNROLL):
            rs = slice(j * QB, (j + 1) * QB)
            kstart, eoff = _att_window(b * ATT_UNROLL + j)
            q_b = q_ref[rs, :].astype(BF16)
            kw_b = kb_ref[pl.ds(kstart, WIN), :]
            vw_b = vb_ref[pl.ds(kstart, WIN), :]
            pt = _att_probs(q_b, kw_b, et_ref[pl.ds(eoff, WIN), :])
            o = _dot_tn(pt.astype(BF16), vw_b)
            o_ref[rs, :] = o
            r = lax.rsqrt(jnp.mean(o * o, axis=-1, keepdims=True) + EPS)
            ag = ag_ref[rs, :]
            y_ref[rs, :] = (o * r * g_ref[...] * (ag * _sigmoid(ag))).astype(BF16)

    blk = lambda col: pl.BlockSpec((ATT_UNROLL * QB, ATT_HD), lambda h, b: (b, col + h))
    seq = lambda col: pl.BlockSpec((s, ATT_HD), lambda h, b: (0, col + h))
    out_blk = pl.BlockSpec((ATT_UNROLL * QB, ATT_HD), lambda h, b: (b, h))
    return pl.pallas_call(
        body, name="att_fwd", grid=(ATT_HEADS, nblk // ATT_UNROLL),
        in_specs=[blk(c_aq), seq(c_ak), seq(c_av), blk(c_ag),
                  pl.BlockSpec((ATT_HEADS, 3 * LANE), lambda h, b: (0, 0)),
                  pl.BlockSpec((1, ATT_HD), lambda h, b: (0, h))],
        out_specs=[out_blk, out_blk],
        out_shape=[jax.ShapeDtypeStruct((s, D_ATT), BF16), jax.ShapeDtypeStruct((s, D_ATT), F32)],
        scratch_shapes=[pltpu.VMEM((ET_ROWS, LANE), F32), pltpu.VMEM((s, ATT_HD), BF16),
                        pltpu.VMEM((s, ATT_HD), BF16)],
        compiler_params=_cparams(("arbitrary", "arbitrary")),
    )(z, z, z, z, rb_pad, g_att)


def _att_bwd(dyc, o_att, z, rb_pad, g_att):
    s = z.shape[0]
    nblk = s // QB
    c_aq, c_ak, c_av, c_ag = [(OFF_AQ + i * D_ATT) // ATT_HD for i in range(4)]
    c_dy = D_GLA // ATT_HD

    def body(dy_ref, o_ref, q_ref, k_ref, v_ref, ag_ref, rb_ref, g_ref,
             dq_ref, dk_ref, dv_ref, dag_ref, drb_ref, dg_ref, et_ref, det_ref, kb_ref, vb_ref):
        h = pl.program_id(0)
        b = pl.program_id(1)

        @pl.when(b == 0)
        def _():
            _build_bias_table(rb_ref[pl.ds(h, 1), :], et_ref)
            kb_ref[...] = k_ref[...].astype(BF16)
            vb_ref[...] = v_ref[...].astype(BF16)
            det_ref[...] = jnp.zeros_like(det_ref)
            dk_ref[...] = jnp.zeros_like(dk_ref)
            dv_ref[...] = jnp.zeros_like(dv_ref)
            dg_ref[...] = jnp.zeros_like(dg_ref)

        g = g_ref[...]
        dg = jnp.zeros((1, ATT_HD), F32)
        for j in range(ATT_UNROLL):
            rs = slice(j * QB, (j + 1) * QB)
            kstart, eoff = _att_window(b * ATT_UNROLL + j)
            q_b = q_ref[rs, :].astype(BF16)
            kw_b = kb_ref[pl.ds(kstart, WIN), :]
            vw_b = vb_ref[pl.ds(kstart, WIN), :]
            pt = _att_probs(q_b, kw_b, et_ref[pl.ds(eoff, WIN), :])
            o = o_ref[rs, :]
            ag = ag_ref[rs, :]
            dy = dy_ref[rs, :]
            r = lax.rsqrt(jnp.mean(o * o, axis=-1, keepdims=True) + EPS)
            sg = _sigmoid(ag)
            don = dy * (ag * sg)
            dag_ref[rs, :] = (dy * (o * r * g) * (sg * (1.0 + ag * (1.0 - sg)))).astype(BF16)
            dg = dg + jnp.sum(don * o * r, axis=0, keepdims=True)
            w = don * g
            do_b = (r * (w - o * (r * r) * jnp.mean(w * o, axis=-1, keepdims=True))).astype(BF16)
            pt_b = pt.astype(BF16)
            dpt = _dot_nt(vw_b, do_b)
            dst = pt * (dpt - jnp.sum(dpt * pt, axis=0, keepdims=True))
            det_ref[pl.ds(eoff, WIN), :] += dst
            ds_b = (dst * ATT_SCALE).astype(BF16)
            dq_ref[rs, :] = _dot_tn(ds_b, kw_b).astype(BF16)
            dk_ref[pl.ds(kstart, WIN), :] += _dot(ds_b, q_b)
            dv_ref[pl.ds(kstart, WIN), :] += _dot(pt_b, do_b)
        dg_ref[...] += dg

        @pl.when(b == nblk // ATT_UNROLL - 1)
        def _():
            drb_ref[0] = jnp.broadcast_to(_reduce_bias_table(det_ref), (8, 3 * LANE))

    blk = lambda col: pl.BlockSpec((ATT_UNROLL * QB, ATT_HD), lambda h, b: (b, col + h))
    seq = lambda col: pl.BlockSpec((s, ATT_HD), lambda h, b: (0, col + h))
    out_blk = pl.BlockSpec((ATT_UNROLL * QB, ATT_HD), lambda h, b: (b, h))
    out_seq = pl.BlockSpec((s, ATT_HD), lambda h, b: (0, h))
    return pl.pallas_call(
        body, name="att_bwd", grid=(ATT_HEADS, nblk // ATT_UNROLL),
        in_specs=[blk(c_dy), blk(0), blk(c_aq), seq(c_ak), seq(c_av), blk(c_ag),
                  pl.BlockSpec((ATT_HEADS, 3 * LANE), lambda h, b: (0, 0)),
                  pl.BlockSpec((1, ATT_HD), lambda h, b: (0, h))],
        out_specs=[out_blk, out_seq, out_seq, out_blk,
                   pl.BlockSpec((1, 8, 3 * LANE), lambda h, b: (h, 0, 0)),
                   pl.BlockSpec((1, ATT_HD), lambda h, b: (0, h))],
        out_shape=[jax.ShapeDtypeStruct((s, D_ATT), BF16), jax.ShapeDtypeStruct((s, D_ATT), F32),
                   jax.ShapeDtypeStruct((s, D_ATT), F32), jax.ShapeDtypeStruct((s, D_ATT), BF16),
                   jax.ShapeDtypeStruct((ATT_HEADS, 8, 3 * LANE), F32),
                   jax.ShapeDtypeStruct((1, D_ATT), F32)],
        scratch_shapes=[pltpu.VMEM((ET_ROWS, LANE), F32), pltpu.VMEM((ET_ROWS, LANE), F32),
                        pltpu.VMEM((s, ATT_HD), BF16), pltpu.VMEM((s, ATT_HD), BF16)],
        compiler_params=_cparams(("arbitrary", "arbitrary")),
    )(dyc, o_att, z, z, z, z, rb_pad, g_att)


ADAM_ROWS = 64
ADAM_COL_ROWS = 32


def _adam_math(w, g, m, v):
    m2 = ADAM_B1 * m + (1.0 - ADAM_B1) * g
    v2 = ADAM_B2 * v + (1.0 - ADAM_B2) * (g * g)
    m_hat = m2 / (1.0 - ADAM_B1 ** ADAM_STEP)
    v_hat = v2 / (1.0 - ADAM_B2 ** ADAM_STEP)
    delta = -ADAM_LR * (m_hat / (jnp.sqrt(v_hat) + ADAM_EPS) + ADAM_WD * w)
    return delta, m2, v2


def _adam_sharded(parts, first, w, m, v, name):
    nl, nr, nc = w.shape

    def body(*refs):
        p_refs = refs[:nl]
        w_ref, m_ref, v_ref, g_ref, d_ref, m2_ref, v2_ref = refs[nl:]
        for k in range(nl):
            @pl.when(pl.program_id(0) == k)
            def _(p_ref=p_refs[k]):
                g = p_ref[0].astype(F32)
                for dev in range(1, N_DEV):
                    g = g + p_ref[dev].astype(F32)
                delta, m2, v2 = _adam_math(w_ref[0], g, m_ref[0], v_ref[0])
                g_ref[0] = g
                d_ref[0] = delta
                m2_ref[0] = m2
                v2_ref[0] = v2

    def part_spec(k):
        return pl.BlockSpec((N_DEV, ADAM_ROWS, nc), lambda l, i: (0, first + jnp.where(l == k, i, 0), 0))

    blk = pl.BlockSpec((1, ADAM_ROWS, nc), lambda l, i: (l, i, 0))
    shp = jax.ShapeDtypeStruct(w.shape, F32)
    return pl.pallas_call(
        body, name=name, grid=(nl, pl.cdiv(nr, ADAM_ROWS)),
        in_specs=[part_spec(k) for k in range(nl)] + [blk, blk, blk],
        out_specs=[blk, blk, blk, blk],
        out_shape=[shp, shp, shp, shp],
        compiler_params=_cparams(("arbitrary", "arbitrary")),
    )(*parts, w, m, v)


def _adam_columns(parts, first, w, m, v):
    nc, nl, d = w.shape

    def body(*refs):
        p_refs = refs[:nl]
        w_ref, m_ref, v_ref, g_ref, d_ref, m2_ref, v2_ref = refs[nl:]
        for l in range(nl):
            g = p_refs[l][0].astype(F32)
            for dev in range(1, N_DEV):
                g = g + p_refs[l][dev].astype(F32)
            delta, m2, v2 = _adam_math(w_ref[:, l, :], g, m_ref[:, l, :], v_ref[:, l, :])
            g_ref[:, l, :] = g
            d_ref[:, l, :] = delta
            m2_ref[:, l, :] = m2
            v2_ref[:, l, :] = v2

    blk = pl.BlockSpec((ADAM_COL_ROWS, nl, d), lambda i: (i, 0, 0))
    part = pl.BlockSpec((N_DEV, ADAM_COL_ROWS, d), lambda i: (0, first + i, 0))
    shp = jax.ShapeDtypeStruct(w.shape, F32)
    return pl.pallas_call(
        body, name="adam_w_in", grid=(pl.cdiv(nc, ADAM_COL_ROWS),),
        in_specs=[part] * nl + [blk, blk, blk],
        out_specs=[blk, blk, blk, blk],
        out_shape=[shp, shp, shp, shp],
        compiler_params=_cparams(("parallel",)),
    )(*parts, w, m, v)


def _adam_small(w, g, m, v):
    def body(w_ref, g_ref, m_ref, v_ref, d_ref, m2_ref, v2_ref):
        delta, m2, v2 = _adam_math(w_ref[...], g_ref[...], m_ref[...], v_ref[...])
        d_ref[...] = delta
        m2_ref[...] = m2
        v2_ref[...] = v2

    shp = jax.ShapeDtypeStruct(w.shape, F32)
    return pl.pallas_call(body, name="adam_small", out_shape=[shp, shp, shp])(w, g, m, v)


def _position():
    return lax.axis_index("x"), lax.axis_index("y"), lax.axis_index("c")


def _slot(p):
    return 4 * p[0] + 2 * p[1] + p[2]


BF16_TILE_ROWS = 16


def _slab_rows(rows, cols):
    return -(-(rows + cols) // BF16_TILE_ROWS) * BF16_TILE_ROWS


RELAYOUT_COLS = 512
RELAYOUT_CHUNK = 64


def _shard_pieces(dev, rows, cols):
    moved = ((0, GA_ORIG, 0), (GA_ORIG, GA_ORIG + GLA_RANK, OFF_GA - GA_ORIG), (GA_ORIG + GLA_RANK, D_IN, -GLA_RANK))
    c0, c1 = dev * cols, (dev + 1) * cols
    return [(rows + max(c0, lo) - c0, max(c0, lo) + off, min(c1, hi) - max(c0, lo))
            for lo, hi, off in moved if max(c0, lo) < min(c1, hi)]


def _move_rows(src, src_row, dst, dst_row, n):
    assert src_row % 2 == 0 and dst_row % 2 == 0 and n % 2 == 0
    for r in range(0, n // 2, RELAYOUT_CHUNK):
        m = min(RELAYOUT_CHUNK, n // 2 - r)
        dst[dst_row // 2 + r:dst_row // 2 + r + m, :] = src[src_row // 2 + r:src_row // 2 + r + m, :]


def _aligned_weight(land, rows, cols):
    _, slab, d = land.shape
    ct = min(RELAYOUT_COLS, d)

    def body(land_ref, wt_ref, wo_ref):
        dev = pl.program_id(1)
        src = land_ref.bitcast(jnp.uint32)
        dst = wt_ref.bitcast(jnp.uint32)
        wo_ref[...] = land_ref[0:rows, :]

        @pl.when(dev == 0)
        def _():
            dst[D_IN // 2:D_ZP // 2, :] = jnp.zeros(((D_ZP - D_IN) // 2, ct), jnp.uint32)

        for k in range(N_DEV):
            @pl.when(dev == k)
            def _(k=k):
                for at, to, n in _shard_pieces(k, rows, cols):
                    _move_rows(src, at, dst, to, n)

    return pl.pallas_call(
        body, name="aligned_weight", grid=(d // ct, N_DEV),
        in_specs=[pl.BlockSpec((slab, ct), lambda c, dev: (dev, c))],
        out_specs=[pl.BlockSpec((D_ZP, ct), lambda c, dev: (0, c)),
                   pl.BlockSpec((rows, ct), lambda c, dev: (dev, c))],
        out_shape=[jax.ShapeDtypeStruct((D_ZP, d), land.dtype),
                   jax.ShapeDtypeStruct((N_DEV * rows, d), land.dtype)],
        compiler_params=_cparams(("parallel", "arbitrary")),
    )(land.reshape(N_DEV * slab, d))


def _partial_slabs(dwt, cols):
    d = dwt.shape[1]
    slab = _slab_rows(0, cols)
    ct = min(RELAYOUT_COLS, d)

    def body(dwt_ref, out_ref):
        dev = pl.program_id(1)
        src = dwt_ref.bitcast(jnp.uint32)
        dst = out_ref.bitcast(jnp.uint32)
        dst[cols // 2:slab // 2, :] = jnp.zeros(((slab - cols) // 2, ct), jnp.uint32)
        for k in range(N_DEV):
            @pl.when(dev == k)
            def _(k=k):
                for to, at, n in _shard_pieces(k, 0, cols):
                    _move_rows(src, at, dst, to, n)

    return pl.pallas_call(
        body, name="partial_slabs", grid=(d // ct, N_DEV),
        in_specs=[pl.BlockSpec((D_ZP, ct), lambda c, dev: (0, c))],
        out_specs=pl.BlockSpec((slab, ct), lambda c, dev: (dev, c)),
        out_shape=jax.ShapeDtypeStruct((N_DEV * slab, d), dwt.dtype),
        compiler_params=_cparams(("parallel", "arbitrary")),
    )(dwt).reshape(N_DEV, slab, d)


def _peer(pos, k):
    x, y, c = pos
    return (1 - x if k & 4 else x, 1 - y if k & 2 else y, 1 - c if k & 1 else c)


HBM_SPEC = pl.BlockSpec(memory_space=pltpu.HBM)
SEM_SPEC = pl.BlockSpec(memory_space=pltpu.SEMAPHORE)
GATHER_PEERS = (1, 4, 2, 6)
ALL_PEERS = (1, 2, 3, 4, 5, 6, 7)


def _hbm(a):
    return pltpu.with_memory_space_constraint(a, pltpu.HBM)


def _split_copies(src_ref, land_ref, send_sems, recv_sems, ks, per_peer, landed):
    me = _position()
    out = []
    for i, k in enumerate(ks):
        peer = _peer(me, k)
        src = src_ref.at[_slot(peer)] if per_peer else src_ref
        dst = land_ref.at[_slot(peer) if landed else _slot(me)]
        out.append(pltpu.make_async_remote_copy(
            src_ref=src, dst_ref=dst, send_sem=send_sems.at[i], recv_sem=recv_sems.at[i],
            device_id=peer, device_id_type=MESH))
    return out


def _exchange_start(src, after, ks, per_peer, name):
    slab = src.shape[1:] if per_peer else src.shape
    land_shape = (N_DEV,) + tuple(slab)
    n = len(ks)

    def body(src_ref, land_ref, after_ref, send_sems, recv_sems, src_thru, land_thru, token):
        for cp in _split_copies(src_ref, land_ref, send_sems, recv_sems, ks, per_peer, landed=False):
            cp.start()
        token[...] = jnp.zeros_like(token)

    return pl.pallas_call(
        body, name=name,
        out_shape=(pltpu.SemaphoreType.DMA((n,)), pltpu.SemaphoreType.DMA((n,)),
                   pltpu.HBM(src.shape, src.dtype), pltpu.HBM(land_shape, src.dtype),
                   jax.ShapeDtypeStruct((8, LANE), F32)),
        in_specs=(HBM_SPEC, HBM_SPEC, ANY),
        out_specs=(SEM_SPEC, SEM_SPEC, HBM_SPEC, HBM_SPEC, pl.BlockSpec(memory_space=pltpu.VMEM)),
        input_output_aliases={0: 2, 1: 3},
        compiler_params=pltpu.CompilerParams(has_side_effects=pltpu.SideEffectType.DATAFLOW_SIDE_EFFECTING),
    )(_hbm(src), _hbm(lax.empty(land_shape, src.dtype)), after)


def _exchange_wait(started, after, ks, per_peer, name):
    send_sems, recv_sems, src_thru, land_thru = started

    def body(src_ref, land_ref, send_sems, recv_sems, after_ref, src_dead, land_out):
        for cp in _split_copies(src_ref, land_ref, send_sems, recv_sems, ks, per_peer, landed=True):
            cp.wait_send()
            cp.wait_recv()

    return pl.pallas_call(
        body, name=name,
        out_shape=(pltpu.HBM(src_thru.shape, src_thru.dtype), pltpu.HBM(land_thru.shape, land_thru.dtype)),
        in_specs=(HBM_SPEC, HBM_SPEC, SEM_SPEC, SEM_SPEC, ANY), out_specs=(HBM_SPEC, HBM_SPEC),
        input_output_aliases={0: 0, 1: 1},
        compiler_params=pltpu.CompilerParams(has_side_effects=pltpu.SideEffectType.DATAFLOW_SIDE_EFFECTING),
    )(src_thru, land_thru, send_sems, recv_sems, after)


def _relay_copies(land_ref, send_sems, recv_sems, landed):
    me = _position()
    sibling = _peer(me, 1)
    out = []
    for i, k in enumerate(GATHER_PEERS[1:]):
        blk = land_ref.at[_slot(_peer(sibling if landed else me, k))]
        out.append(pltpu.make_async_remote_copy(
            src_ref=blk, dst_ref=blk, send_sem=send_sems.at[i], recv_sem=recv_sems.at[i],
            device_id=sibling, device_id_type=MESH))
    return out


def _relay_start(land, name):
    n = len(GATHER_PEERS) - 1

    def body(land_ref, send_sems, recv_sems, land_thru, token):
        for cp in _relay_copies(land_ref, send_sems, recv_sems, landed=False):
            cp.start()
        token[...] = jnp.zeros_like(token)

    return pl.pallas_call(
        body, name=name,
        out_shape=(pltpu.SemaphoreType.DMA((n,)), pltpu.SemaphoreType.DMA((n,)),
                   pltpu.HBM(land.shape, land.dtype), jax.ShapeDtypeStruct((8, LANE), F32)),
        in_specs=(HBM_SPEC,),
        out_specs=(SEM_SPEC, SEM_SPEC, HBM_SPEC, pl.BlockSpec(memory_space=pltpu.VMEM)),
        input_output_aliases={0: 2},
        compiler_params=pltpu.CompilerParams(has_side_effects=pltpu.SideEffectType.DATAFLOW_SIDE_EFFECTING),
    )(_hbm(land))


def _relay_wait(started, after, name):
    send_sems, recv_sems, land_thru = started

    def body(land_ref, send_sems, recv_sems, after_ref, land_out):
        for cp in _relay_copies(land_ref, send_sems, recv_sems, landed=True):
            cp.wait_send()
            cp.wait_recv()

    return pl.pallas_call(
        body, name=name,
        out_shape=pltpu.HBM(land_thru.shape, land_thru.dtype),
        in_specs=(HBM_SPEC, SEM_SPEC, SEM_SPEC, ANY), out_specs=HBM_SPEC,
        input_output_aliases={0: 0},
        compiler_params=pltpu.CompilerParams(has_side_effects=pltpu.SideEffectType.DATAFLOW_SIDE_EFFECTING),
    )(land_thru, send_sems, recv_sems, after)


def _share(vec, name, after=None):
    follows = [] if after is None else [after]

    def body(vec_ref, *rest):
        out_ref, send_sems, recv_sems, local_sem = rest[len(follows):]
        me = _position()

        def copy(k, landed):
            peer = _peer(me, k)
            return pltpu.make_async_remote_copy(
                src_ref=vec_ref, dst_ref=out_ref.at[_slot(peer) if landed else _slot(me)],
                send_sem=send_sems.at[k - 1], recv_sem=recv_sems.at[k - 1], device_id=peer, device_id_type=MESH)

        mine = pltpu.make_async_copy(vec_ref, out_ref.at[_slot(me)], local_sem)
        mine.start()
        sent = [copy(k, False) for k in ALL_PEERS]
        for cp in sent:
            cp.start()
        for k in ALL_PEERS:
            copy(k, True).wait_recv()
        for cp in sent:
            cp.wait_send()
        mine.wait()

    return pl.pallas_call(
        body, name=name,
        in_specs=[ANY] * (1 + len(follows)), out_specs=ANY,
        out_shape=jax.ShapeDtypeStruct((N_DEV,) + vec.shape, vec.dtype),
        scratch_shapes=[pltpu.SemaphoreType.DMA((N_DEV - 1,)), pltpu.SemaphoreType.DMA((N_DEV - 1,)),
                        pltpu.SemaphoreType.DMA],
    )(vec, *follows)


def _sum_slots(parts):
    def body(p_ref, o_ref):
        acc = p_ref[0]
        for dev in range(1, N_DEV):
            acc = acc + p_ref[dev]
        o_ref[...] = acc

    return pl.pallas_call(body, name="sum_slots",
                          out_shape=jax.ShapeDtypeStruct(parts.shape[1:], F32))(parts)


PACK_ROWS = 8


def _packed_rows(size):
    return -(-size // (PACK_ROWS * LANE)) * PACK_ROWS


def _pack(arrs):
    def rows(a):
        flat = a.reshape(-1)
        return jnp.pad(flat, (0, _packed_rows(flat.shape[0]) * LANE - flat.shape[0])).reshape(-1, LANE)

    return jnp.concatenate([rows(a) for a in arrs], axis=0)


def _unpack(packed, shapes):
    out, at = [], 0
    for shp in shapes:
        size = 1
        for dim in shp:
            size *= dim
        nrows = _packed_rows(size)
        out.append(packed[at:at + nrows].reshape(-1)[:size].reshape(shp))
        at += nrows
    return out


def _layer_fwd(x, wt, wo, g_pre, g_post, wa_pad, b_alpha, g_gla, g_att, rb_pad, midway=None):
    h = _rms_fwd(x, g_pre)
    z = _matmul(h, wt, "nt", F32, *TILES["in_proj"], "in_proj", n_outer=True)
    y_gla, o_gla, states = _gla_fwd(z, wa_pad, b_alpha, g_gla)
    if midway is not None:
        g_att = g_att + midway(y_gla)[:1, :1]
    y_att, o_att = _att_fwd(z, rb_pad, g_att)
    ycat = jnp.concatenate([y_gla, y_att], axis=1)
    y = _matmul(ycat, wo, "nn", F32, *TILES["out_proj"], "out_proj", n_outer=True)
    out = _post_fwd(x, y, g_post)
    return out, (x, h, z, o_gla, states, o_att, ycat, y)


def _layer_bwd(dout, saved, wt, wo, g_pre, g_post, wa_pad, b_alpha, g_gla, g_att, rb_pad, on_dwo, on_dwt):
    x, h, z, o_gla, states, o_att, ycat, y = saved
    dy, dg_post = _post_bwd(dout, y, g_post)
    dwo = _matmul(ycat, dy, "tn", BF16, *TILES["out_proj_dw"], "out_proj_dw")
    token = on_dwo(dwo)
    dycat = _matmul(dy, wo, "nt", F32, *TILES["out_proj_dx"], "out_proj_dx", n_outer=True, after=token)
    dq, dk, dv, dgg, dga, dwa, db, dg_gla = _gla_bwd(dycat, o_gla, z, wa_pad, b_alpha, g_gla, states)
    daq, dak, dav, dag, drb, dg_att = _att_bwd(dycat, o_att, z, rb_pad, g_att)
    dz = jnp.concatenate([dq, dk, dv, dgg, daq, dak.astype(BF16), dav.astype(BF16), dag, dga], axis=1)
    dwt = _matmul(dz, h, "tn", BF16, *TILES["in_proj_dw"], "in_proj_dw")
    token = on_dwt(dwt)
    dh = _matmul(dz, wt, "nn", F32, *TILES["in_proj_dx"], "in_proj_dx", n_outer=True, after=token)
    dx, dg_pre = _pre_bwd(dh, x, g_pre, dout)
    small = (dg_pre[0], dg_post[0], dwa[:GLA_RANK], db[0], dg_gla[0], dg_att[0], drb[:, 0, :N_REL])
    return dx, small


def kernel(x, w_in, w_out, g_pre, g_post, w_alpha, b_alpha, g_gla, g_att, rel_bias, loss_target, m_w_in, m_w_out, m_g_pre, m_g_post, m_w_alpha, m_b_alpha, m_g_gla, m_g_att, m_rel_bias, v_w_in, v_w_out, v_g_pre, v_g_post, v_w_alpha, v_b_alpha, v_g_gla, v_g_att, v_rel_bias):
    nl, d, cols = w_in.shape
    rows = w_out.shape[1]
    s = x.shape[1]
    x0 = x.reshape(s, d)
    tgt = loss_target.reshape(s, d)

    cols_first = lambda a: jnp.transpose(a, (2, 0, 1))
    w_c = cols_first(w_in)
    slab = _slab_rows(rows, cols)
    is_out = lax.broadcasted_iota(jnp.int32, (slab, d), 0) < rows

    def shard(l, zero=0.0):
        top = jnp.pad((w_out[l] + zero).astype(BF16), ((0, slab - rows), (0, 0)))
        rest = jnp.pad((w_c[:, l] + zero).astype(BF16), ((rows, slab - rows - cols), (0, 0)))
        return jnp.where(is_out, top, rest)

    first_fetch = _exchange_start(shard(0), x, GATHER_PEERS, False, "gather_start_0")
    began = first_fetch[4][0, 0]
    shards = [None] + [shard(l, began) for l in range(1, nl)]
    alpha = _pack([w_alpha]) + began
    wa_g = _share(alpha, "gather_alpha")
    wa_cols = w_alpha.shape[2]
    wa_full = wa_g.reshape(N_DEV, -1)[:, :nl * GLA_RANK * wa_cols].reshape(N_DEV, nl, GLA_RANK, wa_cols)
    wa_full = jnp.transpose(wa_full, (1, 2, 0, 3)).reshape(nl, GLA_RANK, GLA_KW)
    wa_pad = jnp.pad(wa_full, ((0, 0), (0, LANE - GLA_RANK), (0, 0)))
    rb_pad = jnp.pad(rel_bias, ((0, 0), (0, 0), (0, 3 * LANE - N_REL)))

    def layer_args(l, follows=None):
        gp = g_pre[l:l + 1] if follows is None else g_pre[l:l + 1] + follows[:1, :1]
        return (wts[l], wos[l], gp, g_post[l:l + 1], wa_pad[l], b_alpha[l:l + 1], g_gla[l:l + 1],
                g_att[l:l + 1], rb_pad[l])

    my = _slot(_position())

    def fetch(l, after):
        return _exchange_start(shards[l], after, GATHER_PEERS, False, f"gather_start_{l}")

    def relay(l, first_hop, after):
        own[l], land = _exchange_wait(first_hop[:4], after, GATHER_PEERS, False, f"gather_wait_{l}")
        return _relay_start(land, f"relay_start_{l}")

    def midway(l, y):
        flight["relay"] = relay(l + 1, flight["fetch"], y)
        if l + 2 >= nl:
            return flight["relay"][3]
        flight["fetch"] = fetch(l + 2, flight["relay"][2])
        return flight["fetch"][4]

    act, saved, wts, wos, flight, own = x0, [], [], [], {}, [None] * nl
    prepared = (wa_pad[0, :1, :1] + sum(sh[:1, :1].astype(F32) for sh in shards[1:]))
    flight["relay"] = relay(0, first_fetch, prepared)
    if nl > 1:
        flight["fetch"] = fetch(1, flight["relay"][2])
    for l in range(nl):
        land = _relay_wait(flight["relay"][:3], act, f"relay_wait_{l}")
        land = lax.dynamic_update_slice_in_dim(land, own[l][None], my, 0)
        wt_l, wo_l = _aligned_weight(land, rows, cols)
        wts.append(wt_l)
        wos.append(wo_l)
        act, sv = _layer_fwd(act, *layer_args(l, follows=first_fetch[4] if l == 0 else None),
                             midway=functools.partial(midway, l) if l + 1 < nl else None)
        saved.append(sv)
    dout, sq = _loss_head(act, tgt)
    loss = lax.psum(sq[0, 0] * (0.5 / d), ("x", "y", "c"))

    smalls, pending_out, pending_in = [None] * nl, [None] * nl, [None] * nl

    def send_out(l, dwo):
        pending_out[l] = _exchange_start(dwo.reshape(N_DEV, rows, d), dwo[:1, :1], ALL_PEERS, True,
                                         f"scatter_out_start_{l}")
        return pending_out[l][4]

    def send_in(l, dwt):
        pending_in[l] = _exchange_start(_partial_slabs(dwt, cols), dwt, ALL_PEERS, True, f"scatter_in_start_{l}")
        return pending_in[l][4]

    for l in reversed(range(nl)):
        dout, smalls[l] = _layer_bwd(dout, saved[l], *layer_args(l), on_dwo=functools.partial(send_out, l),
                                     on_dwt=functools.partial(send_in, l))
    grad_x = dout.reshape(x.shape)

    def landed(started, after, name):
        partial, land = _exchange_wait(started[:4], after, ALL_PEERS, True, name)
        return lax.dynamic_update_slice_in_dim(land, lax.dynamic_slice_in_dim(partial, my, 1, 0), my, 0)

    parts_out = [landed(pending_out[l], dout, f"scatter_out_wait_{l}") for l in range(nl)]
    g_w_out, d_w_out, m2_w_out, v2_w_out = _adam_sharded(parts_out, 0, w_out, m_w_out, v_w_out, "adam_w_out")
    names = 7
    small_stacked = [jnp.stack([smalls[l][i] for l in range(nl)]) for i in range(names)]
    shapes = [a.shape for a in small_stacked]
    gathered = _share(_pack(small_stacked), "gather_small_grads", after=d_w_out)
    g_pre_g, g_post_g, wa_g_full, b_g, gla_g, att_g, rb_g = _unpack(_sum_slots(gathered), shapes)
    wa_g_mine = lax.dynamic_slice_in_dim(wa_g_full, my * wa_cols, wa_cols, axis=2)
    grads = [g_pre_g, g_post_g, wa_g_mine, b_g, gla_g, att_g, rb_g]
    ws = [g_pre, g_post, w_alpha, b_alpha, g_gla, g_att, rel_bias]
    ms = [m_g_pre, m_g_post, m_w_alpha, m_b_alpha, m_g_gla, m_g_att, m_rel_bias]
    vs = [v_g_pre, v_g_post, v_w_alpha, v_b_alpha, v_g_gla, v_g_att, v_rel_bias]
    shapes2 = [a.shape for a in ws]
    d_p, m2_p, v2_p = _adam_small(_pack(ws), _pack(grads), _pack(ms), _pack(vs))
    d_s, m2_s, v2_s = _unpack(d_p, shapes2), _unpack(m2_p, shapes2), _unpack(v2_p, shapes2)

    parts_in = [landed(pending_in[l], d_p, f"scatter_in_wait_{l}") for l in range(nl)]
    g_w_in, d_w_in, m2_w_in, v2_w_in = [
        jnp.transpose(a, (1, 2, 0))
        for a in _adam_columns(parts_in, 0, w_c, cols_first(m_w_in), cols_first(v_w_in))]

    def ordered(big_in, big_out, small):
        return [big_in, big_out] + list(small)

    return (loss, grad_x,
            *ordered(g_w_in, g_w_out, grads),
            *ordered(d_w_in, d_w_out, d_s),
            *ordered(m2_w_in, m2_w_out, m2_s),
            *ordered(v2_w_in, v2_w_out, v2_s))
```

```python
import functools

import jax
import jax.numpy as jnp
from jax import lax
from jax.experimental import pallas as pl
from jax.experimental.pallas import tpu as pltpu

F32 = jnp.float32
BF16 = jnp.bfloat16
MESH = pl.DeviceIdType.MESH
ANY = pl.BlockSpec(memory_space=pl.ANY)

CHUNK = 64
GLA_HEADS = 4
GLA_DK = 128
GLA_DV = 256
GLA_KW = GLA_HEADS * GLA_DK
D_GLA = GLA_HEADS * GLA_DV
GLA_RANK = 16
GLA_TAU = 16.0
ATT_HEADS = 8
ATT_HD = 128
D_ATT = ATT_HEADS * ATT_HD
LEFT_CHUNKS = 8
REL_CLIP = 128
N_REL = 2 * REL_CLIP + 1
EPS = 1e-6
D_IN = 2 * GLA_KW + 2 * D_GLA + GLA_RANK + 4 * D_ATT
GLA_SCALE = GLA_DK ** -0.5
ATT_SCALE = ATT_HD ** -0.5

ADAM_LR = 0.001
ADAM_B1 = 0.9
ADAM_B2 = 0.999
ADAM_EPS = 1e-08
ADAM_WD = 0.01
ADAM_STEP = 10

N_DEV = 8
LANE = 128
GA_ORIG = 2 * GLA_KW + 2 * D_GLA
OFF_AQ = GA_ORIG
OFF_GA = GA_ORIG + 4 * D_ATT
D_ZP = OFF_GA + LANE
QB = 2 * CHUNK
ATT_UNROLL = 8
WIN = (LEFT_CHUNKS + 2) * CHUNK
ET_ROWS = WIN + LEFT_CHUNKS * CHUNK
NEG = -1e30
VMEM_LIMIT = 48 * 1024 * 1024


def _cparams(sem):
    return pltpu.CompilerParams(dimension_semantics=sem, vmem_limit_bytes=VMEM_LIMIT)


def _dot(a, b):
    return jnp.dot(a, b, preferred_element_type=F32)


def _dot_nt(a, b):
    return lax.dot_general(a, b, (((1,), (1,)), ((), ())), preferred_element_type=F32)


def _dot_tn(a, b):
    return lax.dot_general(a, b, (((0,), (0,)), ((), ())), preferred_element_type=F32)


def _dot01(t, x, left=True):
    if not left:
        t, x = x, t
    hi = x.astype(BF16)
    r = x - hi.astype(F32)
    mid = r.astype(BF16)
    lo = (r - mid.astype(F32)).astype(BF16)
    if left:
        return _dot(t, hi) + _dot(t, mid) + _dot(t, lo)
    return _dot(hi, t) + _dot(mid, t) + _dot(lo, t)


def _sigmoid(x):
    return 1.0 / (1.0 + jnp.exp(-x))


def _log_sigmoid(x):
    return jnp.minimum(x, 0.0) - jnp.log(1.0 + jnp.exp(-jnp.abs(x)))


TILES = {
    "in_proj": (512, D_ZP // 3, None),
    "in_proj_dx": (512, 512, None),
    "in_proj_dw": (D_ZP // 3, 512, None),
    "out_proj": (512, 1024, None),
    "out_proj_dx": (512, 1024, None),
    "out_proj_dw": (1024, 1024, None),
}


def _matmul(a, b, mode, out_dtype, tm, tn, tk, name, n_outer=False, after=None):
    if mode == "nn":
        (m, k), n = a.shape, b.shape[1]
    elif mode == "nt":
        (m, k), n = a.shape, b.shape[0]
    else:
        (k, m), n = a.shape, b.shape[1]
    tm, tn, tk = min(tm, m), min(tn, n), k if tk is None else min(tk, k)
    assert m % tm == 0 and n % tn == 0 and k % tk == 0, (name, m, n, k)
    nk = k // tk
    dot = {"nn": _dot, "nt": _dot_nt, "tn": _dot_tn}[mode]

    follows = [] if after is None else [after]

    def body_whole_k(a_ref, b_ref, *rest):
        o_ref = rest[-1]
        o_ref[...] = dot(a_ref[...], b_ref[...]).astype(out_dtype)

    def body(a_ref, b_ref, *rest):
        o_ref, acc_ref = rest[-2:]
        kk = pl.program_id(2)

        @pl.when(kk == 0)
        def _():
            acc_ref[...] = jnp.zeros_like(acc_ref)

        acc_ref[...] += dot(a_ref[...], b_ref[...])

        @pl.when(kk == nk - 1)
        def _():
            o_ref[...] = acc_ref[...].astype(out_dtype)

    def at(index):
        return (lambda j, i, kk: index(i, j, kk)) if n_outer else index

    if mode == "tn":
        a_spec = pl.BlockSpec((tk, tm), at(lambda i, j, kk: (kk, i)))
    else:
        a_spec = pl.BlockSpec((tm, tk), at(lambda i, j, kk: (i, kk)))
    if mode == "nt":
        b_spec = pl.BlockSpec((tn, tk), at(lambda i, j, kk: (j, kk)))
    else:
        b_spec = pl.BlockSpec((tk, tn), at(lambda i, j, kk: (kk, j)))
    return pl.pallas_call(
        body_whole_k if nk == 1 else body, name=name,
        grid=(n // tn, m // tm, nk) if n_outer else (m // tm, n // tn, nk),
        in_specs=[a_spec, b_spec] + [ANY] * len(follows),
        out_specs=pl.BlockSpec((tm, tn), at(lambda i, j, kk: (i, j))),
        out_shape=jax.ShapeDtypeStruct((m, n), out_dtype),
        scratch_shapes=[] if nk == 1 else [pltpu.VMEM((tm, tn), F32)],
        compiler_params=_cparams(("parallel", "parallel", "arbitrary")),
    )(a, b, *follows)


ROWS = 256


def _rms_fwd(x, g):
    s, d = x.shape

    def body(x_ref, g_ref, h_ref):
        xv = x_ref[...]
        r = lax.rsqrt(jnp.mean(xv * xv, axis=-1, keepdims=True) + EPS)
        h_ref[...] = (xv * r * g_ref[...]).astype(BF16)

    return pl.pallas_call(
        body, name="rms_fwd", grid=(s // ROWS,),
        in_specs=[pl.BlockSpec((ROWS, d), lambda i: (i, 0)), pl.BlockSpec((1, d), lambda i: (0, 0))],
        out_specs=pl.BlockSpec((ROWS, d), lambda i: (i, 0)),
        out_shape=jax.ShapeDtypeStruct((s, d), BF16),
        compiler_params=_cparams(("parallel",)),
    )(x, g)


def _post_fwd(x, y, g):
    s, d = x.shape

    def body(x_ref, y_ref, g_ref, o_ref):
        yv = y_ref[...]
        r = lax.rsqrt(jnp.mean(yv * yv, axis=-1, keepdims=True) + EPS)
        o_ref[...] = x_ref[...] + yv * r * g_ref[...]

    row = pl.BlockSpec((ROWS, d), lambda i: (i, 0))
    return pl.pallas_call(
        body, name="post_fwd", grid=(s // ROWS,),
        in_specs=[row, row, pl.BlockSpec((1, d), lambda i: (0, 0))],
        out_specs=row,
        out_shape=jax.ShapeDtypeStruct((s, d), F32),
        compiler_params=_cparams(("parallel",)),
    )(x, y, g)


def _loss_head(out, tgt):
    s, d = out.shape

    def body(o_ref, t_ref, dout_ref, sum_ref):
        @pl.when(pl.program_id(0) == 0)
        def _():
            sum_ref[...] = jnp.zeros_like(sum_ref)

        e = o_ref[...] - t_ref[...]
        dout_ref[...] = e * (1.0 / d)
        sum_ref[...] += jnp.sum(jnp.sum(e * e, axis=1, keepdims=True), axis=0, keepdims=True)

    row = pl.BlockSpec((ROWS, d), lambda i: (i, 0))
    return pl.pallas_call(
        body, name="loss_head", grid=(s // ROWS,),
        in_specs=[row, row],
        out_specs=[row, pl.BlockSpec((1, 1), lambda i: (0, 0))],
        out_shape=[jax.ShapeDtypeStruct((s, d), F32), jax.ShapeDtypeStruct((1, 1), F32)],
        compiler_params=_cparams(("arbitrary",)),
    )(out, tgt)


def _post_bwd(dout, y, g):
    s, d = y.shape

    def body(do_ref, y_ref, g_ref, dy_ref, dg_ref):
        @pl.when(pl.program_id(0) == 0)
        def _():
            dg_ref[...] = jnp.zeros_like(dg_ref)

        yv = y_ref[...]
        dv = do_ref[...]
        r = lax.rsqrt(jnp.mean(yv * yv, axis=-1, keepdims=True) + EPS)
        dg_ref[...] += jnp.sum(dv * yv * r, axis=0, keepdims=True)
        w = dv * g_ref[...]
        dy = r * (w - yv * (r * r) * jnp.mean(w * yv, axis=-1, keepdims=True))
        dy_ref[...] = dy.astype(BF16)

    row = pl.BlockSpec((ROWS, d), lambda i: (i, 0))
    vec = pl.BlockSpec((1, d), lambda i: (0, 0))
    return pl.pallas_call(
        body, name="post_bwd", grid=(s // ROWS,),
        in_specs=[row, row, vec],
        out_specs=[row, vec],
        out_shape=[jax.ShapeDtypeStruct((s, d), BF16), jax.ShapeDtypeStruct((1, d), F32)],
        compiler_params=_cparams(("arbitrary",)),
    )(dout, y, g)


def _pre_bwd(dh, x, g, dout):
    s, d = x.shape

    def body(dh_ref, x_ref, g_ref, do_ref, dx_ref, dg_ref):
        @pl.when(pl.program_id(0) == 0)
        def _():
            dg_ref[...] = jnp.zeros_like(dg_ref)

        xv = x_ref[...]
        dv = dh_ref[...]
        r = lax.rsqrt(jnp.mean(xv * xv, axis=-1, keepdims=True) + EPS)
        dg_ref[...] += jnp.sum(dv * xv * r, axis=0, keepdims=True)
        w = dv * g_ref[...]
        dx_ref[...] = do_ref[...] + r * (w - xv * (r * r) * jnp.mean(w * xv, axis=-1, keepdims=True))

    row = pl.BlockSpec((ROWS, d), lambda i: (i, 0))
    vec = pl.BlockSpec((1, d), lambda i: (0, 0))
    return pl.pallas_call(
        body, name="pre_bwd", grid=(s // ROWS,),
        in_specs=[row, row, vec, row],
        out_specs=[row, vec],
        out_shape=[jax.ShapeDtypeStruct((s, d), F32), jax.ShapeDtypeStruct((1, d), F32)],
        compiler_params=_cparams(("arbitrary",)),
    )(dh, x, g, dout)


GLA_STEP = 4
GLA_ROWS = GLA_STEP * CHUNK
GLA_CHUNKS = [slice(c * CHUNK, (c + 1) * CHUNK) for c in range(GLA_STEP)]


def _chunk_triangles():
    ri = lax.broadcasted_iota(jnp.int32, (GLA_ROWS, GLA_ROWS), 0)
    ci = lax.broadcasted_iota(jnp.int32, (GLA_ROWS, GLA_ROWS), 1)
    same = (ri // CHUNK) == (ci // CHUNK)
    return (jnp.where(same & (ri >= ci), 1.0, 0.0).astype(BF16), jnp.where(same & (ci >= ri), 1.0, 0.0).astype(BF16))


def _per_chunk(fn, like):
    row = lax.broadcasted_iota(jnp.int32, like.shape, 0)
    return [fn((row >= c * CHUNK) & (row < (c + 1) * CHUNK)) for c in range(GLA_STEP)]


def _spread(per_chunk, like):
    row = lax.broadcasted_iota(jnp.int32, like.shape, 0)
    out = per_chunk[-1]
    for c in reversed(range(GLA_STEP - 1)):
        out = jnp.where(row < (c + 1) * CHUNK, per_chunk[c], out)
    return out


def _gla_gate(ga_b, wa_b, b_ref, tri):
    pre = _dot(ga_b, wa_b) + b_ref[...]
    la = _log_sigmoid(pre) * (1.0 / GLA_TAU)
    cum = _dot01(tri, la)
    row = lax.broadcasted_iota(jnp.int32, cum.shape, 0)
    cends = [jnp.sum(jnp.where(row == (c + 1) * CHUNK - 1, cum, 0.0), axis=0, keepdims=True)
             for c in range(GLA_STEP)]
    return pre, cum, cends


def _heads(width):
    return [slice(h * width, (h + 1) * width) for h in range(GLA_HEADS)]


def _z_specs_gla(rev=None):
    idx = (lambda n: n) if rev is None else rev
    return [
        pl.BlockSpec((GLA_ROWS, GLA_KW), lambda n: (idx(n), 0)),
        pl.BlockSpec((GLA_ROWS, GLA_KW), lambda n: (idx(n), 1)),
        pl.BlockSpec((GLA_ROWS, D_GLA), lambda n: (idx(n), 1)),
        pl.BlockSpec((GLA_ROWS, D_GLA), lambda n: (idx(n), 2)),
        pl.BlockSpec((GLA_ROWS, LANE), lambda n: (idx(n), OFF_GA // LANE)),
    ]


def _gla_fwd(z, wa_pad, b_alpha, g_gla):
    s = z.shape[0]
    nchunk = s // CHUNK

    def body(q_ref, k_ref, v_ref, gg_ref, ga_ref, wa_ref, b_ref, g_ref, y_ref, o_ref, st_ref, state):
        @pl.when(pl.program_id(0) == 0)
        def _():
            state[...] = jnp.zeros_like(state)

        ga_b = ga_ref[...].astype(BF16)
        tri, _ = _chunk_triangles()
        nh = range(GLA_HEADS)
        keys, vals = _heads(GLA_DK), _heads(GLA_DV)
        _, cum, cends = _gla_gate(ga_b, wa_ref[...].astype(BF16), b_ref, tri)
        kd_b = (k_ref[...] * jnp.exp(_spread(cends, cum) - cum)).astype(BF16)
        qs = (q_ref[...] * GLA_SCALE).astype(BF16)
        v_b = v_ref[...].astype(BF16)
        uts = [[_dot_tn(v_b[rs, vals[h]], kd_b[rs, keys[h]]) for h in nh] for rs in GLA_CHUNKS]
        sts, prev = [], [state[h] for h in nh]
        for c in range(GLA_STEP):
            a = jnp.exp(cends[c])
            prev = [prev[h] * a[:, keys[h]] + uts[c][h] for h in nh]
            sts.append(prev)
        for h in nh:
            state[h] = prev[h]
            for c in range(GLA_STEP):
                st_ref[c, h] = sts[c][h]
        outs = [[_dot_nt(qs[rs, keys[h]], sts[c][h].astype(BF16)) for h in nh] for c, rs in enumerate(GLA_CHUNKS)]
        for h in nh:
            o, vs = jnp.concatenate([outs[c][h] for c in range(GLA_STEP)], axis=0), vals[h]
            o_ref[:, vs] = o
            r = lax.rsqrt(jnp.mean(o * o, axis=-1, keepdims=True) + EPS)
            gg = gg_ref[:, vs]
            y_ref[:, vs] = (o * r * g_ref[:, vs] * (gg * _sigmoid(gg))).astype(BF16)

    full = lambda shape: pl.BlockSpec(shape, lambda n: tuple(0 for _ in shape))
    wide = pl.BlockSpec((GLA_ROWS, D_GLA), lambda n: (n, 0))
    return pl.pallas_call(
        body, name="gla_fwd", grid=(nchunk // GLA_STEP,),
        in_specs=_z_specs_gla() + [full((LANE, GLA_KW)), full((1, GLA_KW)), full((1, D_GLA))],
        out_specs=[wide, wide, pl.BlockSpec((GLA_STEP, GLA_HEADS, GLA_DV, GLA_DK), lambda n: (n, 0, 0, 0))],
        out_shape=[jax.ShapeDtypeStruct((s, D_GLA), BF16), jax.ShapeDtypeStruct((s, D_GLA), F32),
                   jax.ShapeDtypeStruct((nchunk, GLA_HEADS, GLA_DV, GLA_DK), F32)],
        scratch_shapes=[pltpu.VMEM((GLA_HEADS, GLA_DV, GLA_DK), F32)],
        compiler_params=_cparams(("arbitrary",)),
    )(z, z, z, z, z, wa_pad, b_alpha, g_gla)


def _gla_bwd(dyc, o_gla, z, wa_pad, b_alpha, g_gla, states):
    s = z.shape[0]
    nsteps = s // GLA_ROWS
    rev = lambda n: nsteps - 1 - n

    def body(dy_ref, o_ref, q_ref, k_ref, v_ref, gg_ref, ga_ref, wa_ref, b_ref, g_ref, st_ref, stp_ref,
             dq_ref, dk_ref, dv_ref, dgg_ref, dga_ref, dwa_ref, db_ref, dg_ref, carry):
        step = pl.program_id(0)

        @pl.when(step == 0)
        def _():
            carry[...] = jnp.zeros_like(carry)
            dwa_ref[...] = jnp.zeros_like(dwa_ref)
            db_ref[...] = jnp.zeros_like(db_ref)
            dg_ref[...] = jnp.zeros_like(dg_ref)

        has_prev = (step < nsteps - 1).astype(F32)
        ga_b = ga_ref[...].astype(BF16)
        tri, tri_up = _chunk_triangles()
        nh, nc = range(GLA_HEADS), range(GLA_STEP)
        keys, vals = _heads(GLA_DK), _heads(GLA_DV)
        wa_b = wa_ref[...].astype(BF16)
        pre, cum, cends = _gla_gate(ga_b, wa_b, b_ref, tri)
        e = jnp.exp(_spread(cends, cum) - cum)
        a = [jnp.exp(cends[c]) for c in nc]
        kf = k_ref[...]
        kd_b = (kf * e).astype(BF16)
        v_b = v_ref[...].astype(BF16)
        qs = (q_ref[...] * GLA_SCALE).astype(BF16)
        do_b = []
        for h in nh:
            vs = vals[h]
            o = o_ref[:, vs]
            gg = gg_ref[:, vs]
            g = g_ref[:, vs]
            dy = dy_ref[:, vs]
            r = lax.rsqrt(jnp.mean(o * o, axis=-1, keepdims=True) + EPS)
            sg = _sigmoid(gg)
            dogn = dy * (gg * sg)
            dgg_ref[:, vs] = (dy * (o * r * g) * (sg * (1.0 + gg * (1.0 - sg)))).astype(BF16)
            dg_ref[:, vs] += jnp.sum(dogn * o * r, axis=0, keepdims=True)
            w = dogn * g
            do_b.append((r * (w - o * (r * r) * jnp.mean(w * o, axis=-1, keepdims=True))).astype(BF16))
        dqs = [jnp.concatenate([_dot(do_b[h][rs], st_ref[c, h].astype(BF16)) for c, rs in enumerate(GLA_CHUNKS)],
                               axis=0) for h in nh]
        dq_ref[...] = (jnp.concatenate(dqs, axis=1) * GLA_SCALE).astype(BF16)
        own = [[_dot_tn(do_b[h][rs], qs[rs, keys[h]]) for h in nh] for rs in GLA_CHUNKS]
        gts, later = [None] * GLA_STEP, [carry[h] for h in nh]
        for c in reversed(nc):
            gts[c] = [own[c][h] + later[h] for h in nh]
            later = [gts[c][h] * a[c][:, keys[h]] for h in nh]
        for h in nh:
            carry[h] = later[h]
        gt_b = [[gts[c][h].astype(BF16) for h in nh] for c in nc]
        dkd = jnp.concatenate([jnp.concatenate([_dot(v_b[rs, vals[h]], gt_b[c][h]) for h in nh], axis=1)
                               for c, rs in enumerate(GLA_CHUNKS)], axis=0)
        dvs = [[_dot_nt(kd_b[rs, keys[h]], gt_b[c][h]) for h in nh] for c, rs in enumerate(GLA_CHUNKS)]
        before = lambda c, h: st_ref[c - 1, h] if c > 0 else stp_ref[0, h] * has_prev
        da = [jnp.concatenate([jnp.sum(gts[c][h] * before(c, h), axis=0, keepdims=True) for h in nh], axis=1)
              for c in nc]
        for h in nh:
            dv_ref[:, vals[h]] = jnp.concatenate([dvs[c][h] for c in nc], axis=0).astype(BF16)
        dk_ref[...] = (dkd * e).astype(BF16)
        dd = dkd * kf * e
        dsum = _per_chunk(lambda mine: jnp.sum(jnp.where(mine, dd, 0.0), axis=0, keepdims=True), dd)
        dcend = _spread([dsum[c] + da[c] * a[c] for c in nc], dd)
        dla = dcend - _dot01(tri_up, dd)
        dpre = dla * (1.0 / GLA_TAU) * (1.0 - _sigmoid(pre))
        dpre_b = dpre.astype(BF16)
        dga_ref[...] = _dot_nt(dpre_b, wa_b).astype(BF16)
        dwa_ref[...] += _dot_tn(ga_b, dpre_b)
        db_ref[...] += jnp.sum(dpre, axis=0, keepdims=True)

    full = lambda shape: pl.BlockSpec(shape, lambda n: tuple(0 for _ in shape))
    wide = pl.BlockSpec((GLA_ROWS, D_GLA), lambda n: (rev(n), 0))
    keyw = pl.BlockSpec((GLA_ROWS, GLA_KW), lambda n: (rev(n), 0))
    st_spec = pl.BlockSpec((GLA_STEP, GLA_HEADS, GLA_DV, GLA_DK), lambda n: (rev(n), 0, 0, 0))
    stp_spec = pl.BlockSpec((1, GLA_HEADS, GLA_DV, GLA_DK),
                            lambda n: (jnp.maximum(GLA_STEP * rev(n) - 1, 0), 0, 0, 0))
    return pl.pallas_call(
        body, name="gla_bwd", grid=(nsteps,),
        in_specs=[wide, wide] + _z_specs_gla(rev)
        + [full((LANE, GLA_KW)), full((1, GLA_KW)), full((1, D_GLA)), st_spec, stp_spec],
        out_specs=[keyw, keyw, wide, wide, pl.BlockSpec((GLA_ROWS, LANE), lambda n: (rev(n), 0)),
                   full((LANE, GLA_KW)), full((1, GLA_KW)), full((1, D_GLA))],
        out_shape=[jax.ShapeDtypeStruct((s, GLA_KW), BF16), jax.ShapeDtypeStruct((s, GLA_KW), BF16),
                   jax.ShapeDtypeStruct((s, D_GLA), BF16), jax.ShapeDtypeStruct((s, D_GLA), BF16),
                   jax.ShapeDtypeStruct((s, LANE), BF16),
                   jax.ShapeDtypeStruct((LANE, GLA_KW), F32), jax.ShapeDtypeStruct((1, GLA_KW), F32),
                   jax.ShapeDtypeStruct((1, D_GLA), F32)],
        scratch_shapes=[pltpu.VMEM((GLA_HEADS, GLA_DV, GLA_DK), F32)],
        compiler_params=_cparams(("arbitrary",)),
    )(dyc, o_gla, z, z, z, z, z, wa_pad, b_alpha, g_gla, states, states)


def _build_bias_table(rb_row, et_ref):
    far = jnp.broadcast_to(rb_row[:, 2 * REL_CLIP:2 * REL_CLIP + 1], (1, LANE))
    near_hi = rb_row[:, REL_CLIP:2 * REL_CLIP]
    near_lo = rb_row[:, 0:REL_CLIP]
    past = jnp.broadcast_to(rb_row[:, 0:1], (1, LANE))
    seg = [far, far, far, far, near_hi, near_lo] + [past] * (ET_ROWS // LANE - 5)
    ri = lax.broadcasted_iota(jnp.int32, (LANE, LANE), 0)
    ci = lax.broadcasted_iota(jnp.int32, (LANE, LANE), 1)
    for kb in range(ET_ROWS // LANE):
        wmat = jnp.where(ri + ci < LANE, seg[kb], seg[kb + 1])
        blk = pltpu.roll(wmat, 0, 1, stride=1, stride_axis=0)
        lag = LEFT_CHUNKS + ci // CHUNK - (2 * kb + ri // CHUNK)
        et_ref[kb * LANE:(kb + 1) * LANE, :] = jnp.where((lag >= 0) & (lag <= LEFT_CHUNKS), blk, NEG)


def _reduce_bias_table(det_ref):
    lane = lax.broadcasted_iota(jnp.int32, (1, LANE), 1)
    ri = lax.broadcasted_iota(jnp.int32, (LANE, LANE), 0)
    ci = lax.broadcasted_iota(jnp.int32, (LANE, LANE), 1)
    flip = jnp.where(ri + ci == LANE - 1, 1.0, 0.0).astype(BF16)
    segs = jnp.zeros((8, LANE), F32)
    seg_row = lax.broadcasted_iota(jnp.int32, (8, LANE), 0)
    prev_minus = jnp.zeros((1, LANE), F32)
    for kb in range(6):
        rolled = pltpu.roll(_dot01(det_ref[kb * LANE:(kb + 1) * LANE, :], flip, left=False), 0, 1,
                            stride=1, stride_axis=0)
        plus = jnp.sum(jnp.where(ci >= ri, rolled, 0.0), axis=0, keepdims=True)
        minus = jnp.sum(jnp.where(ci < ri, rolled, 0.0), axis=0, keepdims=True)
        segs = segs + jnp.where(seg_row == kb, plus + prev_minus, 0.0)
        prev_minus = minus
    segs = _dot01(segs, flip, left=False)
    pick = lambda kb: jnp.sum(jnp.where(seg_row == kb, segs, 0.0), axis=0, keepdims=True)
    far = jnp.sum(pick(0) + pick(1) + pick(2) + pick(3), axis=1, keepdims=True)
    last = jnp.where(lane == 0, far, 0.0)
    return jnp.concatenate([pick(5), pick(4), last], axis=1)


def _att_window(b):
    c0 = 2 * b
    kstart = pl.multiple_of(jnp.maximum(c0 - LEFT_CHUNKS, 0) * CHUNK, CHUNK)
    eoff = pl.multiple_of(jnp.maximum(LEFT_CHUNKS - c0, 0) * CHUNK, CHUNK)
    return kstart, eoff


def _att_probs(q_b, kw_b, et):
    st = _dot_nt(kw_b, q_b) * ATT_SCALE + et
    m = jnp.max(st, axis=0, keepdims=True)
    ex = jnp.exp(st - m)
    return ex * (1.0 / jnp.sum(ex, axis=0, keepdims=True))


def _att_fwd(z, rb_pad, g_att):
    s = z.shape[0]
    nblk = s // QB
    c_aq, c_ak, c_av, c_ag = [(OFF_AQ + i * D_ATT) // ATT_HD for i in range(4)]

    def body(q_ref, k_ref, v_ref, ag_ref, rb_ref, g_ref, y_ref, o_ref, et_ref, kb_ref, vb_ref):
        h = pl.program_id(0)
        b = pl.program_id(1)

        @pl.when(b == 0)
        def _():
            _build_bias_table(rb_ref[pl.ds(h, 1), :], et_ref)
            kb_ref[...] = k_ref[...].astype(BF16)
            vb_ref[...] = v_ref[...].astype(BF16)

        for j in range(ATT_UNROLL):
            rs = slice(j * QB, (j + 1) * QB)
            kstart, eoff = _att_window(b * ATT_UNROLL + j)
            q_b = q_ref[rs, :].astype(BF16)
            kw_b = kb_ref[pl.ds(kstart, WIN), :]
            vw_b = vb_ref[pl.ds(kstart, WIN), :]
            pt = _att_probs(q_b, kw_b, et_ref[pl.ds(eoff, WIN), :])
            o = _dot_tn(pt.astype(BF16), vw_b)
            o_ref[rs, :] = o
            r = lax.rsqrt(jnp.mean(o * o, axis=-1, keepdims=True) + EPS)
            ag = ag_ref[rs, :]
            y_ref[rs, :] = (o * r * g_ref[...] * (ag * _sigmoid(ag))).astype(BF16)

    blk = lambda col: pl.BlockSpec((ATT_UNROLL * QB, ATT_HD), lambda h, b: (b, col + h))
    seq = lambda col: pl.BlockSpec((s, ATT_HD), lambda h, b: (0, col + h))
    out_blk = pl.BlockSpec((ATT_UNROLL * QB, ATT_HD), lambda h, b: (b, h))
    return pl.pallas_call(
        body, name="att_fwd", grid=(ATT_HEADS, nblk // ATT_UNROLL),
        in_specs=[blk(c_aq), seq(c_ak), seq(c_av), blk(c_ag),
                  pl.BlockSpec((ATT_HEADS, 3 * LANE), lambda h, b: (0, 0)),
                  pl.BlockSpec((1, ATT_HD), lambda h, b: (0, h))],
        out_specs=[out_blk, out_blk],
        out_shape=[jax.ShapeDtypeStruct((s, D_ATT), BF16), jax.ShapeDtypeStruct((s, D_ATT), F32)],
        scratch_shapes=[pltpu.VMEM((ET_ROWS, LANE), F32), pltpu.VMEM((s, ATT_HD), BF16),
                        pltpu.VMEM((s, ATT_HD), BF16)],
        compiler_params=_cparams(("arbitrary", "arbitrary")),
    )(z, z, z, z, rb_pad, g_att)


def _att_bwd(dyc, o_att, z, rb_pad, g_att):
    s = z.shape[0]
    nblk = s // QB
    c_aq, c_ak, c_av, c_ag = [(OFF_AQ + i * D_ATT) // ATT_HD for i in range(4)]
    c_dy = D_GLA // ATT_HD

    def body(dy_ref, o_ref, q_ref, k_ref, v_ref, ag_ref, rb_ref, g_ref,
             dq_ref, dk_ref, dv_ref, dag_ref, drb_ref, dg_ref, et_ref, det_ref, kb_ref, vb_ref):
        h = pl.program_id(0)
        b = pl.program_id(1)

        @pl.when(b == 0)
        def _():
            _build_bias_table(rb_ref[pl.ds(h, 1), :], et_ref)
            kb_ref[...] = k_ref[...].astype(BF16)
            vb_ref[...] = v_ref[...].astype(BF16)
            det_ref[...] = jnp.zeros_like(det_ref)
            dk_ref[...] = jnp.zeros_like(dk_ref)
            dv_ref[...] = jnp.zeros_like(dv_ref)
            dg_ref[...] = jnp.zeros_like(dg_ref)

        g = g_ref[...]
        dg = jnp.zeros((1, ATT_HD), F32)
        for j in range(ATT_UNROLL):
            rs = slice(j * QB, (j + 1) * QB)
            kstart, eoff = _att_window(b * ATT_UNROLL + j)
            q_b = q_ref[rs, :].astype(BF16)
            kw_b = kb_ref[pl.ds(kstart, WIN), :]
            vw_b = vb_ref[pl.ds(kstart, WIN), :]
            pt = _att_probs(q_b, kw_b, et_ref[pl.ds(eoff, WIN), :])
            o = o_ref[rs, :]
            ag = ag_ref[rs, :]
            dy = dy_ref[rs, :]
            r = lax.rsqrt(jnp.mean(o * o, axis=-1, keepdims=True) + EPS)
            sg = _sigmoid(ag)
            don = dy * (ag * sg)
            dag_ref[rs, :] = (dy * (o * r * g) * (sg * (1.0 + ag * (1.0 - sg)))).astype(BF16)
            dg = dg + jnp.sum(don * o * r, axis=0, keepdims=True)
            w = don * g
            do_b = (r * (w - o * (r * r) * jnp.mean(w * o, axis=-1, keepdims=True))).astype(BF16)
            pt_b = pt.astype(BF16)
            dpt = _dot_nt(vw_b, do_b)
            dst = pt * (dpt - jnp.sum(dpt * pt, axis=0, keepdims=True))
            det_ref[pl.ds(eoff, WIN), :] += dst
            ds_b = (dst * ATT_SCALE).astype(BF16)
            dq_ref[rs, :] = _dot_tn(ds_b, kw_b).astype(BF16)
            dk_ref[pl.ds(kstart, WIN), :] += _dot(ds_b, q_b)
            dv_ref[pl.ds(kstart, WIN), :] += _dot(pt_b, do_b)
        dg_ref[...] += dg

        @pl.when(b == nblk // ATT_UNROLL - 1)
        def _():
            drb_ref[0] = jnp.broadcast_to(_reduce_bias_table(det_ref), (8, 3 * LANE))

    blk = lambda col: pl.BlockSpec((ATT_UNROLL * QB, ATT_HD), lambda h, b: (b, col + h))
    seq = lambda col: pl.BlockSpec((s, ATT_HD), lambda h, b: (0, col + h))
    out_blk = pl.BlockSpec((ATT_UNROLL * QB, ATT_HD), lambda h, b: (b, h))
    out_seq = pl.BlockSpec((s, ATT_HD), lambda h, b: (0, h))
    return pl.pallas_call(
        body, name="att_bwd", grid=(ATT_HEADS, nblk // ATT_UNROLL),
        in_specs=[blk(c_dy), blk(0), blk(c_aq), seq(c_ak), seq(c_av), blk(c_ag),
                  pl.BlockSpec((ATT_HEADS, 3 * LANE), lambda h, b: (0, 0)),
                  pl.BlockSpec((1, ATT_HD), lambda h, b: (0, h))],
        out_specs=[out_blk, out_seq, out_seq, out_blk,
                   pl.BlockSpec((1, 8, 3 * LANE), lambda h, b: (h, 0, 0)),
                   pl.BlockSpec((1, ATT_HD), lambda h, b: (0, h))],
        out_shape=[jax.ShapeDtypeStruct((s, D_ATT), BF16), jax.ShapeDtypeStruct((s, D_ATT), F32),
                   jax.ShapeDtypeStruct((s, D_ATT), F32), jax.ShapeDtypeStruct((s, D_ATT), BF16),
                   jax.ShapeDtypeStruct((ATT_HEADS, 8, 3 * LANE), F32),
                   jax.ShapeDtypeStruct((1, D_ATT), F32)],
        scratch_shapes=[pltpu.VMEM((ET_ROWS, LANE), F32), pltpu.VMEM((ET_ROWS, LANE), F32),
                        pltpu.VMEM((s, ATT_HD), BF16), pltpu.VMEM((s, ATT_HD), BF16)],
        compiler_params=_cparams(("arbitrary", "arbitrary")),
    )(dyc, o_att, z, z, z, z, rb_pad, g_att)


ADAM_ROWS = 64
ADAM_COL_ROWS = 32


def _adam_math(w, g, m, v):
    m2 = ADAM_B1 * m + (1.0 - ADAM_B1) * g
    v2 = ADAM_B2 * v + (1.0 - ADAM_B2) * (g * g)
    m_hat = m2 / (1.0 - ADAM_B1 ** ADAM_STEP)
    v_hat = v2 / (1.0 - ADAM_B2 ** ADAM_STEP)
    delta = -ADAM_LR * (m_hat / (jnp.sqrt(v_hat) + ADAM_EPS) + ADAM_WD * w)
    return delta, m2, v2


def _adam_sharded(parts, first, w, m, v, name):
    nl, nr, nc = w.shape

    def body(*refs):
        p_refs = refs[:nl]
        w_ref, m_ref, v_ref, g_ref, d_ref, m2_ref, v2_ref = refs[nl:]
        for k in range(nl):
            @pl.when(pl.program_id(0) == k)
            def _(p_ref=p_refs[k]):
                g = p_ref[0].astype(F32)
                for dev in range(1, N_DEV):
                    g = g + p_ref[dev].astype(F32)
                delta, m2, v2 = _adam_math(w_ref[0], g, m_ref[0], v_ref[0])
                g_ref[0] = g
                d_ref[0] = delta
                m2_ref[0] = m2
                v2_ref[0] = v2

    def part_spec(k):
        return pl.BlockSpec((N_DEV, ADAM_ROWS, nc), lambda l, i: (0, first + jnp.where(l == k, i, 0), 0))

    blk = pl.BlockSpec((1, ADAM_ROWS, nc), lambda l, i: (l, i, 0))
    shp = jax.ShapeDtypeStruct(w.shape, F32)
    return pl.pallas_call(
        body, name=name, grid=(nl, pl.cdiv(nr, ADAM_ROWS)),
        in_specs=[part_spec(k) for k in range(nl)] + [blk, blk, blk],
        out_specs=[blk, blk, blk, blk],
        out_shape=[shp, shp, shp, shp],
        compiler_params=_cparams(("arbitrary", "arbitrary")),
    )(*parts, w, m, v)


def _adam_columns(parts, first, w, m, v):
    nc, nl, d = w.shape

    def body(*refs):
        p_refs = refs[:nl]
        w_ref, m_ref, v_ref, g_ref, d_ref, m2_ref, v2_ref = refs[nl:]
        for l in range(nl):
            g = p_refs[l][0].astype(F32)
            for dev in range(1, N_DEV):
                g = g + p_refs[l][dev].astype(F32)
            delta, m2, v2 = _adam_math(w_ref[:, l, :], g, m_ref[:, l, :], v_ref[:, l, :])
            g_ref[:, l, :] = g
            d_ref[:, l, :] = delta
            m2_ref[:, l, :] = m2
            v2_ref[:, l, :] = v2

    blk = pl.BlockSpec((ADAM_COL_ROWS, nl, d), lambda i: (i, 0, 0))
    part = pl.BlockSpec((N_DEV, ADAM_COL_ROWS, d), lambda i: (0, first + i, 0))
    shp = jax.ShapeDtypeStruct(w.shape, F32)
    return pl.pallas_call(
        body, name="adam_w_in", grid=(pl.cdiv(nc, ADAM_COL_ROWS),),
        in_specs=[part] * nl + [blk, blk, blk],
        out_specs=[blk, blk, blk, blk],
        out_shape=[shp, shp, shp, shp],
        compiler_params=_cparams(("parallel",)),
    )(*parts, w, m, v)


def _adam_small(ws, gs, ms, vs):
    n = len(ws)

    def body(*refs):
        w_refs, g_refs, m_refs, v_refs, d_refs, m2_refs, v2_refs = [refs[i * n:(i + 1) * n] for i in range(7)]
        for i in range(n):
            delta, m2, v2 = _adam_math(w_refs[i][...], g_refs[i][...], m_refs[i][...], v_refs[i][...])
            d_refs[i][...] = delta
            m2_refs[i][...] = m2
            v2_refs[i][...] = v2

    shapes = [jax.ShapeDtypeStruct(w.shape, F32) for w in ws]
    out = pl.pallas_call(body, name="adam_small", out_shape=shapes * 3)(*ws, *gs, *ms, *vs)
    return out[:n], out[n:2 * n], out[2 * n:]


def _position():
    return lax.axis_index("x"), lax.axis_index("y"), lax.axis_index("c")


def _slot(p):
    return 4 * p[0] + 2 * p[1] + p[2]


BF16_TILE_ROWS = 16


def _slab_rows(rows, cols):
    return -(-(rows + cols) // BF16_TILE_ROWS) * BF16_TILE_ROWS


RELAYOUT_COLS = 1024
RELAYOUT_CHUNK = 64


def _shard_pieces(dev, rows, cols):
    moved = ((0, GA_ORIG, 0), (GA_ORIG, GA_ORIG + GLA_RANK, OFF_GA - GA_ORIG), (GA_ORIG + GLA_RANK, D_IN, -GLA_RANK))
    c0, c1 = dev * cols, (dev + 1) * cols
    return [(rows + max(c0, lo) - c0, max(c0, lo) + off, min(c1, hi) - max(c0, lo))
            for lo, hi, off in moved if max(c0, lo) < min(c1, hi)]


def _move_rows(src, src_row, dst, dst_row, n):
    assert src_row % 2 == 0 and dst_row % 2 == 0 and n % 2 == 0
    for r in range(0, n // 2, RELAYOUT_CHUNK):
        m = min(RELAYOUT_CHUNK, n // 2 - r)
        dst[dst_row // 2 + r:dst_row // 2 + r + m, :] = src[src_row // 2 + r:src_row // 2 + r + m, :]


def _aligned_weight(land, rows, cols):
    _, slab, d = land.shape
    ct = min(RELAYOUT_COLS, d)

    def body(land_ref, wt_ref, wo_ref):
        dev = pl.program_id(1)
        src = land_ref.bitcast(jnp.uint32)
        dst = wt_ref.bitcast(jnp.uint32)
        wo_ref[...] = land_ref[0:rows, :]

        @pl.when(dev == 0)
        def _():
            dst[D_IN // 2:D_ZP // 2, :] = jnp.zeros(((D_ZP - D_IN) // 2, ct), jnp.uint32)

        for k in range(N_DEV):
            @pl.when(dev == k)
            def _(k=k):
                for at, to, n in _shard_pieces(k, rows, cols):
                    _move_rows(src, at, dst, to, n)

    return pl.pallas_call(
        body, name="aligned_weight", grid=(d // ct, N_DEV),
        in_specs=[pl.BlockSpec((slab, ct), lambda c, dev: (dev, c))],
        out_specs=[pl.BlockSpec((D_ZP, ct), lambda c, dev: (0, c)),
                   pl.BlockSpec((rows, ct), lambda c, dev: (dev, c))],
        out_shape=[jax.ShapeDtypeStruct((D_ZP, d), land.dtype),
                   jax.ShapeDtypeStruct((N_DEV * rows, d), land.dtype)],
        compiler_params=_cparams(("parallel", "arbitrary")),
    )(land.reshape(N_DEV * slab, d))


def _partial_slabs(dwt, cols):
    d = dwt.shape[1]
    slab = _slab_rows(0, cols)
    ct = min(RELAYOUT_COLS, d)

    def body(dwt_ref, out_ref):
        dev = pl.program_id(1)
        src = dwt_ref.bitcast(jnp.uint32)
        dst = out_ref.bitcast(jnp.uint32)
        dst[cols // 2:slab // 2, :] = jnp.zeros(((slab - cols) // 2, ct), jnp.uint32)
        for k in range(N_DEV):
            @pl.when(dev == k)
            def _(k=k):
                for to, at, n in _shard_pieces(k, 0, cols):
                    _move_rows(src, at, dst, to, n)

    return pl.pallas_call(
        body, name="partial_slabs", grid=(d // ct, N_DEV),
        in_specs=[pl.BlockSpec((D_ZP, ct), lambda c, dev: (0, c))],
        out_specs=pl.BlockSpec((slab, ct), lambda c, dev: (dev, c)),
        out_shape=jax.ShapeDtypeStruct((N_DEV * slab, d), dwt.dtype),
        compiler_params=_cparams(("parallel", "arbitrary")),
    )(dwt).reshape(N_DEV, slab, d)


def _peer(pos, k):
    x, y, c = pos
    return (1 - x if k & 4 else x, 1 - y if k & 2 else y, 1 - c if k & 1 else c)


HBM_SPEC = pl.BlockSpec(memory_space=pltpu.HBM)
SEM_SPEC = pl.BlockSpec(memory_space=pltpu.SEMAPHORE)
GATHER_PEERS = (1, 4, 2, 6)
ALL_PEERS = (1, 2, 3, 4, 5, 6, 7)


def _hbm(a):
    return pltpu.with_memory_space_constraint(a, pltpu.HBM)


def _split_copies(src_ref, land_ref, send_sems, recv_sems, ks, per_peer, landed):
    me = _position()
    out = []
    for i, k in enumerate(ks):
        peer = _peer(me, k)
        src = src_ref.at[_slot(peer)] if per_peer else src_ref
        dst = land_ref.at[_slot(peer) if landed else _slot(me)]
        out.append(pltpu.make_async_remote_copy(
            src_ref=src, dst_ref=dst, send_sem=send_sems.at[i], recv_sem=recv_sems.at[i],
            device_id=peer, device_id_type=MESH))
    return out


def _exchange_start(src, after, ks, per_peer, name):
    slab = src.shape[1:] if per_peer else src.shape
    land_shape = (N_DEV,) + tuple(slab)
    n = len(ks)

    def body(src_ref, land_ref, after_ref, send_sems, recv_sems, src_thru, land_thru, token):
        for cp in _split_copies(src_ref, land_ref, send_sems, recv_sems, ks, per_peer, landed=False):
            cp.start()
        token[...] = jnp.zeros_like(token)

    return pl.pallas_call(
        body, name=name,
        out_shape=(pltpu.SemaphoreType.DMA((n,)), pltpu.SemaphoreType.DMA((n,)),
                   pltpu.HBM(src.shape, src.dtype), pltpu.HBM(land_shape, src.dtype),
                   jax.ShapeDtypeStruct((8, LANE), F32)),
        in_specs=(HBM_SPEC, HBM_SPEC, ANY),
        out_specs=(SEM_SPEC, SEM_SPEC, HBM_SPEC, HBM_SPEC, pl.BlockSpec(memory_space=pltpu.VMEM)),
        input_output_aliases={0: 2, 1: 3},
        compiler_params=pltpu.CompilerParams(has_side_effects=pltpu.SideEffectType.DATAFLOW_SIDE_EFFECTING),
    )(_hbm(src), _hbm(lax.empty(land_shape, src.dtype)), after)


def _exchange_wait(started, after, ks, per_peer, name):
    send_sems, recv_sems, src_thru, land_thru = started

    def body(src_ref, land_ref, send_sems, recv_sems, after_ref, src_dead, land_out):
        for cp in _split_copies(src_ref, land_ref, send_sems, recv_sems, ks, per_peer, landed=True):
            cp.wait_send()
            cp.wait_recv()

    return pl.pallas_call(
        body, name=name,
        out_shape=(pltpu.HBM(src_thru.shape, src_thru.dtype), pltpu.HBM(land_thru.shape, land_thru.dtype)),
        in_specs=(HBM_SPEC, HBM_SPEC, SEM_SPEC, SEM_SPEC, ANY), out_specs=(HBM_SPEC, HBM_SPEC),
        input_output_aliases={0: 0, 1: 1},
        compiler_params=pltpu.CompilerParams(has_side_effects=pltpu.SideEffectType.DATAFLOW_SIDE_EFFECTING),
    )(src_thru, land_thru, send_sems, recv_sems, after)


def _relay_copies(land_ref, send_sems, recv_sems, landed):
    me = _position()
    sibling = _peer(me, 1)
    out = []
    for i, k in enumerate(GATHER_PEERS[1:]):
        blk = land_ref.at[_slot(_peer(sibling if landed else me, k))]
        out.append(pltpu.make_async_remote_copy(
            src_ref=blk, dst_ref=blk, send_sem=send_sems.at[i], recv_sem=recv_sems.at[i],
            device_id=sibling, device_id_type=MESH))
    return out


def _relay_start(land, name):
    n = len(GATHER_PEERS) - 1

    def body(land_ref, send_sems, recv_sems, land_thru, token):
        for cp in _relay_copies(land_ref, send_sems, recv_sems, landed=False):
            cp.start()
        token[...] = jnp.zeros_like(token)

    return pl.pallas_call(
        body, name=name,
        out_shape=(pltpu.SemaphoreType.DMA((n,)), pltpu.SemaphoreType.DMA((n,)),
                   pltpu.HBM(land.shape, land.dtype), jax.ShapeDtypeStruct((8, LANE), F32)),
        in_specs=(HBM_SPEC,),
        out_specs=(SEM_SPEC, SEM_SPEC, HBM_SPEC, pl.BlockSpec(memory_space=pltpu.VMEM)),
        input_output_aliases={0: 2},
        compiler_params=pltpu.CompilerParams(has_side_effects=pltpu.SideEffectType.DATAFLOW_SIDE_EFFECTING),
    )(_hbm(land))


def _relay_wait(started, after, name):
    send_sems, recv_sems, land_thru = started

    def body(land_ref, send_sems, recv_sems, after_ref, land_out):
        for cp in _relay_copies(land_ref, send_sems, recv_sems, landed=True):
            cp.wait_send()
            cp.wait_recv()

    return pl.pallas_call(
        body, name=name,
        out_shape=pltpu.HBM(land_thru.shape, land_thru.dtype),
        in_specs=(HBM_SPEC, SEM_SPEC, SEM_SPEC, ANY), out_specs=HBM_SPEC,
        input_output_aliases={0: 0},
        compiler_params=pltpu.CompilerParams(has_side_effects=pltpu.SideEffectType.DATAFLOW_SIDE_EFFECTING),
    )(land_thru, send_sems, recv_sems, after)


def _share(vec, name, after=None):
    follows = [] if after is None else [after]

    def body(vec_ref, *rest):
        out_ref, send_sems, recv_sems, local_sem = rest[len(follows):]
        me = _position()

        def copy(k, landed):
            peer = _peer(me, k)
            return pltpu.make_async_remote_copy(
                src_ref=vec_ref, dst_ref=out_ref.at[_slot(peer) if landed else _slot(me)],
                send_sem=send_sems.at[k - 1], recv_sem=recv_sems.at[k - 1], device_id=peer, device_id_type=MESH)

        mine = pltpu.make_async_copy(vec_ref, out_ref.at[_slot(me)], local_sem)
        mine.start()
        sent = [copy(k, False) for k in ALL_PEERS]
        for cp in sent:
            cp.start()
        for k in ALL_PEERS:
            copy(k, True).wait_recv()
        for cp in sent:
            cp.wait_send()
        mine.wait()

    return pl.pallas_call(
        body, name=name,
        in_specs=[ANY] * (1 + len(follows)), out_specs=ANY,
        out_shape=jax.ShapeDtypeStruct((N_DEV,) + vec.shape, vec.dtype),
        scratch_shapes=[pltpu.SemaphoreType.DMA((N_DEV - 1,)), pltpu.SemaphoreType.DMA((N_DEV - 1,)),
                        pltpu.SemaphoreType.DMA],
    )(vec, *follows)


def _sum_slots(parts):
    def body(p_ref, o_ref):
        acc = p_ref[0]
        for dev in range(1, N_DEV):
            acc = acc + p_ref[dev]
        o_ref[...] = acc

    return pl.pallas_call(body, name="sum_slots",
                          out_shape=jax.ShapeDtypeStruct(parts.shape[1:], F32))(parts)


PACK_ROWS = 8


def _packed_rows(size):
    return -(-size // (PACK_ROWS * LANE)) * PACK_ROWS


def _pack(arrs):
    def rows(a):
        flat = a.reshape(-1)
        return jnp.pad(flat, (0, _packed_rows(flat.shape[0]) * LANE - flat.shape[0])).reshape(-1, LANE)

    return jnp.concatenate([rows(a) for a in arrs], axis=0)


def _unpack(packed, shapes):
    out, at = [], 0
    for shp in shapes:
        size = 1
        for dim in shp:
            size *= dim
        nrows = _packed_rows(size)
        out.append(packed[at:at + nrows].reshape(-1)[:size].reshape(shp))
        at += nrows
    return out


def _layer_fwd(x, wt, wo, g_pre, g_post, wa_pad, b_alpha, g_gla, g_att, rb_pad, midway=None):
    h = _rms_fwd(x, g_pre)
    z = _matmul(h, wt, "nt", F32, *TILES["in_proj"], "in_proj", n_outer=True)
    y_gla, o_gla, states = _gla_fwd(z, wa_pad, b_alpha, g_gla)
    if midway is not None:
        g_att = g_att + midway(y_gla)[:1, :1]
    y_att, o_att = _att_fwd(z, rb_pad, g_att)
    ycat = jnp.concatenate([y_gla, y_att], axis=1)
    y = _matmul(ycat, wo, "nn", F32, *TILES["out_proj"], "out_proj", n_outer=True)
    out = _post_fwd(x, y, g_post)
    return out, (x, h, z, o_gla, states, o_att, ycat, y)


def _layer_bwd(dout, saved, wt, wo, g_pre, g_post, wa_pad, b_alpha, g_gla, g_att, rb_pad, on_dwo, on_dwt):
    x, h, z, o_gla, states, o_att, ycat, y = saved
    dy, dg_post = _post_bwd(dout, y, g_post)
    dwo = _matmul(ycat, dy, "tn", BF16, *TILES["out_proj_dw"], "out_proj_dw")
    token = on_dwo(dwo)
    dycat = _matmul(dy, wo, "nt", F32, *TILES["out_proj_dx"], "out_proj_dx", n_outer=True, after=token)
    dq, dk, dv, dgg, dga, dwa, db, dg_gla = _gla_bwd(dycat, o_gla, z, wa_pad, b_alpha, g_gla, states)
    daq, dak, dav, dag, drb, dg_att = _att_bwd(dycat, o_att, z, rb_pad, g_att)
    dz = jnp.concatenate([dq, dk, dv, dgg, daq, dak.astype(BF16), dav.astype(BF16), dag, dga], axis=1)
    dwt = _matmul(dz, h, "tn", BF16, *TILES["in_proj_dw"], "in_proj_dw")
    token = on_dwt(dwt)
    dh = _matmul(dz, wt, "nn", F32, *TILES["in_proj_dx"], "in_proj_dx", n_outer=True, after=token)
    dx, dg_pre = _pre_bwd(dh, x, g_pre, dout)
    small = (dg_pre[0], dg_post[0], dwa[:GLA_RANK], db[0], dg_gla[0], dg_att[0], drb[:, 0, :N_REL])
    return dx, small


def kernel(x, w_in, w_out, g_pre, g_post, w_alpha, b_alpha, g_gla, g_att, rel_bias, loss_target, m_w_in, m_w_out, m_g_pre, m_g_post, m_w_alpha, m_b_alpha, m_g_gla, m_g_att, m_rel_bias, v_w_in, v_w_out, v_g_pre, v_g_post, v_w_alpha, v_b_alpha, v_g_gla, v_g_att, v_rel_bias):
    nl, d, cols = w_in.shape
    rows = w_out.shape[1]
    s = x.shape[1]
    x0 = x.reshape(s, d)
    tgt = loss_target.reshape(s, d)

    cols_first = lambda a: jnp.transpose(a, (2, 0, 1))
    w_c = cols_first(w_in)
    slab = _slab_rows(rows, cols)
    is_out = lax.broadcasted_iota(jnp.int32, (slab, d), 0) < rows

    def shard(l, zero=0.0):
        top = jnp.pad((w_out[l] + zero).astype(BF16), ((0, slab - rows), (0, 0)))
        rest = jnp.pad((w_c[:, l] + zero).astype(BF16), ((rows, slab - rows - cols), (0, 0)))
        return jnp.where(is_out, top, rest)

    first_fetch = _exchange_start(shard(0), x, GATHER_PEERS, False, "gather_start_0")
    began = first_fetch[4][0, 0]
    shards = [None] + [shard(l, began) for l in range(1, nl)]
    alpha = _pack([w_alpha]) + began
    wa_g = _share(alpha, "gather_alpha")
    wa_cols = w_alpha.shape[2]
    wa_full = wa_g.reshape(N_DEV, -1)[:, :nl * GLA_RANK * wa_cols].reshape(N_DEV, nl, GLA_RANK, wa_cols)
    wa_full = jnp.transpose(wa_full, (1, 2, 0, 3)).reshape(nl, GLA_RANK, GLA_KW)
    wa_pad = jnp.pad(wa_full, ((0, 0), (0, LANE - GLA_RANK), (0, 0)))
    rb_pad = jnp.pad(rel_bias, ((0, 0), (0, 0), (0, 3 * LANE - N_REL)))

    def layer_args(l, follows=None):
        gp = g_pre[l:l + 1] if follows is None else g_pre[l:l + 1] + follows[:1, :1]
        return (wts[l], wos[l], gp, g_post[l:l + 1], wa_pad[l], b_alpha[l:l + 1], g_gla[l:l + 1],
                g_att[l:l + 1], rb_pad[l])

    my = _slot(_position())

    def fetch(l, after):
        return _exchange_start(shards[l], after, GATHER_PEERS, False, f"gather_start_{l}")

    def relay(l, first_hop, after):
        own[l], land = _exchange_wait(first_hop[:4], after, GATHER_PEERS, False, f"gather_wait_{l}")
        return _relay_start(land, f"relay_start_{l}")

    def midway(l, y):
        flight["relay"] = relay(l + 1, flight["fetch"], y)
        if l + 2 >= nl:
            return flight["relay"][3]
        flight["fetch"] = fetch(l + 2, flight["relay"][2])
        return flight["fetch"][4]

    act, saved, wts, wos, flight, own = x0, [], [], [], {}, [None] * nl
    prepared = (wa_pad[0, :1, :1] + sum(sh[:1, :1].astype(F32) for sh in shards[1:]))
    flight["relay"] = relay(0, first_fetch, prepared)
    if nl > 1:
        flight["fetch"] = fetch(1, flight["relay"][2])
    for l in range(nl):
        land = _relay_wait(flight["relay"][:3], act, f"relay_wait_{l}")
        land = lax.dynamic_update_slice_in_dim(land, own[l][None], my, 0)
        wt_l, wo_l = _aligned_weight(land, rows, cols)
        wts.append(wt_l)
        wos.append(wo_l)
        act, sv = _layer_fwd(act, *layer_args(l, follows=first_fetch[4] if l == 0 else None),
                             midway=functools.partial(midway, l) if l + 1 < nl else None)
        saved.append(sv)
    dout, sq = _loss_head(act, tgt)
    loss = lax.psum(sq[0, 0] * (0.5 / d), ("x", "y", "c"))

    smalls, pending_out, pending_in = [None] * nl, [None] * nl, [None] * nl

    def send_out(l, dwo):
        pending_out[l] = _exchange_start(dwo.reshape(N_DEV, rows, d), dwo[:1, :1], ALL_PEERS, True,
                                         f"scatter_out_start_{l}")
        return pending_out[l][4]

    def send_in(l, dwt):
        pending_in[l] = _exchange_start(_partial_slabs(dwt, cols), dwt, ALL_PEERS, True, f"scatter_in_start_{l}")
        return pending_in[l][4]

    for l in reversed(range(nl)):
        dout, smalls[l] = _layer_bwd(dout, saved[l], *layer_args(l), on_dwo=functools.partial(send_out, l),
                                     on_dwt=functools.partial(send_in, l))
    grad_x = dout.reshape(x.shape)

    def landed(started, after, name):
        partial, land = _exchange_wait(started[:4], after, ALL_PEERS, True, name)
        return lax.dynamic_update_slice_in_dim(land, lax.dynamic_slice_in_dim(partial, my, 1, 0), my, 0)

    parts_out = [landed(pending_out[l], dout, f"scatter_out_wait_{l}") for l in range(nl)]
    g_w_out, d_w_out, m2_w_out, v2_w_out = _adam_sharded(parts_out, 0, w_out, m_w_out, v_w_out, "adam_w_out")
    names = 7
    small_stacked = [jnp.stack([smalls[l][i] for l in range(nl)]) for i in range(names)]
    shapes = [a.shape for a in small_stacked]
    gathered = _share(_pack(small_stacked), "gather_small_grads", after=d_w_out)
    g_pre_g, g_post_g, wa_g_full, b_g, gla_g, att_g, rb_g = _unpack(_sum_slots(gathered), shapes)
    wa_g_mine = lax.dynamic_slice_in_dim(wa_g_full, my * wa_cols, wa_cols, axis=2)
    grads = [g_pre_g, g_post_g, wa_g_mine, b_g, gla_g, att_g, rb_g]
    ws = [g_pre, g_post, w_alpha, b_alpha, g_gla, g_att, rel_bias]
    ms = [m_g_pre, m_g_post, m_w_alpha, m_b_alpha, m_g_gla, m_g_att, m_rel_bias]
    vs = [v_g_pre, v_g_post, v_w_alpha, v_b_alpha, v_g_gla, v_g_att, v_rel_bias]
    d_s, m2_s, v2_s = _adam_small(ws, grads, ms, vs)

    parts_in = [landed(pending_in[l], d_s[0], f"scatter_in_wait_{l}") for l in range(nl)]
    g_w_in, d_w_in, m2_w_in, v2_w_in = [
        jnp.transpose(a, (1, 2, 0))
        for a in _adam_columns(parts_in, 0, w_c, cols_first(m_w_in), cols_first(v_w_in))]

    def ordered(big_in, big_out, small):
        return [big_in, big_out] + list(small)

    return (loss, grad_x,
            *ordered(g_w_in, g_w_out, grads),
            *ordered(d_w_in, d_w_out, d_s),
            *ordered(m2_w_in, m2_w_out, m2_s),
            *ordered(v2_w_in, v2_w_out, v2_s))
```

```python
import functools

import jax
import jax.numpy as jnp
from jax import lax
from jax.experimental import pallas as pl
from jax.experimental.pallas import tpu as pltpu

F32 = jnp.float32
BF16 = jnp.bfloat16
MESH = pl.DeviceIdType.MESH
ANY = pl.BlockSpec(memory_space=pl.ANY)

CHUNK = 64
GLA_HEADS = 4
GLA_DK = 128
GLA_DV = 256
GLA_KW = GLA_HEADS * GLA_DK
D_GLA = GLA_HEADS * GLA_DV
GLA_RANK = 16
GLA_TAU = 16.0
ATT_HEADS = 8
ATT_HD = 128
D_ATT = ATT_HEADS * ATT_HD
LEFT_CHUNKS = 8
REL_CLIP = 128
N_REL = 2 * REL_CLIP + 1
EPS = 1e-6
D_IN = 2 * GLA_KW + 2 * D_GLA + GLA_RANK + 4 * D_ATT
GLA_SCALE = GLA_DK ** -0.5
ATT_SCALE = ATT_HD ** -0.5

ADAM_LR = 0.001
ADAM_B1 = 0.9
ADAM_B2 = 0.999
ADAM_EPS = 1e-08
ADAM_WD = 0.01
ADAM_STEP = 10

N_DEV = 8
LANE = 128
GA_ORIG = 2 * GLA_KW + 2 * D_GLA
OFF_AQ = GA_ORIG
OFF_GA = GA_ORIG + 4 * D_ATT
D_ZP = OFF_GA + LANE
QB = 2 * CHUNK
ATT_UNROLL = 8
WIN = (LEFT_CHUNKS + 2) * CHUNK
ET_ROWS = WIN + LEFT_CHUNKS * CHUNK
NEG = -1e30
VMEM_LIMIT = 48 * 1024 * 1024


def _cparams(sem):
    return pltpu.CompilerParams(dimension_semantics=sem, vmem_limit_bytes=VMEM_LIMIT)


def _dot(a, b):
    return jnp.dot(a, b, preferred_element_type=F32)


def _dot_nt(a, b):
    return lax.dot_general(a, b, (((1,), (1,)), ((), ())), preferred_element_type=F32)


def _dot_tn(a, b):
    return lax.dot_general(a, b, (((0,), (0,)), ((), ())), preferred_element_type=F32)


def _dot01(t, x, left=True):
    if not left:
        t, x = x, t
    hi = x.astype(BF16)
    r = x - hi.astype(F32)
    mid = r.astype(BF16)
    lo = (r - mid.astype(F32)).astype(BF16)
    if left:
        return _dot(t, hi) + _dot(t, mid) + _dot(t, lo)
    return _dot(hi, t) + _dot(mid, t) + _dot(lo, t)


def _sigmoid(x):
    return 1.0 / (1.0 + jnp.exp(-x))


def _log_sigmoid(x):
    return jnp.minimum(x, 0.0) - jnp.log(1.0 + jnp.exp(-jnp.abs(x)))


TILES = {
    "in_proj": (512, D_ZP // 3, None),
    "in_proj_dx": (512, 512, None),
    "in_proj_dw": (512, 1024, None),
    "out_proj": (512, 1024, None),
    "out_proj_dx": (512, 1024, None),
    "out_proj_dw": (1024, 1024, None),
}


def _matmul(a, b, mode, out_dtype, tm, tn, tk, name, n_outer=False, after=None):
    if mode == "nn":
        (m, k), n = a.shape, b.shape[1]
    elif mode == "nt":
        (m, k), n = a.shape, b.shape[0]
    else:
        (k, m), n = a.shape, b.shape[1]
    tm, tn, tk = min(tm, m), min(tn, n), k if tk is None else min(tk, k)
    assert m % tm == 0 and n % tn == 0 and k % tk == 0, (name, m, n, k)
    nk = k // tk
    dot = {"nn": _dot, "nt": _dot_nt, "tn": _dot_tn}[mode]

    follows = [] if after is None else [after]

    def body_whole_k(a_ref, b_ref, *rest):
        o_ref = rest[-1]
        o_ref[...] = dot(a_ref[...], b_ref[...]).astype(out_dtype)

    def body(a_ref, b_ref, *rest):
        o_ref, acc_ref = rest[-2:]
        kk = pl.program_id(2)

        @pl.when(kk == 0)
        def _():
            acc_ref[...] = jnp.zeros_like(acc_ref)

        acc_ref[...] += dot(a_ref[...], b_ref[...])

        @pl.when(kk == nk - 1)
        def _():
            o_ref[...] = acc_ref[...].astype(out_dtype)

    def at(index):
        return (lambda j, i, kk: index(i, j, kk)) if n_outer else index

    if mode == "tn":
        a_spec = pl.BlockSpec((tk, tm), at(lambda i, j, kk: (kk, i)))
    else:
        a_spec = pl.BlockSpec((tm, tk), at(lambda i, j, kk: (i, kk)))
    if mode == "nt":
        b_spec = pl.BlockSpec((tn, tk), at(lambda i, j, kk: (j, kk)))
    else:
        b_spec = pl.BlockSpec((tk, tn), at(lambda i, j, kk: (kk, j)))
    return pl.pallas_call(
        body_whole_k if nk == 1 else body, name=name,
        grid=(n // tn, m // tm, nk) if n_outer else (m // tm, n // tn, nk),
        in_specs=[a_spec, b_spec] + [ANY] * len(follows),
        out_specs=pl.BlockSpec((tm, tn), at(lambda i, j, kk: (i, j))),
        out_shape=jax.ShapeDtypeStruct((m, n), out_dtype),
        scratch_shapes=[] if nk == 1 else [pltpu.VMEM((tm, tn), F32)],
        compiler_params=_cparams(("parallel", "parallel", "arbitrary")),
    )(a, b, *follows)


def _matmul_cols(pieces, b, out_dtype, tm, tn, name, after=None):
    m, n = pieces[0].shape[0], b.shape[1]
    widths = [p.shape[1] for p in pieces]
    starts = [sum(widths[:i]) for i in range(len(pieces))]
    follows = [] if after is None else [after]
    tm, tn = min(tm, m), min(tn, n)
    assert sum(widths) == b.shape[0] and m % tm == 0 and n % tn == 0, name

    def body(*refs):
        b_ref, o_ref = refs[len(pieces)], refs[-1]
        acc = None
        for p_ref, at, width in zip(refs, starts, widths):
            part = _dot(p_ref[...], b_ref[at:at + width, :])
            acc = part if acc is None else acc + part
        o_ref[...] = acc.astype(out_dtype)

    return pl.pallas_call(
        body, name=name, grid=(n // tn, m // tm),
        in_specs=[pl.BlockSpec((tm, width), lambda j, i: (i, 0)) for width in widths]
        + [pl.BlockSpec((b.shape[0], tn), lambda j, i: (0, j))] + [ANY] * len(follows),
        out_specs=pl.BlockSpec((tm, tn), lambda j, i: (i, j)),
        out_shape=jax.ShapeDtypeStruct((m, n), out_dtype),
        compiler_params=_cparams(("parallel", "parallel")),
    )(*pieces, b, *follows)


def _matmul_rows(pieces, b, out_dtype, tw, tn, name):
    k, n = b.shape
    tn = min(tn, n)
    counts = [p.shape[1] // tw for p in pieces]
    firsts = [sum(counts[:i]) for i in range(len(pieces))]
    assert all(p.shape[1] % tw == 0 for p in pieces) and n % tn == 0, name

    def body(*refs):
        b_ref, o_ref = refs[len(pieces):]
        for p_ref, first, count in zip(refs, firsts, counts):
            @pl.when((pl.program_id(0) >= first) & (pl.program_id(0) < first + count))
            def _(p_ref=p_ref):
                o_ref[...] = _dot_tn(p_ref[...], b_ref[...]).astype(out_dtype)

    def piece_spec(first, count):
        return pl.BlockSpec((k, tw), lambda i, j: (0, jnp.clip(i - first, 0, count - 1)))

    return pl.pallas_call(
        body, name=name, grid=(sum(counts), n // tn),
        in_specs=[piece_spec(first, count) for first, count in zip(firsts, counts)]
        + [pl.BlockSpec((k, tn), lambda i, j: (0, j))],
        out_specs=pl.BlockSpec((tw, tn), lambda i, j: (i, j)),
        out_shape=jax.ShapeDtypeStruct((sum(counts) * tw, n), out_dtype),
        compiler_params=_cparams(("parallel", "parallel")),
    )(*pieces, b)


ROWS = 256


def _rms_fwd(x, g):
    s, d = x.shape

    def body(x_ref, g_ref, h_ref):
        xv = x_ref[...]
        r = lax.rsqrt(jnp.mean(xv * xv, axis=-1, keepdims=True) + EPS)
        h_ref[...] = (xv * r * g_ref[...]).astype(BF16)

    return pl.pallas_call(
        body, name="rms_fwd", grid=(s // ROWS,),
        in_specs=[pl.BlockSpec((ROWS, d), lambda i: (i, 0)), pl.BlockSpec((1, d), lambda i: (0, 0))],
        out_specs=pl.BlockSpec((ROWS, d), lambda i: (i, 0)),
        out_shape=jax.ShapeDtypeStruct((s, d), BF16),
        compiler_params=_cparams(("parallel",)),
    )(x, g)


def _post_fwd(x, y, g):
    s, d = x.shape

    def body(x_ref, y_ref, g_ref, o_ref):
        yv = y_ref[...]
        r = lax.rsqrt(jnp.mean(yv * yv, axis=-1, keepdims=True) + EPS)
        o_ref[...] = x_ref[...] + yv * r * g_ref[...]

    row = pl.BlockSpec((ROWS, d), lambda i: (i, 0))
    return pl.pallas_call(
        body, name="post_fwd", grid=(s // ROWS,),
        in_specs=[row, row, pl.BlockSpec((1, d), lambda i: (0, 0))],
        out_specs=row,
        out_shape=jax.ShapeDtypeStruct((s, d), F32),
        compiler_params=_cparams(("parallel",)),
    )(x, y, g)


def _loss_head(out, tgt):
    s, d = out.shape

    def body(o_ref, t_ref, dout_ref, sum_ref):
        @pl.when(pl.program_id(0) == 0)
        def _():
            sum_ref[...] = jnp.zeros_like(sum_ref)

        e = o_ref[...] - t_ref[...]
        dout_ref[...] = e * (1.0 / d)
        sum_ref[...] += jnp.sum(jnp.sum(e * e, axis=1, keepdims=True), axis=0, keepdims=True)

    row = pl.BlockSpec((ROWS, d), lambda i: (i, 0))
    return pl.pallas_call(
        body, name="loss_head", grid=(s // ROWS,),
        in_specs=[row, row],
        out_specs=[row, pl.BlockSpec((1, 1), lambda i: (0, 0))],
        out_shape=[jax.ShapeDtypeStruct((s, d), F32), jax.ShapeDtypeStruct((1, 1), F32)],
        compiler_params=_cparams(("arbitrary",)),
    )(out, tgt)


def _post_bwd(dout, y, g):
    s, d = y.shape

    def body(do_ref, y_ref, g_ref, dy_ref, dg_ref):
        @pl.when(pl.program_id(0) == 0)
        def _():
            dg_ref[...] = jnp.zeros_like(dg_ref)

        yv = y_ref[...]
        dv = do_ref[...]
        r = lax.rsqrt(jnp.mean(yv * yv, axis=-1, keepdims=True) + EPS)
        dg_ref[...] += jnp.sum(dv * yv * r, axis=0, keepdims=True)
        w = dv * g_ref[...]
        dy = r * (w - yv * (r * r) * jnp.mean(w * yv, axis=-1, keepdims=True))
        dy_ref[...] = dy.astype(BF16)

    row = pl.BlockSpec((ROWS, d), lambda i: (i, 0))
    vec = pl.BlockSpec((1, d), lambda i: (0, 0))
    return pl.pallas_call(
        body, name="post_bwd", grid=(s // ROWS,),
        in_specs=[row, row, vec],
        out_specs=[row, vec],
        out_shape=[jax.ShapeDtypeStruct((s, d), BF16), jax.ShapeDtypeStruct((1, d), F32)],
        compiler_params=_cparams(("arbitrary",)),
    )(dout, y, g)


def _pre_bwd(dh, x, g, dout):
    s, d = x.shape

    def body(dh_ref, x_ref, g_ref, do_ref, dx_ref, dg_ref):
        @pl.when(pl.program_id(0) == 0)
        def _():
            dg_ref[...] = jnp.zeros_like(dg_ref)

        xv = x_ref[...]
        dv = dh_ref[...]
        r = lax.rsqrt(jnp.mean(xv * xv, axis=-1, keepdims=True) + EPS)
        dg_ref[...] += jnp.sum(dv * xv * r, axis=0, keepdims=True)
        w = dv * g_ref[...]
        dx_ref[...] = do_ref[...] + r * (w - xv * (r * r) * jnp.mean(w * xv, axis=-1, keepdims=True))

    row = pl.BlockSpec((ROWS, d), lambda i: (i, 0))
    vec = pl.BlockSpec((1, d), lambda i: (0, 0))
    return pl.pallas_call(
        body, name="pre_bwd", grid=(s // ROWS,),
        in_specs=[row, row, vec, row],
        out_specs=[row, vec],
        out_shape=[jax.ShapeDtypeStruct((s, d), F32), jax.ShapeDtypeStruct((1, d), F32)],
        compiler_params=_cparams(("arbitrary",)),
    )(dh, x, g, dout)


GLA_STEP = 4
GLA_ROWS = GLA_STEP * CHUNK
GLA_CHUNKS = [slice(c * CHUNK, (c + 1) * CHUNK) for c in range(GLA_STEP)]


def _chunk_triangles():
    ri = lax.broadcasted_iota(jnp.int32, (GLA_ROWS, GLA_ROWS), 0)
    ci = lax.broadcasted_iota(jnp.int32, (GLA_ROWS, GLA_ROWS), 1)
    same = (ri // CHUNK) == (ci // CHUNK)
    return (jnp.where(same & (ri >= ci), 1.0, 0.0).astype(BF16), jnp.where(same & (ci >= ri), 1.0, 0.0).astype(BF16))


def _per_chunk(fn, like):
    row = lax.broadcasted_iota(jnp.int32, like.shape, 0)
    return [fn((row >= c * CHUNK) & (row < (c + 1) * CHUNK)) for c in range(GLA_STEP)]


def _spread(per_chunk, like):
    row = lax.broadcasted_iota(jnp.int32, like.shape, 0)
    out = per_chunk[-1]
    for c in reversed(range(GLA_STEP - 1)):
        out = jnp.where(row < (c + 1) * CHUNK, per_chunk[c], out)
    return out


def _gla_gate(ga_b, wa_b, b_ref, tri):
    pre = _dot(ga_b, wa_b) + b_ref[...]
    la = _log_sigmoid(pre) * (1.0 / GLA_TAU)
    cum = _dot01(tri, la)
    row = lax.broadcasted_iota(jnp.int32, cum.shape, 0)
    cends = [jnp.sum(jnp.where(row == (c + 1) * CHUNK - 1, cum, 0.0), axis=0, keepdims=True)
             for c in range(GLA_STEP)]
    return pre, cum, cends


def _heads(width):
    return [slice(h * width, (h + 1) * width) for h in range(GLA_HEADS)]


def _z_specs_gla(rev=None):
    idx = (lambda n: n) if rev is None else rev
    return [
        pl.BlockSpec((GLA_ROWS, GLA_KW), lambda n: (idx(n), 0)),
        pl.BlockSpec((GLA_ROWS, GLA_KW), lambda n: (idx(n), 1)),
        pl.BlockSpec((GLA_ROWS, D_GLA), lambda n: (idx(n), 1)),
        pl.BlockSpec((GLA_ROWS, D_GLA), lambda n: (idx(n), 2)),
        pl.BlockSpec((GLA_ROWS, LANE), lambda n: (idx(n), OFF_GA // LANE)),
    ]


def _gla_fwd(z, wa_pad, b_alpha, g_gla):
    s = z.shape[0]
    nchunk = s // CHUNK

    def body(q_ref, k_ref, v_ref, gg_ref, ga_ref, wa_ref, b_ref, g_ref, y_ref, o_ref, st_ref, state):
        @pl.when(pl.program_id(0) == 0)
        def _():
            state[...] = jnp.zeros_like(state)

        ga_b = ga_ref[...].astype(BF16)
        tri, _ = _chunk_triangles()
        nh = range(GLA_HEADS)
        keys, vals = _heads(GLA_DK), _heads(GLA_DV)
        _, cum, cends = _gla_gate(ga_b, wa_ref[...].astype(BF16), b_ref, tri)
        kd_b = (k_ref[...] * jnp.exp(_spread(cends, cum) - cum)).astype(BF16)
        qs = (q_ref[...] * GLA_SCALE).astype(BF16)
        v_b = v_ref[...].astype(BF16)
        uts = [[_dot_tn(v_b[rs, vals[h]], kd_b[rs, keys[h]]) for h in nh] for rs in GLA_CHUNKS]
        sts, prev = [], [state[h] for h in nh]
        for c in range(GLA_STEP):
            a = jnp.exp(cends[c])
            prev = [prev[h] * a[:, keys[h]] + uts[c][h] for h in nh]
            sts.append(prev)
        for h in nh:
            state[h] = prev[h]
            for c in range(GLA_STEP):
                st_ref[c, h] = sts[c][h]
        outs = [[_dot_nt(qs[rs, keys[h]], sts[c][h].astype(BF16)) for h in nh] for c, rs in enumerate(GLA_CHUNKS)]
        for h in nh:
            o, vs = jnp.concatenate([outs[c][h] for c in range(GLA_STEP)], axis=0), vals[h]
            o_ref[:, vs] = o
            r = lax.rsqrt(jnp.mean(o * o, axis=-1, keepdims=True) + EPS)
            gg = gg_ref[:, vs]
            y_ref[:, vs] = (o * r * g_ref[:, vs] * (gg * _sigmoid(gg))).astype(BF16)

    full = lambda shape: pl.BlockSpec(shape, lambda n: tuple(0 for _ in shape))
    wide = pl.BlockSpec((GLA_ROWS, D_GLA), lambda n: (n, 0))
    return pl.pallas_call(
        body, name="gla_fwd", grid=(nchunk // GLA_STEP,),
        in_specs=_z_specs_gla() + [full((LANE, GLA_KW)), full((1, GLA_KW)), full((1, D_GLA))],
        out_specs=[wide, wide, pl.BlockSpec((GLA_STEP, GLA_HEADS, GLA_DV, GLA_DK), lambda n: (n, 0, 0, 0))],
        out_shape=[jax.ShapeDtypeStruct((s, D_GLA), BF16), jax.ShapeDtypeStruct((s, D_GLA), F32),
                   jax.ShapeDtypeStruct((nchunk, GLA_HEADS, GLA_DV, GLA_DK), F32)],
        scratch_shapes=[pltpu.VMEM((GLA_HEADS, GLA_DV, GLA_DK), F32)],
        compiler_params=_cparams(("arbitrary",)),
    )(z, z, z, z, z, wa_pad, b_alpha, g_gla)


def _gla_bwd(dyc, o_gla, z, wa_pad, b_alpha, g_gla, states):
    s = z.shape[0]
    nsteps = s // GLA_ROWS
    rev = lambda n: nsteps - 1 - n

    def body(dy_ref, o_ref, q_ref, k_ref, v_ref, gg_ref, ga_ref, wa_ref, b_ref, g_ref, st_ref, stp_ref,
             dq_ref, dk_ref, dv_ref, dgg_ref, dga_ref, dwa_ref, db_ref, dg_ref, carry):
        step = pl.program_id(0)

        @pl.when(step == 0)
        def _():
            carry[...] = jnp.zeros_like(carry)
            dwa_ref[...] = jnp.zeros_like(dwa_ref)
            db_ref[...] = jnp.zeros_like(db_ref)
            dg_ref[...] = jnp.zeros_like(dg_ref)

        has_prev = (step < nsteps - 1).astype(F32)
        ga_b = ga_ref[...].astype(BF16)
        tri, tri_up = _chunk_triangles()
        nh, nc = range(GLA_HEADS), range(GLA_STEP)
        keys, vals = _heads(GLA_DK), _heads(GLA_DV)
        wa_b = wa_ref[...].astype(BF16)
        pre, cum, cends = _gla_gate(ga_b, wa_b, b_ref, tri)
        e = jnp.exp(_spread(cends, cum) - cum)
        a = [jnp.exp(cends[c]) for c in nc]
        kf = k_ref[...]
        kd_b = (kf * e).astype(BF16)
        v_b = v_ref[...].astype(BF16)
        qs = (q_ref[...] * GLA_SCALE).astype(BF16)
        do_b = []
        for h in nh:
            vs = vals[h]
            o = o_ref[:, vs]
            gg = gg_ref[:, vs]
            g = g_ref[:, vs]
            dy = dy_ref[:, vs]
            r = lax.rsqrt(jnp.mean(o * o, axis=-1, keepdims=True) + EPS)
            sg = _sigmoid(gg)
            dogn = dy * (gg * sg)
            dgg_ref[:, vs] = (dy * (o * r * g) * (sg * (1.0 + gg * (1.0 - sg)))).astype(BF16)
            dg_ref[:, vs] += jnp.sum(dogn * o * r, axis=0, keepdims=True)
            w = dogn * g
            do_b.append((r * (w - o * (r * r) * jnp.mean(w * o, axis=-1, keepdims=True))).astype(BF16))
        dqs = [jnp.concatenate([_dot(do_b[h][rs], st_ref[c, h].astype(BF16)) for c, rs in enumerate(GLA_CHUNKS)],
                               axis=0) for h in nh]
        dq_ref[...] = (jnp.concatenate(dqs, axis=1) * GLA_SCALE).astype(BF16)
        own = [[_dot_tn(do_b[h][rs], qs[rs, keys[h]]) for h in nh] for rs in GLA_CHUNKS]
        gts, later = [None] * GLA_STEP, [carry[h] for h in nh]
        for c in reversed(nc):
            gts[c] = [own[c][h] + later[h] for h in nh]
            later = [gts[c][h] * a[c][:, keys[h]] for h in nh]
        for h in nh:
            carry[h] = later[h]
        gt_b = [[gts[c][h].astype(BF16) for h in nh] for c in nc]
        dkd = jnp.concatenate([jnp.concatenate([_dot(v_b[rs, vals[h]], gt_b[c][h]) for h in nh], axis=1)
                               for c, rs in enumerate(GLA_CHUNKS)], axis=0)
        dvs = [[_dot_nt(kd_b[rs, keys[h]], gt_b[c][h]) for h in nh] for c, rs in enumerate(GLA_CHUNKS)]
        before = lambda c, h: st_ref[c - 1, h] if c > 0 else stp_ref[0, h] * has_prev
        da = [jnp.concatenate([jnp.sum(gts[c][h] * before(c, h), axis=0, keepdims=True) for h in nh], axis=1)
              for c in nc]
        for h in nh:
            dv_ref[:, vals[h]] = jnp.concatenate([dvs[c][h] for c in nc], axis=0).astype(BF16)
        dk_ref[...] = (dkd * e).astype(BF16)
        dd = dkd * kf * e
        dsum = _per_chunk(lambda mine: jnp.sum(jnp.where(mine, dd, 0.0), axis=0, keepdims=True), dd)
        dcend = _spread([dsum[c] + da[c] * a[c] for c in nc], dd)
        dla = dcend - _dot01(tri_up, dd)
        dpre = dla * (1.0 / GLA_TAU) * (1.0 - _sigmoid(pre))
        dpre_b = dpre.astype(BF16)
        dga_ref[...] = _dot_nt(dpre_b, wa_b).astype(BF16)
        dwa_ref[...] += _dot_tn(ga_b, dpre_b)
        db_ref[...] += jnp.sum(dpre, axis=0, keepdims=True)

    full = lambda shape: pl.BlockSpec(shape, lambda n: tuple(0 for _ in shape))
    wide = pl.BlockSpec((GLA_ROWS, D_GLA), lambda n: (rev(n), 0))
    keyw = pl.BlockSpec((GLA_ROWS, GLA_KW), lambda n: (rev(n), 0))
    st_spec = pl.BlockSpec((GLA_STEP, GLA_HEADS, GLA_DV, GLA_DK), lambda n: (rev(n), 0, 0, 0))
    stp_spec = pl.BlockSpec((1, GLA_HEADS, GLA_DV, GLA_DK),
                            lambda n: (jnp.maximum(GLA_STEP * rev(n) - 1, 0), 0, 0, 0))
    return pl.pallas_call(
        body, name="gla_bwd", grid=(nsteps,),
        in_specs=[wide, wide] + _z_specs_gla(rev)
        + [full((LANE, GLA_KW)), full((1, GLA_KW)), full((1, D_GLA)), st_spec, stp_spec],
        out_specs=[keyw, keyw, wide, wide, pl.BlockSpec((GLA_ROWS, LANE), lambda n: (rev(n), 0)),
                   full((LANE, GLA_KW)), full((1, GLA_KW)), full((1, D_GLA))],
        out_shape=[jax.ShapeDtypeStruct((s, GLA_KW), BF16), jax.ShapeDtypeStruct((s, GLA_KW), BF16),
                   jax.ShapeDtypeStruct((s, D_GLA), BF16), jax.ShapeDtypeStruct((s, D_GLA), BF16),
                   jax.ShapeDtypeStruct((s, LANE), BF16),
                   jax.ShapeDtypeStruct((LANE, GLA_KW), F32), jax.ShapeDtypeStruct((1, GLA_KW), F32),
                   jax.ShapeDtypeStruct((1, D_GLA), F32)],
        scratch_shapes=[pltpu.VMEM((GLA_HEADS, GLA_DV, GLA_DK), F32)],
        compiler_params=_cparams(("arbitrary",)),
    )(dyc, o_gla, z, z, z, z, z, wa_pad, b_alpha, g_gla, states, states)


def _build_bias_table(rb_row, et_ref):
    far = jnp.broadcast_to(rb_row[:, 2 * REL_CLIP:2 * REL_CLIP + 1], (1, LANE))
    near_hi = rb_row[:, REL_CLIP:2 * REL_CLIP]
    near_lo = rb_row[:, 0:REL_CLIP]
    past = jnp.broadcast_to(rb_row[:, 0:1], (1, LANE))
    seg = [far, far, far, far, near_hi, near_lo] + [past] * (ET_ROWS // LANE - 5)
    ri = lax.broadcasted_iota(jnp.int32, (LANE, LANE), 0)
    ci = lax.broadcasted_iota(jnp.int32, (LANE, LANE), 1)
    for kb in range(ET_ROWS // LANE):
        wmat = jnp.where(ri + ci < LANE, seg[kb], seg[kb + 1])
        blk = pltpu.roll(wmat, 0, 1, stride=1, stride_axis=0)
        lag = LEFT_CHUNKS + ci // CHUNK - (2 * kb + ri // CHUNK)
        et_ref[kb * LANE:(kb + 1) * LANE, :] = jnp.where((lag >= 0) & (lag <= LEFT_CHUNKS), blk, NEG)


def _reduce_bias_table(det_ref):
    lane = lax.broadcasted_iota(jnp.int32, (1, LANE), 1)
    ri = lax.broadcasted_iota(jnp.int32, (LANE, LANE), 0)
    ci = lax.broadcasted_iota(jnp.int32, (LANE, LANE), 1)
    flip = jnp.where(ri + ci == LANE - 1, 1.0, 0.0).astype(BF16)
    segs = jnp.zeros((8, LANE), F32)
    seg_row = lax.broadcasted_iota(jnp.int32, (8, LANE), 0)
    prev_minus = jnp.zeros((1, LANE), F32)
    for kb in range(6):
        rolled = pltpu.roll(_dot01(det_ref[kb * LANE:(kb + 1) * LANE, :], flip, left=False), 0, 1,
                            stride=1, stride_axis=0)
        plus = jnp.sum(jnp.where(ci >= ri, rolled, 0.0), axis=0, keepdims=True)
        minus = jnp.sum(jnp.where(ci < ri, rolled, 0.0), axis=0, keepdims=True)
        segs = segs + jnp.where(seg_row == kb, plus + prev_minus, 0.0)
        prev_minus = minus
    segs = _dot01(segs, flip, left=False)
    pick = lambda kb: jnp.sum(jnp.where(seg_row == kb, segs, 0.0), axis=0, keepdims=True)
    far = jnp.sum(pick(0) + pick(1) + pick(2) + pick(3), axis=1, keepdims=True)
    last = jnp.where(lane == 0, far, 0.0)
    return jnp.concatenate([pick(5), pick(4), last], axis=1)


def _att_window(b):
    c0 = 2 * b
    kstart = pl.multiple_of(jnp.maximum(c0 - LEFT_CHUNKS, 0) * CHUNK, CHUNK)
    eoff = pl.multiple_of(jnp.maximum(LEFT_CHUNKS - c0, 0) * CHUNK, CHUNK)
    return kstart, eoff


def _att_probs(q_b, kw_b, et):
    st = _dot_nt(kw_b, q_b) * ATT_SCALE + et
    m = jnp.max(st, axis=0, keepdims=True)
    ex = jnp.exp(st - m)
    return ex * (1.0 / jnp.sum(ex, axis=0, keepdims=True))


def _att_fwd(z, rb_pad, g_att):
    s = z.shape[0]
    nblk = s // QB
    c_aq, c_ak, c_av, c_ag = [(OFF_AQ + i * D_ATT) // ATT_HD for i in range(4)]

    def body(q_ref, k_ref, v_ref, ag_ref, rb_ref, g_ref, y_ref, o_ref, et_ref, kb_ref, vb_ref):
        h = pl.program_id(0)
        b = pl.program_id(1)

        @pl.when(b == 0)
        def _():
            _build_bias_table(rb_ref[pl.ds(h, 1), :], et_ref)
            kb_ref[...] = k_ref[...].astype(BF16)
            vb_ref[...] = v_ref[...].astype(BF16)

        for j in range(ATT_UNROLL):
            rs = slice(j * QB, (j + 1) * QB)
            kstart, eoff = _att_window(b * ATT_UNROLL + j)
            q_b = q_ref[rs, :].astype(BF16)
            kw_b = kb_ref[pl.ds(kstart, WIN), :]
            vw_b = vb_ref[pl.ds(kstart, WIN), :]
            pt = _att_probs(q_b, kw_b, et_ref[pl.ds(eoff, WIN), :])
            o = _dot_tn(pt.astype(BF16), vw_b)
            o_ref[rs, :] = o
            r = lax.rsqrt(jnp.mean(o * o, axis=-1, keepdims=True) + EPS)
            ag = ag_ref[rs, :]
            y_ref[rs, :] = (o * r * g_ref[...] * (ag * _sigmoid(ag))).astype(BF16)

    blk = lambda col: pl.BlockSpec((ATT_UNROLL * QB, ATT_HD), lambda h, b: (b, col + h))
    seq = lambda col: pl.BlockSpec((s, ATT_HD), lambda h, b: (0, col + h))
    out_blk = pl.BlockSpec((ATT_UNROLL * QB, ATT_HD), lambda h, b: (b, h))
    return pl.pallas_call(
        body, name="att_fwd", grid=(ATT_HEADS, nblk // ATT_UNROLL),
        in_specs=[blk(c_aq), seq(c_ak), seq(c_av), blk(c_ag),
                  pl.BlockSpec((ATT_HEADS, 3 * LANE), lambda h, b: (0, 0)),
                  pl.BlockSpec((1, ATT_HD), lambda h, b: (0, h))],
        out_specs=[out_blk, out_blk],
        out_shape=[jax.ShapeDtypeStruct((s, D_ATT), BF16), jax.ShapeDtypeStruct((s, D_ATT), F32)],
        scratch_shapes=[pltpu.VMEM((ET_ROWS, LANE), F32), pltpu.VMEM((s, ATT_HD), BF16),
                        pltpu.VMEM((s, ATT_HD), BF16)],
        compiler_params=_cparams(("arbitrary", "arbitrary")),
    )(z, z, z, z, rb_pad, g_att)


def _att_bwd(dyc, o_att, z, rb_pad, g_att):
    s = z.shape[0]
    nblk = s // QB
    c_aq, c_ak, c_av, c_ag = [(OFF_AQ + i * D_ATT) // ATT_HD for i in range(4)]
    c_dy = D_GLA // ATT_HD

    def body(dy_ref, o_ref, q_ref, k_ref, v_ref, ag_ref, rb_ref, g_ref,
             dq_ref, dk_ref, dv_ref, dag_ref, drb_ref, dg_ref, et_ref, det_ref, kb_ref, vb_ref, dk_acc, dv_acc):
        h = pl.program_id(0)
        b = pl.program_id(1)

        @pl.when(b == 0)
        def _():
            _build_bias_table(rb_ref[pl.ds(h, 1), :], et_ref)
            kb_ref[...] = k_ref[...].astype(BF16)
            vb_ref[...] = v_ref[...].astype(BF16)
            det_ref[...] = jnp.zeros_like(det_ref)
            dk_acc[...] = jnp.zeros_like(dk_acc)
            dv_acc[...] = jnp.zeros_like(dv_acc)
            dg_ref[...] = jnp.zeros_like(dg_ref)

        g = g_ref[...]
        dg = jnp.zeros((1, ATT_HD), F32)
        for j in range(ATT_UNROLL):
            rs = slice(j * QB, (j + 1) * QB)
            kstart, eoff = _att_window(b * ATT_UNROLL + j)
            q_b = q_ref[rs, :].astype(BF16)
            kw_b = kb_ref[pl.ds(kstart, WIN), :]
            vw_b = vb_ref[pl.ds(kstart, WIN), :]
            pt = _att_probs(q_b, kw_b, et_ref[pl.ds(eoff, WIN), :])
            o = o_ref[rs, :]
            ag = ag_ref[rs, :]
            dy = dy_ref[rs, :]
            r = lax.rsqrt(jnp.mean(o * o, axis=-1, keepdims=True) + EPS)
            sg = _sigmoid(ag)
            don = dy * (ag * sg)
            dag_ref[rs, :] = (dy * (o * r * g) * (sg * (1.0 + ag * (1.0 - sg)))).astype(BF16)
            dg = dg + jnp.sum(don * o * r, axis=0, keepdims=True)
            w = don * g
            do_b = (r * (w - o * (r * r) * jnp.mean(w * o, axis=-1, keepdims=True))).astype(BF16)
            pt_b = pt.astype(BF16)
            dpt = _dot_nt(vw_b, do_b)
            dst = pt * (dpt - jnp.sum(dpt * pt, axis=0, keepdims=True))
            det_ref[pl.ds(eoff, WIN), :] += dst
            ds_b = (dst * ATT_SCALE).astype(BF16)
            dq_ref[rs, :] = _dot_tn(ds_b, kw_b).astype(BF16)
            dk_acc[pl.ds(kstart, WIN), :] += _dot(ds_b, q_b)
            dv_acc[pl.ds(kstart, WIN), :] += _dot(pt_b, do_b)
        dg_ref[...] += dg

        @pl.when(b == nblk // ATT_UNROLL - 1)
        def _():
            drb_ref[0] = jnp.broadcast_to(_reduce_bias_table(det_ref), (8, 3 * LANE))
            dk_ref[...] = dk_acc[...].astype(BF16)
            dv_ref[...] = dv_acc[...].astype(BF16)

    blk = lambda col: pl.BlockSpec((ATT_UNROLL * QB, ATT_HD), lambda h, b: (b, col + h))
    seq = lambda col: pl.BlockSpec((s, ATT_HD), lambda h, b: (0, col + h))
    out_blk = pl.BlockSpec((ATT_UNROLL * QB, ATT_HD), lambda h, b: (b, h))
    out_seq = pl.BlockSpec((s, ATT_HD), lambda h, b: (0, h))
    return pl.pallas_call(
        body, name="att_bwd", grid=(ATT_HEADS, nblk // ATT_UNROLL),
        in_specs=[blk(c_dy), blk(0), blk(c_aq), seq(c_ak), seq(c_av), blk(c_ag),
                  pl.BlockSpec((ATT_HEADS, 3 * LANE), lambda h, b: (0, 0)),
                  pl.BlockSpec((1, ATT_HD), lambda h, b: (0, h))],
        out_specs=[out_blk, out_seq, out_seq, out_blk,
                   pl.BlockSpec((1, 8, 3 * LANE), lambda h, b: (h, 0, 0)),
                   pl.BlockSpec((1, ATT_HD), lambda h, b: (0, h))],
        out_shape=[jax.ShapeDtypeStruct((s, D_ATT), BF16), jax.ShapeDtypeStruct((s, D_ATT), BF16),
                   jax.ShapeDtypeStruct((s, D_ATT), BF16), jax.ShapeDtypeStruct((s, D_ATT), BF16),
                   jax.ShapeDtypeStruct((ATT_HEADS, 8, 3 * LANE), F32),
                   jax.ShapeDtypeStruct((1, D_ATT), F32)],
        scratch_shapes=[pltpu.VMEM((ET_ROWS, LANE), F32), pltpu.VMEM((ET_ROWS, LANE), F32),
                        pltpu.VMEM((s, ATT_HD), BF16), pltpu.VMEM((s, ATT_HD), BF16),
                        pltpu.VMEM((s, ATT_HD), F32), pltpu.VMEM((s, ATT_HD), F32)],
        compiler_params=_cparams(("arbitrary", "arbitrary")),
    )(dyc, o_att, z, z, z, z, rb_pad, g_att)


ADAM_ROWS = 64
ADAM_COL_ROWS = 32


def _adam_math(w, g, m, v):
    m2 = ADAM_B1 * m + (1.0 - ADAM_B1) * g
    v2 = ADAM_B2 * v + (1.0 - ADAM_B2) * (g * g)
    m_hat = m2 / (1.0 - ADAM_B1 ** ADAM_STEP)
    v_hat = v2 / (1.0 - ADAM_B2 ** ADAM_STEP)
    delta = -ADAM_LR * (m_hat / (jnp.sqrt(v_hat) + ADAM_EPS) + ADAM_WD * w)
    return delta, m2, v2


def _adam_sharded(parts, first, w, m, v, name):
    nl, nr, nc = w.shape

    def body(*refs):
        p_refs = refs[:nl]
        w_ref, m_ref, v_ref, g_ref, d_ref, m2_ref, v2_ref = refs[nl:]
        for k in range(nl):
            @pl.when(pl.program_id(0) == k)
            def _(p_ref=p_refs[k]):
                g = p_ref[0].astype(F32)
                for dev in range(1, N_DEV):
                    g = g + p_ref[dev].astype(F32)
                delta, m2, v2 = _adam_math(w_ref[0], g, m_ref[0], v_ref[0])
                g_ref[0] = g
                d_ref[0] = delta
                m2_ref[0] = m2
                v2_ref[0] = v2

    def part_spec(k):
        return pl.BlockSpec((N_DEV, ADAM_ROWS, nc), lambda l, i: (0, first + jnp.where(l == k, i, 0), 0))

    blk = pl.BlockSpec((1, ADAM_ROWS, nc), lambda l, i: (l, i, 0))
    shp = jax.ShapeDtypeStruct(w.shape, F32)
    return pl.pallas_call(
        body, name=name, grid=(nl, pl.cdiv(nr, ADAM_ROWS)),
        in_specs=[part_spec(k) for k in range(nl)] + [blk, blk, blk],
        out_specs=[blk, blk, blk, blk],
        out_shape=[shp, shp, shp, shp],
        compiler_params=_cparams(("arbitrary", "arbitrary")),
    )(*parts, w, m, v)


def _adam_columns(parts, first, w, m, v):
    nc, nl, d = w.shape

    def body(*refs):
        p_refs = refs[:nl]
        w_ref, m_ref, v_ref, g_ref, d_ref, m2_ref, v2_ref = refs[nl:]
        for l in range(nl):
            g = p_refs[l][0].astype(F32)
            for dev in range(1, N_DEV):
                g = g + p_refs[l][dev].astype(F32)
            delta, m2, v2 = _adam_math(w_ref[:, l, :], g, m_ref[:, l, :], v_ref[:, l, :])
            g_ref[:, l, :] = g
            d_ref[:, l, :] = delta
            m2_ref[:, l, :] = m2
            v2_ref[:, l, :] = v2

    blk = pl.BlockSpec((ADAM_COL_ROWS, nl, d), lambda i: (i, 0, 0))
    part = pl.BlockSpec((N_DEV, ADAM_COL_ROWS, d), lambda i: (0, first + i, 0))
    shp = jax.ShapeDtypeStruct(w.shape, F32)
    return pl.pallas_call(
        body, name="adam_w_in", grid=(pl.cdiv(nc, ADAM_COL_ROWS),),
        in_specs=[part] * nl + [blk, blk, blk],
        out_specs=[blk, blk, blk, blk],
        out_shape=[shp, shp, shp, shp],
        compiler_params=_cparams(("parallel",)),
    )(*parts, w, m, v)


def _adam_small(ws, gs, ms, vs):
    n = len(ws)

    def body(*refs):
        w_refs, g_refs, m_refs, v_refs, d_refs, m2_refs, v2_refs = [refs[i * n:(i + 1) * n] for i in range(7)]
        for i in range(n):
            delta, m2, v2 = _adam_math(w_refs[i][...], g_refs[i][...], m_refs[i][...], v_refs[i][...])
            d_refs[i][...] = delta
            m2_refs[i][...] = m2
            v2_refs[i][...] = v2

    shapes = [jax.ShapeDtypeStruct(w.shape, F32) for w in ws]
    out = pl.pallas_call(body, name="adam_small", out_shape=shapes * 3)(*ws, *gs, *ms, *vs)
    return out[:n], out[n:2 * n], out[2 * n:]


def _position():
    return lax.axis_index("x"), lax.axis_index("y"), lax.axis_index("c")


def _slot(p):
    return 4 * p[0] + 2 * p[1] + p[2]


BF16_TILE_ROWS = 16


def _slab_rows(rows, cols):
    return -(-(rows + cols) // BF16_TILE_ROWS) * BF16_TILE_ROWS


RELAYOUT_COLS = 1024
RELAYOUT_CHUNK = 64


def _shard_pieces(dev, rows, cols):
    moved = ((0, GA_ORIG, 0), (GA_ORIG, GA_ORIG + GLA_RANK, OFF_GA - GA_ORIG), (GA_ORIG + GLA_RANK, D_IN, -GLA_RANK))
    c0, c1 = dev * cols, (dev + 1) * cols
    return [(rows + max(c0, lo) - c0, max(c0, lo) + off, min(c1, hi) - max(c0, lo))
            for lo, hi, off in moved if max(c0, lo) < min(c1, hi)]


def _move_rows(src, src_row, dst, dst_row, n):
    assert src_row % 2 == 0 and dst_row % 2 == 0 and n % 2 == 0
    for r in range(0, n // 2, RELAYOUT_CHUNK):
        m = min(RELAYOUT_CHUNK, n // 2 - r)
        dst[dst_row // 2 + r:dst_row // 2 + r + m, :] = src[src_row // 2 + r:src_row // 2 + r + m, :]


def _aligned_weight(land, rows, cols):
    _, slab, d = land.shape
    ct = min(RELAYOUT_COLS, d)

    def body(land_ref, wt_ref, wo_ref):
        dev = pl.program_id(1)
        src = land_ref.bitcast(jnp.uint32)
        dst = wt_ref.bitcast(jnp.uint32)
        wo_ref[...] = land_ref[0:rows, :]

        @pl.when(dev == 0)
        def _():
            dst[D_IN // 2:D_ZP // 2, :] = jnp.zeros(((D_ZP - D_IN) // 2, ct), jnp.uint32)

        for k in range(N_DEV):
            @pl.when(dev == k)
            def _(k=k):
                for at, to, n in _shard_pieces(k, rows, cols):
                    _move_rows(src, at, dst, to, n)

    return pl.pallas_call(
        body, name="aligned_weight", grid=(d // ct, N_DEV),
        in_specs=[pl.BlockSpec((slab, ct), lambda c, dev: (dev, c))],
        out_specs=[pl.BlockSpec((D_ZP, ct), lambda c, dev: (0, c)),
                   pl.BlockSpec((rows, ct), lambda c, dev: (dev, c))],
        out_shape=[jax.ShapeDtypeStruct((D_ZP, d), land.dtype),
                   jax.ShapeDtypeStruct((N_DEV * rows, d), land.dtype)],
        compiler_params=_cparams(("parallel", "arbitrary")),
    )(land.reshape(N_DEV * slab, d))


def _partial_slabs(dwt, cols):
    d = dwt[0].shape[1]
    bounds = (0, GA_ORIG, OFF_GA, D_ZP)
    assert tuple(a.shape[0] for a in dwt) == tuple(hi - lo for lo, hi in zip(bounds, bounds[1:]))
    slab = _slab_rows(0, cols)
    ct = min(RELAYOUT_COLS, d)

    def body(*refs):
        out_ref = refs[-1]
        dev = pl.program_id(1)
        srcs = [ref.bitcast(jnp.uint32) for ref in refs[:-1]]
        dst = out_ref.bitcast(jnp.uint32)
        dst[cols // 2:slab // 2, :] = jnp.zeros(((slab - cols) // 2, ct), jnp.uint32)
        for k in range(N_DEV):
            @pl.when(dev == k)
            def _(k=k):
                for to, at, n in _shard_pieces(k, 0, cols):
                    which = max(i for i, lo in enumerate(bounds[:-1]) if lo <= at)
                    assert at + n <= bounds[which + 1]
                    _move_rows(srcs[which], at - bounds[which], dst, to, n)

    return pl.pallas_call(
        body, name="partial_slabs", grid=(d // ct, N_DEV),
        in_specs=[pl.BlockSpec((a.shape[0], ct), lambda c, dev: (0, c)) for a in dwt],
        out_specs=pl.BlockSpec((slab, ct), lambda c, dev: (dev, c)),
        out_shape=jax.ShapeDtypeStruct((N_DEV * slab, d), dwt[0].dtype),
        compiler_params=_cparams(("parallel", "arbitrary")),
    )(*dwt).reshape(N_DEV, slab, d)


def _peer(pos, k):
    x, y, c = pos
    return (1 - x if k & 4 else x, 1 - y if k & 2 else y, 1 - c if k & 1 else c)


HBM_SPEC = pl.BlockSpec(memory_space=pltpu.HBM)
SEM_SPEC = pl.BlockSpec(memory_space=pltpu.SEMAPHORE)
GATHER_PEERS = (1, 4, 2, 6)
ALL_PEERS = (1, 2, 3, 4, 5, 6, 7)


def _hbm(a):
    return pltpu.with_memory_space_constraint(a, pltpu.HBM)


def _split_copies(src_ref, land_ref, send_sems, recv_sems, ks, per_peer, landed):
    me = _position()
    out = []
    for i, k in enumerate(ks):
        peer = _peer(me, k)
        src = src_ref.at[_slot(peer)] if per_peer else src_ref
        dst = land_ref.at[_slot(peer) if landed else _slot(me)]
        out.append(pltpu.make_async_remote_copy(
            src_ref=src, dst_ref=dst, send_sem=send_sems.at[i], recv_sem=recv_sems.at[i],
            device_id=peer, device_id_type=MESH))
    return out


def _exchange_start(src, after, ks, per_peer, name):
    slab = src.shape[1:] if per_peer else src.shape
    land_shape = (N_DEV,) + tuple(slab)
    n = len(ks)

    def body(src_ref, land_ref, after_ref, send_sems, recv_sems, src_thru, land_thru, token):
        for cp in _split_copies(src_ref, land_ref, send_sems, recv_sems, ks, per_peer, landed=False):
            cp.start()
        token[...] = jnp.zeros_like(token)

    return pl.pallas_call(
        body, name=name,
        out_shape=(pltpu.SemaphoreType.DMA((n,)), pltpu.SemaphoreType.DMA((n,)),
                   pltpu.HBM(src.shape, src.dtype), pltpu.HBM(land_shape, src.dtype),
                   jax.ShapeDtypeStruct((8, LANE), F32)),
        in_specs=(HBM_SPEC, HBM_SPEC, ANY),
        out_specs=(SEM_SPEC, SEM_SPEC, HBM_SPEC, HBM_SPEC, pl.BlockSpec(memory_space=pltpu.VMEM)),
        input_output_aliases={0: 2, 1: 3},
        compiler_params=pltpu.CompilerParams(has_side_effects=pltpu.SideEffectType.DATAFLOW_SIDE_EFFECTING),
    )(_hbm(src), _hbm(lax.empty(land_shape, src.dtype)), after)


def _exchange_wait(started, after, ks, per_peer, name):
    send_sems, recv_sems, src_thru, land_thru = started

    def body(src_ref, land_ref, send_sems, recv_sems, after_ref, src_dead, land_out):
        for cp in _split_copies(src_ref, land_ref, send_sems, recv_sems, ks, per_peer, landed=True):
            cp.wait_send()
            cp.wait_recv()

    return pl.pallas_call(
        body, name=name,
        out_shape=(pltpu.HBM(src_thru.shape, src_thru.dtype), pltpu.HBM(land_thru.shape, land_thru.dtype)),
        in_specs=(HBM_SPEC, HBM_SPEC, SEM_SPEC, SEM_SPEC, ANY), out_specs=(HBM_SPEC, HBM_SPEC),
        input_output_aliases={0: 0, 1: 1},
        compiler_params=pltpu.CompilerParams(has_side_effects=pltpu.SideEffectType.DATAFLOW_SIDE_EFFECTING),
    )(src_thru, land_thru, send_sems, recv_sems, after)


def _relay_copies(land_ref, send_sems, recv_sems, landed):
    me = _position()
    sibling = _peer(me, 1)
    out = []
    for i, k in enumerate(GATHER_PEERS[1:]):
        blk = land_ref.at[_slot(_peer(sibling if landed else me, k))]
        out.append(pltpu.make_async_remote_copy(
            src_ref=blk, dst_ref=blk, send_sem=send_sems.at[i], recv_sem=recv_sems.at[i],
            device_id=sibling, device_id_type=MESH))
    return out


def _relay_start(land, name):
    n = len(GATHER_PEERS) - 1

    def body(land_ref, send_sems, recv_sems, land_thru, token):
        for cp in _relay_copies(land_ref, send_sems, recv_sems, landed=False):
            cp.start()
        token[...] = jnp.zeros_like(token)

    return pl.pallas_call(
        body, name=name,
        out_shape=(pltpu.SemaphoreType.DMA((n,)), pltpu.SemaphoreType.DMA((n,)),
                   pltpu.HBM(land.shape, land.dtype), jax.ShapeDtypeStruct((8, LANE), F32)),
        in_specs=(HBM_SPEC,),
        out_specs=(SEM_SPEC, SEM_SPEC, HBM_SPEC, pl.BlockSpec(memory_space=pltpu.VMEM)),
        input_output_aliases={0: 2},
        compiler_params=pltpu.CompilerParams(has_side_effects=pltpu.SideEffectType.DATAFLOW_SIDE_EFFECTING),
    )(_hbm(land))


def _relay_wait(started, after, name):
    send_sems, recv_sems, land_thru = started

    def body(land_ref, send_sems, recv_sems, after_ref, land_out):
        for cp in _relay_copies(land_ref, send_sems, recv_sems, landed=True):
            cp.wait_send()
            cp.wait_recv()

    return pl.pallas_call(
        body, name=name,
        out_shape=pltpu.HBM(land_thru.shape, land_thru.dtype),
        in_specs=(HBM_SPEC, SEM_SPEC, SEM_SPEC, ANY), out_specs=HBM_SPEC,
        input_output_aliases={0: 0},
        compiler_params=pltpu.CompilerParams(has_side_effects=pltpu.SideEffectType.DATAFLOW_SIDE_EFFECTING),
    )(land_thru, send_sems, recv_sems, after)


def _share(vec, name, after=None):
    follows = [] if after is None else [after]

    def body(vec_ref, *rest):
        out_ref, send_sems, recv_sems, local_sem = rest[len(follows):]
        me = _position()

        def copy(k, landed):
            peer = _peer(me, k)
            return pltpu.make_async_remote_copy(
                src_ref=vec_ref, dst_ref=out_ref.at[_slot(peer) if landed else _slot(me)],
                send_sem=send_sems.at[k - 1], recv_sem=recv_sems.at[k - 1], device_id=peer, device_id_type=MESH)

        mine = pltpu.make_async_copy(vec_ref, out_ref.at[_slot(me)], local_sem)
        mine.start()
        sent = [copy(k, False) for k in ALL_PEERS]
        for cp in sent:
            cp.start()
        for k in ALL_PEERS:
            copy(k, True).wait_recv()
        for cp in sent:
            cp.wait_send()
        mine.wait()

    return pl.pallas_call(
        body, name=name,
        in_specs=[ANY] * (1 + len(follows)), out_specs=ANY,
        out_shape=jax.ShapeDtypeStruct((N_DEV,) + vec.shape, vec.dtype),
        scratch_shapes=[pltpu.SemaphoreType.DMA((N_DEV - 1,)), pltpu.SemaphoreType.DMA((N_DEV - 1,)),
                        pltpu.SemaphoreType.DMA],
    )(vec, *follows)


def _sum_slots(parts):
    def body(p_ref, o_ref):
        acc = p_ref[0]
        for dev in range(1, N_DEV):
            acc = acc + p_ref[dev]
        o_ref[...] = acc

    return pl.pallas_call(body, name="sum_slots",
                          out_shape=jax.ShapeDtypeStruct(parts.shape[1:], F32))(parts)


PACK_ROWS = 8


def _packed_rows(size):
    return -(-size // (PACK_ROWS * LANE)) * PACK_ROWS


def _pack(arrs):
    def rows(a):
        flat = a.reshape(-1)
        return jnp.pad(flat, (0, _packed_rows(flat.shape[0]) * LANE - flat.shape[0])).reshape(-1, LANE)

    return jnp.concatenate([rows(a) for a in arrs], axis=0)


def _unpack(packed, shapes):
    out, at = [], 0
    for shp in shapes:
        size = 1
        for dim in shp:
            size *= dim
        nrows = _packed_rows(size)
        out.append(packed[at:at + nrows].reshape(-1)[:size].reshape(shp))
        at += nrows
    return out


def _layer_fwd(x, wt, wo, g_pre, g_post, wa_pad, b_alpha, g_gla, g_att, rb_pad, midway=None):
    h = _rms_fwd(x, g_pre)
    z = _matmul(h, wt, "nt", F32, *TILES["in_proj"], "in_proj", n_outer=True)
    y_gla, o_gla, states = _gla_fwd(z, wa_pad, b_alpha, g_gla)
    if midway is not None:
        g_att = g_att + midway(y_gla)[:1, :1]
    y_att, o_att = _att_fwd(z, rb_pad, g_att)
    y = _matmul_cols([y_gla, y_att], wo, F32, *TILES["out_proj"][:2], "out_proj")
    out = _post_fwd(x, y, g_post)
    return out, (x, h, z, o_gla, states, o_att, y_gla, y_att, y)


def _layer_bwd(dout, saved, wt, wo, g_pre, g_post, wa_pad, b_alpha, g_gla, g_att, rb_pad, on_dwo, on_dwt):
    x, h, z, o_gla, states, o_att, y_gla, y_att, y = saved
    dy, dg_post = _post_bwd(dout, y, g_post)
    dwo = _matmul_rows([y_gla, y_att], dy, BF16, *TILES["out_proj_dw"][:2], "out_proj_dw")
    token = on_dwo(dwo)
    dycat = _matmul(dy, wo, "nt", F32, *TILES["out_proj_dx"], "out_proj_dx", n_outer=True, after=token)
    dq, dk, dv, dgg, dga, dwa, db, dg_gla = _gla_bwd(dycat, o_gla, z, wa_pad, b_alpha, g_gla, states)
    daq, dak, dav, dag, drb, dg_att = _att_bwd(dycat, o_att, z, rb_pad, g_att)
    tw, tn = TILES["in_proj_dw"][:2]
    dwt = (_matmul_rows([dq, dk, dv, dgg], h, BF16, tw, tn, "in_proj_dw_gla"),
           _matmul_rows([daq, dak, dav, dag], h, BF16, tw, tn, "in_proj_dw_att"),
           _matmul_rows([dga], h, BF16, LANE, tn, "in_proj_dw_gate"))
    token = on_dwt(dwt)
    dh = _matmul_cols([dq, dk, dv, dgg, daq, dak, dav, dag, dga], wt, F32, *TILES["in_proj_dx"][:2],
                      "in_proj_dx", after=token)
    dx, dg_pre = _pre_bwd(dh, x, g_pre, dout)
    small = (dg_pre[0], dg_post[0], dwa[:GLA_RANK], db[0], dg_gla[0], dg_att[0], drb[:, 0, :N_REL])
    return dx, small


def kernel(x, w_in, w_out, g_pre, g_post, w_alpha, b_alpha, g_gla, g_att, rel_bias, loss_target, m_w_in, m_w_out, m_g_pre, m_g_post, m_w_alpha, m_b_alpha, m_g_gla, m_g_att, m_rel_bias, v_w_in, v_w_out, v_g_pre, v_g_post, v_w_alpha, v_b_alpha, v_g_gla, v_g_att, v_rel_bias):
    nl, d, cols = w_in.shape
    rows = w_out.shape[1]
    s = x.shape[1]
    x0 = x.reshape(s, d)
    tgt = loss_target.reshape(s, d)

    cols_first = lambda a: jnp.transpose(a, (2, 0, 1))
    w_c = cols_first(w_in)
    slab = _slab_rows(rows, cols)
    is_out = lax.broadcasted_iota(jnp.int32, (slab, d), 0) < rows

    def shard(l, zero=0.0):
        top = jnp.pad((w_out[l] + zero).astype(BF16), ((0, slab - rows), (0, 0)))
        rest = jnp.pad((w_c[:, l] + zero).astype(BF16), ((rows, slab - rows - cols), (0, 0)))
        return jnp.where(is_out, top, rest)

    first_fetch = _exchange_start(shard(0), x, GATHER_PEERS, False, "gather_start_0")
    began = first_fetch[4][0, 0]
    shards = [None] + [shard(l, began) for l in range(1, nl)]
    alpha = _pack([w_alpha]) + began
    wa_g = _share(alpha, "gather_alpha")
    wa_cols = w_alpha.shape[2]
    wa_full = wa_g.reshape(N_DEV, -1)[:, :nl * GLA_RANK * wa_cols].reshape(N_DEV, nl, GLA_RANK, wa_cols)
    wa_full = jnp.transpose(wa_full, (1, 2, 0, 3)).reshape(nl, GLA_RANK, GLA_KW)
    wa_pad = jnp.pad(wa_full, ((0, 0), (0, LANE - GLA_RANK), (0, 0)))
    rb_pad = jnp.pad(rel_bias, ((0, 0), (0, 0), (0, 3 * LANE - N_REL)))

    def layer_args(l, follows=None):
        gp = g_pre[l:l + 1] if follows is None else g_pre[l:l + 1] + follows[:1, :1]
        return (wts[l], wos[l], gp, g_post[l:l + 1], wa_pad[l], b_alpha[l:l + 1], g_gla[l:l + 1],
                g_att[l:l + 1], rb_pad[l])

    my = _slot(_position())

    def fetch(l, after):
        return _exchange_start(shards[l], after, GATHER_PEERS, False, f"gather_start_{l}")

    def relay(l, first_hop, after):
        own[l], land = _exchange_wait(first_hop[:4], after, GATHER_PEERS, False, f"gather_wait_{l}")
        return _relay_start(land, f"relay_start_{l}")

    def midway(l, y):
        flight["relay"] = relay(l + 1, flight["fetch"], y)
        if l + 2 >= nl:
            return flight["relay"][3]
        flight["fetch"] = fetch(l + 2, flight["relay"][2])
        return flight["fetch"][4]

    act, saved, wts, wos, flight, own = x0, [], [], [], {}, [None] * nl
    prepared = (wa_pad[0, :1, :1] + sum(sh[:1, :1].astype(F32) for sh in shards[1:]))
    flight["relay"] = relay(0, first_fetch, prepared)
    if nl > 1:
        flight["fetch"] = fetch(1, flight["relay"][2])
    for l in range(nl):
        land = _relay_wait(flight["relay"][:3], act, f"relay_wait_{l}")
        land = lax.dynamic_update_slice_in_dim(land, own[l][None], my, 0)
        wt_l, wo_l = _aligned_weight(land, rows, cols)
        wts.append(wt_l)
        wos.append(wo_l)
        act, sv = _layer_fwd(act, *layer_args(l, follows=first_fetch[4] if l == 0 else None),
                             midway=functools.partial(midway, l) if l + 1 < nl else None)
        saved.append(sv)
    dout, sq = _loss_head(act, tgt)
    loss = lax.psum(sq[0, 0] * (0.5 / d), ("x", "y", "c"))

    smalls, pending_out, pending_in = [None] * nl, [None] * nl, [None] * nl

    def send_out(l, dwo):
        pending_out[l] = _exchange_start(dwo.reshape(N_DEV, rows, d), dwo[:1, :1], ALL_PEERS, True,
                                         f"scatter_out_start_{l}")
        return pending_out[l][4]

    def send_in(l, dwt):
        pending_in[l] = _exchange_start(_partial_slabs(dwt, cols), dwt[-1], ALL_PEERS, True,
                                        f"scatter_in_start_{l}")
        return pending_in[l][4]

    for l in reversed(range(nl)):
        dout, smalls[l] = _layer_bwd(dout, saved[l], *layer_args(l), on_dwo=functools.partial(send_out, l),
                                     on_dwt=functools.partial(send_in, l))
    grad_x = dout.reshape(x.shape)

    def landed(started, after, name):
        partial, land = _exchange_wait(started[:4], after, ALL_PEERS, True, name)
        return lax.dynamic_update_slice_in_dim(land, lax.dynamic_slice_in_dim(partial, my, 1, 0), my, 0)

    parts_out = [landed(pending_out[l], dout, f"scatter_out_wait_{l}") for l in range(nl)]
    g_w_out, d_w_out, m2_w_out, v2_w_out = _adam_sharded(parts_out, 0, w_out, m_w_out, v_w_out, "adam_w_out")
    names = 7
    small_stacked = [jnp.stack([smalls[l][i] for l in range(nl)]) for i in range(names)]
    shapes = [a.shape for a in small_stacked]
    gathered = _share(_pack(small_stacked), "gather_small_grads", after=d_w_out)
    g_pre_g, g_post_g, wa_g_full, b_g, gla_g, att_g, rb_g = _unpack(_sum_slots(gathered), shapes)
    wa_g_mine = lax.dynamic_slice_in_dim(wa_g_full, my * wa_cols, wa_cols, axis=2)
    grads = [g_pre_g, g_post_g, wa_g_mine, b_g, gla_g, att_g, rb_g]
    ws = [g_pre, g_post, w_alpha, b_alpha, g_gla, g_att, rel_bias]
    ms = [m_g_pre, m_g_post, m_w_alpha, m_b_alpha, m_g_gla, m_g_att, m_rel_bias]
    vs = [v_g_pre, v_g_post, v_w_alpha, v_b_alpha, v_g_gla, v_g_att, v_rel_bias]
    d_s, m2_s, v2_s = _adam_small(ws, grads, ms, vs)

    parts_in = [landed(pending_in[l], d_s[0], f"scatter_in_wait_{l}") for l in range(nl)]
    g_w_in, d_w_in, m2_w_in, v2_w_in = [
        jnp.transpose(a, (1, 2, 0))
        for a in _adam_columns(parts_in, 0, w_c, cols_first(m_w_in), cols_first(v_w_in))]

    def ordered(big_in, big_out, small):
        return [big_in, big_out] + list(small)

    return (loss, grad_x,
            *ordered(g_w_in, g_w_out, grads),
            *ordered(d_w_in, d_w_out, d_s),
            *ordered(m2_w_in, m2_w_out, m2_s),
            *ordered(v2_w_in, v2_w_out, v2_s))
```

```python
import functools

import jax
import jax.numpy as jnp
from jax import lax
from jax.experimental import pallas as pl
from jax.experimental.pallas import tpu as pltpu

F32 = jnp.float32
BF16 = jnp.bfloat16
MESH = pl.DeviceIdType.MESH
ANY = pl.BlockSpec(memory_space=pl.ANY)

CHUNK = 64
GLA_HEADS = 4
GLA_DK = 128
GLA_DV = 256
GLA_KW = GLA_HEADS * GLA_DK
D_GLA = GLA_HEADS * GLA_DV
GLA_RANK = 16
GLA_TAU = 16.0
ATT_HEADS = 8
ATT_HD = 128
D_ATT = ATT_HEADS * ATT_HD
LEFT_CHUNKS = 8
REL_CLIP = 128
N_REL = 2 * REL_CLIP + 1
EPS = 1e-6
D_IN = 2 * GLA_KW + 2 * D_GLA + GLA_RANK + 4 * D_ATT
GLA_SCALE = GLA_DK ** -0.5
ATT_SCALE = ATT_HD ** -0.5

ADAM_LR = 0.001
ADAM_B1 = 0.9
ADAM_B2 = 0.999
ADAM_EPS = 1e-08
ADAM_WD = 0.01
ADAM_STEP = 10

N_DEV = 8
LANE = 128
GA_ORIG = 2 * GLA_KW + 2 * D_GLA
OFF_AQ = GA_ORIG
OFF_GA = GA_ORIG + 4 * D_ATT
D_ZP = OFF_GA + LANE
QB = 2 * CHUNK
ATT_UNROLL = 8
WIN = (LEFT_CHUNKS + 2) * CHUNK
ET_ROWS = WIN + LEFT_CHUNKS * CHUNK
NEG = -1e30
VMEM_LIMIT = 48 * 1024 * 1024


def _cparams(sem):
    return pltpu.CompilerParams(dimension_semantics=sem, vmem_limit_bytes=VMEM_LIMIT)


def _dot(a, b):
    return jnp.dot(a, b, preferred_element_type=F32)


def _dot_nt(a, b):
    return lax.dot_general(a, b, (((1,), (1,)), ((), ())), preferred_element_type=F32)


def _dot_tn(a, b):
    return lax.dot_general(a, b, (((0,), (0,)), ((), ())), preferred_element_type=F32)


def _dot01(t, x, left=True):
    if not left:
        t, x = x, t
    hi = x.astype(BF16)
    r = x - hi.astype(F32)
    mid = r.astype(BF16)
    lo = (r - mid.astype(F32)).astype(BF16)
    if left:
        return _dot(t, hi) + _dot(t, mid) + _dot(t, lo)
    return _dot(hi, t) + _dot(mid, t) + _dot(lo, t)


def _sigmoid(x):
    return 1.0 / (1.0 + jnp.exp(-x))


def _log_sigmoid(x):
    return jnp.minimum(x, 0.0) - jnp.log(1.0 + jnp.exp(-jnp.abs(x)))


TILES = {
    "in_proj": (512, D_ZP // 3, None),
    "in_proj_dx": (512, 512, None),
    "in_proj_dw": (512, 2048, None),
    "out_proj": (512, 1024, None),
    "out_proj_dx": (512, 1024, None),
    "out_proj_dw": (1024, 1024, None),
}


def _matmul(a, b, mode, out_dtype, tm, tn, tk, name, n_outer=False, after=None):
    if mode == "nn":
        (m, k), n = a.shape, b.shape[1]
    elif mode == "nt":
        (m, k), n = a.shape, b.shape[0]
    else:
        (k, m), n = a.shape, b.shape[1]
    tm, tn, tk = min(tm, m), min(tn, n), k if tk is None else min(tk, k)
    assert m % tm == 0 and n % tn == 0 and k % tk == 0, (name, m, n, k)
    nk = k // tk
    dot = {"nn": _dot, "nt": _dot_nt, "tn": _dot_tn}[mode]

    follows = [] if after is None else [after]

    def body_whole_k(a_ref, b_ref, *rest):
        o_ref = rest[-1]
        o_ref[...] = dot(a_ref[...], b_ref[...]).astype(out_dtype)

    def body(a_ref, b_ref, *rest):
        o_ref, acc_ref = rest[-2:]
        kk = pl.program_id(2)

        @pl.when(kk == 0)
        def _():
            acc_ref[...] = jnp.zeros_like(acc_ref)

        acc_ref[...] += dot(a_ref[...], b_ref[...])

        @pl.when(kk == nk - 1)
        def _():
            o_ref[...] = acc_ref[...].astype(out_dtype)

    def at(index):
        return (lambda j, i, kk: index(i, j, kk)) if n_outer else index

    if mode == "tn":
        a_spec = pl.BlockSpec((tk, tm), at(lambda i, j, kk: (kk, i)))
    else:
        a_spec = pl.BlockSpec((tm, tk), at(lambda i, j, kk: (i, kk)))
    if mode == "nt":
        b_spec = pl.BlockSpec((tn, tk), at(lambda i, j, kk: (j, kk)))
    else:
        b_spec = pl.BlockSpec((tk, tn), at(lambda i, j, kk: (kk, j)))
    return pl.pallas_call(
        body_whole_k if nk == 1 else body, name=name,
        grid=(n // tn, m // tm, nk) if n_outer else (m // tm, n // tn, nk),
        in_specs=[a_spec, b_spec] + [ANY] * len(follows),
        out_specs=pl.BlockSpec((tm, tn), at(lambda i, j, kk: (i, j))),
        out_shape=jax.ShapeDtypeStruct((m, n), out_dtype),
        scratch_shapes=[] if nk == 1 else [pltpu.VMEM((tm, tn), F32)],
        compiler_params=_cparams(("parallel", "parallel", "arbitrary")),
    )(a, b, *follows)


def _matmul_cols(pieces, b, out_dtype, tm, tn, name, after=None):
    m, n = pieces[0].shape[0], b.shape[1]
    widths = [p.shape[1] for p in pieces]
    starts = [sum(widths[:i]) for i in range(len(pieces))]
    follows = [] if after is None else [after]
    tm, tn = min(tm, m), min(tn, n)
    assert sum(widths) == b.shape[0] and m % tm == 0 and n % tn == 0, name

    def body(*refs):
        b_ref, o_ref = refs[len(pieces)], refs[-1]
        acc = None
        for p_ref, at, width in zip(refs, starts, widths):
            part = _dot(p_ref[...], b_ref[at:at + width, :])
            acc = part if acc is None else acc + part
        o_ref[...] = acc.astype(out_dtype)

    return pl.pallas_call(
        body, name=name, grid=(n // tn, m // tm),
        in_specs=[pl.BlockSpec((tm, width), lambda j, i: (i, 0)) for width in widths]
        + [pl.BlockSpec((b.shape[0], tn), lambda j, i: (0, j))] + [ANY] * len(follows),
        out_specs=pl.BlockSpec((tm, tn), lambda j, i: (i, j)),
        out_shape=jax.ShapeDtypeStruct((m, n), out_dtype),
        compiler_params=_cparams(("parallel", "parallel")),
    )(*pieces, b, *follows)


def _matmul_rows(pieces, b, out_dtype, tw, tn, name):
    k, n = b.shape
    tn = min(tn, n)
    counts = [p.shape[1] // tw for p in pieces]
    firsts = [sum(counts[:i]) for i in range(len(pieces))]
    assert all(p.shape[1] % tw == 0 for p in pieces) and n % tn == 0, name

    def body(*refs):
        b_ref, o_ref = refs[len(pieces):]
        for p_ref, first, count in zip(refs, firsts, counts):
            @pl.when((pl.program_id(0) >= first) & (pl.program_id(0) < first + count))
            def _(p_ref=p_ref):
                o_ref[...] = _dot_tn(p_ref[...], b_ref[...]).astype(out_dtype)

    def piece_spec(first, count):
        return pl.BlockSpec((k, tw), lambda i, j: (0, jnp.clip(i - first, 0, count - 1)))

    return pl.pallas_call(
        body, name=name, grid=(sum(counts), n // tn),
        in_specs=[piece_spec(first, count) for first, count in zip(firsts, counts)]
        + [pl.BlockSpec((k, tn), lambda i, j: (0, j))],
        out_specs=pl.BlockSpec((tw, tn), lambda i, j: (i, j)),
        out_shape=jax.ShapeDtypeStruct((sum(counts) * tw, n), out_dtype),
        compiler_params=_cparams(("parallel", "parallel")),
    )(*pieces, b)


ROWS = 256


def _rms_fwd(x, g):
    s, d = x.shape

    def body(x_ref, g_ref, h_ref):
        xv = x_ref[...]
        r = lax.rsqrt(jnp.mean(xv * xv, axis=-1, keepdims=True) + EPS)
        h_ref[...] = (xv * r * g_ref[...]).astype(BF16)

    return pl.pallas_call(
        body, name="rms_fwd", grid=(s // ROWS,),
        in_specs=[pl.BlockSpec((ROWS, d), lambda i: (i, 0)), pl.BlockSpec((1, d), lambda i: (0, 0))],
        out_specs=pl.BlockSpec((ROWS, d), lambda i: (i, 0)),
        out_shape=jax.ShapeDtypeStruct((s, d), BF16),
        compiler_params=_cparams(("parallel",)),
    )(x, g)


def _post_fwd(x, y, g):
    s, d = x.shape

    def body(x_ref, y_ref, g_ref, o_ref):
        yv = y_ref[...]
        r = lax.rsqrt(jnp.mean(yv * yv, axis=-1, keepdims=True) + EPS)
        o_ref[...] = x_ref[...] + yv * r * g_ref[...]

    row = pl.BlockSpec((ROWS, d), lambda i: (i, 0))
    return pl.pallas_call(
        body, name="post_fwd", grid=(s // ROWS,),
        in_specs=[row, row, pl.BlockSpec((1, d), lambda i: (0, 0))],
        out_specs=row,
        out_shape=jax.ShapeDtypeStruct((s, d), F32),
        compiler_params=_cparams(("parallel",)),
    )(x, y, g)


def _loss_head(out, tgt):
    s, d = out.shape

    def body(o_ref, t_ref, dout_ref, sum_ref):
        @pl.when(pl.program_id(0) == 0)
        def _():
            sum_ref[...] = jnp.zeros_like(sum_ref)

        e = o_ref[...] - t_ref[...]
        dout_ref[...] = e * (1.0 / d)
        sum_ref[...] += jnp.sum(jnp.sum(e * e, axis=1, keepdims=True), axis=0, keepdims=True)

    row = pl.BlockSpec((ROWS, d), lambda i: (i, 0))
    return pl.pallas_call(
        body, name="loss_head", grid=(s // ROWS,),
        in_specs=[row, row],
        out_specs=[row, pl.BlockSpec((1, 1), lambda i: (0, 0))],
        out_shape=[jax.ShapeDtypeStruct((s, d), F32), jax.ShapeDtypeStruct((1, 1), F32)],
        compiler_params=_cparams(("arbitrary",)),
    )(out, tgt)


def _post_bwd(dout, y, g):
    s, d = y.shape

    def body(do_ref, y_ref, g_ref, dy_ref, dg_ref):
        @pl.when(pl.program_id(0) == 0)
        def _():
            dg_ref[...] = jnp.zeros_like(dg_ref)

        yv = y_ref[...]
        dv = do_ref[...]
        r = lax.rsqrt(jnp.mean(yv * yv, axis=-1, keepdims=True) + EPS)
        dg_ref[...] += jnp.sum(dv * yv * r, axis=0, keepdims=True)
        w = dv * g_ref[...]
        dy = r * (w - yv * (r * r) * jnp.mean(w * yv, axis=-1, keepdims=True))
        dy_ref[...] = dy.astype(BF16)

    row = pl.BlockSpec((ROWS, d), lambda i: (i, 0))
    vec = pl.BlockSpec((1, d), lambda i: (0, 0))
    return pl.pallas_call(
        body, name="post_bwd", grid=(s // ROWS,),
        in_specs=[row, row, vec],
        out_specs=[row, vec],
        out_shape=[jax.ShapeDtypeStruct((s, d), BF16), jax.ShapeDtypeStruct((1, d), F32)],
        compiler_params=_cparams(("arbitrary",)),
    )(dout, y, g)


def _pre_bwd(dh, x, g, dout):
    s, d = x.shape

    def body(dh_ref, x_ref, g_ref, do_ref, dx_ref, dg_ref):
        @pl.when(pl.program_id(0) == 0)
        def _():
            dg_ref[...] = jnp.zeros_like(dg_ref)

        xv = x_ref[...]
        dv = dh_ref[...]
        r = lax.rsqrt(jnp.mean(xv * xv, axis=-1, keepdims=True) + EPS)
        dg_ref[...] += jnp.sum(dv * xv * r, axis=0, keepdims=True)
        w = dv * g_ref[...]
        dx_ref[...] = do_ref[...] + r * (w - xv * (r * r) * jnp.mean(w * xv, axis=-1, keepdims=True))

    row = pl.BlockSpec((ROWS, d), lambda i: (i, 0))
    vec = pl.BlockSpec((1, d), lambda i: (0, 0))
    return pl.pallas_call(
        body, name="pre_bwd", grid=(s // ROWS,),
        in_specs=[row, row, vec, row],
        out_specs=[row, vec],
        out_shape=[jax.ShapeDtypeStruct((s, d), F32), jax.ShapeDtypeStruct((1, d), F32)],
        compiler_params=_cparams(("arbitrary",)),
    )(dh, x, g, dout)


GLA_STEP = 4
GLA_ROWS = GLA_STEP * CHUNK
GLA_CHUNKS = [slice(c * CHUNK, (c + 1) * CHUNK) for c in range(GLA_STEP)]


def _chunk_triangles():
    ri = lax.broadcasted_iota(jnp.int32, (GLA_ROWS, GLA_ROWS), 0)
    ci = lax.broadcasted_iota(jnp.int32, (GLA_ROWS, GLA_ROWS), 1)
    same = (ri // CHUNK) == (ci // CHUNK)
    return (jnp.where(same & (ri >= ci), 1.0, 0.0).astype(BF16), jnp.where(same & (ci >= ri), 1.0, 0.0).astype(BF16))


def _per_chunk(fn, like):
    row = lax.broadcasted_iota(jnp.int32, like.shape, 0)
    return [fn((row >= c * CHUNK) & (row < (c + 1) * CHUNK)) for c in range(GLA_STEP)]


def _spread(per_chunk, like):
    row = lax.broadcasted_iota(jnp.int32, like.shape, 0)
    out = per_chunk[-1]
    for c in reversed(range(GLA_STEP - 1)):
        out = jnp.where(row < (c + 1) * CHUNK, per_chunk[c], out)
    return out


def _gla_gate(ga_b, wa_b, b_ref, tri):
    pre = _dot(ga_b, wa_b) + b_ref[...]
    la = _log_sigmoid(pre) * (1.0 / GLA_TAU)
    cum = _dot01(tri, la)
    row = lax.broadcasted_iota(jnp.int32, cum.shape, 0)
    cends = [jnp.sum(jnp.where(row == (c + 1) * CHUNK - 1, cum, 0.0), axis=0, keepdims=True)
             for c in range(GLA_STEP)]
    return pre, cum, cends


def _heads(width):
    return [slice(h * width, (h + 1) * width) for h in range(GLA_HEADS)]


def _z_specs_gla(rev=None):
    idx = (lambda n: n) if rev is None else rev
    return [
        pl.BlockSpec((GLA_ROWS, GLA_KW), lambda n: (idx(n), 0)),
        pl.BlockSpec((GLA_ROWS, GLA_KW), lambda n: (idx(n), 1)),
        pl.BlockSpec((GLA_ROWS, D_GLA), lambda n: (idx(n), 1)),
        pl.BlockSpec((GLA_ROWS, D_GLA), lambda n: (idx(n), 2)),
        pl.BlockSpec((GLA_ROWS, LANE), lambda n: (idx(n), OFF_GA // LANE)),
    ]


def _gla_fwd(z, wa_pad, b_alpha, g_gla):
    s = z.shape[0]
    nchunk = s // CHUNK

    def body(q_ref, k_ref, v_ref, gg_ref, ga_ref, wa_ref, b_ref, g_ref, y_ref, o_ref, st_ref, state):
        @pl.when(pl.program_id(0) == 0)
        def _():
            state[...] = jnp.zeros_like(state)

        ga_b = ga_ref[...].astype(BF16)
        tri, _ = _chunk_triangles()
        nh = range(GLA_HEADS)
        keys, vals = _heads(GLA_DK), _heads(GLA_DV)
        _, cum, cends = _gla_gate(ga_b, wa_ref[...].astype(BF16), b_ref, tri)
        kd_b = (k_ref[...] * jnp.exp(_spread(cends, cum) - cum)).astype(BF16)
        qs = (q_ref[...] * GLA_SCALE).astype(BF16)
        v_b = v_ref[...].astype(BF16)
        uts = [[_dot_tn(v_b[rs, vals[h]], kd_b[rs, keys[h]]) for h in nh] for rs in GLA_CHUNKS]
        sts, prev = [], [state[h] for h in nh]
        for c in range(GLA_STEP):
            a = jnp.exp(cends[c])
            prev = [prev[h] * a[:, keys[h]] + uts[c][h] for h in nh]
            sts.append(prev)
        for h in nh:
            state[h] = prev[h]
            for c in range(GLA_STEP):
                st_ref[c, h] = sts[c][h]
        outs = [[_dot_nt(qs[rs, keys[h]], sts[c][h].astype(BF16)) for h in nh] for c, rs in enumerate(GLA_CHUNKS)]
        for h in nh:
            o, vs = jnp.concatenate([outs[c][h] for c in range(GLA_STEP)], axis=0), vals[h]
            o_ref[:, vs] = o
            r = lax.rsqrt(jnp.mean(o * o, axis=-1, keepdims=True) + EPS)
            gg = gg_ref[:, vs]
            y_ref[:, vs] = (o * r * g_ref[:, vs] * (gg * _sigmoid(gg))).astype(BF16)

    full = lambda shape: pl.BlockSpec(shape, lambda n: tuple(0 for _ in shape))
    wide = pl.BlockSpec((GLA_ROWS, D_GLA), lambda n: (n, 0))
    return pl.pallas_call(
        body, name="gla_fwd", grid=(nchunk // GLA_STEP,),
        in_specs=_z_specs_gla() + [full((LANE, GLA_KW)), full((1, GLA_KW)), full((1, D_GLA))],
        out_specs=[wide, wide, pl.BlockSpec((GLA_STEP, GLA_HEADS, GLA_DV, GLA_DK), lambda n: (n, 0, 0, 0))],
        out_shape=[jax.ShapeDtypeStruct((s, D_GLA), BF16), jax.ShapeDtypeStruct((s, D_GLA), F32),
                   jax.ShapeDtypeStruct((nchunk, GLA_HEADS, GLA_DV, GLA_DK), F32)],
        scratch_shapes=[pltpu.VMEM((GLA_HEADS, GLA_DV, GLA_DK), F32)],
        compiler_params=_cparams(("arbitrary",)),
    )(z, z, z, z, z, wa_pad, b_alpha, g_gla)


def _gla_bwd(dyc, o_gla, z, wa_pad, b_alpha, g_gla, states):
    s = z.shape[0]
    nsteps = s // GLA_ROWS
    rev = lambda n: nsteps - 1 - n

    def body(dy_ref, o_ref, q_ref, k_ref, v_ref, gg_ref, ga_ref, wa_ref, b_ref, g_ref, st_ref, stp_ref,
             dq_ref, dk_ref, dv_ref, dgg_ref, dga_ref, dwa_ref, db_ref, dg_ref, carry):
        step = pl.program_id(0)

        @pl.when(step == 0)
        def _():
            carry[...] = jnp.zeros_like(carry)
            dwa_ref[...] = jnp.zeros_like(dwa_ref)
            db_ref[...] = jnp.zeros_like(db_ref)
            dg_ref[...] = jnp.zeros_like(dg_ref)

        has_prev = (step < nsteps - 1).astype(F32)
        ga_b = ga_ref[...].astype(BF16)
        tri, tri_up = _chunk_triangles()
        nh, nc = range(GLA_HEADS), range(GLA_STEP)
        keys, vals = _heads(GLA_DK), _heads(GLA_DV)
        wa_b = wa_ref[...].astype(BF16)
        pre, cum, cends = _gla_gate(ga_b, wa_b, b_ref, tri)
        e = jnp.exp(_spread(cends, cum) - cum)
        a = [jnp.exp(cends[c]) for c in nc]
        kf = k_ref[...]
        kd_b = (kf * e).astype(BF16)
        v_b = v_ref[...].astype(BF16)
        qs = (q_ref[...] * GLA_SCALE).astype(BF16)
        do_b = []
        for h in nh:
            vs = vals[h]
            o = o_ref[:, vs]
            gg = gg_ref[:, vs]
            g = g_ref[:, vs]
            dy = dy_ref[:, vs]
            r = lax.rsqrt(jnp.mean(o * o, axis=-1, keepdims=True) + EPS)
            sg = _sigmoid(gg)
            dogn = dy * (gg * sg)
            dgg_ref[:, vs] = (dy * (o * r * g) * (sg * (1.0 + gg * (1.0 - sg)))).astype(BF16)
            dg_ref[:, vs] += jnp.sum(dogn * o * r, axis=0, keepdims=True)
            w = dogn * g
            do_b.append((r * (w - o * (r * r) * jnp.mean(w * o, axis=-1, keepdims=True))).astype(BF16))
        dqs = [jnp.concatenate([_dot(do_b[h][rs], st_ref[c, h].astype(BF16)) for c, rs in enumerate(GLA_CHUNKS)],
                               axis=0) for h in nh]
        dq_ref[...] = (jnp.concatenate(dqs, axis=1) * GLA_SCALE).astype(BF16)
        own = [[_dot_tn(do_b[h][rs], qs[rs, keys[h]]) for h in nh] for rs in GLA_CHUNKS]
        gts, later = [None] * GLA_STEP, [carry[h] for h in nh]
        for c in reversed(nc):
            gts[c] = [own[c][h] + later[h] for h in nh]
            later = [gts[c][h] * a[c][:, keys[h]] for h in nh]
        for h in nh:
            carry[h] = later[h]
        gt_b = [[gts[c][h].astype(BF16) for h in nh] for c in nc]
        dkd = jnp.concatenate([jnp.concatenate([_dot(v_b[rs, vals[h]], gt_b[c][h]) for h in nh], axis=1)
                               for c, rs in enumerate(GLA_CHUNKS)], axis=0)
        dvs = [[_dot_nt(kd_b[rs, keys[h]], gt_b[c][h]) for h in nh] for c, rs in enumerate(GLA_CHUNKS)]
        before = lambda c, h: st_ref[c - 1, h] if c > 0 else stp_ref[0, h] * has_prev
        da = [jnp.concatenate([jnp.sum(gts[c][h] * before(c, h), axis=0, keepdims=True) for h in nh], axis=1)
              for c in nc]
        for h in nh:
            dv_ref[:, vals[h]] = jnp.concatenate([dvs[c][h] for c in nc], axis=0).astype(BF16)
        dk_ref[...] = (dkd * e).astype(BF16)
        dd = dkd * kf * e
        dsum = _per_chunk(lambda mine: jnp.sum(jnp.where(mine, dd, 0.0), axis=0, keepdims=True), dd)
        dcend = _spread([dsum[c] + da[c] * a[c] for c in nc], dd)
        dla = dcend - _dot01(tri_up, dd)
        dpre = dla * (1.0 / GLA_TAU) * (1.0 - _sigmoid(pre))
        dpre_b = dpre.astype(BF16)
        dga_ref[...] = _dot_nt(dpre_b, wa_b).astype(BF16)
        dwa_ref[...] += _dot_tn(ga_b, dpre_b)
        db_ref[...] += jnp.sum(dpre, axis=0, keepdims=True)

    full = lambda shape: pl.BlockSpec(shape, lambda n: tuple(0 for _ in shape))
    wide = pl.BlockSpec((GLA_ROWS, D_GLA), lambda n: (rev(n), 0))
    keyw = pl.BlockSpec((GLA_ROWS, GLA_KW), lambda n: (rev(n), 0))
    st_spec = pl.BlockSpec((GLA_STEP, GLA_HEADS, GLA_DV, GLA_DK), lambda n: (rev(n), 0, 0, 0))
    stp_spec = pl.BlockSpec((1, GLA_HEADS, GLA_DV, GLA_DK),
                            lambda n: (jnp.maximum(GLA_STEP * rev(n) - 1, 0), 0, 0, 0))
    return pl.pallas_call(
        body, name="gla_bwd", grid=(nsteps,),
        in_specs=[wide, wide] + _z_specs_gla(rev)
        + [full((LANE, GLA_KW)), full((1, GLA_KW)), full((1, D_GLA)), st_spec, stp_spec],
        out_specs=[keyw, keyw, wide, wide, pl.BlockSpec((GLA_ROWS, LANE), lambda n: (rev(n), 0)),
                   full((LANE, GLA_KW)), full((1, GLA_KW)), full((1, D_GLA))],
        out_shape=[jax.ShapeDtypeStruct((s, GLA_KW), BF16), jax.ShapeDtypeStruct((s, GLA_KW), BF16),
                   jax.ShapeDtypeStruct((s, D_GLA), BF16), jax.ShapeDtypeStruct((s, D_GLA), BF16),
                   jax.ShapeDtypeStruct((s, LANE), BF16),
                   jax.ShapeDtypeStruct((LANE, GLA_KW), F32), jax.ShapeDtypeStruct((1, GLA_KW), F32),
                   jax.ShapeDtypeStruct((1, D_GLA), F32)],
        scratch_shapes=[pltpu.VMEM((GLA_HEADS, GLA_DV, GLA_DK), F32)],
        compiler_params=_cparams(("arbitrary",)),
    )(dyc, o_gla, z, z, z, z, z, wa_pad, b_alpha, g_gla, states, states)


def _build_bias_table(rb_row, et_ref):
    far = jnp.broadcast_to(rb_row[:, 2 * REL_CLIP:2 * REL_CLIP + 1], (1, LANE))
    near_hi = rb_row[:, REL_CLIP:2 * REL_CLIP]
    near_lo = rb_row[:, 0:REL_CLIP]
    past = jnp.broadcast_to(rb_row[:, 0:1], (1, LANE))
    seg = [far, far, far, far, near_hi, near_lo] + [past] * (ET_ROWS // LANE - 5)
    ri = lax.broadcasted_iota(jnp.int32, (LANE, LANE), 0)
    ci = lax.broadcasted_iota(jnp.int32, (LANE, LANE), 1)
    for kb in range(ET_ROWS // LANE):
        wmat = jnp.where(ri + ci < LANE, seg[kb], seg[kb + 1])
        blk = pltpu.roll(wmat, 0, 1, stride=1, stride_axis=0)
        lag = LEFT_CHUNKS + ci // CHUNK - (2 * kb + ri // CHUNK)
        et_ref[kb * LANE:(kb + 1) * LANE, :] = jnp.where((lag >= 0) & (lag <= LEFT_CHUNKS), blk, NEG)


def _reduce_bias_table(det_ref):
    lane = lax.broadcasted_iota(jnp.int32, (1, LANE), 1)
    ri = lax.broadcasted_iota(jnp.int32, (LANE, LANE), 0)
    ci = lax.broadcasted_iota(jnp.int32, (LANE, LANE), 1)
    flip = jnp.where(ri + ci == LANE - 1, 1.0, 0.0).astype(BF16)
    segs = jnp.zeros((8, LANE), F32)
    seg_row = lax.broadcasted_iota(jnp.int32, (8, LANE), 0)
    prev_minus = jnp.zeros((1, LANE), F32)
    for kb in range(6):
        rolled = pltpu.roll(_dot01(det_ref[kb * LANE:(kb + 1) * LANE, :], flip, left=False), 0, 1,
                            stride=1, stride_axis=0)
        plus = jnp.sum(jnp.where(ci >= ri, rolled, 0.0), axis=0, keepdims=True)
        minus = jnp.sum(jnp.where(ci < ri, rolled, 0.0), axis=0, keepdims=True)
        segs = segs + jnp.where(seg_row == kb, plus + prev_minus, 0.0)
        prev_minus = minus
    segs = _dot01(segs, flip, left=False)
    pick = lambda kb: jnp.sum(jnp.where(seg_row == kb, segs, 0.0), axis=0, keepdims=True)
    far = jnp.sum(pick(0) + pick(1) + pick(2) + pick(3), axis=1, keepdims=True)
    last = jnp.where(lane == 0, far, 0.0)
    return jnp.concatenate([pick(5), pick(4), last], axis=1)


def _att_window(b):
    c0 = 2 * b
    kstart = pl.multiple_of(jnp.maximum(c0 - LEFT_CHUNKS, 0) * CHUNK, CHUNK)
    eoff = pl.multiple_of(jnp.maximum(LEFT_CHUNKS - c0, 0) * CHUNK, CHUNK)
    return kstart, eoff


def _att_probs(q_b, kw_b, et):
    st = _dot_nt(kw_b, q_b) * ATT_SCALE + et
    m = jnp.max(st, axis=0, keepdims=True)
    ex = jnp.exp(st - m)
    return ex * (1.0 / jnp.sum(ex, axis=0, keepdims=True))


def _att_fwd(z, rb_pad, g_att):
    s = z.shape[0]
    nblk = s // QB
    c_aq, c_ak, c_av, c_ag = [(OFF_AQ + i * D_ATT) // ATT_HD for i in range(4)]

    def body(q_ref, k_ref, v_ref, ag_ref, rb_ref, g_ref, y_ref, o_ref, et_ref, kb_ref, vb_ref):
        h = pl.program_id(0)
        b = pl.program_id(1)

        @pl.when(b == 0)
        def _():
            _build_bias_table(rb_ref[pl.ds(h, 1), :], et_ref)
            kb_ref[...] = k_ref[...].astype(BF16)
            vb_ref[...] = v_ref[...].astype(BF16)

        for j in range(ATT_UNROLL):
            rs = slice(j * QB, (j + 1) * QB)
            kstart, eoff = _att_window(b * ATT_UNROLL + j)
            q_b = q_ref[rs, :].astype(BF16)
            kw_b = kb_ref[pl.ds(kstart, WIN), :]
            vw_b = vb_ref[pl.ds(kstart, WIN), :]
            pt = _att_probs(q_b, kw_b, et_ref[pl.ds(eoff, WIN), :])
            o = _dot_tn(pt.astype(BF16), vw_b)
            o_ref[rs, :] = o
            r = lax.rsqrt(jnp.mean(o * o, axis=-1, keepdims=True) + EPS)
            ag = ag_ref[rs, :]
            y_ref[rs, :] = (o * r * g_ref[...] * (ag * _sigmoid(ag))).astype(BF16)

    blk = lambda col: pl.BlockSpec((ATT_UNROLL * QB, ATT_HD), lambda h, b: (b, col + h))
    seq = lambda col: pl.BlockSpec((s, ATT_HD), lambda h, b: (0, col + h))
    out_blk = pl.BlockSpec((ATT_UNROLL * QB, ATT_HD), lambda h, b: (b, h))
    return pl.pallas_call(
        body, name="att_fwd", grid=(ATT_HEADS, nblk // ATT_UNROLL),
        in_specs=[blk(c_aq), seq(c_ak), seq(c_av), blk(c_ag),
                  pl.BlockSpec((ATT_HEADS, 3 * LANE), lambda h, b: (0, 0)),
                  pl.BlockSpec((1, ATT_HD), lambda h, b: (0, h))],
        out_specs=[out_blk, out_blk],
        out_shape=[jax.ShapeDtypeStruct((s, D_ATT), BF16), jax.ShapeDtypeStruct((s, D_ATT), F32)],
        scratch_shapes=[pltpu.VMEM((ET_ROWS, LANE), F32), pltpu.VMEM((s, ATT_HD), BF16),
                        pltpu.VMEM((s, ATT_HD), BF16)],
        compiler_params=_cparams(("arbitrary", "arbitrary")),
    )(z, z, z, z, rb_pad, g_att)


def _att_bwd(dyc, o_att, z, rb_pad, g_att):
    s = z.shape[0]
    nblk = s // QB
    c_aq, c_ak, c_av, c_ag = [(OFF_AQ + i * D_ATT) // ATT_HD for i in range(4)]
    c_dy = D_GLA // ATT_HD

    def body(dy_ref, o_ref, q_ref, k_ref, v_ref, ag_ref, rb_ref, g_ref,
             dq_ref, dk_ref, dv_ref, dag_ref, drb_ref, dg_ref, et_ref, det_ref, kb_ref, vb_ref, dk_acc, dv_acc):
        h = pl.program_id(0)
        b = pl.program_id(1)

        @pl.when(b == 0)
        def _():
            _build_bias_table(rb_ref[pl.ds(h, 1), :], et_ref)
            kb_ref[...] = k_ref[...].astype(BF16)
            vb_ref[...] = v_ref[...].astype(BF16)
            det_ref[...] = jnp.zeros_like(det_ref)
            dk_acc[...] = jnp.zeros_like(dk_acc)
            dv_acc[...] = jnp.zeros_like(dv_acc)
            dg_ref[...] = jnp.zeros_like(dg_ref)

        g = g_ref[...]
        dg = jnp.zeros((1, ATT_HD), F32)
        for j in range(ATT_UNROLL):
            rs = slice(j * QB, (j + 1) * QB)
            kstart, eoff = _att_window(b * ATT_UNROLL + j)
            q_b = q_ref[rs, :].astype(BF16)
            kw_b = kb_ref[pl.ds(kstart, WIN), :]
            vw_b = vb_ref[pl.ds(kstart, WIN), :]
            pt = _att_probs(q_b, kw_b, et_ref[pl.ds(eoff, WIN), :])
            o = o_ref[rs, :]
            ag = ag_ref[rs, :]
            dy = dy_ref[rs, :]
            r = lax.rsqrt(jnp.mean(o * o, axis=-1, keepdims=True) + EPS)
            sg = _sigmoid(ag)
            don = dy * (ag * sg)
            dag_ref[rs, :] = (dy * (o * r * g) * (sg * (1.0 + ag * (1.0 - sg)))).astype(BF16)
            dg = dg + jnp.sum(don * o * r, axis=0, keepdims=True)
            w = don * g
            do_b = (r * (w - o * (r * r) * jnp.mean(w * o, axis=-1, keepdims=True))).astype(BF16)
            pt_b = pt.astype(BF16)
            dpt = _dot_nt(vw_b, do_b)
            dst = pt * (dpt - jnp.sum(dpt * pt, axis=0, keepdims=True))
            det_ref[pl.ds(eoff, WIN), :] += dst
            ds_b = (dst * ATT_SCALE).astype(BF16)
            dq_ref[rs, :] = _dot_tn(ds_b, kw_b).astype(BF16)
            dk_acc[pl.ds(kstart, WIN), :] += _dot(ds_b, q_b)
            dv_acc[pl.ds(kstart, WIN), :] += _dot(pt_b, do_b)
        dg_ref[...] += dg

        @pl.when(b == nblk // ATT_UNROLL - 1)
        def _():
            drb_ref[0] = jnp.broadcast_to(_reduce_bias_table(det_ref), (8, 3 * LANE))
            dk_ref[...] = dk_acc[...].astype(BF16)
            dv_ref[...] = dv_acc[...].astype(BF16)

    blk = lambda col: pl.BlockSpec((ATT_UNROLL * QB, ATT_HD), lambda h, b: (b, col + h))
    seq = lambda col: pl.BlockSpec((s, ATT_HD), lambda h, b: (0, col + h))
    out_blk = pl.BlockSpec((ATT_UNROLL * QB, ATT_HD), lambda h, b: (b, h))
    out_seq = pl.BlockSpec((s, ATT_HD), lambda h, b: (0, h))
    return pl.pallas_call(
        body, name="att_bwd", grid=(ATT_HEADS, nblk // ATT_UNROLL),
        in_specs=[blk(c_dy), blk(0), blk(c_aq), seq(c_ak), seq(c_av), blk(c_ag),
                  pl.BlockSpec((ATT_HEADS, 3 * LANE), lambda h, b: (0, 0)),
                  pl.BlockSpec((1, ATT_HD), lambda h, b: (0, h))],
        out_specs=[out_blk, out_seq, out_seq, out_blk,
                   pl.BlockSpec((1, 8, 3 * LANE), lambda h, b: (h, 0, 0)),
                   pl.BlockSpec((1, ATT_HD), lambda h, b: (0, h))],
        out_shape=[jax.ShapeDtypeStruct((s, D_ATT), BF16), jax.ShapeDtypeStruct((s, D_ATT), BF16),
                   jax.ShapeDtypeStruct((s, D_ATT), BF16), jax.ShapeDtypeStruct((s, D_ATT), BF16),
                   jax.ShapeDtypeStruct((ATT_HEADS, 8, 3 * LANE), F32),
                   jax.ShapeDtypeStruct((1, D_ATT), F32)],
        scratch_shapes=[pltpu.VMEM((ET_ROWS, LANE), F32), pltpu.VMEM((ET_ROWS, LANE), F32),
                        pltpu.VMEM((s, ATT_HD), BF16), pltpu.VMEM((s, ATT_HD), BF16),
                        pltpu.VMEM((s, ATT_HD), F32), pltpu.VMEM((s, ATT_HD), F32)],
        compiler_params=_cparams(("arbitrary", "arbitrary")),
    )(dyc, o_att, z, z, z, z, rb_pad, g_att)


ADAM_ROWS = 64
ADAM_COL_ROWS = 32


def _adam_math(w, g, m, v):
    m2 = ADAM_B1 * m + (1.0 - ADAM_B1) * g
    v2 = ADAM_B2 * v + (1.0 - ADAM_B2) * (g * g)
    m_hat = m2 / (1.0 - ADAM_B1 ** ADAM_STEP)
    v_hat = v2 / (1.0 - ADAM_B2 ** ADAM_STEP)
    delta = -ADAM_LR * (m_hat / (jnp.sqrt(v_hat) + ADAM_EPS) + ADAM_WD * w)
    return delta, m2, v2


def _adam_sharded(parts, first, w, m, v, name):
    nl, nr, nc = w.shape

    def body(*refs):
        p_refs = refs[:nl]
        w_ref, m_ref, v_ref, g_ref, d_ref, m2_ref, v2_ref = refs[nl:]
        for k in range(nl):
            @pl.when(pl.program_id(0) == k)
            def _(p_ref=p_refs[k]):
                g = p_ref[0].astype(F32)
                for dev in range(1, N_DEV):
                    g = g + p_ref[dev].astype(F32)
                delta, m2, v2 = _adam_math(w_ref[0], g, m_ref[0], v_ref[0])
                g_ref[0] = g
                d_ref[0] = delta
                m2_ref[0] = m2
                v2_ref[0] = v2

    def part_spec(k):
        return pl.BlockSpec((N_DEV, ADAM_ROWS, nc), lambda l, i: (0, first + jnp.where(l == k, i, 0), 0))

    blk = pl.BlockSpec((1, ADAM_ROWS, nc), lambda l, i: (l, i, 0))
    shp = jax.ShapeDtypeStruct(w.shape, F32)
    return pl.pallas_call(
        body, name=name, grid=(nl, pl.cdiv(nr, ADAM_ROWS)),
        in_specs=[part_spec(k) for k in range(nl)] + [blk, blk, blk],
        out_specs=[blk, blk, blk, blk],
        out_shape=[shp, shp, shp, shp],
        compiler_params=_cparams(("arbitrary", "arbitrary")),
    )(*parts, w, m, v)


def _adam_columns(parts, first, w, m, v):
    nc, nl, d = w.shape

    def body(*refs):
        p_refs = refs[:nl]
        w_ref, m_ref, v_ref, g_ref, d_ref, m2_ref, v2_ref = refs[nl:]
        for l in range(nl):
            g = p_refs[l][0].astype(F32)
            for dev in range(1, N_DEV):
                g = g + p_refs[l][dev].astype(F32)
            delta, m2, v2 = _adam_math(w_ref[:, l, :], g, m_ref[:, l, :], v_ref[:, l, :])
            g_ref[:, l, :] = g
            d_ref[:, l, :] = delta
            m2_ref[:, l, :] = m2
            v2_ref[:, l, :] = v2

    blk = pl.BlockSpec((ADAM_COL_ROWS, nl, d), lambda i: (i, 0, 0))
    part = pl.BlockSpec((N_DEV, ADAM_COL_ROWS, d), lambda i: (0, first + i, 0))
    shp = jax.ShapeDtypeStruct(w.shape, F32)
    return pl.pallas_call(
        body, name="adam_w_in", grid=(pl.cdiv(nc, ADAM_COL_ROWS),),
        in_specs=[part] * nl + [blk, blk, blk],
        out_specs=[blk, blk, blk, blk],
        out_shape=[shp, shp, shp, shp],
        compiler_params=_cparams(("parallel",)),
    )(*parts, w, m, v)


def _adam_small(ws, gs, ms, vs):
    n = len(ws)

    def body(*refs):
        w_refs, g_refs, m_refs, v_refs, d_refs, m2_refs, v2_refs = [refs[i * n:(i + 1) * n] for i in range(7)]
        for i in range(n):
            delta, m2, v2 = _adam_math(w_refs[i][...], g_refs[i][...], m_refs[i][...], v_refs[i][...])
            d_refs[i][...] = delta
            m2_refs[i][...] = m2
            v2_refs[i][...] = v2

    shapes = [jax.ShapeDtypeStruct(w.shape, F32) for w in ws]
    out = pl.pallas_call(body, name="adam_small", out_shape=shapes * 3)(*ws, *gs, *ms, *vs)
    return out[:n], out[n:2 * n], out[2 * n:]


def _position():
    return lax.axis_index("x"), lax.axis_index("y"), lax.axis_index("c")


def _slot(p):
    return 4 * p[0] + 2 * p[1] + p[2]


BF16_TILE_ROWS = 16


def _slab_rows(rows, cols):
    return -(-(rows + cols) // BF16_TILE_ROWS) * BF16_TILE_ROWS


RELAYOUT_COLS = 1024
RELAYOUT_CHUNK = 64


def _shard_pieces(dev, rows, cols):
    moved = ((0, GA_ORIG, 0), (GA_ORIG, GA_ORIG + GLA_RANK, OFF_GA - GA_ORIG), (GA_ORIG + GLA_RANK, D_IN, -GLA_RANK))
    c0, c1 = dev * cols, (dev + 1) * cols
    return [(rows + max(c0, lo) - c0, max(c0, lo) + off, min(c1, hi) - max(c0, lo))
            for lo, hi, off in moved if max(c0, lo) < min(c1, hi)]


def _move_rows(src, src_row, dst, dst_row, n):
    assert src_row % 2 == 0 and dst_row % 2 == 0 and n % 2 == 0
    for r in range(0, n // 2, RELAYOUT_CHUNK):
        m = min(RELAYOUT_CHUNK, n // 2 - r)
        dst[dst_row // 2 + r:dst_row // 2 + r + m, :] = src[src_row // 2 + r:src_row // 2 + r + m, :]


def _aligned_weight(land, rows, cols):
    _, slab, d = land.shape
    ct = min(RELAYOUT_COLS, d)

    def body(land_ref, wt_ref, wo_ref):
        dev = pl.program_id(1)
        src = land_ref.bitcast(jnp.uint32)
        dst = wt_ref.bitcast(jnp.uint32)
        wo_ref[...] = land_ref[0:rows, :]

        @pl.when(dev == 0)
        def _():
            dst[D_IN // 2:D_ZP // 2, :] = jnp.zeros(((D_ZP - D_IN) // 2, ct), jnp.uint32)

        for k in range(N_DEV):
            @pl.when(dev == k)
            def _(k=k):
                for at, to, n in _shard_pieces(k, rows, cols):
                    _move_rows(src, at, dst, to, n)

    return pl.pallas_call(
        body, name="aligned_weight", grid=(d // ct, N_DEV),
        in_specs=[pl.BlockSpec((slab, ct), lambda c, dev: (dev, c))],
        out_specs=[pl.BlockSpec((D_ZP, ct), lambda c, dev: (0, c)),
                   pl.BlockSpec((rows, ct), lambda c, dev: (dev, c))],
        out_shape=[jax.ShapeDtypeStruct((D_ZP, d), land.dtype),
                   jax.ShapeDtypeStruct((N_DEV * rows, d), land.dtype)],
        compiler_params=_cparams(("parallel", "arbitrary")),
    )(land.reshape(N_DEV * slab, d))


def _partial_slabs(dwt, cols):
    d = dwt[0].shape[1]
    bounds = (0, GA_ORIG, OFF_GA, D_ZP)
    assert tuple(a.shape[0] for a in dwt) == tuple(hi - lo for lo, hi in zip(bounds, bounds[1:]))
    slab = _slab_rows(0, cols)
    ct = min(RELAYOUT_COLS, d)

    def body(*refs):
        out_ref = refs[-1]
        dev = pl.program_id(1)
        srcs = [ref.bitcast(jnp.uint32) for ref in refs[:-1]]
        dst = out_ref.bitcast(jnp.uint32)
        dst[cols // 2:slab // 2, :] = jnp.zeros(((slab - cols) // 2, ct), jnp.uint32)
        for k in range(N_DEV):
            @pl.when(dev == k)
            def _(k=k):
                for to, at, n in _shard_pieces(k, 0, cols):
                    which = max(i for i, lo in enumerate(bounds[:-1]) if lo <= at)
                    assert at + n <= bounds[which + 1]
                    _move_rows(srcs[which], at - bounds[which], dst, to, n)

    return pl.pallas_call(
        body, name="partial_slabs", grid=(d // ct, N_DEV),
        in_specs=[pl.BlockSpec((a.shape[0], ct), lambda c, dev: (0, c)) for a in dwt],
        out_specs=pl.BlockSpec((slab, ct), lambda c, dev: (dev, c)),
        out_shape=jax.ShapeDtypeStruct((N_DEV * slab, d), dwt[0].dtype),
        compiler_params=_cparams(("parallel", "arbitrary")),
    )(*dwt).reshape(N_DEV, slab, d)


def _peer(pos, k):
    x, y, c = pos
    return (1 - x if k & 4 else x, 1 - y if k & 2 else y, 1 - c if k & 1 else c)


HBM_SPEC = pl.BlockSpec(memory_space=pltpu.HBM)
SEM_SPEC = pl.BlockSpec(memory_space=pltpu.SEMAPHORE)
GATHER_PEERS = (1, 4, 2, 6)
ALL_PEERS = (1, 2, 3, 4, 5, 6, 7)


def _hbm(a):
    return pltpu.with_memory_space_constraint(a, pltpu.HBM)


def _split_copies(src_ref, land_ref, send_sems, recv_sems, ks, per_peer, landed):
    me = _position()
    out = []
    for i, k in enumerate(ks):
        peer = _peer(me, k)
        src = src_ref.at[_slot(peer)] if per_peer else src_ref
        dst = land_ref.at[_slot(peer) if landed else _slot(me)]
        out.append(pltpu.make_async_remote_copy(
            src_ref=src, dst_ref=dst, send_sem=send_sems.at[i], recv_sem=recv_sems.at[i],
            device_id=peer, device_id_type=MESH))
    return out


def _exchange_start(src, after, ks, per_peer, name):
    slab = src.shape[1:] if per_peer else src.shape
    land_shape = (N_DEV,) + tuple(slab)
    n = len(ks)

    def body(src_ref, land_ref, after_ref, send_sems, recv_sems, src_thru, land_thru, token):
        for cp in _split_copies(src_ref, land_ref, send_sems, recv_sems, ks, per_peer, landed=False):
            cp.start()
        token[...] = jnp.zeros_like(token)

    return pl.pallas_call(
        body, name=name,
        out_shape=(pltpu.SemaphoreType.DMA((n,)), pltpu.SemaphoreType.DMA((n,)),
                   pltpu.HBM(src.shape, src.dtype), pltpu.HBM(land_shape, src.dtype),
                   jax.ShapeDtypeStruct((8, LANE), F32)),
        in_specs=(HBM_SPEC, HBM_SPEC, ANY),
        out_specs=(SEM_SPEC, SEM_SPEC, HBM_SPEC, HBM_SPEC, pl.BlockSpec(memory_space=pltpu.VMEM)),
        input_output_aliases={0: 2, 1: 3},
        compiler_params=pltpu.CompilerParams(has_side_effects=pltpu.SideEffectType.DATAFLOW_SIDE_EFFECTING),
    )(_hbm(src), _hbm(lax.empty(land_shape, src.dtype)), after)


def _exchange_wait(started, after, ks, per_peer, name):
    send_sems, recv_sems, src_thru, land_thru = started

    def body(src_ref, land_ref, send_sems, recv_sems, after_ref, src_dead, land_out):
        for cp in _split_copies(src_ref, land_ref, send_sems, recv_sems, ks, per_peer, landed=True):
            cp.wait_send()
            cp.wait_recv()

    return pl.pallas_call(
        body, name=name,
        out_shape=(pltpu.HBM(src_thru.shape, src_thru.dtype), pltpu.HBM(land_thru.shape, land_thru.dtype)),
        in_specs=(HBM_SPEC, HBM_SPEC, SEM_SPEC, SEM_SPEC, ANY), out_specs=(HBM_SPEC, HBM_SPEC),
        input_output_aliases={0: 0, 1: 1},
        compiler_params=pltpu.CompilerParams(has_side_effects=pltpu.SideEffectType.DATAFLOW_SIDE_EFFECTING),
    )(src_thru, land_thru, send_sems, recv_sems, after)


def _relay_copies(land_ref, send_sems, recv_sems, landed):
    me = _position()
    sibling = _peer(me, 1)
    out = []
    for i, k in enumerate(GATHER_PEERS[1:]):
        blk = land_ref.at[_slot(_peer(sibling if landed else me, k))]
        out.append(pltpu.make_async_remote_copy(
            src_ref=blk, dst_ref=blk, send_sem=send_sems.at[i], recv_sem=recv_sems.at[i],
            device_id=sibling, device_id_type=MESH))
    return out


def _relay_start(land, name):
    n = len(GATHER_PEERS) - 1

    def body(land_ref, send_sems, recv_sems, land_thru, token):
        for cp in _relay_copies(land_ref, send_sems, recv_sems, landed=False):
            cp.start()
        token[...] = jnp.zeros_like(token)

    return pl.pallas_call(
        body, name=name,
        out_shape=(pltpu.SemaphoreType.DMA((n,)), pltpu.SemaphoreType.DMA((n,)),
                   pltpu.HBM(land.shape, land.dtype), jax.ShapeDtypeStruct((8, LANE), F32)),
        in_specs=(HBM_SPEC,),
        out_specs=(SEM_SPEC, SEM_SPEC, HBM_SPEC, pl.BlockSpec(memory_space=pltpu.VMEM)),
        input_output_aliases={0: 2},
        compiler_params=pltpu.CompilerParams(has_side_effects=pltpu.SideEffectType.DATAFLOW_SIDE_EFFECTING),
    )(_hbm(land))


def _relay_wait(started, after, name):
    send_sems, recv_sems, land_thru = started

    def body(land_ref, send_sems, recv_sems, after_ref, land_out):
        for cp in _relay_copies(land_ref, send_sems, recv_sems, landed=True):
            cp.wait_send()
            cp.wait_recv()

    return pl.pallas_call(
        body, name=name,
        out_shape=pltpu.HBM(land_thru.shape, land_thru.dtype),
        in_specs=(HBM_SPEC, SEM_SPEC, SEM_SPEC, ANY), out_specs=HBM_SPEC,
        input_output_aliases={0: 0},
        compiler_params=pltpu.CompilerParams(has_side_effects=pltpu.SideEffectType.DATAFLOW_SIDE_EFFECTING),
    )(land_thru, send_sems, recv_sems, after)


def _share(vec, name, after=None):
    follows = [] if after is None else [after]

    def body(vec_ref, *rest):
        out_ref, send_sems, recv_sems, local_sem = rest[len(follows):]
        me = _position()

        def copy(k, landed):
            peer = _peer(me, k)
            return pltpu.make_async_remote_copy(
                src_ref=vec_ref, dst_ref=out_ref.at[_slot(peer) if landed else _slot(me)],
                send_sem=send_sems.at[k - 1], recv_sem=recv_sems.at[k - 1], device_id=peer, device_id_type=MESH)

        mine = pltpu.make_async_copy(vec_ref, out_ref.at[_slot(me)], local_sem)
        mine.start()
        sent = [copy(k, False) for k in ALL_PEERS]
        for cp in sent:
            cp.start()
        for k in ALL_PEERS:
            copy(k, True).wait_recv()
        for cp in sent:
            cp.wait_send()
        mine.wait()

    return pl.pallas_call(
        body, name=name,
        in_specs=[ANY] * (1 + len(follows)), out_specs=ANY,
        out_shape=jax.ShapeDtypeStruct((N_DEV,) + vec.shape, vec.dtype),
        scratch_shapes=[pltpu.SemaphoreType.DMA((N_DEV - 1,)), pltpu.SemaphoreType.DMA((N_DEV - 1,)),
                        pltpu.SemaphoreType.DMA],
    )(vec, *follows)


def _sum_slots(parts):
    def body(p_ref, o_ref):
        acc = p_ref[0]
        for dev in range(1, N_DEV):
            acc = acc + p_ref[dev]
        o_ref[...] = acc

    return pl.pallas_call(body, name="sum_slots",
                          out_shape=jax.ShapeDtypeStruct(parts.shape[1:], F32))(parts)


PACK_ROWS = 8


def _packed_rows(size):
    return -(-size // (PACK_ROWS * LANE)) * PACK_ROWS


def _pack(arrs):
    def rows(a):
        flat = a.reshape(-1)
        return jnp.pad(flat, (0, _packed_rows(flat.shape[0]) * LANE - flat.shape[0])).reshape(-1, LANE)

    return jnp.concatenate([rows(a) for a in arrs], axis=0)


def _unpack(packed, shapes):
    out, at = [], 0
    for shp in shapes:
        size = 1
        for dim in shp:
            size *= dim
        nrows = _packed_rows(size)
        out.append(packed[at:at + nrows].reshape(-1)[:size].reshape(shp))
        at += nrows
    return out


def _layer_fwd(x, wt, wo, g_pre, g_post, wa_pad, b_alpha, g_gla, g_att, rb_pad, midway=None):
    h = _rms_fwd(x, g_pre)
    z = _matmul(h, wt, "nt", F32, *TILES["in_proj"], "in_proj", n_outer=True)
    y_gla, o_gla, states = _gla_fwd(z, wa_pad, b_alpha, g_gla)
    if midway is not None:
        g_att = g_att + midway(y_gla)[:1, :1]
    y_att, o_att = _att_fwd(z, rb_pad, g_att)
    y = _matmul_cols([y_gla, y_att], wo, F32, *TILES["out_proj"][:2], "out_proj")
    out = _post_fwd(x, y, g_post)
    return out, (x, h, z, o_gla, states, o_att, y_gla, y_att, y)


def _layer_bwd(dout, saved, wt, wo, g_pre, g_post, wa_pad, b_alpha, g_gla, g_att, rb_pad, on_dwo, on_dwt):
    x, h, z, o_gla, states, o_att, y_gla, y_att, y = saved
    dy, dg_post = _post_bwd(dout, y, g_post)
    dwo = _matmul_rows([y_gla, y_att], dy, BF16, *TILES["out_proj_dw"][:2], "out_proj_dw")
    token = on_dwo(dwo)
    dycat = _matmul(dy, wo, "nt", F32, *TILES["out_proj_dx"], "out_proj_dx", n_outer=True, after=token)
    dq, dk, dv, dgg, dga, dwa, db, dg_gla = _gla_bwd(dycat, o_gla, z, wa_pad, b_alpha, g_gla, states)
    daq, dak, dav, dag, drb, dg_att = _att_bwd(dycat, o_att, z, rb_pad, g_att)
    tw, tn = TILES["in_proj_dw"][:2]
    dwt = (_matmul_rows([dq, dk, dv, dgg], h, BF16, tw, tn, "in_proj_dw_gla"),
           _matmul_rows([daq, dak, dav, dag], h, BF16, tw, tn, "in_proj_dw_att"),
           _matmul_rows([dga], h, BF16, LANE, tn, "in_proj_dw_gate"))
    token = on_dwt(dwt)
    dh = _matmul_cols([dq, dk, dv, dgg, daq, dak, dav, dag, dga], wt, F32, *TILES["in_proj_dx"][:2],
                      "in_proj_dx", after=token)
    dx, dg_pre = _pre_bwd(dh, x, g_pre, dout)
    small = (dg_pre[0], dg_post[0], dwa[:GLA_RANK], db[0], dg_gla[0], dg_att[0], drb[:, 0, :N_REL])
    return dx, small


def kernel(x, w_in, w_out, g_pre, g_post, w_alpha, b_alpha, g_gla, g_att, rel_bias, loss_target, m_w_in, m_w_out, m_g_pre, m_g_post, m_w_alpha, m_b_alpha, m_g_gla, m_g_att, m_rel_bias, v_w_in, v_w_out, v_g_pre, v_g_post, v_w_alpha, v_b_alpha, v_g_gla, v_g_att, v_rel_bias):
    nl, d, cols = w_in.shape
    rows = w_out.shape[1]
    s = x.shape[1]
    x0 = x.reshape(s, d)
    tgt = loss_target.reshape(s, d)

    cols_first = lambda a: jnp.transpose(a, (2, 0, 1))
    w_c = cols_first(w_in)
    slab = _slab_rows(rows, cols)
    is_out = lax.broadcasted_iota(jnp.int32, (slab, d), 0) < rows

    def shard(l, zero=0.0):
        top = jnp.pad((w_out[l] + zero).astype(BF16), ((0, slab - rows), (0, 0)))
        rest = jnp.pad((w_c[:, l] + zero).astype(BF16), ((rows, slab - rows - cols), (0, 0)))
        return jnp.where(is_out, top, rest)

    first_fetch = _exchange_start(shard(0), x, GATHER_PEERS, False, "gather_start_0")
    began = first_fetch[4][0, 0]
    shards = [None] + [shard(l, began) for l in range(1, nl)]
    alpha = _pack([w_alpha]) + began
    wa_g = _share(alpha, "gather_alpha")
    wa_cols = w_alpha.shape[2]
    wa_full = wa_g.reshape(N_DEV, -1)[:, :nl * GLA_RANK * wa_cols].reshape(N_DEV, nl, GLA_RANK, wa_cols)
    wa_full = jnp.transpose(wa_full, (1, 2, 0, 3)).reshape(nl, GLA_RANK, GLA_KW)
    wa_pad = jnp.pad(wa_full, ((0, 0), (0, LANE - GLA_RANK), (0, 0)))
    rb_pad = jnp.pad(rel_bias, ((0, 0), (0, 0), (0, 3 * LANE - N_REL)))

    def layer_args(l, follows=None):
        gp = g_pre[l:l + 1] if follows is None else g_pre[l:l + 1] + follows[:1, :1]
        return (wts[l], wos[l], gp, g_post[l:l + 1], wa_pad[l], b_alpha[l:l + 1], g_gla[l:l + 1],
                g_att[l:l + 1], rb_pad[l])

    my = _slot(_position())

    def fetch(l, after):
        return _exchange_start(shards[l], after, GATHER_PEERS, False, f"gather_start_{l}")

    def relay(l, first_hop, after):
        own[l], land = _exchange_wait(first_hop[:4], after, GATHER_PEERS, False, f"gather_wait_{l}")
        return _relay_start(land, f"relay_start_{l}")

    def midway(l, y):
        flight["relay"] = relay(l + 1, flight["fetch"], y)
        if l + 2 >= nl:
            return flight["relay"][3]
        flight["fetch"] = fetch(l + 2, flight["relay"][2])
        return flight["fetch"][4]

    act, saved, wts, wos, flight, own = x0, [], [], [], {}, [None] * nl
    prepared = (wa_pad[0, :1, :1] + sum(sh[:1, :1].astype(F32) for sh in shards[1:]))
    flight["relay"] = relay(0, first_fetch, prepared)
    if nl > 1:
        flight["fetch"] = fetch(1, flight["relay"][2])
    for l in range(nl):
        land = _relay_wait(flight["relay"][:3], act, f"relay_wait_{l}")
        land = lax.dynamic_update_slice_in_dim(land, own[l][None], my, 0)
        wt_l, wo_l = _aligned_weight(land, rows, cols)
        wts.append(wt_l)
        wos.append(wo_l)
        act, sv = _layer_fwd(act, *layer_args(l, follows=first_fetch[4] if l == 0 else None),
                             midway=functools.partial(midway, l) if l + 1 < nl else None)
        saved.append(sv)
    dout, sq = _loss_head(act, tgt)
    loss = lax.psum(sq[0, 0] * (0.5 / d), ("x", "y", "c"))

    smalls, pending_out, pending_in = [None] * nl, [None] * nl, [None] * nl

    def send_out(l, dwo):
        pending_out[l] = _exchange_start(dwo.reshape(N_DEV, rows, d), dwo[:1, :1], ALL_PEERS, True,
                                         f"scatter_out_start_{l}")
        return pending_out[l][4]

    def send_in(l, dwt):
        pending_in[l] = _exchange_start(_partial_slabs(dwt, cols), dwt[-1], ALL_PEERS, True,
                                        f"scatter_in_start_{l}")
        return pending_in[l][4]

    for l in reversed(range(nl)):
        dout, smalls[l] = _layer_bwd(dout, saved[l], *layer_args(l), on_dwo=functools.partial(send_out, l),
                                     on_dwt=functools.partial(send_in, l))
    grad_x = dout.reshape(x.shape)

    def landed(started, after, name):
        partial, land = _exchange_wait(started[:4], after, ALL_PEERS, True, name)
        return lax.dynamic_update_slice_in_dim(land, lax.dynamic_slice_in_dim(partial, my, 1, 0), my, 0)

    parts_out = [landed(pending_out[l], dout, f"scatter_out_wait_{l}") for l in range(nl)]
    g_w_out, d_w_out, m2_w_out, v2_w_out = _adam_sharded(parts_out, 0, w_out, m_w_out, v_w_out, "adam_w_out")
    names = 7
    small_stacked = [jnp.stack([smalls[l][i] for l in range(nl)]) for i in range(names)]
    shapes = [a.shape for a in small_stacked]
    gathered = _share(_pack(small_stacked), "gather_small_grads", after=d_w_out)
    g_pre_g, g_post_g, wa_g_full, b_g, gla_g, att_g, rb_g = _unpack(_sum_slots(gathered), shapes)
    wa_g_mine = lax.dynamic_slice_in_dim(wa_g_full, my * wa_cols, wa_cols, axis=2)
    grads = [g_pre_g, g_post_g, wa_g_mine, b_g, gla_g, att_g, rb_g]
    ws = [g_pre, g_post, w_alpha, b_alpha, g_gla, g_att, rel_bias]
    ms = [m_g_pre, m_g_post, m_w_alpha, m_b_alpha, m_g_gla, m_g_att, m_rel_bias]
    vs = [v_g_pre, v_g_post, v_w_alpha, v_b_alpha, v_g_gla, v_g_att, v_rel_bias]
    d_s, m2_s, v2_s = _adam_small(ws, grads, ms, vs)

    parts_in = [landed(pending_in[l], d_s[0], f"scatter_in_wait_{l}") for l in range(nl)]
    g_w_in, d_w_in, m2_w_in, v2_w_in = [
        jnp.transpose(a, (1, 2, 0))
        for a in _adam_columns(parts_in, 0, w_c, cols_first(m_w_in), cols_first(v_w_in))]

    def ordered(big_in, big_out, small):
        return [big_in, big_out] + list(small)

    return (loss, grad_x,
            *ordered(g_w_in, g_w_out, grads),
            *ordered(d_w_in, d_w_out, d_s),
            *ordered(m2_w_in, m2_w_out, m2_s),
            *ordered(v2_w_in, v2_w_out, v2_s))
```

```python
import functools

import jax
import jax.numpy as jnp
from jax import lax
from jax.experimental import pallas as pl
from jax.experimental.pallas import tpu as pltpu

F32 = jnp.float32
BF16 = jnp.bfloat16
MESH = pl.DeviceIdType.MESH
ANY = pl.BlockSpec(memory_space=pl.ANY)

CHUNK = 64
GLA_HEADS = 4
GLA_DK = 128
GLA_DV = 256
GLA_KW = GLA_HEADS * GLA_DK
D_GLA = GLA_HEADS * GLA_DV
GLA_RANK = 16
GLA_TAU = 16.0
ATT_HEADS = 8
ATT_HD = 128
D_ATT = ATT_HEADS * ATT_HD
LEFT_CHUNKS = 8
REL_CLIP = 128
N_REL = 2 * REL_CLIP + 1
EPS = 1e-6
D_IN = 2 * GLA_KW + 2 * D_GLA + GLA_RANK + 4 * D_ATT
GLA_SCALE = GLA_DK ** -0.5
ATT_SCALE = ATT_HD ** -0.5

ADAM_LR = 0.001
ADAM_B1 = 0.9
ADAM_B2 = 0.999
ADAM_EPS = 1e-08
ADAM_WD = 0.01
ADAM_STEP = 10

N_DEV = 8
LANE = 128
GA_ORIG = 2 * GLA_KW + 2 * D_GLA
OFF_AQ = GA_ORIG
OFF_GA = GA_ORIG + 4 * D_ATT
D_ZP = OFF_GA + LANE
QB = 2 * CHUNK
ATT_UNROLL = 8
WIN = (LEFT_CHUNKS + 2) * CHUNK
ET_ROWS = WIN + LEFT_CHUNKS * CHUNK
NEG = -1e30
VMEM_LIMIT = 48 * 1024 * 1024


def _cparams(sem):
    return pltpu.CompilerParams(dimension_semantics=sem, vmem_limit_bytes=VMEM_LIMIT)


def _dot(a, b):
    return jnp.dot(a, b, preferred_element_type=F32)


def _dot_nt(a, b):
    return lax.dot_general(a, b, (((1,), (1,)), ((), ())), preferred_element_type=F32)


def _dot_tn(a, b):
    return lax.dot_general(a, b, (((0,), (0,)), ((), ())), preferred_element_type=F32)


def _dot01(t, x, left=True):
    if not left:
        t, x = x, t
    hi = x.astype(BF16)
    r = x - hi.astype(F32)
    mid = r.astype(BF16)
    lo = (r - mid.astype(F32)).astype(BF16)
    if left:
        return _dot(t, hi) + _dot(t, mid) + _dot(t, lo)
    return _dot(hi, t) + _dot(mid, t) + _dot(lo, t)


def _sigmoid(x):
    return 1.0 / (1.0 + jnp.exp(-x))


def _log_sigmoid(x):
    return jnp.minimum(x, 0.0) - jnp.log(1.0 + jnp.exp(-jnp.abs(x)))


TILES = {
    "in_proj": (512, D_ZP // 3, None),
    "in_proj_dx": (512, 512, None),
    "in_proj_dw": (512, 2048, None),
    "out_proj": (512, 1024, None),
    "out_proj_dx": (512, 1024, None),
    "out_proj_dw": (1024, 1024, None),
}


def _matmul(a, b, mode, out_dtype, tm, tn, tk, name, n_outer=False, after=None):
    if mode == "nn":
        (m, k), n = a.shape, b.shape[1]
    elif mode == "nt":
        (m, k), n = a.shape, b.shape[0]
    else:
        (k, m), n = a.shape, b.shape[1]
    tm, tn, tk = min(tm, m), min(tn, n), k if tk is None else min(tk, k)
    assert m % tm == 0 and n % tn == 0 and k % tk == 0, (name, m, n, k)
    nk = k // tk
    dot = {"nn": _dot, "nt": _dot_nt, "tn": _dot_tn}[mode]

    follows = [] if after is None else [after]

    def body_whole_k(a_ref, b_ref, *rest):
        o_ref = rest[-1]
        o_ref[...] = dot(a_ref[...], b_ref[...]).astype(out_dtype)

    def body(a_ref, b_ref, *rest):
        o_ref, acc_ref = rest[-2:]
        kk = pl.program_id(2)

        @pl.when(kk == 0)
        def _():
            acc_ref[...] = jnp.zeros_like(acc_ref)

        acc_ref[...] += dot(a_ref[...], b_ref[...])

        @pl.when(kk == nk - 1)
        def _():
            o_ref[...] = acc_ref[...].astype(out_dtype)

    def at(index):
        return (lambda j, i, kk: index(i, j, kk)) if n_outer else index

    if mode == "tn":
        a_spec = pl.BlockSpec((tk, tm), at(lambda i, j, kk: (kk, i)))
    else:
        a_spec = pl.BlockSpec((tm, tk), at(lambda i, j, kk: (i, kk)))
    if mode == "nt":
        b_spec = pl.BlockSpec((tn, tk), at(lambda i, j, kk: (j, kk)))
    else:
        b_spec = pl.BlockSpec((tk, tn), at(lambda i, j, kk: (kk, j)))
    return pl.pallas_call(
        body_whole_k if nk == 1 else body, name=name,
        grid=(n // tn, m // tm, nk) if n_outer else (m // tm, n // tn, nk),
        in_specs=[a_spec, b_spec] + [ANY] * len(follows),
        out_specs=pl.BlockSpec((tm, tn), at(lambda i, j, kk: (i, j))),
        out_shape=jax.ShapeDtypeStruct((m, n), out_dtype),
        scratch_shapes=[] if nk == 1 else [pltpu.VMEM((tm, tn), F32)],
        compiler_params=_cparams(("parallel", "parallel", "arbitrary")),
    )(a, b, *follows)


def _matmul_cols(pieces, b, out_dtype, tm, tn, name, after=None):
    m, n = pieces[0].shape[0], b.shape[1]
    widths = [p.shape[1] for p in pieces]
    starts = [sum(widths[:i]) for i in range(len(pieces))]
    follows = [] if after is None else [after]
    tm, tn = min(tm, m), min(tn, n)
    assert sum(widths) == b.shape[0] and m % tm == 0 and n % tn == 0, name

    def body(*refs):
        b_ref, o_ref = refs[len(pieces)], refs[-1]
        acc = None
        for p_ref, at, width in zip(refs, starts, widths):
            part = _dot(p_ref[...], b_ref[at:at + width, :])
            acc = part if acc is None else acc + part
        o_ref[...] = acc.astype(out_dtype)

    return pl.pallas_call(
        body, name=name, grid=(n // tn, m // tm),
        in_specs=[pl.BlockSpec((tm, width), lambda j, i: (i, 0)) for width in widths]
        + [pl.BlockSpec((b.shape[0], tn), lambda j, i: (0, j))] + [ANY] * len(follows),
        out_specs=pl.BlockSpec((tm, tn), lambda j, i: (i, j)),
        out_shape=jax.ShapeDtypeStruct((m, n), out_dtype),
        compiler_params=_cparams(("parallel", "parallel")),
    )(*pieces, b, *follows)


def _matmul_rows(pieces, b, out_dtype, tw, tn, name):
    k, n = b.shape
    tn = min(tn, n)
    counts = [p.shape[1] // tw for p in pieces]
    firsts = [sum(counts[:i]) for i in range(len(pieces))]
    assert all(p.shape[1] % tw == 0 for p in pieces) and n % tn == 0, name

    def body(*refs):
        b_ref, o_ref = refs[len(pieces):]
        for p_ref, first, count in zip(refs, firsts, counts):
            @pl.when((pl.program_id(0) >= first) & (pl.program_id(0) < first + count))
            def _(p_ref=p_ref):
                o_ref[...] = _dot_tn(p_ref[...], b_ref[...]).astype(out_dtype)

    def piece_spec(first, count):
        return pl.BlockSpec((k, tw), lambda i, j: (0, jnp.clip(i - first, 0, count - 1)))

    return pl.pallas_call(
        body, name=name, grid=(sum(counts), n // tn),
        in_specs=[piece_spec(first, count) for first, count in zip(firsts, counts)]
        + [pl.BlockSpec((k, tn), lambda i, j: (0, j))],
        out_specs=pl.BlockSpec((tw, tn), lambda i, j: (i, j)),
        out_shape=jax.ShapeDtypeStruct((sum(counts) * tw, n), out_dtype),
        compiler_params=_cparams(("parallel", "parallel")),
    )(*pieces, b)


ROWS = 512


def _rms_fwd(x, g):
    s, d = x.shape

    def body(x_ref, g_ref, h_ref):
        xv = x_ref[...]
        r = lax.rsqrt(jnp.mean(xv * xv, axis=-1, keepdims=True) + EPS)
        h_ref[...] = (xv * r * g_ref[...]).astype(BF16)

    return pl.pallas_call(
        body, name="rms_fwd", grid=(s // ROWS,),
        in_specs=[pl.BlockSpec((ROWS, d), lambda i: (i, 0)), pl.BlockSpec((1, d), lambda i: (0, 0))],
        out_specs=pl.BlockSpec((ROWS, d), lambda i: (i, 0)),
        out_shape=jax.ShapeDtypeStruct((s, d), BF16),
        compiler_params=_cparams(("parallel",)),
    )(x, g)


def _post_fwd(x, y, g):
    s, d = x.shape

    def body(x_ref, y_ref, g_ref, o_ref):
        yv = y_ref[...]
        r = lax.rsqrt(jnp.mean(yv * yv, axis=-1, keepdims=True) + EPS)
        o_ref[...] = x_ref[...] + yv * r * g_ref[...]

    row = pl.BlockSpec((ROWS, d), lambda i: (i, 0))
    return pl.pallas_call(
        body, name="post_fwd", grid=(s // ROWS,),
        in_specs=[row, row, pl.BlockSpec((1, d), lambda i: (0, 0))],
        out_specs=row,
        out_shape=jax.ShapeDtypeStruct((s, d), F32),
        compiler_params=_cparams(("parallel",)),
    )(x, y, g)


def _loss_head(out, tgt):
    s, d = out.shape

    def body(o_ref, t_ref, dout_ref, sum_ref):
        @pl.when(pl.program_id(0) == 0)
        def _():
            sum_ref[...] = jnp.zeros_like(sum_ref)

        e = o_ref[...] - t_ref[...]
        dout_ref[...] = e * (1.0 / d)
        sum_ref[...] += jnp.sum(jnp.sum(e * e, axis=1, keepdims=True), axis=0, keepdims=True)

    row = pl.BlockSpec((ROWS, d), lambda i: (i, 0))
    return pl.pallas_call(
        body, name="loss_head", grid=(s // ROWS,),
        in_specs=[row, row],
        out_specs=[row, pl.BlockSpec((1, 1), lambda i: (0, 0))],
        out_shape=[jax.ShapeDtypeStruct((s, d), F32), jax.ShapeDtypeStruct((1, 1), F32)],
        compiler_params=_cparams(("arbitrary",)),
    )(out, tgt)


def _post_bwd(dout, y, g):
    s, d = y.shape

    def body(do_ref, y_ref, g_ref, dy_ref, dg_ref):
        @pl.when(pl.program_id(0) == 0)
        def _():
            dg_ref[...] = jnp.zeros_like(dg_ref)

        yv = y_ref[...]
        dv = do_ref[...]
        r = lax.rsqrt(jnp.mean(yv * yv, axis=-1, keepdims=True) + EPS)
        dg_ref[...] += jnp.sum(dv * yv * r, axis=0, keepdims=True)
        w = dv * g_ref[...]
        dy = r * (w - yv * (r * r) * jnp.mean(w * yv, axis=-1, keepdims=True))
        dy_ref[...] = dy.astype(BF16)

    row = pl.BlockSpec((ROWS, d), lambda i: (i, 0))
    vec = pl.BlockSpec((1, d), lambda i: (0, 0))
    return pl.pallas_call(
        body, name="post_bwd", grid=(s // ROWS,),
        in_specs=[row, row, vec],
        out_specs=[row, vec],
        out_shape=[jax.ShapeDtypeStruct((s, d), BF16), jax.ShapeDtypeStruct((1, d), F32)],
        compiler_params=_cparams(("arbitrary",)),
    )(dout, y, g)


def _pre_bwd(dh, x, g, dout):
    s, d = x.shape

    def body(dh_ref, x_ref, g_ref, do_ref, dx_ref, dg_ref):
        @pl.when(pl.program_id(0) == 0)
        def _():
            dg_ref[...] = jnp.zeros_like(dg_ref)

        xv = x_ref[...]
        dv = dh_ref[...]
        r = lax.rsqrt(jnp.mean(xv * xv, axis=-1, keepdims=True) + EPS)
        dg_ref[...] += jnp.sum(dv * xv * r, axis=0, keepdims=True)
        w = dv * g_ref[...]
        dx_ref[...] = do_ref[...] + r * (w - xv * (r * r) * jnp.mean(w * xv, axis=-1, keepdims=True))

    row = pl.BlockSpec((ROWS, d), lambda i: (i, 0))
    vec = pl.BlockSpec((1, d), lambda i: (0, 0))
    return pl.pallas_call(
        body, name="pre_bwd", grid=(s // ROWS,),
        in_specs=[row, row, vec, row],
        out_specs=[row, vec],
        out_shape=[jax.ShapeDtypeStruct((s, d), F32), jax.ShapeDtypeStruct((1, d), F32)],
        compiler_params=_cparams(("arbitrary",)),
    )(dh, x, g, dout)


GLA_STEP = 4
GLA_ROWS = GLA_STEP * CHUNK
GLA_CHUNKS = [slice(c * CHUNK, (c + 1) * CHUNK) for c in range(GLA_STEP)]


def _chunk_triangles():
    ri = lax.broadcasted_iota(jnp.int32, (GLA_ROWS, GLA_ROWS), 0)
    ci = lax.broadcasted_iota(jnp.int32, (GLA_ROWS, GLA_ROWS), 1)
    same = (ri // CHUNK) == (ci // CHUNK)
    return (jnp.where(same & (ri >= ci), 1.0, 0.0).astype(BF16), jnp.where(same & (ci >= ri), 1.0, 0.0).astype(BF16))


def _per_chunk(fn, like):
    row = lax.broadcasted_iota(jnp.int32, like.shape, 0)
    return [fn((row >= c * CHUNK) & (row < (c + 1) * CHUNK)) for c in range(GLA_STEP)]


def _spread(per_chunk, like):
    row = lax.broadcasted_iota(jnp.int32, like.shape, 0)
    out = per_chunk[-1]
    for c in reversed(range(GLA_STEP - 1)):
        out = jnp.where(row < (c + 1) * CHUNK, per_chunk[c], out)
    return out


def _gla_gate(ga_b, wa_b, b_ref, tri):
    pre = _dot(ga_b, wa_b) + b_ref[...]
    la = _log_sigmoid(pre) * (1.0 / GLA_TAU)
    cum = _dot01(tri, la)
    row = lax.broadcasted_iota(jnp.int32, cum.shape, 0)
    cends = [jnp.sum(jnp.where(row == (c + 1) * CHUNK - 1, cum, 0.0), axis=0, keepdims=True)
             for c in range(GLA_STEP)]
    return pre, cum, cends


def _heads(width):
    return [slice(h * width, (h + 1) * width) for h in range(GLA_HEADS)]


def _z_specs_gla(rev=None):
    idx = (lambda n: n) if rev is None else rev
    return [
        pl.BlockSpec((GLA_ROWS, GLA_KW), lambda n: (idx(n), 0)),
        pl.BlockSpec((GLA_ROWS, GLA_KW), lambda n: (idx(n), 1)),
        pl.BlockSpec((GLA_ROWS, D_GLA), lambda n: (idx(n), 1)),
        pl.BlockSpec((GLA_ROWS, D_GLA), lambda n: (idx(n), 2)),
        pl.BlockSpec((GLA_ROWS, LANE), lambda n: (idx(n), OFF_GA // LANE)),
    ]


def _gla_fwd(z, wa_pad, b_alpha, g_gla):
    s = z.shape[0]
    nchunk = s // CHUNK

    def body(q_ref, k_ref, v_ref, gg_ref, ga_ref, wa_ref, b_ref, g_ref, y_ref, o_ref, st_ref, state):
        @pl.when(pl.program_id(0) == 0)
        def _():
            state[...] = jnp.zeros_like(state)

        ga_b = ga_ref[...].astype(BF16)
        tri, _ = _chunk_triangles()
        nh = range(GLA_HEADS)
        keys, vals = _heads(GLA_DK), _heads(GLA_DV)
        _, cum, cends = _gla_gate(ga_b, wa_ref[...].astype(BF16), b_ref, tri)
        kd_b = (k_ref[...] * jnp.exp(_spread(cends, cum) - cum)).astype(BF16)
        qs = (q_ref[...] * GLA_SCALE).astype(BF16)
        v_b = v_ref[...].astype(BF16)
        uts = [[_dot_tn(v_b[rs, vals[h]], kd_b[rs, keys[h]]) for h in nh] for rs in GLA_CHUNKS]
        sts, prev = [], [state[h] for h in nh]
        for c in range(GLA_STEP):
            a = jnp.exp(cends[c])
            prev = [prev[h] * a[:, keys[h]] + uts[c][h] for h in nh]
            sts.append(prev)
        for h in nh:
            state[h] = prev[h]
            for c in range(GLA_STEP):
                st_ref[c, h] = sts[c][h]
        outs = [[_dot_nt(qs[rs, keys[h]], sts[c][h].astype(BF16)) for h in nh] for c, rs in enumerate(GLA_CHUNKS)]
        for h in nh:
            o, vs = jnp.concatenate([outs[c][h] for c in range(GLA_STEP)], axis=0), vals[h]
            o_ref[:, vs] = o
            r = lax.rsqrt(jnp.mean(o * o, axis=-1, keepdims=True) + EPS)
            gg = gg_ref[:, vs]
            y_ref[:, vs] = (o * r * g_ref[:, vs] * (gg * _sigmoid(gg))).astype(BF16)

    full = lambda shape: pl.BlockSpec(shape, lambda n: tuple(0 for _ in shape))
    wide = pl.BlockSpec((GLA_ROWS, D_GLA), lambda n: (n, 0))
    return pl.pallas_call(
        body, name="gla_fwd", grid=(nchunk // GLA_STEP,),
        in_specs=_z_specs_gla() + [full((LANE, GLA_KW)), full((1, GLA_KW)), full((1, D_GLA))],
        out_specs=[wide, wide, pl.BlockSpec((GLA_STEP, GLA_HEADS, GLA_DV, GLA_DK), lambda n: (n, 0, 0, 0))],
        out_shape=[jax.ShapeDtypeStruct((s, D_GLA), BF16), jax.ShapeDtypeStruct((s, D_GLA), F32),
                   jax.ShapeDtypeStruct((nchunk, GLA_HEADS, GLA_DV, GLA_DK), F32)],
        scratch_shapes=[pltpu.VMEM((GLA_HEADS, GLA_DV, GLA_DK), F32)],
        compiler_params=_cparams(("arbitrary",)),
    )(z, z, z, z, z, wa_pad, b_alpha, g_gla)


def _gla_bwd(dyc, o_gla, z, wa_pad, b_alpha, g_gla, states):
    s = z.shape[0]
    nsteps = s // GLA_ROWS
    rev = lambda n: nsteps - 1 - n

    def body(dy_ref, o_ref, q_ref, k_ref, v_ref, gg_ref, ga_ref, wa_ref, b_ref, g_ref, st_ref, stp_ref,
             dq_ref, dk_ref, dv_ref, dgg_ref, dga_ref, dwa_ref, db_ref, dg_ref, carry):
        step = pl.program_id(0)

        @pl.when(step == 0)
        def _():
            carry[...] = jnp.zeros_like(carry)
            dwa_ref[...] = jnp.zeros_like(dwa_ref)
            db_ref[...] = jnp.zeros_like(db_ref)
            dg_ref[...] = jnp.zeros_like(dg_ref)

        has_prev = (step < nsteps - 1).astype(F32)
        ga_b = ga_ref[...].astype(BF16)
        tri, tri_up = _chunk_triangles()
        nh, nc = range(GLA_HEADS), range(GLA_STEP)
        keys, vals = _heads(GLA_DK), _heads(GLA_DV)
        wa_b = wa_ref[...].astype(BF16)
        pre, cum, cends = _gla_gate(ga_b, wa_b, b_ref, tri)
        e = jnp.exp(_spread(cends, cum) - cum)
        a = [jnp.exp(cends[c]) for c in nc]
        kf = k_ref[...]
        kd_b = (kf * e).astype(BF16)
        v_b = v_ref[...].astype(BF16)
        qs = (q_ref[...] * GLA_SCALE).astype(BF16)
        do_b = []
        for h in nh:
            vs = vals[h]
            o = o_ref[:, vs]
            gg = gg_ref[:, vs]
            g = g_ref[:, vs]
            dy = dy_ref[:, vs]
            r = lax.rsqrt(jnp.mean(o * o, axis=-1, keepdims=True) + EPS)
            sg = _sigmoid(gg)
            dogn = dy * (gg * sg)
            dgg_ref[:, vs] = (dy * (o * r * g) * (sg * (1.0 + gg * (1.0 - sg)))).astype(BF16)
            dg_ref[:, vs] += jnp.sum(dogn * o * r, axis=0, keepdims=True)
            w = dogn * g
            do_b.append((r * (w - o * (r * r) * jnp.mean(w * o, axis=-1, keepdims=True))).astype(BF16))
        dqs = [jnp.concatenate([_dot(do_b[h][rs], st_ref[c, h].astype(BF16)) for c, rs in enumerate(GLA_CHUNKS)],
                               axis=0) for h in nh]
        dq_ref[...] = (jnp.concatenate(dqs, axis=1) * GLA_SCALE).astype(BF16)
        own = [[_dot_tn(do_b[h][rs], qs[rs, keys[h]]) for h in nh] for rs in GLA_CHUNKS]
        gts, later = [None] * GLA_STEP, [carry[h] for h in nh]
        for c in reversed(nc):
            gts[c] = [own[c][h] + later[h] for h in nh]
            later = [gts[c][h] * a[c][:, keys[h]] for h in nh]
        for h in nh:
            carry[h] = later[h]
        gt_b = [[gts[c][h].astype(BF16) for h in nh] for c in nc]
        dkd = jnp.concatenate([jnp.concatenate([_dot(v_b[rs, vals[h]], gt_b[c][h]) for h in nh], axis=1)
                               for c, rs in enumerate(GLA_CHUNKS)], axis=0)
        dvs = [[_dot_nt(kd_b[rs, keys[h]], gt_b[c][h]) for h in nh] for c, rs in enumerate(GLA_CHUNKS)]
        before = lambda c, h: st_ref[c - 1, h] if c > 0 else stp_ref[0, h] * has_prev
        da = [jnp.concatenate([jnp.sum(gts[c][h] * before(c, h), axis=0, keepdims=True) for h in nh], axis=1)
              for c in nc]
        for h in nh:
            dv_ref[:, vals[h]] = jnp.concatenate([dvs[c][h] for c in nc], axis=0).astype(BF16)
        dk_ref[...] = (dkd * e).astype(BF16)
        dd = dkd * kf * e
        dsum = _per_chunk(lambda mine: jnp.sum(jnp.where(mine, dd, 0.0), axis=0, keepdims=True), dd)
        dcend = _spread([dsum[c] + da[c] * a[c] for c in nc], dd)
        dla = dcend - _dot01(tri_up, dd)
        dpre = dla * (1.0 / GLA_TAU) * (1.0 - _sigmoid(pre))
        dpre_b = dpre.astype(BF16)
        dga_ref[...] = _dot_nt(dpre_b, wa_b).astype(BF16)
        dwa_ref[...] += _dot_tn(ga_b, dpre_b)
        db_ref[...] += jnp.sum(dpre, axis=0, keepdims=True)

    full = lambda shape: pl.BlockSpec(shape, lambda n: tuple(0 for _ in shape))
    wide = pl.BlockSpec((GLA_ROWS, D_GLA), lambda n: (rev(n), 0))
    keyw = pl.BlockSpec((GLA_ROWS, GLA_KW), lambda n: (rev(n), 0))
    st_spec = pl.BlockSpec((GLA_STEP, GLA_HEADS, GLA_DV, GLA_DK), lambda n: (rev(n), 0, 0, 0))
    stp_spec = pl.BlockSpec((1, GLA_HEADS, GLA_DV, GLA_DK),
                            lambda n: (jnp.maximum(GLA_STEP * rev(n) - 1, 0), 0, 0, 0))
    return pl.pallas_call(
        body, name="gla_bwd", grid=(nsteps,),
        in_specs=[wide, wide] + _z_specs_gla(rev)
        + [full((LANE, GLA_KW)), full((1, GLA_KW)), full((1, D_GLA)), st_spec, stp_spec],
        out_specs=[keyw, keyw, wide, wide, pl.BlockSpec((GLA_ROWS, LANE), lambda n: (rev(n), 0)),
                   full((LANE, GLA_KW)), full((1, GLA_KW)), full((1, D_GLA))],
        out_shape=[jax.ShapeDtypeStruct((s, GLA_KW), BF16), jax.ShapeDtypeStruct((s, GLA_KW), BF16),
                   jax.ShapeDtypeStruct((s, D_GLA), BF16), jax.ShapeDtypeStruct((s, D_GLA), BF16),
                   jax.ShapeDtypeStruct((s, LANE), BF16),
                   jax.ShapeDtypeStruct((LANE, GLA_KW), F32), jax.ShapeDtypeStruct((1, GLA_KW), F32),
                   jax.ShapeDtypeStruct((1, D_GLA), F32)],
        scratch_shapes=[pltpu.VMEM((GLA_HEADS, GLA_DV, GLA_DK), F32)],
        compiler_params=_cparams(("arbitrary",)),
    )(dyc, o_gla, z, z, z, z, z, wa_pad, b_alpha, g_gla, states, states)


def _build_bias_table(rb_row, et_ref):
    far = jnp.broadcast_to(rb_row[:, 2 * REL_CLIP:2 * REL_CLIP + 1], (1, LANE))
    near_hi = rb_row[:, REL_CLIP:2 * REL_CLIP]
    near_lo = rb_row[:, 0:REL_CLIP]
    past = jnp.broadcast_to(rb_row[:, 0:1], (1, LANE))
    seg = [far, far, far, far, near_hi, near_lo] + [past] * (ET_ROWS // LANE - 5)
    ri = lax.broadcasted_iota(jnp.int32, (LANE, LANE), 0)
    ci = lax.broadcasted_iota(jnp.int32, (LANE, LANE), 1)
    for kb in range(ET_ROWS // LANE):
        wmat = jnp.where(ri + ci < LANE, seg[kb], seg[kb + 1])
        blk = pltpu.roll(wmat, 0, 1, stride=1, stride_axis=0)
        lag = LEFT_CHUNKS + ci // CHUNK - (2 * kb + ri // CHUNK)
        et_ref[kb * LANE:(kb + 1) * LANE, :] = jnp.where((lag >= 0) & (lag <= LEFT_CHUNKS), blk, NEG)


def _reduce_bias_table(det_ref):
    lane = lax.broadcasted_iota(jnp.int32, (1, LANE), 1)
    ri = lax.broadcasted_iota(jnp.int32, (LANE, LANE), 0)
    ci = lax.broadcasted_iota(jnp.int32, (LANE, LANE), 1)
    flip = jnp.where(ri + ci == LANE - 1, 1.0, 0.0).astype(BF16)
    segs = jnp.zeros((8, LANE), F32)
    seg_row = lax.broadcasted_iota(jnp.int32, (8, LANE), 0)
    prev_minus = jnp.zeros((1, LANE), F32)
    for kb in range(6):
        rolled = pltpu.roll(_dot01(det_ref[kb * LANE:(kb + 1) * LANE, :], flip, left=False), 0, 1,
                            stride=1, stride_axis=0)
        plus = jnp.sum(jnp.where(ci >= ri, rolled, 0.0), axis=0, keepdims=True)
        minus = jnp.sum(jnp.where(ci < ri, rolled, 0.0), axis=0, keepdims=True)
        segs = segs + jnp.where(seg_row == kb, plus + prev_minus, 0.0)
        prev_minus = minus
    segs = _dot01(segs, flip, left=False)
    pick = lambda kb: jnp.sum(jnp.where(seg_row == kb, segs, 0.0), axis=0, keepdims=True)
    far = jnp.sum(pick(0) + pick(1) + pick(2) + pick(3), axis=1, keepdims=True)
    last = jnp.where(lane == 0, far, 0.0)
    return jnp.concatenate([pick(5), pick(4), last], axis=1)


def _att_window(b):
    c0 = 2 * b
    kstart = pl.multiple_of(jnp.maximum(c0 - LEFT_CHUNKS, 0) * CHUNK, CHUNK)
    eoff = pl.multiple_of(jnp.maximum(LEFT_CHUNKS - c0, 0) * CHUNK, CHUNK)
    return kstart, eoff


def _att_probs(q_b, kw_b, et):
    st = _dot_nt(kw_b, q_b) * ATT_SCALE + et
    m = jnp.max(st, axis=0, keepdims=True)
    ex = jnp.exp(st - m)
    return ex * (1.0 / jnp.sum(ex, axis=0, keepdims=True))


def _att_fwd(z, rb_pad, g_att):
    s = z.shape[0]
    nblk = s // QB
    c_aq, c_ak, c_av, c_ag = [(OFF_AQ + i * D_ATT) // ATT_HD for i in range(4)]

    def body(q_ref, k_ref, v_ref, ag_ref, rb_ref, g_ref, y_ref, o_ref, et_ref, kb_ref, vb_ref):
        h = pl.program_id(0)
        b = pl.program_id(1)

        @pl.when(b == 0)
        def _():
            _build_bias_table(rb_ref[pl.ds(h, 1), :], et_ref)
            kb_ref[...] = k_ref[...].astype(BF16)
            vb_ref[...] = v_ref[...].astype(BF16)

        for j in range(ATT_UNROLL):
            rs = slice(j * QB, (j + 1) * QB)
            kstart, eoff = _att_window(b * ATT_UNROLL + j)
            q_b = q_ref[rs, :].astype(BF16)
            kw_b = kb_ref[pl.ds(kstart, WIN), :]
            vw_b = vb_ref[pl.ds(kstart, WIN), :]
            pt = _att_probs(q_b, kw_b, et_ref[pl.ds(eoff, WIN), :])
            o = _dot_tn(pt.astype(BF16), vw_b)
            o_ref[rs, :] = o
            r = lax.rsqrt(jnp.mean(o * o, axis=-1, keepdims=True) + EPS)
            ag = ag_ref[rs, :]
            y_ref[rs, :] = (o * r * g_ref[...] * (ag * _sigmoid(ag))).astype(BF16)

    blk = lambda col: pl.BlockSpec((ATT_UNROLL * QB, ATT_HD), lambda h, b: (b, col + h))
    seq = lambda col: pl.BlockSpec((s, ATT_HD), lambda h, b: (0, col + h))
    out_blk = pl.BlockSpec((ATT_UNROLL * QB, ATT_HD), lambda h, b: (b, h))
    return pl.pallas_call(
        body, name="att_fwd", grid=(ATT_HEADS, nblk // ATT_UNROLL),
        in_specs=[blk(c_aq), seq(c_ak), seq(c_av), blk(c_ag),
                  pl.BlockSpec((ATT_HEADS, 3 * LANE), lambda h, b: (0, 0)),
                  pl.BlockSpec((1, ATT_HD), lambda h, b: (0, h))],
        out_specs=[out_blk, out_blk],
        out_shape=[jax.ShapeDtypeStruct((s, D_ATT), BF16), jax.ShapeDtypeStruct((s, D_ATT), F32)],
        scratch_shapes=[pltpu.VMEM((ET_ROWS, LANE), F32), pltpu.VMEM((s, ATT_HD), BF16),
                        pltpu.VMEM((s, ATT_HD), BF16)],
        compiler_params=_cparams(("arbitrary", "arbitrary")),
    )(z, z, z, z, rb_pad, g_att)


def _att_bwd(dyc, o_att, z, rb_pad, g_att):
    s = z.shape[0]
    nblk = s // QB
    c_aq, c_ak, c_av, c_ag = [(OFF_AQ + i * D_ATT) // ATT_HD for i in range(4)]
    c_dy = D_GLA // ATT_HD

    def body(dy_ref, o_ref, q_ref, k_ref, v_ref, ag_ref, rb_ref, g_ref,
             dq_ref, dk_ref, dv_ref, dag_ref, drb_ref, dg_ref, et_ref, det_ref, kb_ref, vb_ref, dk_acc, dv_acc):
        h = pl.program_id(0)
        b = pl.program_id(1)

        @pl.when(b == 0)
        def _():
            _build_bias_table(rb_ref[pl.ds(h, 1), :], et_ref)
            kb_ref[...] = k_ref[...].astype(BF16)
            vb_ref[...] = v_ref[...].astype(BF16)
            det_ref[...] = jnp.zeros_like(det_ref)
            dk_acc[...] = jnp.zeros_like(dk_acc)
            dv_acc[...] = jnp.zeros_like(dv_acc)
            dg_ref[...] = jnp.zeros_like(dg_ref)

        g = g_ref[...]
        dg = jnp.zeros((1, ATT_HD), F32)
        for j in range(ATT_UNROLL):
            rs = slice(j * QB, (j + 1) * QB)
            kstart, eoff = _att_window(b * ATT_UNROLL + j)
            q_b = q_ref[rs, :].astype(BF16)
            kw_b = kb_ref[pl.ds(kstart, WIN), :]
            vw_b = vb_ref[pl.ds(kstart, WIN), :]
            pt = _att_probs(q_b, kw_b, et_ref[pl.ds(eoff, WIN), :])
            o = o_ref[rs, :]
            ag = ag_ref[rs, :]
            dy = dy_ref[rs, :]
            r = lax.rsqrt(jnp.mean(o * o, axis=-1, keepdims=True) + EPS)
            sg = _sigmoid(ag)
            don = dy * (ag * sg)
            dag_ref[rs, :] = (dy * (o * r * g) * (sg * (1.0 + ag * (1.0 - sg)))).astype(BF16)
            dg = dg + jnp.sum(don * o * r, axis=0, keepdims=True)
            w = don * g
            do_b = (r * (w - o * (r * r) * jnp.mean(w * o, axis=-1, keepdims=True))).astype(BF16)
            pt_b = pt.astype(BF16)
            dpt = _dot_nt(vw_b, do_b)
            dst = pt * (dpt - jnp.sum(dpt * pt, axis=0, keepdims=True))
            det_ref[pl.ds(eoff, WIN), :] += dst
            ds_b = (dst * ATT_SCALE).astype(BF16)
            dq_ref[rs, :] = _dot_tn(ds_b, kw_b).astype(BF16)
            dk_acc[pl.ds(kstart, WIN), :] += _dot(ds_b, q_b)
            dv_acc[pl.ds(kstart, WIN), :] += _dot(pt_b, do_b)
        dg_ref[...] += dg

        @pl.when(b == nblk // ATT_UNROLL - 1)
        def _():
            drb_ref[0] = jnp.broadcast_to(_reduce_bias_table(det_ref), (8, 3 * LANE))
            dk_ref[...] = dk_acc[...].astype(BF16)
            dv_ref[...] = dv_acc[...].astype(BF16)

    blk = lambda col: pl.BlockSpec((ATT_UNROLL * QB, ATT_HD), lambda h, b: (b, col + h))
    seq = lambda col: pl.BlockSpec((s, ATT_HD), lambda h, b: (0, col + h))
    out_blk = pl.BlockSpec((ATT_UNROLL * QB, ATT_HD), lambda h, b: (b, h))
    out_seq = pl.BlockSpec((s, ATT_HD), lambda h, b: (0, h))
    return pl.pallas_call(
        body, name="att_bwd", grid=(ATT_HEADS, nblk // ATT_UNROLL),
        in_specs=[blk(c_dy), blk(0), blk(c_aq), seq(c_ak), seq(c_av), blk(c_ag),
                  pl.BlockSpec((ATT_HEADS, 3 * LANE), lambda h, b: (0, 0)),
                  pl.BlockSpec((1, ATT_HD), lambda h, b: (0, h))],
        out_specs=[out_blk, out_seq, out_seq, out_blk,
                   pl.BlockSpec((1, 8, 3 * LANE), lambda h, b: (h, 0, 0)),
                   pl.BlockSpec((1, ATT_HD), lambda h, b: (0, h))],
        out_shape=[jax.ShapeDtypeStruct((s, D_ATT), BF16), jax.ShapeDtypeStruct((s, D_ATT), BF16),
                   jax.ShapeDtypeStruct((s, D_ATT), BF16), jax.ShapeDtypeStruct((s, D_ATT), BF16),
                   jax.ShapeDtypeStruct((ATT_HEADS, 8, 3 * LANE), F32),
                   jax.ShapeDtypeStruct((1, D_ATT), F32)],
        scratch_shapes=[pltpu.VMEM((ET_ROWS, LANE), F32), pltpu.VMEM((ET_ROWS, LANE), F32),
                        pltpu.VMEM((s, ATT_HD), BF16), pltpu.VMEM((s, ATT_HD), BF16),
                        pltpu.VMEM((s, ATT_HD), F32), pltpu.VMEM((s, ATT_HD), F32)],
        compiler_params=_cparams(("arbitrary", "arbitrary")),
    )(dyc, o_att, z, z, z, z, rb_pad, g_att)


ADAM_ROWS = 64
ADAM_COL_ROWS = 32


def _adam_math(w, g, m, v):
    m2 = ADAM_B1 * m + (1.0 - ADAM_B1) * g
    v2 = ADAM_B2 * v + (1.0 - ADAM_B2) * (g * g)
    m_hat = m2 / (1.0 - ADAM_B1 ** ADAM_STEP)
    v_hat = v2 / (1.0 - ADAM_B2 ** ADAM_STEP)
    delta = -ADAM_LR * (m_hat / (jnp.sqrt(v_hat) + ADAM_EPS) + ADAM_WD * w)
    return delta, m2, v2


def _adam_sharded(parts, first, w, m, v, name):
    nl, nr, nc = w.shape

    def body(*refs):
        p_refs = refs[:nl]
        w_ref, m_ref, v_ref, g_ref, d_ref, m2_ref, v2_ref = refs[nl:]
        for k in range(nl):
            @pl.when(pl.program_id(0) == k)
            def _(p_ref=p_refs[k]):
                g = p_ref[0].astype(F32)
                for dev in range(1, N_DEV):
                    g = g + p_ref[dev].astype(F32)
                delta, m2, v2 = _adam_math(w_ref[0], g, m_ref[0], v_ref[0])
                g_ref[0] = g
                d_ref[0] = delta
                m2_ref[0] = m2
                v2_ref[0] = v2

    def part_spec(k):
        return pl.BlockSpec((N_DEV, ADAM_ROWS, nc), lambda l, i: (0, first + jnp.where(l == k, i, 0), 0))

    blk = pl.BlockSpec((1, ADAM_ROWS, nc), lambda l, i: (l, i, 0))
    shp = jax.ShapeDtypeStruct(w.shape, F32)
    return pl.pallas_call(
        body, name=name, grid=(nl, pl.cdiv(nr, ADAM_ROWS)),
        in_specs=[part_spec(k) for k in range(nl)] + [blk, blk, blk],
        out_specs=[blk, blk, blk, blk],
        out_shape=[shp, shp, shp, shp],
        compiler_params=_cparams(("arbitrary", "arbitrary")),
    )(*parts, w, m, v)


def _adam_columns(parts, first, w, m, v):
    nc, nl, d = w.shape

    def body(*refs):
        p_refs = refs[:nl]
        w_ref, m_ref, v_ref, g_ref, d_ref, m2_ref, v2_ref = refs[nl:]
        for l in range(nl):
            g = p_refs[l][0].astype(F32)
            for dev in range(1, N_DEV):
                g = g + p_refs[l][dev].astype(F32)
            delta, m2, v2 = _adam_math(w_ref[:, l, :], g, m_ref[:, l, :], v_ref[:, l, :])
            g_ref[:, l, :] = g
            d_ref[:, l, :] = delta
            m2_ref[:, l, :] = m2
            v2_ref[:, l, :] = v2

    blk = pl.BlockSpec((ADAM_COL_ROWS, nl, d), lambda i: (i, 0, 0))
    part = pl.BlockSpec((N_DEV, ADAM_COL_ROWS, d), lambda i: (0, first + i, 0))
    shp = jax.ShapeDtypeStruct(w.shape, F32)
    return pl.pallas_call(
        body, name="adam_w_in", grid=(pl.cdiv(nc, ADAM_COL_ROWS),),
        in_specs=[part] * nl + [blk, blk, blk],
        out_specs=[blk, blk, blk, blk],
        out_shape=[shp, shp, shp, shp],
        compiler_params=_cparams(("parallel",)),
    )(*parts, w, m, v)


def _adam_small(ws, gs, ms, vs):
    n = len(ws)

    def body(*refs):
        w_refs, g_refs, m_refs, v_refs, d_refs, m2_refs, v2_refs = [refs[i * n:(i + 1) * n] for i in range(7)]
        for i in range(n):
            delta, m2, v2 = _adam_math(w_refs[i][...], g_refs[i][...], m_refs[i][...], v_refs[i][...])
            d_refs[i][...] = delta
            m2_refs[i][...] = m2
            v2_refs[i][...] = v2

    shapes = [jax.ShapeDtypeStruct(w.shape, F32) for w in ws]
    out = pl.pallas_call(body, name="adam_small", out_shape=shapes * 3)(*ws, *gs, *ms, *vs)
    return out[:n], out[n:2 * n], out[2 * n:]


def _position():
    return lax.axis_index("x"), lax.axis_index("y"), lax.axis_index("c")


def _slot(p):
    return 4 * p[0] + 2 * p[1] + p[2]


BF16_TILE_ROWS = 16


def _slab_rows(rows, cols):
    return -(-(rows + cols) // BF16_TILE_ROWS) * BF16_TILE_ROWS


RELAYOUT_COLS = 1024
RELAYOUT_CHUNK = 64


def _shard_pieces(dev, rows, cols):
    moved = ((0, GA_ORIG, 0), (GA_ORIG, GA_ORIG + GLA_RANK, OFF_GA - GA_ORIG), (GA_ORIG + GLA_RANK, D_IN, -GLA_RANK))
    c0, c1 = dev * cols, (dev + 1) * cols
    return [(rows + max(c0, lo) - c0, max(c0, lo) + off, min(c1, hi) - max(c0, lo))
            for lo, hi, off in moved if max(c0, lo) < min(c1, hi)]


def _move_rows(src, src_row, dst, dst_row, n):
    assert src_row % 2 == 0 and dst_row % 2 == 0 and n % 2 == 0
    for r in range(0, n // 2, RELAYOUT_CHUNK):
        m = min(RELAYOUT_CHUNK, n // 2 - r)
        dst[dst_row // 2 + r:dst_row // 2 + r + m, :] = src[src_row // 2 + r:src_row // 2 + r + m, :]


def _aligned_weight(land, rows, cols):
    _, slab, d = land.shape
    ct = min(RELAYOUT_COLS, d)

    def body(land_ref, wt_ref, wo_ref):
        dev = pl.program_id(1)
        src = land_ref.bitcast(jnp.uint32)
        dst = wt_ref.bitcast(jnp.uint32)
        wo_ref[...] = land_ref[0:rows, :]

        @pl.when(dev == 0)
        def _():
            dst[D_IN // 2:D_ZP // 2, :] = jnp.zeros(((D_ZP - D_IN) // 2, ct), jnp.uint32)

        for k in range(N_DEV):
            @pl.when(dev == k)
            def _(k=k):
                for at, to, n in _shard_pieces(k, rows, cols):
                    _move_rows(src, at, dst, to, n)

    return pl.pallas_call(
        body, name="aligned_weight", grid=(d // ct, N_DEV),
        in_specs=[pl.BlockSpec((slab, ct), lambda c, dev: (dev, c))],
        out_specs=[pl.BlockSpec((D_ZP, ct), lambda c, dev: (0, c)),
                   pl.BlockSpec((rows, ct), lambda c, dev: (dev, c))],
        out_shape=[jax.ShapeDtypeStruct((D_ZP, d), land.dtype),
                   jax.ShapeDtypeStruct((N_DEV * rows, d), land.dtype)],
        compiler_params=_cparams(("parallel", "arbitrary")),
    )(land.reshape(N_DEV * slab, d))


def _partial_slabs(dwt, cols):
    d = dwt[0].shape[1]
    bounds = (0, GA_ORIG, OFF_GA, D_ZP)
    assert tuple(a.shape[0] for a in dwt) == tuple(hi - lo for lo, hi in zip(bounds, bounds[1:]))
    slab = _slab_rows(0, cols)
    ct = min(RELAYOUT_COLS, d)

    def body(*refs):
        out_ref = refs[-1]
        dev = pl.program_id(1)
        srcs = [ref.bitcast(jnp.uint32) for ref in refs[:-1]]
        dst = out_ref.bitcast(jnp.uint32)
        dst[cols // 2:slab // 2, :] = jnp.zeros(((slab - cols) // 2, ct), jnp.uint32)
        for k in range(N_DEV):
            @pl.when(dev == k)
            def _(k=k):
                for to, at, n in _shard_pieces(k, 0, cols):
                    which = max(i for i, lo in enumerate(bounds[:-1]) if lo <= at)
                    assert at + n <= bounds[which + 1]
                    _move_rows(srcs[which], at - bounds[which], dst, to, n)

    return pl.pallas_call(
        body, name="partial_slabs", grid=(d // ct, N_DEV),
        in_specs=[pl.BlockSpec((a.shape[0], ct), lambda c, dev: (0, c)) for a in dwt],
        out_specs=pl.BlockSpec((slab, ct), lambda c, dev: (dev, c)),
        out_shape=jax.ShapeDtypeStruct((N_DEV * slab, d), dwt[0].dtype),
        compiler_params=_cparams(("parallel", "arbitrary")),
    )(*dwt).reshape(N_DEV, slab, d)


def _peer(pos, k):
    x, y, c = pos
    return (1 - x if k & 4 else x, 1 - y if k & 2 else y, 1 - c if k & 1 else c)


HBM_SPEC = pl.BlockSpec(memory_space=pltpu.HBM)
SEM_SPEC = pl.BlockSpec(memory_space=pltpu.SEMAPHORE)
GATHER_PEERS = (1, 4, 2, 6)
ALL_PEERS = (1, 2, 3, 4, 5, 6, 7)


def _hbm(a):
    return pltpu.with_memory_space_constraint(a, pltpu.HBM)


def _split_copies(src_ref, land_ref, send_sems, recv_sems, ks, per_peer, landed):
    me = _position()
    out = []
    for i, k in enumerate(ks):
        peer = _peer(me, k)
        src = src_ref.at[_slot(peer)] if per_peer else src_ref
        dst = land_ref.at[_slot(peer) if landed else _slot(me)]
        out.append(pltpu.make_async_remote_copy(
            src_ref=src, dst_ref=dst, send_sem=send_sems.at[i], recv_sem=recv_sems.at[i],
            device_id=peer, device_id_type=MESH))
    return out


def _exchange_start(src, after, ks, per_peer, name):
    slab = src.shape[1:] if per_peer else src.shape
    land_shape = (N_DEV,) + tuple(slab)
    n = len(ks)

    def body(src_ref, land_ref, after_ref, send_sems, recv_sems, src_thru, land_thru, token):
        for cp in _split_copies(src_ref, land_ref, send_sems, recv_sems, ks, per_peer, landed=False):
            cp.start()
        token[...] = jnp.zeros_like(token)

    return pl.pallas_call(
        body, name=name,
        out_shape=(pltpu.SemaphoreType.DMA((n,)), pltpu.SemaphoreType.DMA((n,)),
                   pltpu.HBM(src.shape, src.dtype), pltpu.HBM(land_shape, src.dtype),
                   jax.ShapeDtypeStruct((8, LANE), F32)),
        in_specs=(HBM_SPEC, HBM_SPEC, ANY),
        out_specs=(SEM_SPEC, SEM_SPEC, HBM_SPEC, HBM_SPEC, pl.BlockSpec(memory_space=pltpu.VMEM)),
        input_output_aliases={0: 2, 1: 3},
        compiler_params=pltpu.CompilerParams(has_side_effects=pltpu.SideEffectType.DATAFLOW_SIDE_EFFECTING),
    )(_hbm(src), _hbm(lax.empty(land_shape, src.dtype)), after)


def _exchange_wait(started, after, ks, per_peer, name):
    send_sems, recv_sems, src_thru, land_thru = started

    def body(src_ref, land_ref, send_sems, recv_sems, after_ref, src_dead, land_out):
        for cp in _split_copies(src_ref, land_ref, send_sems, recv_sems, ks, per_peer, landed=True):
            cp.wait_send()
            cp.wait_recv()

    return pl.pallas_call(
        body, name=name,
        out_shape=(pltpu.HBM(src_thru.shape, src_thru.dtype), pltpu.HBM(land_thru.shape, land_thru.dtype)),
        in_specs=(HBM_SPEC, HBM_SPEC, SEM_SPEC, SEM_SPEC, ANY), out_specs=(HBM_SPEC, HBM_SPEC),
        input_output_aliases={0: 0, 1: 1},
        compiler_params=pltpu.CompilerParams(has_side_effects=pltpu.SideEffectType.DATAFLOW_SIDE_EFFECTING),
    )(src_thru, land_thru, send_sems, recv_sems, after)


def _relay_copies(land_ref, send_sems, recv_sems, landed):
    me = _position()
    sibling = _peer(me, 1)
    out = []
    for i, k in enumerate(GATHER_PEERS[1:]):
        blk = land_ref.at[_slot(_peer(sibling if landed else me, k))]
        out.append(pltpu.make_async_remote_copy(
            src_ref=blk, dst_ref=blk, send_sem=send_sems.at[i], recv_sem=recv_sems.at[i],
            device_id=sibling, device_id_type=MESH))
    return out


def _relay_start(land, name):
    n = len(GATHER_PEERS) - 1

    def body(land_ref, send_sems, recv_sems, land_thru, token):
        for cp in _relay_copies(land_ref, send_sems, recv_sems, landed=False):
            cp.start()
        token[...] = jnp.zeros_like(token)

    return pl.pallas_call(
        body, name=name,
        out_shape=(pltpu.SemaphoreType.DMA((n,)), pltpu.SemaphoreType.DMA((n,)),
                   pltpu.HBM(land.shape, land.dtype), jax.ShapeDtypeStruct((8, LANE), F32)),
        in_specs=(HBM_SPEC,),
        out_specs=(SEM_SPEC, SEM_SPEC, HBM_SPEC, pl.BlockSpec(memory_space=pltpu.VMEM)),
        input_output_aliases={0: 2},
        compiler_params=pltpu.CompilerParams(has_side_effects=pltpu.SideEffectType.DATAFLOW_SIDE_EFFECTING),
    )(_hbm(land))


def _relay_wait(started, after, name):
    send_sems, recv_sems, land_thru = started

    def body(land_ref, send_sems, recv_sems, after_ref, land_out):
        for cp in _relay_copies(land_ref, send_sems, recv_sems, landed=True):
            cp.wait_send()
            cp.wait_recv()

    return pl.pallas_call(
        body, name=name,
        out_shape=pltpu.HBM(land_thru.shape, land_thru.dtype),
        in_specs=(HBM_SPEC, SEM_SPEC, SEM_SPEC, ANY), out_specs=HBM_SPEC,
        input_output_aliases={0: 0},
        compiler_params=pltpu.CompilerParams(has_side_effects=pltpu.SideEffectType.DATAFLOW_SIDE_EFFECTING),
    )(land_thru, send_sems, recv_sems, after)


def _share(vec, name, after=None):
    follows = [] if after is None else [after]

    def body(vec_ref, *rest):
        out_ref, send_sems, recv_sems, local_sem = rest[len(follows):]
        me = _position()

        def copy(k, landed):
            peer = _peer(me, k)
            return pltpu.make_async_remote_copy(
                src_ref=vec_ref, dst_ref=out_ref.at[_slot(peer) if landed else _slot(me)],
                send_sem=send_sems.at[k - 1], recv_sem=recv_sems.at[k - 1], device_id=peer, device_id_type=MESH)

        mine = pltpu.make_async_copy(vec_ref, out_ref.at[_slot(me)], local_sem)
        mine.start()
        sent = [copy(k, False) for k in ALL_PEERS]
        for cp in sent:
            cp.start()
        for k in ALL_PEERS:
            copy(k, True).wait_recv()
        for cp in sent:
            cp.wait_send()
        mine.wait()

    return pl.pallas_call(
        body, name=name,
        in_specs=[ANY] * (1 + len(follows)), out_specs=ANY,
        out_shape=jax.ShapeDtypeStruct((N_DEV,) + vec.shape, vec.dtype),
        scratch_shapes=[pltpu.SemaphoreType.DMA((N_DEV - 1,)), pltpu.SemaphoreType.DMA((N_DEV - 1,)),
                        pltpu.SemaphoreType.DMA],
    )(vec, *follows)


def _sum_slots(parts):
    def body(p_ref, o_ref):
        acc = p_ref[0]
        for dev in range(1, N_DEV):
            acc = acc + p_ref[dev]
        o_ref[...] = acc

    return pl.pallas_call(body, name="sum_slots",
                          out_shape=jax.ShapeDtypeStruct(parts.shape[1:], F32))(parts)


PACK_ROWS = 8


def _packed_rows(size):
    return -(-size // (PACK_ROWS * LANE)) * PACK_ROWS


def _pack(arrs):
    def rows(a):
        flat = a.reshape(-1)
        return jnp.pad(flat, (0, _packed_rows(flat.shape[0]) * LANE - flat.shape[0])).reshape(-1, LANE)

    return jnp.concatenate([rows(a) for a in arrs], axis=0)


def _unpack(packed, shapes):
    out, at = [], 0
    for shp in shapes:
        size = 1
        for dim in shp:
            size *= dim
        nrows = _packed_rows(size)
        out.append(packed[at:at + nrows].reshape(-1)[:size].reshape(shp))
        at += nrows
    return out


def _layer_fwd(x, wt, wo, g_pre, g_post, wa_pad, b_alpha, g_gla, g_att, rb_pad, midway=None):
    h = _rms_fwd(x, g_pre)
    z = _matmul(h, wt, "nt", F32, *TILES["in_proj"], "in_proj", n_outer=True)
    y_gla, o_gla, states = _gla_fwd(z, wa_pad, b_alpha, g_gla)
    if midway is not None:
        g_att = g_att + midway(y_gla)[:1, :1]
    y_att, o_att = _att_fwd(z, rb_pad, g_att)
    y = _matmul_cols([y_gla, y_att], wo, F32, *TILES["out_proj"][:2], "out_proj")
    out = _post_fwd(x, y, g_post)
    return out, (x, h, z, o_gla, states, o_att, y_gla, y_att, y)


def _layer_bwd(dout, saved, wt, wo, g_pre, g_post, wa_pad, b_alpha, g_gla, g_att, rb_pad, on_dwo, on_dwt):
    x, h, z, o_gla, states, o_att, y_gla, y_att, y = saved
    dy, dg_post = _post_bwd(dout, y, g_post)
    dwo = _matmul_rows([y_gla, y_att], dy, BF16, *TILES["out_proj_dw"][:2], "out_proj_dw")
    token = on_dwo(dwo)
    dycat = _matmul(dy, wo, "nt", F32, *TILES["out_proj_dx"], "out_proj_dx", n_outer=True, after=token)
    dq, dk, dv, dgg, dga, dwa, db, dg_gla = _gla_bwd(dycat, o_gla, z, wa_pad, b_alpha, g_gla, states)
    daq, dak, dav, dag, drb, dg_att = _att_bwd(dycat, o_att, z, rb_pad, g_att)
    tw, tn = TILES["in_proj_dw"][:2]
    dwt = (_matmul_rows([dq, dk, dv, dgg], h, BF16, tw, tn, "in_proj_dw_gla"),
           _matmul_rows([daq, dak, dav, dag], h, BF16, tw, tn, "in_proj_dw_att"),
           _matmul_rows([dga], h, BF16, LANE, tn, "in_proj_dw_gate"))
    token = on_dwt(dwt)
    dh = _matmul_cols([dq, dk, dv, dgg, daq, dak, dav, dag, dga], wt, F32, *TILES["in_proj_dx"][:2],
                      "in_proj_dx", after=token)
    dx, dg_pre = _pre_bwd(dh, x, g_pre, dout)
    small = (dg_pre[0], dg_post[0], dwa[:GLA_RANK], db[0], dg_gla[0], dg_att[0], drb[:, 0, :N_REL])
    return dx, small


def kernel(x, w_in, w_out, g_pre, g_post, w_alpha, b_alpha, g_gla, g_att, rel_bias, loss_target, m_w_in, m_w_out, m_g_pre, m_g_post, m_w_alpha, m_b_alpha, m_g_gla, m_g_att, m_rel_bias, v_w_in, v_w_out, v_g_pre, v_g_post, v_w_alpha, v_b_alpha, v_g_gla, v_g_att, v_rel_bias):
    nl, d, cols = w_in.shape
    rows = w_out.shape[1]
    s = x.shape[1]
    x0 = x.reshape(s, d)
    tgt = loss_target.reshape(s, d)

    cols_first = lambda a: jnp.transpose(a, (2, 0, 1))
    w_c = cols_first(w_in)
    slab = _slab_rows(rows, cols)
    is_out = lax.broadcasted_iota(jnp.int32, (slab, d), 0) < rows

    def shard(l, zero=0.0):
        top = jnp.pad((w_out[l] + zero).astype(BF16), ((0, slab - rows), (0, 0)))
        rest = jnp.pad((w_c[:, l] + zero).astype(BF16), ((rows, slab - rows - cols), (0, 0)))
        return jnp.where(is_out, top, rest)

    first_fetch = _exchange_start(shard(0), x, GATHER_PEERS, False, "gather_start_0")
    began = first_fetch[4][0, 0]
    shards = [None] + [shard(l, began) for l in range(1, nl)]
    alpha = _pack([w_alpha]) + began
    wa_g = _share(alpha, "gather_alpha")
    wa_cols = w_alpha.shape[2]
    wa_full = wa_g.reshape(N_DEV, -1)[:, :nl * GLA_RANK * wa_cols].reshape(N_DEV, nl, GLA_RANK, wa_cols)
    wa_full = jnp.transpose(wa_full, (1, 2, 0, 3)).reshape(nl, GLA_RANK, GLA_KW)
    wa_pad = jnp.pad(wa_full, ((0, 0), (0, LANE - GLA_RANK), (0, 0)))
    rb_pad = jnp.pad(rel_bias, ((0, 0), (0, 0), (0, 3 * LANE - N_REL)))

    def layer_args(l, follows=None):
        gp = g_pre[l:l + 1] if follows is None else g_pre[l:l + 1] + follows[:1, :1]
        return (wts[l], wos[l], gp, g_post[l:l + 1], wa_pad[l], b_alpha[l:l + 1], g_gla[l:l + 1],
                g_att[l:l + 1], rb_pad[l])

    my = _slot(_position())

    def fetch(l, after):
        return _exchange_start(shards[l], after, GATHER_PEERS, False, f"gather_start_{l}")

    def relay(l, first_hop, after):
        own[l], land = _exchange_wait(first_hop[:4], after, GATHER_PEERS, False, f"gather_wait_{l}")
        return _relay_start(land, f"relay_start_{l}")

    def midway(l, y):
        flight["relay"] = relay(l + 1, flight["fetch"], y)
        if l + 2 >= nl:
            return flight["relay"][3]
        flight["fetch"] = fetch(l + 2, flight["relay"][2])
        return flight["fetch"][4]

    act, saved, wts, wos, flight, own = x0, [], [], [], {}, [None] * nl
    prepared = (wa_pad[0, :1, :1] + sum(sh[:1, :1].astype(F32) for sh in shards[1:]))
    flight["relay"] = relay(0, first_fetch, prepared)
    if nl > 1:
        flight["fetch"] = fetch(1, flight["relay"][2])
    for l in range(nl):
        land = _relay_wait(flight["relay"][:3], act, f"relay_wait_{l}")
        land = lax.dynamic_update_slice_in_dim(land, own[l][None], my, 0)
        wt_l, wo_l = _aligned_weight(land, rows, cols)
        wts.append(wt_l)
        wos.append(wo_l)
        act, sv = _layer_fwd(act, *layer_args(l, follows=first_fetch[4] if l == 0 else None),
                             midway=functools.partial(midway, l) if l + 1 < nl else None)
        saved.append(sv)
    dout, sq = _loss_head(act, tgt)
    loss = lax.psum(sq[0, 0] * (0.5 / d), ("x", "y", "c"))

    smalls, pending_out, pending_in = [None] * nl, [None] * nl, [None] * nl

    def send_out(l, dwo):
        pending_out[l] = _exchange_start(dwo.reshape(N_DEV, rows, d), dwo[:1, :1], ALL_PEERS, True,
                                         f"scatter_out_start_{l}")
        return pending_out[l][4]

    def send_in(l, dwt):
        pending_in[l] = _exchange_start(_partial_slabs(dwt, cols), dwt[-1], ALL_PEERS, True,
                                        f"scatter_in_start_{l}")
        return pending_in[l][4]

    for l in reversed(range(nl)):
        dout, smalls[l] = _layer_bwd(dout, saved[l], *layer_args(l), on_dwo=functools.partial(send_out, l),
                                     on_dwt=functools.partial(send_in, l))
    grad_x = dout.reshape(x.shape)

    def landed(started, after, name):
        partial, land = _exchange_wait(started[:4], after, ALL_PEERS, True, name)
        return lax.dynamic_update_slice_in_dim(land, lax.dynamic_slice_in_dim(partial, my, 1, 0), my, 0)

    parts_out = [landed(pending_out[l], dout, f"scatter_out_wait_{l}") for l in range(nl)]
    g_w_out, d_w_out, m2_w_out, v2_w_out = _adam_sharded(parts_out, 0, w_out, m_w_out, v_w_out, "adam_w_out")
    names = 7
    small_stacked = [jnp.stack([smalls[l][i] for l in range(nl)]) for i in range(names)]
    shapes = [a.shape for a in small_stacked]
    gathered = _share(_pack(small_stacked), "gather_small_grads", after=d_w_out)
    g_pre_g, g_post_g, wa_g_full, b_g, gla_g, att_g, rb_g = _unpack(_sum_slots(gathered), shapes)
    wa_g_mine = lax.dynamic_slice_in_dim(wa_g_full, my * wa_cols, wa_cols, axis=2)
    grads = [g_pre_g, g_post_g, wa_g_mine, b_g, gla_g, att_g, rb_g]
    ws = [g_pre, g_post, w_alpha, b_alpha, g_gla, g_att, rel_bias]
    ms = [m_g_pre, m_g_post, m_w_alpha, m_b_alpha, m_g_gla, m_g_att, m_rel_bias]
    vs = [v_g_pre, v_g_post, v_w_alpha, v_b_alpha, v_g_gla, v_g_att, v_rel_bias]
    d_s, m2_s, v2_s = _adam_small(ws, grads, ms, vs)

    parts_in = [landed(pending_in[l], d_s[0], f"scatter_in_wait_{l}") for l in range(nl)]
    g_w_in, d_w_in, m2_w_in, v2_w_in = [
        jnp.transpose(a, (1, 2, 0))
        for a in _adam_columns(parts_in, 0, w_c, cols_first(m_w_in), cols_first(v_w_in))]

    def ordered(big_in, big_out, small):
        return [big_in, big_out] + list(small)

    return (loss, grad_x,
            *ordered(g_w_in, g_w_out, grads),
            *ordered(d_w_in, d_w_out, d_s),
            *ordered(m2_w_in, m2_w_out, m2_s),
            *ordered(v2_w_in, v2_w_out, v2_s))
```

```python
import functools

import jax
import jax.numpy as jnp
from jax import lax
from jax.experimental import pallas as pl
from jax.experimental.pallas import tpu as pltpu

F32 = jnp.float32
BF16 = jnp.bfloat16
MESH = pl.DeviceIdType.MESH
ANY = pl.BlockSpec(memory_space=pl.ANY)

CHUNK = 64
GLA_HEADS = 4
GLA_DK = 128
GLA_DV = 256
GLA_KW = GLA_HEADS * GLA_DK
D_GLA = GLA_HEADS * GLA_DV
GLA_RANK = 16
GLA_TAU = 16.0
ATT_HEADS = 8
ATT_HD = 128
D_ATT = ATT_HEADS * ATT_HD
LEFT_CHUNKS = 8
REL_CLIP = 128
N_REL = 2 * REL_CLIP + 1
EPS = 1e-6
D_IN = 2 * GLA_KW + 2 * D_GLA + GLA_RANK + 4 * D_ATT
GLA_SCALE = GLA_DK ** -0.5
ATT_SCALE = ATT_HD ** -0.5

ADAM_LR = 0.001
ADAM_B1 = 0.9
ADAM_B2 = 0.999
ADAM_EPS = 1e-08
ADAM_WD = 0.01
ADAM_STEP = 10

N_DEV = 8
LANE = 128
GA_ORIG = 2 * GLA_KW + 2 * D_GLA
OFF_AQ = GA_ORIG
OFF_GA = GA_ORIG + 4 * D_ATT
D_ZP = OFF_GA + LANE
QB = 2 * CHUNK
ATT_UNROLL = 8
WIN = (LEFT_CHUNKS + 2) * CHUNK
ET_ROWS = WIN + LEFT_CHUNKS * CHUNK
NEG = -1e30
VMEM_LIMIT = 48 * 1024 * 1024


def _cparams(sem):
    return pltpu.CompilerParams(dimension_semantics=sem, vmem_limit_bytes=VMEM_LIMIT)


def _dot(a, b):
    return jnp.dot(a, b, preferred_element_type=F32)


def _dot_nt(a, b):
    return lax.dot_general(a, b, (((1,), (1,)), ((), ())), preferred_element_type=F32)


def _dot_tn(a, b):
    return lax.dot_general(a, b, (((0,), (0,)), ((), ())), preferred_element_type=F32)


def _dot01(t, x, left=True):
    if not left:
        t, x = x, t
    hi = x.astype(BF16)
    r = x - hi.astype(F32)
    mid = r.astype(BF16)
    lo = (r - mid.astype(F32)).astype(BF16)
    if left:
        return _dot(t, hi) + _dot(t, mid) + _dot(t, lo)
    return _dot(hi, t) + _dot(mid, t) + _dot(lo, t)


def _sigmoid(x):
    return 1.0 / (1.0 + jnp.exp(-x))


def _log_sigmoid(x):
    return jnp.minimum(x, 0.0) - jnp.log(1.0 + jnp.exp(-jnp.abs(x)))


TILES = {
    "in_proj": (512, D_ZP // 3, None),
    "in_proj_dx": (512, 512, None),
    "in_proj_dw": (512, 2048, None),
    "out_proj": (512, 1024, None),
    "out_proj_dx": (512, 1024, None),
    "out_proj_dw": (1024, 1024, None),
}


def _matmul(a, b, mode, out_dtype, tm, tn, tk, name, n_outer=False, after=None):
    if mode == "nn":
        (m, k), n = a.shape, b.shape[1]
    elif mode == "nt":
        (m, k), n = a.shape, b.shape[0]
    else:
        (k, m), n = a.shape, b.shape[1]
    tm, tn, tk = min(tm, m), min(tn, n), k if tk is None else min(tk, k)
    assert m % tm == 0 and n % tn == 0 and k % tk == 0, (name, m, n, k)
    nk = k // tk
    dot = {"nn": _dot, "nt": _dot_nt, "tn": _dot_tn}[mode]

    follows = [] if after is None else [after]

    def body_whole_k(a_ref, b_ref, *rest):
        o_ref = rest[-1]
        o_ref[...] = dot(a_ref[...], b_ref[...]).astype(out_dtype)

    def body(a_ref, b_ref, *rest):
        o_ref, acc_ref = rest[-2:]
        kk = pl.program_id(2)

        @pl.when(kk == 0)
        def _():
            acc_ref[...] = jnp.zeros_like(acc_ref)

        acc_ref[...] += dot(a_ref[...], b_ref[...])

        @pl.when(kk == nk - 1)
        def _():
            o_ref[...] = acc_ref[...].astype(out_dtype)

    def at(index):
        return (lambda j, i, kk: index(i, j, kk)) if n_outer else index

    if mode == "tn":
        a_spec = pl.BlockSpec((tk, tm), at(lambda i, j, kk: (kk, i)))
    else:
        a_spec = pl.BlockSpec((tm, tk), at(lambda i, j, kk: (i, kk)))
    if mode == "nt":
        b_spec = pl.BlockSpec((tn, tk), at(lambda i, j, kk: (j, kk)))
    else:
        b_spec = pl.BlockSpec((tk, tn), at(lambda i, j, kk: (kk, j)))
    return pl.pallas_call(
        body_whole_k if nk == 1 else body, name=name,
        grid=(n // tn, m // tm, nk) if n_outer else (m // tm, n // tn, nk),
        in_specs=[a_spec, b_spec] + [ANY] * len(follows),
        out_specs=pl.BlockSpec((tm, tn), at(lambda i, j, kk: (i, j))),
        out_shape=jax.ShapeDtypeStruct((m, n), out_dtype),
        scratch_shapes=[] if nk == 1 else [pltpu.VMEM((tm, tn), F32)],
        compiler_params=_cparams(("parallel", "parallel", "arbitrary")),
    )(a, b, *follows)


def _matmul_cols(pieces, b, out_dtype, tm, tn, name, after=None):
    m, n = pieces[0].shape[0], b.shape[1]
    widths = [p.shape[1] for p in pieces]
    starts = [sum(widths[:i]) for i in range(len(pieces))]
    follows = [] if after is None else [after]
    tm, tn = min(tm, m), min(tn, n)
    assert sum(widths) == b.shape[0] and m % tm == 0 and n % tn == 0, name

    def body(*refs):
        b_ref, o_ref = refs[len(pieces)], refs[-1]
        acc = None
        for p_ref, at, width in zip(refs, starts, widths):
            part = _dot(p_ref[...], b_ref[at:at + width, :])
            acc = part if acc is None else acc + part
        o_ref[...] = acc.astype(out_dtype)

    return pl.pallas_call(
        body, name=name, grid=(n // tn, m // tm),
        in_specs=[pl.BlockSpec((tm, width), lambda j, i: (i, 0)) for width in widths]
        + [pl.BlockSpec((b.shape[0], tn), lambda j, i: (0, j))] + [ANY] * len(follows),
        out_specs=pl.BlockSpec((tm, tn), lambda j, i: (i, j)),
        out_shape=jax.ShapeDtypeStruct((m, n), out_dtype),
        compiler_params=_cparams(("parallel", "parallel")),
    )(*pieces, b, *follows)


def _matmul_rows(pieces, b, out_dtype, tw, tn, name):
    k, n = b.shape
    tn = min(tn, n)
    counts = [p.shape[1] // tw for p in pieces]
    firsts = [sum(counts[:i]) for i in range(len(pieces))]
    assert all(p.shape[1] % tw == 0 for p in pieces) and n % tn == 0, name

    def body(*refs):
        b_ref, o_ref = refs[len(pieces):]
        for p_ref, first, count in zip(refs, firsts, counts):
            @pl.when((pl.program_id(0) >= first) & (pl.program_id(0) < first + count))
            def _(p_ref=p_ref):
                o_ref[...] = _dot_tn(p_ref[...], b_ref[...]).astype(out_dtype)

    def piece_spec(first, count):
        return pl.BlockSpec((k, tw), lambda i, j: (0, jnp.clip(i - first, 0, count - 1)))

    return pl.pallas_call(
        body, name=name, grid=(sum(counts), n // tn),
        in_specs=[piece_spec(first, count) for first, count in zip(firsts, counts)]
        + [pl.BlockSpec((k, tn), lambda i, j: (0, j))],
        out_specs=pl.BlockSpec((tw, tn), lambda i, j: (i, j)),
        out_shape=jax.ShapeDtypeStruct((sum(counts) * tw, n), out_dtype),
        compiler_params=_cparams(("parallel", "parallel")),
    )(*pieces, b)


ROWS = 512


def _rms_fwd(x, g):
    s, d = x.shape

    def body(x_ref, g_ref, h_ref):
        xv = x_ref[...]
        r = lax.rsqrt(jnp.mean(xv * xv, axis=-1, keepdims=True) + EPS)
        h_ref[...] = (xv * r * g_ref[...]).astype(BF16)

    return pl.pallas_call(
        body, name="rms_fwd", grid=(s // ROWS,),
        in_specs=[pl.BlockSpec((ROWS, d), lambda i: (i, 0)), pl.BlockSpec((1, d), lambda i: (0, 0))],
        out_specs=pl.BlockSpec((ROWS, d), lambda i: (i, 0)),
        out_shape=jax.ShapeDtypeStruct((s, d), BF16),
        compiler_params=_cparams(("parallel",)),
    )(x, g)


def _post_fwd(x, y, g):
    s, d = x.shape

    def body(x_ref, y_ref, g_ref, o_ref):
        yv = y_ref[...]
        r = lax.rsqrt(jnp.mean(yv * yv, axis=-1, keepdims=True) + EPS)
        o_ref[...] = x_ref[...] + yv * r * g_ref[...]

    row = pl.BlockSpec((ROWS, d), lambda i: (i, 0))
    return pl.pallas_call(
        body, name="post_fwd", grid=(s // ROWS,),
        in_specs=[row, row, pl.BlockSpec((1, d), lambda i: (0, 0))],
        out_specs=row,
        out_shape=jax.ShapeDtypeStruct((s, d), F32),
        compiler_params=_cparams(("parallel",)),
    )(x, y, g)


def _loss_head(out, tgt):
    s, d = out.shape

    def body(o_ref, t_ref, dout_ref, sum_ref):
        @pl.when(pl.program_id(0) == 0)
        def _():
            sum_ref[...] = jnp.zeros_like(sum_ref)

        e = o_ref[...] - t_ref[...]
        dout_ref[...] = e * (1.0 / d)
        sum_ref[...] += jnp.sum(jnp.sum(e * e, axis=1, keepdims=True), axis=0, keepdims=True)

    row = pl.BlockSpec((ROWS, d), lambda i: (i, 0))
    return pl.pallas_call(
        body, name="loss_head", grid=(s // ROWS,),
        in_specs=[row, row],
        out_specs=[row, pl.BlockSpec((1, 1), lambda i: (0, 0))],
        out_shape=[jax.ShapeDtypeStruct((s, d), F32), jax.ShapeDtypeStruct((1, 1), F32)],
        compiler_params=_cparams(("arbitrary",)),
    )(out, tgt)


def _post_bwd(dout, y, g):
    s, d = y.shape

    def body(do_ref, y_ref, g_ref, dy_ref, dg_ref):
        @pl.when(pl.program_id(0) == 0)
        def _():
            dg_ref[...] = jnp.zeros_like(dg_ref)

        yv = y_ref[...]
        dv = do_ref[...]
        r = lax.rsqrt(jnp.mean(yv * yv, axis=-1, keepdims=True) + EPS)
        dg_ref[...] += jnp.sum(dv * yv * r, axis=0, keepdims=True)
        w = dv * g_ref[...]
        dy = r * (w - yv * (r * r) * jnp.mean(w * yv, axis=-1, keepdims=True))
        dy_ref[...] = dy.astype(BF16)

    row = pl.BlockSpec((ROWS, d), lambda i: (i, 0))
    vec = pl.BlockSpec((1, d), lambda i: (0, 0))
    return pl.pallas_call(
        body, name="post_bwd", grid=(s // ROWS,),
        in_specs=[row, row, vec],
        out_specs=[row, vec],
        out_shape=[jax.ShapeDtypeStruct((s, d), BF16), jax.ShapeDtypeStruct((1, d), F32)],
        compiler_params=_cparams(("arbitrary",)),
    )(dout, y, g)


def _pre_bwd(dh, x, g, dout):
    s, d = x.shape

    def body(dh_ref, x_ref, g_ref, do_ref, dx_ref, dg_ref):
        @pl.when(pl.program_id(0) == 0)
        def _():
            dg_ref[...] = jnp.zeros_like(dg_ref)

        xv = x_ref[...]
        dv = dh_ref[...]
        r = lax.rsqrt(jnp.mean(xv * xv, axis=-1, keepdims=True) + EPS)
        dg_ref[...] += jnp.sum(dv * xv * r, axis=0, keepdims=True)
        w = dv * g_ref[...]
        dx_ref[...] = do_ref[...] + r * (w - xv * (r * r) * jnp.mean(w * xv, axis=-1, keepdims=True))

    row = pl.BlockSpec((ROWS, d), lambda i: (i, 0))
    vec = pl.BlockSpec((1, d), lambda i: (0, 0))
    return pl.pallas_call(
        body, name="pre_bwd", grid=(s // ROWS,),
        in_specs=[row, row, vec, row],
        out_specs=[row, vec],
        out_shape=[jax.ShapeDtypeStruct((s, d), F32), jax.ShapeDtypeStruct((1, d), F32)],
        compiler_params=_cparams(("arbitrary",)),
    )(dh, x, g, dout)


GLA_STEP = 4
GLA_ROWS = GLA_STEP * CHUNK
GLA_CHUNKS = [slice(c * CHUNK, (c + 1) * CHUNK) for c in range(GLA_STEP)]


def _chunk_triangles():
    ri = lax.broadcasted_iota(jnp.int32, (GLA_ROWS, GLA_ROWS), 0)
    ci = lax.broadcasted_iota(jnp.int32, (GLA_ROWS, GLA_ROWS), 1)
    same = (ri // CHUNK) == (ci // CHUNK)
    return (jnp.where(same & (ri >= ci), 1.0, 0.0).astype(BF16), jnp.where(same & (ci >= ri), 1.0, 0.0).astype(BF16))


def _per_chunk(fn, like):
    row = lax.broadcasted_iota(jnp.int32, like.shape, 0)
    return [fn((row >= c * CHUNK) & (row < (c + 1) * CHUNK)) for c in range(GLA_STEP)]


def _spread(per_chunk, like):
    row = lax.broadcasted_iota(jnp.int32, like.shape, 0)
    out = per_chunk[-1]
    for c in reversed(range(GLA_STEP - 1)):
        out = jnp.where(row < (c + 1) * CHUNK, per_chunk[c], out)
    return out


def _gla_gate(ga_b, wa_b, b_ref, tri):
    pre = _dot(ga_b, wa_b) + b_ref[...]
    la = _log_sigmoid(pre) * (1.0 / GLA_TAU)
    cum = _dot01(tri, la)
    row = lax.broadcasted_iota(jnp.int32, cum.shape, 0)
    cends = [jnp.sum(jnp.where(row == (c + 1) * CHUNK - 1, cum, 0.0), axis=0, keepdims=True)
             for c in range(GLA_STEP)]
    return pre, cum, cends


def _heads(width):
    return [slice(h * width, (h + 1) * width) for h in range(GLA_HEADS)]


def _z_specs_gla(rev=None):
    idx = (lambda n: n) if rev is None else rev
    return [
        pl.BlockSpec((GLA_ROWS, GLA_KW), lambda n: (idx(n), 0)),
        pl.BlockSpec((GLA_ROWS, GLA_KW), lambda n: (idx(n), 1)),
        pl.BlockSpec((GLA_ROWS, D_GLA), lambda n: (idx(n), 1)),
        pl.BlockSpec((GLA_ROWS, D_GLA), lambda n: (idx(n), 2)),
        pl.BlockSpec((GLA_ROWS, LANE), lambda n: (idx(n), OFF_GA // LANE)),
    ]


def _gla_fwd(z, wa_pad, b_alpha, g_gla):
    s = z.shape[0]
    nchunk = s // CHUNK

    def body(q_ref, k_ref, v_ref, gg_ref, ga_ref, wa_ref, b_ref, g_ref, y_ref, o_ref, st_ref, state):
        @pl.when(pl.program_id(0) == 0)
        def _():
            state[...] = jnp.zeros_like(state)

        ga_b = ga_ref[...].astype(BF16)
        tri, _ = _chunk_triangles()
        nh = range(GLA_HEADS)
        keys, vals = _heads(GLA_DK), _heads(GLA_DV)
        _, cum, cends = _gla_gate(ga_b, wa_ref[...].astype(BF16), b_ref, tri)
        kd_b = (k_ref[...] * jnp.exp(_spread(cends, cum) - cum)).astype(BF16)
        qs = (q_ref[...] * GLA_SCALE).astype(BF16)
        v_b = v_ref[...].astype(BF16)
        uts = [[_dot_tn(v_b[rs, vals[h]], kd_b[rs, keys[h]]) for h in nh] for rs in GLA_CHUNKS]
        sts, prev = [], [state[h] for h in nh]
        for c in range(GLA_STEP):
            a = jnp.exp(cends[c])
            prev = [prev[h] * a[:, keys[h]] + uts[c][h] for h in nh]
            sts.append(prev)
        for h in nh:
            state[h] = prev[h]
            for c in range(GLA_STEP):
                st_ref[c, h] = sts[c][h]
        outs = [[_dot_nt(qs[rs, keys[h]], sts[c][h].astype(BF16)) for h in nh] for c, rs in enumerate(GLA_CHUNKS)]
        for h in nh:
            o, vs = jnp.concatenate([outs[c][h] for c in range(GLA_STEP)], axis=0), vals[h]
            o_ref[:, vs] = o
            r = lax.rsqrt(jnp.mean(o * o, axis=-1, keepdims=True) + EPS)
            gg = gg_ref[:, vs]
            y_ref[:, vs] = (o * r * g_ref[:, vs] * (gg * _sigmoid(gg))).astype(BF16)

    full = lambda shape: pl.BlockSpec(shape, lambda n: tuple(0 for _ in shape))
    wide = pl.BlockSpec((GLA_ROWS, D_GLA), lambda n: (n, 0))
    return pl.pallas_call(
        body, name="gla_fwd", grid=(nchunk // GLA_STEP,),
        in_specs=_z_specs_gla() + [full((LANE, GLA_KW)), full((1, GLA_KW)), full((1, D_GLA))],
        out_specs=[wide, wide, pl.BlockSpec((GLA_STEP, GLA_HEADS, GLA_DV, GLA_DK), lambda n: (n, 0, 0, 0))],
        out_shape=[jax.ShapeDtypeStruct((s, D_GLA), BF16), jax.ShapeDtypeStruct((s, D_GLA), F32),
                   jax.ShapeDtypeStruct((nchunk, GLA_HEADS, GLA_DV, GLA_DK), F32)],
        scratch_shapes=[pltpu.VMEM((GLA_HEADS, GLA_DV, GLA_DK), F32)],
        compiler_params=_cparams(("arbitrary",)),
    )(z, z, z, z, z, wa_pad, b_alpha, g_gla)


def _gla_bwd(dyc, o_gla, z, wa_pad, b_alpha, g_gla, states):
    s = z.shape[0]
    nsteps = s // GLA_ROWS
    rev = lambda n: nsteps - 1 - n

    def body(dy_ref, o_ref, q_ref, k_ref, v_ref, gg_ref, ga_ref, wa_ref, b_ref, g_ref, st_ref, stp_ref,
             dq_ref, dk_ref, dv_ref, dgg_ref, dga_ref, dwa_ref, db_ref, dg_ref, carry):
        step = pl.program_id(0)

        @pl.when(step == 0)
        def _():
            carry[...] = jnp.zeros_like(carry)
            dwa_ref[...] = jnp.zeros_like(dwa_ref)
            db_ref[...] = jnp.zeros_like(db_ref)
            dg_ref[...] = jnp.zeros_like(dg_ref)

        has_prev = (step < nsteps - 1).astype(F32)
        ga_b = ga_ref[...].astype(BF16)
        tri, tri_up = _chunk_triangles()
        nh, nc = range(GLA_HEADS), range(GLA_STEP)
        keys, vals = _heads(GLA_DK), _heads(GLA_DV)
        wa_b = wa_ref[...].astype(BF16)
        pre, cum, cends = _gla_gate(ga_b, wa_b, b_ref, tri)
        e = jnp.exp(_spread(cends, cum) - cum)
        a = [jnp.exp(cends[c]) for c in nc]
        kf = k_ref[...]
        kd_b = (kf * e).astype(BF16)
        v_b = v_ref[...].astype(BF16)
        qs = (q_ref[...] * GLA_SCALE).astype(BF16)
        do_b = []
        for h in nh:
            vs = vals[h]
            o = o_ref[:, vs]
            gg = gg_ref[:, vs]
            g = g_ref[:, vs]
            dy = dy_ref[:, vs]
            r = lax.rsqrt(jnp.mean(o * o, axis=-1, keepdims=True) + EPS)
            sg = _sigmoid(gg)
            dogn = dy * (gg * sg)
            dgg_ref[:, vs] = (dy * (o * r * g) * (sg * (1.0 + gg * (1.0 - sg)))).astype(BF16)
            dg_ref[:, vs] += jnp.sum(dogn * o * r, axis=0, keepdims=True)
            w = dogn * g
            do_b.append((r * (w - o * (r * r) * jnp.mean(w * o, axis=-1, keepdims=True))).astype(BF16))
        dqs = [jnp.concatenate([_dot(do_b[h][rs], st_ref[c, h].astype(BF16)) for c, rs in enumerate(GLA_CHUNKS)],
                               axis=0) for h in nh]
        dq_ref[...] = (jnp.concatenate(dqs, axis=1) * GLA_SCALE).astype(BF16)
        own = [[_dot_tn(do_b[h][rs], qs[rs, keys[h]]) for h in nh] for rs in GLA_CHUNKS]
        gts, later = [None] * GLA_STEP, [carry[h] for h in nh]
        for c in reversed(nc):
            gts[c] = [own[c][h] + later[h] for h in nh]
            later = [gts[c][h] * a[c][:, keys[h]] for h in nh]
        for h in nh:
            carry[h] = later[h]
        gt_b = [[gts[c][h].astype(BF16) for h in nh] for c in nc]
        dkd = jnp.concatenate([jnp.concatenate([_dot(v_b[rs, vals[h]], gt_b[c][h]) for h in nh], axis=1)
                               for c, rs in enumerate(GLA_CHUNKS)], axis=0)
        dvs = [[_dot_nt(kd_b[rs, keys[h]], gt_b[c][h]) for h in nh] for c, rs in enumerate(GLA_CHUNKS)]
        before = lambda c, h: st_ref[c - 1, h] if c > 0 else stp_ref[0, h] * has_prev
        da = [jnp.concatenate([jnp.sum(gts[c][h] * before(c, h), axis=0, keepdims=True) for h in nh], axis=1)
              for c in nc]
        for h in nh:
            dv_ref[:, vals[h]] = jnp.concatenate([dvs[c][h] for c in nc], axis=0).astype(BF16)
        dk_ref[...] = (dkd * e).astype(BF16)
        dd = dkd * kf * e
        dsum = _per_chunk(lambda mine: jnp.sum(jnp.where(mine, dd, 0.0), axis=0, keepdims=True), dd)
        dcend = _spread([dsum[c] + da[c] * a[c] for c in nc], dd)
        dla = dcend - _dot01(tri_up, dd)
        dpre = dla * (1.0 / GLA_TAU) * (1.0 - _sigmoid(pre))
        dpre_b = dpre.astype(BF16)
        dga_ref[...] = _dot_nt(dpre_b, wa_b).astype(BF16)
        dwa_ref[...] += _dot_tn(ga_b, dpre_b)
        db_ref[...] += jnp.sum(dpre, axis=0, keepdims=True)

    full = lambda shape: pl.BlockSpec(shape, lambda n: tuple(0 for _ in shape))
    wide = pl.BlockSpec((GLA_ROWS, D_GLA), lambda n: (rev(n), 0))
    keyw = pl.BlockSpec((GLA_ROWS, GLA_KW), lambda n: (rev(n), 0))
    st_spec = pl.BlockSpec((GLA_STEP, GLA_HEADS, GLA_DV, GLA_DK), lambda n: (rev(n), 0, 0, 0))
    stp_spec = pl.BlockSpec((1, GLA_HEADS, GLA_DV, GLA_DK),
                            lambda n: (jnp.maximum(GLA_STEP * rev(n) - 1, 0), 0, 0, 0))
    return pl.pallas_call(
        body, name="gla_bwd", grid=(nsteps,),
        in_specs=[wide, wide] + _z_specs_gla(rev)
        + [full((LANE, GLA_KW)), full((1, GLA_KW)), full((1, D_GLA)), st_spec, stp_spec],
        out_specs=[keyw, keyw, wide, wide, pl.BlockSpec((GLA_ROWS, LANE), lambda n: (rev(n), 0)),
                   full((LANE, GLA_KW)), full((1, GLA_KW)), full((1, D_GLA))],
        out_shape=[jax.ShapeDtypeStruct((s, GLA_KW), BF16), jax.ShapeDtypeStruct((s, GLA_KW), BF16),
                   jax.ShapeDtypeStruct((s, D_GLA), BF16), jax.ShapeDtypeStruct((s, D_GLA), BF16),
                   jax.ShapeDtypeStruct((s, LANE), BF16),
                   jax.ShapeDtypeStruct((LANE, GLA_KW), F32), jax.ShapeDtypeStruct((1, GLA_KW), F32),
                   jax.ShapeDtypeStruct((1, D_GLA), F32)],
        scratch_shapes=[pltpu.VMEM((GLA_HEADS, GLA_DV, GLA_DK), F32)],
        compiler_params=_cparams(("arbitrary",)),
    )(dyc, o_gla, z, z, z, z, z, wa_pad, b_alpha, g_gla, states, states)


def _build_bias_table(rb_row, et_ref):
    far = jnp.broadcast_to(rb_row[:, 2 * REL_CLIP:2 * REL_CLIP + 1], (1, LANE))
    near_hi = rb_row[:, REL_CLIP:2 * REL_CLIP]
    near_lo = rb_row[:, 0:REL_CLIP]
    past = jnp.broadcast_to(rb_row[:, 0:1], (1, LANE))
    seg = [far, far, far, far, near_hi, near_lo] + [past] * (ET_ROWS // LANE - 5)
    ri = lax.broadcasted_iota(jnp.int32, (LANE, LANE), 0)
    ci = lax.broadcasted_iota(jnp.int32, (LANE, LANE), 1)
    for kb in range(ET_ROWS // LANE):
        wmat = jnp.where(ri + ci < LANE, seg[kb], seg[kb + 1])
        blk = pltpu.roll(wmat, 0, 1, stride=1, stride_axis=0)
        lag = LEFT_CHUNKS + ci // CHUNK - (2 * kb + ri // CHUNK)
        et_ref[kb * LANE:(kb + 1) * LANE, :] = jnp.where((lag >= 0) & (lag <= LEFT_CHUNKS), blk, NEG)


def _reduce_bias_table(det_ref):
    lane = lax.broadcasted_iota(jnp.int32, (1, LANE), 1)
    ri = lax.broadcasted_iota(jnp.int32, (LANE, LANE), 0)
    ci = lax.broadcasted_iota(jnp.int32, (LANE, LANE), 1)
    flip = jnp.where(ri + ci == LANE - 1, 1.0, 0.0).astype(BF16)
    segs = jnp.zeros((8, LANE), F32)
    seg_row = lax.broadcasted_iota(jnp.int32, (8, LANE), 0)
    prev_minus = jnp.zeros((1, LANE), F32)
    for kb in range(6):
        rolled = pltpu.roll(_dot01(det_ref[kb * LANE:(kb + 1) * LANE, :], flip, left=False), 0, 1,
                            stride=1, stride_axis=0)
        plus = jnp.sum(jnp.where(ci >= ri, rolled, 0.0), axis=0, keepdims=True)
        minus = jnp.sum(jnp.where(ci < ri, rolled, 0.0), axis=0, keepdims=True)
        segs = segs + jnp.where(seg_row == kb, plus + prev_minus, 0.0)
        prev_minus = minus
    segs = _dot01(segs, flip, left=False)
    pick = lambda kb: jnp.sum(jnp.where(seg_row == kb, segs, 0.0), axis=0, keepdims=True)
    far = jnp.sum(pick(0) + pick(1) + pick(2) + pick(3), axis=1, keepdims=True)
    last = jnp.where(lane == 0, far, 0.0)
    return jnp.concatenate([pick(5), pick(4), last], axis=1)


def _att_window(b):
    c0 = 2 * b
    kstart = pl.multiple_of(jnp.maximum(c0 - LEFT_CHUNKS, 0) * CHUNK, CHUNK)
    eoff = pl.multiple_of(jnp.maximum(LEFT_CHUNKS - c0, 0) * CHUNK, CHUNK)
    return kstart, eoff


def _att_probs(q_b, kw_b, et):
    st = _dot_nt(kw_b, q_b) * ATT_SCALE + et
    m = jnp.max(st, axis=0, keepdims=True)
    ex = jnp.exp(st - m)
    return ex * (1.0 / jnp.sum(ex, axis=0, keepdims=True))


def _att_fwd(z, rb_pad, g_att):
    s = z.shape[0]
    nblk = s // QB
    c_aq, c_ak, c_av, c_ag = [(OFF_AQ + i * D_ATT) // ATT_HD for i in range(4)]

    def body(q_ref, k_ref, v_ref, ag_ref, rb_ref, g_ref, y_ref, o_ref, p_ref, et_ref, kb_ref, vb_ref):
        h = pl.program_id(0)
        b = pl.program_id(1)

        @pl.when(b == 0)
        def _():
            _build_bias_table(rb_ref[pl.ds(h, 1), :], et_ref)
            kb_ref[...] = k_ref[...].astype(BF16)
            vb_ref[...] = v_ref[...].astype(BF16)

        for j in range(ATT_UNROLL):
            rs = slice(j * QB, (j + 1) * QB)
            kstart, eoff = _att_window(b * ATT_UNROLL + j)
            q_b = q_ref[rs, :].astype(BF16)
            kw_b = kb_ref[pl.ds(kstart, WIN), :]
            vw_b = vb_ref[pl.ds(kstart, WIN), :]
            pt = _att_probs(q_b, kw_b, et_ref[pl.ds(eoff, WIN), :])
            p_ref[0, j] = pt
            o = _dot_tn(pt.astype(BF16), vw_b)
            o_ref[rs, :] = o
            r = lax.rsqrt(jnp.mean(o * o, axis=-1, keepdims=True) + EPS)
            ag = ag_ref[rs, :]
            y_ref[rs, :] = (o * r * g_ref[...] * (ag * _sigmoid(ag))).astype(BF16)

    blk = lambda col: pl.BlockSpec((ATT_UNROLL * QB, ATT_HD), lambda h, b: (b, col + h))
    seq = lambda col: pl.BlockSpec((s, ATT_HD), lambda h, b: (0, col + h))
    out_blk = pl.BlockSpec((ATT_UNROLL * QB, ATT_HD), lambda h, b: (b, h))
    return pl.pallas_call(
        body, name="att_fwd", grid=(ATT_HEADS, nblk // ATT_UNROLL),
        in_specs=[blk(c_aq), seq(c_ak), seq(c_av), blk(c_ag),
                  pl.BlockSpec((ATT_HEADS, 3 * LANE), lambda h, b: (0, 0)),
                  pl.BlockSpec((1, ATT_HD), lambda h, b: (0, h))],
        out_specs=[out_blk, out_blk, pl.BlockSpec((1, ATT_UNROLL, WIN, QB), lambda h, b: (h, b, 0, 0))],
        out_shape=[jax.ShapeDtypeStruct((s, D_ATT), BF16), jax.ShapeDtypeStruct((s, D_ATT), F32),
                   jax.ShapeDtypeStruct((ATT_HEADS, nblk, WIN, QB), F32)],
        scratch_shapes=[pltpu.VMEM((ET_ROWS, LANE), F32), pltpu.VMEM((s, ATT_HD), BF16),
                        pltpu.VMEM((s, ATT_HD), BF16)],
        compiler_params=_cparams(("arbitrary", "arbitrary")),
    )(z, z, z, z, rb_pad, g_att)


def _att_bwd(dyc, o_att, probs, z, g_att):
    s = z.shape[0]
    nblk = s // QB
    c_aq, c_ak, c_av, c_ag = [(OFF_AQ + i * D_ATT) // ATT_HD for i in range(4)]
    c_dy = D_GLA // ATT_HD

    def body(dy_ref, o_ref, p_ref, q_ref, k_ref, v_ref, ag_ref, g_ref,
             dq_ref, dk_ref, dv_ref, dag_ref, drb_ref, dg_ref, det_ref, kb_ref, vb_ref, dk_acc, dv_acc):
        b = pl.program_id(1)

        @pl.when(b == 0)
        def _():
            kb_ref[...] = k_ref[...].astype(BF16)
            vb_ref[...] = v_ref[...].astype(BF16)
            det_ref[...] = jnp.zeros_like(det_ref)
            dk_acc[...] = jnp.zeros_like(dk_acc)
            dv_acc[...] = jnp.zeros_like(dv_acc)
            dg_ref[...] = jnp.zeros_like(dg_ref)

        g = g_ref[...]
        dg = jnp.zeros((1, ATT_HD), F32)
        for j in range(ATT_UNROLL):
            rs = slice(j * QB, (j + 1) * QB)
            kstart, eoff = _att_window(b * ATT_UNROLL + j)
            q_b = q_ref[rs, :].astype(BF16)
            kw_b = kb_ref[pl.ds(kstart, WIN), :]
            vw_b = vb_ref[pl.ds(kstart, WIN), :]
            pt = p_ref[0, j]
            o = o_ref[rs, :]
            ag = ag_ref[rs, :]
            dy = dy_ref[rs, :]
            r = lax.rsqrt(jnp.mean(o * o, axis=-1, keepdims=True) + EPS)
            sg = _sigmoid(ag)
            don = dy * (ag * sg)
            dag_ref[rs, :] = (dy * (o * r * g) * (sg * (1.0 + ag * (1.0 - sg)))).astype(BF16)
            dg = dg + jnp.sum(don * o * r, axis=0, keepdims=True)
            w = don * g
            do_b = (r * (w - o * (r * r) * jnp.mean(w * o, axis=-1, keepdims=True))).astype(BF16)
            pt_b = pt.astype(BF16)
            dpt = _dot_nt(vw_b, do_b)
            dst = pt * (dpt - jnp.sum(dpt * pt, axis=0, keepdims=True))
            det_ref[pl.ds(eoff, WIN), :] += dst
            ds_b = (dst * ATT_SCALE).astype(BF16)
            dq_ref[rs, :] = _dot_tn(ds_b, kw_b).astype(BF16)
            dk_acc[pl.ds(kstart, WIN), :] += _dot(ds_b, q_b)
            dv_acc[pl.ds(kstart, WIN), :] += _dot(pt_b, do_b)
        dg_ref[...] += dg

        @pl.when(b == nblk // ATT_UNROLL - 1)
        def _():
            drb_ref[0] = jnp.broadcast_to(_reduce_bias_table(det_ref), (8, 3 * LANE))
            dk_ref[...] = dk_acc[...].astype(BF16)
            dv_ref[...] = dv_acc[...].astype(BF16)

    blk = lambda col: pl.BlockSpec((ATT_UNROLL * QB, ATT_HD), lambda h, b: (b, col + h))
    seq = lambda col: pl.BlockSpec((s, ATT_HD), lambda h, b: (0, col + h))
    out_blk = pl.BlockSpec((ATT_UNROLL * QB, ATT_HD), lambda h, b: (b, h))
    out_seq = pl.BlockSpec((s, ATT_HD), lambda h, b: (0, h))
    return pl.pallas_call(
        body, name="att_bwd", grid=(ATT_HEADS, nblk // ATT_UNROLL),
        in_specs=[blk(c_dy), blk(0), pl.BlockSpec((1, ATT_UNROLL, WIN, QB), lambda h, b: (h, b, 0, 0)),
                  blk(c_aq), seq(c_ak), seq(c_av), blk(c_ag),
                  pl.BlockSpec((1, ATT_HD), lambda h, b: (0, h))],
        out_specs=[out_blk, out_seq, out_seq, out_blk,
                   pl.BlockSpec((1, 8, 3 * LANE), lambda h, b: (h, 0, 0)),
                   pl.BlockSpec((1, ATT_HD), lambda h, b: (0, h))],
        out_shape=[jax.ShapeDtypeStruct((s, D_ATT), BF16), jax.ShapeDtypeStruct((s, D_ATT), BF16),
                   jax.ShapeDtypeStruct((s, D_ATT), BF16), jax.ShapeDtypeStruct((s, D_ATT), BF16),
                   jax.ShapeDtypeStruct((ATT_HEADS, 8, 3 * LANE), F32),
                   jax.ShapeDtypeStruct((1, D_ATT), F32)],
        scratch_shapes=[pltpu.VMEM((ET_ROWS, LANE), F32),
                        pltpu.VMEM((s, ATT_HD), BF16), pltpu.VMEM((s, ATT_HD), BF16),
                        pltpu.VMEM((s, ATT_HD), F32), pltpu.VMEM((s, ATT_HD), F32)],
        compiler_params=_cparams(("arbitrary", "arbitrary")),
    )(dyc, o_att, probs, z, z, z, z, g_att)


ADAM_ROWS = 64
ADAM_COL_ROWS = 32


def _adam_math(w, g, m, v):
    m2 = ADAM_B1 * m + (1.0 - ADAM_B1) * g
    v2 = ADAM_B2 * v + (1.0 - ADAM_B2) * (g * g)
    m_hat = m2 / (1.0 - ADAM_B1 ** ADAM_STEP)
    v_hat = v2 / (1.0 - ADAM_B2 ** ADAM_STEP)
    delta = -ADAM_LR * (m_hat / (jnp.sqrt(v_hat) + ADAM_EPS) + ADAM_WD * w)
    return delta, m2, v2


def _adam_sharded(parts, first, w, m, v, name):
    nl, nr, nc = w.shape

    def body(*refs):
        p_refs = refs[:nl]
        w_ref, m_ref, v_ref, g_ref, d_ref, m2_ref, v2_ref = refs[nl:]
        for k in range(nl):
            @pl.when(pl.program_id(0) == k)
            def _(p_ref=p_refs[k]):
                g = p_ref[0].astype(F32)
                for dev in range(1, N_DEV):
                    g = g + p_ref[dev].astype(F32)
                delta, m2, v2 = _adam_math(w_ref[0], g, m_ref[0], v_ref[0])
                g_ref[0] = g
                d_ref[0] = delta
                m2_ref[0] = m2
                v2_ref[0] = v2

    def part_spec(k):
        return pl.BlockSpec((N_DEV, ADAM_ROWS, nc), lambda l, i: (0, first + jnp.where(l == k, i, 0), 0))

    blk = pl.BlockSpec((1, ADAM_ROWS, nc), lambda l, i: (l, i, 0))
    shp = jax.ShapeDtypeStruct(w.shape, F32)
    return pl.pallas_call(
        body, name=name, grid=(nl, pl.cdiv(nr, ADAM_ROWS)),
        in_specs=[part_spec(k) for k in range(nl)] + [blk, blk, blk],
        out_specs=[blk, blk, blk, blk],
        out_shape=[shp, shp, shp, shp],
        compiler_params=_cparams(("arbitrary", "arbitrary")),
    )(*parts, w, m, v)


def _adam_columns(parts, first, w, m, v):
    nc, nl, d = w.shape

    def body(*refs):
        p_refs = refs[:nl]
        w_ref, m_ref, v_ref, g_ref, d_ref, m2_ref, v2_ref = refs[nl:]
        for l in range(nl):
            g = p_refs[l][0].astype(F32)
            for dev in range(1, N_DEV):
                g = g + p_refs[l][dev].astype(F32)
            delta, m2, v2 = _adam_math(w_ref[:, l, :], g, m_ref[:, l, :], v_ref[:, l, :])
            g_ref[:, l, :] = g
            d_ref[:, l, :] = delta
            m2_ref[:, l, :] = m2
            v2_ref[:, l, :] = v2

    blk = pl.BlockSpec((ADAM_COL_ROWS, nl, d), lambda i: (i, 0, 0))
    part = pl.BlockSpec((N_DEV, ADAM_COL_ROWS, d), lambda i: (0, first + i, 0))
    shp = jax.ShapeDtypeStruct(w.shape, F32)
    return pl.pallas_call(
        body, name="adam_w_in", grid=(pl.cdiv(nc, ADAM_COL_ROWS),),
        in_specs=[part] * nl + [blk, blk, blk],
        out_specs=[blk, blk, blk, blk],
        out_shape=[shp, shp, shp, shp],
        compiler_params=_cparams(("parallel",)),
    )(*parts, w, m, v)


def _adam_small(ws, gs, ms, vs):
    n = len(ws)

    def body(*refs):
        w_refs, g_refs, m_refs, v_refs, d_refs, m2_refs, v2_refs = [refs[i * n:(i + 1) * n] for i in range(7)]
        for i in range(n):
            delta, m2, v2 = _adam_math(w_refs[i][...], g_refs[i][...], m_refs[i][...], v_refs[i][...])
            d_refs[i][...] = delta
            m2_refs[i][...] = m2
            v2_refs[i][...] = v2

    shapes = [jax.ShapeDtypeStruct(w.shape, F32) for w in ws]
    out = pl.pallas_call(body, name="adam_small", out_shape=shapes * 3)(*ws, *gs, *ms, *vs)
    return out[:n], out[n:2 * n], out[2 * n:]


def _position():
    return lax.axis_index("x"), lax.axis_index("y"), lax.axis_index("c")


def _slot(p):
    return 4 * p[0] + 2 * p[1] + p[2]


BF16_TILE_ROWS = 16


def _slab_rows(rows, cols):
    return -(-(rows + cols) // BF16_TILE_ROWS) * BF16_TILE_ROWS


RELAYOUT_COLS = 1024
RELAYOUT_CHUNK = 64


def _shard_pieces(dev, rows, cols):
    moved = ((0, GA_ORIG, 0), (GA_ORIG, GA_ORIG + GLA_RANK, OFF_GA - GA_ORIG), (GA_ORIG + GLA_RANK, D_IN, -GLA_RANK))
    c0, c1 = dev * cols, (dev + 1) * cols
    return [(rows + max(c0, lo) - c0, max(c0, lo) + off, min(c1, hi) - max(c0, lo))
            for lo, hi, off in moved if max(c0, lo) < min(c1, hi)]


def _move_rows(src, src_row, dst, dst_row, n):
    assert src_row % 2 == 0 and dst_row % 2 == 0 and n % 2 == 0
    for r in range(0, n // 2, RELAYOUT_CHUNK):
        m = min(RELAYOUT_CHUNK, n // 2 - r)
        dst[dst_row // 2 + r:dst_row // 2 + r + m, :] = src[src_row // 2 + r:src_row // 2 + r + m, :]


def _aligned_weight(land, rows, cols):
    _, slab, d = land.shape
    ct = min(RELAYOUT_COLS, d)

    def body(land_ref, wt_ref, wo_ref):
        dev = pl.program_id(1)
        src = land_ref.bitcast(jnp.uint32)
        dst = wt_ref.bitcast(jnp.uint32)
        wo_ref[...] = land_ref[0:rows, :]

        @pl.when(dev == 0)
        def _():
            dst[D_IN // 2:D_ZP // 2, :] = jnp.zeros(((D_ZP - D_IN) // 2, ct), jnp.uint32)

        for k in range(N_DEV):
            @pl.when(dev == k)
            def _(k=k):
                for at, to, n in _shard_pieces(k, rows, cols):
                    _move_rows(src, at, dst, to, n)

    return pl.pallas_call(
        body, name="aligned_weight", grid=(d // ct, N_DEV),
        in_specs=[pl.BlockSpec((slab, ct), lambda c, dev: (dev, c))],
        out_specs=[pl.BlockSpec((D_ZP, ct), lambda c, dev: (0, c)),
                   pl.BlockSpec((rows, ct), lambda c, dev: (dev, c))],
        out_shape=[jax.ShapeDtypeStruct((D_ZP, d), land.dtype),
                   jax.ShapeDtypeStruct((N_DEV * rows, d), land.dtype)],
        compiler_params=_cparams(("parallel", "arbitrary")),
    )(land.reshape(N_DEV * slab, d))


def _partial_slabs(dwt, cols):
    d = dwt[0].shape[1]
    bounds = (0, GA_ORIG, OFF_GA, D_ZP)
    assert tuple(a.shape[0] for a in dwt) == tuple(hi - lo for lo, hi in zip(bounds, bounds[1:]))
    slab = _slab_rows(0, cols)
    ct = min(RELAYOUT_COLS, d)

    def body(*refs):
        out_ref = refs[-1]
        dev = pl.program_id(1)
        srcs = [ref.bitcast(jnp.uint32) for ref in refs[:-1]]
        dst = out_ref.bitcast(jnp.uint32)
        dst[cols // 2:slab // 2, :] = jnp.zeros(((slab - cols) // 2, ct), jnp.uint32)
        for k in range(N_DEV):
            @pl.when(dev == k)
            def _(k=k):
                for to, at, n in _shard_pieces(k, 0, cols):
                    which = max(i for i, lo in enumerate(bounds[:-1]) if lo <= at)
                    assert at + n <= bounds[which + 1]
                    _move_rows(srcs[which], at - bounds[which], dst, to, n)

    return pl.pallas_call(
        body, name="partial_slabs", grid=(d // ct, N_DEV),
        in_specs=[pl.BlockSpec((a.shape[0], ct), lambda c, dev: (0, c)) for a in dwt],
        out_specs=pl.BlockSpec((slab, ct), lambda c, dev: (dev, c)),
        out_shape=jax.ShapeDtypeStruct((N_DEV * slab, d), dwt[0].dtype),
        compiler_params=_cparams(("parallel", "arbitrary")),
    )(*dwt).reshape(N_DEV, slab, d)


def _peer(pos, k):
    x, y, c = pos
    return (1 - x if k & 4 else x, 1 - y if k & 2 else y, 1 - c if k & 1 else c)


HBM_SPEC = pl.BlockSpec(memory_space=pltpu.HBM)
SEM_SPEC = pl.BlockSpec(memory_space=pltpu.SEMAPHORE)
GATHER_PEERS = (1, 4, 2, 6)
ALL_PEERS = (1, 2, 3, 4, 5, 6, 7)


def _hbm(a):
    return pltpu.with_memory_space_constraint(a, pltpu.HBM)


def _split_copies(src_ref, land_ref, send_sems, recv_sems, ks, per_peer, landed):
    me = _position()
    out = []
    for i, k in enumerate(ks):
        peer = _peer(me, k)
        src = src_ref.at[_slot(peer)] if per_peer else src_ref
        dst = land_ref.at[_slot(peer) if landed else _slot(me)]
        out.append(pltpu.make_async_remote_copy(
            src_ref=src, dst_ref=dst, send_sem=send_sems.at[i], recv_sem=recv_sems.at[i],
            device_id=peer, device_id_type=MESH))
    return out


def _exchange_start(src, after, ks, per_peer, name):
    slab = src.shape[1:] if per_peer else src.shape
    land_shape = (N_DEV,) + tuple(slab)
    n = len(ks)

    def body(src_ref, land_ref, after_ref, send_sems, recv_sems, src_thru, land_thru, token):
        for cp in _split_copies(src_ref, land_ref, send_sems, recv_sems, ks, per_peer, landed=False):
            cp.start()
        token[...] = jnp.zeros_like(token)

    return pl.pallas_call(
        body, name=name,
        out_shape=(pltpu.SemaphoreType.DMA((n,)), pltpu.SemaphoreType.DMA((n,)),
                   pltpu.HBM(src.shape, src.dtype), pltpu.HBM(land_shape, src.dtype),
                   jax.ShapeDtypeStruct((8, LANE), F32)),
        in_specs=(HBM_SPEC, HBM_SPEC, ANY),
        out_specs=(SEM_SPEC, SEM_SPEC, HBM_SPEC, HBM_SPEC, pl.BlockSpec(memory_space=pltpu.VMEM)),
        input_output_aliases={0: 2, 1: 3},
        compiler_params=pltpu.CompilerParams(has_side_effects=pltpu.SideEffectType.DATAFLOW_SIDE_EFFECTING),
    )(_hbm(src), _hbm(lax.empty(land_shape, src.dtype)), after)


def _exchange_wait(started, after, ks, per_peer, name):
    send_sems, recv_sems, src_thru, land_thru = started

    def body(src_ref, land_ref, send_sems, recv_sems, after_ref, src_dead, land_out):
        for cp in _split_copies(src_ref, land_ref, send_sems, recv_sems, ks, per_peer, landed=True):
            cp.wait_send()
            cp.wait_recv()

    return pl.pallas_call(
        body, name=name,
        out_shape=(pltpu.HBM(src_thru.shape, src_thru.dtype), pltpu.HBM(land_thru.shape, land_thru.dtype)),
        in_specs=(HBM_SPEC, HBM_SPEC, SEM_SPEC, SEM_SPEC, ANY), out_specs=(HBM_SPEC, HBM_SPEC),
        input_output_aliases={0: 0, 1: 1},
        compiler_params=pltpu.CompilerParams(has_side_effects=pltpu.SideEffectType.DATAFLOW_SIDE_EFFECTING),
    )(src_thru, land_thru, send_sems, recv_sems, after)


def _relay_copies(land_ref, send_sems, recv_sems, landed):
    me = _position()
    sibling = _peer(me, 1)
    out = []
    for i, k in enumerate(GATHER_PEERS[1:]):
        blk = land_ref.at[_slot(_peer(sibling if landed else me, k))]
        out.append(pltpu.make_async_remote_copy(
            src_ref=blk, dst_ref=blk, send_sem=send_sems.at[i], recv_sem=recv_sems.at[i],
            device_id=sibling, device_id_type=MESH))
    return out


def _relay_start(land, name):
    n = len(GATHER_PEERS) - 1

    def body(land_ref, send_sems, recv_sems, land_thru, token):
        for cp in _relay_copies(land_ref, send_sems, recv_sems, landed=False):
            cp.start()
        token[...] = jnp.zeros_like(token)

    return pl.pallas_call(
        body, name=name,
        out_shape=(pltpu.SemaphoreType.DMA((n,)), pltpu.SemaphoreType.DMA((n,)),
                   pltpu.HBM(land.shape, land.dtype), jax.ShapeDtypeStruct((8, LANE), F32)),
        in_specs=(HBM_SPEC,),
        out_specs=(SEM_SPEC, SEM_SPEC, HBM_SPEC, pl.BlockSpec(memory_space=pltpu.VMEM)),
        input_output_aliases={0: 2},
        compiler_params=pltpu.CompilerParams(has_side_effects=pltpu.SideEffectType.DATAFLOW_SIDE_EFFECTING),
    )(_hbm(land))


def _relay_wait(started, after, name):
    send_sems, recv_sems, land_thru = started

    def body(land_ref, send_sems, recv_sems, after_ref, land_out):
        for cp in _relay_copies(land_ref, send_sems, recv_sems, landed=True):
            cp.wait_send()
            cp.wait_recv()

    return pl.pallas_call(
        body, name=name,
        out_shape=pltpu.HBM(land_thru.shape, land_thru.dtype),
        in_specs=(HBM_SPEC, SEM_SPEC, SEM_SPEC, ANY), out_specs=HBM_SPEC,
        input_output_aliases={0: 0},
        compiler_params=pltpu.CompilerParams(has_side_effects=pltpu.SideEffectType.DATAFLOW_SIDE_EFFECTING),
    )(land_thru, send_sems, recv_sems, after)


def _share(vec, name, after=None):
    follows = [] if after is None else [after]

    def body(vec_ref, *rest):
        out_ref, send_sems, recv_sems, local_sem = rest[len(follows):]
        me = _position()

        def copy(k, landed):
            peer = _peer(me, k)
            return pltpu.make_async_remote_copy(
                src_ref=vec_ref, dst_ref=out_ref.at[_slot(peer) if landed else _slot(me)],
                send_sem=send_sems.at[k - 1], recv_sem=recv_sems.at[k - 1], device_id=peer, device_id_type=MESH)

        mine = pltpu.make_async_copy(vec_ref, out_ref.at[_slot(me)], local_sem)
        mine.start()
        sent = [copy(k, False) for k in ALL_PEERS]
        for cp in sent:
            cp.start()
        for k in ALL_PEERS:
            copy(k, True).wait_recv()
        for cp in sent:
            cp.wait_send()
        mine.wait()

    return pl.pallas_call(
        body, name=name,
        in_specs=[ANY] * (1 + len(follows)), out_specs=ANY,
        out_shape=jax.ShapeDtypeStruct((N_DEV,) + vec.shape, vec.dtype),
        scratch_shapes=[pltpu.SemaphoreType.DMA((N_DEV - 1,)), pltpu.SemaphoreType.DMA((N_DEV - 1,)),
                        pltpu.SemaphoreType.DMA],
    )(vec, *follows)


def _sum_slots(parts):
    def body(p_ref, o_ref):
        acc = p_ref[0]
        for dev in range(1, N_DEV):
            acc = acc + p_ref[dev]
        o_ref[...] = acc

    return pl.pallas_call(body, name="sum_slots",
                          out_shape=jax.ShapeDtypeStruct(parts.shape[1:], F32))(parts)


PACK_ROWS = 8


def _packed_rows(size):
    return -(-size // (PACK_ROWS * LANE)) * PACK_ROWS


def _pack(arrs):
    def rows(a):
        flat = a.reshape(-1)
        return jnp.pad(flat, (0, _packed_rows(flat.shape[0]) * LANE - flat.shape[0])).reshape(-1, LANE)

    return jnp.concatenate([rows(a) for a in arrs], axis=0)


def _unpack(packed, shapes):
    out, at = [], 0
    for shp in shapes:
        size = 1
        for dim in shp:
            size *= dim
        nrows = _packed_rows(size)
        out.append(packed[at:at + nrows].reshape(-1)[:size].reshape(shp))
        at += nrows
    return out


def _layer_fwd(x, wt, wo, g_pre, g_post, wa_pad, b_alpha, g_gla, g_att, rb_pad, midway=None):
    h = _rms_fwd(x, g_pre)
    z = _matmul(h, wt, "nt", F32, *TILES["in_proj"], "in_proj", n_outer=True)
    y_gla, o_gla, states = _gla_fwd(z, wa_pad, b_alpha, g_gla)
    if midway is not None:
        g_att = g_att + midway(y_gla)[:1, :1]
    y_att, o_att, probs = _att_fwd(z, rb_pad, g_att)
    y = _matmul_cols([y_gla, y_att], wo, F32, *TILES["out_proj"][:2], "out_proj")
    out = _post_fwd(x, y, g_post)
    return out, (x, h, z, o_gla, states, o_att, probs, y_gla, y_att, y)


def _layer_bwd(dout, saved, wt, wo, g_pre, g_post, wa_pad, b_alpha, g_gla, g_att, rb_pad, on_dwo, on_dwt):
    x, h, z, o_gla, states, o_att, probs, y_gla, y_att, y = saved
    dy, dg_post = _post_bwd(dout, y, g_post)
    dwo = _matmul_rows([y_gla, y_att], dy, BF16, *TILES["out_proj_dw"][:2], "out_proj_dw")
    token = on_dwo(dwo)
    dycat = _matmul(dy, wo, "nt", F32, *TILES["out_proj_dx"], "out_proj_dx", n_outer=True, after=token)
    dq, dk, dv, dgg, dga, dwa, db, dg_gla = _gla_bwd(dycat, o_gla, z, wa_pad, b_alpha, g_gla, states)
    daq, dak, dav, dag, drb, dg_att = _att_bwd(dycat, o_att, probs, z, g_att)
    tw, tn = TILES["in_proj_dw"][:2]
    dwt = (_matmul_rows([dq, dk, dv, dgg], h, BF16, tw, tn, "in_proj_dw_gla"),
           _matmul_rows([daq, dak, dav, dag], h, BF16, tw, tn, "in_proj_dw_att"),
           _matmul_rows([dga], h, BF16, LANE, tn, "in_proj_dw_gate"))
    token = on_dwt(dwt)
    dh = _matmul_cols([dq, dk, dv, dgg, daq, dak, dav, dag, dga], wt, F32, *TILES["in_proj_dx"][:2],
                      "in_proj_dx", after=token)
    dx, dg_pre = _pre_bwd(dh, x, g_pre, dout)
    small = (dg_pre[0], dg_post[0], dwa[:GLA_RANK], db[0], dg_gla[0], dg_att[0], drb[:, 0, :N_REL])
    return dx, small


def kernel(x, w_in, w_out, g_pre, g_post, w_alpha, b_alpha, g_gla, g_att, rel_bias, loss_target, m_w_in, m_w_out, m_g_pre, m_g_post, m_w_alpha, m_b_alpha, m_g_gla, m_g_att, m_rel_bias, v_w_in, v_w_out, v_g_pre, v_g_post, v_w_alpha, v_b_alpha, v_g_gla, v_g_att, v_rel_bias):
    nl, d, cols = w_in.shape
    rows = w_out.shape[1]
    s = x.shape[1]
    x0 = x.reshape(s, d)
    tgt = loss_target.reshape(s, d)

    cols_first = lambda a: jnp.transpose(a, (2, 0, 1))
    w_c = cols_first(w_in)
    slab = _slab_rows(rows, cols)
    is_out = lax.broadcasted_iota(jnp.int32, (slab, d), 0) < rows

    def shard(l, zero=0.0):
        top = jnp.pad((w_out[l] + zero).astype(BF16), ((0, slab - rows), (0, 0)))
        rest = jnp.pad((w_c[:, l] + zero).astype(BF16), ((rows, slab - rows - cols), (0, 0)))
        return jnp.where(is_out, top, rest)

    first_fetch = _exchange_start(shard(0), x, GATHER_PEERS, False, "gather_start_0")
    began = first_fetch[4][0, 0]
    shards = [None] + [shard(l, began) for l in range(1, nl)]
    alpha = _pack([w_alpha]) + began
    wa_g = _share(alpha, "gather_alpha")
    wa_cols = w_alpha.shape[2]
    wa_full = wa_g.reshape(N_DEV, -1)[:, :nl * GLA_RANK * wa_cols].reshape(N_DEV, nl, GLA_RANK, wa_cols)
    wa_full = jnp.transpose(wa_full, (1, 2, 0, 3)).reshape(nl, GLA_RANK, GLA_KW)
    wa_pad = jnp.pad(wa_full, ((0, 0), (0, LANE - GLA_RANK), (0, 0)))
    rb_pad = jnp.pad(rel_bias, ((0, 0), (0, 0), (0, 3 * LANE - N_REL)))

    def layer_args(l, follows=None):
        gp = g_pre[l:l + 1] if follows is None else g_pre[l:l + 1] + follows[:1, :1]
        return (wts[l], wos[l], gp, g_post[l:l + 1], wa_pad[l], b_alpha[l:l + 1], g_gla[l:l + 1],
                g_att[l:l + 1], rb_pad[l])

    my = _slot(_position())

    def fetch(l, after):
        return _exchange_start(shards[l], after, GATHER_PEERS, False, f"gather_start_{l}")

    def relay(l, first_hop, after):
        own[l], land = _exchange_wait(first_hop[:4], after, GATHER_PEERS, False, f"gather_wait_{l}")
        return _relay_start(land, f"relay_start_{l}")

    def midway(l, y):
        flight["relay"] = relay(l + 1, flight["fetch"], y)
        if l + 2 >= nl:
            return flight["relay"][3]
        flight["fetch"] = fetch(l + 2, flight["relay"][2])
        return flight["fetch"][4]

    act, saved, wts, wos, flight, own = x0, [], [], [], {}, [None] * nl
    prepared = (wa_pad[0, :1, :1] + sum(sh[:1, :1].astype(F32) for sh in shards[1:]))
    flight["relay"] = relay(0, first_fetch, prepared)
    if nl > 1:
        flight["fetch"] = fetch(1, flight["relay"][2])
    for l in range(nl):
        land = _relay_wait(flight["relay"][:3], act, f"relay_wait_{l}")
        land = lax.dynamic_update_slice_in_dim(land, own[l][None], my, 0)
        wt_l, wo_l = _aligned_weight(land, rows, cols)
        wts.append(wt_l)
        wos.append(wo_l)
        act, sv = _layer_fwd(act, *layer_args(l, follows=first_fetch[4] if l == 0 else None),
                             midway=functools.partial(midway, l) if l + 1 < nl else None)
        saved.append(sv)
    dout, sq = _loss_head(act, tgt)
    loss = lax.psum(sq[0, 0] * (0.5 / d), ("x", "y", "c"))

    smalls, pending_out, pending_in = [None] * nl, [None] * nl, [None] * nl

    def send_out(l, dwo):
        pending_out[l] = _exchange_start(dwo.reshape(N_DEV, rows, d), dwo[:1, :1], ALL_PEERS, True,
                                         f"scatter_out_start_{l}")
        return pending_out[l][4]

    def send_in(l, dwt):
        pending_in[l] = _exchange_start(_partial_slabs(dwt, cols), dwt[-1], ALL_PEERS, True,
                                        f"scatter_in_start_{l}")
        return pending_in[l][4]

    for l in reversed(range(nl)):
        dout, smalls[l] = _layer_bwd(dout, saved[l], *layer_args(l), on_dwo=functools.partial(send_out, l),
                                     on_dwt=functools.partial(send_in, l))
    grad_x = dout.reshape(x.shape)

    def landed(started, after, name):
        partial, land = _exchange_wait(started[:4], after, ALL_PEERS, True, name)
        return lax.dynamic_update_slice_in_dim(land, lax.dynamic_slice_in_dim(partial, my, 1, 0), my, 0)

    parts_out = [landed(pending_out[l], dout, f"scatter_out_wait_{l}") for l in range(nl)]
    g_w_out, d_w_out, m2_w_out, v2_w_out = _adam_sharded(parts_out, 0, w_out, m_w_out, v_w_out, "adam_w_out")
    names = 7
    small_stacked = [jnp.stack([smalls[l][i] for l in range(nl)]) for i in range(names)]
    shapes = [a.shape for a in small_stacked]
    gathered = _share(_pack(small_stacked), "gather_small_grads", after=d_w_out)
    g_pre_g, g_post_g, wa_g_full, b_g, gla_g, att_g, rb_g = _unpack(_sum_slots(gathered), shapes)
    wa_g_mine = lax.dynamic_slice_in_dim(wa_g_full, my * wa_cols, wa_cols, axis=2)
    grads = [g_pre_g, g_post_g, wa_g_mine, b_g, gla_g, att_g, rb_g]
    ws = [g_pre, g_post, w_alpha, b_alpha, g_gla, g_att, rel_bias]
    ms = [m_g_pre, m_g_post, m_w_alpha, m_b_alpha, m_g_gla, m_g_att, m_rel_bias]
    vs = [v_g_pre, v_g_post, v_w_alpha, v_b_alpha, v_g_gla, v_g_att, v_rel_bias]
    d_s, m2_s, v2_s = _adam_small(ws, grads, ms, vs)

    parts_in = [landed(pending_in[l], d_s[0], f"scatter_in_wait_{l}") for l in range(nl)]
    g_w_in, d_w_in, m2_w_in, v2_w_in = [
        jnp.transpose(a, (1, 2, 0))
        for a in _adam_columns(parts_in, 0, w_c, cols_first(m_w_in), cols_first(v_w_in))]

    def ordered(big_in, big_out, small):
        return [big_in, big_out] + list(small)

    return (loss, grad_x,
            *ordered(g_w_in, g_w_out, grads),
            *ordered(d_w_in, d_w_out, d_s),
            *ordered(m2_w_in, m2_w_out, m2_s),
            *ordered(v2_w_in, v2_w_out, v2_s))
```

```python
import functools

import jax
import jax.numpy as jnp
from jax import lax
from jax.experimental import pallas as pl
from jax.experimental.pallas import tpu as pltpu

F32 = jnp.float32
BF16 = jnp.bfloat16
MESH = pl.DeviceIdType.MESH
ANY = pl.BlockSpec(memory_space=pl.ANY)

CHUNK = 64
GLA_HEADS = 4
GLA_DK = 128
GLA_DV = 256
GLA_KW = GLA_HEADS * GLA_DK
D_GLA = GLA_HEADS * GLA_DV
GLA_RANK = 16
GLA_TAU = 16.0
ATT_HEADS = 8
ATT_HD = 128
D_ATT = ATT_HEADS * ATT_HD
LEFT_CHUNKS = 8
REL_CLIP = 128
N_REL = 2 * REL_CLIP + 1
EPS = 1e-6
D_IN = 2 * GLA_KW + 2 * D_GLA + GLA_RANK + 4 * D_ATT
GLA_SCALE = GLA_DK ** -0.5
ATT_SCALE = ATT_HD ** -0.5

ADAM_LR = 0.001
ADAM_B1 = 0.9
ADAM_B2 = 0.999
ADAM_EPS = 1e-08
ADAM_WD = 0.01
ADAM_STEP = 10

N_DEV = 8
LANE = 128
GA_ORIG = 2 * GLA_KW + 2 * D_GLA
OFF_AQ = GA_ORIG
OFF_GA = GA_ORIG + 4 * D_ATT
D_ZP = OFF_GA + LANE
QB = 2 * CHUNK
ATT_UNROLL = 8
WIN = (LEFT_CHUNKS + 2) * CHUNK
ET_ROWS = WIN + LEFT_CHUNKS * CHUNK
NEG = -1e30
VMEM_LIMIT = 48 * 1024 * 1024


def _cparams(sem):
    return pltpu.CompilerParams(dimension_semantics=sem, vmem_limit_bytes=VMEM_LIMIT)


def _dot(a, b):
    return jnp.dot(a, b, preferred_element_type=F32)


def _dot_nt(a, b):
    return lax.dot_general(a, b, (((1,), (1,)), ((), ())), preferred_element_type=F32)


def _dot_tn(a, b):
    return lax.dot_general(a, b, (((0,), (0,)), ((), ())), preferred_element_type=F32)


def _dot01(t, x, left=True):
    if not left:
        t, x = x, t
    hi = x.astype(BF16)
    r = x - hi.astype(F32)
    mid = r.astype(BF16)
    lo = (r - mid.astype(F32)).astype(BF16)
    if left:
        return _dot(t, hi) + _dot(t, mid) + _dot(t, lo)
    return _dot(hi, t) + _dot(mid, t) + _dot(lo, t)


def _sigmoid(x):
    return 1.0 / (1.0 + jnp.exp(-x))


def _log_sigmoid(x):
    return jnp.minimum(x, 0.0) - jnp.log(1.0 + jnp.exp(-jnp.abs(x)))


TILES = {
    "in_proj": (512, D_ZP // 3, None),
    "in_proj_dx": (512, 512, None),
    "in_proj_dw": (512, 2048, None),
    "out_proj": (512, 1024, None),
    "out_proj_dx": (512, 1024, None),
    "out_proj_dw": (1024, 1024, None),
}


def _matmul(a, b, mode, out_dtype, tm, tn, tk, name, n_outer=False, after=None):
    if mode == "nn":
        (m, k), n = a.shape, b.shape[1]
    elif mode == "nt":
        (m, k), n = a.shape, b.shape[0]
    else:
        (k, m), n = a.shape, b.shape[1]
    tm, tn, tk = min(tm, m), min(tn, n), k if tk is None else min(tk, k)
    assert m % tm == 0 and n % tn == 0 and k % tk == 0, (name, m, n, k)
    nk = k // tk
    dot = {"nn": _dot, "nt": _dot_nt, "tn": _dot_tn}[mode]

    follows = [] if after is None else [after]

    def body_whole_k(a_ref, b_ref, *rest):
        o_ref = rest[-1]
        o_ref[...] = dot(a_ref[...], b_ref[...]).astype(out_dtype)

    def body(a_ref, b_ref, *rest):
        o_ref, acc_ref = rest[-2:]
        kk = pl.program_id(2)

        @pl.when(kk == 0)
        def _():
            acc_ref[...] = jnp.zeros_like(acc_ref)

        acc_ref[...] += dot(a_ref[...], b_ref[...])

        @pl.when(kk == nk - 1)
        def _():
            o_ref[...] = acc_ref[...].astype(out_dtype)

    def at(index):
        return (lambda j, i, kk: index(i, j, kk)) if n_outer else index

    if mode == "tn":
        a_spec = pl.BlockSpec((tk, tm), at(lambda i, j, kk: (kk, i)))
    else:
        a_spec = pl.BlockSpec((tm, tk), at(lambda i, j, kk: (i, kk)))
    if mode == "nt":
        b_spec = pl.BlockSpec((tn, tk), at(lambda i, j, kk: (j, kk)))
    else:
        b_spec = pl.BlockSpec((tk, tn), at(lambda i, j, kk: (kk, j)))
    return pl.pallas_call(
        body_whole_k if nk == 1 else body, name=name,
        grid=(n // tn, m // tm, nk) if n_outer else (m // tm, n // tn, nk),
        in_specs=[a_spec, b_spec] + [ANY] * len(follows),
        out_specs=pl.BlockSpec((tm, tn), at(lambda i, j, kk: (i, j))),
        out_shape=jax.ShapeDtypeStruct((m, n), out_dtype),
        scratch_shapes=[] if nk == 1 else [pltpu.VMEM((tm, tn), F32)],
        compiler_params=_cparams(("parallel", "parallel", "arbitrary")),
    )(a, b, *follows)


def _matmul_cols(pieces, b, out_dtype, tm, tn, name, after=None):
    m, n = pieces[0].shape[0], b.shape[1]
    widths = [p.shape[1] for p in pieces]
    starts = [sum(widths[:i]) for i in range(len(pieces))]
    follows = [] if after is None else [after]
    tm, tn = min(tm, m), min(tn, n)
    assert sum(widths) == b.shape[0] and m % tm == 0 and n % tn == 0, name

    def body(*refs):
        b_ref, o_ref = refs[len(pieces)], refs[-1]
        acc = None
        for p_ref, at, width in zip(refs, starts, widths):
            part = _dot(p_ref[...], b_ref[at:at + width, :])
            acc = part if acc is None else acc + part
        o_ref[...] = acc.astype(out_dtype)

    return pl.pallas_call(
        body, name=name, grid=(n // tn, m // tm),
        in_specs=[pl.BlockSpec((tm, width), lambda j, i: (i, 0)) for width in widths]
        + [pl.BlockSpec((b.shape[0], tn), lambda j, i: (0, j))] + [ANY] * len(follows),
        out_specs=pl.BlockSpec((tm, tn), lambda j, i: (i, j)),
        out_shape=jax.ShapeDtypeStruct((m, n), out_dtype),
        compiler_params=_cparams(("parallel", "parallel")),
    )(*pieces, b, *follows)


def _matmul_rows(pieces, b, out_dtype, tw, tn, name):
    k, n = b.shape
    tn = min(tn, n)
    counts = [p.shape[1] // tw for p in pieces]
    firsts = [sum(counts[:i]) for i in range(len(pieces))]
    assert all(p.shape[1] % tw == 0 for p in pieces) and n % tn == 0, name

    def body(*refs):
        b_ref, o_ref = refs[len(pieces):]
        for p_ref, first, count in zip(refs, firsts, counts):
            @pl.when((pl.program_id(0) >= first) & (pl.program_id(0) < first + count))
            def _(p_ref=p_ref):
                o_ref[...] = _dot_tn(p_ref[...], b_ref[...]).astype(out_dtype)

    def piece_spec(first, count):
        return pl.BlockSpec((k, tw), lambda i, j: (0, jnp.clip(i - first, 0, count - 1)))

    return pl.pallas_call(
        body, name=name, grid=(sum(counts), n // tn),
        in_specs=[piece_spec(first, count) for first, count in zip(firsts, counts)]
        + [pl.BlockSpec((k, tn), lambda i, j: (0, j))],
        out_specs=pl.BlockSpec((tw, tn), lambda i, j: (i, j)),
        out_shape=jax.ShapeDtypeStruct((sum(counts) * tw, n), out_dtype),
        compiler_params=_cparams(("parallel", "parallel")),
    )(*pieces, b)


ROWS = 512


def _rms_fwd(x, g):
    s, d = x.shape

    def body(x_ref, g_ref, h_ref):
        xv = x_ref[...]
        r = lax.rsqrt(jnp.mean(xv * xv, axis=-1, keepdims=True) + EPS)
        h_ref[...] = (xv * r * g_ref[...]).astype(BF16)

    return pl.pallas_call(
        body, name="rms_fwd", grid=(s // ROWS,),
        in_specs=[pl.BlockSpec((ROWS, d), lambda i: (i, 0)), pl.BlockSpec((1, d), lambda i: (0, 0))],
        out_specs=pl.BlockSpec((ROWS, d), lambda i: (i, 0)),
        out_shape=jax.ShapeDtypeStruct((s, d), BF16),
        compiler_params=_cparams(("parallel",)),
    )(x, g)


def _post_fwd(x, y, g):
    s, d = x.shape

    def body(x_ref, y_ref, g_ref, o_ref):
        yv = y_ref[...]
        r = lax.rsqrt(jnp.mean(yv * yv, axis=-1, keepdims=True) + EPS)
        o_ref[...] = x_ref[...] + yv * r * g_ref[...]

    row = pl.BlockSpec((ROWS, d), lambda i: (i, 0))
    return pl.pallas_call(
        body, name="post_fwd", grid=(s // ROWS,),
        in_specs=[row, row, pl.BlockSpec((1, d), lambda i: (0, 0))],
        out_specs=row,
        out_shape=jax.ShapeDtypeStruct((s, d), F32),
        compiler_params=_cparams(("parallel",)),
    )(x, y, g)


def _loss_head(out, tgt):
    s, d = out.shape

    def body(o_ref, t_ref, dout_ref, sum_ref):
        @pl.when(pl.program_id(0) == 0)
        def _():
            sum_ref[...] = jnp.zeros_like(sum_ref)

        e = o_ref[...] - t_ref[...]
        dout_ref[...] = e * (1.0 / d)
        sum_ref[...] += jnp.sum(jnp.sum(e * e, axis=1, keepdims=True), axis=0, keepdims=True)

    row = pl.BlockSpec((ROWS, d), lambda i: (i, 0))
    return pl.pallas_call(
        body, name="loss_head", grid=(s // ROWS,),
        in_specs=[row, row],
        out_specs=[row, pl.BlockSpec((1, 1), lambda i: (0, 0))],
        out_shape=[jax.ShapeDtypeStruct((s, d), F32), jax.ShapeDtypeStruct((1, 1), F32)],
        compiler_params=_cparams(("arbitrary",)),
    )(out, tgt)


def _post_bwd(dout, y, g):
    s, d = y.shape

    def body(do_ref, y_ref, g_ref, dy_ref, dg_ref):
        @pl.when(pl.program_id(0) == 0)
        def _():
            dg_ref[...] = jnp.zeros_like(dg_ref)

        yv = y_ref[...]
        dv = do_ref[...]
        r = lax.rsqrt(jnp.mean(yv * yv, axis=-1, keepdims=True) + EPS)
        dg_ref[...] += jnp.sum(dv * yv * r, axis=0, keepdims=True)
        w = dv * g_ref[...]
        dy = r * (w - yv * (r * r) * jnp.mean(w * yv, axis=-1, keepdims=True))
        dy_ref[...] = dy.astype(BF16)

    row = pl.BlockSpec((ROWS, d), lambda i: (i, 0))
    vec = pl.BlockSpec((1, d), lambda i: (0, 0))
    return pl.pallas_call(
        body, name="post_bwd", grid=(s // ROWS,),
        in_specs=[row, row, vec],
        out_specs=[row, vec],
        out_shape=[jax.ShapeDtypeStruct((s, d), BF16), jax.ShapeDtypeStruct((1, d), F32)],
        compiler_params=_cparams(("arbitrary",)),
    )(dout, y, g)


def _pre_bwd(dh, x, g, dout):
    s, d = x.shape

    def body(dh_ref, x_ref, g_ref, do_ref, dx_ref, dg_ref):
        @pl.when(pl.program_id(0) == 0)
        def _():
            dg_ref[...] = jnp.zeros_like(dg_ref)

        xv = x_ref[...]
        dv = dh_ref[...]
        r = lax.rsqrt(jnp.mean(xv * xv, axis=-1, keepdims=True) + EPS)
        dg_ref[...] += jnp.sum(dv * xv * r, axis=0, keepdims=True)
        w = dv * g_ref[...]
        dx_ref[...] = do_ref[...] + r * (w - xv * (r * r) * jnp.mean(w * xv, axis=-1, keepdims=True))

    row = pl.BlockSpec((ROWS, d), lambda i: (i, 0))
    vec = pl.BlockSpec((1, d), lambda i: (0, 0))
    return pl.pallas_call(
        body, name="pre_bwd", grid=(s // ROWS,),
        in_specs=[row, row, vec, row],
        out_specs=[row, vec],
        out_shape=[jax.ShapeDtypeStruct((s, d), F32), jax.ShapeDtypeStruct((1, d), F32)],
        compiler_params=_cparams(("arbitrary",)),
    )(dh, x, g, dout)


GLA_STEP = 4
GLA_ROWS = GLA_STEP * CHUNK
GLA_CHUNKS = [slice(c * CHUNK, (c + 1) * CHUNK) for c in range(GLA_STEP)]


def _chunk_triangles():
    ri = lax.broadcasted_iota(jnp.int32, (GLA_ROWS, GLA_ROWS), 0)
    ci = lax.broadcasted_iota(jnp.int32, (GLA_ROWS, GLA_ROWS), 1)
    same = (ri // CHUNK) == (ci // CHUNK)
    return (jnp.where(same & (ri >= ci), 1.0, 0.0).astype(BF16), jnp.where(same & (ci >= ri), 1.0, 0.0).astype(BF16))


def _per_chunk(fn, like):
    row = lax.broadcasted_iota(jnp.int32, like.shape, 0)
    return [fn((row >= c * CHUNK) & (row < (c + 1) * CHUNK)) for c in range(GLA_STEP)]


def _spread(per_chunk, like):
    row = lax.broadcasted_iota(jnp.int32, like.shape, 0)
    out = per_chunk[-1]
    for c in reversed(range(GLA_STEP - 1)):
        out = jnp.where(row < (c + 1) * CHUNK, per_chunk[c], out)
    return out


def _gla_gate(ga_b, wa_b, b_ref, tri):
    pre = _dot(ga_b, wa_b) + b_ref[...]
    la = _log_sigmoid(pre) * (1.0 / GLA_TAU)
    return pre, _dot01(tri, la)


def _chunk_ends(cum):
    row = lax.broadcasted_iota(jnp.int32, cum.shape, 0)
    return [jnp.sum(jnp.where(row == (c + 1) * CHUNK - 1, cum, 0.0), axis=0, keepdims=True)
            for c in range(GLA_STEP)]


def _heads(width):
    return [slice(h * width, (h + 1) * width) for h in range(GLA_HEADS)]


def _z_specs_gla(rev=None):
    idx = (lambda n: n) if rev is None else rev
    return [
        pl.BlockSpec((GLA_ROWS, GLA_KW), lambda n: (idx(n), 0)),
        pl.BlockSpec((GLA_ROWS, GLA_KW), lambda n: (idx(n), 1)),
        pl.BlockSpec((GLA_ROWS, D_GLA), lambda n: (idx(n), 1)),
        pl.BlockSpec((GLA_ROWS, D_GLA), lambda n: (idx(n), 2)),
        pl.BlockSpec((GLA_ROWS, LANE), lambda n: (idx(n), OFF_GA // LANE)),
    ]


def _gla_fwd(z, wa_pad, b_alpha, g_gla):
    s = z.shape[0]
    nchunk = s // CHUNK

    def body(q_ref, k_ref, v_ref, gg_ref, ga_ref, wa_ref, b_ref, g_ref, y_ref, o_ref, st_ref, pre_ref, cum_ref,
             state):
        @pl.when(pl.program_id(0) == 0)
        def _():
            state[...] = jnp.zeros_like(state)

        ga_b = ga_ref[...].astype(BF16)
        tri, _ = _chunk_triangles()
        nh = range(GLA_HEADS)
        keys, vals = _heads(GLA_DK), _heads(GLA_DV)
        pre, cum = _gla_gate(ga_b, wa_ref[...].astype(BF16), b_ref, tri)
        pre_ref[...] = pre
        cum_ref[...] = cum
        cends = _chunk_ends(cum)
        kd_b = (k_ref[...] * jnp.exp(_spread(cends, cum) - cum)).astype(BF16)
        qs = (q_ref[...] * GLA_SCALE).astype(BF16)
        v_b = v_ref[...].astype(BF16)
        uts = [[_dot_tn(v_b[rs, vals[h]], kd_b[rs, keys[h]]) for h in nh] for rs in GLA_CHUNKS]
        sts, prev = [], [state[h] for h in nh]
        for c in range(GLA_STEP):
            a = jnp.exp(cends[c])
            prev = [prev[h] * a[:, keys[h]] + uts[c][h] for h in nh]
            sts.append(prev)
        for h in nh:
            state[h] = prev[h]
            for c in range(GLA_STEP):
                st_ref[c, h] = sts[c][h]
        outs = [[_dot_nt(qs[rs, keys[h]], sts[c][h].astype(BF16)) for h in nh] for c, rs in enumerate(GLA_CHUNKS)]
        for h in nh:
            o, vs = jnp.concatenate([outs[c][h] for c in range(GLA_STEP)], axis=0), vals[h]
            o_ref[:, vs] = o
            r = lax.rsqrt(jnp.mean(o * o, axis=-1, keepdims=True) + EPS)
            gg = gg_ref[:, vs]
            y_ref[:, vs] = (o * r * g_ref[:, vs] * (gg * _sigmoid(gg))).astype(BF16)

    full = lambda shape: pl.BlockSpec(shape, lambda n: tuple(0 for _ in shape))
    wide = pl.BlockSpec((GLA_ROWS, D_GLA), lambda n: (n, 0))
    return pl.pallas_call(
        body, name="gla_fwd", grid=(nchunk // GLA_STEP,),
        in_specs=_z_specs_gla() + [full((LANE, GLA_KW)), full((1, GLA_KW)), full((1, D_GLA))],
        out_specs=[wide, wide, pl.BlockSpec((GLA_STEP, GLA_HEADS, GLA_DV, GLA_DK), lambda n: (n, 0, 0, 0)),
                   pl.BlockSpec((GLA_ROWS, GLA_KW), lambda n: (n, 0)), pl.BlockSpec((GLA_ROWS, GLA_KW), lambda n: (n, 0))],
        out_shape=[jax.ShapeDtypeStruct((s, D_GLA), BF16), jax.ShapeDtypeStruct((s, D_GLA), F32),
                   jax.ShapeDtypeStruct((nchunk, GLA_HEADS, GLA_DV, GLA_DK), F32),
                   jax.ShapeDtypeStruct((s, GLA_KW), F32), jax.ShapeDtypeStruct((s, GLA_KW), F32)],
        scratch_shapes=[pltpu.VMEM((GLA_HEADS, GLA_DV, GLA_DK), F32)],
        compiler_params=_cparams(("arbitrary",)),
    )(z, z, z, z, z, wa_pad, b_alpha, g_gla)


def _gla_bwd(dyc, o_gla, z, wa_pad, g_gla, states, gate_pre, gate_cum):
    s = z.shape[0]
    nsteps = s // GLA_ROWS
    rev = lambda n: nsteps - 1 - n

    def body(dy_ref, o_ref, q_ref, k_ref, v_ref, gg_ref, ga_ref, wa_ref, g_ref, st_ref, stp_ref, pre_ref, cum_ref,
             dq_ref, dk_ref, dv_ref, dgg_ref, dga_ref, dwa_ref, db_ref, dg_ref, carry):
        step = pl.program_id(0)

        @pl.when(step == 0)
        def _():
            carry[...] = jnp.zeros_like(carry)
            dwa_ref[...] = jnp.zeros_like(dwa_ref)
            db_ref[...] = jnp.zeros_like(db_ref)
            dg_ref[...] = jnp.zeros_like(dg_ref)

        has_prev = (step < nsteps - 1).astype(F32)
        ga_b = ga_ref[...].astype(BF16)
        _, tri_up = _chunk_triangles()
        nh, nc = range(GLA_HEADS), range(GLA_STEP)
        keys, vals = _heads(GLA_DK), _heads(GLA_DV)
        wa_b = wa_ref[...].astype(BF16)
        pre, cum = pre_ref[...], cum_ref[...]
        cends = _chunk_ends(cum)
        e = jnp.exp(_spread(cends, cum) - cum)
        a = [jnp.exp(cends[c]) for c in nc]
        kf = k_ref[...]
        kd_b = (kf * e).astype(BF16)
        v_b = v_ref[...].astype(BF16)
        qs = (q_ref[...] * GLA_SCALE).astype(BF16)
        do_b = []
        for h in nh:
            vs = vals[h]
            o = o_ref[:, vs]
            gg = gg_ref[:, vs]
            g = g_ref[:, vs]
            dy = dy_ref[:, vs]
            r = lax.rsqrt(jnp.mean(o * o, axis=-1, keepdims=True) + EPS)
            sg = _sigmoid(gg)
            dogn = dy * (gg * sg)
            dgg_ref[:, vs] = (dy * (o * r * g) * (sg * (1.0 + gg * (1.0 - sg)))).astype(BF16)
            dg_ref[:, vs] += jnp.sum(dogn * o * r, axis=0, keepdims=True)
            w = dogn * g
            do_b.append((r * (w - o * (r * r) * jnp.mean(w * o, axis=-1, keepdims=True))).astype(BF16))
        dqs = [jnp.concatenate([_dot(do_b[h][rs], st_ref[c, h].astype(BF16)) for c, rs in enumerate(GLA_CHUNKS)],
                               axis=0) for h in nh]
        dq_ref[...] = (jnp.concatenate(dqs, axis=1) * GLA_SCALE).astype(BF16)
        own = [[_dot_tn(do_b[h][rs], qs[rs, keys[h]]) for h in nh] for rs in GLA_CHUNKS]
        gts, later = [None] * GLA_STEP, [carry[h] for h in nh]
        for c in reversed(nc):
            gts[c] = [own[c][h] + later[h] for h in nh]
            later = [gts[c][h] * a[c][:, keys[h]] for h in nh]
        for h in nh:
            carry[h] = later[h]
        gt_b = [[gts[c][h].astype(BF16) for h in nh] for c in nc]
        dkd = jnp.concatenate([jnp.concatenate([_dot(v_b[rs, vals[h]], gt_b[c][h]) for h in nh], axis=1)
                               for c, rs in enumerate(GLA_CHUNKS)], axis=0)
        dvs = [[_dot_nt(kd_b[rs, keys[h]], gt_b[c][h]) for h in nh] for c, rs in enumerate(GLA_CHUNKS)]
        before = lambda c, h: st_ref[c - 1, h] if c > 0 else stp_ref[0, h] * has_prev
        da = [jnp.concatenate([jnp.sum(gts[c][h] * before(c, h), axis=0, keepdims=True) for h in nh], axis=1)
              for c in nc]
        for h in nh:
            dv_ref[:, vals[h]] = jnp.concatenate([dvs[c][h] for c in nc], axis=0).astype(BF16)
        dk_ref[...] = (dkd * e).astype(BF16)
        dd = dkd * kf * e
        dsum = _per_chunk(lambda mine: jnp.sum(jnp.where(mine, dd, 0.0), axis=0, keepdims=True), dd)
        dcend = _spread([dsum[c] + da[c] * a[c] for c in nc], dd)
        dla = dcend - _dot01(tri_up, dd)
        dpre = dla * (1.0 / GLA_TAU) * (1.0 - _sigmoid(pre))
        dpre_b = dpre.astype(BF16)
        dga_ref[...] = _dot_nt(dpre_b, wa_b).astype(BF16)
        dwa_ref[...] += _dot_tn(ga_b, dpre_b)
        db_ref[...] += jnp.sum(dpre, axis=0, keepdims=True)

    full = lambda shape: pl.BlockSpec(shape, lambda n: tuple(0 for _ in shape))
    wide = pl.BlockSpec((GLA_ROWS, D_GLA), lambda n: (rev(n), 0))
    keyw = pl.BlockSpec((GLA_ROWS, GLA_KW), lambda n: (rev(n), 0))
    st_spec = pl.BlockSpec((GLA_STEP, GLA_HEADS, GLA_DV, GLA_DK), lambda n: (rev(n), 0, 0, 0))
    stp_spec = pl.BlockSpec((1, GLA_HEADS, GLA_DV, GLA_DK),
                            lambda n: (jnp.maximum(GLA_STEP * rev(n) - 1, 0), 0, 0, 0))
    return pl.pallas_call(
        body, name="gla_bwd", grid=(nsteps,),
        in_specs=[wide, wide] + _z_specs_gla(rev)
        + [full((LANE, GLA_KW)), full((1, D_GLA)), st_spec, stp_spec, keyw, keyw],
        out_specs=[keyw, keyw, wide, wide, pl.BlockSpec((GLA_ROWS, LANE), lambda n: (rev(n), 0)),
                   full((LANE, GLA_KW)), full((1, GLA_KW)), full((1, D_GLA))],
        out_shape=[jax.ShapeDtypeStruct((s, GLA_KW), BF16), jax.ShapeDtypeStruct((s, GLA_KW), BF16),
                   jax.ShapeDtypeStruct((s, D_GLA), BF16), jax.ShapeDtypeStruct((s, D_GLA), BF16),
                   jax.ShapeDtypeStruct((s, LANE), BF16),
                   jax.ShapeDtypeStruct((LANE, GLA_KW), F32), jax.ShapeDtypeStruct((1, GLA_KW), F32),
                   jax.ShapeDtypeStruct((1, D_GLA), F32)],
        scratch_shapes=[pltpu.VMEM((GLA_HEADS, GLA_DV, GLA_DK), F32)],
        compiler_params=_cparams(("arbitrary",)),
    )(dyc, o_gla, z, z, z, z, z, wa_pad, g_gla, states, states, gate_pre, gate_cum)


def _build_bias_table(rb_row, et_ref):
    far = jnp.broadcast_to(rb_row[:, 2 * REL_CLIP:2 * REL_CLIP + 1], (1, LANE))
    near_hi = rb_row[:, REL_CLIP:2 * REL_CLIP]
    near_lo = rb_row[:, 0:REL_CLIP]
    past = jnp.broadcast_to(rb_row[:, 0:1], (1, LANE))
    seg = [far, far, far, far, near_hi, near_lo] + [past] * (ET_ROWS // LANE - 5)
    ri = lax.broadcasted_iota(jnp.int32, (LANE, LANE), 0)
    ci = lax.broadcasted_iota(jnp.int32, (LANE, LANE), 1)
    for kb in range(ET_ROWS // LANE):
        wmat = jnp.where(ri + ci < LANE, seg[kb], seg[kb + 1])
        blk = pltpu.roll(wmat, 0, 1, stride=1, stride_axis=0)
        lag = LEFT_CHUNKS + ci // CHUNK - (2 * kb + ri // CHUNK)
        et_ref[kb * LANE:(kb + 1) * LANE, :] = jnp.where((lag >= 0) & (lag <= LEFT_CHUNKS), blk, NEG)


def _reduce_bias_table(det_ref):
    lane = lax.broadcasted_iota(jnp.int32, (1, LANE), 1)
    ri = lax.broadcasted_iota(jnp.int32, (LANE, LANE), 0)
    ci = lax.broadcasted_iota(jnp.int32, (LANE, LANE), 1)
    flip = jnp.where(ri + ci == LANE - 1, 1.0, 0.0).astype(BF16)
    segs = jnp.zeros((8, LANE), F32)
    seg_row = lax.broadcasted_iota(jnp.int32, (8, LANE), 0)
    prev_minus = jnp.zeros((1, LANE), F32)
    for kb in range(6):
        rolled = pltpu.roll(_dot01(det_ref[kb * LANE:(kb + 1) * LANE, :], flip, left=False), 0, 1,
                            stride=1, stride_axis=0)
        plus = jnp.sum(jnp.where(ci >= ri, rolled, 0.0), axis=0, keepdims=True)
        minus = jnp.sum(jnp.where(ci < ri, rolled, 0.0), axis=0, keepdims=True)
        segs = segs + jnp.where(seg_row == kb, plus + prev_minus, 0.0)
        prev_minus = minus
    segs = _dot01(segs, flip, left=False)
    pick = lambda kb: jnp.sum(jnp.where(seg_row == kb, segs, 0.0), axis=0, keepdims=True)
    far = jnp.sum(pick(0) + pick(1) + pick(2) + pick(3), axis=1, keepdims=True)
    last = jnp.where(lane == 0, far, 0.0)
    return jnp.concatenate([pick(5), pick(4), last], axis=1)


def _att_window(b):
    c0 = 2 * b
    kstart = pl.multiple_of(jnp.maximum(c0 - LEFT_CHUNKS, 0) * CHUNK, CHUNK)
    eoff = pl.multiple_of(jnp.maximum(LEFT_CHUNKS - c0, 0) * CHUNK, CHUNK)
    return kstart, eoff


def _att_probs(q_b, kw_b, et):
    st = _dot_nt(kw_b, q_b) * ATT_SCALE + et
    m = jnp.max(st, axis=0, keepdims=True)
    ex = jnp.exp(st - m)
    return ex * (1.0 / jnp.sum(ex, axis=0, keepdims=True))


def _att_fwd(z, rb_pad, g_att):
    s = z.shape[0]
    nblk = s // QB
    c_aq, c_ak, c_av, c_ag = [(OFF_AQ + i * D_ATT) // ATT_HD for i in range(4)]

    def body(q_ref, k_ref, v_ref, ag_ref, rb_ref, g_ref, y_ref, o_ref, p_ref, et_ref, kb_ref, vb_ref):
        h = pl.program_id(0)
        b = pl.program_id(1)

        @pl.when(b == 0)
        def _():
            _build_bias_table(rb_ref[pl.ds(h, 1), :], et_ref)
            kb_ref[...] = k_ref[...].astype(BF16)
            vb_ref[...] = v_ref[...].astype(BF16)

        for j in range(ATT_UNROLL):
            rs = slice(j * QB, (j + 1) * QB)
            kstart, eoff = _att_window(b * ATT_UNROLL + j)
            q_b = q_ref[rs, :].astype(BF16)
            kw_b = kb_ref[pl.ds(kstart, WIN), :]
            vw_b = vb_ref[pl.ds(kstart, WIN), :]
            pt = _att_probs(q_b, kw_b, et_ref[pl.ds(eoff, WIN), :])
            p_ref[0, j] = pt
            o = _dot_tn(pt.astype(BF16), vw_b)
            o_ref[rs, :] = o
            r = lax.rsqrt(jnp.mean(o * o, axis=-1, keepdims=True) + EPS)
            ag = ag_ref[rs, :]
            y_ref[rs, :] = (o * r * g_ref[...] * (ag * _sigmoid(ag))).astype(BF16)

    blk = lambda col: pl.BlockSpec((ATT_UNROLL * QB, ATT_HD), lambda h, b: (b, col + h))
    seq = lambda col: pl.BlockSpec((s, ATT_HD), lambda h, b: (0, col + h))
    out_blk = pl.BlockSpec((ATT_UNROLL * QB, ATT_HD), lambda h, b: (b, h))
    return pl.pallas_call(
        body, name="att_fwd", grid=(ATT_HEADS, nblk // ATT_UNROLL),
        in_specs=[blk(c_aq), seq(c_ak), seq(c_av), blk(c_ag),
                  pl.BlockSpec((ATT_HEADS, 3 * LANE), lambda h, b: (0, 0)),
                  pl.BlockSpec((1, ATT_HD), lambda h, b: (0, h))],
        out_specs=[out_blk, out_blk, pl.BlockSpec((1, ATT_UNROLL, WIN, QB), lambda h, b: (h, b, 0, 0))],
        out_shape=[jax.ShapeDtypeStruct((s, D_ATT), BF16), jax.ShapeDtypeStruct((s, D_ATT), F32),
                   jax.ShapeDtypeStruct((ATT_HEADS, nblk, WIN, QB), F32)],
        scratch_shapes=[pltpu.VMEM((ET_ROWS, LANE), F32), pltpu.VMEM((s, ATT_HD), BF16),
                        pltpu.VMEM((s, ATT_HD), BF16)],
        compiler_params=_cparams(("arbitrary", "arbitrary")),
    )(z, z, z, z, rb_pad, g_att)


def _att_bwd(dyc, o_att, probs, z, g_att):
    s = z.shape[0]
    nblk = s // QB
    c_aq, c_ak, c_av, c_ag = [(OFF_AQ + i * D_ATT) // ATT_HD for i in range(4)]
    c_dy = D_GLA // ATT_HD

    def body(dy_ref, o_ref, p_ref, q_ref, k_ref, v_ref, ag_ref, g_ref,
             dq_ref, dk_ref, dv_ref, dag_ref, drb_ref, dg_ref, det_ref, kb_ref, vb_ref, dk_acc, dv_acc):
        b = pl.program_id(1)

        @pl.when(b == 0)
        def _():
            kb_ref[...] = k_ref[...].astype(BF16)
            vb_ref[...] = v_ref[...].astype(BF16)
            det_ref[...] = jnp.zeros_like(det_ref)
            dk_acc[...] = jnp.zeros_like(dk_acc)
            dv_acc[...] = jnp.zeros_like(dv_acc)
            dg_ref[...] = jnp.zeros_like(dg_ref)

        g = g_ref[...]
        dg = jnp.zeros((1, ATT_HD), F32)
        for j in range(ATT_UNROLL):
            rs = slice(j * QB, (j + 1) * QB)
            kstart, eoff = _att_window(b * ATT_UNROLL + j)
            q_b = q_ref[rs, :].astype(BF16)
            kw_b = kb_ref[pl.ds(kstart, WIN), :]
            vw_b = vb_ref[pl.ds(kstart, WIN), :]
            pt = p_ref[0, j]
            o = o_ref[rs, :]
            ag = ag_ref[rs, :]
            dy = dy_ref[rs, :]
            r = lax.rsqrt(jnp.mean(o * o, axis=-1, keepdims=True) + EPS)
            sg = _sigmoid(ag)
            don = dy * (ag * sg)
            dag_ref[rs, :] = (dy * (o * r * g) * (sg * (1.0 + ag * (1.0 - sg)))).astype(BF16)
            dg = dg + jnp.sum(don * o * r, axis=0, keepdims=True)
            w = don * g
            do_b = (r * (w - o * (r * r) * jnp.mean(w * o, axis=-1, keepdims=True))).astype(BF16)
            pt_b = pt.astype(BF16)
            dpt = _dot_nt(vw_b, do_b)
            dst = pt * (dpt - jnp.sum(dpt * pt, axis=0, keepdims=True))
            det_ref[pl.ds(eoff, WIN), :] += dst
            ds_b = (dst * ATT_SCALE).astype(BF16)
            dq_ref[rs, :] = _dot_tn(ds_b, kw_b).astype(BF16)
            dk_acc[pl.ds(kstart, WIN), :] += _dot(ds_b, q_b)
            dv_acc[pl.ds(kstart, WIN), :] += _dot(pt_b, do_b)
        dg_ref[...] += dg

        @pl.when(b == nblk // ATT_UNROLL - 1)
        def _():
            drb_ref[0] = jnp.broadcast_to(_reduce_bias_table(det_ref), (8, 3 * LANE))
            dk_ref[...] = dk_acc[...].astype(BF16)
            dv_ref[...] = dv_acc[...].astype(BF16)

    blk = lambda col: pl.BlockSpec((ATT_UNROLL * QB, ATT_HD), lambda h, b: (b, col + h))
    seq = lambda col: pl.BlockSpec((s, ATT_HD), lambda h, b: (0, col + h))
    out_blk = pl.BlockSpec((ATT_UNROLL * QB, ATT_HD), lambda h, b: (b, h))
    out_seq = pl.BlockSpec((s, ATT_HD), lambda h, b: (0, h))
    return pl.pallas_call(
        body, name="att_bwd", grid=(ATT_HEADS, nblk // ATT_UNROLL),
        in_specs=[blk(c_dy), blk(0), pl.BlockSpec((1, ATT_UNROLL, WIN, QB), lambda h, b: (h, b, 0, 0)),
                  blk(c_aq), seq(c_ak), seq(c_av), blk(c_ag),
                  pl.BlockSpec((1, ATT_HD), lambda h, b: (0, h))],
        out_specs=[out_blk, out_seq, out_seq, out_blk,
                   pl.BlockSpec((1, 8, 3 * LANE), lambda h, b: (h, 0, 0)),
                   pl.BlockSpec((1, ATT_HD), lambda h, b: (0, h))],
        out_shape=[jax.ShapeDtypeStruct((s, D_ATT), BF16), jax.ShapeDtypeStruct((s, D_ATT), BF16),
                   jax.ShapeDtypeStruct((s, D_ATT), BF16), jax.ShapeDtypeStruct((s, D_ATT), BF16),
                   jax.ShapeDtypeStruct((ATT_HEADS, 8, 3 * LANE), F32),
                   jax.ShapeDtypeStruct((1, D_ATT), F32)],
        scratch_shapes=[pltpu.VMEM((ET_ROWS, LANE), F32),
                        pltpu.VMEM((s, ATT_HD), BF16), pltpu.VMEM((s, ATT_HD), BF16),
                        pltpu.VMEM((s, ATT_HD), F32), pltpu.VMEM((s, ATT_HD), F32)],
        compiler_params=_cparams(("arbitrary", "arbitrary")),
    )(dyc, o_att, probs, z, z, z, z, g_att)


ADAM_ROWS = 64
ADAM_COL_ROWS = 32


def _adam_math(w, g, m, v):
    m2 = ADAM_B1 * m + (1.0 - ADAM_B1) * g
    v2 = ADAM_B2 * v + (1.0 - ADAM_B2) * (g * g)
    m_hat = m2 / (1.0 - ADAM_B1 ** ADAM_STEP)
    v_hat = v2 / (1.0 - ADAM_B2 ** ADAM_STEP)
    delta = -ADAM_LR * (m_hat / (jnp.sqrt(v_hat) + ADAM_EPS) + ADAM_WD * w)
    return delta, m2, v2


def _adam_sharded(parts, first, w, m, v, name):
    nl, nr, nc = w.shape

    def body(*refs):
        p_refs = refs[:nl]
        w_ref, m_ref, v_ref, g_ref, d_ref, m2_ref, v2_ref = refs[nl:]
        for k in range(nl):
            @pl.when(pl.program_id(0) == k)
            def _(p_ref=p_refs[k]):
                g = p_ref[0].astype(F32)
                for dev in range(1, N_DEV):
                    g = g + p_ref[dev].astype(F32)
                delta, m2, v2 = _adam_math(w_ref[0], g, m_ref[0], v_ref[0])
                g_ref[0] = g
                d_ref[0] = delta
                m2_ref[0] = m2
                v2_ref[0] = v2

    def part_spec(k):
        return pl.BlockSpec((N_DEV, ADAM_ROWS, nc), lambda l, i: (0, first + jnp.where(l == k, i, 0), 0))

    blk = pl.BlockSpec((1, ADAM_ROWS, nc), lambda l, i: (l, i, 0))
    shp = jax.ShapeDtypeStruct(w.shape, F32)
    return pl.pallas_call(
        body, name=name, grid=(nl, pl.cdiv(nr, ADAM_ROWS)),
        in_specs=[part_spec(k) for k in range(nl)] + [blk, blk, blk],
        out_specs=[blk, blk, blk, blk],
        out_shape=[shp, shp, shp, shp],
        compiler_params=_cparams(("arbitrary", "arbitrary")),
    )(*parts, w, m, v)


def _adam_columns(parts, first, w, m, v):
    nc, nl, d = w.shape

    def body(*refs):
        p_refs = refs[:nl]
        w_ref, m_ref, v_ref, g_ref, d_ref, m2_ref, v2_ref = refs[nl:]
        for l in range(nl):
            g = p_refs[l][0].astype(F32)
            for dev in range(1, N_DEV):
                g = g + p_refs[l][dev].astype(F32)
            delta, m2, v2 = _adam_math(w_ref[:, l, :], g, m_ref[:, l, :], v_ref[:, l, :])
            g_ref[:, l, :] = g
            d_ref[:, l, :] = delta
            m2_ref[:, l, :] = m2
            v2_ref[:, l, :] = v2

    blk = pl.BlockSpec((ADAM_COL_ROWS, nl, d), lambda i: (i, 0, 0))
    part = pl.BlockSpec((N_DEV, ADAM_COL_ROWS, d), lambda i: (0, first + i, 0))
    shp = jax.ShapeDtypeStruct(w.shape, F32)
    return pl.pallas_call(
        body, name="adam_w_in", grid=(pl.cdiv(nc, ADAM_COL_ROWS),),
        in_specs=[part] * nl + [blk, blk, blk],
        out_specs=[blk, blk, blk, blk],
        out_shape=[shp, shp, shp, shp],
        compiler_params=_cparams(("parallel",)),
    )(*parts, w, m, v)


def _adam_small(ws, gs, ms, vs):
    n = len(ws)

    def body(*refs):
        w_refs, g_refs, m_refs, v_refs, d_refs, m2_refs, v2_refs = [refs[i * n:(i + 1) * n] for i in range(7)]
        for i in range(n):
            delta, m2, v2 = _adam_math(w_refs[i][...], g_refs[i][...], m_refs[i][...], v_refs[i][...])
            d_refs[i][...] = delta
            m2_refs[i][...] = m2
            v2_refs[i][...] = v2

    shapes = [jax.ShapeDtypeStruct(w.shape, F32) for w in ws]
    out = pl.pallas_call(body, name="adam_small", out_shape=shapes * 3)(*ws, *gs, *ms, *vs)
    return out[:n], out[n:2 * n], out[2 * n:]


def _position():
    return lax.axis_index("x"), lax.axis_index("y"), lax.axis_index("c")


def _slot(p):
    return 4 * p[0] + 2 * p[1] + p[2]


BF16_TILE_ROWS = 16


def _slab_rows(rows, cols):
    return -(-(rows + cols) // BF16_TILE_ROWS) * BF16_TILE_ROWS


RELAYOUT_COLS = 1024
RELAYOUT_CHUNK = 64


def _shard_pieces(dev, rows, cols):
    moved = ((0, GA_ORIG, 0), (GA_ORIG, GA_ORIG + GLA_RANK, OFF_GA - GA_ORIG), (GA_ORIG + GLA_RANK, D_IN, -GLA_RANK))
    c0, c1 = dev * cols, (dev + 1) * cols
    return [(rows + max(c0, lo) - c0, max(c0, lo) + off, min(c1, hi) - max(c0, lo))
            for lo, hi, off in moved if max(c0, lo) < min(c1, hi)]


def _move_rows(src, src_row, dst, dst_row, n):
    assert src_row % 2 == 0 and dst_row % 2 == 0 and n % 2 == 0
    for r in range(0, n // 2, RELAYOUT_CHUNK):
        m = min(RELAYOUT_CHUNK, n // 2 - r)
        dst[dst_row // 2 + r:dst_row // 2 + r + m, :] = src[src_row // 2 + r:src_row // 2 + r + m, :]


def _aligned_weight(land, rows, cols):
    _, slab, d = land.shape
    ct = min(RELAYOUT_COLS, d)

    def body(land_ref, wt_ref, wo_ref):
        dev = pl.program_id(1)
        src = land_ref.bitcast(jnp.uint32)
        dst = wt_ref.bitcast(jnp.uint32)
        wo_ref[...] = land_ref[0:rows, :]

        @pl.when(dev == 0)
        def _():
            dst[D_IN // 2:D_ZP // 2, :] = jnp.zeros(((D_ZP - D_IN) // 2, ct), jnp.uint32)

        for k in range(N_DEV):
            @pl.when(dev == k)
            def _(k=k):
                for at, to, n in _shard_pieces(k, rows, cols):
                    _move_rows(src, at, dst, to, n)

    return pl.pallas_call(
        body, name="aligned_weight", grid=(d // ct, N_DEV),
        in_specs=[pl.BlockSpec((slab, ct), lambda c, dev: (dev, c))],
        out_specs=[pl.BlockSpec((D_ZP, ct), lambda c, dev: (0, c)),
                   pl.BlockSpec((rows, ct), lambda c, dev: (dev, c))],
        out_shape=[jax.ShapeDtypeStruct((D_ZP, d), land.dtype),
                   jax.ShapeDtypeStruct((N_DEV * rows, d), land.dtype)],
        compiler_params=_cparams(("parallel", "arbitrary")),
    )(land.reshape(N_DEV * slab, d))


def _partial_slabs(dwt, cols):
    d = dwt[0].shape[1]
    bounds = (0, GA_ORIG, OFF_GA, D_ZP)
    assert tuple(a.shape[0] for a in dwt) == tuple(hi - lo for lo, hi in zip(bounds, bounds[1:]))
    slab = _slab_rows(0, cols)
    ct = min(RELAYOUT_COLS, d)

    def body(*refs):
        out_ref = refs[-1]
        dev = pl.program_id(1)
        srcs = [ref.bitcast(jnp.uint32) for ref in refs[:-1]]
        dst = out_ref.bitcast(jnp.uint32)
        dst[cols // 2:slab // 2, :] = jnp.zeros(((slab - cols) // 2, ct), jnp.uint32)
        for k in range(N_DEV):
            @pl.when(dev == k)
            def _(k=k):
                for to, at, n in _shard_pieces(k, 0, cols):
                    which = max(i for i, lo in enumerate(bounds[:-1]) if lo <= at)
                    assert at + n <= bounds[which + 1]
                    _move_rows(srcs[which], at - bounds[which], dst, to, n)

    return pl.pallas_call(
        body, name="partial_slabs", grid=(d // ct, N_DEV),
        in_specs=[pl.BlockSpec((a.shape[0], ct), lambda c, dev: (0, c)) for a in dwt],
        out_specs=pl.BlockSpec((slab, ct), lambda c, dev: (dev, c)),
        out_shape=jax.ShapeDtypeStruct((N_DEV * slab, d), dwt[0].dtype),
        compiler_params=_cparams(("parallel", "arbitrary")),
    )(*dwt).reshape(N_DEV, slab, d)


def _peer(pos, k):
    x, y, c = pos
    return (1 - x if k & 4 else x, 1 - y if k & 2 else y, 1 - c if k & 1 else c)


HBM_SPEC = pl.BlockSpec(memory_space=pltpu.HBM)
SEM_SPEC = pl.BlockSpec(memory_space=pltpu.SEMAPHORE)
GATHER_PEERS = (1, 4, 2, 6)
ALL_PEERS = (1, 2, 3, 4, 5, 6, 7)


def _hbm(a):
    return pltpu.with_memory_space_constraint(a, pltpu.HBM)


def _split_copies(src_ref, land_ref, send_sems, recv_sems, ks, per_peer, landed):
    me = _position()
    out = []
    for i, k in enumerate(ks):
        peer = _peer(me, k)
        src = src_ref.at[_slot(peer)] if per_peer else src_ref
        dst = land_ref.at[_slot(peer) if landed else _slot(me)]
        out.append(pltpu.make_async_remote_copy(
            src_ref=src, dst_ref=dst, send_sem=send_sems.at[i], recv_sem=recv_sems.at[i],
            device_id=peer, device_id_type=MESH))
    return out


def _exchange_start(src, after, ks, per_peer, name):
    slab = src.shape[1:] if per_peer else src.shape
    land_shape = (N_DEV,) + tuple(slab)
    n = len(ks)

    def body(src_ref, land_ref, after_ref, send_sems, recv_sems, src_thru, land_thru, token):
        for cp in _split_copies(src_ref, land_ref, send_sems, recv_sems, ks, per_peer, landed=False):
            cp.start()
        token[...] = jnp.zeros_like(token)

    return pl.pallas_call(
        body, name=name,
        out_shape=(pltpu.SemaphoreType.DMA((n,)), pltpu.SemaphoreType.DMA((n,)),
                   pltpu.HBM(src.shape, src.dtype), pltpu.HBM(land_shape, src.dtype),
                   jax.ShapeDtypeStruct((8, LANE), F32)),
        in_specs=(HBM_SPEC, HBM_SPEC, ANY),
        out_specs=(SEM_SPEC, SEM_SPEC, HBM_SPEC, HBM_SPEC, pl.BlockSpec(memory_space=pltpu.VMEM)),
        input_output_aliases={0: 2, 1: 3},
        compiler_params=pltpu.CompilerParams(has_side_effects=pltpu.SideEffectType.DATAFLOW_SIDE_EFFECTING),
    )(_hbm(src), _hbm(lax.empty(land_shape, src.dtype)), after)


def _exchange_wait(started, after, ks, per_peer, name):
    send_sems, recv_sems, src_thru, land_thru = started

    def body(src_ref, land_ref, send_sems, recv_sems, after_ref, src_dead, land_out):
        for cp in _split_copies(src_ref, land_ref, send_sems, recv_sems, ks, per_peer, landed=True):
            cp.wait_send()
            cp.wait_recv()

    return pl.pallas_call(
        body, name=name,
        out_shape=(pltpu.HBM(src_thru.shape, src_thru.dtype), pltpu.HBM(land_thru.shape, land_thru.dtype)),
        in_specs=(HBM_SPEC, HBM_SPEC, SEM_SPEC, SEM_SPEC, ANY), out_specs=(HBM_SPEC, HBM_SPEC),
        input_output_aliases={0: 0, 1: 1},
        compiler_params=pltpu.CompilerParams(has_side_effects=pltpu.SideEffectType.DATAFLOW_SIDE_EFFECTING),
    )(src_thru, land_thru, send_sems, recv_sems, after)


def _relay_copies(land_ref, send_sems, recv_sems, landed):
    me = _position()
    sibling = _peer(me, 1)
    out = []
    for i, k in enumerate(GATHER_PEERS[1:]):
        blk = land_ref.at[_slot(_peer(sibling if landed else me, k))]
        out.append(pltpu.make_async_remote_copy(
            src_ref=blk, dst_ref=blk, send_sem=send_sems.at[i], recv_sem=recv_sems.at[i],
            device_id=sibling, device_id_type=MESH))
    return out


def _relay_start(land, name):
    n = len(GATHER_PEERS) - 1

    def body(land_ref, send_sems, recv_sems, land_thru, token):
        for cp in _relay_copies(land_ref, send_sems, recv_sems, landed=False):
            cp.start()
        token[...] = jnp.zeros_like(token)

    return pl.pallas_call(
        body, name=name,
        out_shape=(pltpu.SemaphoreType.DMA((n,)), pltpu.SemaphoreType.DMA((n,)),
                   pltpu.HBM(land.shape, land.dtype), jax.ShapeDtypeStruct((8, LANE), F32)),
        in_specs=(HBM_SPEC,),
        out_specs=(SEM_SPEC, SEM_SPEC, HBM_SPEC, pl.BlockSpec(memory_space=pltpu.VMEM)),
        input_output_aliases={0: 2},
        compiler_params=pltpu.CompilerParams(has_side_effects=pltpu.SideEffectType.DATAFLOW_SIDE_EFFECTING),
    )(_hbm(land))


def _relay_wait(started, after, name):
    send_sems, recv_sems, land_thru = started

    def body(land_ref, send_sems, recv_sems, after_ref, land_out):
        for cp in _relay_copies(land_ref, send_sems, recv_sems, landed=True):
            cp.wait_send()
            cp.wait_recv()

    return pl.pallas_call(
        body, name=name,
        out_shape=pltpu.HBM(land_thru.shape, land_thru.dtype),
        in_specs=(HBM_SPEC, SEM_SPEC, SEM_SPEC, ANY), out_specs=HBM_SPEC,
        input_output_aliases={0: 0},
        compiler_params=pltpu.CompilerParams(has_side_effects=pltpu.SideEffectType.DATAFLOW_SIDE_EFFECTING),
    )(land_thru, send_sems, recv_sems, after)


def _share(vec, name, after=None):
    follows = [] if after is None else [after]

    def body(vec_ref, *rest):
        out_ref, send_sems, recv_sems, local_sem = rest[len(follows):]
        me = _position()

        def copy(k, landed):
            peer = _peer(me, k)
            return pltpu.make_async_remote_copy(
                src_ref=vec_ref, dst_ref=out_ref.at[_slot(peer) if landed else _slot(me)],
                send_sem=send_sems.at[k - 1], recv_sem=recv_sems.at[k - 1], device_id=peer, device_id_type=MESH)

        mine = pltpu.make_async_copy(vec_ref, out_ref.at[_slot(me)], local_sem)
        mine.start()
        sent = [copy(k, False) for k in ALL_PEERS]
        for cp in sent:
            cp.start()
        for k in ALL_PEERS:
            copy(k, True).wait_recv()
        for cp in sent:
            cp.wait_send()
        mine.wait()

    return pl.pallas_call(
        body, name=name,
        in_specs=[ANY] * (1 + len(follows)), out_specs=ANY,
        out_shape=jax.ShapeDtypeStruct((N_DEV,) + vec.shape, vec.dtype),
        scratch_shapes=[pltpu.SemaphoreType.DMA((N_DEV - 1,)), pltpu.SemaphoreType.DMA((N_DEV - 1,)),
                        pltpu.SemaphoreType.DMA],
    )(vec, *follows)


def _sum_slots(parts):
    def body(p_ref, o_ref):
        acc = p_ref[0]
        for dev in range(1, N_DEV):
            acc = acc + p_ref[dev]
        o_ref[...] = acc

    return pl.pallas_call(body, name="sum_slots",
                          out_shape=jax.ShapeDtypeStruct(parts.shape[1:], F32))(parts)


PACK_ROWS = 8


def _packed_rows(size):
    return -(-size // (PACK_ROWS * LANE)) * PACK_ROWS


def _pack(arrs):
    def rows(a):
        flat = a.reshape(-1)
        return jnp.pad(flat, (0, _packed_rows(flat.shape[0]) * LANE - flat.shape[0])).reshape(-1, LANE)

    return jnp.concatenate([rows(a) for a in arrs], axis=0)


def _unpack(packed, shapes):
    out, at = [], 0
    for shp in shapes:
        size = 1
        for dim in shp:
            size *= dim
        nrows = _packed_rows(size)
        out.append(packed[at:at + nrows].reshape(-1)[:size].reshape(shp))
        at += nrows
    return out


def _layer_fwd(x, wt, wo, g_pre, g_post, wa_pad, b_alpha, g_gla, g_att, rb_pad, midway=None):
    h = _rms_fwd(x, g_pre)
    z = _matmul(h, wt, "nt", F32, *TILES["in_proj"], "in_proj", n_outer=True)
    y_gla, o_gla, *gla_kept = _gla_fwd(z, wa_pad, b_alpha, g_gla)
    if midway is not None:
        g_att = g_att + midway(y_gla)[:1, :1]
    y_att, o_att, probs = _att_fwd(z, rb_pad, g_att)
    y = _matmul_cols([y_gla, y_att], wo, F32, *TILES["out_proj"][:2], "out_proj")
    out = _post_fwd(x, y, g_post)
    return out, (x, h, z, o_gla, gla_kept, o_att, probs, y_gla, y_att, y)


def _layer_bwd(dout, saved, wt, wo, g_pre, g_post, wa_pad, b_alpha, g_gla, g_att, rb_pad, on_dwo, on_dwt):
    x, h, z, o_gla, gla_kept, o_att, probs, y_gla, y_att, y = saved
    dy, dg_post = _post_bwd(dout, y, g_post)
    dwo = _matmul_rows([y_gla, y_att], dy, BF16, *TILES["out_proj_dw"][:2], "out_proj_dw")
    token = on_dwo(dwo)
    dycat = _matmul(dy, wo, "nt", F32, *TILES["out_proj_dx"], "out_proj_dx", n_outer=True, after=token)
    dq, dk, dv, dgg, dga, dwa, db, dg_gla = _gla_bwd(dycat, o_gla, z, wa_pad, g_gla, *gla_kept)
    daq, dak, dav, dag, drb, dg_att = _att_bwd(dycat, o_att, probs, z, g_att)
    tw, tn = TILES["in_proj_dw"][:2]
    dwt = (_matmul_rows([dq, dk, dv, dgg], h, BF16, tw, tn, "in_proj_dw_gla"),
           _matmul_rows([daq, dak, dav, dag], h, BF16, tw, tn, "in_proj_dw_att"),
           _matmul_rows([dga], h, BF16, LANE, tn, "in_proj_dw_gate"))
    token = on_dwt(dwt)
    dh = _matmul_cols([dq, dk, dv, dgg, daq, dak, dav, dag, dga], wt, F32, *TILES["in_proj_dx"][:2],
                      "in_proj_dx", after=token)
    dx, dg_pre = _pre_bwd(dh, x, g_pre, dout)
    small = (dg_pre[0], dg_post[0], dwa[:GLA_RANK], db[0], dg_gla[0], dg_att[0], drb[:, 0, :N_REL])
    return dx, small


def kernel(x, w_in, w_out, g_pre, g_post, w_alpha, b_alpha, g_gla, g_att, rel_bias, loss_target, m_w_in, m_w_out, m_g_pre, m_g_post, m_w_alpha, m_b_alpha, m_g_gla, m_g_att, m_rel_bias, v_w_in, v_w_out, v_g_pre, v_g_post, v_w_alpha, v_b_alpha, v_g_gla, v_g_att, v_rel_bias):
    nl, d, cols = w_in.shape
    rows = w_out.shape[1]
    s = x.shape[1]
    x0 = x.reshape(s, d)
    tgt = loss_target.reshape(s, d)

    cols_first = lambda a: jnp.transpose(a, (2, 0, 1))
    w_c = cols_first(w_in)
    slab = _slab_rows(rows, cols)
    is_out = lax.broadcasted_iota(jnp.int32, (slab, d), 0) < rows

    def shard(l, zero=0.0):
        top = jnp.pad((w_out[l] + zero).astype(BF16), ((0, slab - rows), (0, 0)))
        rest = jnp.pad((w_c[:, l] + zero).astype(BF16), ((rows, slab - rows - cols), (0, 0)))
        return jnp.where(is_out, top, rest)

    first_fetch = _exchange_start(shard(0), x, GATHER_PEERS, False, "gather_start_0")
    began = first_fetch[4][0, 0]
    shards = [None] + [shard(l, began) for l in range(1, nl)]
    alpha = _pack([w_alpha]) + began
    wa_g = _share(alpha, "gather_alpha")
    wa_cols = w_alpha.shape[2]
    wa_full = wa_g.reshape(N_DEV, -1)[:, :nl * GLA_RANK * wa_cols].reshape(N_DEV, nl, GLA_RANK, wa_cols)
    wa_full = jnp.transpose(wa_full, (1, 2, 0, 3)).reshape(nl, GLA_RANK, GLA_KW)
    wa_pad = jnp.pad(wa_full, ((0, 0), (0, LANE - GLA_RANK), (0, 0)))
    rb_pad = jnp.pad(rel_bias, ((0, 0), (0, 0), (0, 3 * LANE - N_REL)))

    def layer_args(l, follows=None):
        gp = g_pre[l:l + 1] if follows is None else g_pre[l:l + 1] + follows[:1, :1]
        return (wts[l], wos[l], gp, g_post[l:l + 1], wa_pad[l], b_alpha[l:l + 1], g_gla[l:l + 1],
                g_att[l:l + 1], rb_pad[l])

    my = _slot(_position())

    def fetch(l, after):
        return _exchange_start(shards[l], after, GATHER_PEERS, False, f"gather_start_{l}")

    def relay(l, first_hop, after):
        own[l], land = _exchange_wait(first_hop[:4], after, GATHER_PEERS, False, f"gather_wait_{l}")
        return _relay_start(land, f"relay_start_{l}")

    def midway(l, y):
        flight["relay"] = relay(l + 1, flight["fetch"], y)
        if l + 2 >= nl:
            return flight["relay"][3]
        flight["fetch"] = fetch(l + 2, flight["relay"][2])
        return flight["fetch"][4]

    act, saved, wts, wos, flight, own = x0, [], [], [], {}, [None] * nl
    prepared = (wa_pad[0, :1, :1] + sum(sh[:1, :1].astype(F32) for sh in shards[1:]))
    flight["relay"] = relay(0, first_fetch, prepared)
    if nl > 1:
        flight["fetch"] = fetch(1, flight["relay"][2])
    for l in range(nl):
        land = _relay_wait(flight["relay"][:3], act, f"relay_wait_{l}")
        land = lax.dynamic_update_slice_in_dim(land, own[l][None], my, 0)
        wt_l, wo_l = _aligned_weight(land, rows, cols)
        wts.append(wt_l)
        wos.append(wo_l)
        act, sv = _layer_fwd(act, *layer_args(l, follows=first_fetch[4] if l == 0 else None),
                             midway=functools.partial(midway, l) if l + 1 < nl else None)
        saved.append(sv)
    dout, sq = _loss_head(act, tgt)
    loss = lax.psum(sq[0, 0] * (0.5 / d), ("x", "y", "c"))

    smalls, pending_out, pending_in = [None] * nl, [None] * nl, [None] * nl

    def send_out(l, dwo):
        pending_out[l] = _exchange_start(dwo.reshape(N_DEV, rows, d), dwo[:1, :1], ALL_PEERS, True,
                                         f"scatter_out_start_{l}")
        return pending_out[l][4]

    def send_in(l, dwt):
        pending_in[l] = _exchange_start(_partial_slabs(dwt, cols), dwt[-1], ALL_PEERS, True,
                                        f"scatter_in_start_{l}")
        return pending_in[l][4]

    for l in reversed(range(nl)):
        dout, smalls[l] = _layer_bwd(dout, saved[l], *layer_args(l), on_dwo=functools.partial(send_out, l),
                                     on_dwt=functools.partial(send_in, l))
    grad_x = dout.reshape(x.shape)

    def landed(started, after, name):
        partial, land = _exchange_wait(started[:4], after, ALL_PEERS, True, name)
        return lax.dynamic_update_slice_in_dim(land, lax.dynamic_slice_in_dim(partial, my, 1, 0), my, 0)

    parts_out = [landed(pending_out[l], dout, f"scatter_out_wait_{l}") for l in range(nl)]
    g_w_out, d_w_out, m2_w_out, v2_w_out = _adam_sharded(parts_out, 0, w_out, m_w_out, v_w_out, "adam_w_out")
    names = 7
    small_stacked = [jnp.stack([smalls[l][i] for l in range(nl)]) for i in range(names)]
    shapes = [a.shape for a in small_stacked]
    gathered = _share(_pack(small_stacked), "gather_small_grads", after=d_w_out)
    g_pre_g, g_post_g, wa_g_full, b_g, gla_g, att_g, rb_g = _unpack(_sum_slots(gathered), shapes)
    wa_g_mine = lax.dynamic_slice_in_dim(wa_g_full, my * wa_cols, wa_cols, axis=2)
    grads = [g_pre_g, g_post_g, wa_g_mine, b_g, gla_g, att_g, rb_g]
    ws = [g_pre, g_post, w_alpha, b_alpha, g_gla, g_att, rel_bias]
    ms = [m_g_pre, m_g_post, m_w_alpha, m_b_alpha, m_g_gla, m_g_att, m_rel_bias]
    vs = [v_g_pre, v_g_post, v_w_alpha, v_b_alpha, v_g_gla, v_g_att, v_rel_bias]
    d_s, m2_s, v2_s = _adam_small(ws, grads, ms, vs)

    parts_in = [landed(pending_in[l], d_s[0], f"scatter_in_wait_{l}") for l in range(nl)]
    g_w_in, d_w_in, m2_w_in, v2_w_in = [
        jnp.transpose(a, (1, 2, 0))
        for a in _adam_columns(parts_in, 0, w_c, cols_first(m_w_in), cols_first(v_w_in))]

    def ordered(big_in, big_out, small):
        return [big_in, big_out] + list(small)

    return (loss, grad_x,
            *ordered(g_w_in, g_w_out, grads),
            *ordered(d_w_in, d_w_out, d_s),
            *ordered(m2_w_in, m2_w_out, m2_s),
            *ordered(v2_w_in, v2_w_out, v2_s))
```

```python
import functools

import jax
import jax.numpy as jnp
from jax import lax
from jax.experimental import pallas as pl
from jax.experimental.pallas import tpu as pltpu

F32 = jnp.float32
BF16 = jnp.bfloat16
MESH = pl.DeviceIdType.MESH
ANY = pl.BlockSpec(memory_space=pl.ANY)

CHUNK = 64
GLA_HEADS = 4
GLA_DK = 128
GLA_DV = 256
GLA_KW = GLA_HEADS * GLA_DK
D_GLA = GLA_HEADS * GLA_DV
GLA_RANK = 16
GLA_TAU = 16.0
ATT_HEADS = 8
ATT_HD = 128
D_ATT = ATT_HEADS * ATT_HD
LEFT_CHUNKS = 8
REL_CLIP = 128
N_REL = 2 * REL_CLIP + 1
EPS = 1e-6
D_IN = 2 * GLA_KW + 2 * D_GLA + GLA_RANK + 4 * D_ATT
GLA_SCALE = GLA_DK ** -0.5
ATT_SCALE = ATT_HD ** -0.5

ADAM_LR = 0.001
ADAM_B1 = 0.9
ADAM_B2 = 0.999
ADAM_EPS = 1e-08
ADAM_WD = 0.01
ADAM_STEP = 10

N_DEV = 8
LANE = 128
GA_ORIG = 2 * GLA_KW + 2 * D_GLA
OFF_AQ = GA_ORIG
OFF_GA = GA_ORIG + 4 * D_ATT
D_ZP = OFF_GA + LANE
QB = 2 * CHUNK
ATT_UNROLL = 8
WIN = (LEFT_CHUNKS + 2) * CHUNK
ET_ROWS = WIN + LEFT_CHUNKS * CHUNK
NEG = -1e30
VMEM_LIMIT = 48 * 1024 * 1024


def _cparams(sem):
    return pltpu.CompilerParams(dimension_semantics=sem, vmem_limit_bytes=VMEM_LIMIT)


def _dot(a, b):
    return jnp.dot(a, b, preferred_element_type=F32)


def _dot_nt(a, b):
    return lax.dot_general(a, b, (((1,), (1,)), ((), ())), preferred_element_type=F32)


def _dot_tn(a, b):
    return lax.dot_general(a, b, (((0,), (0,)), ((), ())), preferred_element_type=F32)


def _dot01(t, x, left=True):
    if not left:
        t, x = x, t
    hi = x.astype(BF16)
    r = x - hi.astype(F32)
    mid = r.astype(BF16)
    lo = (r - mid.astype(F32)).astype(BF16)
    if left:
        return _dot(t, hi) + _dot(t, mid) + _dot(t, lo)
    return _dot(hi, t) + _dot(mid, t) + _dot(lo, t)


def _sigmoid(x):
    return 1.0 / (1.0 + jnp.exp(-x))


def _log_sigmoid(x):
    return jnp.minimum(x, 0.0) - jnp.log(1.0 + jnp.exp(-jnp.abs(x)))


TILES = {
    "in_proj": (512, D_ZP // 3, None),
    "in_proj_dx": (512, 512, None),
    "in_proj_dw": (512, 2048, None),
    "out_proj": (512, 1024, None),
    "out_proj_dx": (512, 1024, None),
    "out_proj_dw": (1024, 1024, None),
}


def _matmul(a, b, mode, out_dtype, tm, tn, tk, name, n_outer=False, after=None):
    if mode == "nn":
        (m, k), n = a.shape, b.shape[1]
    elif mode == "nt":
        (m, k), n = a.shape, b.shape[0]
    else:
        (k, m), n = a.shape, b.shape[1]
    tm, tn, tk = min(tm, m), min(tn, n), k if tk is None else min(tk, k)
    assert m % tm == 0 and n % tn == 0 and k % tk == 0, (name, m, n, k)
    nk = k // tk
    dot = {"nn": _dot, "nt": _dot_nt, "tn": _dot_tn}[mode]

    follows = [] if after is None else [after]

    def body_whole_k(a_ref, b_ref, *rest):
        o_ref = rest[-1]
        o_ref[...] = dot(a_ref[...], b_ref[...]).astype(out_dtype)

    def body(a_ref, b_ref, *rest):
        o_ref, acc_ref = rest[-2:]
        kk = pl.program_id(2)

        @pl.when(kk == 0)
        def _():
            acc_ref[...] = jnp.zeros_like(acc_ref)

        acc_ref[...] += dot(a_ref[...], b_ref[...])

        @pl.when(kk == nk - 1)
        def _():
            o_ref[...] = acc_ref[...].astype(out_dtype)

    def at(index):
        return (lambda j, i, kk: index(i, j, kk)) if n_outer else index

    if mode == "tn":
        a_spec = pl.BlockSpec((tk, tm), at(lambda i, j, kk: (kk, i)))
    else:
        a_spec = pl.BlockSpec((tm, tk), at(lambda i, j, kk: (i, kk)))
    if mode == "nt":
        b_spec = pl.BlockSpec((tn, tk), at(lambda i, j, kk: (j, kk)))
    else:
        b_spec = pl.BlockSpec((tk, tn), at(lambda i, j, kk: (kk, j)))
    return pl.pallas_call(
        body_whole_k if nk == 1 else body, name=name,
        grid=(n // tn, m // tm, nk) if n_outer else (m // tm, n // tn, nk),
        in_specs=[a_spec, b_spec] + [ANY] * len(follows),
        out_specs=pl.BlockSpec((tm, tn), at(lambda i, j, kk: (i, j))),
        out_shape=jax.ShapeDtypeStruct((m, n), out_dtype),
        scratch_shapes=[] if nk == 1 else [pltpu.VMEM((tm, tn), F32)],
        compiler_params=_cparams(("parallel", "parallel", "arbitrary")),
    )(a, b, *follows)


def _matmul_cols(pieces, b, out_dtype, tm, tn, name, after=None):
    m, n = pieces[0].shape[0], b.shape[1]
    widths = [p.shape[1] for p in pieces]
    starts = [sum(widths[:i]) for i in range(len(pieces))]
    follows = [] if after is None else [after]
    tm, tn = min(tm, m), min(tn, n)
    assert sum(widths) == b.shape[0] and m % tm == 0 and n % tn == 0, name

    def body(*refs):
        b_ref, o_ref = refs[len(pieces)], refs[-1]
        acc = None
        for p_ref, at, width in zip(refs, starts, widths):
            part = _dot(p_ref[...], b_ref[at:at + width, :])
            acc = part if acc is None else acc + part
        o_ref[...] = acc.astype(out_dtype)

    return pl.pallas_call(
        body, name=name, grid=(n // tn, m // tm),
        in_specs=[pl.BlockSpec((tm, width), lambda j, i: (i, 0)) for width in widths]
        + [pl.BlockSpec((b.shape[0], tn), lambda j, i: (0, j))] + [ANY] * len(follows),
        out_specs=pl.BlockSpec((tm, tn), lambda j, i: (i, j)),
        out_shape=jax.ShapeDtypeStruct((m, n), out_dtype),
        compiler_params=_cparams(("parallel", "parallel")),
    )(*pieces, b, *follows)


def _matmul_rows(pieces, b, out_dtype, tw, tn, name):
    k, n = b.shape
    tn = min(tn, n)
    counts = [p.shape[1] // tw for p in pieces]
    firsts = [sum(counts[:i]) for i in range(len(pieces))]
    assert all(p.shape[1] % tw == 0 for p in pieces) and n % tn == 0, name

    def body(*refs):
        b_ref, o_ref = refs[len(pieces):]
        for p_ref, first, count in zip(refs, firsts, counts):
            @pl.when((pl.program_id(0) >= first) & (pl.program_id(0) < first + count))
            def _(p_ref=p_ref):
                o_ref[...] = _dot_tn(p_ref[...], b_ref[...]).astype(out_dtype)

    def piece_spec(first, count):
        return pl.BlockSpec((k, tw), lambda i, j: (0, jnp.clip(i - first, 0, count - 1)))

    return pl.pallas_call(
        body, name=name, grid=(sum(counts), n // tn),
        in_specs=[piece_spec(first, count) for first, count in zip(firsts, counts)]
        + [pl.BlockSpec((k, tn), lambda i, j: (0, j))],
        out_specs=pl.BlockSpec((tw, tn), lambda i, j: (i, j)),
        out_shape=jax.ShapeDtypeStruct((sum(counts) * tw, n), out_dtype),
        compiler_params=_cparams(("parallel", "parallel")),
    )(*pieces, b)


ROWS = 512


def _rms_fwd(x, g):
    s, d = x.shape

    def body(x_ref, g_ref, h_ref):
        xv = x_ref[...]
        r = lax.rsqrt(jnp.mean(xv * xv, axis=-1, keepdims=True) + EPS)
        h_ref[...] = (xv * r * g_ref[...]).astype(BF16)

    return pl.pallas_call(
        body, name="rms_fwd", grid=(s // ROWS,),
        in_specs=[pl.BlockSpec((ROWS, d), lambda i: (i, 0)), pl.BlockSpec((1, d), lambda i: (0, 0))],
        out_specs=pl.BlockSpec((ROWS, d), lambda i: (i, 0)),
        out_shape=jax.ShapeDtypeStruct((s, d), BF16),
        compiler_params=_cparams(("parallel",)),
    )(x, g)


def _post_fwd(x, y, g):
    s, d = x.shape

    def body(x_ref, y_ref, g_ref, o_ref):
        yv = y_ref[...]
        r = lax.rsqrt(jnp.mean(yv * yv, axis=-1, keepdims=True) + EPS)
        o_ref[...] = x_ref[...] + yv * r * g_ref[...]

    row = pl.BlockSpec((ROWS, d), lambda i: (i, 0))
    return pl.pallas_call(
        body, name="post_fwd", grid=(s // ROWS,),
        in_specs=[row, row, pl.BlockSpec((1, d), lambda i: (0, 0))],
        out_specs=row,
        out_shape=jax.ShapeDtypeStruct((s, d), F32),
        compiler_params=_cparams(("parallel",)),
    )(x, y, g)


def _loss_head(out, tgt):
    s, d = out.shape

    def body(o_ref, t_ref, dout_ref, sum_ref):
        @pl.when(pl.program_id(0) == 0)
        def _():
            sum_ref[...] = jnp.zeros_like(sum_ref)

        e = o_ref[...] - t_ref[...]
        dout_ref[...] = e * (1.0 / d)
        sum_ref[...] += jnp.sum(jnp.sum(e * e, axis=1, keepdims=True), axis=0, keepdims=True)

    row = pl.BlockSpec((ROWS, d), lambda i: (i, 0))
    return pl.pallas_call(
        body, name="loss_head", grid=(s // ROWS,),
        in_specs=[row, row],
        out_specs=[row, pl.BlockSpec((1, 1), lambda i: (0, 0))],
        out_shape=[jax.ShapeDtypeStruct((s, d), F32), jax.ShapeDtypeStruct((1, 1), F32)],
        compiler_params=_cparams(("arbitrary",)),
    )(out, tgt)


def _post_bwd(dout, y, g):
    s, d = y.shape

    def body(do_ref, y_ref, g_ref, dy_ref, dg_ref):
        @pl.when(pl.program_id(0) == 0)
        def _():
            dg_ref[...] = jnp.zeros_like(dg_ref)

        yv = y_ref[...]
        dv = do_ref[...]
        r = lax.rsqrt(jnp.mean(yv * yv, axis=-1, keepdims=True) + EPS)
        dg_ref[...] += jnp.sum(dv * yv * r, axis=0, keepdims=True)
        w = dv * g_ref[...]
        dy = r * (w - yv * (r * r) * jnp.mean(w * yv, axis=-1, keepdims=True))
        dy_ref[...] = dy.astype(BF16)

    row = pl.BlockSpec((ROWS, d), lambda i: (i, 0))
    vec = pl.BlockSpec((1, d), lambda i: (0, 0))
    return pl.pallas_call(
        body, name="post_bwd", grid=(s // ROWS,),
        in_specs=[row, row, vec],
        out_specs=[row, vec],
        out_shape=[jax.ShapeDtypeStruct((s, d), BF16), jax.ShapeDtypeStruct((1, d), F32)],
        compiler_params=_cparams(("arbitrary",)),
    )(dout, y, g)


def _pre_bwd(dh, x, g, dout):
    s, d = x.shape

    def body(dh_ref, x_ref, g_ref, do_ref, dx_ref, dg_ref):
        @pl.when(pl.program_id(0) == 0)
        def _():
            dg_ref[...] = jnp.zeros_like(dg_ref)

        xv = x_ref[...]
        dv = dh_ref[...]
        r = lax.rsqrt(jnp.mean(xv * xv, axis=-1, keepdims=True) + EPS)
        dg_ref[...] += jnp.sum(dv * xv * r, axis=0, keepdims=True)
        w = dv * g_ref[...]
        dx_ref[...] = do_ref[...] + r * (w - xv * (r * r) * jnp.mean(w * xv, axis=-1, keepdims=True))

    row = pl.BlockSpec((ROWS, d), lambda i: (i, 0))
    vec = pl.BlockSpec((1, d), lambda i: (0, 0))
    return pl.pallas_call(
        body, name="pre_bwd", grid=(s // ROWS,),
        in_specs=[row, row, vec, row],
        out_specs=[row, vec],
        out_shape=[jax.ShapeDtypeStruct((s, d), F32), jax.ShapeDtypeStruct((1, d), F32)],
        compiler_params=_cparams(("arbitrary",)),
    )(dh, x, g, dout)


GLA_STEP = 4
GLA_ROWS = GLA_STEP * CHUNK
GLA_CHUNKS = [slice(c * CHUNK, (c + 1) * CHUNK) for c in range(GLA_STEP)]


def _chunk_triangles():
    ri = lax.broadcasted_iota(jnp.int32, (GLA_ROWS, GLA_ROWS), 0)
    ci = lax.broadcasted_iota(jnp.int32, (GLA_ROWS, GLA_ROWS), 1)
    same = (ri // CHUNK) == (ci // CHUNK)
    return (jnp.where(same & (ri >= ci), 1.0, 0.0).astype(BF16), jnp.where(same & (ci >= ri), 1.0, 0.0).astype(BF16))


def _per_chunk(fn, like):
    row = lax.broadcasted_iota(jnp.int32, like.shape, 0)
    return [fn((row >= c * CHUNK) & (row < (c + 1) * CHUNK)) for c in range(GLA_STEP)]


def _spread(per_chunk, like):
    row = lax.broadcasted_iota(jnp.int32, like.shape, 0)
    out = per_chunk[-1]
    for c in reversed(range(GLA_STEP - 1)):
        out = jnp.where(row < (c + 1) * CHUNK, per_chunk[c], out)
    return out


def _gla_gate(ga_b, wa_b, b_ref, tri):
    pre = _dot(ga_b, wa_b) + b_ref[...]
    la = _log_sigmoid(pre) * (1.0 / GLA_TAU)
    return pre, _dot01(tri, la)


def _chunk_ends(cum):
    row = lax.broadcasted_iota(jnp.int32, cum.shape, 0)
    return [jnp.sum(jnp.where(row == (c + 1) * CHUNK - 1, cum, 0.0), axis=0, keepdims=True)
            for c in range(GLA_STEP)]


def _heads(width):
    return [slice(h * width, (h + 1) * width) for h in range(GLA_HEADS)]


def _z_specs_gla(rev=None):
    idx = (lambda n: n) if rev is None else rev
    return [
        pl.BlockSpec((GLA_ROWS, GLA_KW), lambda n: (idx(n), 0)),
        pl.BlockSpec((GLA_ROWS, GLA_KW), lambda n: (idx(n), 1)),
        pl.BlockSpec((GLA_ROWS, D_GLA), lambda n: (idx(n), 1)),
        pl.BlockSpec((GLA_ROWS, D_GLA), lambda n: (idx(n), 2)),
        pl.BlockSpec((GLA_ROWS, LANE), lambda n: (idx(n), OFF_GA // LANE)),
    ]


def _gla_fwd(z, wa_pad, b_alpha, g_gla):
    s = z.shape[0]
    nchunk = s // CHUNK

    def body(q_ref, k_ref, v_ref, gg_ref, ga_ref, wa_ref, b_ref, g_ref, y_ref, o_ref, st_ref, pre_ref, cum_ref,
             state):
        @pl.when(pl.program_id(0) == 0)
        def _():
            state[...] = jnp.zeros_like(state)

        ga_b = ga_ref[...].astype(BF16)
        tri, _ = _chunk_triangles()
        nh = range(GLA_HEADS)
        keys, vals = _heads(GLA_DK), _heads(GLA_DV)
        pre, cum = _gla_gate(ga_b, wa_ref[...].astype(BF16), b_ref, tri)
        pre_ref[...] = pre
        cum_ref[...] = cum
        cends = _chunk_ends(cum)
        kd_b = (k_ref[...] * jnp.exp(_spread(cends, cum) - cum)).astype(BF16)
        qs = (q_ref[...] * GLA_SCALE).astype(BF16)
        v_b = v_ref[...].astype(BF16)
        uts = [[_dot_tn(v_b[rs, vals[h]], kd_b[rs, keys[h]]) for h in nh] for rs in GLA_CHUNKS]
        sts, prev = [], [state[h] for h in nh]
        for c in range(GLA_STEP):
            a = jnp.exp(cends[c])
            prev = [prev[h] * a[:, keys[h]] + uts[c][h] for h in nh]
            sts.append(prev)
        for h in nh:
            state[h] = prev[h]
            for c in range(GLA_STEP):
                st_ref[c, h] = sts[c][h]
        outs = [[_dot_nt(qs[rs, keys[h]], sts[c][h].astype(BF16)) for h in nh] for c, rs in enumerate(GLA_CHUNKS)]
        for h in nh:
            o, vs = jnp.concatenate([outs[c][h] for c in range(GLA_STEP)], axis=0), vals[h]
            o_ref[:, vs] = o
            r = lax.rsqrt(jnp.mean(o * o, axis=-1, keepdims=True) + EPS)
            gg = gg_ref[:, vs]
            y_ref[:, vs] = (o * r * g_ref[:, vs] * (gg * _sigmoid(gg))).astype(BF16)

    full = lambda shape: pl.BlockSpec(shape, lambda n: tuple(0 for _ in shape))
    wide = pl.BlockSpec((GLA_ROWS, D_GLA), lambda n: (n, 0))
    return pl.pallas_call(
        body, name="gla_fwd", grid=(nchunk // GLA_STEP,),
        in_specs=_z_specs_gla() + [full((LANE, GLA_KW)), full((1, GLA_KW)), full((1, D_GLA))],
        out_specs=[wide, wide, pl.BlockSpec((GLA_STEP, GLA_HEADS, GLA_DV, GLA_DK), lambda n: (n, 0, 0, 0)),
                   pl.BlockSpec((GLA_ROWS, GLA_KW), lambda n: (n, 0)), pl.BlockSpec((GLA_ROWS, GLA_KW), lambda n: (n, 0))],
        out_shape=[jax.ShapeDtypeStruct((s, D_GLA), BF16), jax.ShapeDtypeStruct((s, D_GLA), F32),
                   jax.ShapeDtypeStruct((nchunk, GLA_HEADS, GLA_DV, GLA_DK), F32),
                   jax.ShapeDtypeStruct((s, GLA_KW), F32), jax.ShapeDtypeStruct((s, GLA_KW), F32)],
        scratch_shapes=[pltpu.VMEM((GLA_HEADS, GLA_DV, GLA_DK), F32)],
        compiler_params=_cparams(("arbitrary",)),
    )(z, z, z, z, z, wa_pad, b_alpha, g_gla)


def _gla_bwd(dyc, o_gla, z, wa_pad, g_gla, states, gate_pre, gate_cum):
    s = z.shape[0]
    nsteps = s // GLA_ROWS
    rev = lambda n: nsteps - 1 - n

    def body(dy_ref, o_ref, q_ref, k_ref, v_ref, gg_ref, ga_ref, wa_ref, g_ref, st_ref, stp_ref, pre_ref, cum_ref,
             dq_ref, dk_ref, dv_ref, dgg_ref, dga_ref, dwa_ref, db_ref, dg_ref, carry):
        step = pl.program_id(0)

        @pl.when(step == 0)
        def _():
            carry[...] = jnp.zeros_like(carry)
            dwa_ref[...] = jnp.zeros_like(dwa_ref)
            db_ref[...] = jnp.zeros_like(db_ref)
            dg_ref[...] = jnp.zeros_like(dg_ref)

        has_prev = (step < nsteps - 1).astype(F32)
        ga_b = ga_ref[...].astype(BF16)
        _, tri_up = _chunk_triangles()
        nh, nc = range(GLA_HEADS), range(GLA_STEP)
        keys, vals = _heads(GLA_DK), _heads(GLA_DV)
        wa_b = wa_ref[...].astype(BF16)
        pre, cum = pre_ref[...], cum_ref[...]
        cends = _chunk_ends(cum)
        e = jnp.exp(_spread(cends, cum) - cum)
        a = [jnp.exp(cends[c]) for c in nc]
        kf = k_ref[...]
        kd_b = (kf * e).astype(BF16)
        v_b = v_ref[...].astype(BF16)
        qs = (q_ref[...] * GLA_SCALE).astype(BF16)
        do_b = []
        for h in nh:
            vs = vals[h]
            o = o_ref[:, vs]
            gg = gg_ref[:, vs]
            g = g_ref[:, vs]
            dy = dy_ref[:, vs]
            r = lax.rsqrt(jnp.mean(o * o, axis=-1, keepdims=True) + EPS)
            sg = _sigmoid(gg)
            dogn = dy * (gg * sg)
            dgg_ref[:, vs] = (dy * (o * r * g) * (sg * (1.0 + gg * (1.0 - sg)))).astype(BF16)
            dg_ref[:, vs] += jnp.sum(dogn * o * r, axis=0, keepdims=True)
            w = dogn * g
            do_b.append((r * (w - o * (r * r) * jnp.mean(w * o, axis=-1, keepdims=True))).astype(BF16))
        dqs = [jnp.concatenate([_dot(do_b[h][rs], st_ref[c, h].astype(BF16)) for c, rs in enumerate(GLA_CHUNKS)],
                               axis=0) for h in nh]
        dq_ref[...] = (jnp.concatenate(dqs, axis=1) * GLA_SCALE).astype(BF16)
        own = [[_dot_tn(do_b[h][rs], qs[rs, keys[h]]) for h in nh] for rs in GLA_CHUNKS]
        gts, later = [None] * GLA_STEP, [carry[h] for h in nh]
        for c in reversed(nc):
            gts[c] = [own[c][h] + later[h] for h in nh]
            later = [gts[c][h] * a[c][:, keys[h]] for h in nh]
        for h in nh:
            carry[h] = later[h]
        gt_b = [[gts[c][h].astype(BF16) for h in nh] for c in nc]
        dkd = jnp.concatenate([jnp.concatenate([_dot(v_b[rs, vals[h]], gt_b[c][h]) for h in nh], axis=1)
                               for c, rs in enumerate(GLA_CHUNKS)], axis=0)
        dvs = [[_dot_nt(kd_b[rs, keys[h]], gt_b[c][h]) for h in nh] for c, rs in enumerate(GLA_CHUNKS)]
        before = lambda c, h: st_ref[c - 1, h] if c > 0 else stp_ref[0, h] * has_prev
        da = [jnp.concatenate([jnp.sum(gts[c][h] * before(c, h), axis=0, keepdims=True) for h in nh], axis=1)
              for c in nc]
        for h in nh:
            dv_ref[:, vals[h]] = jnp.concatenate([dvs[c][h] for c in nc], axis=0).astype(BF16)
        dk_ref[...] = (dkd * e).astype(BF16)
        dd = dkd * kf * e
        dsum = _per_chunk(lambda mine: jnp.sum(jnp.where(mine, dd, 0.0), axis=0, keepdims=True), dd)
        dcend = _spread([dsum[c] + da[c] * a[c] for c in nc], dd)
        dla = dcend - _dot01(tri_up, dd)
        dpre = dla * (1.0 / GLA_TAU) * (1.0 - _sigmoid(pre))
        dpre_b = dpre.astype(BF16)
        dga_ref[...] = _dot_nt(dpre_b, wa_b).astype(BF16)
        dwa_ref[...] += _dot_tn(ga_b, dpre_b)
        db_ref[...] += jnp.sum(dpre, axis=0, keepdims=True)

    full = lambda shape: pl.BlockSpec(shape, lambda n: tuple(0 for _ in shape))
    wide = pl.BlockSpec((GLA_ROWS, D_GLA), lambda n: (rev(n), 0))
    keyw = pl.BlockSpec((GLA_ROWS, GLA_KW), lambda n: (rev(n), 0))
    st_spec = pl.BlockSpec((GLA_STEP, GLA_HEADS, GLA_DV, GLA_DK), lambda n: (rev(n), 0, 0, 0))
    stp_spec = pl.BlockSpec((1, GLA_HEADS, GLA_DV, GLA_DK),
                            lambda n: (jnp.maximum(GLA_STEP * rev(n) - 1, 0), 0, 0, 0))
    return pl.pallas_call(
        body, name="gla_bwd", grid=(nsteps,),
        in_specs=[wide, wide] + _z_specs_gla(rev)
        + [full((LANE, GLA_KW)), full((1, D_GLA)), st_spec, stp_spec, keyw, keyw],
        out_specs=[keyw, keyw, wide, wide, pl.BlockSpec((GLA_ROWS, LANE), lambda n: (rev(n), 0)),
                   full((LANE, GLA_KW)), full((1, GLA_KW)), full((1, D_GLA))],
        out_shape=[jax.ShapeDtypeStruct((s, GLA_KW), BF16), jax.ShapeDtypeStruct((s, GLA_KW), BF16),
                   jax.ShapeDtypeStruct((s, D_GLA), BF16), jax.ShapeDtypeStruct((s, D_GLA), BF16),
                   jax.ShapeDtypeStruct((s, LANE), BF16),
                   jax.ShapeDtypeStruct((LANE, GLA_KW), F32), jax.ShapeDtypeStruct((1, GLA_KW), F32),
                   jax.ShapeDtypeStruct((1, D_GLA), F32)],
        scratch_shapes=[pltpu.VMEM((GLA_HEADS, GLA_DV, GLA_DK), F32)],
        compiler_params=_cparams(("arbitrary",)),
    )(dyc, o_gla, z, z, z, z, z, wa_pad, g_gla, states, states, gate_pre, gate_cum)


def _build_bias_table(rb_row, et_ref):
    far = jnp.broadcast_to(rb_row[:, 2 * REL_CLIP:2 * REL_CLIP + 1], (1, LANE))
    near_hi = rb_row[:, REL_CLIP:2 * REL_CLIP]
    near_lo = rb_row[:, 0:REL_CLIP]
    past = jnp.broadcast_to(rb_row[:, 0:1], (1, LANE))
    seg = [far, far, far, far, near_hi, near_lo] + [past] * (ET_ROWS // LANE - 5)
    ri = lax.broadcasted_iota(jnp.int32, (LANE, LANE), 0)
    ci = lax.broadcasted_iota(jnp.int32, (LANE, LANE), 1)
    for kb in range(ET_ROWS // LANE):
        wmat = jnp.where(ri + ci < LANE, seg[kb], seg[kb + 1])
        blk = pltpu.roll(wmat, 0, 1, stride=1, stride_axis=0)
        lag = LEFT_CHUNKS + ci // CHUNK - (2 * kb + ri // CHUNK)
        et_ref[kb * LANE:(kb + 1) * LANE, :] = jnp.where((lag >= 0) & (lag <= LEFT_CHUNKS), blk, NEG)


def _reduce_bias_table(det_ref):
    lane = lax.broadcasted_iota(jnp.int32, (1, LANE), 1)
    ri = lax.broadcasted_iota(jnp.int32, (LANE, LANE), 0)
    ci = lax.broadcasted_iota(jnp.int32, (LANE, LANE), 1)
    flip = jnp.where(ri + ci == LANE - 1, 1.0, 0.0).astype(BF16)
    segs = jnp.zeros((8, LANE), F32)
    seg_row = lax.broadcasted_iota(jnp.int32, (8, LANE), 0)
    prev_minus = jnp.zeros((1, LANE), F32)
    for kb in range(6):
        rolled = pltpu.roll(_dot01(det_ref[kb * LANE:(kb + 1) * LANE, :], flip, left=False), 0, 1,
                            stride=1, stride_axis=0)
        plus = jnp.sum(jnp.where(ci >= ri, rolled, 0.0), axis=0, keepdims=True)
        minus = jnp.sum(jnp.where(ci < ri, rolled, 0.0), axis=0, keepdims=True)
        segs = segs + jnp.where(seg_row == kb, plus + prev_minus, 0.0)
        prev_minus = minus
    segs = _dot01(segs, flip, left=False)
    pick = lambda kb: jnp.sum(jnp.where(seg_row == kb, segs, 0.0), axis=0, keepdims=True)
    far = jnp.sum(pick(0) + pick(1) + pick(2) + pick(3), axis=1, keepdims=True)
    last = jnp.where(lane == 0, far, 0.0)
    return jnp.concatenate([pick(5), pick(4), last], axis=1)


def _att_window(b):
    c0 = 2 * b
    kstart = pl.multiple_of(jnp.maximum(c0 - LEFT_CHUNKS, 0) * CHUNK, CHUNK)
    eoff = pl.multiple_of(jnp.maximum(LEFT_CHUNKS - c0, 0) * CHUNK, CHUNK)
    return kstart, eoff


def _att_probs(q_b, kw_b, et):
    st = _dot_nt(kw_b, q_b) * ATT_SCALE + et
    m = jnp.max(st, axis=0, keepdims=True)
    ex = jnp.exp(st - m)
    return ex * (1.0 / jnp.sum(ex, axis=0, keepdims=True))


def _att_fwd(z, rb_pad, g_att):
    s = z.shape[0]
    nblk = s // QB
    c_aq, c_ak, c_av, c_ag = [(OFF_AQ + i * D_ATT) // ATT_HD for i in range(4)]

    def body(q_ref, k_ref, v_ref, ag_ref, rb_ref, g_ref, y_ref, o_ref, p_ref, et_ref, kb_ref, vb_ref):
        h = pl.program_id(0)
        b = pl.program_id(1)

        @pl.when(b == 0)
        def _():
            _build_bias_table(rb_ref[pl.ds(h, 1), :], et_ref)
            kb_ref[...] = k_ref[...].astype(BF16)
            vb_ref[...] = v_ref[...].astype(BF16)

        for j in range(ATT_UNROLL):
            rs = slice(j * QB, (j + 1) * QB)
            kstart, eoff = _att_window(b * ATT_UNROLL + j)
            q_b = q_ref[rs, :].astype(BF16)
            kw_b = kb_ref[pl.ds(kstart, WIN), :]
            vw_b = vb_ref[pl.ds(kstart, WIN), :]
            pt = _att_probs(q_b, kw_b, et_ref[pl.ds(eoff, WIN), :])
            p_ref[0, j] = pt
            o = _dot_tn(pt.astype(BF16), vw_b)
            o_ref[rs, :] = o
            r = lax.rsqrt(jnp.mean(o * o, axis=-1, keepdims=True) + EPS)
            ag = ag_ref[rs, :]
            y_ref[rs, :] = (o * r * g_ref[...] * (ag * _sigmoid(ag))).astype(BF16)

    blk = lambda col: pl.BlockSpec((ATT_UNROLL * QB, ATT_HD), lambda h, b: (b, col + h))
    seq = lambda col: pl.BlockSpec((s, ATT_HD), lambda h, b: (0, col + h))
    out_blk = pl.BlockSpec((ATT_UNROLL * QB, ATT_HD), lambda h, b: (b, h))
    return pl.pallas_call(
        body, name="att_fwd", grid=(ATT_HEADS, nblk // ATT_UNROLL),
        in_specs=[blk(c_aq), seq(c_ak), seq(c_av), blk(c_ag),
                  pl.BlockSpec((ATT_HEADS, 3 * LANE), lambda h, b: (0, 0)),
                  pl.BlockSpec((1, ATT_HD), lambda h, b: (0, h))],
        out_specs=[out_blk, out_blk, pl.BlockSpec((1, ATT_UNROLL, WIN, QB), lambda h, b: (h, b, 0, 0))],
        out_shape=[jax.ShapeDtypeStruct((s, D_ATT), BF16), jax.ShapeDtypeStruct((s, D_ATT), F32),
                   jax.ShapeDtypeStruct((ATT_HEADS, nblk, WIN, QB), F32)],
        scratch_shapes=[pltpu.VMEM((ET_ROWS, LANE), F32), pltpu.VMEM((s, ATT_HD), BF16),
                        pltpu.VMEM((s, ATT_HD), BF16)],
        compiler_params=_cparams(("arbitrary", "arbitrary")),
    )(z, z, z, z, rb_pad, g_att)


def _att_bwd(dyc, o_att, probs, z, g_att):
    s = z.shape[0]
    nblk = s // QB
    c_aq, c_ak, c_av, c_ag = [(OFF_AQ + i * D_ATT) // ATT_HD for i in range(4)]
    c_dy = D_GLA // ATT_HD

    def body(dy_ref, o_ref, p_ref, q_ref, k_ref, v_ref, ag_ref, g_ref,
             dq_ref, dk_ref, dv_ref, dag_ref, drb_ref, dg_ref, det_ref, kb_ref, vb_ref, dk_acc, dv_acc):
        b = pl.program_id(1)

        @pl.when(b == 0)
        def _():
            kb_ref[...] = k_ref[...].astype(BF16)
            vb_ref[...] = v_ref[...].astype(BF16)
            det_ref[...] = jnp.zeros_like(det_ref)
            dk_acc[...] = jnp.zeros_like(dk_acc)
            dv_acc[...] = jnp.zeros_like(dv_acc)
            dg_ref[...] = jnp.zeros_like(dg_ref)

        g = g_ref[...]
        dg = jnp.zeros((1, ATT_HD), F32)
        for j in range(ATT_UNROLL):
            rs = slice(j * QB, (j + 1) * QB)
            kstart, eoff = _att_window(b * ATT_UNROLL + j)
            q_b = q_ref[rs, :].astype(BF16)
            kw_b = kb_ref[pl.ds(kstart, WIN), :]
            vw_b = vb_ref[pl.ds(kstart, WIN), :]
            pt = p_ref[0, j]
            o = o_ref[rs, :]
            ag = ag_ref[rs, :]
            dy = dy_ref[rs, :]
            r = lax.rsqrt(jnp.mean(o * o, axis=-1, keepdims=True) + EPS)
            sg = _sigmoid(ag)
            don = dy * (ag * sg)
            dag_ref[rs, :] = (dy * (o * r * g) * (sg * (1.0 + ag * (1.0 - sg)))).astype(BF16)
            dg = dg + jnp.sum(don * o * r, axis=0, keepdims=True)
            w = don * g
            do_b = (r * (w - o * (r * r) * jnp.mean(w * o, axis=-1, keepdims=True))).astype(BF16)
            pt_b = pt.astype(BF16)
            dpt = _dot_nt(vw_b, do_b)
            dst = pt * (dpt - jnp.sum(dpt * pt, axis=0, keepdims=True))
            det_ref[pl.ds(eoff, WIN), :] += dst
            ds_b = (dst * ATT_SCALE).astype(BF16)
            dq_ref[rs, :] = _dot_tn(ds_b, kw_b).astype(BF16)
            dk_acc[pl.ds(kstart, WIN), :] += _dot(ds_b, q_b)
            dv_acc[pl.ds(kstart, WIN), :] += _dot(pt_b, do_b)
        dg_ref[...] += dg

        @pl.when(b == nblk // ATT_UNROLL - 1)
        def _():
            drb_ref[0] = jnp.broadcast_to(_reduce_bias_table(det_ref), (8, 3 * LANE))
            dk_ref[...] = dk_acc[...].astype(BF16)
            dv_ref[...] = dv_acc[...].astype(BF16)

    blk = lambda col: pl.BlockSpec((ATT_UNROLL * QB, ATT_HD), lambda h, b: (b, col + h))
    seq = lambda col: pl.BlockSpec((s, ATT_HD), lambda h, b: (0, col + h))
    out_blk = pl.BlockSpec((ATT_UNROLL * QB, ATT_HD), lambda h, b: (b, h))
    out_seq = pl.BlockSpec((s, ATT_HD), lambda h, b: (0, h))
    return pl.pallas_call(
        body, name="att_bwd", grid=(ATT_HEADS, nblk // ATT_UNROLL),
        in_specs=[blk(c_dy), blk(0), pl.BlockSpec((1, ATT_UNROLL, WIN, QB), lambda h, b: (h, b, 0, 0)),
                  blk(c_aq), seq(c_ak), seq(c_av), blk(c_ag),
                  pl.BlockSpec((1, ATT_HD), lambda h, b: (0, h))],
        out_specs=[out_blk, out_seq, out_seq, out_blk,
                   pl.BlockSpec((1, 8, 3 * LANE), lambda h, b: (h, 0, 0)),
                   pl.BlockSpec((1, ATT_HD), lambda h, b: (0, h))],
        out_shape=[jax.ShapeDtypeStruct((s, D_ATT), BF16), jax.ShapeDtypeStruct((s, D_ATT), BF16),
                   jax.ShapeDtypeStruct((s, D_ATT), BF16), jax.ShapeDtypeStruct((s, D_ATT), BF16),
                   jax.ShapeDtypeStruct((ATT_HEADS, 8, 3 * LANE), F32),
                   jax.ShapeDtypeStruct((1, D_ATT), F32)],
        scratch_shapes=[pltpu.VMEM((ET_ROWS, LANE), F32),
                        pltpu.VMEM((s, ATT_HD), BF16), pltpu.VMEM((s, ATT_HD), BF16),
                        pltpu.VMEM((s, ATT_HD), F32), pltpu.VMEM((s, ATT_HD), F32)],
        compiler_params=_cparams(("arbitrary", "arbitrary")),
    )(dyc, o_att, probs, z, z, z, z, g_att)


ADAM_ROWS = 64
ADAM_COL_ROWS = 32


def _adam_math(w, g, m, v):
    m2 = ADAM_B1 * m + (1.0 - ADAM_B1) * g
    v2 = ADAM_B2 * v + (1.0 - ADAM_B2) * (g * g)
    m_hat = m2 / (1.0 - ADAM_B1 ** ADAM_STEP)
    v_hat = v2 / (1.0 - ADAM_B2 ** ADAM_STEP)
    delta = -ADAM_LR * (m_hat / (jnp.sqrt(v_hat) + ADAM_EPS) + ADAM_WD * w)
    return delta, m2, v2


def _adam_sharded(parts, first, w, m, v, name):
    nl, nr, nc = w.shape

    def body(*refs):
        p_refs = refs[:nl]
        w_ref, m_ref, v_ref, g_ref, d_ref, m2_ref, v2_ref = refs[nl:]
        for k in range(nl):
            @pl.when(pl.program_id(0) == k)
            def _(p_ref=p_refs[k]):
                g = p_ref[0].astype(F32)
                for dev in range(1, N_DEV):
                    g = g + p_ref[dev].astype(F32)
                delta, m2, v2 = _adam_math(w_ref[0], g, m_ref[0], v_ref[0])
                g_ref[0] = g
                d_ref[0] = delta
                m2_ref[0] = m2
                v2_ref[0] = v2

    def part_spec(k):
        return pl.BlockSpec((N_DEV, ADAM_ROWS, nc), lambda l, i: (0, first + jnp.where(l == k, i, 0), 0))

    blk = pl.BlockSpec((1, ADAM_ROWS, nc), lambda l, i: (l, i, 0))
    shp = jax.ShapeDtypeStruct(w.shape, F32)
    return pl.pallas_call(
        body, name=name, grid=(nl, pl.cdiv(nr, ADAM_ROWS)),
        in_specs=[part_spec(k) for k in range(nl)] + [blk, blk, blk],
        out_specs=[blk, blk, blk, blk],
        out_shape=[shp, shp, shp, shp],
        compiler_params=_cparams(("arbitrary", "arbitrary")),
    )(*parts, w, m, v)


def _adam_columns(parts, first, w, m, v):
    nc, nl, d = w.shape

    def body(*refs):
        p_refs = refs[:nl]
        w_ref, m_ref, v_ref, g_ref, d_ref, m2_ref, v2_ref = refs[nl:]
        for l in range(nl):
            g = p_refs[l][0].astype(F32)
            for slot in range(1, parts[l].shape[0]):
                g = g + p_refs[l][slot].astype(F32)
            delta, m2, v2 = _adam_math(w_ref[:, l, :], g, m_ref[:, l, :], v_ref[:, l, :])
            g_ref[:, l, :] = g
            d_ref[:, l, :] = delta
            m2_ref[:, l, :] = m2
            v2_ref[:, l, :] = v2

    blk = pl.BlockSpec((ADAM_COL_ROWS, nl, d), lambda i: (i, 0, 0))
    shp = jax.ShapeDtypeStruct(w.shape, F32)
    return pl.pallas_call(
        body, name="adam_w_in", grid=(pl.cdiv(nc, ADAM_COL_ROWS),),
        in_specs=[pl.BlockSpec((p.shape[0], ADAM_COL_ROWS, d), lambda i: (0, first + i, 0)) for p in parts]
        + [blk, blk, blk],
        out_specs=[blk, blk, blk, blk],
        out_shape=[shp, shp, shp, shp],
        compiler_params=_cparams(("parallel",)),
    )(*parts, w, m, v)


def _adam_small(ws, gs, ms, vs):
    n = len(ws)

    def body(*refs):
        w_refs, g_refs, m_refs, v_refs, d_refs, m2_refs, v2_refs = [refs[i * n:(i + 1) * n] for i in range(7)]
        for i in range(n):
            delta, m2, v2 = _adam_math(w_refs[i][...], g_refs[i][...], m_refs[i][...], v_refs[i][...])
            d_refs[i][...] = delta
            m2_refs[i][...] = m2
            v2_refs[i][...] = v2

    shapes = [jax.ShapeDtypeStruct(w.shape, F32) for w in ws]
    out = pl.pallas_call(body, name="adam_small", out_shape=shapes * 3)(*ws, *gs, *ms, *vs)
    return out[:n], out[n:2 * n], out[2 * n:]


def _position():
    return lax.axis_index("x"), lax.axis_index("y"), lax.axis_index("c")


def _slot(p):
    return 4 * p[0] + 2 * p[1] + p[2]


BF16_TILE_ROWS = 16


def _slab_rows(rows, cols):
    return -(-(rows + cols) // BF16_TILE_ROWS) * BF16_TILE_ROWS


RELAYOUT_COLS = 1024
RELAYOUT_CHUNK = 64


def _shard_pieces(dev, rows, cols):
    moved = ((0, GA_ORIG, 0), (GA_ORIG, GA_ORIG + GLA_RANK, OFF_GA - GA_ORIG), (GA_ORIG + GLA_RANK, D_IN, -GLA_RANK))
    c0, c1 = dev * cols, (dev + 1) * cols
    return [(rows + max(c0, lo) - c0, max(c0, lo) + off, min(c1, hi) - max(c0, lo))
            for lo, hi, off in moved if max(c0, lo) < min(c1, hi)]


def _move_rows(src, src_row, dst, dst_row, n):
    assert src_row % 2 == 0 and dst_row % 2 == 0 and n % 2 == 0
    for r in range(0, n // 2, RELAYOUT_CHUNK):
        m = min(RELAYOUT_CHUNK, n // 2 - r)
        dst[dst_row // 2 + r:dst_row // 2 + r + m, :] = src[src_row // 2 + r:src_row // 2 + r + m, :]


def _aligned_weight(land, rows, cols):
    _, slab, d = land.shape
    ct = min(RELAYOUT_COLS, d)

    def body(land_ref, wt_ref, wo_ref):
        dev = pl.program_id(1)
        src = land_ref.bitcast(jnp.uint32)
        dst = wt_ref.bitcast(jnp.uint32)
        wo_ref[...] = land_ref[0:rows, :]

        @pl.when(dev == 0)
        def _():
            dst[D_IN // 2:D_ZP // 2, :] = jnp.zeros(((D_ZP - D_IN) // 2, ct), jnp.uint32)

        for k in range(N_DEV):
            @pl.when(dev == k)
            def _(k=k):
                for at, to, n in _shard_pieces(k, rows, cols):
                    _move_rows(src, at, dst, to, n)

    return pl.pallas_call(
        body, name="aligned_weight", grid=(d // ct, N_DEV),
        in_specs=[pl.BlockSpec((slab, ct), lambda c, dev: (dev, c))],
        out_specs=[pl.BlockSpec((D_ZP, ct), lambda c, dev: (0, c)),
                   pl.BlockSpec((rows, ct), lambda c, dev: (dev, c))],
        out_shape=[jax.ShapeDtypeStruct((D_ZP, d), land.dtype),
                   jax.ShapeDtypeStruct((N_DEV * rows, d), land.dtype)],
        compiler_params=_cparams(("parallel", "arbitrary")),
    )(land.reshape(N_DEV * slab, d))


def _partial_slabs(dwt, cols, by_core=False):
    d = dwt[0].shape[1]
    bounds = (0, GA_ORIG, OFF_GA, D_ZP)
    assert tuple(a.shape[0] for a in dwt) == tuple(hi - lo for lo, hi in zip(bounds, bounds[1:]))
    slab = _slab_rows(0, cols)
    ct = min(RELAYOUT_COLS, d)

    def body(*refs):
        out_ref = refs[-1]
        dev = pl.program_id(1)
        srcs = [ref.bitcast(jnp.uint32) for ref in refs[:-1]]
        dst = out_ref.bitcast(jnp.uint32)
        dst[cols // 2:slab // 2, :] = jnp.zeros(((slab - cols) // 2, ct), jnp.uint32)
        for k in range(N_DEV):
            @pl.when(dev == k)
            def _(k=k):
                for to, at, n in _shard_pieces(k, 0, cols):
                    which = max(i for i, lo in enumerate(bounds[:-1]) if lo <= at)
                    assert at + n <= bounds[which + 1]
                    _move_rows(srcs[which], at - bounds[which], dst, to, n)

    place = (lambda dev: (dev % 2) * (N_DEV // 2) + dev // 2) if by_core else (lambda dev: dev)
    out = pl.pallas_call(
        body, name="partial_slabs", grid=(d // ct, N_DEV),
        in_specs=[pl.BlockSpec((a.shape[0], ct), lambda c, dev: (0, c)) for a in dwt],
        out_specs=pl.BlockSpec((slab, ct), lambda c, dev: (place(dev), c)),
        out_shape=jax.ShapeDtypeStruct((N_DEV * slab, d), dwt[0].dtype),
        compiler_params=_cparams(("parallel", "arbitrary")),
    )(*dwt)
    return out.reshape((2, N_DEV // 2, slab, d) if by_core else (N_DEV, slab, d))


def _pair_sum(mine, theirs):
    _, nchip, slab, d = mine.shape
    rows = next(r for r in range(512, 0, -BF16_TILE_ROWS) if slab % r == 0)

    def body(m_ref, t_ref, o_ref):
        south = lax.axis_index("c") == 0
        own = jnp.where(south, m_ref[0, 0], m_ref[1, 0]).astype(F32)
        got = jnp.where(south, t_ref[1, 0], t_ref[0, 0]).astype(F32)
        o_ref[0] = (own + got).astype(o_ref.dtype)

    both = pl.BlockSpec((2, 1, rows, d), lambda j, i: (0, j, i, 0))
    return pl.pallas_call(
        body, name="pair_sum", grid=(nchip, slab // rows),
        in_specs=[both, both],
        out_specs=pl.BlockSpec((1, rows, d), lambda j, i: (j, i, 0)),
        out_shape=jax.ShapeDtypeStruct((nchip, slab, d), mine.dtype),
        compiler_params=_cparams(("parallel", "parallel")),
    )(mine, theirs)


def _peer(pos, k):
    x, y, c = pos
    return (1 - x if k & 4 else x, 1 - y if k & 2 else y, 1 - c if k & 1 else c)


HBM_SPEC = pl.BlockSpec(memory_space=pltpu.HBM)
SEM_SPEC = pl.BlockSpec(memory_space=pltpu.SEMAPHORE)
GATHER_PEERS = (1, 4, 2, 6)
ALL_PEERS = (1, 2, 3, 4, 5, 6, 7)


def _hbm(a):
    return pltpu.with_memory_space_constraint(a, pltpu.HBM)


BY_DEVICE = (_slot, N_DEV)
BY_CORE = (lambda p: p[2], 2)
BY_CHIP = (lambda p: 2 * p[0] + p[1], 4)


def _split_copies(src_ref, land_ref, send_sems, recv_sems, ks, per_peer, landed, slots):
    slot_of = slots[0]
    me = _position()
    out = []
    for i, k in enumerate(ks):
        peer = _peer(me, k)
        src = src_ref.at[slot_of(peer)] if per_peer else src_ref
        dst = land_ref.at[slot_of(peer) if landed else slot_of(me)]
        out.append(pltpu.make_async_remote_copy(
            src_ref=src, dst_ref=dst, send_sem=send_sems.at[i], recv_sem=recv_sems.at[i],
            device_id=peer, device_id_type=MESH))
    return out


def _exchange_start(src, after, ks, per_peer, name, slots=BY_DEVICE):
    slab = src.shape[1:] if per_peer else src.shape
    land_shape = (slots[1],) + tuple(slab)
    n = len(ks)

    def body(src_ref, land_ref, after_ref, send_sems, recv_sems, src_thru, land_thru, token):
        for cp in _split_copies(src_ref, land_ref, send_sems, recv_sems, ks, per_peer, False, slots):
            cp.start()
        token[...] = jnp.zeros_like(token)

    return pl.pallas_call(
        body, name=name,
        out_shape=(pltpu.SemaphoreType.DMA((n,)), pltpu.SemaphoreType.DMA((n,)),
                   pltpu.HBM(src.shape, src.dtype), pltpu.HBM(land_shape, src.dtype),
                   jax.ShapeDtypeStruct((8, LANE), F32)),
        in_specs=(HBM_SPEC, HBM_SPEC, ANY),
        out_specs=(SEM_SPEC, SEM_SPEC, HBM_SPEC, HBM_SPEC, pl.BlockSpec(memory_space=pltpu.VMEM)),
        input_output_aliases={0: 2, 1: 3},
        compiler_params=pltpu.CompilerParams(has_side_effects=pltpu.SideEffectType.DATAFLOW_SIDE_EFFECTING),
    )(_hbm(src), _hbm(lax.empty(land_shape, src.dtype)), after)


def _exchange_wait(started, after, ks, per_peer, name, slots=BY_DEVICE):
    send_sems, recv_sems, src_thru, land_thru = started

    def body(src_ref, land_ref, send_sems, recv_sems, after_ref, src_dead, land_out):
        for cp in _split_copies(src_ref, land_ref, send_sems, recv_sems, ks, per_peer, True, slots):
            cp.wait_send()
            cp.wait_recv()

    return pl.pallas_call(
        body, name=name,
        out_shape=(pltpu.HBM(src_thru.shape, src_thru.dtype), pltpu.HBM(land_thru.shape, land_thru.dtype)),
        in_specs=(HBM_SPEC, HBM_SPEC, SEM_SPEC, SEM_SPEC, ANY), out_specs=(HBM_SPEC, HBM_SPEC),
        input_output_aliases={0: 0, 1: 1},
        compiler_params=pltpu.CompilerParams(has_side_effects=pltpu.SideEffectType.DATAFLOW_SIDE_EFFECTING),
    )(src_thru, land_thru, send_sems, recv_sems, after)


def _relay_copies(land_ref, send_sems, recv_sems, landed):
    me = _position()
    sibling = _peer(me, 1)
    out = []
    for i, k in enumerate(GATHER_PEERS[1:]):
        blk = land_ref.at[_slot(_peer(sibling if landed else me, k))]
        out.append(pltpu.make_async_remote_copy(
            src_ref=blk, dst_ref=blk, send_sem=send_sems.at[i], recv_sem=recv_sems.at[i],
            device_id=sibling, device_id_type=MESH))
    return out


def _relay_start(land, name):
    n = len(GATHER_PEERS) - 1

    def body(land_ref, send_sems, recv_sems, land_thru, token):
        for cp in _relay_copies(land_ref, send_sems, recv_sems, landed=False):
            cp.start()
        token[...] = jnp.zeros_like(token)

    return pl.pallas_call(
        body, name=name,
        out_shape=(pltpu.SemaphoreType.DMA((n,)), pltpu.SemaphoreType.DMA((n,)),
                   pltpu.HBM(land.shape, land.dtype), jax.ShapeDtypeStruct((8, LANE), F32)),
        in_specs=(HBM_SPEC,),
        out_specs=(SEM_SPEC, SEM_SPEC, HBM_SPEC, pl.BlockSpec(memory_space=pltpu.VMEM)),
        input_output_aliases={0: 2},
        compiler_params=pltpu.CompilerParams(has_side_effects=pltpu.SideEffectType.DATAFLOW_SIDE_EFFECTING),
    )(_hbm(land))


def _relay_wait(started, after, name):
    send_sems, recv_sems, land_thru = started

    def body(land_ref, send_sems, recv_sems, after_ref, land_out):
        for cp in _relay_copies(land_ref, send_sems, recv_sems, landed=True):
            cp.wait_send()
            cp.wait_recv()

    return pl.pallas_call(
        body, name=name,
        out_shape=pltpu.HBM(land_thru.shape, land_thru.dtype),
        in_specs=(HBM_SPEC, SEM_SPEC, SEM_SPEC, ANY), out_specs=HBM_SPEC,
        input_output_aliases={0: 0},
        compiler_params=pltpu.CompilerParams(has_side_effects=pltpu.SideEffectType.DATAFLOW_SIDE_EFFECTING),
    )(land_thru, send_sems, recv_sems, after)


def _share(vec, name, after=None):
    follows = [] if after is None else [after]

    def body(vec_ref, *rest):
        out_ref, send_sems, recv_sems, local_sem = rest[len(follows):]
        me = _position()

        def copy(k, landed):
            peer = _peer(me, k)
            return pltpu.make_async_remote_copy(
                src_ref=vec_ref, dst_ref=out_ref.at[_slot(peer) if landed else _slot(me)],
                send_sem=send_sems.at[k - 1], recv_sem=recv_sems.at[k - 1], device_id=peer, device_id_type=MESH)

        mine = pltpu.make_async_copy(vec_ref, out_ref.at[_slot(me)], local_sem)
        mine.start()
        sent = [copy(k, False) for k in ALL_PEERS]
        for cp in sent:
            cp.start()
        for k in ALL_PEERS:
            copy(k, True).wait_recv()
        for cp in sent:
            cp.wait_send()
        mine.wait()

    return pl.pallas_call(
        body, name=name,
        in_specs=[ANY] * (1 + len(follows)), out_specs=ANY,
        out_shape=jax.ShapeDtypeStruct((N_DEV,) + vec.shape, vec.dtype),
        scratch_shapes=[pltpu.SemaphoreType.DMA((N_DEV - 1,)), pltpu.SemaphoreType.DMA((N_DEV - 1,)),
                        pltpu.SemaphoreType.DMA],
    )(vec, *follows)


def _sum_slots(parts):
    def body(p_ref, o_ref):
        acc = p_ref[0]
        for dev in range(1, N_DEV):
            acc = acc + p_ref[dev]
        o_ref[...] = acc

    return pl.pallas_call(body, name="sum_slots",
                          out_shape=jax.ShapeDtypeStruct(parts.shape[1:], F32))(parts)


PACK_ROWS = 8


def _packed_rows(size):
    return -(-size // (PACK_ROWS * LANE)) * PACK_ROWS


def _pack(arrs):
    def rows(a):
        flat = a.reshape(-1)
        return jnp.pad(flat, (0, _packed_rows(flat.shape[0]) * LANE - flat.shape[0])).reshape(-1, LANE)

    return jnp.concatenate([rows(a) for a in arrs], axis=0)


def _unpack(packed, shapes):
    out, at = [], 0
    for shp in shapes:
        size = 1
        for dim in shp:
            size *= dim
        nrows = _packed_rows(size)
        out.append(packed[at:at + nrows].reshape(-1)[:size].reshape(shp))
        at += nrows
    return out


def _layer_fwd(x, wt, wo, g_pre, g_post, wa_pad, b_alpha, g_gla, g_att, rb_pad, midway=None):
    h = _rms_fwd(x, g_pre)
    z = _matmul(h, wt, "nt", F32, *TILES["in_proj"], "in_proj", n_outer=True)
    y_gla, o_gla, *gla_kept = _gla_fwd(z, wa_pad, b_alpha, g_gla)
    if midway is not None:
        g_att = g_att + midway(y_gla)[:1, :1]
    y_att, o_att, probs = _att_fwd(z, rb_pad, g_att)
    y = _matmul_cols([y_gla, y_att], wo, F32, *TILES["out_proj"][:2], "out_proj")
    out = _post_fwd(x, y, g_post)
    return out, (x, h, z, o_gla, gla_kept, o_att, probs, y_gla, y_att, y)


def _layer_bwd(dout, saved, wt, wo, g_pre, g_post, wa_pad, b_alpha, g_gla, g_att, rb_pad, on_dwo, on_dwt):
    x, h, z, o_gla, gla_kept, o_att, probs, y_gla, y_att, y = saved
    dy, dg_post = _post_bwd(dout, y, g_post)
    dwo = _matmul_rows([y_gla, y_att], dy, BF16, *TILES["out_proj_dw"][:2], "out_proj_dw")
    token = on_dwo(dwo)
    dycat = _matmul(dy, wo, "nt", F32, *TILES["out_proj_dx"], "out_proj_dx", n_outer=True, after=token)
    dq, dk, dv, dgg, dga, dwa, db, dg_gla = _gla_bwd(dycat, o_gla, z, wa_pad, g_gla, *gla_kept)
    daq, dak, dav, dag, drb, dg_att = _att_bwd(dycat, o_att, probs, z, g_att)
    tw, tn = TILES["in_proj_dw"][:2]
    dwt = (_matmul_rows([dq, dk, dv, dgg], h, BF16, tw, tn, "in_proj_dw_gla"),
           _matmul_rows([daq, dak, dav, dag], h, BF16, tw, tn, "in_proj_dw_att"),
           _matmul_rows([dga], h, BF16, LANE, tn, "in_proj_dw_gate"))
    token = on_dwt(dwt)
    dh = _matmul_cols([dq, dk, dv, dgg, daq, dak, dav, dag, dga], wt, F32, *TILES["in_proj_dx"][:2],
                      "in_proj_dx", after=token)
    dx, dg_pre = _pre_bwd(dh, x, g_pre, dout)
    small = (dg_pre[0], dg_post[0], dwa[:GLA_RANK], db[0], dg_gla[0], dg_att[0], drb[:, 0, :N_REL])
    return dx, small


def kernel(x, w_in, w_out, g_pre, g_post, w_alpha, b_alpha, g_gla, g_att, rel_bias, loss_target, m_w_in, m_w_out, m_g_pre, m_g_post, m_w_alpha, m_b_alpha, m_g_gla, m_g_att, m_rel_bias, v_w_in, v_w_out, v_g_pre, v_g_post, v_w_alpha, v_b_alpha, v_g_gla, v_g_att, v_rel_bias):
    nl, d, cols = w_in.shape
    rows = w_out.shape[1]
    s = x.shape[1]
    x0 = x.reshape(s, d)
    tgt = loss_target.reshape(s, d)

    cols_first = lambda a: jnp.transpose(a, (2, 0, 1))
    w_c = cols_first(w_in)
    slab = _slab_rows(rows, cols)
    is_out = lax.broadcasted_iota(jnp.int32, (slab, d), 0) < rows

    def shard(l, zero=0.0):
        top = jnp.pad((w_out[l] + zero).astype(BF16), ((0, slab - rows), (0, 0)))
        rest = jnp.pad((w_c[:, l] + zero).astype(BF16), ((rows, slab - rows - cols), (0, 0)))
        return jnp.where(is_out, top, rest)

    first_fetch = _exchange_start(shard(0), x, GATHER_PEERS, False, "gather_start_0")
    began = first_fetch[4][0, 0]
    shards = [None] + [shard(l, began) for l in range(1, nl)]
    alpha = _pack([w_alpha]) + began
    wa_g = _share(alpha, "gather_alpha")
    wa_cols = w_alpha.shape[2]
    wa_full = wa_g.reshape(N_DEV, -1)[:, :nl * GLA_RANK * wa_cols].reshape(N_DEV, nl, GLA_RANK, wa_cols)
    wa_full = jnp.transpose(wa_full, (1, 2, 0, 3)).reshape(nl, GLA_RANK, GLA_KW)
    wa_pad = jnp.pad(wa_full, ((0, 0), (0, LANE - GLA_RANK), (0, 0)))
    rb_pad = jnp.pad(rel_bias, ((0, 0), (0, 0), (0, 3 * LANE - N_REL)))

    def layer_args(l, follows=None):
        gp = g_pre[l:l + 1] if follows is None else g_pre[l:l + 1] + follows[:1, :1]
        return (wts[l], wos[l], gp, g_post[l:l + 1], wa_pad[l], b_alpha[l:l + 1], g_gla[l:l + 1],
                g_att[l:l + 1], rb_pad[l])

    my = _slot(_position())

    def fetch(l, after):
        return _exchange_start(shards[l], after, GATHER_PEERS, False, f"gather_start_{l}")

    def relay(l, first_hop, after):
        own[l], land = _exchange_wait(first_hop[:4], after, GATHER_PEERS, False, f"gather_wait_{l}")
        return _relay_start(land, f"relay_start_{l}")

    def midway(l, y):
        flight["relay"] = relay(l + 1, flight["fetch"], y)
        if l + 2 >= nl:
            return flight["relay"][3]
        flight["fetch"] = fetch(l + 2, flight["relay"][2])
        return flight["fetch"][4]

    act, saved, wts, wos, flight, own = x0, [], [], [], {}, [None] * nl
    prepared = (wa_pad[0, :1, :1] + sum(sh[:1, :1].astype(F32) for sh in shards[1:]))
    flight["relay"] = relay(0, first_fetch, prepared)
    if nl > 1:
        flight["fetch"] = fetch(1, flight["relay"][2])
    for l in range(nl):
        land = _relay_wait(flight["relay"][:3], act, f"relay_wait_{l}")
        land = lax.dynamic_update_slice_in_dim(land, own[l][None], my, 0)
        wt_l, wo_l = _aligned_weight(land, rows, cols)
        wts.append(wt_l)
        wos.append(wo_l)
        act, sv = _layer_fwd(act, *layer_args(l, follows=first_fetch[4] if l == 0 else None),
                             midway=functools.partial(midway, l) if l + 1 < nl else None)
        saved.append(sv)
    dout, sq = _loss_head(act, tgt)
    loss = lax.psum(sq[0, 0] * (0.5 / d), ("x", "y", "c"))

    smalls, pending_out, pending_in = [None] * nl, [None] * nl, [None] * nl

    def send_out(l, dwo):
        pending_out[l] = _exchange_start(dwo.reshape(N_DEV, rows, d), dwo[:1, :1], ALL_PEERS, True,
                                         f"scatter_out_start_{l}")
        return pending_out[l][4]

    def send_in(l, dwt):
        if l > 0:
            pending_in[l] = _exchange_start(_partial_slabs(dwt, cols), dwt[-1], ALL_PEERS, True,
                                            f"scatter_in_start_{l}")
            return pending_in[l][4]
        pending_in[l] = _exchange_start(_partial_slabs(dwt, cols, by_core=True), dwt[-1], (1,), True,
                                        "pair_start_0", slots=BY_CORE)
        return pending_in[l][4]

    for l in reversed(range(nl)):
        dout, smalls[l] = _layer_bwd(dout, saved[l], *layer_args(l), on_dwo=functools.partial(send_out, l),
                                     on_dwt=functools.partial(send_in, l))
    grad_x = dout.reshape(x.shape)
    by_core, from_sibling = _exchange_wait(pending_in[0][:4], dout, (1,), True, "pair_wait_0", slots=BY_CORE)
    pending_in[0] = _exchange_start(_pair_sum(by_core, from_sibling), dout, GATHER_PEERS[1:], True,
                                    "scatter_in_start_0", slots=BY_CHIP)

    def landed(started, after, name, ks=ALL_PEERS, slots=BY_DEVICE):
        partial, land = _exchange_wait(started[:4], after, ks, True, name, slots=slots)
        mine = slots[0](_position())
        return lax.dynamic_update_slice_in_dim(land, lax.dynamic_slice_in_dim(partial, mine, 1, 0), mine, 0)

    parts_out = [landed(pending_out[l], dout, f"scatter_out_wait_{l}") for l in range(nl)]
    g_w_out, d_w_out, m2_w_out, v2_w_out = _adam_sharded(parts_out, 0, w_out, m_w_out, v_w_out, "adam_w_out")
    names = 7
    small_stacked = [jnp.stack([smalls[l][i] for l in range(nl)]) for i in range(names)]
    shapes = [a.shape for a in small_stacked]
    gathered = _share(_pack(small_stacked), "gather_small_grads", after=d_w_out)
    g_pre_g, g_post_g, wa_g_full, b_g, gla_g, att_g, rb_g = _unpack(_sum_slots(gathered), shapes)
    wa_g_mine = lax.dynamic_slice_in_dim(wa_g_full, my * wa_cols, wa_cols, axis=2)
    grads = [g_pre_g, g_post_g, wa_g_mine, b_g, gla_g, att_g, rb_g]
    ws = [g_pre, g_post, w_alpha, b_alpha, g_gla, g_att, rel_bias]
    ms = [m_g_pre, m_g_post, m_w_alpha, m_b_alpha, m_g_gla, m_g_att, m_rel_bias]
    vs = [v_g_pre, v_g_post, v_w_alpha, v_b_alpha, v_g_gla, v_g_att, v_rel_bias]
    d_s, m2_s, v2_s = _adam_small(ws, grads, ms, vs)

    parts_in = [landed(pending_in[0], d_s[0], "scatter_in_wait_0", GATHER_PEERS[1:], BY_CHIP)]
    parts_in += [landed(pending_in[l], d_s[0], f"scatter_in_wait_{l}") for l in range(1, nl)]
    g_w_in, d_w_in, m2_w_in, v2_w_in = [
        jnp.transpose(a, (1, 2, 0))
        for a in _adam_columns(parts_in, 0, w_c, cols_first(m_w_in), cols_first(v_w_in))]

    def ordered(big_in, big_out, small):
        return [big_in, big_out] + list(small)

    return (loss, grad_x,
            *ordered(g_w_in, g_w_out, grads),
            *ordered(d_w_in, d_w_out, d_s),
            *ordered(m2_w_in, m2_w_out, m2_s),
            *ordered(v2_w_in, v2_w_out, v2_s))
```

```python
import functools

import jax
import jax.numpy as jnp
from jax import lax
from jax.experimental import pallas as pl
from jax.experimental.pallas import tpu as pltpu

F32 = jnp.float32
BF16 = jnp.bfloat16
MESH = pl.DeviceIdType.MESH
ANY = pl.BlockSpec(memory_space=pl.ANY)

CHUNK = 64
GLA_HEADS = 4
GLA_DK = 128
GLA_DV = 256
GLA_KW = GLA_HEADS * GLA_DK
D_GLA = GLA_HEADS * GLA_DV
GLA_RANK = 16
GLA_TAU = 16.0
ATT_HEADS = 8
ATT_HD = 128
D_ATT = ATT_HEADS * ATT_HD
LEFT_CHUNKS = 8
REL_CLIP = 128
N_REL = 2 * REL_CLIP + 1
EPS = 1e-6
D_IN = 2 * GLA_KW + 2 * D_GLA + GLA_RANK + 4 * D_ATT
GLA_SCALE = GLA_DK ** -0.5
ATT_SCALE = ATT_HD ** -0.5

ADAM_LR = 0.001
ADAM_B1 = 0.9
ADAM_B2 = 0.999
ADAM_EPS = 1e-08
ADAM_WD = 0.01
ADAM_STEP = 10

N_DEV = 8
LANE = 128
GA_ORIG = 2 * GLA_KW + 2 * D_GLA
OFF_AQ = GA_ORIG
OFF_GA = GA_ORIG + 4 * D_ATT
D_ZP = OFF_GA + LANE
QB = 2 * CHUNK
ATT_UNROLL = 8
WIN = (LEFT_CHUNKS + 2) * CHUNK
ET_ROWS = WIN + LEFT_CHUNKS * CHUNK
NEG = -1e30
VMEM_LIMIT = 48 * 1024 * 1024


def _cparams(sem):
    return pltpu.CompilerParams(dimension_semantics=sem, vmem_limit_bytes=VMEM_LIMIT)


def _dot(a, b):
    return jnp.dot(a, b, preferred_element_type=F32)


def _dot_nt(a, b):
    return lax.dot_general(a, b, (((1,), (1,)), ((), ())), preferred_element_type=F32)


def _dot_tn(a, b):
    return lax.dot_general(a, b, (((0,), (0,)), ((), ())), preferred_element_type=F32)


def _dot01(t, x, left=True):
    if not left:
        t, x = x, t
    hi = x.astype(BF16)
    r = x - hi.astype(F32)
    mid = r.astype(BF16)
    lo = (r - mid.astype(F32)).astype(BF16)
    if left:
        return _dot(t, hi) + _dot(t, mid) + _dot(t, lo)
    return _dot(hi, t) + _dot(mid, t) + _dot(lo, t)


def _sigmoid(x):
    return 1.0 / (1.0 + jnp.exp(-x))


def _log_sigmoid(x):
    return jnp.minimum(x, 0.0) - jnp.log(1.0 + jnp.exp(-jnp.abs(x)))


TILES = {
    "in_proj": (512, D_ZP // 3, None),
    "in_proj_dx": (512, 512, None),
    "in_proj_dw": (512, 2048, None),
    "out_proj": (512, 1024, None),
    "out_proj_dx": (512, 1024, None),
    "out_proj_dw": (1024, 1024, None),
}


def _matmul(a, b, mode, out_dtype, tm, tn, tk, name, n_outer=False, after=None):
    if mode == "nn":
        (m, k), n = a.shape, b.shape[1]
    elif mode == "nt":
        (m, k), n = a.shape, b.shape[0]
    else:
        (k, m), n = a.shape, b.shape[1]
    tm, tn, tk = min(tm, m), min(tn, n), k if tk is None else min(tk, k)
    assert m % tm == 0 and n % tn == 0 and k % tk == 0, (name, m, n, k)
    nk = k // tk
    dot = {"nn": _dot, "nt": _dot_nt, "tn": _dot_tn}[mode]

    follows = [] if after is None else [after]

    def body_whole_k(a_ref, b_ref, *rest):
        o_ref = rest[-1]
        o_ref[...] = dot(a_ref[...], b_ref[...]).astype(out_dtype)

    def body(a_ref, b_ref, *rest):
        o_ref, acc_ref = rest[-2:]
        kk = pl.program_id(2)

        @pl.when(kk == 0)
        def _():
            acc_ref[...] = jnp.zeros_like(acc_ref)

        acc_ref[...] += dot(a_ref[...], b_ref[...])

        @pl.when(kk == nk - 1)
        def _():
            o_ref[...] = acc_ref[...].astype(out_dtype)

    def at(index):
        return (lambda j, i, kk: index(i, j, kk)) if n_outer else index

    if mode == "tn":
        a_spec = pl.BlockSpec((tk, tm), at(lambda i, j, kk: (kk, i)))
    else:
        a_spec = pl.BlockSpec((tm, tk), at(lambda i, j, kk: (i, kk)))
    if mode == "nt":
        b_spec = pl.BlockSpec((tn, tk), at(lambda i, j, kk: (j, kk)))
    else:
        b_spec = pl.BlockSpec((tk, tn), at(lambda i, j, kk: (kk, j)))
    return pl.pallas_call(
        body_whole_k if nk == 1 else body, name=name,
        grid=(n // tn, m // tm, nk) if n_outer else (m // tm, n // tn, nk),
        in_specs=[a_spec, b_spec] + [ANY] * len(follows),
        out_specs=pl.BlockSpec((tm, tn), at(lambda i, j, kk: (i, j))),
        out_shape=jax.ShapeDtypeStruct((m, n), out_dtype),
        scratch_shapes=[] if nk == 1 else [pltpu.VMEM((tm, tn), F32)],
        compiler_params=_cparams(("parallel", "parallel", "arbitrary")),
    )(a, b, *follows)


def _matmul_cols(pieces, b, out_dtype, tm, tn, name, after=None):
    m, n = pieces[0].shape[0], b.shape[1]
    widths = [p.shape[1] for p in pieces]
    starts = [sum(widths[:i]) for i in range(len(pieces))]
    follows = [] if after is None else [after]
    tm, tn = min(tm, m), min(tn, n)
    assert sum(widths) == b.shape[0] and m % tm == 0 and n % tn == 0, name

    def body(*refs):
        b_ref, o_ref = refs[len(pieces)], refs[-1]
        acc = None
        for p_ref, at, width in zip(refs, starts, widths):
            part = _dot(p_ref[...], b_ref[at:at + width, :])
            acc = part if acc is None else acc + part
        o_ref[...] = acc.astype(out_dtype)

    return pl.pallas_call(
        body, name=name, grid=(n // tn, m // tm),
        in_specs=[pl.BlockSpec((tm, width), lambda j, i: (i, 0)) for width in widths]
        + [pl.BlockSpec((b.shape[0], tn), lambda j, i: (0, j))] + [ANY] * len(follows),
        out_specs=pl.BlockSpec((tm, tn), lambda j, i: (i, j)),
        out_shape=jax.ShapeDtypeStruct((m, n), out_dtype),
        compiler_params=_cparams(("parallel", "parallel")),
    )(*pieces, b, *follows)


def _matmul_rows(pieces, b, out_dtype, tw, tn, name):
    k, n = b.shape
    tn = min(tn, n)
    counts = [p.shape[1] // tw for p in pieces]
    firsts = [sum(counts[:i]) for i in range(len(pieces))]
    assert all(p.shape[1] % tw == 0 for p in pieces) and n % tn == 0, name

    def body(*refs):
        b_ref, o_ref = refs[len(pieces):]
        for p_ref, first, count in zip(refs, firsts, counts):
            @pl.when((pl.program_id(0) >= first) & (pl.program_id(0) < first + count))
            def _(p_ref=p_ref):
                o_ref[...] = _dot_tn(p_ref[...], b_ref[...]).astype(out_dtype)

    def piece_spec(first, count):
        return pl.BlockSpec((k, tw), lambda i, j: (0, jnp.clip(i - first, 0, count - 1)))

    return pl.pallas_call(
        body, name=name, grid=(sum(counts), n // tn),
        in_specs=[piece_spec(first, count) for first, count in zip(firsts, counts)]
        + [pl.BlockSpec((k, tn), lambda i, j: (0, j))],
        out_specs=pl.BlockSpec((tw, tn), lambda i, j: (i, j)),
        out_shape=jax.ShapeDtypeStruct((sum(counts) * tw, n), out_dtype),
        compiler_params=_cparams(("parallel", "parallel")),
    )(*pieces, b)


ROWS = 512


def _rms_fwd(x, g):
    s, d = x.shape

    def body(x_ref, g_ref, h_ref):
        xv = x_ref[...]
        r = lax.rsqrt(jnp.mean(xv * xv, axis=-1, keepdims=True) + EPS)
        h_ref[...] = (xv * r * g_ref[...]).astype(BF16)

    return pl.pallas_call(
        body, name="rms_fwd", grid=(s // ROWS,),
        in_specs=[pl.BlockSpec((ROWS, d), lambda i: (i, 0)), pl.BlockSpec((1, d), lambda i: (0, 0))],
        out_specs=pl.BlockSpec((ROWS, d), lambda i: (i, 0)),
        out_shape=jax.ShapeDtypeStruct((s, d), BF16),
        compiler_params=_cparams(("parallel",)),
    )(x, g)


def _post_fwd(x, y, g):
    s, d = x.shape

    def body(x_ref, y_ref, g_ref, o_ref):
        yv = y_ref[...]
        r = lax.rsqrt(jnp.mean(yv * yv, axis=-1, keepdims=True) + EPS)
        o_ref[...] = x_ref[...] + yv * r * g_ref[...]

    row = pl.BlockSpec((ROWS, d), lambda i: (i, 0))
    return pl.pallas_call(
        body, name="post_fwd", grid=(s // ROWS,),
        in_specs=[row, row, pl.BlockSpec((1, d), lambda i: (0, 0))],
        out_specs=row,
        out_shape=jax.ShapeDtypeStruct((s, d), F32),
        compiler_params=_cparams(("parallel",)),
    )(x, y, g)


def _loss_head(out, tgt):
    s, d = out.shape

    def body(o_ref, t_ref, dout_ref, sum_ref):
        @pl.when(pl.program_id(0) == 0)
        def _():
            sum_ref[...] = jnp.zeros_like(sum_ref)

        e = o_ref[...] - t_ref[...]
        dout_ref[...] = e * (1.0 / d)
        sum_ref[...] += jnp.sum(jnp.sum(e * e, axis=1, keepdims=True), axis=0, keepdims=True)

    row = pl.BlockSpec((ROWS, d), lambda i: (i, 0))
    return pl.pallas_call(
        body, name="loss_head", grid=(s // ROWS,),
        in_specs=[row, row],
        out_specs=[row, pl.BlockSpec((1, 1), lambda i: (0, 0))],
        out_shape=[jax.ShapeDtypeStruct((s, d), F32), jax.ShapeDtypeStruct((1, 1), F32)],
        compiler_params=_cparams(("arbitrary",)),
    )(out, tgt)


def _post_bwd(dout, y, g):
    s, d = y.shape

    def body(do_ref, y_ref, g_ref, dy_ref, dg_ref):
        @pl.when(pl.program_id(0) == 0)
        def _():
            dg_ref[...] = jnp.zeros_like(dg_ref)

        yv = y_ref[...]
        dv = do_ref[...]
        r = lax.rsqrt(jnp.mean(yv * yv, axis=-1, keepdims=True) + EPS)
        dg_ref[...] += jnp.sum(dv * yv * r, axis=0, keepdims=True)
        w = dv * g_ref[...]
        dy = r * (w - yv * (r * r) * jnp.mean(w * yv, axis=-1, keepdims=True))
        dy_ref[...] = dy.astype(BF16)

    row = pl.BlockSpec((ROWS, d), lambda i: (i, 0))
    vec = pl.BlockSpec((1, d), lambda i: (0, 0))
    return pl.pallas_call(
        body, name="post_bwd", grid=(s // ROWS,),
        in_specs=[row, row, vec],
        out_specs=[row, vec],
        out_shape=[jax.ShapeDtypeStruct((s, d), BF16), jax.ShapeDtypeStruct((1, d), F32)],
        compiler_params=_cparams(("arbitrary",)),
    )(dout, y, g)


def _pre_bwd(dh, x, g, dout):
    s, d = x.shape

    def body(dh_ref, x_ref, g_ref, do_ref, dx_ref, dg_ref):
        @pl.when(pl.program_id(0) == 0)
        def _():
            dg_ref[...] = jnp.zeros_like(dg_ref)

        xv = x_ref[...]
        dv = dh_ref[...]
        r = lax.rsqrt(jnp.mean(xv * xv, axis=-1, keepdims=True) + EPS)
        dg_ref[...] += jnp.sum(dv * xv * r, axis=0, keepdims=True)
        w = dv * g_ref[...]
        dx_ref[...] = do_ref[...] + r * (w - xv * (r * r) * jnp.mean(w * xv, axis=-1, keepdims=True))

    row = pl.BlockSpec((ROWS, d), lambda i: (i, 0))
    vec = pl.BlockSpec((1, d), lambda i: (0, 0))
    return pl.pallas_call(
        body, name="pre_bwd", grid=(s // ROWS,),
        in_specs=[row, row, vec, row],
        out_specs=[row, vec],
        out_shape=[jax.ShapeDtypeStruct((s, d), F32), jax.ShapeDtypeStruct((1, d), F32)],
        compiler_params=_cparams(("arbitrary",)),
    )(dh, x, g, dout)


GLA_STEP = 4
GLA_ROWS = GLA_STEP * CHUNK
GLA_CHUNKS = [slice(c * CHUNK, (c + 1) * CHUNK) for c in range(GLA_STEP)]


def _chunk_triangles():
    ri = lax.broadcasted_iota(jnp.int32, (GLA_ROWS, GLA_ROWS), 0)
    ci = lax.broadcasted_iota(jnp.int32, (GLA_ROWS, GLA_ROWS), 1)
    same = (ri // CHUNK) == (ci // CHUNK)
    return (jnp.where(same & (ri >= ci), 1.0, 0.0).astype(BF16), jnp.where(same & (ci >= ri), 1.0, 0.0).astype(BF16))


def _per_chunk(fn, like):
    row = lax.broadcasted_iota(jnp.int32, like.shape, 0)
    return [fn((row >= c * CHUNK) & (row < (c + 1) * CHUNK)) for c in range(GLA_STEP)]


def _spread(per_chunk, like):
    row = lax.broadcasted_iota(jnp.int32, like.shape, 0)
    out = per_chunk[-1]
    for c in reversed(range(GLA_STEP - 1)):
        out = jnp.where(row < (c + 1) * CHUNK, per_chunk[c], out)
    return out


def _gla_gate(ga_b, wa_b, b_ref, tri):
    pre = _dot(ga_b, wa_b) + b_ref[...]
    la = _log_sigmoid(pre) * (1.0 / GLA_TAU)
    return pre, _dot01(tri, la)


def _chunk_ends(cum):
    row = lax.broadcasted_iota(jnp.int32, cum.shape, 0)
    return [jnp.sum(jnp.where(row == (c + 1) * CHUNK - 1, cum, 0.0), axis=0, keepdims=True)
            for c in range(GLA_STEP)]


def _heads(width):
    return [slice(h * width, (h + 1) * width) for h in range(GLA_HEADS)]


def _z_specs_gla(rev=None):
    idx = (lambda n: n) if rev is None else rev
    return [
        pl.BlockSpec((GLA_ROWS, GLA_KW), lambda n: (idx(n), 0)),
        pl.BlockSpec((GLA_ROWS, GLA_KW), lambda n: (idx(n), 1)),
        pl.BlockSpec((GLA_ROWS, D_GLA), lambda n: (idx(n), 1)),
        pl.BlockSpec((GLA_ROWS, D_GLA), lambda n: (idx(n), 2)),
        pl.BlockSpec((GLA_ROWS, LANE), lambda n: (idx(n), OFF_GA // LANE)),
    ]


def _gla_fwd(z, wa_pad, b_alpha, g_gla):
    s = z.shape[0]
    nchunk = s // CHUNK

    def body(q_ref, k_ref, v_ref, gg_ref, ga_ref, wa_ref, b_ref, g_ref, y_ref, o_ref, st_ref, pre_ref, cum_ref,
             state):
        @pl.when(pl.program_id(0) == 0)
        def _():
            state[...] = jnp.zeros_like(state)

        ga_b = ga_ref[...].astype(BF16)
        tri, _ = _chunk_triangles()
        nh = range(GLA_HEADS)
        keys, vals = _heads(GLA_DK), _heads(GLA_DV)
        pre, cum = _gla_gate(ga_b, wa_ref[...].astype(BF16), b_ref, tri)
        pre_ref[...] = pre
        cum_ref[...] = cum
        cends = _chunk_ends(cum)
        kd_b = (k_ref[...] * jnp.exp(_spread(cends, cum) - cum)).astype(BF16)
        qs = (q_ref[...] * GLA_SCALE).astype(BF16)
        v_b = v_ref[...].astype(BF16)
        uts = [[_dot_tn(v_b[rs, vals[h]], kd_b[rs, keys[h]]) for h in nh] for rs in GLA_CHUNKS]
        sts, prev = [], [state[h] for h in nh]
        for c in range(GLA_STEP):
            a = jnp.exp(cends[c])
            prev = [prev[h] * a[:, keys[h]] + uts[c][h] for h in nh]
            sts.append(prev)
        for h in nh:
            state[h] = prev[h]
            for c in range(GLA_STEP):
                st_ref[c, h] = sts[c][h]
        outs = [[_dot_nt(qs[rs, keys[h]], sts[c][h].astype(BF16)) for h in nh] for c, rs in enumerate(GLA_CHUNKS)]
        for h in nh:
            o, vs = jnp.concatenate([outs[c][h] for c in range(GLA_STEP)], axis=0), vals[h]
            o_ref[:, vs] = o
            r = lax.rsqrt(jnp.mean(o * o, axis=-1, keepdims=True) + EPS)
            gg = gg_ref[:, vs]
            y_ref[:, vs] = (o * r * g_ref[:, vs] * (gg * _sigmoid(gg))).astype(BF16)

    full = lambda shape: pl.BlockSpec(shape, lambda n: tuple(0 for _ in shape))
    wide = pl.BlockSpec((GLA_ROWS, D_GLA), lambda n: (n, 0))
    return pl.pallas_call(
        body, name="gla_fwd", grid=(nchunk // GLA_STEP,),
        in_specs=_z_specs_gla() + [full((LANE, GLA_KW)), full((1, GLA_KW)), full((1, D_GLA))],
        out_specs=[wide, wide, pl.BlockSpec((GLA_STEP, GLA_HEADS, GLA_DV, GLA_DK), lambda n: (n, 0, 0, 0)),
                   pl.BlockSpec((GLA_ROWS, GLA_KW), lambda n: (n, 0)), pl.BlockSpec((GLA_ROWS, GLA_KW), lambda n: (n, 0))],
        out_shape=[jax.ShapeDtypeStruct((s, D_GLA), BF16), jax.ShapeDtypeStruct((s, D_GLA), F32),
                   jax.ShapeDtypeStruct((nchunk, GLA_HEADS, GLA_DV, GLA_DK), F32),
                   jax.ShapeDtypeStruct((s, GLA_KW), F32), jax.ShapeDtypeStruct((s, GLA_KW), F32)],
        scratch_shapes=[pltpu.VMEM((GLA_HEADS, GLA_DV, GLA_DK), F32)],
        compiler_params=_cparams(("arbitrary",)),
    )(z, z, z, z, z, wa_pad, b_alpha, g_gla)


def _gla_bwd(dyc, o_gla, z, wa_pad, g_gla, states, gate_pre, gate_cum):
    s = z.shape[0]
    nsteps = s // GLA_ROWS
    rev = lambda n: nsteps - 1 - n

    def body(dy_ref, o_ref, q_ref, k_ref, v_ref, gg_ref, ga_ref, wa_ref, g_ref, st_ref, stp_ref, pre_ref, cum_ref,
             dq_ref, dk_ref, dv_ref, dgg_ref, dga_ref, dwa_ref, db_ref, dg_ref, carry):
        step = pl.program_id(0)

        @pl.when(step == 0)
        def _():
            carry[...] = jnp.zeros_like(carry)
            dwa_ref[...] = jnp.zeros_like(dwa_ref)
            db_ref[...] = jnp.zeros_like(db_ref)
            dg_ref[...] = jnp.zeros_like(dg_ref)

        has_prev = (step < nsteps - 1).astype(F32)
        ga_b = ga_ref[...].astype(BF16)
        _, tri_up = _chunk_triangles()
        nh, nc = range(GLA_HEADS), range(GLA_STEP)
        keys, vals = _heads(GLA_DK), _heads(GLA_DV)
        wa_b = wa_ref[...].astype(BF16)
        pre, cum = pre_ref[...], cum_ref[...]
        cends = _chunk_ends(cum)
        e = jnp.exp(_spread(cends, cum) - cum)
        a = [jnp.exp(cends[c]) for c in nc]
        kf = k_ref[...]
        kd_b = (kf * e).astype(BF16)
        v_b = v_ref[...].astype(BF16)
        qs = (q_ref[...] * GLA_SCALE).astype(BF16)
        do_b = []
        for h in nh:
            vs = vals[h]
            o = o_ref[:, vs]
            gg = gg_ref[:, vs]
            g = g_ref[:, vs]
            dy = dy_ref[:, vs]
            r = lax.rsqrt(jnp.mean(o * o, axis=-1, keepdims=True) + EPS)
            sg = _sigmoid(gg)
            dogn = dy * (gg * sg)
            dgg_ref[:, vs] = (dy * (o * r * g) * (sg * (1.0 + gg * (1.0 - sg)))).astype(BF16)
            dg_ref[:, vs] += jnp.sum(dogn * o * r, axis=0, keepdims=True)
            w = dogn * g
            do_b.append((r * (w - o * (r * r) * jnp.mean(w * o, axis=-1, keepdims=True))).astype(BF16))
        dqs = [jnp.concatenate([_dot(do_b[h][rs], st_ref[c, h].astype(BF16)) for c, rs in enumerate(GLA_CHUNKS)],
                               axis=0) for h in nh]
        dq_ref[...] = (jnp.concatenate(dqs, axis=1) * GLA_SCALE).astype(BF16)
        own = [[_dot_tn(do_b[h][rs], qs[rs, keys[h]]) for h in nh] for rs in GLA_CHUNKS]
        gts, later = [None] * GLA_STEP, [carry[h] for h in nh]
        for c in reversed(nc):
            gts[c] = [own[c][h] + later[h] for h in nh]
            later = [gts[c][h] * a[c][:, keys[h]] for h in nh]
        for h in nh:
            carry[h] = later[h]
        gt_b = [[gts[c][h].astype(BF16) for h in nh] for c in nc]
        dkd = jnp.concatenate([jnp.concatenate([_dot(v_b[rs, vals[h]], gt_b[c][h]) for h in nh], axis=1)
                               for c, rs in enumerate(GLA_CHUNKS)], axis=0)
        dvs = [[_dot_nt(kd_b[rs, keys[h]], gt_b[c][h]) for h in nh] for c, rs in enumerate(GLA_CHUNKS)]
        before = lambda c, h: st_ref[c - 1, h] if c > 0 else stp_ref[0, h] * has_prev
        da = [jnp.concatenate([jnp.sum(gts[c][h] * before(c, h), axis=0, keepdims=True) for h in nh], axis=1)
              for c in nc]
        for h in nh:
            dv_ref[:, vals[h]] = jnp.concatenate([dvs[c][h] for c in nc], axis=0).astype(BF16)
        dk_ref[...] = (dkd * e).astype(BF16)
        dd = dkd * kf * e
        dsum = _per_chunk(lambda mine: jnp.sum(jnp.where(mine, dd, 0.0), axis=0, keepdims=True), dd)
        dcend = _spread([dsum[c] + da[c] * a[c] for c in nc], dd)
        dla = dcend - _dot01(tri_up, dd)
        dpre = dla * (1.0 / GLA_TAU) * (1.0 - _sigmoid(pre))
        dpre_b = dpre.astype(BF16)
        dga_ref[...] = _dot_nt(dpre_b, wa_b).astype(BF16)
        dwa_ref[...] += _dot_tn(ga_b, dpre_b)
        db_ref[...] += jnp.sum(dpre, axis=0, keepdims=True)

    full = lambda shape: pl.BlockSpec(shape, lambda n: tuple(0 for _ in shape))
    wide = pl.BlockSpec((GLA_ROWS, D_GLA), lambda n: (rev(n), 0))
    keyw = pl.BlockSpec((GLA_ROWS, GLA_KW), lambda n: (rev(n), 0))
    st_spec = pl.BlockSpec((GLA_STEP, GLA_HEADS, GLA_DV, GLA_DK), lambda n: (rev(n), 0, 0, 0))
    stp_spec = pl.BlockSpec((1, GLA_HEADS, GLA_DV, GLA_DK),
                            lambda n: (jnp.maximum(GLA_STEP * rev(n) - 1, 0), 0, 0, 0))
    return pl.pallas_call(
        body, name="gla_bwd", grid=(nsteps,),
        in_specs=[wide, wide] + _z_specs_gla(rev)
        + [full((LANE, GLA_KW)), full((1, D_GLA)), st_spec, stp_spec, keyw, keyw],
        out_specs=[keyw, keyw, wide, wide, pl.BlockSpec((GLA_ROWS, LANE), lambda n: (rev(n), 0)),
                   full((LANE, GLA_KW)), full((1, GLA_KW)), full((1, D_GLA))],
        out_shape=[jax.ShapeDtypeStruct((s, GLA_KW), BF16), jax.ShapeDtypeStruct((s, GLA_KW), BF16),
                   jax.ShapeDtypeStruct((s, D_GLA), BF16), jax.ShapeDtypeStruct((s, D_GLA), BF16),
                   jax.ShapeDtypeStruct((s, LANE), BF16),
                   jax.ShapeDtypeStruct((LANE, GLA_KW), F32), jax.ShapeDtypeStruct((1, GLA_KW), F32),
                   jax.ShapeDtypeStruct((1, D_GLA), F32)],
        scratch_shapes=[pltpu.VMEM((GLA_HEADS, GLA_DV, GLA_DK), F32)],
        compiler_params=_cparams(("arbitrary",)),
    )(dyc, o_gla, z, z, z, z, z, wa_pad, g_gla, states, states, gate_pre, gate_cum)


def _build_bias_table(rb_row, et_ref):
    far = jnp.broadcast_to(rb_row[:, 2 * REL_CLIP:2 * REL_CLIP + 1], (1, LANE))
    near_hi = rb_row[:, REL_CLIP:2 * REL_CLIP]
    near_lo = rb_row[:, 0:REL_CLIP]
    past = jnp.broadcast_to(rb_row[:, 0:1], (1, LANE))
    seg = [far, far, far, far, near_hi, near_lo] + [past] * (ET_ROWS // LANE - 5)
    ri = lax.broadcasted_iota(jnp.int32, (LANE, LANE), 0)
    ci = lax.broadcasted_iota(jnp.int32, (LANE, LANE), 1)
    for kb in range(ET_ROWS // LANE):
        wmat = jnp.where(ri + ci < LANE, seg[kb], seg[kb + 1])
        blk = pltpu.roll(wmat, 0, 1, stride=1, stride_axis=0)
        lag = LEFT_CHUNKS + ci // CHUNK - (2 * kb + ri // CHUNK)
        et_ref[kb * LANE:(kb + 1) * LANE, :] = jnp.where((lag >= 0) & (lag <= LEFT_CHUNKS), blk, NEG)


def _reduce_bias_table(det_ref):
    lane = lax.broadcasted_iota(jnp.int32, (1, LANE), 1)
    ri = lax.broadcasted_iota(jnp.int32, (LANE, LANE), 0)
    ci = lax.broadcasted_iota(jnp.int32, (LANE, LANE), 1)
    flip = jnp.where(ri + ci == LANE - 1, 1.0, 0.0).astype(BF16)
    segs = jnp.zeros((8, LANE), F32)
    seg_row = lax.broadcasted_iota(jnp.int32, (8, LANE), 0)
    prev_minus = jnp.zeros((1, LANE), F32)
    for kb in range(6):
        rolled = pltpu.roll(_dot01(det_ref[kb * LANE:(kb + 1) * LANE, :], flip, left=False), 0, 1,
                            stride=1, stride_axis=0)
        plus = jnp.sum(jnp.where(ci >= ri, rolled, 0.0), axis=0, keepdims=True)
        minus = jnp.sum(jnp.where(ci < ri, rolled, 0.0), axis=0, keepdims=True)
        segs = segs + jnp.where(seg_row == kb, plus + prev_minus, 0.0)
        prev_minus = minus
    segs = _dot01(segs, flip, left=False)
    pick = lambda kb: jnp.sum(jnp.where(seg_row == kb, segs, 0.0), axis=0, keepdims=True)
    far = jnp.sum(pick(0) + pick(1) + pick(2) + pick(3), axis=1, keepdims=True)
    last = jnp.where(lane == 0, far, 0.0)
    return jnp.concatenate([pick(5), pick(4), last], axis=1)


def _att_window(b):
    c0 = 2 * b
    kstart = pl.multiple_of(jnp.maximum(c0 - LEFT_CHUNKS, 0) * CHUNK, CHUNK)
    eoff = pl.multiple_of(jnp.maximum(LEFT_CHUNKS - c0, 0) * CHUNK, CHUNK)
    return kstart, eoff


def _att_probs(q_b, kw_b, et):
    st = _dot_nt(kw_b, q_b) * ATT_SCALE + et
    m = jnp.max(st, axis=0, keepdims=True)
    ex = jnp.exp(st - m)
    return ex * (1.0 / jnp.sum(ex, axis=0, keepdims=True))


def _att_fwd(z, rb_pad, g_att):
    s = z.shape[0]
    nblk = s // QB
    c_aq, c_ak, c_av, c_ag = [(OFF_AQ + i * D_ATT) // ATT_HD for i in range(4)]

    def body(q_ref, k_ref, v_ref, ag_ref, rb_ref, g_ref, y_ref, o_ref, p_ref, et_ref, kb_ref, vb_ref):
        h = pl.program_id(0)
        b = pl.program_id(1)

        @pl.when(b == 0)
        def _():
            _build_bias_table(rb_ref[pl.ds(h, 1), :], et_ref)
            kb_ref[...] = k_ref[...].astype(BF16)
            vb_ref[...] = v_ref[...].astype(BF16)

        for j in range(ATT_UNROLL):
            rs = slice(j * QB, (j + 1) * QB)
            kstart, eoff = _att_window(b * ATT_UNROLL + j)
            q_b = q_ref[rs, :].astype(BF16)
            kw_b = kb_ref[pl.ds(kstart, WIN), :]
            vw_b = vb_ref[pl.ds(kstart, WIN), :]
            pt = _att_probs(q_b, kw_b, et_ref[pl.ds(eoff, WIN), :])
            p_ref[0, j] = pt
            o = _dot_tn(pt.astype(BF16), vw_b)
            o_ref[rs, :] = o
            r = lax.rsqrt(jnp.mean(o * o, axis=-1, keepdims=True) + EPS)
            ag = ag_ref[rs, :]
            y_ref[rs, :] = (o * r * g_ref[...] * (ag * _sigmoid(ag))).astype(BF16)

    blk = lambda col: pl.BlockSpec((ATT_UNROLL * QB, ATT_HD), lambda h, b: (b, col + h))
    seq = lambda col: pl.BlockSpec((s, ATT_HD), lambda h, b: (0, col + h))
    out_blk = pl.BlockSpec((ATT_UNROLL * QB, ATT_HD), lambda h, b: (b, h))
    return pl.pallas_call(
        body, name="att_fwd", grid=(ATT_HEADS, nblk // ATT_UNROLL),
        in_specs=[blk(c_aq), seq(c_ak), seq(c_av), blk(c_ag),
                  pl.BlockSpec((ATT_HEADS, 3 * LANE), lambda h, b: (0, 0)),
                  pl.BlockSpec((1, ATT_HD), lambda h, b: (0, h))],
        out_specs=[out_blk, out_blk, pl.BlockSpec((1, ATT_UNROLL, WIN, QB), lambda h, b: (h, b, 0, 0))],
        out_shape=[jax.ShapeDtypeStruct((s, D_ATT), BF16), jax.ShapeDtypeStruct((s, D_ATT), F32),
                   jax.ShapeDtypeStruct((ATT_HEADS, nblk, WIN, QB), F32)],
        scratch_shapes=[pltpu.VMEM((ET_ROWS, LANE), F32), pltpu.VMEM((s, ATT_HD), BF16),
                        pltpu.VMEM((s, ATT_HD), BF16)],
        compiler_params=_cparams(("arbitrary", "arbitrary")),
    )(z, z, z, z, rb_pad, g_att)


def _att_bwd(dyc, o_att, probs, z, g_att):
    s = z.shape[0]
    nblk = s // QB
    c_aq, c_ak, c_av, c_ag = [(OFF_AQ + i * D_ATT) // ATT_HD for i in range(4)]
    c_dy = D_GLA // ATT_HD

    def body(dy_ref, o_ref, p_ref, q_ref, k_ref, v_ref, ag_ref, g_ref,
             dq_ref, dk_ref, dv_ref, dag_ref, drb_ref, dg_ref, det_ref, kb_ref, vb_ref, dk_acc, dv_acc):
        b = pl.program_id(1)

        @pl.when(b == 0)
        def _():
            kb_ref[...] = k_ref[...].astype(BF16)
            vb_ref[...] = v_ref[...].astype(BF16)
            det_ref[...] = jnp.zeros_like(det_ref)
            dk_acc[...] = jnp.zeros_like(dk_acc)
            dv_acc[...] = jnp.zeros_like(dv_acc)
            dg_ref[...] = jnp.zeros_like(dg_ref)

        g = g_ref[...]
        dg = jnp.zeros((1, ATT_HD), F32)
        for j in range(ATT_UNROLL):
            rs = slice(j * QB, (j + 1) * QB)
            kstart, eoff = _att_window(b * ATT_UNROLL + j)
            q_b = q_ref[rs, :].astype(BF16)
            kw_b = kb_ref[pl.ds(kstart, WIN), :]
            vw_b = vb_ref[pl.ds(kstart, WIN), :]
            pt = p_ref[0, j]
            o = o_ref[rs, :]
            ag = ag_ref[rs, :]
            dy = dy_ref[rs, :]
            r = lax.rsqrt(jnp.mean(o * o, axis=-1, keepdims=True) + EPS)
            sg = _sigmoid(ag)
            don = dy * (ag * sg)
            dag_ref[rs, :] = (dy * (o * r * g) * (sg * (1.0 + ag * (1.0 - sg)))).astype(BF16)
            dg = dg + jnp.sum(don * o * r, axis=0, keepdims=True)
            w = don * g
            do_b = (r * (w - o * (r * r) * jnp.mean(w * o, axis=-1, keepdims=True))).astype(BF16)
            pt_b = pt.astype(BF16)
            dpt = _dot_nt(vw_b, do_b)
            dst = pt * (dpt - jnp.sum(dpt * pt, axis=0, keepdims=True))
            det_ref[pl.ds(eoff, WIN), :] += dst
            ds_b = (dst * ATT_SCALE).astype(BF16)
            dq_ref[rs, :] = _dot_tn(ds_b, kw_b).astype(BF16)
            dk_acc[pl.ds(kstart, WIN), :] += _dot(ds_b, q_b)
            dv_acc[pl.ds(kstart, WIN), :] += _dot(pt_b, do_b)
        dg_ref[...] += dg

        @pl.when(b == nblk // ATT_UNROLL - 1)
        def _():
            drb_ref[0] = jnp.broadcast_to(_reduce_bias_table(det_ref), (8, 3 * LANE))
            dk_ref[...] = dk_acc[...].astype(BF16)
            dv_ref[...] = dv_acc[...].astype(BF16)

    blk = lambda col: pl.BlockSpec((ATT_UNROLL * QB, ATT_HD), lambda h, b: (b, col + h))
    seq = lambda col: pl.BlockSpec((s, ATT_HD), lambda h, b: (0, col + h))
    out_blk = pl.BlockSpec((ATT_UNROLL * QB, ATT_HD), lambda h, b: (b, h))
    out_seq = pl.BlockSpec((s, ATT_HD), lambda h, b: (0, h))
    return pl.pallas_call(
        body, name="att_bwd", grid=(ATT_HEADS, nblk // ATT_UNROLL),
        in_specs=[blk(c_dy), blk(0), pl.BlockSpec((1, ATT_UNROLL, WIN, QB), lambda h, b: (h, b, 0, 0)),
                  blk(c_aq), seq(c_ak), seq(c_av), blk(c_ag),
                  pl.BlockSpec((1, ATT_HD), lambda h, b: (0, h))],
        out_specs=[out_blk, out_seq, out_seq, out_blk,
                   pl.BlockSpec((1, 8, 3 * LANE), lambda h, b: (h, 0, 0)),
                   pl.BlockSpec((1, ATT_HD), lambda h, b: (0, h))],
        out_shape=[jax.ShapeDtypeStruct((s, D_ATT), BF16), jax.ShapeDtypeStruct((s, D_ATT), BF16),
                   jax.ShapeDtypeStruct((s, D_ATT), BF16), jax.ShapeDtypeStruct((s, D_ATT), BF16),
                   jax.ShapeDtypeStruct((ATT_HEADS, 8, 3 * LANE), F32),
                   jax.ShapeDtypeStruct((1, D_ATT), F32)],
        scratch_shapes=[pltpu.VMEM((ET_ROWS, LANE), F32),
                        pltpu.VMEM((s, ATT_HD), BF16), pltpu.VMEM((s, ATT_HD), BF16),
                        pltpu.VMEM((s, ATT_HD), F32), pltpu.VMEM((s, ATT_HD), F32)],
        compiler_params=_cparams(("arbitrary", "arbitrary")),
    )(dyc, o_att, probs, z, z, z, z, g_att)


ADAM_ROWS = 64
ADAM_COL_ROWS = 32


def _adam_math(w, g, m, v):
    m2 = ADAM_B1 * m + (1.0 - ADAM_B1) * g
    v2 = ADAM_B2 * v + (1.0 - ADAM_B2) * (g * g)
    m_hat = m2 / (1.0 - ADAM_B1 ** ADAM_STEP)
    v_hat = v2 / (1.0 - ADAM_B2 ** ADAM_STEP)
    delta = -ADAM_LR * (m_hat / (jnp.sqrt(v_hat) + ADAM_EPS) + ADAM_WD * w)
    return delta, m2, v2


def _adam_sharded(parts, first, w, m, v, name):
    nl, nr, nc = w.shape

    def body(*refs):
        p_refs = refs[:nl]
        w_ref, m_ref, v_ref, g_ref, d_ref, m2_ref, v2_ref = refs[nl:]
        for k in range(nl):
            @pl.when(pl.program_id(0) == k)
            def _(p_ref=p_refs[k]):
                g = p_ref[0].astype(F32)
                for dev in range(1, N_DEV):
                    g = g + p_ref[dev].astype(F32)
                delta, m2, v2 = _adam_math(w_ref[0], g, m_ref[0], v_ref[0])
                g_ref[0] = g
                d_ref[0] = delta
                m2_ref[0] = m2
                v2_ref[0] = v2

    def part_spec(k):
        return pl.BlockSpec((N_DEV, ADAM_ROWS, nc), lambda l, i: (0, first + jnp.where(l == k, i, 0), 0))

    blk = pl.BlockSpec((1, ADAM_ROWS, nc), lambda l, i: (l, i, 0))
    shp = jax.ShapeDtypeStruct(w.shape, F32)
    return pl.pallas_call(
        body, name=name, grid=(nl, pl.cdiv(nr, ADAM_ROWS)),
        in_specs=[part_spec(k) for k in range(nl)] + [blk, blk, blk],
        out_specs=[blk, blk, blk, blk],
        out_shape=[shp, shp, shp, shp],
        compiler_params=_cparams(("arbitrary", "arbitrary")),
    )(*parts, w, m, v)


def _adam_columns(parts, first, w, m, v):
    nc, nl, d = w.shape

    def body(*refs):
        p_refs = refs[:nl]
        w_ref, m_ref, v_ref, g_ref, d_ref, m2_ref, v2_ref = refs[nl:]
        for l in range(nl):
            g = p_refs[l][0].astype(F32)
            for slot in range(1, parts[l].shape[0]):
                g = g + p_refs[l][slot].astype(F32)
            delta, m2, v2 = _adam_math(w_ref[:, l, :], g, m_ref[:, l, :], v_ref[:, l, :])
            g_ref[:, l, :] = g
            d_ref[:, l, :] = delta
            m2_ref[:, l, :] = m2
            v2_ref[:, l, :] = v2

    blk = pl.BlockSpec((ADAM_COL_ROWS, nl, d), lambda i: (i, 0, 0))
    shp = jax.ShapeDtypeStruct(w.shape, F32)
    return pl.pallas_call(
        body, name="adam_w_in", grid=(pl.cdiv(nc, ADAM_COL_ROWS),),
        in_specs=[pl.BlockSpec((p.shape[0], ADAM_COL_ROWS, d), lambda i: (0, first + i, 0)) for p in parts]
        + [blk, blk, blk],
        out_specs=[blk, blk, blk, blk],
        out_shape=[shp, shp, shp, shp],
        compiler_params=_cparams(("parallel",)),
    )(*parts, w, m, v)


def _adam_small(ws, gs, ms, vs):
    n = len(ws)

    def body(*refs):
        w_refs, g_refs, m_refs, v_refs, d_refs, m2_refs, v2_refs = [refs[i * n:(i + 1) * n] for i in range(7)]
        for i in range(n):
            delta, m2, v2 = _adam_math(w_refs[i][...], g_refs[i][...], m_refs[i][...], v_refs[i][...])
            d_refs[i][...] = delta
            m2_refs[i][...] = m2
            v2_refs[i][...] = v2

    shapes = [jax.ShapeDtypeStruct(w.shape, F32) for w in ws]
    out = pl.pallas_call(body, name="adam_small", out_shape=shapes * 3)(*ws, *gs, *ms, *vs)
    return out[:n], out[n:2 * n], out[2 * n:]


def _position():
    return lax.axis_index("x"), lax.axis_index("y"), lax.axis_index("c")


def _slot(p):
    return 4 * p[0] + 2 * p[1] + p[2]


BF16_TILE_ROWS = 16


def _slab_rows(rows, cols):
    return -(-(rows + cols) // BF16_TILE_ROWS) * BF16_TILE_ROWS


RELAYOUT_COLS = 1024
RELAYOUT_CHUNK = 64


def _shard_pieces(dev, rows, cols):
    moved = ((0, GA_ORIG, 0), (GA_ORIG, GA_ORIG + GLA_RANK, OFF_GA - GA_ORIG), (GA_ORIG + GLA_RANK, D_IN, -GLA_RANK))
    c0, c1 = dev * cols, (dev + 1) * cols
    return [(rows + max(c0, lo) - c0, max(c0, lo) + off, min(c1, hi) - max(c0, lo))
            for lo, hi, off in moved if max(c0, lo) < min(c1, hi)]


def _move_rows(src, src_row, dst, dst_row, n):
    assert src_row % 2 == 0 and dst_row % 2 == 0 and n % 2 == 0
    for r in range(0, n // 2, RELAYOUT_CHUNK):
        m = min(RELAYOUT_CHUNK, n // 2 - r)
        dst[dst_row // 2 + r:dst_row // 2 + r + m, :] = src[src_row // 2 + r:src_row // 2 + r + m, :]


def _aligned_weight(land, rows, cols):
    _, slab, d = land.shape
    ct = min(RELAYOUT_COLS, d)

    def body(land_ref, wt_ref, wo_ref):
        dev = pl.program_id(1)
        src = land_ref.bitcast(jnp.uint32)
        dst = wt_ref.bitcast(jnp.uint32)
        wo_ref[...] = land_ref[0:rows, :]

        @pl.when(dev == 0)
        def _():
            dst[D_IN // 2:D_ZP // 2, :] = jnp.zeros(((D_ZP - D_IN) // 2, ct), jnp.uint32)

        for k in range(N_DEV):
            @pl.when(dev == k)
            def _(k=k):
                for at, to, n in _shard_pieces(k, rows, cols):
                    _move_rows(src, at, dst, to, n)

    return pl.pallas_call(
        body, name="aligned_weight", grid=(d // ct, N_DEV),
        in_specs=[pl.BlockSpec((slab, ct), lambda c, dev: (dev, c))],
        out_specs=[pl.BlockSpec((D_ZP, ct), lambda c, dev: (0, c)),
                   pl.BlockSpec((rows, ct), lambda c, dev: (dev, c))],
        out_shape=[jax.ShapeDtypeStruct((D_ZP, d), land.dtype),
                   jax.ShapeDtypeStruct((N_DEV * rows, d), land.dtype)],
        compiler_params=_cparams(("parallel", "arbitrary")),
    )(land.reshape(N_DEV * slab, d))


def _partial_slabs(dwt, cols, by_core=False):
    d = dwt[0].shape[1]
    bounds = (0, GA_ORIG, OFF_GA, D_ZP)
    assert tuple(a.shape[0] for a in dwt) == tuple(hi - lo for lo, hi in zip(bounds, bounds[1:]))
    slab = _slab_rows(0, cols)
    ct = min(RELAYOUT_COLS, d)

    def body(*refs):
        out_ref = refs[-1]
        dev = pl.program_id(1)
        srcs = [ref.bitcast(jnp.uint32) for ref in refs[:-1]]
        dst = out_ref.bitcast(jnp.uint32)
        dst[cols // 2:slab // 2, :] = jnp.zeros(((slab - cols) // 2, ct), jnp.uint32)
        for k in range(N_DEV):
            @pl.when(dev == k)
            def _(k=k):
                for to, at, n in _shard_pieces(k, 0, cols):
                    which = max(i for i, lo in enumerate(bounds[:-1]) if lo <= at)
                    assert at + n <= bounds[which + 1]
                    _move_rows(srcs[which], at - bounds[which], dst, to, n)

    place = (lambda dev: (dev % 2) * (N_DEV // 2) + dev // 2) if by_core else (lambda dev: dev)
    out = pl.pallas_call(
        body, name="partial_slabs", grid=(d // ct, N_DEV),
        in_specs=[pl.BlockSpec((a.shape[0], ct), lambda c, dev: (0, c)) for a in dwt],
        out_specs=pl.BlockSpec((slab, ct), lambda c, dev: (place(dev), c)),
        out_shape=jax.ShapeDtypeStruct((N_DEV * slab, d), dwt[0].dtype),
        compiler_params=_cparams(("parallel", "arbitrary")),
    )(*dwt)
    return out.reshape((2, N_DEV // 2, slab, d) if by_core else (N_DEV, slab, d))


def _pair_sum(mine, theirs):
    _, nchip, slab, d = mine.shape
    rows = next(r for r in range(512, 0, -BF16_TILE_ROWS) if slab % r == 0)

    def body(m_ref, t_ref, o_ref):
        south = lax.axis_index("c") == 0
        own = jnp.where(south, m_ref[0, 0], m_ref[1, 0]).astype(F32)
        got = jnp.where(south, t_ref[1, 0], t_ref[0, 0]).astype(F32)
        o_ref[0] = (own + got).astype(o_ref.dtype)

    both = pl.BlockSpec((2, 1, rows, d), lambda j, i: (0, j, i, 0))
    return pl.pallas_call(
        body, name="pair_sum", grid=(nchip, slab // rows),
        in_specs=[both, both],
        out_specs=pl.BlockSpec((1, rows, d), lambda j, i: (j, i, 0)),
        out_shape=jax.ShapeDtypeStruct((nchip, slab, d), mine.dtype),
        compiler_params=_cparams(("parallel", "parallel")),
    )(mine, theirs)


def _peer(pos, k):
    x, y, c = pos
    return (1 - x if k & 4 else x, 1 - y if k & 2 else y, 1 - c if k & 1 else c)


HBM_SPEC = pl.BlockSpec(memory_space=pltpu.HBM)
SEM_SPEC = pl.BlockSpec(memory_space=pltpu.SEMAPHORE)
GATHER_PEERS = (1, 4, 2, 6)
ALL_PEERS = (1, 2, 3, 4, 5, 6, 7)


def _hbm(a):
    return pltpu.with_memory_space_constraint(a, pltpu.HBM)


BY_DEVICE = (_slot, N_DEV)
BY_CORE = (lambda p: p[2], 2)
BY_CHIP = (lambda p: 2 * p[0] + p[1], 4)


def _split_copies(src_ref, land_ref, send_sems, recv_sems, ks, per_peer, landed, slots):
    slot_of = slots[0]
    me = _position()
    out = []
    for i, k in enumerate(ks):
        peer = _peer(me, k)
        src = src_ref.at[slot_of(peer)] if per_peer else src_ref
        dst = land_ref.at[slot_of(peer) if landed else slot_of(me)]
        out.append(pltpu.make_async_remote_copy(
            src_ref=src, dst_ref=dst, send_sem=send_sems.at[i], recv_sem=recv_sems.at[i],
            device_id=peer, device_id_type=MESH))
    return out


def _exchange_start(src, after, ks, per_peer, name, slots=BY_DEVICE):
    slab = src.shape[1:] if per_peer else src.shape
    land_shape = (slots[1],) + tuple(slab)
    n = len(ks)

    def body(src_ref, land_ref, after_ref, send_sems, recv_sems, src_thru, land_thru, token):
        for cp in _split_copies(src_ref, land_ref, send_sems, recv_sems, ks, per_peer, False, slots):
            cp.start()
        token[...] = jnp.zeros_like(token)

    return pl.pallas_call(
        body, name=name,
        out_shape=(pltpu.SemaphoreType.DMA((n,)), pltpu.SemaphoreType.DMA((n,)),
                   pltpu.HBM(src.shape, src.dtype), pltpu.HBM(land_shape, src.dtype),
                   jax.ShapeDtypeStruct((8, LANE), F32)),
        in_specs=(HBM_SPEC, HBM_SPEC, ANY),
        out_specs=(SEM_SPEC, SEM_SPEC, HBM_SPEC, HBM_SPEC, pl.BlockSpec(memory_space=pltpu.VMEM)),
        input_output_aliases={0: 2, 1: 3},
        compiler_params=pltpu.CompilerParams(has_side_effects=pltpu.SideEffectType.DATAFLOW_SIDE_EFFECTING),
    )(_hbm(src), _hbm(lax.empty(land_shape, src.dtype)), after)


def _exchange_wait(started, after, ks, per_peer, name, slots=BY_DEVICE):
    send_sems, recv_sems, src_thru, land_thru = started

    def body(src_ref, land_ref, send_sems, recv_sems, after_ref, src_dead, land_out):
        for cp in _split_copies(src_ref, land_ref, send_sems, recv_sems, ks, per_peer, True, slots):
            cp.wait_send()
            cp.wait_recv()

    return pl.pallas_call(
        body, name=name,
        out_shape=(pltpu.HBM(src_thru.shape, src_thru.dtype), pltpu.HBM(land_thru.shape, land_thru.dtype)),
        in_specs=(HBM_SPEC, HBM_SPEC, SEM_SPEC, SEM_SPEC, ANY), out_specs=(HBM_SPEC, HBM_SPEC),
        input_output_aliases={0: 0, 1: 1},
        compiler_params=pltpu.CompilerParams(has_side_effects=pltpu.SideEffectType.DATAFLOW_SIDE_EFFECTING),
    )(src_thru, land_thru, send_sems, recv_sems, after)


def _relay_copies(land_ref, send_sems, recv_sems, landed):
    me = _position()
    sibling = _peer(me, 1)
    out = []
    for i, k in enumerate(GATHER_PEERS[1:]):
        blk = land_ref.at[_slot(_peer(sibling if landed else me, k))]
        out.append(pltpu.make_async_remote_copy(
            src_ref=blk, dst_ref=blk, send_sem=send_sems.at[i], recv_sem=recv_sems.at[i],
            device_id=sibling, device_id_type=MESH))
    return out


def _relay_start(land, name):
    n = len(GATHER_PEERS) - 1

    def body(land_ref, send_sems, recv_sems, land_thru, token):
        for cp in _relay_copies(land_ref, send_sems, recv_sems, landed=False):
            cp.start()
        token[...] = jnp.zeros_like(token)

    return pl.pallas_call(
        body, name=name,
        out_shape=(pltpu.SemaphoreType.DMA((n,)), pltpu.SemaphoreType.DMA((n,)),
                   pltpu.HBM(land.shape, land.dtype), jax.ShapeDtypeStruct((8, LANE), F32)),
        in_specs=(HBM_SPEC,),
        out_specs=(SEM_SPEC, SEM_SPEC, HBM_SPEC, pl.BlockSpec(memory_space=pltpu.VMEM)),
        input_output_aliases={0: 2},
        compiler_params=pltpu.CompilerParams(has_side_effects=pltpu.SideEffectType.DATAFLOW_SIDE_EFFECTING),
    )(_hbm(land))


def _relay_wait(started, after, name):
    send_sems, recv_sems, land_thru = started

    def body(land_ref, send_sems, recv_sems, after_ref, land_out):
        for cp in _relay_copies(land_ref, send_sems, recv_sems, landed=True):
            cp.wait_send()
            cp.wait_recv()

    return pl.pallas_call(
        body, name=name,
        out_shape=pltpu.HBM(land_thru.shape, land_thru.dtype),
        in_specs=(HBM_SPEC, SEM_SPEC, SEM_SPEC, ANY), out_specs=HBM_SPEC,
        input_output_aliases={0: 0},
        compiler_params=pltpu.CompilerParams(has_side_effects=pltpu.SideEffectType.DATAFLOW_SIDE_EFFECTING),
    )(land_thru, send_sems, recv_sems, after)


def _share(vec, name, after=None):
    follows = [] if after is None else [after]

    def body(vec_ref, *rest):
        out_ref, send_sems, recv_sems, local_sem = rest[len(follows):]
        me = _position()

        def copy(k, landed):
            peer = _peer(me, k)
            return pltpu.make_async_remote_copy(
                src_ref=vec_ref, dst_ref=out_ref.at[_slot(peer) if landed else _slot(me)],
                send_sem=send_sems.at[k - 1], recv_sem=recv_sems.at[k - 1], device_id=peer, device_id_type=MESH)

        mine = pltpu.make_async_copy(vec_ref, out_ref.at[_slot(me)], local_sem)
        mine.start()
        sent = [copy(k, False) for k in ALL_PEERS]
        for cp in sent:
            cp.start()
        for k in ALL_PEERS:
            copy(k, True).wait_recv()
        for cp in sent:
            cp.wait_send()
        mine.wait()

    return pl.pallas_call(
        body, name=name,
        in_specs=[ANY] * (1 + len(follows)), out_specs=ANY,
        out_shape=jax.ShapeDtypeStruct((N_DEV,) + vec.shape, vec.dtype),
        scratch_shapes=[pltpu.SemaphoreType.DMA((N_DEV - 1,)), pltpu.SemaphoreType.DMA((N_DEV - 1,)),
                        pltpu.SemaphoreType.DMA],
    )(vec, *follows)


def _sum_slots(parts):
    def body(p_ref, o_ref):
        acc = p_ref[0]
        for dev in range(1, N_DEV):
            acc = acc + p_ref[dev]
        o_ref[...] = acc

    return pl.pallas_call(body, name="sum_slots",
                          out_shape=jax.ShapeDtypeStruct(parts.shape[1:], F32))(parts)


PACK_ROWS = 8


def _packed_rows(size):
    return -(-size // (PACK_ROWS * LANE)) * PACK_ROWS


def _pack(arrs):
    def rows(a):
        flat = a.reshape(-1)
        return jnp.pad(flat, (0, _packed_rows(flat.shape[0]) * LANE - flat.shape[0])).reshape(-1, LANE)

    return jnp.concatenate([rows(a) for a in arrs], axis=0)


def _unpack(packed, shapes):
    out, at = [], 0
    for shp in shapes:
        size = 1
        for dim in shp:
            size *= dim
        nrows = _packed_rows(size)
        out.append(packed[at:at + nrows].reshape(-1)[:size].reshape(shp))
        at += nrows
    return out


def _layer_fwd(x, wt, wo, g_pre, g_post, wa_pad, b_alpha, g_gla, g_att, rb_pad, midway=None):
    h = _rms_fwd(x, g_pre)
    z = _matmul(h, wt, "nt", F32, *TILES["in_proj"], "in_proj", n_outer=True)
    y_gla, o_gla, *gla_kept = _gla_fwd(z, wa_pad, b_alpha, g_gla)
    if midway is not None:
        g_att = g_att + midway(y_gla)[:1, :1]
    y_att, o_att, probs = _att_fwd(z, rb_pad, g_att)
    y = _matmul_cols([y_gla, y_att], wo, F32, *TILES["out_proj"][:2], "out_proj")
    out = _post_fwd(x, y, g_post)
    return out, (x, h, z, o_gla, gla_kept, o_att, probs, y_gla, y_att, y)


def _layer_bwd(dout, saved, wt, wo, g_pre, g_post, wa_pad, b_alpha, g_gla, g_att, rb_pad, on_dwo, on_dwt):
    x, h, z, o_gla, gla_kept, o_att, probs, y_gla, y_att, y = saved
    dy, dg_post = _post_bwd(dout, y, g_post)
    dwo = _matmul_rows([y_gla, y_att], dy, BF16, *TILES["out_proj_dw"][:2], "out_proj_dw")
    token = on_dwo(dwo)
    dycat = _matmul(dy, wo, "nt", F32, *TILES["out_proj_dx"], "out_proj_dx", n_outer=True, after=token)
    dq, dk, dv, dgg, dga, dwa, db, dg_gla = _gla_bwd(dycat, o_gla, z, wa_pad, g_gla, *gla_kept)
    daq, dak, dav, dag, drb, dg_att = _att_bwd(dycat, o_att, probs, z, g_att)
    tw, tn = TILES["in_proj_dw"][:2]
    dwt = (_matmul_rows([dq, dk, dv, dgg], h, BF16, tw, tn, "in_proj_dw_gla"),
           _matmul_rows([daq, dak, dav, dag], h, BF16, tw, tn, "in_proj_dw_att"),
           _matmul_rows([dga], h, BF16, LANE, tn, "in_proj_dw_gate"))
    token = on_dwt(dwt)
    dh = _matmul_cols([dq, dk, dv, dgg, daq, dak, dav, dag, dga], wt, F32, *TILES["in_proj_dx"][:2],
                      "in_proj_dx", after=token)
    dx, dg_pre = _pre_bwd(dh, x, g_pre, dout)
    small = (dg_pre[0], dg_post[0], dwa[:GLA_RANK], db[0], dg_gla[0], dg_att[0], drb[:, 0, :N_REL])
    return dx, small


def kernel(x, w_in, w_out, g_pre, g_post, w_alpha, b_alpha, g_gla, g_att, rel_bias, loss_target, m_w_in, m_w_out, m_g_pre, m_g_post, m_w_alpha, m_b_alpha, m_g_gla, m_g_att, m_rel_bias, v_w_in, v_w_out, v_g_pre, v_g_post, v_w_alpha, v_b_alpha, v_g_gla, v_g_att, v_rel_bias):
    nl, d, cols = w_in.shape
    rows = w_out.shape[1]
    s = x.shape[1]
    x0 = x.reshape(s, d)
    tgt = loss_target.reshape(s, d)

    cols_first = lambda a: jnp.transpose(a, (2, 0, 1))
    w_c = cols_first(w_in)
    slab = _slab_rows(rows, cols)
    is_out = lax.broadcasted_iota(jnp.int32, (slab, d), 0) < rows

    def shard(l, zero=0.0):
        top = jnp.pad((w_out[l] + zero).astype(BF16), ((0, slab - rows), (0, 0)))
        rest = jnp.pad((w_c[:, l] + zero).astype(BF16), ((rows, slab - rows - cols), (0, 0)))
        return jnp.where(is_out, top, rest)

    first_fetch = _exchange_start(shard(0), x, GATHER_PEERS, False, "gather_start_0")
    began = first_fetch[4][0, 0]
    shards = [None] + [shard(l, began) for l in range(1, nl)]
    alpha = _pack([w_alpha]) + began
    wa_g = _share(alpha, "gather_alpha")
    wa_cols = w_alpha.shape[2]
    wa_full = wa_g.reshape(N_DEV, -1)[:, :nl * GLA_RANK * wa_cols].reshape(N_DEV, nl, GLA_RANK, wa_cols)
    wa_full = jnp.transpose(wa_full, (1, 2, 0, 3)).reshape(nl, GLA_RANK, GLA_KW)
    wa_pad = jnp.pad(wa_full, ((0, 0), (0, LANE - GLA_RANK), (0, 0)))
    rb_pad = jnp.pad(rel_bias, ((0, 0), (0, 0), (0, 3 * LANE - N_REL)))

    def layer_args(l, follows=None):
        gp = g_pre[l:l + 1] if follows is None else g_pre[l:l + 1] + follows[:1, :1]
        return (wts[l], wos[l], gp, g_post[l:l + 1], wa_pad[l], b_alpha[l:l + 1], g_gla[l:l + 1],
                g_att[l:l + 1], rb_pad[l])

    my = _slot(_position())

    def fetch(l, after):
        return _exchange_start(shards[l], after, GATHER_PEERS, False, f"gather_start_{l}")

    def relay(l, first_hop, after):
        own[l], land = _exchange_wait(first_hop[:4], after, GATHER_PEERS, False, f"gather_wait_{l}")
        return _relay_start(land, f"relay_start_{l}")

    def midway(l, y):
        flight["relay"] = relay(l + 1, flight["fetch"], y)
        if l + 2 >= nl:
            return flight["relay"][3]
        flight["fetch"] = fetch(l + 2, flight["relay"][2])
        return flight["fetch"][4]

    act, saved, wts, wos, flight, own = x0, [], [], [], {}, [None] * nl
    prepared = (wa_pad[0, :1, :1] + sum(sh[:1, :1].astype(F32) for sh in shards[1:]))
    flight["relay"] = relay(0, first_fetch, prepared)
    if nl > 1:
        flight["fetch"] = fetch(1, flight["relay"][2])
    for l in range(nl):
        land = _relay_wait(flight["relay"][:3], act, f"relay_wait_{l}")
        land = lax.dynamic_update_slice_in_dim(land, own[l][None], my, 0)
        wt_l, wo_l = _aligned_weight(land, rows, cols)
        wts.append(wt_l)
        wos.append(wo_l)
        act, sv = _layer_fwd(act, *layer_args(l, follows=first_fetch[4] if l == 0 else None),
                             midway=functools.partial(midway, l) if l + 1 < nl else None)
        saved.append(sv)
    dout, sq = _loss_head(act, tgt)
    loss = lax.psum(sq[0, 0] * (0.5 / d), ("x", "y", "c"))

    smalls, pending_out, pending_in = [None] * nl, [None] * nl, [None] * nl

    def send_out(l, dwo):
        pending_out[l] = _exchange_start(dwo.reshape(N_DEV, rows, d), dwo[:1, :1], ALL_PEERS, True,
                                         f"scatter_out_start_{l}")
        return pending_out[l][4]

    def send_in(l, dwt):
        if l > 0:
            pending_in[l] = _exchange_start(_partial_slabs(dwt, cols), dwt[-1], ALL_PEERS, True,
                                            f"scatter_in_start_{l}")
            return pending_in[l][4]
        pair = _exchange_start(_partial_slabs(dwt, cols, by_core=True), dwt[-1], (1,), True,
                               "pair_start_0", slots=BY_CORE)
        by_core, from_sibling = _exchange_wait(pair[:4], pair[4], (1,), True, "pair_wait_0", slots=BY_CORE)
        pending_in[l] = _exchange_start(_pair_sum(by_core, from_sibling), dwt[-1], GATHER_PEERS[1:], True,
                                        "scatter_in_start_0", slots=BY_CHIP)
        return pending_in[l][4]

    for l in reversed(range(nl)):
        dout, smalls[l] = _layer_bwd(dout, saved[l], *layer_args(l), on_dwo=functools.partial(send_out, l),
                                     on_dwt=functools.partial(send_in, l))
    grad_x = dout.reshape(x.shape)

    def landed(started, after, name, ks=ALL_PEERS, slots=BY_DEVICE):
        partial, land = _exchange_wait(started[:4], after, ks, True, name, slots=slots)
        mine = slots[0](_position())
        return lax.dynamic_update_slice_in_dim(land, lax.dynamic_slice_in_dim(partial, mine, 1, 0), mine, 0)

    parts_out = [landed(pending_out[l], dout, f"scatter_out_wait_{l}") for l in range(nl)]
    g_w_out, d_w_out, m2_w_out, v2_w_out = _adam_sharded(parts_out, 0, w_out, m_w_out, v_w_out, "adam_w_out")
    names = 7
    small_stacked = [jnp.stack([smalls[l][i] for l in range(nl)]) for i in range(names)]
    shapes = [a.shape for a in small_stacked]
    gathered = _share(_pack(small_stacked), "gather_small_grads", after=d_w_out)
    g_pre_g, g_post_g, wa_g_full, b_g, gla_g, att_g, rb_g = _unpack(_sum_slots(gathered), shapes)
    wa_g_mine = lax.dynamic_slice_in_dim(wa_g_full, my * wa_cols, wa_cols, axis=2)
    grads = [g_pre_g, g_post_g, wa_g_mine, b_g, gla_g, att_g, rb_g]
    ws = [g_pre, g_post, w_alpha, b_alpha, g_gla, g_att, rel_bias]
    ms = [m_g_pre, m_g_post, m_w_alpha, m_b_alpha, m_g_gla, m_g_att, m_rel_bias]
    vs = [v_g_pre, v_g_post, v_w_alpha, v_b_alpha, v_g_gla, v_g_att, v_rel_bias]
    d_s, m2_s, v2_s = _adam_small(ws, grads, ms, vs)

    parts_in = [landed(pending_in[0], d_s[0], "scatter_in_wait_0", GATHER_PEERS[1:], BY_CHIP)]
    parts_in += [landed(pending_in[l], d_s[0], f"scatter_in_wait_{l}") for l in range(1, nl)]
    g_w_in, d_w_in, m2_w_in, v2_w_in = [
        jnp.transpose(a, (1, 2, 0))
        for a in _adam_columns(parts_in, 0, w_c, cols_first(m_w_in), cols_first(v_w_in))]

    def ordered(big_in, big_out, small):
        return [big_in, big_out] + list(small)

    return (loss, grad_x,
            *ordered(g_w_in, g_w_out, grads),
            *ordered(d_w_in, d_w_out, d_s),
            *ordered(m2_w_in, m2_w_out, m2_s),
            *ordered(v2_w_in, v2_w_out, v2_s))
```

```python
import functools

import jax
import jax.numpy as jnp
from jax import lax
from jax.experimental import pallas as pl
from jax.experimental.pallas import tpu as pltpu

F32 = jnp.float32
BF16 = jnp.bfloat16
MESH = pl.DeviceIdType.MESH
ANY = pl.BlockSpec(memory_space=pl.ANY)

CHUNK = 64
GLA_HEADS = 4
GLA_DK = 128
GLA_DV = 256
GLA_KW = GLA_HEADS * GLA_DK
D_GLA = GLA_HEADS * GLA_DV
GLA_RANK = 16
GLA_TAU = 16.0
ATT_HEADS = 8
ATT_HD = 128
D_ATT = ATT_HEADS * ATT_HD
LEFT_CHUNKS = 8
REL_CLIP = 128
N_REL = 2 * REL_CLIP + 1
EPS = 1e-6
D_IN = 2 * GLA_KW + 2 * D_GLA + GLA_RANK + 4 * D_ATT
GLA_SCALE = GLA_DK ** -0.5
ATT_SCALE = ATT_HD ** -0.5

ADAM_LR = 0.001
ADAM_B1 = 0.9
ADAM_B2 = 0.999
ADAM_EPS = 1e-08
ADAM_WD = 0.01
ADAM_STEP = 10

N_DEV = 8
LANE = 128
GA_ORIG = 2 * GLA_KW + 2 * D_GLA
OFF_AQ = GA_ORIG
OFF_GA = GA_ORIG + 4 * D_ATT
D_ZP = OFF_GA + LANE
QB = 2 * CHUNK
ATT_UNROLL = 8
WIN = (LEFT_CHUNKS + 2) * CHUNK
ET_ROWS = WIN + LEFT_CHUNKS * CHUNK
NEG = -1e30
VMEM_LIMIT = 48 * 1024 * 1024


def _cparams(sem):
    return pltpu.CompilerParams(dimension_semantics=sem, vmem_limit_bytes=VMEM_LIMIT)


def _dot(a, b):
    return jnp.dot(a, b, preferred_element_type=F32)


def _dot_nt(a, b):
    return lax.dot_general(a, b, (((1,), (1,)), ((), ())), preferred_element_type=F32)


def _dot_tn(a, b):
    return lax.dot_general(a, b, (((0,), (0,)), ((), ())), preferred_element_type=F32)


def _dot01(t, x, left=True):
    if not left:
        t, x = x, t
    hi = x.astype(BF16)
    r = x - hi.astype(F32)
    mid = r.astype(BF16)
    lo = (r - mid.astype(F32)).astype(BF16)
    if left:
        return _dot(t, hi) + _dot(t, mid) + _dot(t, lo)
    return _dot(hi, t) + _dot(mid, t) + _dot(lo, t)


def _sigmoid(x):
    return 1.0 / (1.0 + jnp.exp(-x))


def _log_sigmoid(x):
    return jnp.minimum(x, 0.0) - jnp.log(1.0 + jnp.exp(-jnp.abs(x)))


TILES = {
    "in_proj": (512, D_ZP // 3, None),
    "in_proj_dx": (512, 512, None),
    "in_proj_dw": (512, 2048, None),
    "out_proj": (512, 1024, None),
    "out_proj_dx": (512, 1024, None),
    "out_proj_dw": (1024, 1024, None),
}


def _matmul(a, b, mode, out_dtype, tm, tn, tk, name, n_outer=False, after=None):
    if mode == "nn":
        (m, k), n = a.shape, b.shape[1]
    elif mode == "nt":
        (m, k), n = a.shape, b.shape[0]
    else:
        (k, m), n = a.shape, b.shape[1]
    tm, tn, tk = min(tm, m), min(tn, n), k if tk is None else min(tk, k)
    assert m % tm == 0 and n % tn == 0 and k % tk == 0, (name, m, n, k)
    nk = k // tk
    dot = {"nn": _dot, "nt": _dot_nt, "tn": _dot_tn}[mode]

    follows = [] if after is None else [after]

    def body_whole_k(a_ref, b_ref, *rest):
        o_ref = rest[-1]
        o_ref[...] = dot(a_ref[...], b_ref[...]).astype(out_dtype)

    def body(a_ref, b_ref, *rest):
        o_ref, acc_ref = rest[-2:]
        kk = pl.program_id(2)

        @pl.when(kk == 0)
        def _():
            acc_ref[...] = jnp.zeros_like(acc_ref)

        acc_ref[...] += dot(a_ref[...], b_ref[...])

        @pl.when(kk == nk - 1)
        def _():
            o_ref[...] = acc_ref[...].astype(out_dtype)

    def at(index):
        return (lambda j, i, kk: index(i, j, kk)) if n_outer else index

    if mode == "tn":
        a_spec = pl.BlockSpec((tk, tm), at(lambda i, j, kk: (kk, i)))
    else:
        a_spec = pl.BlockSpec((tm, tk), at(lambda i, j, kk: (i, kk)))
    if mode == "nt":
        b_spec = pl.BlockSpec((tn, tk), at(lambda i, j, kk: (j, kk)))
    else:
        b_spec = pl.BlockSpec((tk, tn), at(lambda i, j, kk: (kk, j)))
    return pl.pallas_call(
        body_whole_k if nk == 1 else body, name=name,
        grid=(n // tn, m // tm, nk) if n_outer else (m // tm, n // tn, nk),
        in_specs=[a_spec, b_spec] + [ANY] * len(follows),
        out_specs=pl.BlockSpec((tm, tn), at(lambda i, j, kk: (i, j))),
        out_shape=jax.ShapeDtypeStruct((m, n), out_dtype),
        scratch_shapes=[] if nk == 1 else [pltpu.VMEM((tm, tn), F32)],
        compiler_params=_cparams(("parallel", "parallel", "arbitrary")),
    )(a, b, *follows)


def _matmul_cols(pieces, b, out_dtype, tm, tn, name, after=None):
    m, n = pieces[0].shape[0], b.shape[1]
    widths = [p.shape[1] for p in pieces]
    starts = [sum(widths[:i]) for i in range(len(pieces))]
    follows = [] if after is None else [after]
    tm, tn = min(tm, m), min(tn, n)
    assert sum(widths) == b.shape[0] and m % tm == 0 and n % tn == 0, name

    def body(*refs):
        b_ref, o_ref = refs[len(pieces)], refs[-1]
        acc = None
        for p_ref, at, width in zip(refs, starts, widths):
            part = _dot(p_ref[...], b_ref[at:at + width, :])
            acc = part if acc is None else acc + part
        o_ref[...] = acc.astype(out_dtype)

    return pl.pallas_call(
        body, name=name, grid=(n // tn, m // tm),
        in_specs=[pl.BlockSpec((tm, width), lambda j, i: (i, 0)) for width in widths]
        + [pl.BlockSpec((b.shape[0], tn), lambda j, i: (0, j))] + [ANY] * len(follows),
        out_specs=pl.BlockSpec((tm, tn), lambda j, i: (i, j)),
        out_shape=jax.ShapeDtypeStruct((m, n), out_dtype),
        compiler_params=_cparams(("parallel", "parallel")),
    )(*pieces, b, *follows)


def _matmul_rows(pieces, b, out_dtype, tw, tn, name):
    k, n = b.shape
    tn = min(tn, n)
    counts = [p.shape[1] // tw for p in pieces]
    firsts = [sum(counts[:i]) for i in range(len(pieces))]
    assert all(p.shape[1] % tw == 0 for p in pieces) and n % tn == 0, name

    def body(*refs):
        b_ref, o_ref = refs[len(pieces):]
        for p_ref, first, count in zip(refs, firsts, counts):
            @pl.when((pl.program_id(0) >= first) & (pl.program_id(0) < first + count))
            def _(p_ref=p_ref):
                o_ref[...] = _dot_tn(p_ref[...], b_ref[...]).astype(out_dtype)

    def piece_spec(first, count):
        return pl.BlockSpec((k, tw), lambda i, j: (0, jnp.clip(i - first, 0, count - 1)))

    return pl.pallas_call(
        body, name=name, grid=(sum(counts), n // tn),
        in_specs=[piece_spec(first, count) for first, count in zip(firsts, counts)]
        + [pl.BlockSpec((k, tn), lambda i, j: (0, j))],
        out_specs=pl.BlockSpec((tw, tn), lambda i, j: (i, j)),
        out_shape=jax.ShapeDtypeStruct((sum(counts) * tw, n), out_dtype),
        compiler_params=_cparams(("parallel", "parallel")),
    )(*pieces, b)


ROWS = 512


def _rms_fwd(x, g):
    s, d = x.shape

    def body(x_ref, g_ref, h_ref):
        xv = x_ref[...]
        r = lax.rsqrt(jnp.mean(xv * xv, axis=-1, keepdims=True) + EPS)
        h_ref[...] = (xv * r * g_ref[...]).astype(BF16)

    return pl.pallas_call(
        body, name="rms_fwd", grid=(s // ROWS,),
        in_specs=[pl.BlockSpec((ROWS, d), lambda i: (i, 0)), pl.BlockSpec((1, d), lambda i: (0, 0))],
        out_specs=pl.BlockSpec((ROWS, d), lambda i: (i, 0)),
        out_shape=jax.ShapeDtypeStruct((s, d), BF16),
        compiler_params=_cparams(("parallel",)),
    )(x, g)


def _post_fwd(x, y, g):
    s, d = x.shape

    def body(x_ref, y_ref, g_ref, o_ref):
        yv = y_ref[...]
        r = lax.rsqrt(jnp.mean(yv * yv, axis=-1, keepdims=True) + EPS)
        o_ref[...] = x_ref[...] + yv * r * g_ref[...]

    row = pl.BlockSpec((ROWS, d), lambda i: (i, 0))
    return pl.pallas_call(
        body, name="post_fwd", grid=(s // ROWS,),
        in_specs=[row, row, pl.BlockSpec((1, d), lambda i: (0, 0))],
        out_specs=row,
        out_shape=jax.ShapeDtypeStruct((s, d), F32),
        compiler_params=_cparams(("parallel",)),
    )(x, y, g)


def _loss_head(out, tgt):
    s, d = out.shape

    def body(o_ref, t_ref, dout_ref, sum_ref):
        @pl.when(pl.program_id(0) == 0)
        def _():
            sum_ref[...] = jnp.zeros_like(sum_ref)

        e = o_ref[...] - t_ref[...]
        dout_ref[...] = e * (1.0 / d)
        sum_ref[...] += jnp.sum(jnp.sum(e * e, axis=1, keepdims=True), axis=0, keepdims=True)

    row = pl.BlockSpec((ROWS, d), lambda i: (i, 0))
    return pl.pallas_call(
        body, name="loss_head", grid=(s // ROWS,),
        in_specs=[row, row],
        out_specs=[row, pl.BlockSpec((1, 1), lambda i: (0, 0))],
        out_shape=[jax.ShapeDtypeStruct((s, d), F32), jax.ShapeDtypeStruct((1, 1), F32)],
        compiler_params=_cparams(("arbitrary",)),
    )(out, tgt)


def _post_bwd(dout, y, g):
    s, d = y.shape

    def body(do_ref, y_ref, g_ref, dy_ref, dg_ref):
        @pl.when(pl.program_id(0) == 0)
        def _():
            dg_ref[...] = jnp.zeros_like(dg_ref)

        yv = y_ref[...]
        dv = do_ref[...]
        r = lax.rsqrt(jnp.mean(yv * yv, axis=-1, keepdims=True) + EPS)
        dg_ref[...] += jnp.sum(dv * yv * r, axis=0, keepdims=True)
        w = dv * g_ref[...]
        dy = r * (w - yv * (r * r) * jnp.mean(w * yv, axis=-1, keepdims=True))
        dy_ref[...] = dy.astype(BF16)

    row = pl.BlockSpec((ROWS, d), lambda i: (i, 0))
    vec = pl.BlockSpec((1, d), lambda i: (0, 0))
    return pl.pallas_call(
        body, name="post_bwd", grid=(s // ROWS,),
        in_specs=[row, row, vec],
        out_specs=[row, vec],
        out_shape=[jax.ShapeDtypeStruct((s, d), BF16), jax.ShapeDtypeStruct((1, d), F32)],
        compiler_params=_cparams(("arbitrary",)),
    )(dout, y, g)


def _pre_bwd(dh, x, g, dout):
    s, d = x.shape

    def body(dh_ref, x_ref, g_ref, do_ref, dx_ref, dg_ref):
        @pl.when(pl.program_id(0) == 0)
        def _():
            dg_ref[...] = jnp.zeros_like(dg_ref)

        xv = x_ref[...]
        dv = dh_ref[...]
        r = lax.rsqrt(jnp.mean(xv * xv, axis=-1, keepdims=True) + EPS)
        dg_ref[...] += jnp.sum(dv * xv * r, axis=0, keepdims=True)
        w = dv * g_ref[...]
        dx_ref[...] = do_ref[...] + r * (w - xv * (r * r) * jnp.mean(w * xv, axis=-1, keepdims=True))

    row = pl.BlockSpec((ROWS, d), lambda i: (i, 0))
    vec = pl.BlockSpec((1, d), lambda i: (0, 0))
    return pl.pallas_call(
        body, name="pre_bwd", grid=(s // ROWS,),
        in_specs=[row, row, vec, row],
        out_specs=[row, vec],
        out_shape=[jax.ShapeDtypeStruct((s, d), F32), jax.ShapeDtypeStruct((1, d), F32)],
        compiler_params=_cparams(("arbitrary",)),
    )(dh, x, g, dout)


GLA_STEP = 4
GLA_ROWS = GLA_STEP * CHUNK
GLA_CHUNKS = [slice(c * CHUNK, (c + 1) * CHUNK) for c in range(GLA_STEP)]


def _chunk_triangles():
    ri = lax.broadcasted_iota(jnp.int32, (GLA_ROWS, GLA_ROWS), 0)
    ci = lax.broadcasted_iota(jnp.int32, (GLA_ROWS, GLA_ROWS), 1)
    same = (ri // CHUNK) == (ci // CHUNK)
    return (jnp.where(same & (ri >= ci), 1.0, 0.0).astype(BF16), jnp.where(same & (ci >= ri), 1.0, 0.0).astype(BF16))


def _per_chunk(fn, like):
    row = lax.broadcasted_iota(jnp.int32, like.shape, 0)
    return [fn((row >= c * CHUNK) & (row < (c + 1) * CHUNK)) for c in range(GLA_STEP)]


def _spread(per_chunk, like):
    row = lax.broadcasted_iota(jnp.int32, like.shape, 0)
    out = per_chunk[-1]
    for c in reversed(range(GLA_STEP - 1)):
        out = jnp.where(row < (c + 1) * CHUNK, per_chunk[c], out)
    return out


def _gla_gate(ga_b, wa_b, b_ref, tri):
    pre = _dot(ga_b, wa_b) + b_ref[...]
    la = _log_sigmoid(pre) * (1.0 / GLA_TAU)
    return pre, _dot01(tri, la)


def _chunk_ends(cum):
    row = lax.broadcasted_iota(jnp.int32, cum.shape, 0)
    return [jnp.sum(jnp.where(row == (c + 1) * CHUNK - 1, cum, 0.0), axis=0, keepdims=True)
            for c in range(GLA_STEP)]


def _heads(width):
    return [slice(h * width, (h + 1) * width) for h in range(GLA_HEADS)]


def _z_specs_gla(rev=None):
    idx = (lambda n: n) if rev is None else rev
    return [
        pl.BlockSpec((GLA_ROWS, GLA_KW), lambda n: (idx(n), 0)),
        pl.BlockSpec((GLA_ROWS, GLA_KW), lambda n: (idx(n), 1)),
        pl.BlockSpec((GLA_ROWS, D_GLA), lambda n: (idx(n), 1)),
        pl.BlockSpec((GLA_ROWS, D_GLA), lambda n: (idx(n), 2)),
        pl.BlockSpec((GLA_ROWS, LANE), lambda n: (idx(n), OFF_GA // LANE)),
    ]


def _gla_fwd(z, wa_pad, b_alpha, g_gla):
    s = z.shape[0]
    nchunk = s // CHUNK

    def body(q_ref, k_ref, v_ref, gg_ref, ga_ref, wa_ref, b_ref, g_ref, y_ref, o_ref, st_ref, pre_ref, cum_ref,
             state):
        @pl.when(pl.program_id(0) == 0)
        def _():
            state[...] = jnp.zeros_like(state)

        ga_b = ga_ref[...].astype(BF16)
        tri, _ = _chunk_triangles()
        nh = range(GLA_HEADS)
        keys, vals = _heads(GLA_DK), _heads(GLA_DV)
        pre, cum = _gla_gate(ga_b, wa_ref[...].astype(BF16), b_ref, tri)
        pre_ref[...] = pre
        cum_ref[...] = cum
        cends = _chunk_ends(cum)
        kd_b = (k_ref[...] * jnp.exp(_spread(cends, cum) - cum)).astype(BF16)
        qs = (q_ref[...] * GLA_SCALE).astype(BF16)
        v_b = v_ref[...].astype(BF16)
        uts = [[_dot_tn(v_b[rs, vals[h]], kd_b[rs, keys[h]]) for h in nh] for rs in GLA_CHUNKS]
        sts, prev = [], [state[h] for h in nh]
        for c in range(GLA_STEP):
            a = jnp.exp(cends[c])
            prev = [prev[h] * a[:, keys[h]] + uts[c][h] for h in nh]
            sts.append(prev)
        for h in nh:
            state[h] = prev[h]
            for c in range(GLA_STEP):
                st_ref[c, h] = sts[c][h]
        outs = [[_dot_nt(qs[rs, keys[h]], sts[c][h].astype(BF16)) for h in nh] for c, rs in enumerate(GLA_CHUNKS)]
        for h in nh:
            o, vs = jnp.concatenate([outs[c][h] for c in range(GLA_STEP)], axis=0), vals[h]
            o_ref[:, vs] = o
            r = lax.rsqrt(jnp.mean(o * o, axis=-1, keepdims=True) + EPS)
            gg = gg_ref[:, vs]
            y_ref[:, vs] = (o * r * g_ref[:, vs] * (gg * _sigmoid(gg))).astype(BF16)

    full = lambda shape: pl.BlockSpec(shape, lambda n: tuple(0 for _ in shape))
    wide = pl.BlockSpec((GLA_ROWS, D_GLA), lambda n: (n, 0))
    return pl.pallas_call(
        body, name="gla_fwd", grid=(nchunk // GLA_STEP,),
        in_specs=_z_specs_gla() + [full((LANE, GLA_KW)), full((1, GLA_KW)), full((1, D_GLA))],
        out_specs=[wide, wide, pl.BlockSpec((GLA_STEP, GLA_HEADS, GLA_DV, GLA_DK), lambda n: (n, 0, 0, 0)),
                   pl.BlockSpec((GLA_ROWS, GLA_KW), lambda n: (n, 0)), pl.BlockSpec((GLA_ROWS, GLA_KW), lambda n: (n, 0))],
        out_shape=[jax.ShapeDtypeStruct((s, D_GLA), BF16), jax.ShapeDtypeStruct((s, D_GLA), F32),
                   jax.ShapeDtypeStruct((nchunk, GLA_HEADS, GLA_DV, GLA_DK), F32),
                   jax.ShapeDtypeStruct((s, GLA_KW), F32), jax.ShapeDtypeStruct((s, GLA_KW), F32)],
        scratch_shapes=[pltpu.VMEM((GLA_HEADS, GLA_DV, GLA_DK), F32)],
        compiler_params=_cparams(("arbitrary",)),
    )(z, z, z, z, z, wa_pad, b_alpha, g_gla)


def _gla_bwd(dyc, o_gla, z, wa_pad, g_gla, states, gate_pre, gate_cum):
    s = z.shape[0]
    nsteps = s // GLA_ROWS
    rev = lambda n: nsteps - 1 - n

    def body(dy_ref, o_ref, q_ref, k_ref, v_ref, gg_ref, ga_ref, wa_ref, g_ref, st_ref, stp_ref, pre_ref, cum_ref,
             dq_ref, dk_ref, dv_ref, dgg_ref, dga_ref, dwa_ref, db_ref, dg_ref, carry):
        step = pl.program_id(0)

        @pl.when(step == 0)
        def _():
            carry[...] = jnp.zeros_like(carry)
            dwa_ref[...] = jnp.zeros_like(dwa_ref)
            db_ref[...] = jnp.zeros_like(db_ref)
            dg_ref[...] = jnp.zeros_like(dg_ref)

        has_prev = (step < nsteps - 1).astype(F32)
        ga_b = ga_ref[...].astype(BF16)
        _, tri_up = _chunk_triangles()
        nh, nc = range(GLA_HEADS), range(GLA_STEP)
        keys, vals = _heads(GLA_DK), _heads(GLA_DV)
        wa_b = wa_ref[...].astype(BF16)
        pre, cum = pre_ref[...], cum_ref[...]
        cends = _chunk_ends(cum)
        e = jnp.exp(_spread(cends, cum) - cum)
        a = [jnp.exp(cends[c]) for c in nc]
        kf = k_ref[...]
        kd_b = (kf * e).astype(BF16)
        v_b = v_ref[...].astype(BF16)
        qs = (q_ref[...] * GLA_SCALE).astype(BF16)
        do_b = []
        for h in nh:
            vs = vals[h]
            o = o_ref[:, vs]
            gg = gg_ref[:, vs]
            g = g_ref[:, vs]
            dy = dy_ref[:, vs]
            r = lax.rsqrt(jnp.mean(o * o, axis=-1, keepdims=True) + EPS)
            sg = _sigmoid(gg)
            dogn = dy * (gg * sg)
            dgg_ref[:, vs] = (dy * (o * r * g) * (sg * (1.0 + gg * (1.0 - sg)))).astype(BF16)
            dg_ref[:, vs] += jnp.sum(dogn * o * r, axis=0, keepdims=True)
            w = dogn * g
            do_b.append((r * (w - o * (r * r) * jnp.mean(w * o, axis=-1, keepdims=True))).astype(BF16))
        dqs = [jnp.concatenate([_dot(do_b[h][rs], st_ref[c, h].astype(BF16)) for c, rs in enumerate(GLA_CHUNKS)],
                               axis=0) for h in nh]
        dq_ref[...] = (jnp.concatenate(dqs, axis=1) * GLA_SCALE).astype(BF16)
        own = [[_dot_tn(do_b[h][rs], qs[rs, keys[h]]) for h in nh] for rs in GLA_CHUNKS]
        gts, later = [None] * GLA_STEP, [carry[h] for h in nh]
        for c in reversed(nc):
            gts[c] = [own[c][h] + later[h] for h in nh]
            later = [gts[c][h] * a[c][:, keys[h]] for h in nh]
        for h in nh:
            carry[h] = later[h]
        gt_b = [[gts[c][h].astype(BF16) for h in nh] for c in nc]
        dkd = jnp.concatenate([jnp.concatenate([_dot(v_b[rs, vals[h]], gt_b[c][h]) for h in nh], axis=1)
                               for c, rs in enumerate(GLA_CHUNKS)], axis=0)
        dvs = [[_dot_nt(kd_b[rs, keys[h]], gt_b[c][h]) for h in nh] for c, rs in enumerate(GLA_CHUNKS)]
        before = lambda c, h: st_ref[c - 1, h] if c > 0 else stp_ref[0, h] * has_prev
        da = [jnp.concatenate([jnp.sum(gts[c][h] * before(c, h), axis=0, keepdims=True) for h in nh], axis=1)
              for c in nc]
        for h in nh:
            dv_ref[:, vals[h]] = jnp.concatenate([dvs[c][h] for c in nc], axis=0).astype(BF16)
        dk_ref[...] = (dkd * e).astype(BF16)
        dd = dkd * kf * e
        dsum = _per_chunk(lambda mine: jnp.sum(jnp.where(mine, dd, 0.0), axis=0, keepdims=True), dd)
        dcend = _spread([dsum[c] + da[c] * a[c] for c in nc], dd)
        dla = dcend - _dot01(tri_up, dd)
        dpre = dla * (1.0 / GLA_TAU) * (1.0 - _sigmoid(pre))
        dpre_b = dpre.astype(BF16)
        dga_ref[...] = _dot_nt(dpre_b, wa_b).astype(BF16)
        dwa_ref[...] += _dot_tn(ga_b, dpre_b)
        db_ref[...] += jnp.sum(dpre, axis=0, keepdims=True)

    full = lambda shape: pl.BlockSpec(shape, lambda n: tuple(0 for _ in shape))
    wide = pl.BlockSpec((GLA_ROWS, D_GLA), lambda n: (rev(n), 0))
    keyw = pl.BlockSpec((GLA_ROWS, GLA_KW), lambda n: (rev(n), 0))
    st_spec = pl.BlockSpec((GLA_STEP, GLA_HEADS, GLA_DV, GLA_DK), lambda n: (rev(n), 0, 0, 0))
    stp_spec = pl.BlockSpec((1, GLA_HEADS, GLA_DV, GLA_DK),
                            lambda n: (jnp.maximum(GLA_STEP * rev(n) - 1, 0), 0, 0, 0))
    return pl.pallas_call(
        body, name="gla_bwd", grid=(nsteps,),
        in_specs=[wide, wide] + _z_specs_gla(rev)
        + [full((LANE, GLA_KW)), full((1, D_GLA)), st_spec, stp_spec, keyw, keyw],
        out_specs=[keyw, keyw, wide, wide, pl.BlockSpec((GLA_ROWS, LANE), lambda n: (rev(n), 0)),
                   full((LANE, GLA_KW)), full((1, GLA_KW)), full((1, D_GLA))],
        out_shape=[jax.ShapeDtypeStruct((s, GLA_KW), BF16), jax.ShapeDtypeStruct((s, GLA_KW), BF16),
                   jax.ShapeDtypeStruct((s, D_GLA), BF16), jax.ShapeDtypeStruct((s, D_GLA), BF16),
                   jax.ShapeDtypeStruct((s, LANE), BF16),
                   jax.ShapeDtypeStruct((LANE, GLA_KW), F32), jax.ShapeDtypeStruct((1, GLA_KW), F32),
                   jax.ShapeDtypeStruct((1, D_GLA), F32)],
        scratch_shapes=[pltpu.VMEM((GLA_HEADS, GLA_DV, GLA_DK), F32)],
        compiler_params=_cparams(("arbitrary",)),
    )(dyc, o_gla, z, z, z, z, z, wa_pad, g_gla, states, states, gate_pre, gate_cum)


def _build_bias_table(rb_row, et_ref):
    far = jnp.broadcast_to(rb_row[:, 2 * REL_CLIP:2 * REL_CLIP + 1], (1, LANE))
    near_hi = rb_row[:, REL_CLIP:2 * REL_CLIP]
    near_lo = rb_row[:, 0:REL_CLIP]
    past = jnp.broadcast_to(rb_row[:, 0:1], (1, LANE))
    seg = [far, far, far, far, near_hi, near_lo] + [past] * (ET_ROWS // LANE - 5)
    ri = lax.broadcasted_iota(jnp.int32, (LANE, LANE), 0)
    ci = lax.broadcasted_iota(jnp.int32, (LANE, LANE), 1)
    for kb in range(ET_ROWS // LANE):
        wmat = jnp.where(ri + ci < LANE, seg[kb], seg[kb + 1])
        blk = pltpu.roll(wmat, 0, 1, stride=1, stride_axis=0)
        lag = LEFT_CHUNKS + ci // CHUNK - (2 * kb + ri // CHUNK)
        et_ref[kb * LANE:(kb + 1) * LANE, :] = jnp.where((lag >= 0) & (lag <= LEFT_CHUNKS), blk, NEG)


def _reduce_bias_table(det_ref):
    lane = lax.broadcasted_iota(jnp.int32, (1, LANE), 1)
    ri = lax.broadcasted_iota(jnp.int32, (LANE, LANE), 0)
    ci = lax.broadcasted_iota(jnp.int32, (LANE, LANE), 1)
    flip = jnp.where(ri + ci == LANE - 1, 1.0, 0.0).astype(BF16)
    segs = jnp.zeros((8, LANE), F32)
    seg_row = lax.broadcasted_iota(jnp.int32, (8, LANE), 0)
    prev_minus = jnp.zeros((1, LANE), F32)
    for kb in range(6):
        rolled = pltpu.roll(_dot01(det_ref[kb * LANE:(kb + 1) * LANE, :], flip, left=False), 0, 1,
                            stride=1, stride_axis=0)
        plus = jnp.sum(jnp.where(ci >= ri, rolled, 0.0), axis=0, keepdims=True)
        minus = jnp.sum(jnp.where(ci < ri, rolled, 0.0), axis=0, keepdims=True)
        segs = segs + jnp.where(seg_row == kb, plus + prev_minus, 0.0)
        prev_minus = minus
    segs = _dot01(segs, flip, left=False)
    pick = lambda kb: jnp.sum(jnp.where(seg_row == kb, segs, 0.0), axis=0, keepdims=True)
    far = jnp.sum(pick(0) + pick(1) + pick(2) + pick(3), axis=1, keepdims=True)
    last = jnp.where(lane == 0, far, 0.0)
    return jnp.concatenate([pick(5), pick(4), last], axis=1)


def _att_window(b):
    c0 = 2 * b
    kstart = pl.multiple_of(jnp.maximum(c0 - LEFT_CHUNKS, 0) * CHUNK, CHUNK)
    eoff = pl.multiple_of(jnp.maximum(LEFT_CHUNKS - c0, 0) * CHUNK, CHUNK)
    return kstart, eoff


def _att_probs(q_b, kw_b, et):
    st = _dot_nt(kw_b, q_b) * ATT_SCALE + et
    m = jnp.max(st, axis=0, keepdims=True)
    ex = jnp.exp(st - m)
    return ex * (1.0 / jnp.sum(ex, axis=0, keepdims=True))


def _att_fwd(z, rb_pad, g_att):
    s = z.shape[0]
    nblk = s // QB
    c_aq, c_ak, c_av, c_ag = [(OFF_AQ + i * D_ATT) // ATT_HD for i in range(4)]

    def body(q_ref, k_ref, v_ref, ag_ref, rb_ref, g_ref, y_ref, o_ref, p_ref, et_ref, kb_ref, vb_ref):
        h = pl.program_id(0)
        b = pl.program_id(1)

        @pl.when(b == 0)
        def _():
            _build_bias_table(rb_ref[pl.ds(h, 1), :], et_ref)
            kb_ref[...] = k_ref[...].astype(BF16)
            vb_ref[...] = v_ref[...].astype(BF16)

        for j in range(ATT_UNROLL):
            rs = slice(j * QB, (j + 1) * QB)
            kstart, eoff = _att_window(b * ATT_UNROLL + j)
            q_b = q_ref[rs, :].astype(BF16)
            kw_b = kb_ref[pl.ds(kstart, WIN), :]
            vw_b = vb_ref[pl.ds(kstart, WIN), :]
            pt = _att_probs(q_b, kw_b, et_ref[pl.ds(eoff, WIN), :])
            p_ref[0, j] = pt
            o = _dot_tn(pt.astype(BF16), vw_b)
            o_ref[rs, :] = o
            r = lax.rsqrt(jnp.mean(o * o, axis=-1, keepdims=True) + EPS)
            ag = ag_ref[rs, :]
            y_ref[rs, :] = (o * r * g_ref[...] * (ag * _sigmoid(ag))).astype(BF16)

    blk = lambda col: pl.BlockSpec((ATT_UNROLL * QB, ATT_HD), lambda h, b: (b, col + h))
    seq = lambda col: pl.BlockSpec((s, ATT_HD), lambda h, b: (0, col + h))
    out_blk = pl.BlockSpec((ATT_UNROLL * QB, ATT_HD), lambda h, b: (b, h))
    return pl.pallas_call(
        body, name="att_fwd", grid=(ATT_HEADS, nblk // ATT_UNROLL),
        in_specs=[blk(c_aq), seq(c_ak), seq(c_av), blk(c_ag),
                  pl.BlockSpec((ATT_HEADS, 3 * LANE), lambda h, b: (0, 0)),
                  pl.BlockSpec((1, ATT_HD), lambda h, b: (0, h))],
        out_specs=[out_blk, out_blk, pl.BlockSpec((1, ATT_UNROLL, WIN, QB), lambda h, b: (h, b, 0, 0))],
        out_shape=[jax.ShapeDtypeStruct((s, D_ATT), BF16), jax.ShapeDtypeStruct((s, D_ATT), F32),
                   jax.ShapeDtypeStruct((ATT_HEADS, nblk, WIN, QB), F32)],
        scratch_shapes=[pltpu.VMEM((ET_ROWS, LANE), F32), pltpu.VMEM((s, ATT_HD), BF16),
                        pltpu.VMEM((s, ATT_HD), BF16)],
        compiler_params=_cparams(("arbitrary", "arbitrary")),
    )(z, z, z, z, rb_pad, g_att)


def _att_bwd(dyc, o_att, probs, z, g_att):
    s = z.shape[0]
    nblk = s // QB
    c_aq, c_ak, c_av, c_ag = [(OFF_AQ + i * D_ATT) // ATT_HD for i in range(4)]
    c_dy = D_GLA // ATT_HD

    def body(dy_ref, o_ref, p_ref, q_ref, k_ref, v_ref, ag_ref, g_ref,
             dq_ref, dk_ref, dv_ref, dag_ref, drb_ref, dg_ref, det_ref, kb_ref, vb_ref, dk_acc, dv_acc):
        b = pl.program_id(1)

        @pl.when(b == 0)
        def _():
            kb_ref[...] = k_ref[...].astype(BF16)
            vb_ref[...] = v_ref[...].astype(BF16)
            det_ref[...] = jnp.zeros_like(det_ref)
            dk_acc[...] = jnp.zeros_like(dk_acc)
            dv_acc[...] = jnp.zeros_like(dv_acc)
            dg_ref[...] = jnp.zeros_like(dg_ref)

        g = g_ref[...]
        dg = jnp.zeros((1, ATT_HD), F32)
        for j in range(ATT_UNROLL):
            rs = slice(j * QB, (j + 1) * QB)
            kstart, eoff = _att_window(b * ATT_UNROLL + j)
            q_b = q_ref[rs, :].astype(BF16)
            kw_b = kb_ref[pl.ds(kstart, WIN), :]
            vw_b = vb_ref[pl.ds(kstart, WIN), :]
            pt = p_ref[0, j]
            o = o_ref[rs, :]
            ag = ag_ref[rs, :]
            dy = dy_ref[rs, :]
            r = lax.rsqrt(jnp.mean(o * o, axis=-1, keepdims=True) + EPS)
            sg = _sigmoid(ag)
            don = dy * (ag * sg)
            dag_ref[rs, :] = (dy * (o * r * g) * (sg * (1.0 + ag * (1.0 - sg)))).astype(BF16)
            dg = dg + jnp.sum(don * o * r, axis=0, keepdims=True)
            w = don * g
            do_b = (r * (w - o * (r * r) * jnp.mean(w * o, axis=-1, keepdims=True))).astype(BF16)
            pt_b = pt.astype(BF16)
            dpt = _dot_nt(vw_b, do_b)
            dst = pt * (dpt - jnp.sum(dpt * pt, axis=0, keepdims=True))
            det_ref[pl.ds(eoff, WIN), :] += dst
            ds_b = (dst * ATT_SCALE).astype(BF16)
            dq_ref[rs, :] = _dot_tn(ds_b, kw_b).astype(BF16)
            dk_acc[pl.ds(kstart, WIN), :] += _dot(ds_b, q_b)
            dv_acc[pl.ds(kstart, WIN), :] += _dot(pt_b, do_b)
        dg_ref[...] += dg

        @pl.when(b == nblk // ATT_UNROLL - 1)
        def _():
            drb_ref[0] = jnp.broadcast_to(_reduce_bias_table(det_ref), (8, 3 * LANE))
            dk_ref[...] = dk_acc[...].astype(BF16)
            dv_ref[...] = dv_acc[...].astype(BF16)

    blk = lambda col: pl.BlockSpec((ATT_UNROLL * QB, ATT_HD), lambda h, b: (b, col + h))
    seq = lambda col: pl.BlockSpec((s, ATT_HD), lambda h, b: (0, col + h))
    out_blk = pl.BlockSpec((ATT_UNROLL * QB, ATT_HD), lambda h, b: (b, h))
    out_seq = pl.BlockSpec((s, ATT_HD), lambda h, b: (0, h))
    return pl.pallas_call(
        body, name="att_bwd", grid=(ATT_HEADS, nblk // ATT_UNROLL),
        in_specs=[blk(c_dy), blk(0), pl.BlockSpec((1, ATT_UNROLL, WIN, QB), lambda h, b: (h, b, 0, 0)),
                  blk(c_aq), seq(c_ak), seq(c_av), blk(c_ag),
                  pl.BlockSpec((1, ATT_HD), lambda h, b: (0, h))],
        out_specs=[out_blk, out_seq, out_seq, out_blk,
                   pl.BlockSpec((1, 8, 3 * LANE), lambda h, b: (h, 0, 0)),
                   pl.BlockSpec((1, ATT_HD), lambda h, b: (0, h))],
        out_shape=[jax.ShapeDtypeStruct((s, D_ATT), BF16), jax.ShapeDtypeStruct((s, D_ATT), BF16),
                   jax.ShapeDtypeStruct((s, D_ATT), BF16), jax.ShapeDtypeStruct((s, D_ATT), BF16),
                   jax.ShapeDtypeStruct((ATT_HEADS, 8, 3 * LANE), F32),
                   jax.ShapeDtypeStruct((1, D_ATT), F32)],
        scratch_shapes=[pltpu.VMEM((ET_ROWS, LANE), F32),
                        pltpu.VMEM((s, ATT_HD), BF16), pltpu.VMEM((s, ATT_HD), BF16),
                        pltpu.VMEM((s, ATT_HD), F32), pltpu.VMEM((s, ATT_HD), F32)],
        compiler_params=_cparams(("arbitrary", "arbitrary")),
    )(dyc, o_att, probs, z, z, z, z, g_att)


ADAM_ROWS = 64
ADAM_COL_ROWS = 32


def _adam_math(w, g, m, v):
    m2 = ADAM_B1 * m + (1.0 - ADAM_B1) * g
    v2 = ADAM_B2 * v + (1.0 - ADAM_B2) * (g * g)
    m_hat = m2 / (1.0 - ADAM_B1 ** ADAM_STEP)
    v_hat = v2 / (1.0 - ADAM_B2 ** ADAM_STEP)
    delta = -ADAM_LR * (m_hat / (jnp.sqrt(v_hat) + ADAM_EPS) + ADAM_WD * w)
    return delta, m2, v2


def _adam_sharded(parts, first, w, m, v, name):
    nl, nr, nc = w.shape

    def body(*refs):
        p_refs = refs[:nl]
        w_ref, m_ref, v_ref, g_ref, d_ref, m2_ref, v2_ref = refs[nl:]
        for k in range(nl):
            @pl.when(pl.program_id(0) == k)
            def _(p_ref=p_refs[k]):
                g = p_ref[0].astype(F32)
                for dev in range(1, N_DEV):
                    g = g + p_ref[dev].astype(F32)
                delta, m2, v2 = _adam_math(w_ref[0], g, m_ref[0], v_ref[0])
                g_ref[0] = g
                d_ref[0] = delta
                m2_ref[0] = m2
                v2_ref[0] = v2

    def part_spec(k):
        return pl.BlockSpec((N_DEV, ADAM_ROWS, nc), lambda l, i: (0, first + jnp.where(l == k, i, 0), 0))

    blk = pl.BlockSpec((1, ADAM_ROWS, nc), lambda l, i: (l, i, 0))
    shp = jax.ShapeDtypeStruct(w.shape, F32)
    return pl.pallas_call(
        body, name=name, grid=(nl, pl.cdiv(nr, ADAM_ROWS)),
        in_specs=[part_spec(k) for k in range(nl)] + [blk, blk, blk],
        out_specs=[blk, blk, blk, blk],
        out_shape=[shp, shp, shp, shp],
        compiler_params=_cparams(("arbitrary", "arbitrary")),
    )(*parts, w, m, v)


def _adam_columns(parts, first, w, m, v):
    nc, nl, d = w.shape

    def body(*refs):
        p_refs = refs[:nl]
        w_ref, m_ref, v_ref, g_ref, d_ref, m2_ref, v2_ref = refs[nl:]
        for l in range(nl):
            g = p_refs[l][0].astype(F32)
            for slot in range(1, parts[l].shape[0]):
                g = g + p_refs[l][slot].astype(F32)
            delta, m2, v2 = _adam_math(w_ref[:, l, :], g, m_ref[:, l, :], v_ref[:, l, :])
            g_ref[:, l, :] = g
            d_ref[:, l, :] = delta
            m2_ref[:, l, :] = m2
            v2_ref[:, l, :] = v2

    blk = pl.BlockSpec((ADAM_COL_ROWS, nl, d), lambda i: (i, 0, 0))
    shp = jax.ShapeDtypeStruct(w.shape, F32)
    return pl.pallas_call(
        body, name="adam_w_in", grid=(pl.cdiv(nc, ADAM_COL_ROWS),),
        in_specs=[pl.BlockSpec((p.shape[0], ADAM_COL_ROWS, d), lambda i: (0, first + i, 0)) for p in parts]
        + [blk, blk, blk],
        out_specs=[blk, blk, blk, blk],
        out_shape=[shp, shp, shp, shp],
        compiler_params=_cparams(("parallel",)),
    )(*parts, w, m, v)


def _adam_small(ws, gs, ms, vs):
    n = len(ws)

    def body(*refs):
        w_refs, g_refs, m_refs, v_refs, d_refs, m2_refs, v2_refs = [refs[i * n:(i + 1) * n] for i in range(7)]
        for i in range(n):
            delta, m2, v2 = _adam_math(w_refs[i][...], g_refs[i][...], m_refs[i][...], v_refs[i][...])
            d_refs[i][...] = delta
            m2_refs[i][...] = m2
            v2_refs[i][...] = v2

    shapes = [jax.ShapeDtypeStruct(w.shape, F32) for w in ws]
    out = pl.pallas_call(body, name="adam_small", out_shape=shapes * 3)(*ws, *gs, *ms, *vs)
    return out[:n], out[n:2 * n], out[2 * n:]


def _position():
    return lax.axis_index("x"), lax.axis_index("y"), lax.axis_index("c")


def _slot(p):
    return 4 * p[0] + 2 * p[1] + p[2]


BF16_TILE_ROWS = 16


def _slab_rows(rows, cols):
    return -(-(rows + cols) // BF16_TILE_ROWS) * BF16_TILE_ROWS


RELAYOUT_COLS = 1024
RELAYOUT_CHUNK = 64


def _shard_pieces(dev, rows, cols):
    moved = ((0, GA_ORIG, 0), (GA_ORIG, GA_ORIG + GLA_RANK, OFF_GA - GA_ORIG), (GA_ORIG + GLA_RANK, D_IN, -GLA_RANK))
    c0, c1 = dev * cols, (dev + 1) * cols
    return [(rows + max(c0, lo) - c0, max(c0, lo) + off, min(c1, hi) - max(c0, lo))
            for lo, hi, off in moved if max(c0, lo) < min(c1, hi)]


def _move_rows(src, src_row, dst, dst_row, n):
    assert src_row % 2 == 0 and dst_row % 2 == 0 and n % 2 == 0
    for r in range(0, n // 2, RELAYOUT_CHUNK):
        m = min(RELAYOUT_CHUNK, n // 2 - r)
        dst[dst_row // 2 + r:dst_row // 2 + r + m, :] = src[src_row // 2 + r:src_row // 2 + r + m, :]


def _aligned_weight(land, rows, cols):
    _, slab, d = land.shape
    ct = min(RELAYOUT_COLS, d)

    def body(land_ref, wt_ref, wo_ref):
        dev = pl.program_id(1)
        src = land_ref.bitcast(jnp.uint32)
        dst = wt_ref.bitcast(jnp.uint32)
        wo_ref[...] = land_ref[0:rows, :]

        @pl.when(dev == 0)
        def _():
            dst[D_IN // 2:D_ZP // 2, :] = jnp.zeros(((D_ZP - D_IN) // 2, ct), jnp.uint32)

        for k in range(N_DEV):
            @pl.when(dev == k)
            def _(k=k):
                for at, to, n in _shard_pieces(k, rows, cols):
                    _move_rows(src, at, dst, to, n)

    return pl.pallas_call(
        body, name="aligned_weight", grid=(d // ct, N_DEV),
        in_specs=[pl.BlockSpec((slab, ct), lambda c, dev: (dev, c))],
        out_specs=[pl.BlockSpec((D_ZP, ct), lambda c, dev: (0, c)),
                   pl.BlockSpec((rows, ct), lambda c, dev: (dev, c))],
        out_shape=[jax.ShapeDtypeStruct((D_ZP, d), land.dtype),
                   jax.ShapeDtypeStruct((N_DEV * rows, d), land.dtype)],
        compiler_params=_cparams(("parallel", "arbitrary")),
    )(land.reshape(N_DEV * slab, d))


def _partial_slabs(dwt, cols, by_core=False):
    d = dwt[0].shape[1]
    bounds = (0, GA_ORIG, OFF_GA, D_ZP)
    assert tuple(a.shape[0] for a in dwt) == tuple(hi - lo for lo, hi in zip(bounds, bounds[1:]))
    slab = _slab_rows(0, cols)
    ct = min(RELAYOUT_COLS, d)

    def body(*refs):
        out_ref = refs[-1]
        dev = pl.program_id(1)
        srcs = [ref.bitcast(jnp.uint32) for ref in refs[:-1]]
        dst = out_ref.bitcast(jnp.uint32)
        dst[cols // 2:slab // 2, :] = jnp.zeros(((slab - cols) // 2, ct), jnp.uint32)
        for k in range(N_DEV):
            @pl.when(dev == k)
            def _(k=k):
                for to, at, n in _shard_pieces(k, 0, cols):
                    which = max(i for i, lo in enumerate(bounds[:-1]) if lo <= at)
                    assert at + n <= bounds[which + 1]
                    _move_rows(srcs[which], at - bounds[which], dst, to, n)

    place = (lambda dev: (dev % 2) * (N_DEV // 2) + dev // 2) if by_core else (lambda dev: dev)
    out = pl.pallas_call(
        body, name="partial_slabs", grid=(d // ct, N_DEV),
        in_specs=[pl.BlockSpec((a.shape[0], ct), lambda c, dev: (0, c)) for a in dwt],
        out_specs=pl.BlockSpec((slab, ct), lambda c, dev: (place(dev), c)),
        out_shape=jax.ShapeDtypeStruct((N_DEV * slab, d), dwt[0].dtype),
        compiler_params=_cparams(("parallel", "arbitrary")),
    )(*dwt)
    return out.reshape((2, N_DEV // 2, slab, d) if by_core else (N_DEV, slab, d))


def _pair_sum(mine, theirs):
    _, nchip, slab, d = mine.shape
    rows = next(r for r in range(512, 0, -BF16_TILE_ROWS) if slab % r == 0)

    def body(m_ref, t_ref, o_ref):
        south = lax.axis_index("c") == 0
        own = jnp.where(south, m_ref[0, 0], m_ref[1, 0]).astype(F32)
        got = jnp.where(south, t_ref[1, 0], t_ref[0, 0]).astype(F32)
        o_ref[0] = (own + got).astype(o_ref.dtype)

    both = pl.BlockSpec((2, 1, rows, d), lambda j, i: (0, j, i, 0))
    return pl.pallas_call(
        body, name="pair_sum", grid=(nchip, slab // rows),
        in_specs=[both, both],
        out_specs=pl.BlockSpec((1, rows, d), lambda j, i: (j, i, 0)),
        out_shape=jax.ShapeDtypeStruct((nchip, slab, d), mine.dtype),
        compiler_params=_cparams(("parallel", "parallel")),
    )(mine, theirs)


def _peer(pos, k):
    x, y, c = pos
    return (1 - x if k & 4 else x, 1 - y if k & 2 else y, 1 - c if k & 1 else c)


HBM_SPEC = pl.BlockSpec(memory_space=pltpu.HBM)
SEM_SPEC = pl.BlockSpec(memory_space=pltpu.SEMAPHORE)
GATHER_PEERS = (1, 4, 2, 6)
ALL_PEERS = (1, 2, 3, 4, 5, 6, 7)


def _hbm(a):
    return pltpu.with_memory_space_constraint(a, pltpu.HBM)


BY_DEVICE = (_slot, N_DEV)
BY_CORE = (lambda p: p[2], 2)
BY_CHIP = (lambda p: 2 * p[0] + p[1], 4)


def _split_copies(src_ref, land_ref, send_sems, recv_sems, ks, per_peer, landed, slots):
    slot_of = slots[0]
    me = _position()
    out = []
    for i, k in enumerate(ks):
        peer = _peer(me, k)
        src = src_ref.at[slot_of(peer)] if per_peer else src_ref
        dst = land_ref.at[slot_of(peer) if landed else slot_of(me)]
        out.append(pltpu.make_async_remote_copy(
            src_ref=src, dst_ref=dst, send_sem=send_sems.at[i], recv_sem=recv_sems.at[i],
            device_id=peer, device_id_type=MESH))
    return out


def _exchange_start(src, after, ks, per_peer, name, slots=BY_DEVICE):
    slab = src.shape[1:] if per_peer else src.shape
    land_shape = (slots[1],) + tuple(slab)
    n = len(ks)

    def body(src_ref, land_ref, after_ref, send_sems, recv_sems, src_thru, land_thru, token):
        for cp in _split_copies(src_ref, land_ref, send_sems, recv_sems, ks, per_peer, False, slots):
            cp.start()
        token[...] = jnp.zeros_like(token)

    return pl.pallas_call(
        body, name=name,
        out_shape=(pltpu.SemaphoreType.DMA((n,)), pltpu.SemaphoreType.DMA((n,)),
                   pltpu.HBM(src.shape, src.dtype), pltpu.HBM(land_shape, src.dtype),
                   jax.ShapeDtypeStruct((8, LANE), F32)),
        in_specs=(HBM_SPEC, HBM_SPEC, ANY),
        out_specs=(SEM_SPEC, SEM_SPEC, HBM_SPEC, HBM_SPEC, pl.BlockSpec(memory_space=pltpu.VMEM)),
        input_output_aliases={0: 2, 1: 3},
        compiler_params=pltpu.CompilerParams(has_side_effects=pltpu.SideEffectType.DATAFLOW_SIDE_EFFECTING),
    )(_hbm(src), _hbm(lax.empty(land_shape, src.dtype)), after)


def _exchange_wait(started, after, ks, per_peer, name, slots=BY_DEVICE):
    send_sems, recv_sems, src_thru, land_thru = started

    def body(src_ref, land_ref, send_sems, recv_sems, after_ref, src_dead, land_out):
        for cp in _split_copies(src_ref, land_ref, send_sems, recv_sems, ks, per_peer, True, slots):
            cp.wait_send()
            cp.wait_recv()

    return pl.pallas_call(
        body, name=name,
        out_shape=(pltpu.HBM(src_thru.shape, src_thru.dtype), pltpu.HBM(land_thru.shape, land_thru.dtype)),
        in_specs=(HBM_SPEC, HBM_SPEC, SEM_SPEC, SEM_SPEC, ANY), out_specs=(HBM_SPEC, HBM_SPEC),
        input_output_aliases={0: 0, 1: 1},
        compiler_params=pltpu.CompilerParams(has_side_effects=pltpu.SideEffectType.DATAFLOW_SIDE_EFFECTING),
    )(src_thru, land_thru, send_sems, recv_sems, after)


def _relay_copies(land_ref, send_sems, recv_sems, landed):
    me = _position()
    sibling = _peer(me, 1)
    out = []
    for i, k in enumerate(GATHER_PEERS[1:]):
        blk = land_ref.at[_slot(_peer(sibling if landed else me, k))]
        out.append(pltpu.make_async_remote_copy(
            src_ref=blk, dst_ref=blk, send_sem=send_sems.at[i], recv_sem=recv_sems.at[i],
            device_id=sibling, device_id_type=MESH))
    return out


def _relay_start(land, name):
    n = len(GATHER_PEERS) - 1

    def body(land_ref, send_sems, recv_sems, land_thru, token):
        for cp in _relay_copies(land_ref, send_sems, recv_sems, landed=False):
            cp.start()
        token[...] = jnp.zeros_like(token)

    return pl.pallas_call(
        body, name=name,
        out_shape=(pltpu.SemaphoreType.DMA((n,)), pltpu.SemaphoreType.DMA((n,)),
                   pltpu.HBM(land.shape, land.dtype), jax.ShapeDtypeStruct((8, LANE), F32)),
        in_specs=(HBM_SPEC,),
        out_specs=(SEM_SPEC, SEM_SPEC, HBM_SPEC, pl.BlockSpec(memory_space=pltpu.VMEM)),
        input_output_aliases={0: 2},
        compiler_params=pltpu.CompilerParams(has_side_effects=pltpu.SideEffectType.DATAFLOW_SIDE_EFFECTING),
    )(_hbm(land))


def _relay_wait(started, after, name):
    send_sems, recv_sems, land_thru = started

    def body(land_ref, send_sems, recv_sems, after_ref, land_out):
        for cp in _relay_copies(land_ref, send_sems, recv_sems, landed=True):
            cp.wait_send()
            cp.wait_recv()

    return pl.pallas_call(
        body, name=name,
        out_shape=pltpu.HBM(land_thru.shape, land_thru.dtype),
        in_specs=(HBM_SPEC, SEM_SPEC, SEM_SPEC, ANY), out_specs=HBM_SPEC,
        input_output_aliases={0: 0},
        compiler_params=pltpu.CompilerParams(has_side_effects=pltpu.SideEffectType.DATAFLOW_SIDE_EFFECTING),
    )(land_thru, send_sems, recv_sems, after)


def _share(vec, name, after=None):
    follows = [] if after is None else [after]

    def body(vec_ref, *rest):
        out_ref, send_sems, recv_sems, local_sem = rest[len(follows):]
        me = _position()

        def copy(k, landed):
            peer = _peer(me, k)
            return pltpu.make_async_remote_copy(
                src_ref=vec_ref, dst_ref=out_ref.at[_slot(peer) if landed else _slot(me)],
                send_sem=send_sems.at[k - 1], recv_sem=recv_sems.at[k - 1], device_id=peer, device_id_type=MESH)

        mine = pltpu.make_async_copy(vec_ref, out_ref.at[_slot(me)], local_sem)
        mine.start()
        sent = [copy(k, False) for k in ALL_PEERS]
        for cp in sent:
            cp.start()
        for k in ALL_PEERS:
            copy(k, True).wait_recv()
        for cp in sent:
            cp.wait_send()
        mine.wait()

    return pl.pallas_call(
        body, name=name,
        in_specs=[ANY] * (1 + len(follows)), out_specs=ANY,
        out_shape=jax.ShapeDtypeStruct((N_DEV,) + vec.shape, vec.dtype),
        scratch_shapes=[pltpu.SemaphoreType.DMA((N_DEV - 1,)), pltpu.SemaphoreType.DMA((N_DEV - 1,)),
                        pltpu.SemaphoreType.DMA],
    )(vec, *follows)


def _sum_slots(parts):
    def body(p_ref, o_ref):
        acc = p_ref[0]
        for dev in range(1, N_DEV):
            acc = acc + p_ref[dev]
        o_ref[...] = acc

    return pl.pallas_call(body, name="sum_slots",
                          out_shape=jax.ShapeDtypeStruct(parts.shape[1:], F32))(parts)


PACK_ROWS = 8


def _packed_rows(size):
    return -(-size // (PACK_ROWS * LANE)) * PACK_ROWS


def _pack(arrs):
    def rows(a):
        flat = a.reshape(-1)
        return jnp.pad(flat, (0, _packed_rows(flat.shape[0]) * LANE - flat.shape[0])).reshape(-1, LANE)

    return jnp.concatenate([rows(a) for a in arrs], axis=0)


def _unpack(packed, shapes):
    out, at = [], 0
    for shp in shapes:
        size = 1
        for dim in shp:
            size *= dim
        nrows = _packed_rows(size)
        out.append(packed[at:at + nrows].reshape(-1)[:size].reshape(shp))
        at += nrows
    return out


def _layer_fwd(x, wt, wo, g_pre, g_post, wa_pad, b_alpha, g_gla, g_att, rb_pad, midway=None):
    h = _rms_fwd(x, g_pre)
    z = _matmul(h, wt, "nt", F32, *TILES["in_proj"], "in_proj", n_outer=True)
    y_gla, o_gla, *gla_kept = _gla_fwd(z, wa_pad, b_alpha, g_gla)
    y_att, o_att, probs = _att_fwd(z, rb_pad, g_att)
    token = None if midway is None else midway(y_att)
    y = _matmul_cols([y_gla, y_att], wo, F32, *TILES["out_proj"][:2], "out_proj", after=token)
    out = _post_fwd(x, y, g_post)
    return out, (x, h, z, o_gla, gla_kept, o_att, probs, y_gla, y_att, y)


def _layer_bwd(dout, saved, wt, wo, g_pre, g_post, wa_pad, b_alpha, g_gla, g_att, rb_pad, on_dwo, on_dwt):
    x, h, z, o_gla, gla_kept, o_att, probs, y_gla, y_att, y = saved
    dy, dg_post = _post_bwd(dout, y, g_post)
    dwo = _matmul_rows([y_gla, y_att], dy, BF16, *TILES["out_proj_dw"][:2], "out_proj_dw")
    token = on_dwo(dwo)
    dycat = _matmul(dy, wo, "nt", F32, *TILES["out_proj_dx"], "out_proj_dx", n_outer=True, after=token)
    dq, dk, dv, dgg, dga, dwa, db, dg_gla = _gla_bwd(dycat, o_gla, z, wa_pad, g_gla, *gla_kept)
    daq, dak, dav, dag, drb, dg_att = _att_bwd(dycat, o_att, probs, z, g_att)
    tw, tn = TILES["in_proj_dw"][:2]
    dwt = (_matmul_rows([dq, dk, dv, dgg], h, BF16, tw, tn, "in_proj_dw_gla"),
           _matmul_rows([daq, dak, dav, dag], h, BF16, tw, tn, "in_proj_dw_att"),
           _matmul_rows([dga], h, BF16, LANE, tn, "in_proj_dw_gate"))
    token = on_dwt(dwt)
    dh = _matmul_cols([dq, dk, dv, dgg, daq, dak, dav, dag, dga], wt, F32, *TILES["in_proj_dx"][:2],
                      "in_proj_dx", after=token)
    dx, dg_pre = _pre_bwd(dh, x, g_pre, dout)
    small = (dg_pre[0], dg_post[0], dwa[:GLA_RANK], db[0], dg_gla[0], dg_att[0], drb[:, 0, :N_REL])
    return dx, small


def kernel(x, w_in, w_out, g_pre, g_post, w_alpha, b_alpha, g_gla, g_att, rel_bias, loss_target, m_w_in, m_w_out, m_g_pre, m_g_post, m_w_alpha, m_b_alpha, m_g_gla, m_g_att, m_rel_bias, v_w_in, v_w_out, v_g_pre, v_g_post, v_w_alpha, v_b_alpha, v_g_gla, v_g_att, v_rel_bias):
    nl, d, cols = w_in.shape
    rows = w_out.shape[1]
    s = x.shape[1]
    x0 = x.reshape(s, d)
    tgt = loss_target.reshape(s, d)

    cols_first = lambda a: jnp.transpose(a, (2, 0, 1))
    w_c = cols_first(w_in)
    slab = _slab_rows(rows, cols)
    is_out = lax.broadcasted_iota(jnp.int32, (slab, d), 0) < rows

    def shard(l, zero=0.0):
        top = jnp.pad((w_out[l] + zero).astype(BF16), ((0, slab - rows), (0, 0)))
        rest = jnp.pad((w_c[:, l] + zero).astype(BF16), ((rows, slab - rows - cols), (0, 0)))
        return jnp.where(is_out, top, rest)

    first_fetch = _exchange_start(shard(0), x, GATHER_PEERS, False, "gather_start_0")
    began = first_fetch[4][0, 0]
    shards = [None] + [shard(l, began) for l in range(1, nl)]
    alpha = _pack([w_alpha]) + began
    wa_g = _share(alpha, "gather_alpha")
    wa_cols = w_alpha.shape[2]
    wa_full = wa_g.reshape(N_DEV, -1)[:, :nl * GLA_RANK * wa_cols].reshape(N_DEV, nl, GLA_RANK, wa_cols)
    wa_full = jnp.transpose(wa_full, (1, 2, 0, 3)).reshape(nl, GLA_RANK, GLA_KW)
    wa_pad = jnp.pad(wa_full, ((0, 0), (0, LANE - GLA_RANK), (0, 0)))
    rb_pad = jnp.pad(rel_bias, ((0, 0), (0, 0), (0, 3 * LANE - N_REL)))

    def layer_args(l, follows=None):
        gp = g_pre[l:l + 1] if follows is None else g_pre[l:l + 1] + follows[:1, :1]
        return (wts[l], wos[l], gp, g_post[l:l + 1], wa_pad[l], b_alpha[l:l + 1], g_gla[l:l + 1],
                g_att[l:l + 1], rb_pad[l])

    my = _slot(_position())

    def fetch(l, after):
        return _exchange_start(shards[l], after, GATHER_PEERS, False, f"gather_start_{l}")

    def relay(l, first_hop, after):
        own[l], land = _exchange_wait(first_hop[:4], after, GATHER_PEERS, False, f"gather_wait_{l}")
        return _relay_start(land, f"relay_start_{l}")

    def midway(l, y):
        flight["relay"] = relay(l + 1, flight["fetch"], y)
        if l + 2 >= nl:
            return flight["relay"][3]
        flight["fetch"] = fetch(l + 2, flight["relay"][2])
        return flight["fetch"][4]

    act, saved, wts, wos, flight, own = x0, [], [], [], {}, [None] * nl
    prepared = (wa_pad[0, :1, :1] + sum(sh[:1, :1].astype(F32) for sh in shards[1:]))
    flight["relay"] = relay(0, first_fetch, prepared)
    if nl > 1:
        flight["fetch"] = fetch(1, flight["relay"][2])
    for l in range(nl):
        land = _relay_wait(flight["relay"][:3], act, f"relay_wait_{l}")
        land = lax.dynamic_update_slice_in_dim(land, own[l][None], my, 0)
        wt_l, wo_l = _aligned_weight(land, rows, cols)
        wts.append(wt_l)
        wos.append(wo_l)
        act, sv = _layer_fwd(act, *layer_args(l, follows=first_fetch[4] if l == 0 else None),
                             midway=functools.partial(midway, l) if l + 1 < nl else None)
        saved.append(sv)
    dout, sq = _loss_head(act, tgt)
    loss = lax.psum(sq[0, 0] * (0.5 / d), ("x", "y", "c"))

    smalls, pending_out, pending_in = [None] * nl, [None] * nl, [None] * nl

    def send_out(l, dwo):
        pending_out[l] = _exchange_start(dwo.reshape(N_DEV, rows, d), dwo[:1, :1], ALL_PEERS, True,
                                         f"scatter_out_start_{l}")
        return pending_out[l][4]

    def send_in(l, dwt):
        if l > 0:
            pending_in[l] = _exchange_start(_partial_slabs(dwt, cols), dwt[-1], ALL_PEERS, True,
                                            f"scatter_in_start_{l}")
            return pending_in[l][4]
        pair = _exchange_start(_partial_slabs(dwt, cols, by_core=True), dwt[-1], (1,), True,
                               "pair_start_0", slots=BY_CORE)
        by_core, from_sibling = _exchange_wait(pair[:4], pair[4], (1,), True, "pair_wait_0", slots=BY_CORE)
        pending_in[l] = _exchange_start(_pair_sum(by_core, from_sibling), dwt[-1], GATHER_PEERS[1:], True,
                                        "scatter_in_start_0", slots=BY_CHIP)
        return pending_in[l][4]

    for l in reversed(range(nl)):
        dout, smalls[l] = _layer_bwd(dout, saved[l], *layer_args(l), on_dwo=functools.partial(send_out, l),
                                     on_dwt=functools.partial(send_in, l))
    grad_x = dout.reshape(x.shape)

    def landed(started, after, name, ks=ALL_PEERS, slots=BY_DEVICE):
        partial, land = _exchange_wait(started[:4], after, ks, True, name, slots=slots)
        mine = slots[0](_position())
        return lax.dynamic_update_slice_in_dim(land, lax.dynamic_slice_in_dim(partial, mine, 1, 0), mine, 0)

    parts_out = [landed(pending_out[l], dout, f"scatter_out_wait_{l}") for l in range(nl)]
    g_w_out, d_w_out, m2_w_out, v2_w_out = _adam_sharded(parts_out, 0, w_out, m_w_out, v_w_out, "adam_w_out")
    names = 7
    small_stacked = [jnp.stack([smalls[l][i] for l in range(nl)]) for i in range(names)]
    shapes = [a.shape for a in small_stacked]
    gathered = _share(_pack(small_stacked), "gather_small_grads", after=d_w_out)
    g_pre_g, g_post_g, wa_g_full, b_g, gla_g, att_g, rb_g = _unpack(_sum_slots(gathered), shapes)
    wa_g_mine = lax.dynamic_slice_in_dim(wa_g_full, my * wa_cols, wa_cols, axis=2)
    grads = [g_pre_g, g_post_g, wa_g_mine, b_g, gla_g, att_g, rb_g]
    ws = [g_pre, g_post, w_alpha, b_alpha, g_gla, g_att, rel_bias]
    ms = [m_g_pre, m_g_post, m_w_alpha, m_b_alpha, m_g_gla, m_g_att, m_rel_bias]
    vs = [v_g_pre, v_g_post, v_w_alpha, v_b_alpha, v_g_gla, v_g_att, v_rel_bias]
    d_s, m2_s, v2_s = _adam_small(ws, grads, ms, vs)

    parts_in = [landed(pending_in[0], d_s[0], "scatter_in_wait_0", GATHER_PEERS[1:], BY_CHIP)]
    parts_in += [landed(pending_in[l], d_s[0], f"scatter_in_wait_{l}") for l in range(1, nl)]
    g_w_in, d_w_in, m2_w_in, v2_w_in = [
        jnp.transpose(a, (1, 2, 0))
        for a in _adam_columns(parts_in, 0, w_c, cols_first(m_w_in), cols_first(v_w_in))]

    def ordered(big_in, big_out, small):
        return [big_in, big_out] + list(small)

    return (loss, grad_x,
            *ordered(g_w_in, g_w_out, grads),
            *ordered(d_w_in, d_w_out, d_s),
            *ordered(m2_w_in, m2_w_out, m2_s),
            *ordered(v2_w_in, v2_w_out, v2_s))
```

```python
import functools

import jax
import jax.numpy as jnp
from jax import lax
from jax.experimental import pallas as pl
from jax.experimental.pallas import tpu as pltpu

F32 = jnp.float32
BF16 = jnp.bfloat16
MESH = pl.DeviceIdType.MESH
ANY = pl.BlockSpec(memory_space=pl.ANY)

CHUNK = 64
GLA_HEADS = 4
GLA_DK = 128
GLA_DV = 256
GLA_KW = GLA_HEADS * GLA_DK
D_GLA = GLA_HEADS * GLA_DV
GLA_RANK = 16
GLA_TAU = 16.0
ATT_HEADS = 8
ATT_HD = 128
D_ATT = ATT_HEADS * ATT_HD
LEFT_CHUNKS = 8
REL_CLIP = 128
N_REL = 2 * REL_CLIP + 1
EPS = 1e-6
D_IN = 2 * GLA_KW + 2 * D_GLA + GLA_RANK + 4 * D_ATT
GLA_SCALE = GLA_DK ** -0.5
ATT_SCALE = ATT_HD ** -0.5

ADAM_LR = 0.001
ADAM_B1 = 0.9
ADAM_B2 = 0.999
ADAM_EPS = 1e-08
ADAM_WD = 0.01
ADAM_STEP = 10

N_DEV = 8
LANE = 128
GA_ORIG = 2 * GLA_KW + 2 * D_GLA
OFF_AQ = GA_ORIG
OFF_GA = GA_ORIG + 4 * D_ATT
D_ZP = OFF_GA + LANE
QB = 2 * CHUNK
ATT_UNROLL = 16
WIN = (LEFT_CHUNKS + 2) * CHUNK
ET_ROWS = WIN + LEFT_CHUNKS * CHUNK
NEG = -1e30
VMEM_LIMIT = 48 * 1024 * 1024


def _cparams(sem):
    return pltpu.CompilerParams(dimension_semantics=sem, vmem_limit_bytes=VMEM_LIMIT)


def _dot(a, b):
    return jnp.dot(a, b, preferred_element_type=F32)


def _dot_nt(a, b):
    return lax.dot_general(a, b, (((1,), (1,)), ((), ())), preferred_element_type=F32)


def _dot_tn(a, b):
    return lax.dot_general(a, b, (((0,), (0,)), ((), ())), preferred_element_type=F32)


def _dot01(t, x, left=True):
    if not left:
        t, x = x, t
    hi = x.astype(BF16)
    r = x - hi.astype(F32)
    mid = r.astype(BF16)
    lo = (r - mid.astype(F32)).astype(BF16)
    if left:
        return _dot(t, hi) + _dot(t, mid) + _dot(t, lo)
    return _dot(hi, t) + _dot(mid, t) + _dot(lo, t)


def _sigmoid(x):
    return 1.0 / (1.0 + jnp.exp(-x))


def _log_sigmoid(x):
    return jnp.minimum(x, 0.0) - jnp.log(1.0 + jnp.exp(-jnp.abs(x)))


TILES = {
    "in_proj": (512, D_ZP // 3, None),
    "in_proj_dx": (512, 512, None),
    "in_proj_dw": (512, 2048, None),
    "out_proj": (512, 1024, None),
    "out_proj_dx": (512, 1024, None),
    "out_proj_dw": (1024, 1024, None),
}


def _matmul(a, b, mode, out_dtype, tm, tn, tk, name, n_outer=False, after=None):
    if mode == "nn":
        (m, k), n = a.shape, b.shape[1]
    elif mode == "nt":
        (m, k), n = a.shape, b.shape[0]
    else:
        (k, m), n = a.shape, b.shape[1]
    tm, tn, tk = min(tm, m), min(tn, n), k if tk is None else min(tk, k)
    assert m % tm == 0 and n % tn == 0 and k % tk == 0, (name, m, n, k)
    nk = k // tk
    dot = {"nn": _dot, "nt": _dot_nt, "tn": _dot_tn}[mode]

    follows = [] if after is None else [after]

    def body_whole_k(a_ref, b_ref, *rest):
        o_ref = rest[-1]
        o_ref[...] = dot(a_ref[...], b_ref[...]).astype(out_dtype)

    def body(a_ref, b_ref, *rest):
        o_ref, acc_ref = rest[-2:]
        kk = pl.program_id(2)

        @pl.when(kk == 0)
        def _():
            acc_ref[...] = jnp.zeros_like(acc_ref)

        acc_ref[...] += dot(a_ref[...], b_ref[...])

        @pl.when(kk == nk - 1)
        def _():
            o_ref[...] = acc_ref[...].astype(out_dtype)

    def at(index):
        return (lambda j, i, kk: index(i, j, kk)) if n_outer else index

    if mode == "tn":
        a_spec = pl.BlockSpec((tk, tm), at(lambda i, j, kk: (kk, i)))
    else:
        a_spec = pl.BlockSpec((tm, tk), at(lambda i, j, kk: (i, kk)))
    if mode == "nt":
        b_spec = pl.BlockSpec((tn, tk), at(lambda i, j, kk: (j, kk)))
    else:
        b_spec = pl.BlockSpec((tk, tn), at(lambda i, j, kk: (kk, j)))
    return pl.pallas_call(
        body_whole_k if nk == 1 else body, name=name,
        grid=(n // tn, m // tm, nk) if n_outer else (m // tm, n // tn, nk),
        in_specs=[a_spec, b_spec] + [ANY] * len(follows),
        out_specs=pl.BlockSpec((tm, tn), at(lambda i, j, kk: (i, j))),
        out_shape=jax.ShapeDtypeStruct((m, n), out_dtype),
        scratch_shapes=[] if nk == 1 else [pltpu.VMEM((tm, tn), F32)],
        compiler_params=_cparams(("parallel", "parallel", "arbitrary")),
    )(a, b, *follows)


def _matmul_cols(pieces, b, out_dtype, tm, tn, name, after=None):
    m, n = pieces[0].shape[0], b.shape[1]
    widths = [p.shape[1] for p in pieces]
    starts = [sum(widths[:i]) for i in range(len(pieces))]
    follows = [] if after is None else [after]
    tm, tn = min(tm, m), min(tn, n)
    assert sum(widths) == b.shape[0] and m % tm == 0 and n % tn == 0, name

    def body(*refs):
        b_ref, o_ref = refs[len(pieces)], refs[-1]
        acc = None
        for p_ref, at, width in zip(refs, starts, widths):
            part = _dot(p_ref[...], b_ref[at:at + width, :])
            acc = part if acc is None else acc + part
        o_ref[...] = acc.astype(out_dtype)

    return pl.pallas_call(
        body, name=name, grid=(n // tn, m // tm),
        in_specs=[pl.BlockSpec((tm, width), lambda j, i: (i, 0)) for width in widths]
        + [pl.BlockSpec((b.shape[0], tn), lambda j, i: (0, j))] + [ANY] * len(follows),
        out_specs=pl.BlockSpec((tm, tn), lambda j, i: (i, j)),
        out_shape=jax.ShapeDtypeStruct((m, n), out_dtype),
        compiler_params=_cparams(("parallel", "parallel")),
    )(*pieces, b, *follows)


def _matmul_rows(pieces, b, out_dtype, tw, tn, name):
    k, n = b.shape
    tn = min(tn, n)
    counts = [p.shape[1] // tw for p in pieces]
    firsts = [sum(counts[:i]) for i in range(len(pieces))]
    assert all(p.shape[1] % tw == 0 for p in pieces) and n % tn == 0, name

    def body(*refs):
        b_ref, o_ref = refs[len(pieces):]
        for p_ref, first, count in zip(refs, firsts, counts):
            @pl.when((pl.program_id(0) >= first) & (pl.program_id(0) < first + count))
            def _(p_ref=p_ref):
                o_ref[...] = _dot_tn(p_ref[...], b_ref[...]).astype(out_dtype)

    def piece_spec(first, count):
        return pl.BlockSpec((k, tw), lambda i, j: (0, jnp.clip(i - first, 0, count - 1)))

    return pl.pallas_call(
        body, name=name, grid=(sum(counts), n // tn),
        in_specs=[piece_spec(first, count) for first, count in zip(firsts, counts)]
        + [pl.BlockSpec((k, tn), lambda i, j: (0, j))],
        out_specs=pl.BlockSpec((tw, tn), lambda i, j: (i, j)),
        out_shape=jax.ShapeDtypeStruct((sum(counts) * tw, n), out_dtype),
        compiler_params=_cparams(("parallel", "parallel")),
    )(*pieces, b)


ROWS = 512


def _rms_fwd(x, g):
    s, d = x.shape

    def body(x_ref, g_ref, h_ref):
        xv = x_ref[...]
        r = lax.rsqrt(jnp.mean(xv * xv, axis=-1, keepdims=True) + EPS)
        h_ref[...] = (xv * r * g_ref[...]).astype(BF16)

    return pl.pallas_call(
        body, name="rms_fwd", grid=(s // ROWS,),
        in_specs=[pl.BlockSpec((ROWS, d), lambda i: (i, 0)), pl.BlockSpec((1, d), lambda i: (0, 0))],
        out_specs=pl.BlockSpec((ROWS, d), lambda i: (i, 0)),
        out_shape=jax.ShapeDtypeStruct((s, d), BF16),
        compiler_params=_cparams(("parallel",)),
    )(x, g)


def _post_fwd(x, y, g):
    s, d = x.shape

    def body(x_ref, y_ref, g_ref, o_ref):
        yv = y_ref[...]
        r = lax.rsqrt(jnp.mean(yv * yv, axis=-1, keepdims=True) + EPS)
        o_ref[...] = x_ref[...] + yv * r * g_ref[...]

    row = pl.BlockSpec((ROWS, d), lambda i: (i, 0))
    return pl.pallas_call(
        body, name="post_fwd", grid=(s // ROWS,),
        in_specs=[row, row, pl.BlockSpec((1, d), lambda i: (0, 0))],
        out_specs=row,
        out_shape=jax.ShapeDtypeStruct((s, d), F32),
        compiler_params=_cparams(("parallel",)),
    )(x, y, g)


def _loss_head(out, tgt):
    s, d = out.shape

    def body(o_ref, t_ref, dout_ref, sum_ref):
        @pl.when(pl.program_id(0) == 0)
        def _():
            sum_ref[...] = jnp.zeros_like(sum_ref)

        e = o_ref[...] - t_ref[...]
        dout_ref[...] = e * (1.0 / d)
        sum_ref[...] += jnp.sum(jnp.sum(e * e, axis=1, keepdims=True), axis=0, keepdims=True)

    row = pl.BlockSpec((ROWS, d), lambda i: (i, 0))
    return pl.pallas_call(
        body, name="loss_head", grid=(s // ROWS,),
        in_specs=[row, row],
        out_specs=[row, pl.BlockSpec((1, 1), lambda i: (0, 0))],
        out_shape=[jax.ShapeDtypeStruct((s, d), F32), jax.ShapeDtypeStruct((1, 1), F32)],
        compiler_params=_cparams(("arbitrary",)),
    )(out, tgt)


def _post_bwd(dout, y, g):
    s, d = y.shape

    def body(do_ref, y_ref, g_ref, dy_ref, dg_ref):
        @pl.when(pl.program_id(0) == 0)
        def _():
            dg_ref[...] = jnp.zeros_like(dg_ref)

        yv = y_ref[...]
        dv = do_ref[...]
        r = lax.rsqrt(jnp.mean(yv * yv, axis=-1, keepdims=True) + EPS)
        dg_ref[...] += jnp.sum(dv * yv * r, axis=0, keepdims=True)
        w = dv * g_ref[...]
        dy = r * (w - yv * (r * r) * jnp.mean(w * yv, axis=-1, keepdims=True))
        dy_ref[...] = dy.astype(BF16)

    row = pl.BlockSpec((ROWS, d), lambda i: (i, 0))
    vec = pl.BlockSpec((1, d), lambda i: (0, 0))
    return pl.pallas_call(
        body, name="post_bwd", grid=(s // ROWS,),
        in_specs=[row, row, vec],
        out_specs=[row, vec],
        out_shape=[jax.ShapeDtypeStruct((s, d), BF16), jax.ShapeDtypeStruct((1, d), F32)],
        compiler_params=_cparams(("arbitrary",)),
    )(dout, y, g)


def _pre_bwd(dh, x, g, dout):
    s, d = x.shape

    def body(dh_ref, x_ref, g_ref, do_ref, dx_ref, dg_ref):
        @pl.when(pl.program_id(0) == 0)
        def _():
            dg_ref[...] = jnp.zeros_like(dg_ref)

        xv = x_ref[...]
        dv = dh_ref[...]
        r = lax.rsqrt(jnp.mean(xv * xv, axis=-1, keepdims=True) + EPS)
        dg_ref[...] += jnp.sum(dv * xv * r, axis=0, keepdims=True)
        w = dv * g_ref[...]
        dx_ref[...] = do_ref[...] + r * (w - xv * (r * r) * jnp.mean(w * xv, axis=-1, keepdims=True))

    row = pl.BlockSpec((ROWS, d), lambda i: (i, 0))
    vec = pl.BlockSpec((1, d), lambda i: (0, 0))
    return pl.pallas_call(
        body, name="pre_bwd", grid=(s // ROWS,),
        in_specs=[row, row, vec, row],
        out_specs=[row, vec],
        out_shape=[jax.ShapeDtypeStruct((s, d), F32), jax.ShapeDtypeStruct((1, d), F32)],
        compiler_params=_cparams(("arbitrary",)),
    )(dh, x, g, dout)


GLA_STEP = 4
GLA_ROWS = GLA_STEP * CHUNK
GLA_CHUNKS = [slice(c * CHUNK, (c + 1) * CHUNK) for c in range(GLA_STEP)]


def _chunk_triangles():
    ri = lax.broadcasted_iota(jnp.int32, (GLA_ROWS, GLA_ROWS), 0)
    ci = lax.broadcasted_iota(jnp.int32, (GLA_ROWS, GLA_ROWS), 1)
    same = (ri // CHUNK) == (ci // CHUNK)
    return (jnp.where(same & (ri >= ci), 1.0, 0.0).astype(BF16), jnp.where(same & (ci >= ri), 1.0, 0.0).astype(BF16))


def _per_chunk(fn, like):
    row = lax.broadcasted_iota(jnp.int32, like.shape, 0)
    return [fn((row >= c * CHUNK) & (row < (c + 1) * CHUNK)) for c in range(GLA_STEP)]


def _spread(per_chunk, like):
    row = lax.broadcasted_iota(jnp.int32, like.shape, 0)
    out = per_chunk[-1]
    for c in reversed(range(GLA_STEP - 1)):
        out = jnp.where(row < (c + 1) * CHUNK, per_chunk[c], out)
    return out


def _gla_gate(ga_b, wa_b, b_ref, tri):
    pre = _dot(ga_b, wa_b) + b_ref[...]
    la = _log_sigmoid(pre) * (1.0 / GLA_TAU)
    return pre, _dot01(tri, la)


def _chunk_ends(cum):
    row = lax.broadcasted_iota(jnp.int32, cum.shape, 0)
    return [jnp.sum(jnp.where(row == (c + 1) * CHUNK - 1, cum, 0.0), axis=0, keepdims=True)
            for c in range(GLA_STEP)]


def _heads(width):
    return [slice(h * width, (h + 1) * width) for h in range(GLA_HEADS)]


def _z_specs_gla(rev=None):
    idx = (lambda n: n) if rev is None else rev
    return [
        pl.BlockSpec((GLA_ROWS, GLA_KW), lambda n: (idx(n), 0)),
        pl.BlockSpec((GLA_ROWS, GLA_KW), lambda n: (idx(n), 1)),
        pl.BlockSpec((GLA_ROWS, D_GLA), lambda n: (idx(n), 1)),
        pl.BlockSpec((GLA_ROWS, D_GLA), lambda n: (idx(n), 2)),
        pl.BlockSpec((GLA_ROWS, LANE), lambda n: (idx(n), OFF_GA // LANE)),
    ]


def _gla_fwd(z, wa_pad, b_alpha, g_gla):
    s = z.shape[0]
    nchunk = s // CHUNK

    def body(q_ref, k_ref, v_ref, gg_ref, ga_ref, wa_ref, b_ref, g_ref, y_ref, o_ref, st_ref, pre_ref, cum_ref,
             state):
        @pl.when(pl.program_id(0) == 0)
        def _():
            state[...] = jnp.zeros_like(state)

        ga_b = ga_ref[...].astype(BF16)
        tri, _ = _chunk_triangles()
        nh = range(GLA_HEADS)
        keys, vals = _heads(GLA_DK), _heads(GLA_DV)
        pre, cum = _gla_gate(ga_b, wa_ref[...].astype(BF16), b_ref, tri)
        pre_ref[...] = pre
        cum_ref[...] = cum
        cends = _chunk_ends(cum)
        kd_b = (k_ref[...] * jnp.exp(_spread(cends, cum) - cum)).astype(BF16)
        qs = (q_ref[...] * GLA_SCALE).astype(BF16)
        v_b = v_ref[...].astype(BF16)
        uts = [[_dot_tn(v_b[rs, vals[h]], kd_b[rs, keys[h]]) for h in nh] for rs in GLA_CHUNKS]
        sts, prev = [], [state[h] for h in nh]
        for c in range(GLA_STEP):
            a = jnp.exp(cends[c])
            prev = [prev[h] * a[:, keys[h]] + uts[c][h] for h in nh]
            sts.append(prev)
        for h in nh:
            state[h] = prev[h]
            for c in range(GLA_STEP):
                st_ref[c, h] = sts[c][h]
        outs = [[_dot_nt(qs[rs, keys[h]], sts[c][h].astype(BF16)) for h in nh] for c, rs in enumerate(GLA_CHUNKS)]
        for h in nh:
            o, vs = jnp.concatenate([outs[c][h] for c in range(GLA_STEP)], axis=0), vals[h]
            o_ref[:, vs] = o
            r = lax.rsqrt(jnp.mean(o * o, axis=-1, keepdims=True) + EPS)
            gg = gg_ref[:, vs]
            y_ref[:, vs] = (o * r * g_ref[:, vs] * (gg * _sigmoid(gg))).astype(BF16)

    full = lambda shape: pl.BlockSpec(shape, lambda n: tuple(0 for _ in shape))
    wide = pl.BlockSpec((GLA_ROWS, D_GLA), lambda n: (n, 0))
    return pl.pallas_call(
        body, name="gla_fwd", grid=(nchunk // GLA_STEP,),
        in_specs=_z_specs_gla() + [full((LANE, GLA_KW)), full((1, GLA_KW)), full((1, D_GLA))],
        out_specs=[wide, wide, pl.BlockSpec((GLA_STEP, GLA_HEADS, GLA_DV, GLA_DK), lambda n: (n, 0, 0, 0)),
                   pl.BlockSpec((GLA_ROWS, GLA_KW), lambda n: (n, 0)), pl.BlockSpec((GLA_ROWS, GLA_KW), lambda n: (n, 0))],
        out_shape=[jax.ShapeDtypeStruct((s, D_GLA), BF16), jax.ShapeDtypeStruct((s, D_GLA), F32),
                   jax.ShapeDtypeStruct((nchunk, GLA_HEADS, GLA_DV, GLA_DK), F32),
                   jax.ShapeDtypeStruct((s, GLA_KW), F32), jax.ShapeDtypeStruct((s, GLA_KW), F32)],
        scratch_shapes=[pltpu.VMEM((GLA_HEADS, GLA_DV, GLA_DK), F32)],
        compiler_params=_cparams(("arbitrary",)),
    )(z, z, z, z, z, wa_pad, b_alpha, g_gla)


def _gla_bwd(dyc, o_gla, z, wa_pad, g_gla, states, gate_pre, gate_cum):
    s = z.shape[0]
    nsteps = s // GLA_ROWS
    rev = lambda n: nsteps - 1 - n

    def body(dy_ref, o_ref, q_ref, k_ref, v_ref, gg_ref, ga_ref, wa_ref, g_ref, st_ref, stp_ref, pre_ref, cum_ref,
             dq_ref, dk_ref, dv_ref, dgg_ref, dga_ref, dwa_ref, db_ref, dg_ref, carry):
        step = pl.program_id(0)

        @pl.when(step == 0)
        def _():
            carry[...] = jnp.zeros_like(carry)
            dwa_ref[...] = jnp.zeros_like(dwa_ref)
            db_ref[...] = jnp.zeros_like(db_ref)
            dg_ref[...] = jnp.zeros_like(dg_ref)

        has_prev = (step < nsteps - 1).astype(F32)
        ga_b = ga_ref[...].astype(BF16)
        _, tri_up = _chunk_triangles()
        nh, nc = range(GLA_HEADS), range(GLA_STEP)
        keys, vals = _heads(GLA_DK), _heads(GLA_DV)
        wa_b = wa_ref[...].astype(BF16)
        pre, cum = pre_ref[...], cum_ref[...]
        cends = _chunk_ends(cum)
        e = jnp.exp(_spread(cends, cum) - cum)
        a = [jnp.exp(cends[c]) for c in nc]
        kf = k_ref[...]
        kd_b = (kf * e).astype(BF16)
        v_b = v_ref[...].astype(BF16)
        qs = (q_ref[...] * GLA_SCALE).astype(BF16)
        do_b = []
        for h in nh:
            vs = vals[h]
            o = o_ref[:, vs]
            gg = gg_ref[:, vs]
            g = g_ref[:, vs]
            dy = dy_ref[:, vs]
            r = lax.rsqrt(jnp.mean(o * o, axis=-1, keepdims=True) + EPS)
            sg = _sigmoid(gg)
            dogn = dy * (gg * sg)
            dgg_ref[:, vs] = (dy * (o * r * g) * (sg * (1.0 + gg * (1.0 - sg)))).astype(BF16)
            dg_ref[:, vs] += jnp.sum(dogn * o * r, axis=0, keepdims=True)
            w = dogn * g
            do_b.append((r * (w - o * (r * r) * jnp.mean(w * o, axis=-1, keepdims=True))).astype(BF16))
        dqs = [jnp.concatenate([_dot(do_b[h][rs], st_ref[c, h].astype(BF16)) for c, rs in enumerate(GLA_CHUNKS)],
                               axis=0) for h in nh]
        dq_ref[...] = (jnp.concatenate(dqs, axis=1) * GLA_SCALE).astype(BF16)
        own = [[_dot_tn(do_b[h][rs], qs[rs, keys[h]]) for h in nh] for rs in GLA_CHUNKS]
        gts, later = [None] * GLA_STEP, [carry[h] for h in nh]
        for c in reversed(nc):
            gts[c] = [own[c][h] + later[h] for h in nh]
            later = [gts[c][h] * a[c][:, keys[h]] for h in nh]
        for h in nh:
            carry[h] = later[h]
        gt_b = [[gts[c][h].astype(BF16) for h in nh] for c in nc]
        dkd = jnp.concatenate([jnp.concatenate([_dot(v_b[rs, vals[h]], gt_b[c][h]) for h in nh], axis=1)
                               for c, rs in enumerate(GLA_CHUNKS)], axis=0)
        dvs = [[_dot_nt(kd_b[rs, keys[h]], gt_b[c][h]) for h in nh] for c, rs in enumerate(GLA_CHUNKS)]
        before = lambda c, h: st_ref[c - 1, h] if c > 0 else stp_ref[0, h] * has_prev
        da = [jnp.concatenate([jnp.sum(gts[c][h] * before(c, h), axis=0, keepdims=True) for h in nh], axis=1)
              for c in nc]
        for h in nh:
            dv_ref[:, vals[h]] = jnp.concatenate([dvs[c][h] for c in nc], axis=0).astype(BF16)
        dk_ref[...] = (dkd * e).astype(BF16)
        dd = dkd * kf * e
        dsum = _per_chunk(lambda mine: jnp.sum(jnp.where(mine, dd, 0.0), axis=0, keepdims=True), dd)
        dcend = _spread([dsum[c] + da[c] * a[c] for c in nc], dd)
        dla = dcend - _dot01(tri_up, dd)
        dpre = dla * (1.0 / GLA_TAU) * (1.0 - _sigmoid(pre))
        dpre_b = dpre.astype(BF16)
        dga_ref[...] = _dot_nt(dpre_b, wa_b).astype(BF16)
        dwa_ref[...] += _dot_tn(ga_b, dpre_b)
        db_ref[...] += jnp.sum(dpre, axis=0, keepdims=True)

    full = lambda shape: pl.BlockSpec(shape, lambda n: tuple(0 for _ in shape))
    wide = pl.BlockSpec((GLA_ROWS, D_GLA), lambda n: (rev(n), 0))
    keyw = pl.BlockSpec((GLA_ROWS, GLA_KW), lambda n: (rev(n), 0))
    st_spec = pl.BlockSpec((GLA_STEP, GLA_HEADS, GLA_DV, GLA_DK), lambda n: (rev(n), 0, 0, 0))
    stp_spec = pl.BlockSpec((1, GLA_HEADS, GLA_DV, GLA_DK),
                            lambda n: (jnp.maximum(GLA_STEP * rev(n) - 1, 0), 0, 0, 0))
    return pl.pallas_call(
        body, name="gla_bwd", grid=(nsteps,),
        in_specs=[wide, wide] + _z_specs_gla(rev)
        + [full((LANE, GLA_KW)), full((1, D_GLA)), st_spec, stp_spec, keyw, keyw],
        out_specs=[keyw, keyw, wide, wide, pl.BlockSpec((GLA_ROWS, LANE), lambda n: (rev(n), 0)),
                   full((LANE, GLA_KW)), full((1, GLA_KW)), full((1, D_GLA))],
        out_shape=[jax.ShapeDtypeStruct((s, GLA_KW), BF16), jax.ShapeDtypeStruct((s, GLA_KW), BF16),
                   jax.ShapeDtypeStruct((s, D_GLA), BF16), jax.ShapeDtypeStruct((s, D_GLA), BF16),
                   jax.ShapeDtypeStruct((s, LANE), BF16),
                   jax.ShapeDtypeStruct((LANE, GLA_KW), F32), jax.ShapeDtypeStruct((1, GLA_KW), F32),
                   jax.ShapeDtypeStruct((1, D_GLA), F32)],
        scratch_shapes=[pltpu.VMEM((GLA_HEADS, GLA_DV, GLA_DK), F32)],
        compiler_params=_cparams(("arbitrary",)),
    )(dyc, o_gla, z, z, z, z, z, wa_pad, g_gla, states, states, gate_pre, gate_cum)


def _build_bias_table(rb_row, et_ref):
    far = jnp.broadcast_to(rb_row[:, 2 * REL_CLIP:2 * REL_CLIP + 1], (1, LANE))
    near_hi = rb_row[:, REL_CLIP:2 * REL_CLIP]
    near_lo = rb_row[:, 0:REL_CLIP]
    past = jnp.broadcast_to(rb_row[:, 0:1], (1, LANE))
    seg = [far, far, far, far, near_hi, near_lo] + [past] * (ET_ROWS // LANE - 5)
    ri = lax.broadcasted_iota(jnp.int32, (LANE, LANE), 0)
    ci = lax.broadcasted_iota(jnp.int32, (LANE, LANE), 1)
    for kb in range(ET_ROWS // LANE):
        wmat = jnp.where(ri + ci < LANE, seg[kb], seg[kb + 1])
        blk = pltpu.roll(wmat, 0, 1, stride=1, stride_axis=0)
        lag = LEFT_CHUNKS + ci // CHUNK - (2 * kb + ri // CHUNK)
        et_ref[kb * LANE:(kb + 1) * LANE, :] = jnp.where((lag >= 0) & (lag <= LEFT_CHUNKS), blk, NEG)


def _reduce_bias_table(det_ref):
    lane = lax.broadcasted_iota(jnp.int32, (1, LANE), 1)
    ri = lax.broadcasted_iota(jnp.int32, (LANE, LANE), 0)
    ci = lax.broadcasted_iota(jnp.int32, (LANE, LANE), 1)
    flip = jnp.where(ri + ci == LANE - 1, 1.0, 0.0).astype(BF16)
    segs = jnp.zeros((8, LANE), F32)
    seg_row = lax.broadcasted_iota(jnp.int32, (8, LANE), 0)
    prev_minus = jnp.zeros((1, LANE), F32)
    for kb in range(6):
        rolled = pltpu.roll(_dot01(det_ref[kb * LANE:(kb + 1) * LANE, :], flip, left=False), 0, 1,
                            stride=1, stride_axis=0)
        plus = jnp.sum(jnp.where(ci >= ri, rolled, 0.0), axis=0, keepdims=True)
        minus = jnp.sum(jnp.where(ci < ri, rolled, 0.0), axis=0, keepdims=True)
        segs = segs + jnp.where(seg_row == kb, plus + prev_minus, 0.0)
        prev_minus = minus
    segs = _dot01(segs, flip, left=False)
    pick = lambda kb: jnp.sum(jnp.where(seg_row == kb, segs, 0.0), axis=0, keepdims=True)
    far = jnp.sum(pick(0) + pick(1) + pick(2) + pick(3), axis=1, keepdims=True)
    last = jnp.where(lane == 0, far, 0.0)
    return jnp.concatenate([pick(5), pick(4), last], axis=1)


def _att_window(b):
    c0 = 2 * b
    kstart = pl.multiple_of(jnp.maximum(c0 - LEFT_CHUNKS, 0) * CHUNK, CHUNK)
    eoff = pl.multiple_of(jnp.maximum(LEFT_CHUNKS - c0, 0) * CHUNK, CHUNK)
    return kstart, eoff


def _att_probs(q_b, kw_b, et):
    st = _dot_nt(kw_b, q_b) * ATT_SCALE + et
    m = jnp.max(st, axis=0, keepdims=True)
    ex = jnp.exp(st - m)
    return ex * (1.0 / jnp.sum(ex, axis=0, keepdims=True))


def _att_fwd(z, rb_pad, g_att):
    s = z.shape[0]
    nblk = s // QB
    c_aq, c_ak, c_av, c_ag = [(OFF_AQ + i * D_ATT) // ATT_HD for i in range(4)]

    def body(q_ref, k_ref, v_ref, ag_ref, rb_ref, g_ref, y_ref, o_ref, p_ref, et_ref, kb_ref, vb_ref):
        h = pl.program_id(0)
        b = pl.program_id(1)

        @pl.when(b == 0)
        def _():
            _build_bias_table(rb_ref[pl.ds(h, 1), :], et_ref)
            kb_ref[...] = k_ref[...].astype(BF16)
            vb_ref[...] = v_ref[...].astype(BF16)

        for j in range(ATT_UNROLL):
            rs = slice(j * QB, (j + 1) * QB)
            kstart, eoff = _att_window(b * ATT_UNROLL + j)
            q_b = q_ref[rs, :].astype(BF16)
            kw_b = kb_ref[pl.ds(kstart, WIN), :]
            vw_b = vb_ref[pl.ds(kstart, WIN), :]
            pt = _att_probs(q_b, kw_b, et_ref[pl.ds(eoff, WIN), :])
            p_ref[0, j] = pt
            o = _dot_tn(pt.astype(BF16), vw_b)
            o_ref[rs, :] = o
            r = lax.rsqrt(jnp.mean(o * o, axis=-1, keepdims=True) + EPS)
            ag = ag_ref[rs, :]
            y_ref[rs, :] = (o * r * g_ref[...] * (ag * _sigmoid(ag))).astype(BF16)

    blk = lambda col: pl.BlockSpec((ATT_UNROLL * QB, ATT_HD), lambda h, b: (b, col + h))
    seq = lambda col: pl.BlockSpec((s, ATT_HD), lambda h, b: (0, col + h))
    out_blk = pl.BlockSpec((ATT_UNROLL * QB, ATT_HD), lambda h, b: (b, h))
    return pl.pallas_call(
        body, name="att_fwd", grid=(ATT_HEADS, nblk // ATT_UNROLL),
        in_specs=[blk(c_aq), seq(c_ak), seq(c_av), blk(c_ag),
                  pl.BlockSpec((ATT_HEADS, 3 * LANE), lambda h, b: (0, 0)),
                  pl.BlockSpec((1, ATT_HD), lambda h, b: (0, h))],
        out_specs=[out_blk, out_blk, pl.BlockSpec((1, ATT_UNROLL, WIN, QB), lambda h, b: (h, b, 0, 0))],
        out_shape=[jax.ShapeDtypeStruct((s, D_ATT), BF16), jax.ShapeDtypeStruct((s, D_ATT), F32),
                   jax.ShapeDtypeStruct((ATT_HEADS, nblk, WIN, QB), F32)],
        scratch_shapes=[pltpu.VMEM((ET_ROWS, LANE), F32), pltpu.VMEM((s, ATT_HD), BF16),
                        pltpu.VMEM((s, ATT_HD), BF16)],
        compiler_params=_cparams(("arbitrary", "arbitrary")),
    )(z, z, z, z, rb_pad, g_att)


def _att_bwd(dyc, o_att, probs, z, g_att):
    s = z.shape[0]
    nblk = s // QB
    c_aq, c_ak, c_av, c_ag = [(OFF_AQ + i * D_ATT) // ATT_HD for i in range(4)]
    c_dy = D_GLA // ATT_HD

    def body(dy_ref, o_ref, p_ref, q_ref, k_ref, v_ref, ag_ref, g_ref,
             dq_ref, dk_ref, dv_ref, dag_ref, drb_ref, dg_ref, det_ref, kb_ref, vb_ref, dk_acc, dv_acc):
        b = pl.program_id(1)

        @pl.when(b == 0)
        def _():
            kb_ref[...] = k_ref[...].astype(BF16)
            vb_ref[...] = v_ref[...].astype(BF16)
            det_ref[...] = jnp.zeros_like(det_ref)
            dk_acc[...] = jnp.zeros_like(dk_acc)
            dv_acc[...] = jnp.zeros_like(dv_acc)
            dg_ref[...] = jnp.zeros_like(dg_ref)

        g = g_ref[...]
        dg = jnp.zeros((1, ATT_HD), F32)
        for j in range(ATT_UNROLL):
            rs = slice(j * QB, (j + 1) * QB)
            kstart, eoff = _att_window(b * ATT_UNROLL + j)
            q_b = q_ref[rs, :].astype(BF16)
            kw_b = kb_ref[pl.ds(kstart, WIN), :]
            vw_b = vb_ref[pl.ds(kstart, WIN), :]
            pt = p_ref[0, j]
            o = o_ref[rs, :]
            ag = ag_ref[rs, :]
            dy = dy_ref[rs, :]
            r = lax.rsqrt(jnp.mean(o * o, axis=-1, keepdims=True) + EPS)
            sg = _sigmoid(ag)
            don = dy * (ag * sg)
            dag_ref[rs, :] = (dy * (o * r * g) * (sg * (1.0 + ag * (1.0 - sg)))).astype(BF16)
            dg = dg + jnp.sum(don * o * r, axis=0, keepdims=True)
            w = don * g
            do_b = (r * (w - o * (r * r) * jnp.mean(w * o, axis=-1, keepdims=True))).astype(BF16)
            pt_b = pt.astype(BF16)
            dpt = _dot_nt(vw_b, do_b)
            dst = pt * (dpt - jnp.sum(dpt * pt, axis=0, keepdims=True))
            det_ref[pl.ds(eoff, WIN), :] += dst
            ds_b = (dst * ATT_SCALE).astype(BF16)
            dq_ref[rs, :] = _dot_tn(ds_b, kw_b).astype(BF16)
            dk_acc[pl.ds(kstart, WIN), :] += _dot(ds_b, q_b)
            dv_acc[pl.ds(kstart, WIN), :] += _dot(pt_b, do_b)
        dg_ref[...] += dg

        @pl.when(b == nblk // ATT_UNROLL - 1)
        def _():
            drb_ref[0] = jnp.broadcast_to(_reduce_bias_table(det_ref), (8, 3 * LANE))
            dk_ref[...] = dk_acc[...].astype(BF16)
            dv_ref[...] = dv_acc[...].astype(BF16)

    blk = lambda col: pl.BlockSpec((ATT_UNROLL * QB, ATT_HD), lambda h, b: (b, col + h))
    seq = lambda col: pl.BlockSpec((s, ATT_HD), lambda h, b: (0, col + h))
    out_blk = pl.BlockSpec((ATT_UNROLL * QB, ATT_HD), lambda h, b: (b, h))
    out_seq = pl.BlockSpec((s, ATT_HD), lambda h, b: (0, h))
    return pl.pallas_call(
        body, name="att_bwd", grid=(ATT_HEADS, nblk // ATT_UNROLL),
        in_specs=[blk(c_dy), blk(0), pl.BlockSpec((1, ATT_UNROLL, WIN, QB), lambda h, b: (h, b, 0, 0)),
                  blk(c_aq), seq(c_ak), seq(c_av), blk(c_ag),
                  pl.BlockSpec((1, ATT_HD), lambda h, b: (0, h))],
        out_specs=[out_blk, out_seq, out_seq, out_blk,
                   pl.BlockSpec((1, 8, 3 * LANE), lambda h, b: (h, 0, 0)),
                   pl.BlockSpec((1, ATT_HD), lambda h, b: (0, h))],
        out_shape=[jax.ShapeDtypeStruct((s, D_ATT), BF16), jax.ShapeDtypeStruct((s, D_ATT), BF16),
                   jax.ShapeDtypeStruct((s, D_ATT), BF16), jax.ShapeDtypeStruct((s, D_ATT), BF16),
                   jax.ShapeDtypeStruct((ATT_HEADS, 8, 3 * LANE), F32),
                   jax.ShapeDtypeStruct((1, D_ATT), F32)],
        scratch_shapes=[pltpu.VMEM((ET_ROWS, LANE), F32),
                        pltpu.VMEM((s, ATT_HD), BF16), pltpu.VMEM((s, ATT_HD), BF16),
                        pltpu.VMEM((s, ATT_HD), F32), pltpu.VMEM((s, ATT_HD), F32)],
        compiler_params=_cparams(("arbitrary", "arbitrary")),
    )(dyc, o_att, probs, z, z, z, z, g_att)


ADAM_ROWS = 64
ADAM_COL_ROWS = 32


def _adam_math(w, g, m, v):
    m2 = ADAM_B1 * m + (1.0 - ADAM_B1) * g
    v2 = ADAM_B2 * v + (1.0 - ADAM_B2) * (g * g)
    m_hat = m2 / (1.0 - ADAM_B1 ** ADAM_STEP)
    v_hat = v2 / (1.0 - ADAM_B2 ** ADAM_STEP)
    delta = -ADAM_LR * (m_hat / (jnp.sqrt(v_hat) + ADAM_EPS) + ADAM_WD * w)
    return delta, m2, v2


def _adam_sharded(parts, first, w, m, v, name):
    nl, nr, nc = w.shape

    def body(*refs):
        p_refs = refs[:nl]
        w_ref, m_ref, v_ref, g_ref, d_ref, m2_ref, v2_ref = refs[nl:]
        for k in range(nl):
            @pl.when(pl.program_id(0) == k)
            def _(p_ref=p_refs[k]):
                g = p_ref[0].astype(F32)
                for dev in range(1, N_DEV):
                    g = g + p_ref[dev].astype(F32)
                delta, m2, v2 = _adam_math(w_ref[0], g, m_ref[0], v_ref[0])
                g_ref[0] = g
                d_ref[0] = delta
                m2_ref[0] = m2
                v2_ref[0] = v2

    def part_spec(k):
        return pl.BlockSpec((N_DEV, ADAM_ROWS, nc), lambda l, i: (0, first + jnp.where(l == k, i, 0), 0))

    blk = pl.BlockSpec((1, ADAM_ROWS, nc), lambda l, i: (l, i, 0))
    shp = jax.ShapeDtypeStruct(w.shape, F32)
    return pl.pallas_call(
        body, name=name, grid=(nl, pl.cdiv(nr, ADAM_ROWS)),
        in_specs=[part_spec(k) for k in range(nl)] + [blk, blk, blk],
        out_specs=[blk, blk, blk, blk],
        out_shape=[shp, shp, shp, shp],
        compiler_params=_cparams(("arbitrary", "arbitrary")),
    )(*parts, w, m, v)


def _adam_columns(parts, first, w, m, v):
    nc, nl, d = w.shape

    def body(*refs):
        p_refs = refs[:nl]
        w_ref, m_ref, v_ref, g_ref, d_ref, m2_ref, v2_ref = refs[nl:]
        for l in range(nl):
            g = p_refs[l][0].astype(F32)
            for slot in range(1, parts[l].shape[0]):
                g = g + p_refs[l][slot].astype(F32)
            delta, m2, v2 = _adam_math(w_ref[:, l, :], g, m_ref[:, l, :], v_ref[:, l, :])
            g_ref[:, l, :] = g
            d_ref[:, l, :] = delta
            m2_ref[:, l, :] = m2
            v2_ref[:, l, :] = v2

    blk = pl.BlockSpec((ADAM_COL_ROWS, nl, d), lambda i: (i, 0, 0))
    shp = jax.ShapeDtypeStruct(w.shape, F32)
    return pl.pallas_call(
        body, name="adam_w_in", grid=(pl.cdiv(nc, ADAM_COL_ROWS),),
        in_specs=[pl.BlockSpec((p.shape[0], ADAM_COL_ROWS, d), lambda i: (0, first + i, 0)) for p in parts]
        + [blk, blk, blk],
        out_specs=[blk, blk, blk, blk],
        out_shape=[shp, shp, shp, shp],
        compiler_params=_cparams(("parallel",)),
    )(*parts, w, m, v)


def _adam_small(ws, gs, ms, vs):
    n = len(ws)

    def body(*refs):
        w_refs, g_refs, m_refs, v_refs, d_refs, m2_refs, v2_refs = [refs[i * n:(i + 1) * n] for i in range(7)]
        for i in range(n):
            delta, m2, v2 = _adam_math(w_refs[i][...], g_refs[i][...], m_refs[i][...], v_refs[i][...])
            d_refs[i][...] = delta
            m2_refs[i][...] = m2
            v2_refs[i][...] = v2

    shapes = [jax.ShapeDtypeStruct(w.shape, F32) for w in ws]
    out = pl.pallas_call(body, name="adam_small", out_shape=shapes * 3)(*ws, *gs, *ms, *vs)
    return out[:n], out[n:2 * n], out[2 * n:]


def _position():
    return lax.axis_index("x"), lax.axis_index("y"), lax.axis_index("c")


def _slot(p):
    return 4 * p[0] + 2 * p[1] + p[2]


BF16_TILE_ROWS = 16


def _slab_rows(rows, cols):
    return -(-(rows + cols) // BF16_TILE_ROWS) * BF16_TILE_ROWS


RELAYOUT_COLS = 1024
RELAYOUT_CHUNK = 64


def _shard_pieces(dev, rows, cols):
    moved = ((0, GA_ORIG, 0), (GA_ORIG, GA_ORIG + GLA_RANK, OFF_GA - GA_ORIG), (GA_ORIG + GLA_RANK, D_IN, -GLA_RANK))
    c0, c1 = dev * cols, (dev + 1) * cols
    return [(rows + max(c0, lo) - c0, max(c0, lo) + off, min(c1, hi) - max(c0, lo))
            for lo, hi, off in moved if max(c0, lo) < min(c1, hi)]


def _move_rows(src, src_row, dst, dst_row, n):
    assert src_row % 2 == 0 and dst_row % 2 == 0 and n % 2 == 0
    for r in range(0, n // 2, RELAYOUT_CHUNK):
        m = min(RELAYOUT_CHUNK, n // 2 - r)
        dst[dst_row // 2 + r:dst_row // 2 + r + m, :] = src[src_row // 2 + r:src_row // 2 + r + m, :]


def _aligned_weight(land, rows, cols):
    _, slab, d = land.shape
    ct = min(RELAYOUT_COLS, d)

    def body(land_ref, wt_ref, wo_ref):
        dev = pl.program_id(1)
        src = land_ref.bitcast(jnp.uint32)
        dst = wt_ref.bitcast(jnp.uint32)
        wo_ref[...] = land_ref[0:rows, :]

        @pl.when(dev == 0)
        def _():
            dst[D_IN // 2:D_ZP // 2, :] = jnp.zeros(((D_ZP - D_IN) // 2, ct), jnp.uint32)

        for k in range(N_DEV):
            @pl.when(dev == k)
            def _(k=k):
                for at, to, n in _shard_pieces(k, rows, cols):
                    _move_rows(src, at, dst, to, n)

    return pl.pallas_call(
        body, name="aligned_weight", grid=(d // ct, N_DEV),
        in_specs=[pl.BlockSpec((slab, ct), lambda c, dev: (dev, c))],
        out_specs=[pl.BlockSpec((D_ZP, ct), lambda c, dev: (0, c)),
                   pl.BlockSpec((rows, ct), lambda c, dev: (dev, c))],
        out_shape=[jax.ShapeDtypeStruct((D_ZP, d), land.dtype),
                   jax.ShapeDtypeStruct((N_DEV * rows, d), land.dtype)],
        compiler_params=_cparams(("parallel", "arbitrary")),
    )(land.reshape(N_DEV * slab, d))


def _partial_slabs(dwt, cols, by_core=False):
    d = dwt[0].shape[1]
    bounds = (0, GA_ORIG, OFF_GA, D_ZP)
    assert tuple(a.shape[0] for a in dwt) == tuple(hi - lo for lo, hi in zip(bounds, bounds[1:]))
    slab = _slab_rows(0, cols)
    ct = min(RELAYOUT_COLS, d)

    def body(*refs):
        out_ref = refs[-1]
        dev = pl.program_id(1)
        srcs = [ref.bitcast(jnp.uint32) for ref in refs[:-1]]
        dst = out_ref.bitcast(jnp.uint32)
        dst[cols // 2:slab // 2, :] = jnp.zeros(((slab - cols) // 2, ct), jnp.uint32)
        for k in range(N_DEV):
            @pl.when(dev == k)
            def _(k=k):
                for to, at, n in _shard_pieces(k, 0, cols):
                    which = max(i for i, lo in enumerate(bounds[:-1]) if lo <= at)
                    assert at + n <= bounds[which + 1]
                    _move_rows(srcs[which], at - bounds[which], dst, to, n)

    place = (lambda dev: (dev % 2) * (N_DEV // 2) + dev // 2) if by_core else (lambda dev: dev)
    out = pl.pallas_call(
        body, name="partial_slabs", grid=(d // ct, N_DEV),
        in_specs=[pl.BlockSpec((a.shape[0], ct), lambda c, dev: (0, c)) for a in dwt],
        out_specs=pl.BlockSpec((slab, ct), lambda c, dev: (place(dev), c)),
        out_shape=jax.ShapeDtypeStruct((N_DEV * slab, d), dwt[0].dtype),
        compiler_params=_cparams(("parallel", "arbitrary")),
    )(*dwt)
    return out.reshape((2, N_DEV // 2, slab, d) if by_core else (N_DEV, slab, d))


def _pair_sum(mine, theirs):
    _, nchip, slab, d = mine.shape
    rows = next(r for r in range(512, 0, -BF16_TILE_ROWS) if slab % r == 0)

    def body(m_ref, t_ref, o_ref):
        south = lax.axis_index("c") == 0
        own = jnp.where(south, m_ref[0, 0], m_ref[1, 0]).astype(F32)
        got = jnp.where(south, t_ref[1, 0], t_ref[0, 0]).astype(F32)
        o_ref[0] = (own + got).astype(o_ref.dtype)

    both = pl.BlockSpec((2, 1, rows, d), lambda j, i: (0, j, i, 0))
    return pl.pallas_call(
        body, name="pair_sum", grid=(nchip, slab // rows),
        in_specs=[both, both],
        out_specs=pl.BlockSpec((1, rows, d), lambda j, i: (j, i, 0)),
        out_shape=jax.ShapeDtypeStruct((nchip, slab, d), mine.dtype),
        compiler_params=_cparams(("parallel", "parallel")),
    )(mine, theirs)


def _peer(pos, k):
    x, y, c = pos
    return (1 - x if k & 4 else x, 1 - y if k & 2 else y, 1 - c if k & 1 else c)


HBM_SPEC = pl.BlockSpec(memory_space=pltpu.HBM)
SEM_SPEC = pl.BlockSpec(memory_space=pltpu.SEMAPHORE)
GATHER_PEERS = (1, 4, 2, 6)
ALL_PEERS = (1, 2, 3, 4, 5, 6, 7)


def _hbm(a):
    return pltpu.with_memory_space_constraint(a, pltpu.HBM)


BY_DEVICE = (_slot, N_DEV)
BY_CORE = (lambda p: p[2], 2)
BY_CHIP = (lambda p: 2 * p[0] + p[1], 4)


def _split_copies(src_ref, land_ref, send_sems, recv_sems, ks, per_peer, landed, slots):
    slot_of = slots[0]
    me = _position()
    out = []
    for i, k in enumerate(ks):
        peer = _peer(me, k)
        src = src_ref.at[slot_of(peer)] if per_peer else src_ref
        dst = land_ref.at[slot_of(peer) if landed else slot_of(me)]
        out.append(pltpu.make_async_remote_copy(
            src_ref=src, dst_ref=dst, send_sem=send_sems.at[i], recv_sem=recv_sems.at[i],
            device_id=peer, device_id_type=MESH))
    return out


def _exchange_start(src, after, ks, per_peer, name, slots=BY_DEVICE):
    slab = src.shape[1:] if per_peer else src.shape
    land_shape = (slots[1],) + tuple(slab)
    n = len(ks)

    def body(src_ref, land_ref, after_ref, send_sems, recv_sems, src_thru, land_thru, token):
        for cp in _split_copies(src_ref, land_ref, send_sems, recv_sems, ks, per_peer, False, slots):
            cp.start()
        token[...] = jnp.zeros_like(token)

    return pl.pallas_call(
        body, name=name,
        out_shape=(pltpu.SemaphoreType.DMA((n,)), pltpu.SemaphoreType.DMA((n,)),
                   pltpu.HBM(src.shape, src.dtype), pltpu.HBM(land_shape, src.dtype),
                   jax.ShapeDtypeStruct((8, LANE), F32)),
        in_specs=(HBM_SPEC, HBM_SPEC, ANY),
        out_specs=(SEM_SPEC, SEM_SPEC, HBM_SPEC, HBM_SPEC, pl.BlockSpec(memory_space=pltpu.VMEM)),
        input_output_aliases={0: 2, 1: 3},
        compiler_params=pltpu.CompilerParams(has_side_effects=pltpu.SideEffectType.DATAFLOW_SIDE_EFFECTING),
    )(_hbm(src), _hbm(lax.empty(land_shape, src.dtype)), after)


def _exchange_wait(started, after, ks, per_peer, name, slots=BY_DEVICE):
    send_sems, recv_sems, src_thru, land_thru = started

    def body(src_ref, land_ref, send_sems, recv_sems, after_ref, src_dead, land_out):
        for cp in _split_copies(src_ref, land_ref, send_sems, recv_sems, ks, per_peer, True, slots):
            cp.wait_send()
            cp.wait_recv()

    return pl.pallas_call(
        body, name=name,
        out_shape=(pltpu.HBM(src_thru.shape, src_thru.dtype), pltpu.HBM(land_thru.shape, land_thru.dtype)),
        in_specs=(HBM_SPEC, HBM_SPEC, SEM_SPEC, SEM_SPEC, ANY), out_specs=(HBM_SPEC, HBM_SPEC),
        input_output_aliases={0: 0, 1: 1},
        compiler_params=pltpu.CompilerParams(has_side_effects=pltpu.SideEffectType.DATAFLOW_SIDE_EFFECTING),
    )(src_thru, land_thru, send_sems, recv_sems, after)


def _relay_copies(land_ref, send_sems, recv_sems, landed):
    me = _position()
    sibling = _peer(me, 1)
    out = []
    for i, k in enumerate(GATHER_PEERS[1:]):
        blk = land_ref.at[_slot(_peer(sibling if landed else me, k))]
        out.append(pltpu.make_async_remote_copy(
            src_ref=blk, dst_ref=blk, send_sem=send_sems.at[i], recv_sem=recv_sems.at[i],
            device_id=sibling, device_id_type=MESH))
    return out


def _relay_start(land, name):
    n = len(GATHER_PEERS) - 1

    def body(land_ref, send_sems, recv_sems, land_thru, token):
        for cp in _relay_copies(land_ref, send_sems, recv_sems, landed=False):
            cp.start()
        token[...] = jnp.zeros_like(token)

    return pl.pallas_call(
        body, name=name,
        out_shape=(pltpu.SemaphoreType.DMA((n,)), pltpu.SemaphoreType.DMA((n,)),
                   pltpu.HBM(land.shape, land.dtype), jax.ShapeDtypeStruct((8, LANE), F32)),
        in_specs=(HBM_SPEC,),
        out_specs=(SEM_SPEC, SEM_SPEC, HBM_SPEC, pl.BlockSpec(memory_space=pltpu.VMEM)),
        input_output_aliases={0: 2},
        compiler_params=pltpu.CompilerParams(has_side_effects=pltpu.SideEffectType.DATAFLOW_SIDE_EFFECTING),
    )(_hbm(land))


def _relay_wait(started, after, name):
    send_sems, recv_sems, land_thru = started

    def body(land_ref, send_sems, recv_sems, after_ref, land_out):
        for cp in _relay_copies(land_ref, send_sems, recv_sems, landed=True):
            cp.wait_send()
            cp.wait_recv()

    return pl.pallas_call(
        body, name=name,
        out_shape=pltpu.HBM(land_thru.shape, land_thru.dtype),
        in_specs=(HBM_SPEC, SEM_SPEC, SEM_SPEC, ANY), out_specs=HBM_SPEC,
        input_output_aliases={0: 0},
        compiler_params=pltpu.CompilerParams(has_side_effects=pltpu.SideEffectType.DATAFLOW_SIDE_EFFECTING),
    )(land_thru, send_sems, recv_sems, after)


def _share(vec, name, after=None):
    follows = [] if after is None else [after]

    def body(vec_ref, *rest):
        out_ref, send_sems, recv_sems, local_sem = rest[len(follows):]
        me = _position()

        def copy(k, landed):
            peer = _peer(me, k)
            return pltpu.make_async_remote_copy(
                src_ref=vec_ref, dst_ref=out_ref.at[_slot(peer) if landed else _slot(me)],
                send_sem=send_sems.at[k - 1], recv_sem=recv_sems.at[k - 1], device_id=peer, device_id_type=MESH)

        mine = pltpu.make_async_copy(vec_ref, out_ref.at[_slot(me)], local_sem)
        mine.start()
        sent = [copy(k, False) for k in ALL_PEERS]
        for cp in sent:
            cp.start()
        for k in ALL_PEERS:
            copy(k, True).wait_recv()
        for cp in sent:
            cp.wait_send()
        mine.wait()

    return pl.pallas_call(
        body, name=name,
        in_specs=[ANY] * (1 + len(follows)), out_specs=ANY,
        out_shape=jax.ShapeDtypeStruct((N_DEV,) + vec.shape, vec.dtype),
        scratch_shapes=[pltpu.SemaphoreType.DMA((N_DEV - 1,)), pltpu.SemaphoreType.DMA((N_DEV - 1,)),
                        pltpu.SemaphoreType.DMA],
    )(vec, *follows)


def _sum_slots(parts):
    def body(p_ref, o_ref):
        acc = p_ref[0]
        for dev in range(1, N_DEV):
            acc = acc + p_ref[dev]
        o_ref[...] = acc

    return pl.pallas_call(body, name="sum_slots",
                          out_shape=jax.ShapeDtypeStruct(parts.shape[1:], F32))(parts)


PACK_ROWS = 8


def _packed_rows(size):
    return -(-size // (PACK_ROWS * LANE)) * PACK_ROWS


def _pack(arrs):
    def rows(a):
        flat = a.reshape(-1)
        return jnp.pad(flat, (0, _packed_rows(flat.shape[0]) * LANE - flat.shape[0])).reshape(-1, LANE)

    return jnp.concatenate([rows(a) for a in arrs], axis=0)


def _unpack(packed, shapes):
    out, at = [], 0
    for shp in shapes:
        size = 1
        for dim in shp:
            size *= dim
        nrows = _packed_rows(size)
        out.append(packed[at:at + nrows].reshape(-1)[:size].reshape(shp))
        at += nrows
    return out


def _layer_fwd(x, wt, wo, g_pre, g_post, wa_pad, b_alpha, g_gla, g_att, rb_pad, midway=None):
    h = _rms_fwd(x, g_pre)
    z = _matmul(h, wt, "nt", F32, *TILES["in_proj"], "in_proj", n_outer=True)
    y_gla, o_gla, *gla_kept = _gla_fwd(z, wa_pad, b_alpha, g_gla)
    y_att, o_att, probs = _att_fwd(z, rb_pad, g_att)
    token = None if midway is None else midway(y_att)
    y = _matmul_cols([y_gla, y_att], wo, F32, *TILES["out_proj"][:2], "out_proj", after=token)
    out = _post_fwd(x, y, g_post)
    return out, (x, h, z, o_gla, gla_kept, o_att, probs, y_gla, y_att, y)


def _layer_bwd(dout, saved, wt, wo, g_pre, g_post, wa_pad, b_alpha, g_gla, g_att, rb_pad, on_dwo, on_dwt):
    x, h, z, o_gla, gla_kept, o_att, probs, y_gla, y_att, y = saved
    dy, dg_post = _post_bwd(dout, y, g_post)
    dwo = _matmul_rows([y_gla, y_att], dy, BF16, *TILES["out_proj_dw"][:2], "out_proj_dw")
    token = on_dwo(dwo)
    dycat = _matmul(dy, wo, "nt", F32, *TILES["out_proj_dx"], "out_proj_dx", n_outer=True, after=token)
    dq, dk, dv, dgg, dga, dwa, db, dg_gla = _gla_bwd(dycat, o_gla, z, wa_pad, g_gla, *gla_kept)
    daq, dak, dav, dag, drb, dg_att = _att_bwd(dycat, o_att, probs, z, g_att)
    tw, tn = TILES["in_proj_dw"][:2]
    dwt = (_matmul_rows([dq, dk, dv, dgg], h, BF16, tw, tn, "in_proj_dw_gla"),
           _matmul_rows([daq, dak, dav, dag], h, BF16, tw, tn, "in_proj_dw_att"),
           _matmul_rows([dga], h, BF16, LANE, tn, "in_proj_dw_gate"))
    token = on_dwt(dwt)
    dh = _matmul_cols([dq, dk, dv, dgg, daq, dak, dav, dag, dga], wt, F32, *TILES["in_proj_dx"][:2],
                      "in_proj_dx", after=token)
    dx, dg_pre = _pre_bwd(dh, x, g_pre, dout)
    small = (dg_pre[0], dg_post[0], dwa[:GLA_RANK], db[0], dg_gla[0], dg_att[0], drb[:, 0, :N_REL])
    return dx, small


def kernel(x, w_in, w_out, g_pre, g_post, w_alpha, b_alpha, g_gla, g_att, rel_bias, loss_target, m_w_in, m_w_out, m_g_pre, m_g_post, m_w_alpha, m_b_alpha, m_g_gla, m_g_att, m_rel_bias, v_w_in, v_w_out, v_g_pre, v_g_post, v_w_alpha, v_b_alpha, v_g_gla, v_g_att, v_rel_bias):
    nl, d, cols = w_in.shape
    rows = w_out.shape[1]
    s = x.shape[1]
    x0 = x.reshape(s, d)
    tgt = loss_target.reshape(s, d)

    cols_first = lambda a: jnp.transpose(a, (2, 0, 1))
    w_c = cols_first(w_in)
    slab = _slab_rows(rows, cols)
    is_out = lax.broadcasted_iota(jnp.int32, (slab, d), 0) < rows

    def shard(l, zero=0.0):
        top = jnp.pad((w_out[l] + zero).astype(BF16), ((0, slab - rows), (0, 0)))
        rest = jnp.pad((w_c[:, l] + zero).astype(BF16), ((rows, slab - rows - cols), (0, 0)))
        return jnp.where(is_out, top, rest)

    first_fetch = _exchange_start(shard(0), x, GATHER_PEERS, False, "gather_start_0")
    began = first_fetch[4][0, 0]
    shards = [None] + [shard(l, began) for l in range(1, nl)]
    alpha = _pack([w_alpha]) + began
    wa_g = _share(alpha, "gather_alpha")
    wa_cols = w_alpha.shape[2]
    wa_full = wa_g.reshape(N_DEV, -1)[:, :nl * GLA_RANK * wa_cols].reshape(N_DEV, nl, GLA_RANK, wa_cols)
    wa_full = jnp.transpose(wa_full, (1, 2, 0, 3)).reshape(nl, GLA_RANK, GLA_KW)
    wa_pad = jnp.pad(wa_full, ((0, 0), (0, LANE - GLA_RANK), (0, 0)))
    rb_pad = jnp.pad(rel_bias, ((0, 0), (0, 0), (0, 3 * LANE - N_REL)))

    def layer_args(l, follows=None):
        gp = g_pre[l:l + 1] if follows is None else g_pre[l:l + 1] + follows[:1, :1]
        return (wts[l], wos[l], gp, g_post[l:l + 1], wa_pad[l], b_alpha[l:l + 1], g_gla[l:l + 1],
                g_att[l:l + 1], rb_pad[l])

    my = _slot(_position())

    def fetch(l, after):
        return _exchange_start(shards[l], after, GATHER_PEERS, False, f"gather_start_{l}")

    def relay(l, first_hop, after):
        own[l], land = _exchange_wait(first_hop[:4], after, GATHER_PEERS, False, f"gather_wait_{l}")
        return _relay_start(land, f"relay_start_{l}")

    def midway(l, y):
        flight["relay"] = relay(l + 1, flight["fetch"], y)
        if l + 2 >= nl:
            return flight["relay"][3]
        flight["fetch"] = fetch(l + 2, flight["relay"][2])
        return flight["fetch"][4]

    act, saved, wts, wos, flight, own = x0, [], [], [], {}, [None] * nl
    prepared = (wa_pad[0, :1, :1] + sum(sh[:1, :1].astype(F32) for sh in shards[1:]))
    flight["relay"] = relay(0, first_fetch, prepared)
    if nl > 1:
        flight["fetch"] = fetch(1, flight["relay"][2])
    for l in range(nl):
        land = _relay_wait(flight["relay"][:3], act, f"relay_wait_{l}")
        land = lax.dynamic_update_slice_in_dim(land, own[l][None], my, 0)
        wt_l, wo_l = _aligned_weight(land, rows, cols)
        wts.append(wt_l)
        wos.append(wo_l)
        act, sv = _layer_fwd(act, *layer_args(l, follows=first_fetch[4] if l == 0 else None),
                             midway=functools.partial(midway, l) if l + 1 < nl else None)
        saved.append(sv)
    dout, sq = _loss_head(act, tgt)
    loss = lax.psum(sq[0, 0] * (0.5 / d), ("x", "y", "c"))

    smalls, pending_out, pending_in = [None] * nl, [None] * nl, [None] * nl

    def send_out(l, dwo):
        pending_out[l] = _exchange_start(dwo.reshape(N_DEV, rows, d), dwo[:1, :1], ALL_PEERS, True,
                                         f"scatter_out_start_{l}")
        return pending_out[l][4]

    def send_in(l, dwt):
        if l > 0:
            pending_in[l] = _exchange_start(_partial_slabs(dwt, cols), dwt[-1], ALL_PEERS, True,
                                            f"scatter_in_start_{l}")
            return pending_in[l][4]
        pair = _exchange_start(_partial_slabs(dwt, cols, by_core=True), dwt[-1], (1,), True,
                               "pair_start_0", slots=BY_CORE)
        by_core, from_sibling = _exchange_wait(pair[:4], pair[4], (1,), True, "pair_wait_0", slots=BY_CORE)
        pending_in[l] = _exchange_start(_pair_sum(by_core, from_sibling), dwt[-1], GATHER_PEERS[1:], True,
                                        "scatter_in_start_0", slots=BY_CHIP)
        return pending_in[l][4]

    for l in reversed(range(nl)):
        dout, smalls[l] = _layer_bwd(dout, saved[l], *layer_args(l), on_dwo=functools.partial(send_out, l),
                                     on_dwt=functools.partial(send_in, l))
    grad_x = dout.reshape(x.shape)

    def landed(started, after, name, ks=ALL_PEERS, slots=BY_DEVICE):
        partial, land = _exchange_wait(started[:4], after, ks, True, name, slots=slots)
        mine = slots[0](_position())
        return lax.dynamic_update_slice_in_dim(land, lax.dynamic_slice_in_dim(partial, mine, 1, 0), mine, 0)

    parts_out = [landed(pending_out[l], dout, f"scatter_out_wait_{l}") for l in range(nl)]
    g_w_out, d_w_out, m2_w_out, v2_w_out = _adam_sharded(parts_out, 0, w_out, m_w_out, v_w_out, "adam_w_out")
    names = 7
    small_stacked = [jnp.stack([smalls[l][i] for l in range(nl)]) for i in range(names)]
    shapes = [a.shape for a in small_stacked]
    gathered = _share(_pack(small_stacked), "gather_small_grads", after=d_w_out)
    g_pre_g, g_post_g, wa_g_full, b_g, gla_g, att_g, rb_g = _unpack(_sum_slots(gathered), shapes)
    wa_g_mine = lax.dynamic_slice_in_dim(wa_g_full, my * wa_cols, wa_cols, axis=2)
    grads = [g_pre_g, g_post_g, wa_g_mine, b_g, gla_g, att_g, rb_g]
    ws = [g_pre, g_post, w_alpha, b_alpha, g_gla, g_att, rel_bias]
    ms = [m_g_pre, m_g_post, m_w_alpha, m_b_alpha, m_g_gla, m_g_att, m_rel_bias]
    vs = [v_g_pre, v_g_post, v_w_alpha, v_b_alpha, v_g_gla, v_g_att, v_rel_bias]
    d_s, m2_s, v2_s = _adam_small(ws, grads, ms, vs)

    parts_in = [landed(pending_in[0], d_s[0], "scatter_in_wait_0", GATHER_PEERS[1:], BY_CHIP)]
    parts_in += [landed(pending_in[l], d_s[0], f"scatter_in_wait_{l}") for l in range(1, nl)]
    g_w_in, d_w_in, m2_w_in, v2_w_in = [
        jnp.transpose(a, (1, 2, 0))
        for a in _adam_columns(parts_in, 0, w_c, cols_first(m_w_in), cols_first(v_w_in))]

    def ordered(big_in, big_out, small):
        return [big_in, big_out] + list(small)

    return (loss, grad_x,
            *ordered(g_w_in, g_w_out, grads),
            *ordered(d_w_in, d_w_out, d_s),
            *ordered(m2_w_in, m2_w_out, m2_s),
            *ordered(v2_w_in, v2_w_out, v2_s))
```

```python
import functools

import jax
import jax.numpy as jnp
from jax import lax
from jax.experimental import pallas as pl
from jax.experimental.pallas import tpu as pltpu

F32 = jnp.float32
BF16 = jnp.bfloat16
MESH = pl.DeviceIdType.MESH
ANY = pl.BlockSpec(memory_space=pl.ANY)

CHUNK = 64
GLA_HEADS = 4
GLA_DK = 128
GLA_DV = 256
GLA_KW = GLA_HEADS * GLA_DK
D_GLA = GLA_HEADS * GLA_DV
GLA_RANK = 16
GLA_TAU = 16.0
ATT_HEADS = 8
ATT_HD = 128
D_ATT = ATT_HEADS * ATT_HD
LEFT_CHUNKS = 8
REL_CLIP = 128
N_REL = 2 * REL_CLIP + 1
EPS = 1e-6
D_IN = 2 * GLA_KW + 2 * D_GLA + GLA_RANK + 4 * D_ATT
GLA_SCALE = GLA_DK ** -0.5
ATT_SCALE = ATT_HD ** -0.5

ADAM_LR = 0.001
ADAM_B1 = 0.9
ADAM_B2 = 0.999
ADAM_EPS = 1e-08
ADAM_WD = 0.01
ADAM_STEP = 10

N_DEV = 8
LANE = 128
GA_ORIG = 2 * GLA_KW + 2 * D_GLA
OFF_AQ = GA_ORIG
OFF_GA = GA_ORIG + 4 * D_ATT
D_ZP = OFF_GA + LANE
QB = 2 * CHUNK
ATT_UNROLL = 16
WIN = (LEFT_CHUNKS + 2) * CHUNK
ET_ROWS = WIN + LEFT_CHUNKS * CHUNK
NEG = -1e30
VMEM_LIMIT = 48 * 1024 * 1024


def _cparams(sem):
    return pltpu.CompilerParams(dimension_semantics=sem, vmem_limit_bytes=VMEM_LIMIT)


def _dot(a, b):
    return jnp.dot(a, b, preferred_element_type=F32)


def _dot_nt(a, b):
    return lax.dot_general(a, b, (((1,), (1,)), ((), ())), preferred_element_type=F32)


def _dot_tn(a, b):
    return lax.dot_general(a, b, (((0,), (0,)), ((), ())), preferred_element_type=F32)


def _dot01(t, x, left=True):
    if not left:
        t, x = x, t
    hi = x.astype(BF16)
    r = x - hi.astype(F32)
    mid = r.astype(BF16)
    lo = (r - mid.astype(F32)).astype(BF16)
    if left:
        return _dot(t, hi) + _dot(t, mid) + _dot(t, lo)
    return _dot(hi, t) + _dot(mid, t) + _dot(lo, t)


def _sigmoid(x):
    return 1.0 / (1.0 + jnp.exp(-x))


def _log_sigmoid(x):
    return jnp.minimum(x, 0.0) - jnp.log(1.0 + jnp.exp(-jnp.abs(x)))


TILES = {
    "in_proj": (512, D_ZP // 3, None),
    "in_proj_dx": (512, 512, None),
    "in_proj_dw": (512, 2048, None),
    "out_proj": (512, 2048, None),
    "out_proj_dx": (512, 2048, None),
    "out_proj_dw": (1024, 2048, None),
}


def _matmul(a, b, mode, out_dtype, tm, tn, tk, name, n_outer=False, after=None):
    if mode == "nn":
        (m, k), n = a.shape, b.shape[1]
    elif mode == "nt":
        (m, k), n = a.shape, b.shape[0]
    else:
        (k, m), n = a.shape, b.shape[1]
    tm, tn, tk = min(tm, m), min(tn, n), k if tk is None else min(tk, k)
    assert m % tm == 0 and n % tn == 0 and k % tk == 0, (name, m, n, k)
    nk = k // tk
    dot = {"nn": _dot, "nt": _dot_nt, "tn": _dot_tn}[mode]

    follows = [] if after is None else [after]

    def body_whole_k(a_ref, b_ref, *rest):
        o_ref = rest[-1]
        o_ref[...] = dot(a_ref[...], b_ref[...]).astype(out_dtype)

    def body(a_ref, b_ref, *rest):
        o_ref, acc_ref = rest[-2:]
        kk = pl.program_id(2)

        @pl.when(kk == 0)
        def _():
            acc_ref[...] = jnp.zeros_like(acc_ref)

        acc_ref[...] += dot(a_ref[...], b_ref[...])

        @pl.when(kk == nk - 1)
        def _():
            o_ref[...] = acc_ref[...].astype(out_dtype)

    def at(index):
        return (lambda j, i, kk: index(i, j, kk)) if n_outer else index

    if mode == "tn":
        a_spec = pl.BlockSpec((tk, tm), at(lambda i, j, kk: (kk, i)))
    else:
        a_spec = pl.BlockSpec((tm, tk), at(lambda i, j, kk: (i, kk)))
    if mode == "nt":
        b_spec = pl.BlockSpec((tn, tk), at(lambda i, j, kk: (j, kk)))
    else:
        b_spec = pl.BlockSpec((tk, tn), at(lambda i, j, kk: (kk, j)))
    return pl.pallas_call(
        body_whole_k if nk == 1 else body, name=name,
        grid=(n // tn, m // tm, nk) if n_outer else (m // tm, n // tn, nk),
        in_specs=[a_spec, b_spec] + [ANY] * len(follows),
        out_specs=pl.BlockSpec((tm, tn), at(lambda i, j, kk: (i, j))),
        out_shape=jax.ShapeDtypeStruct((m, n), out_dtype),
        scratch_shapes=[] if nk == 1 else [pltpu.VMEM((tm, tn), F32)],
        compiler_params=_cparams(("parallel", "parallel", "arbitrary")),
    )(a, b, *follows)


def _matmul_cols(pieces, b, out_dtype, tm, tn, name, after=None):
    m, n = pieces[0].shape[0], b.shape[1]
    widths = [p.shape[1] for p in pieces]
    starts = [sum(widths[:i]) for i in range(len(pieces))]
    follows = [] if after is None else [after]
    tm, tn = min(tm, m), min(tn, n)
    assert sum(widths) == b.shape[0] and m % tm == 0 and n % tn == 0, name

    def body(*refs):
        b_ref, o_ref = refs[len(pieces)], refs[-1]
        acc = None
        for p_ref, at, width in zip(refs, starts, widths):
            part = _dot(p_ref[...], b_ref[at:at + width, :])
            acc = part if acc is None else acc + part
        o_ref[...] = acc.astype(out_dtype)

    return pl.pallas_call(
        body, name=name, grid=(n // tn, m // tm),
        in_specs=[pl.BlockSpec((tm, width), lambda j, i: (i, 0)) for width in widths]
        + [pl.BlockSpec((b.shape[0], tn), lambda j, i: (0, j))] + [ANY] * len(follows),
        out_specs=pl.BlockSpec((tm, tn), lambda j, i: (i, j)),
        out_shape=jax.ShapeDtypeStruct((m, n), out_dtype),
        compiler_params=_cparams(("parallel", "parallel")),
    )(*pieces, b, *follows)


def _matmul_rows(pieces, b, out_dtype, tw, tn, name):
    k, n = b.shape
    tn = min(tn, n)
    counts = [p.shape[1] // tw for p in pieces]
    firsts = [sum(counts[:i]) for i in range(len(pieces))]
    assert all(p.shape[1] % tw == 0 for p in pieces) and n % tn == 0, name

    def body(*refs):
        b_ref, o_ref = refs[len(pieces):]
        for p_ref, first, count in zip(refs, firsts, counts):
            @pl.when((pl.program_id(0) >= first) & (pl.program_id(0) < first + count))
            def _(p_ref=p_ref):
                o_ref[...] = _dot_tn(p_ref[...], b_ref[...]).astype(out_dtype)

    def piece_spec(first, count):
        return pl.BlockSpec((k, tw), lambda i, j: (0, jnp.clip(i - first, 0, count - 1)))

    return pl.pallas_call(
        body, name=name, grid=(sum(counts), n // tn),
        in_specs=[piece_spec(first, count) for first, count in zip(firsts, counts)]
        + [pl.BlockSpec((k, tn), lambda i, j: (0, j))],
        out_specs=pl.BlockSpec((tw, tn), lambda i, j: (i, j)),
        out_shape=jax.ShapeDtypeStruct((sum(counts) * tw, n), out_dtype),
        compiler_params=_cparams(("parallel", "parallel")),
    )(*pieces, b)


ROWS = 512


def _rms_fwd(x, g):
    s, d = x.shape

    def body(x_ref, g_ref, h_ref):
        xv = x_ref[...]
        r = lax.rsqrt(jnp.mean(xv * xv, axis=-1, keepdims=True) + EPS)
        h_ref[...] = (xv * r * g_ref[...]).astype(BF16)

    return pl.pallas_call(
        body, name="rms_fwd", grid=(s // ROWS,),
        in_specs=[pl.BlockSpec((ROWS, d), lambda i: (i, 0)), pl.BlockSpec((1, d), lambda i: (0, 0))],
        out_specs=pl.BlockSpec((ROWS, d), lambda i: (i, 0)),
        out_shape=jax.ShapeDtypeStruct((s, d), BF16),
        compiler_params=_cparams(("parallel",)),
    )(x, g)


def _post_fwd(x, y, g):
    s, d = x.shape

    def body(x_ref, y_ref, g_ref, o_ref):
        yv = y_ref[...]
        r = lax.rsqrt(jnp.mean(yv * yv, axis=-1, keepdims=True) + EPS)
        o_ref[...] = x_ref[...] + yv * r * g_ref[...]

    row = pl.BlockSpec((ROWS, d), lambda i: (i, 0))
    return pl.pallas_call(
        body, name="post_fwd", grid=(s // ROWS,),
        in_specs=[row, row, pl.BlockSpec((1, d), lambda i: (0, 0))],
        out_specs=row,
        out_shape=jax.ShapeDtypeStruct((s, d), F32),
        compiler_params=_cparams(("parallel",)),
    )(x, y, g)


def _loss_head(out, tgt):
    s, d = out.shape

    def body(o_ref, t_ref, dout_ref, sum_ref):
        @pl.when(pl.program_id(0) == 0)
        def _():
            sum_ref[...] = jnp.zeros_like(sum_ref)

        e = o_ref[...] - t_ref[...]
        dout_ref[...] = e * (1.0 / d)
        sum_ref[...] += jnp.sum(jnp.sum(e * e, axis=1, keepdims=True), axis=0, keepdims=True)

    row = pl.BlockSpec((ROWS, d), lambda i: (i, 0))
    return pl.pallas_call(
        body, name="loss_head", grid=(s // ROWS,),
        in_specs=[row, row],
        out_specs=[row, pl.BlockSpec((1, 1), lambda i: (0, 0))],
        out_shape=[jax.ShapeDtypeStruct((s, d), F32), jax.ShapeDtypeStruct((1, 1), F32)],
        compiler_params=_cparams(("arbitrary",)),
    )(out, tgt)


def _post_bwd(dout, y, g):
    s, d = y.shape

    def body(do_ref, y_ref, g_ref, dy_ref, dg_ref):
        @pl.when(pl.program_id(0) == 0)
        def _():
            dg_ref[...] = jnp.zeros_like(dg_ref)

        yv = y_ref[...]
        dv = do_ref[...]
        r = lax.rsqrt(jnp.mean(yv * yv, axis=-1, keepdims=True) + EPS)
        dg_ref[...] += jnp.sum(dv * yv * r, axis=0, keepdims=True)
        w = dv * g_ref[...]
        dy = r * (w - yv * (r * r) * jnp.mean(w * yv, axis=-1, keepdims=True))
        dy_ref[...] = dy.astype(BF16)

    row = pl.BlockSpec((ROWS, d), lambda i: (i, 0))
    vec = pl.BlockSpec((1, d), lambda i: (0, 0))
    return pl.pallas_call(
        body, name="post_bwd", grid=(s // ROWS,),
        in_specs=[row, row, vec],
        out_specs=[row, vec],
        out_shape=[jax.ShapeDtypeStruct((s, d), BF16), jax.ShapeDtypeStruct((1, d), F32)],
        compiler_params=_cparams(("arbitrary",)),
    )(dout, y, g)


def _pre_bwd(dh, x, g, dout):
    s, d = x.shape

    def body(dh_ref, x_ref, g_ref, do_ref, dx_ref, dg_ref):
        @pl.when(pl.program_id(0) == 0)
        def _():
            dg_ref[...] = jnp.zeros_like(dg_ref)

        xv = x_ref[...]
        dv = dh_ref[...]
        r = lax.rsqrt(jnp.mean(xv * xv, axis=-1, keepdims=True) + EPS)
        dg_ref[...] += jnp.sum(dv * xv * r, axis=0, keepdims=True)
        w = dv * g_ref[...]
        dx_ref[...] = do_ref[...] + r * (w - xv * (r * r) * jnp.mean(w * xv, axis=-1, keepdims=True))

    row = pl.BlockSpec((ROWS, d), lambda i: (i, 0))
    vec = pl.BlockSpec((1, d), lambda i: (0, 0))
    return pl.pallas_call(
        body, name="pre_bwd", grid=(s // ROWS,),
        in_specs=[row, row, vec, row],
        out_specs=[row, vec],
        out_shape=[jax.ShapeDtypeStruct((s, d), F32), jax.ShapeDtypeStruct((1, d), F32)],
        compiler_params=_cparams(("arbitrary",)),
    )(dh, x, g, dout)


GLA_STEP = 4
GLA_ROWS = GLA_STEP * CHUNK
GLA_CHUNKS = [slice(c * CHUNK, (c + 1) * CHUNK) for c in range(GLA_STEP)]


def _chunk_triangles():
    ri = lax.broadcasted_iota(jnp.int32, (GLA_ROWS, GLA_ROWS), 0)
    ci = lax.broadcasted_iota(jnp.int32, (GLA_ROWS, GLA_ROWS), 1)
    same = (ri // CHUNK) == (ci // CHUNK)
    return (jnp.where(same & (ri >= ci), 1.0, 0.0).astype(BF16), jnp.where(same & (ci >= ri), 1.0, 0.0).astype(BF16))


def _per_chunk(fn, like):
    row = lax.broadcasted_iota(jnp.int32, like.shape, 0)
    return [fn((row >= c * CHUNK) & (row < (c + 1) * CHUNK)) for c in range(GLA_STEP)]


def _spread(per_chunk, like):
    row = lax.broadcasted_iota(jnp.int32, like.shape, 0)
    out = per_chunk[-1]
    for c in reversed(range(GLA_STEP - 1)):
        out = jnp.where(row < (c + 1) * CHUNK, per_chunk[c], out)
    return out


def _gla_gate(ga_b, wa_b, b_ref, tri):
    pre = _dot(ga_b, wa_b) + b_ref[...]
    la = _log_sigmoid(pre) * (1.0 / GLA_TAU)
    return pre, _dot01(tri, la)


def _chunk_ends(cum):
    row = lax.broadcasted_iota(jnp.int32, cum.shape, 0)
    return [jnp.sum(jnp.where(row == (c + 1) * CHUNK - 1, cum, 0.0), axis=0, keepdims=True)
            for c in range(GLA_STEP)]


def _heads(width):
    return [slice(h * width, (h + 1) * width) for h in range(GLA_HEADS)]


def _z_specs_gla(rev=None):
    idx = (lambda n: n) if rev is None else rev
    return [
        pl.BlockSpec((GLA_ROWS, GLA_KW), lambda n: (idx(n), 0)),
        pl.BlockSpec((GLA_ROWS, GLA_KW), lambda n: (idx(n), 1)),
        pl.BlockSpec((GLA_ROWS, D_GLA), lambda n: (idx(n), 1)),
        pl.BlockSpec((GLA_ROWS, D_GLA), lambda n: (idx(n), 2)),
        pl.BlockSpec((GLA_ROWS, LANE), lambda n: (idx(n), OFF_GA // LANE)),
    ]


def _gla_fwd(z, wa_pad, b_alpha, g_gla):
    s = z.shape[0]
    nchunk = s // CHUNK

    def body(q_ref, k_ref, v_ref, gg_ref, ga_ref, wa_ref, b_ref, g_ref, y_ref, o_ref, st_ref, pre_ref, cum_ref,
             state):
        @pl.when(pl.program_id(0) == 0)
        def _():
            state[...] = jnp.zeros_like(state)

        ga_b = ga_ref[...].astype(BF16)
        tri, _ = _chunk_triangles()
        nh = range(GLA_HEADS)
        keys, vals = _heads(GLA_DK), _heads(GLA_DV)
        pre, cum = _gla_gate(ga_b, wa_ref[...].astype(BF16), b_ref, tri)
        pre_ref[...] = pre
        cum_ref[...] = cum
        cends = _chunk_ends(cum)
        kd_b = (k_ref[...] * jnp.exp(_spread(cends, cum) - cum)).astype(BF16)
        qs = (q_ref[...] * GLA_SCALE).astype(BF16)
        v_b = v_ref[...].astype(BF16)
        uts = [[_dot_tn(v_b[rs, vals[h]], kd_b[rs, keys[h]]) for h in nh] for rs in GLA_CHUNKS]
        sts, prev = [], [state[h] for h in nh]
        for c in range(GLA_STEP):
            a = jnp.exp(cends[c])
            prev = [prev[h] * a[:, keys[h]] + uts[c][h] for h in nh]
            sts.append(prev)
        for h in nh:
            state[h] = prev[h]
            for c in range(GLA_STEP):
                st_ref[c, h] = sts[c][h]
        outs = [[_dot_nt(qs[rs, keys[h]], sts[c][h].astype(BF16)) for h in nh] for c, rs in enumerate(GLA_CHUNKS)]
        for h in nh:
            o, vs = jnp.concatenate([outs[c][h] for c in range(GLA_STEP)], axis=0), vals[h]
            o_ref[:, vs] = o
            r = lax.rsqrt(jnp.mean(o * o, axis=-1, keepdims=True) + EPS)
            gg = gg_ref[:, vs]
            y_ref[:, vs] = (o * r * g_ref[:, vs] * (gg * _sigmoid(gg))).astype(BF16)

    full = lambda shape: pl.BlockSpec(shape, lambda n: tuple(0 for _ in shape))
    wide = pl.BlockSpec((GLA_ROWS, D_GLA), lambda n: (n, 0))
    return pl.pallas_call(
        body, name="gla_fwd", grid=(nchunk // GLA_STEP,),
        in_specs=_z_specs_gla() + [full((LANE, GLA_KW)), full((1, GLA_KW)), full((1, D_GLA))],
        out_specs=[wide, wide, pl.BlockSpec((GLA_STEP, GLA_HEADS, GLA_DV, GLA_DK), lambda n: (n, 0, 0, 0)),
                   pl.BlockSpec((GLA_ROWS, GLA_KW), lambda n: (n, 0)), pl.BlockSpec((GLA_ROWS, GLA_KW), lambda n: (n, 0))],
        out_shape=[jax.ShapeDtypeStruct((s, D_GLA), BF16), jax.ShapeDtypeStruct((s, D_GLA), F32),
                   jax.ShapeDtypeStruct((nchunk, GLA_HEADS, GLA_DV, GLA_DK), F32),
                   jax.ShapeDtypeStruct((s, GLA_KW), F32), jax.ShapeDtypeStruct((s, GLA_KW), F32)],
        scratch_shapes=[pltpu.VMEM((GLA_HEADS, GLA_DV, GLA_DK), F32)],
        compiler_params=_cparams(("arbitrary",)),
    )(z, z, z, z, z, wa_pad, b_alpha, g_gla)


def _gla_bwd(dyc, o_gla, z, wa_pad, g_gla, states, gate_pre, gate_cum):
    s = z.shape[0]
    nsteps = s // GLA_ROWS
    rev = lambda n: nsteps - 1 - n

    def body(dy_ref, o_ref, q_ref, k_ref, v_ref, gg_ref, ga_ref, wa_ref, g_ref, st_ref, stp_ref, pre_ref, cum_ref,
             dq_ref, dk_ref, dv_ref, dgg_ref, dga_ref, dwa_ref, db_ref, dg_ref, carry):
        step = pl.program_id(0)

        @pl.when(step == 0)
        def _():
            carry[...] = jnp.zeros_like(carry)
            dwa_ref[...] = jnp.zeros_like(dwa_ref)
            db_ref[...] = jnp.zeros_like(db_ref)
            dg_ref[...] = jnp.zeros_like(dg_ref)

        has_prev = (step < nsteps - 1).astype(F32)
        ga_b = ga_ref[...].astype(BF16)
        _, tri_up = _chunk_triangles()
        nh, nc = range(GLA_HEADS), range(GLA_STEP)
        keys, vals = _heads(GLA_DK), _heads(GLA_DV)
        wa_b = wa_ref[...].astype(BF16)
        pre, cum = pre_ref[...], cum_ref[...]
        cends = _chunk_ends(cum)
        e = jnp.exp(_spread(cends, cum) - cum)
        a = [jnp.exp(cends[c]) for c in nc]
        kf = k_ref[...]
        kd_b = (kf * e).astype(BF16)
        v_b = v_ref[...].astype(BF16)
        qs = (q_ref[...] * GLA_SCALE).astype(BF16)
        do_b = []
        for h in nh:
            vs = vals[h]
            o = o_ref[:, vs]
            gg = gg_ref[:, vs]
            g = g_ref[:, vs]
            dy = dy_ref[:, vs]
            r = lax.rsqrt(jnp.mean(o * o, axis=-1, keepdims=True) + EPS)
            sg = _sigmoid(gg)
            dogn = dy * (gg * sg)
            dgg_ref[:, vs] = (dy * (o * r * g) * (sg * (1.0 + gg * (1.0 - sg)))).astype(BF16)
            dg_ref[:, vs] += jnp.sum(dogn * o * r, axis=0, keepdims=True)
            w = dogn * g
            do_b.append((r * (w - o * (r * r) * jnp.mean(w * o, axis=-1, keepdims=True))).astype(BF16))
        dqs = [jnp.concatenate([_dot(do_b[h][rs], st_ref[c, h].astype(BF16)) for c, rs in enumerate(GLA_CHUNKS)],
                               axis=0) for h in nh]
        dq_ref[...] = (jnp.concatenate(dqs, axis=1) * GLA_SCALE).astype(BF16)
        own = [[_dot_tn(do_b[h][rs], qs[rs, keys[h]]) for h in nh] for rs in GLA_CHUNKS]
        gts, later = [None] * GLA_STEP, [carry[h] for h in nh]
        for c in reversed(nc):
            gts[c] = [own[c][h] + later[h] for h in nh]
            later = [gts[c][h] * a[c][:, keys[h]] for h in nh]
        for h in nh:
            carry[h] = later[h]
        gt_b = [[gts[c][h].astype(BF16) for h in nh] for c in nc]
        dkd = jnp.concatenate([jnp.concatenate([_dot(v_b[rs, vals[h]], gt_b[c][h]) for h in nh], axis=1)
                               for c, rs in enumerate(GLA_CHUNKS)], axis=0)
        dvs = [[_dot_nt(kd_b[rs, keys[h]], gt_b[c][h]) for h in nh] for c, rs in enumerate(GLA_CHUNKS)]
        before = lambda c, h: st_ref[c - 1, h] if c > 0 else stp_ref[0, h] * has_prev
        da = [jnp.concatenate([jnp.sum(gts[c][h] * before(c, h), axis=0, keepdims=True) for h in nh], axis=1)
              for c in nc]
        for h in nh:
            dv_ref[:, vals[h]] = jnp.concatenate([dvs[c][h] for c in nc], axis=0).astype(BF16)
        dk_ref[...] = (dkd * e).astype(BF16)
        dd = dkd * kf * e
        dsum = _per_chunk(lambda mine: jnp.sum(jnp.where(mine, dd, 0.0), axis=0, keepdims=True), dd)
        dcend = _spread([dsum[c] + da[c] * a[c] for c in nc], dd)
        dla = dcend - _dot01(tri_up, dd)
        dpre = dla * (1.0 / GLA_TAU) * (1.0 - _sigmoid(pre))
        dpre_b = dpre.astype(BF16)
        dga_ref[...] = _dot_nt(dpre_b, wa_b).astype(BF16)
        dwa_ref[...] += _dot_tn(ga_b, dpre_b)
        db_ref[...] += jnp.sum(dpre, axis=0, keepdims=True)

    full = lambda shape: pl.BlockSpec(shape, lambda n: tuple(0 for _ in shape))
    wide = pl.BlockSpec((GLA_ROWS, D_GLA), lambda n: (rev(n), 0))
    keyw = pl.BlockSpec((GLA_ROWS, GLA_KW), lambda n: (rev(n), 0))
    st_spec = pl.BlockSpec((GLA_STEP, GLA_HEADS, GLA_DV, GLA_DK), lambda n: (rev(n), 0, 0, 0))
    stp_spec = pl.BlockSpec((1, GLA_HEADS, GLA_DV, GLA_DK),
                            lambda n: (jnp.maximum(GLA_STEP * rev(n) - 1, 0), 0, 0, 0))
    return pl.pallas_call(
        body, name="gla_bwd", grid=(nsteps,),
        in_specs=[wide, wide] + _z_specs_gla(rev)
        + [full((LANE, GLA_KW)), full((1, D_GLA)), st_spec, stp_spec, keyw, keyw],
        out_specs=[keyw, keyw, wide, wide, pl.BlockSpec((GLA_ROWS, LANE), lambda n: (rev(n), 0)),
                   full((LANE, GLA_KW)), full((1, GLA_KW)), full((1, D_GLA))],
        out_shape=[jax.ShapeDtypeStruct((s, GLA_KW), BF16), jax.ShapeDtypeStruct((s, GLA_KW), BF16),
                   jax.ShapeDtypeStruct((s, D_GLA), BF16), jax.ShapeDtypeStruct((s, D_GLA), BF16),
                   jax.ShapeDtypeStruct((s, LANE), BF16),
                   jax.ShapeDtypeStruct((LANE, GLA_KW), F32), jax.ShapeDtypeStruct((1, GLA_KW), F32),
                   jax.ShapeDtypeStruct((1, D_GLA), F32)],
        scratch_shapes=[pltpu.VMEM((GLA_HEADS, GLA_DV, GLA_DK), F32)],
        compiler_params=_cparams(("arbitrary",)),
    )(dyc, o_gla, z, z, z, z, z, wa_pad, g_gla, states, states, gate_pre, gate_cum)


def _build_bias_table(rb_row, et_ref):
    far = jnp.broadcast_to(rb_row[:, 2 * REL_CLIP:2 * REL_CLIP + 1], (1, LANE))
    near_hi = rb_row[:, REL_CLIP:2 * REL_CLIP]
    near_lo = rb_row[:, 0:REL_CLIP]
    past = jnp.broadcast_to(rb_row[:, 0:1], (1, LANE))
    seg = [far, far, far, far, near_hi, near_lo] + [past] * (ET_ROWS // LANE - 5)
    ri = lax.broadcasted_iota(jnp.int32, (LANE, LANE), 0)
    ci = lax.broadcasted_iota(jnp.int32, (LANE, LANE), 1)
    for kb in range(ET_ROWS // LANE):
        wmat = jnp.where(ri + ci < LANE, seg[kb], seg[kb + 1])
        blk = pltpu.roll(wmat, 0, 1, stride=1, stride_axis=0)
        lag = LEFT_CHUNKS + ci // CHUNK - (2 * kb + ri // CHUNK)
        et_ref[kb * LANE:(kb + 1) * LANE, :] = jnp.where((lag >= 0) & (lag <= LEFT_CHUNKS), blk, NEG)


def _reduce_bias_table(det_ref):
    lane = lax.broadcasted_iota(jnp.int32, (1, LANE), 1)
    ri = lax.broadcasted_iota(jnp.int32, (LANE, LANE), 0)
    ci = lax.broadcasted_iota(jnp.int32, (LANE, LANE), 1)
    flip = jnp.where(ri + ci == LANE - 1, 1.0, 0.0).astype(BF16)
    segs = jnp.zeros((8, LANE), F32)
    seg_row = lax.broadcasted_iota(jnp.int32, (8, LANE), 0)
    prev_minus = jnp.zeros((1, LANE), F32)
    for kb in range(6):
        rolled = pltpu.roll(_dot01(det_ref[kb * LANE:(kb + 1) * LANE, :], flip, left=False), 0, 1,
                            stride=1, stride_axis=0)
        plus = jnp.sum(jnp.where(ci >= ri, rolled, 0.0), axis=0, keepdims=True)
        minus = jnp.sum(jnp.where(ci < ri, rolled, 0.0), axis=0, keepdims=True)
        segs = segs + jnp.where(seg_row == kb, plus + prev_minus, 0.0)
        prev_minus = minus
    segs = _dot01(segs, flip, left=False)
    pick = lambda kb: jnp.sum(jnp.where(seg_row == kb, segs, 0.0), axis=0, keepdims=True)
    far = jnp.sum(pick(0) + pick(1) + pick(2) + pick(3), axis=1, keepdims=True)
    last = jnp.where(lane == 0, far, 0.0)
    return jnp.concatenate([pick(5), pick(4), last], axis=1)


def _att_window(b):
    c0 = 2 * b
    kstart = pl.multiple_of(jnp.maximum(c0 - LEFT_CHUNKS, 0) * CHUNK, CHUNK)
    eoff = pl.multiple_of(jnp.maximum(LEFT_CHUNKS - c0, 0) * CHUNK, CHUNK)
    return kstart, eoff


def _att_probs(q_b, kw_b, et):
    st = _dot_nt(kw_b, q_b) * ATT_SCALE + et
    m = jnp.max(st, axis=0, keepdims=True)
    ex = jnp.exp(st - m)
    return ex * (1.0 / jnp.sum(ex, axis=0, keepdims=True))


def _att_fwd(z, rb_pad, g_att):
    s = z.shape[0]
    nblk = s // QB
    c_aq, c_ak, c_av, c_ag = [(OFF_AQ + i * D_ATT) // ATT_HD for i in range(4)]

    def body(q_ref, k_ref, v_ref, ag_ref, rb_ref, g_ref, y_ref, o_ref, p_ref, et_ref, kb_ref, vb_ref):
        h = pl.program_id(0)
        b = pl.program_id(1)

        @pl.when(b == 0)
        def _():
            _build_bias_table(rb_ref[pl.ds(h, 1), :], et_ref)
            kb_ref[...] = k_ref[...].astype(BF16)
            vb_ref[...] = v_ref[...].astype(BF16)

        for j in range(ATT_UNROLL):
            rs = slice(j * QB, (j + 1) * QB)
            kstart, eoff = _att_window(b * ATT_UNROLL + j)
            q_b = q_ref[rs, :].astype(BF16)
            kw_b = kb_ref[pl.ds(kstart, WIN), :]
            vw_b = vb_ref[pl.ds(kstart, WIN), :]
            pt = _att_probs(q_b, kw_b, et_ref[pl.ds(eoff, WIN), :])
            p_ref[0, j] = pt
            o = _dot_tn(pt.astype(BF16), vw_b)
            o_ref[rs, :] = o
            r = lax.rsqrt(jnp.mean(o * o, axis=-1, keepdims=True) + EPS)
            ag = ag_ref[rs, :]
            y_ref[rs, :] = (o * r * g_ref[...] * (ag * _sigmoid(ag))).astype(BF16)

    blk = lambda col: pl.BlockSpec((ATT_UNROLL * QB, ATT_HD), lambda h, b: (b, col + h))
    seq = lambda col: pl.BlockSpec((s, ATT_HD), lambda h, b: (0, col + h))
    out_blk = pl.BlockSpec((ATT_UNROLL * QB, ATT_HD), lambda h, b: (b, h))
    return pl.pallas_call(
        body, name="att_fwd", grid=(ATT_HEADS, nblk // ATT_UNROLL),
        in_specs=[blk(c_aq), seq(c_ak), seq(c_av), blk(c_ag),
                  pl.BlockSpec((ATT_HEADS, 3 * LANE), lambda h, b: (0, 0)),
                  pl.BlockSpec((1, ATT_HD), lambda h, b: (0, h))],
        out_specs=[out_blk, out_blk, pl.BlockSpec((1, ATT_UNROLL, WIN, QB), lambda h, b: (h, b, 0, 0))],
        out_shape=[jax.ShapeDtypeStruct((s, D_ATT), BF16), jax.ShapeDtypeStruct((s, D_ATT), F32),
                   jax.ShapeDtypeStruct((ATT_HEADS, nblk, WIN, QB), F32)],
        scratch_shapes=[pltpu.VMEM((ET_ROWS, LANE), F32), pltpu.VMEM((s, ATT_HD), BF16),
                        pltpu.VMEM((s, ATT_HD), BF16)],
        compiler_params=_cparams(("arbitrary", "arbitrary")),
    )(z, z, z, z, rb_pad, g_att)


def _att_bwd(dyc, o_att, probs, z, g_att):
    s = z.shape[0]
    nblk = s // QB
    c_aq, c_ak, c_av, c_ag = [(OFF_AQ + i * D_ATT) // ATT_HD for i in range(4)]
    c_dy = D_GLA // ATT_HD

    def body(dy_ref, o_ref, p_ref, q_ref, k_ref, v_ref, ag_ref, g_ref,
             dq_ref, dk_ref, dv_ref, dag_ref, drb_ref, dg_ref, det_ref, kb_ref, vb_ref, dk_acc, dv_acc):
        b = pl.program_id(1)

        @pl.when(b == 0)
        def _():
            kb_ref[...] = k_ref[...].astype(BF16)
            vb_ref[...] = v_ref[...].astype(BF16)
            det_ref[...] = jnp.zeros_like(det_ref)
            dk_acc[...] = jnp.zeros_like(dk_acc)
            dv_acc[...] = jnp.zeros_like(dv_acc)
            dg_ref[...] = jnp.zeros_like(dg_ref)

        g = g_ref[...]
        dg = jnp.zeros((1, ATT_HD), F32)
        for j in range(ATT_UNROLL):
            rs = slice(j * QB, (j + 1) * QB)
            kstart, eoff = _att_window(b * ATT_UNROLL + j)
            q_b = q_ref[rs, :].astype(BF16)
            kw_b = kb_ref[pl.ds(kstart, WIN), :]
            vw_b = vb_ref[pl.ds(kstart, WIN), :]
            pt = p_ref[0, j]
            o = o_ref[rs, :]
            ag = ag_ref[rs, :]
            dy = dy_ref[rs, :]
            r = lax.rsqrt(jnp.mean(o * o, axis=-1, keepdims=True) + EPS)
            sg = _sigmoid(ag)
            don = dy * (ag * sg)
            dag_ref[rs, :] = (dy * (o * r * g) * (sg * (1.0 + ag * (1.0 - sg)))).astype(BF16)
            dg = dg + jnp.sum(don * o * r, axis=0, keepdims=True)
            w = don * g
            do_b = (r * (w - o * (r * r) * jnp.mean(w * o, axis=-1, keepdims=True))).astype(BF16)
            pt_b = pt.astype(BF16)
            dpt = _dot_nt(vw_b, do_b)
            dst = pt * (dpt - jnp.sum(dpt * pt, axis=0, keepdims=True))
            det_ref[pl.ds(eoff, WIN), :] += dst
            ds_b = (dst * ATT_SCALE).astype(BF16)
            dq_ref[rs, :] = _dot_tn(ds_b, kw_b).astype(BF16)
            dk_acc[pl.ds(kstart, WIN), :] += _dot(ds_b, q_b)
            dv_acc[pl.ds(kstart, WIN), :] += _dot(pt_b, do_b)
        dg_ref[...] += dg

        @pl.when(b == nblk // ATT_UNROLL - 1)
        def _():
            drb_ref[0] = jnp.broadcast_to(_reduce_bias_table(det_ref), (8, 3 * LANE))
            dk_ref[...] = dk_acc[...].astype(BF16)
            dv_ref[...] = dv_acc[...].astype(BF16)

    blk = lambda col: pl.BlockSpec((ATT_UNROLL * QB, ATT_HD), lambda h, b: (b, col + h))
    seq = lambda col: pl.BlockSpec((s, ATT_HD), lambda h, b: (0, col + h))
    out_blk = pl.BlockSpec((ATT_UNROLL * QB, ATT_HD), lambda h, b: (b, h))
    out_seq = pl.BlockSpec((s, ATT_HD), lambda h, b: (0, h))
    return pl.pallas_call(
        body, name="att_bwd", grid=(ATT_HEADS, nblk // ATT_UNROLL),
        in_specs=[blk(c_dy), blk(0), pl.BlockSpec((1, ATT_UNROLL, WIN, QB), lambda h, b: (h, b, 0, 0)),
                  blk(c_aq), seq(c_ak), seq(c_av), blk(c_ag),
                  pl.BlockSpec((1, ATT_HD), lambda h, b: (0, h))],
        out_specs=[out_blk, out_seq, out_seq, out_blk,
                   pl.BlockSpec((1, 8, 3 * LANE), lambda h, b: (h, 0, 0)),
                   pl.BlockSpec((1, ATT_HD), lambda h, b: (0, h))],
        out_shape=[jax.ShapeDtypeStruct((s, D_ATT), BF16), jax.ShapeDtypeStruct((s, D_ATT), BF16),
                   jax.ShapeDtypeStruct((s, D_ATT), BF16), jax.ShapeDtypeStruct((s, D_ATT), BF16),
                   jax.ShapeDtypeStruct((ATT_HEADS, 8, 3 * LANE), F32),
                   jax.ShapeDtypeStruct((1, D_ATT), F32)],
        scratch_shapes=[pltpu.VMEM((ET_ROWS, LANE), F32),
                        pltpu.VMEM((s, ATT_HD), BF16), pltpu.VMEM((s, ATT_HD), BF16),
                        pltpu.VMEM((s, ATT_HD), F32), pltpu.VMEM((s, ATT_HD), F32)],
        compiler_params=_cparams(("arbitrary", "arbitrary")),
    )(dyc, o_att, probs, z, z, z, z, g_att)


ADAM_ROWS = 64
ADAM_COL_ROWS = 32


def _adam_math(w, g, m, v):
    m2 = ADAM_B1 * m + (1.0 - ADAM_B1) * g
    v2 = ADAM_B2 * v + (1.0 - ADAM_B2) * (g * g)
    m_hat = m2 / (1.0 - ADAM_B1 ** ADAM_STEP)
    v_hat = v2 / (1.0 - ADAM_B2 ** ADAM_STEP)
    delta = -ADAM_LR * (m_hat / (jnp.sqrt(v_hat) + ADAM_EPS) + ADAM_WD * w)
    return delta, m2, v2


def _adam_sharded(parts, first, w, m, v, name):
    nl, nr, nc = w.shape

    def body(*refs):
        p_refs = refs[:nl]
        w_ref, m_ref, v_ref, g_ref, d_ref, m2_ref, v2_ref = refs[nl:]
        for k in range(nl):
            @pl.when(pl.program_id(0) == k)
            def _(p_ref=p_refs[k]):
                g = p_ref[0].astype(F32)
                for dev in range(1, N_DEV):
                    g = g + p_ref[dev].astype(F32)
                delta, m2, v2 = _adam_math(w_ref[0], g, m_ref[0], v_ref[0])
                g_ref[0] = g
                d_ref[0] = delta
                m2_ref[0] = m2
                v2_ref[0] = v2

    def part_spec(k):
        return pl.BlockSpec((N_DEV, ADAM_ROWS, nc), lambda l, i: (0, first + jnp.where(l == k, i, 0), 0))

    blk = pl.BlockSpec((1, ADAM_ROWS, nc), lambda l, i: (l, i, 0))
    shp = jax.ShapeDtypeStruct(w.shape, F32)
    return pl.pallas_call(
        body, name=name, grid=(nl, pl.cdiv(nr, ADAM_ROWS)),
        in_specs=[part_spec(k) for k in range(nl)] + [blk, blk, blk],
        out_specs=[blk, blk, blk, blk],
        out_shape=[shp, shp, shp, shp],
        compiler_params=_cparams(("arbitrary", "arbitrary")),
    )(*parts, w, m, v)


def _adam_columns(parts, first, w, m, v):
    nc, nl, d = w.shape

    def body(*refs):
        p_refs = refs[:nl]
        w_ref, m_ref, v_ref, g_ref, d_ref, m2_ref, v2_ref = refs[nl:]
        for l in range(nl):
            g = p_refs[l][0].astype(F32)
            for slot in range(1, parts[l].shape[0]):
                g = g + p_refs[l][slot].astype(F32)
            delta, m2, v2 = _adam_math(w_ref[:, l, :], g, m_ref[:, l, :], v_ref[:, l, :])
            g_ref[:, l, :] = g
            d_ref[:, l, :] = delta
            m2_ref[:, l, :] = m2
            v2_ref[:, l, :] = v2

    blk = pl.BlockSpec((ADAM_COL_ROWS, nl, d), lambda i: (i, 0, 0))
    shp = jax.ShapeDtypeStruct(w.shape, F32)
    return pl.pallas_call(
        body, name="adam_w_in", grid=(pl.cdiv(nc, ADAM_COL_ROWS),),
        in_specs=[pl.BlockSpec((p.shape[0], ADAM_COL_ROWS, d), lambda i: (0, first + i, 0)) for p in parts]
        + [blk, blk, blk],
        out_specs=[blk, blk, blk, blk],
        out_shape=[shp, shp, shp, shp],
        compiler_params=_cparams(("parallel",)),
    )(*parts, w, m, v)


def _adam_small(ws, gs, ms, vs):
    n = len(ws)

    def body(*refs):
        w_refs, g_refs, m_refs, v_refs, d_refs, m2_refs, v2_refs = [refs[i * n:(i + 1) * n] for i in range(7)]
        for i in range(n):
            delta, m2, v2 = _adam_math(w_refs[i][...], g_refs[i][...], m_refs[i][...], v_refs[i][...])
            d_refs[i][...] = delta
            m2_refs[i][...] = m2
            v2_refs[i][...] = v2

    shapes = [jax.ShapeDtypeStruct(w.shape, F32) for w in ws]
    out = pl.pallas_call(body, name="adam_small", out_shape=shapes * 3)(*ws, *gs, *ms, *vs)
    return out[:n], out[n:2 * n], out[2 * n:]


def _position():
    return lax.axis_index("x"), lax.axis_index("y"), lax.axis_index("c")


def _slot(p):
    return 4 * p[0] + 2 * p[1] + p[2]


BF16_TILE_ROWS = 16


def _slab_rows(rows, cols):
    return -(-(rows + cols) // BF16_TILE_ROWS) * BF16_TILE_ROWS


RELAYOUT_COLS = 1024
RELAYOUT_CHUNK = 64


def _shard_pieces(dev, rows, cols):
    moved = ((0, GA_ORIG, 0), (GA_ORIG, GA_ORIG + GLA_RANK, OFF_GA - GA_ORIG), (GA_ORIG + GLA_RANK, D_IN, -GLA_RANK))
    c0, c1 = dev * cols, (dev + 1) * cols
    return [(rows + max(c0, lo) - c0, max(c0, lo) + off, min(c1, hi) - max(c0, lo))
            for lo, hi, off in moved if max(c0, lo) < min(c1, hi)]


def _move_rows(src, src_row, dst, dst_row, n):
    assert src_row % 2 == 0 and dst_row % 2 == 0 and n % 2 == 0
    for r in range(0, n // 2, RELAYOUT_CHUNK):
        m = min(RELAYOUT_CHUNK, n // 2 - r)
        dst[dst_row // 2 + r:dst_row // 2 + r + m, :] = src[src_row // 2 + r:src_row // 2 + r + m, :]


def _aligned_weight(land, rows, cols):
    _, slab, d = land.shape
    ct = min(RELAYOUT_COLS, d)

    def body(land_ref, wt_ref, wo_ref):
        dev = pl.program_id(1)
        src = land_ref.bitcast(jnp.uint32)
        dst = wt_ref.bitcast(jnp.uint32)
        wo_ref[...] = land_ref[0:rows, :]

        @pl.when(dev == 0)
        def _():
            dst[D_IN // 2:D_ZP // 2, :] = jnp.zeros(((D_ZP - D_IN) // 2, ct), jnp.uint32)

        for k in range(N_DEV):
            @pl.when(dev == k)
            def _(k=k):
                for at, to, n in _shard_pieces(k, rows, cols):
                    _move_rows(src, at, dst, to, n)

    return pl.pallas_call(
        body, name="aligned_weight", grid=(d // ct, N_DEV),
        in_specs=[pl.BlockSpec((slab, ct), lambda c, dev: (dev, c))],
        out_specs=[pl.BlockSpec((D_ZP, ct), lambda c, dev: (0, c)),
                   pl.BlockSpec((rows, ct), lambda c, dev: (dev, c))],
        out_shape=[jax.ShapeDtypeStruct((D_ZP, d), land.dtype),
                   jax.ShapeDtypeStruct((N_DEV * rows, d), land.dtype)],
        compiler_params=_cparams(("parallel", "arbitrary")),
    )(land.reshape(N_DEV * slab, d))


def _partial_slabs(dwt, cols, by_core=False):
    d = dwt[0].shape[1]
    bounds = (0, GA_ORIG, OFF_GA, D_ZP)
    assert tuple(a.shape[0] for a in dwt) == tuple(hi - lo for lo, hi in zip(bounds, bounds[1:]))
    slab = _slab_rows(0, cols)
    ct = min(RELAYOUT_COLS, d)

    def body(*refs):
        out_ref = refs[-1]
        dev = pl.program_id(1)
        srcs = [ref.bitcast(jnp.uint32) for ref in refs[:-1]]
        dst = out_ref.bitcast(jnp.uint32)
        dst[cols // 2:slab // 2, :] = jnp.zeros(((slab - cols) // 2, ct), jnp.uint32)
        for k in range(N_DEV):
            @pl.when(dev == k)
            def _(k=k):
                for to, at, n in _shard_pieces(k, 0, cols):
                    which = max(i for i, lo in enumerate(bounds[:-1]) if lo <= at)
                    assert at + n <= bounds[which + 1]
                    _move_rows(srcs[which], at - bounds[which], dst, to, n)

    place = (lambda dev: (dev % 2) * (N_DEV // 2) + dev // 2) if by_core else (lambda dev: dev)
    out = pl.pallas_call(
        body, name="partial_slabs", grid=(d // ct, N_DEV),
        in_specs=[pl.BlockSpec((a.shape[0], ct), lambda c, dev: (0, c)) for a in dwt],
        out_specs=pl.BlockSpec((slab, ct), lambda c, dev: (place(dev), c)),
        out_shape=jax.ShapeDtypeStruct((N_DEV * slab, d), dwt[0].dtype),
        compiler_params=_cparams(("parallel", "arbitrary")),
    )(*dwt)
    return out.reshape((2, N_DEV // 2, slab, d) if by_core else (N_DEV, slab, d))


def _pair_sum(mine, theirs):
    _, nchip, slab, d = mine.shape
    rows = next(r for r in range(512, 0, -BF16_TILE_ROWS) if slab % r == 0)

    def body(m_ref, t_ref, o_ref):
        south = lax.axis_index("c") == 0
        own = jnp.where(south, m_ref[0, 0], m_ref[1, 0]).astype(F32)
        got = jnp.where(south, t_ref[1, 0], t_ref[0, 0]).astype(F32)
        o_ref[0] = (own + got).astype(o_ref.dtype)

    both = pl.BlockSpec((2, 1, rows, d), lambda j, i: (0, j, i, 0))
    return pl.pallas_call(
        body, name="pair_sum", grid=(nchip, slab // rows),
        in_specs=[both, both],
        out_specs=pl.BlockSpec((1, rows, d), lambda j, i: (j, i, 0)),
        out_shape=jax.ShapeDtypeStruct((nchip, slab, d), mine.dtype),
        compiler_params=_cparams(("parallel", "parallel")),
    )(mine, theirs)


def _peer(pos, k):
    x, y, c = pos
    return (1 - x if k & 4 else x, 1 - y if k & 2 else y, 1 - c if k & 1 else c)


HBM_SPEC = pl.BlockSpec(memory_space=pltpu.HBM)
SEM_SPEC = pl.BlockSpec(memory_space=pltpu.SEMAPHORE)
GATHER_PEERS = (1, 4, 2, 6)
ALL_PEERS = (1, 2, 3, 4, 5, 6, 7)


def _hbm(a):
    return pltpu.with_memory_space_constraint(a, pltpu.HBM)


BY_DEVICE = (_slot, N_DEV)
BY_CORE = (lambda p: p[2], 2)
BY_CHIP = (lambda p: 2 * p[0] + p[1], 4)


def _split_copies(src_ref, land_ref, send_sems, recv_sems, ks, per_peer, landed, slots):
    slot_of = slots[0]
    me = _position()
    out = []
    for i, k in enumerate(ks):
        peer = _peer(me, k)
        src = src_ref.at[slot_of(peer)] if per_peer else src_ref
        dst = land_ref.at[slot_of(peer) if landed else slot_of(me)]
        out.append(pltpu.make_async_remote_copy(
            src_ref=src, dst_ref=dst, send_sem=send_sems.at[i], recv_sem=recv_sems.at[i],
            device_id=peer, device_id_type=MESH))
    return out


def _exchange_start(src, after, ks, per_peer, name, slots=BY_DEVICE):
    slab = src.shape[1:] if per_peer else src.shape
    land_shape = (slots[1],) + tuple(slab)
    n = len(ks)

    def body(src_ref, land_ref, after_ref, send_sems, recv_sems, src_thru, land_thru, token):
        for cp in _split_copies(src_ref, land_ref, send_sems, recv_sems, ks, per_peer, False, slots):
            cp.start()
        token[...] = jnp.zeros_like(token)

    return pl.pallas_call(
        body, name=name,
        out_shape=(pltpu.SemaphoreType.DMA((n,)), pltpu.SemaphoreType.DMA((n,)),
                   pltpu.HBM(src.shape, src.dtype), pltpu.HBM(land_shape, src.dtype),
                   jax.ShapeDtypeStruct((8, LANE), F32)),
        in_specs=(HBM_SPEC, HBM_SPEC, ANY),
        out_specs=(SEM_SPEC, SEM_SPEC, HBM_SPEC, HBM_SPEC, pl.BlockSpec(memory_space=pltpu.VMEM)),
        input_output_aliases={0: 2, 1: 3},
        compiler_params=pltpu.CompilerParams(has_side_effects=pltpu.SideEffectType.DATAFLOW_SIDE_EFFECTING),
    )(_hbm(src), _hbm(lax.empty(land_shape, src.dtype)), after)


def _exchange_wait(started, after, ks, per_peer, name, slots=BY_DEVICE):
    send_sems, recv_sems, src_thru, land_thru = started

    def body(src_ref, land_ref, send_sems, recv_sems, after_ref, src_dead, land_out):
        for cp in _split_copies(src_ref, land_ref, send_sems, recv_sems, ks, per_peer, True, slots):
            cp.wait_send()
            cp.wait_recv()

    return pl.pallas_call(
        body, name=name,
        out_shape=(pltpu.HBM(src_thru.shape, src_thru.dtype), pltpu.HBM(land_thru.shape, land_thru.dtype)),
        in_specs=(HBM_SPEC, HBM_SPEC, SEM_SPEC, SEM_SPEC, ANY), out_specs=(HBM_SPEC, HBM_SPEC),
        input_output_aliases={0: 0, 1: 1},
        compiler_params=pltpu.CompilerParams(has_side_effects=pltpu.SideEffectType.DATAFLOW_SIDE_EFFECTING),
    )(src_thru, land_thru, send_sems, recv_sems, after)


def _relay_copies(land_ref, send_sems, recv_sems, landed):
    me = _position()
    sibling = _peer(me, 1)
    out = []
    for i, k in enumerate(GATHER_PEERS[1:]):
        blk = land_ref.at[_slot(_peer(sibling if landed else me, k))]
        out.append(pltpu.make_async_remote_copy(
            src_ref=blk, dst_ref=blk, send_sem=send_sems.at[i], recv_sem=recv_sems.at[i],
            device_id=sibling, device_id_type=MESH))
    return out


def _relay_start(land, name):
    n = len(GATHER_PEERS) - 1

    def body(land_ref, send_sems, recv_sems, land_thru, token):
        for cp in _relay_copies(land_ref, send_sems, recv_sems, landed=False):
            cp.start()
        token[...] = jnp.zeros_like(token)

    return pl.pallas_call(
        body, name=name,
        out_shape=(pltpu.SemaphoreType.DMA((n,)), pltpu.SemaphoreType.DMA((n,)),
                   pltpu.HBM(land.shape, land.dtype), jax.ShapeDtypeStruct((8, LANE), F32)),
        in_specs=(HBM_SPEC,),
        out_specs=(SEM_SPEC, SEM_SPEC, HBM_SPEC, pl.BlockSpec(memory_space=pltpu.VMEM)),
        input_output_aliases={0: 2},
        compiler_params=pltpu.CompilerParams(has_side_effects=pltpu.SideEffectType.DATAFLOW_SIDE_EFFECTING),
    )(_hbm(land))


def _relay_wait(started, after, name):
    send_sems, recv_sems, land_thru = started

    def body(land_ref, send_sems, recv_sems, after_ref, land_out):
        for cp in _relay_copies(land_ref, send_sems, recv_sems, landed=True):
            cp.wait_send()
            cp.wait_recv()

    return pl.pallas_call(
        body, name=name,
        out_shape=pltpu.HBM(land_thru.shape, land_thru.dtype),
        in_specs=(HBM_SPEC, SEM_SPEC, SEM_SPEC, ANY), out_specs=HBM_SPEC,
        input_output_aliases={0: 0},
        compiler_params=pltpu.CompilerParams(has_side_effects=pltpu.SideEffectType.DATAFLOW_SIDE_EFFECTING),
    )(land_thru, send_sems, recv_sems, after)


def _share(vec, name, after=None):
    follows = [] if after is None else [after]

    def body(vec_ref, *rest):
        out_ref, send_sems, recv_sems, local_sem = rest[len(follows):]
        me = _position()

        def copy(k, landed):
            peer = _peer(me, k)
            return pltpu.make_async_remote_copy(
                src_ref=vec_ref, dst_ref=out_ref.at[_slot(peer) if landed else _slot(me)],
                send_sem=send_sems.at[k - 1], recv_sem=recv_sems.at[k - 1], device_id=peer, device_id_type=MESH)

        mine = pltpu.make_async_copy(vec_ref, out_ref.at[_slot(me)], local_sem)
        mine.start()
        sent = [copy(k, False) for k in ALL_PEERS]
        for cp in sent:
            cp.start()
        for k in ALL_PEERS:
            copy(k, True).wait_recv()
        for cp in sent:
            cp.wait_send()
        mine.wait()

    return pl.pallas_call(
        body, name=name,
        in_specs=[ANY] * (1 + len(follows)), out_specs=ANY,
        out_shape=jax.ShapeDtypeStruct((N_DEV,) + vec.shape, vec.dtype),
        scratch_shapes=[pltpu.SemaphoreType.DMA((N_DEV - 1,)), pltpu.SemaphoreType.DMA((N_DEV - 1,)),
                        pltpu.SemaphoreType.DMA],
    )(vec, *follows)


def _sum_slots(parts):
    def body(p_ref, o_ref):
        acc = p_ref[0]
        for dev in range(1, N_DEV):
            acc = acc + p_ref[dev]
        o_ref[...] = acc

    return pl.pallas_call(body, name="sum_slots",
                          out_shape=jax.ShapeDtypeStruct(parts.shape[1:], F32))(parts)


PACK_ROWS = 8


def _packed_rows(size):
    return -(-size // (PACK_ROWS * LANE)) * PACK_ROWS


def _pack(arrs):
    def rows(a):
        flat = a.reshape(-1)
        return jnp.pad(flat, (0, _packed_rows(flat.shape[0]) * LANE - flat.shape[0])).reshape(-1, LANE)

    return jnp.concatenate([rows(a) for a in arrs], axis=0)


def _unpack(packed, shapes):
    out, at = [], 0
    for shp in shapes:
        size = 1
        for dim in shp:
            size *= dim
        nrows = _packed_rows(size)
        out.append(packed[at:at + nrows].reshape(-1)[:size].reshape(shp))
        at += nrows
    return out


def _layer_fwd(x, wt, wo, g_pre, g_post, wa_pad, b_alpha, g_gla, g_att, rb_pad, midway=None):
    h = _rms_fwd(x, g_pre)
    z = _matmul(h, wt, "nt", F32, *TILES["in_proj"], "in_proj", n_outer=True)
    y_gla, o_gla, *gla_kept = _gla_fwd(z, wa_pad, b_alpha, g_gla)
    y_att, o_att, probs = _att_fwd(z, rb_pad, g_att)
    token = None if midway is None else midway(y_att)
    y = _matmul_cols([y_gla, y_att], wo, F32, *TILES["out_proj"][:2], "out_proj", after=token)
    out = _post_fwd(x, y, g_post)
    return out, (x, h, z, o_gla, gla_kept, o_att, probs, y_gla, y_att, y)


def _layer_bwd(dout, saved, wt, wo, g_pre, g_post, wa_pad, b_alpha, g_gla, g_att, rb_pad, on_dwo, on_dwt):
    x, h, z, o_gla, gla_kept, o_att, probs, y_gla, y_att, y = saved
    dy, dg_post = _post_bwd(dout, y, g_post)
    dwo = _matmul_rows([y_gla, y_att], dy, BF16, *TILES["out_proj_dw"][:2], "out_proj_dw")
    token = on_dwo(dwo)
    dycat = _matmul(dy, wo, "nt", F32, *TILES["out_proj_dx"], "out_proj_dx", n_outer=True, after=token)
    dq, dk, dv, dgg, dga, dwa, db, dg_gla = _gla_bwd(dycat, o_gla, z, wa_pad, g_gla, *gla_kept)
    daq, dak, dav, dag, drb, dg_att = _att_bwd(dycat, o_att, probs, z, g_att)
    tw, tn = TILES["in_proj_dw"][:2]
    dwt = (_matmul_rows([dq, dk, dv, dgg], h, BF16, tw, tn, "in_proj_dw_gla"),
           _matmul_rows([daq, dak, dav, dag], h, BF16, tw, tn, "in_proj_dw_att"),
           _matmul_rows([dga], h, BF16, LANE, tn, "in_proj_dw_gate"))
    token = on_dwt(dwt)
    dh = _matmul_cols([dq, dk, dv, dgg, daq, dak, dav, dag, dga], wt, F32, *TILES["in_proj_dx"][:2],
                      "in_proj_dx", after=token)
    dx, dg_pre = _pre_bwd(dh, x, g_pre, dout)
    small = (dg_pre[0], dg_post[0], dwa[:GLA_RANK], db[0], dg_gla[0], dg_att[0], drb[:, 0, :N_REL])
    return dx, small


def kernel(x, w_in, w_out, g_pre, g_post, w_alpha, b_alpha, g_gla, g_att, rel_bias, loss_target, m_w_in, m_w_out, m_g_pre, m_g_post, m_w_alpha, m_b_alpha, m_g_gla, m_g_att, m_rel_bias, v_w_in, v_w_out, v_g_pre, v_g_post, v_w_alpha, v_b_alpha, v_g_gla, v_g_att, v_rel_bias):
    nl, d, cols = w_in.shape
    rows = w_out.shape[1]
    s = x.shape[1]
    x0 = x.reshape(s, d)
    tgt = loss_target.reshape(s, d)

    cols_first = lambda a: jnp.transpose(a, (2, 0, 1))
    w_c = cols_first(w_in)
    slab = _slab_rows(rows, cols)
    is_out = lax.broadcasted_iota(jnp.int32, (slab, d), 0) < rows

    def shard(l, zero=0.0):
        top = jnp.pad((w_out[l] + zero).astype(BF16), ((0, slab - rows), (0, 0)))
        rest = jnp.pad((w_c[:, l] + zero).astype(BF16), ((rows, slab - rows - cols), (0, 0)))
        return jnp.where(is_out, top, rest)

    first_fetch = _exchange_start(shard(0), x, GATHER_PEERS, False, "gather_start_0")
    began = first_fetch[4][0, 0]
    shards = [None] + [shard(l, began) for l in range(1, nl)]
    alpha = _pack([w_alpha]) + began
    wa_g = _share(alpha, "gather_alpha")
    wa_cols = w_alpha.shape[2]
    wa_full = wa_g.reshape(N_DEV, -1)[:, :nl * GLA_RANK * wa_cols].reshape(N_DEV, nl, GLA_RANK, wa_cols)
    wa_full = jnp.transpose(wa_full, (1, 2, 0, 3)).reshape(nl, GLA_RANK, GLA_KW)
    wa_pad = jnp.pad(wa_full, ((0, 0), (0, LANE - GLA_RANK), (0, 0)))
    rb_pad = jnp.pad(rel_bias, ((0, 0), (0, 0), (0, 3 * LANE - N_REL)))

    def layer_args(l, follows=None):
        gp = g_pre[l:l + 1] if follows is None else g_pre[l:l + 1] + follows[:1, :1]
        return (wts[l], wos[l], gp, g_post[l:l + 1], wa_pad[l], b_alpha[l:l + 1], g_gla[l:l + 1],
                g_att[l:l + 1], rb_pad[l])

    my = _slot(_position())

    def fetch(l, after):
        return _exchange_start(shards[l], after, GATHER_PEERS, False, f"gather_start_{l}")

    def relay(l, first_hop, after):
        own[l], land = _exchange_wait(first_hop[:4], after, GATHER_PEERS, False, f"gather_wait_{l}")
        return _relay_start(land, f"relay_start_{l}")

    def midway(l, y):
        flight["relay"] = relay(l + 1, flight["fetch"], y)
        if l + 2 >= nl:
            return flight["relay"][3]
        flight["fetch"] = fetch(l + 2, flight["relay"][2])
        return flight["fetch"][4]

    act, saved, wts, wos, flight, own = x0, [], [], [], {}, [None] * nl
    prepared = (wa_pad[0, :1, :1] + sum(sh[:1, :1].astype(F32) for sh in shards[1:]))
    flight["relay"] = relay(0, first_fetch, prepared)
    if nl > 1:
        flight["fetch"] = fetch(1, flight["relay"][2])
    for l in range(nl):
        land = _relay_wait(flight["relay"][:3], act, f"relay_wait_{l}")
        land = lax.dynamic_update_slice_in_dim(land, own[l][None], my, 0)
        wt_l, wo_l = _aligned_weight(land, rows, cols)
        wts.append(wt_l)
        wos.append(wo_l)
        act, sv = _layer_fwd(act, *layer_args(l, follows=first_fetch[4] if l == 0 else None),
                             midway=functools.partial(midway, l) if l + 1 < nl else None)
        saved.append(sv)
    dout, sq = _loss_head(act, tgt)
    loss = lax.psum(sq[0, 0] * (0.5 / d), ("x", "y", "c"))

    smalls, pending_out, pending_in = [None] * nl, [None] * nl, [None] * nl

    def send_out(l, dwo):
        pending_out[l] = _exchange_start(dwo.reshape(N_DEV, rows, d), dwo[:1, :1], ALL_PEERS, True,
                                         f"scatter_out_start_{l}")
        return pending_out[l][4]

    def send_in(l, dwt):
        if l > 0:
            pending_in[l] = _exchange_start(_partial_slabs(dwt, cols), dwt[-1], ALL_PEERS, True,
                                            f"scatter_in_start_{l}")
            return pending_in[l][4]
        pair = _exchange_start(_partial_slabs(dwt, cols, by_core=True), dwt[-1], (1,), True,
                               "pair_start_0", slots=BY_CORE)
        by_core, from_sibling = _exchange_wait(pair[:4], pair[4], (1,), True, "pair_wait_0", slots=BY_CORE)
        pending_in[l] = _exchange_start(_pair_sum(by_core, from_sibling), dwt[-1], GATHER_PEERS[1:], True,
                                        "scatter_in_start_0", slots=BY_CHIP)
        return pending_in[l][4]

    for l in reversed(range(nl)):
        dout, smalls[l] = _layer_bwd(dout, saved[l], *layer_args(l), on_dwo=functools.partial(send_out, l),
                                     on_dwt=functools.partial(send_in, l))
    grad_x = dout.reshape(x.shape)

    def landed(started, after, name, ks=ALL_PEERS, slots=BY_DEVICE):
        partial, land = _exchange_wait(started[:4], after, ks, True, name, slots=slots)
        mine = slots[0](_position())
        return lax.dynamic_update_slice_in_dim(land, lax.dynamic_slice_in_dim(partial, mine, 1, 0), mine, 0)

    parts_out = [landed(pending_out[l], dout, f"scatter_out_wait_{l}") for l in range(nl)]
    g_w_out, d_w_out, m2_w_out, v2_w_out = _adam_sharded(parts_out, 0, w_out, m_w_out, v_w_out, "adam_w_out")
    names = 7
    small_stacked = [jnp.stack([smalls[l][i] for l in range(nl)]) for i in range(names)]
    shapes = [a.shape for a in small_stacked]
    gathered = _share(_pack(small_stacked), "gather_small_grads", after=d_w_out)
    g_pre_g, g_post_g, wa_g_full, b_g, gla_g, att_g, rb_g = _unpack(_sum_slots(gathered), shapes)
    wa_g_mine = lax.dynamic_slice_in_dim(wa_g_full, my * wa_cols, wa_cols, axis=2)
    grads = [g_pre_g, g_post_g, wa_g_mine, b_g, gla_g, att_g, rb_g]
    ws = [g_pre, g_post, w_alpha, b_alpha, g_gla, g_att, rel_bias]
    ms = [m_g_pre, m_g_post, m_w_alpha, m_b_alpha, m_g_gla, m_g_att, m_rel_bias]
    vs = [v_g_pre, v_g_post, v_w_alpha, v_b_alpha, v_g_gla, v_g_att, v_rel_bias]
    d_s, m2_s, v2_s = _adam_small(ws, grads, ms, vs)

    parts_in = [landed(pending_in[0], d_s[0], "scatter_in_wait_0", GATHER_PEERS[1:], BY_CHIP)]
    parts_in += [landed(pending_in[l], d_s[0], f"scatter_in_wait_{l}") for l in range(1, nl)]
    g_w_in, d_w_in, m2_w_in, v2_w_in = [
        jnp.transpose(a, (1, 2, 0))
        for a in _adam_columns(parts_in, 0, w_c, cols_first(m_w_in), cols_first(v_w_in))]

    def ordered(big_in, big_out, small):
        return [big_in, big_out] + list(small)

    return (loss, grad_x,
            *ordered(g_w_in, g_w_out, grads),
            *ordered(d_w_in, d_w_out, d_s),
            *ordered(m2_w_in, m2_w_out, m2_s),
            *ordered(v2_w_in, v2_w_out, v2_s))
```

```python
import functools

import jax
import jax.numpy as jnp
from jax import lax
from jax.experimental import pallas as pl
from jax.experimental.pallas import tpu as pltpu

F32 = jnp.float32
BF16 = jnp.bfloat16
MESH = pl.DeviceIdType.MESH
ANY = pl.BlockSpec(memory_space=pl.ANY)

CHUNK = 64
GLA_HEADS = 4
GLA_DK = 128
GLA_DV = 256
GLA_KW = GLA_HEADS * GLA_DK
D_GLA = GLA_HEADS * GLA_DV
GLA_RANK = 16
GLA_TAU = 16.0
ATT_HEADS = 8
ATT_HD = 128
D_ATT = ATT_HEADS * ATT_HD
LEFT_CHUNKS = 8
REL_CLIP = 128
N_REL = 2 * REL_CLIP + 1
EPS = 1e-6
D_IN = 2 * GLA_KW + 2 * D_GLA + GLA_RANK + 4 * D_ATT
GLA_SCALE = GLA_DK ** -0.5
ATT_SCALE = ATT_HD ** -0.5

ADAM_LR = 0.001
ADAM_B1 = 0.9
ADAM_B2 = 0.999
ADAM_EPS = 1e-08
ADAM_WD = 0.01
ADAM_STEP = 10

N_DEV = 8
LANE = 128
GA_ORIG = 2 * GLA_KW + 2 * D_GLA
OFF_AQ = GA_ORIG
OFF_GA = GA_ORIG + 4 * D_ATT
D_ZP = OFF_GA + LANE
QB = 2 * CHUNK
ATT_UNROLL = 16
WIN = (LEFT_CHUNKS + 2) * CHUNK
ET_ROWS = WIN + LEFT_CHUNKS * CHUNK
NEG = -1e30
VMEM_LIMIT = 48 * 1024 * 1024


def _cparams(sem):
    return pltpu.CompilerParams(dimension_semantics=sem, vmem_limit_bytes=VMEM_LIMIT)


def _dot(a, b):
    return jnp.dot(a, b, preferred_element_type=F32)


def _dot_nt(a, b):
    return lax.dot_general(a, b, (((1,), (1,)), ((), ())), preferred_element_type=F32)


def _dot_tn(a, b):
    return lax.dot_general(a, b, (((0,), (0,)), ((), ())), preferred_element_type=F32)


def _dot01(t, x, left=True):
    if not left:
        t, x = x, t
    hi = x.astype(BF16)
    r = x - hi.astype(F32)
    mid = r.astype(BF16)
    lo = (r - mid.astype(F32)).astype(BF16)
    if left:
        return _dot(t, hi) + _dot(t, mid) + _dot(t, lo)
    return _dot(hi, t) + _dot(mid, t) + _dot(lo, t)


def _sigmoid(x):
    return 1.0 / (1.0 + jnp.exp(-x))


def _log_sigmoid(x):
    return jnp.minimum(x, 0.0) - jnp.log(1.0 + jnp.exp(-jnp.abs(x)))


TILES = {
    "in_proj": (512, D_ZP // 3, None),
    "in_proj_dx": (512, 512, None),
    "in_proj_dw": (512, 2048, None),
    "out_proj": (512, 1024, None),
    "out_proj_dx": (512, 1024, None),
    "out_proj_dw": (1024, 1024, None),
}


def _matmul(a, b, mode, out_dtype, tm, tn, tk, name, n_outer=False, after=None):
    if mode == "nn":
        (m, k), n = a.shape, b.shape[1]
    elif mode == "nt":
        (m, k), n = a.shape, b.shape[0]
    else:
        (k, m), n = a.shape, b.shape[1]
    tm, tn, tk = min(tm, m), min(tn, n), k if tk is None else min(tk, k)
    assert m % tm == 0 and n % tn == 0 and k % tk == 0, (name, m, n, k)
    nk = k // tk
    dot = {"nn": _dot, "nt": _dot_nt, "tn": _dot_tn}[mode]

    follows = [] if after is None else [after]

    def body_whole_k(a_ref, b_ref, *rest):
        o_ref = rest[-1]
        o_ref[...] = dot(a_ref[...], b_ref[...]).astype(out_dtype)

    def body(a_ref, b_ref, *rest):
        o_ref, acc_ref = rest[-2:]
        kk = pl.program_id(2)

        @pl.when(kk == 0)
        def _():
            acc_ref[...] = jnp.zeros_like(acc_ref)

        acc_ref[...] += dot(a_ref[...], b_ref[...])

        @pl.when(kk == nk - 1)
        def _():
            o_ref[...] = acc_ref[...].astype(out_dtype)

    def at(index):
        return (lambda j, i, kk: index(i, j, kk)) if n_outer else index

    if mode == "tn":
        a_spec = pl.BlockSpec((tk, tm), at(lambda i, j, kk: (kk, i)))
    else:
        a_spec = pl.BlockSpec((tm, tk), at(lambda i, j, kk: (i, kk)))
    if mode == "nt":
        b_spec = pl.BlockSpec((tn, tk), at(lambda i, j, kk: (j, kk)))
    else:
        b_spec = pl.BlockSpec((tk, tn), at(lambda i, j, kk: (kk, j)))
    return pl.pallas_call(
        body_whole_k if nk == 1 else body, name=name,
        grid=(n // tn, m // tm, nk) if n_outer else (m // tm, n // tn, nk),
        in_specs=[a_spec, b_spec] + [ANY] * len(follows),
        out_specs=pl.BlockSpec((tm, tn), at(lambda i, j, kk: (i, j))),
        out_shape=jax.ShapeDtypeStruct((m, n), out_dtype),
        scratch_shapes=[] if nk == 1 else [pltpu.VMEM((tm, tn), F32)],
        compiler_params=_cparams(("parallel", "parallel", "arbitrary")),
    )(a, b, *follows)


def _matmul_cols(pieces, b, out_dtype, tm, tn, name, after=None):
    m, n = pieces[0].shape[0], b.shape[1]
    widths = [p.shape[1] for p in pieces]
    starts = [sum(widths[:i]) for i in range(len(pieces))]
    follows = [] if after is None else [after]
    tm, tn = min(tm, m), min(tn, n)
    assert sum(widths) == b.shape[0] and m % tm == 0 and n % tn == 0, name

    def body(*refs):
        b_ref, o_ref = refs[len(pieces)], refs[-1]
        acc = None
        for p_ref, at, width in zip(refs, starts, widths):
            part = _dot(p_ref[...], b_ref[at:at + width, :])
            acc = part if acc is None else acc + part
        o_ref[...] = acc.astype(out_dtype)

    return pl.pallas_call(
        body, name=name, grid=(n // tn, m // tm),
        in_specs=[pl.BlockSpec((tm, width), lambda j, i: (i, 0)) for width in widths]
        + [pl.BlockSpec((b.shape[0], tn), lambda j, i: (0, j))] + [ANY] * len(follows),
        out_specs=pl.BlockSpec((tm, tn), lambda j, i: (i, j)),
        out_shape=jax.ShapeDtypeStruct((m, n), out_dtype),
        compiler_params=_cparams(("parallel", "parallel")),
    )(*pieces, b, *follows)


def _matmul_rows(pieces, b, out_dtype, tw, tn, name):
    k, n = b.shape
    tn = min(tn, n)
    counts = [p.shape[1] // tw for p in pieces]
    firsts = [sum(counts[:i]) for i in range(len(pieces))]
    assert all(p.shape[1] % tw == 0 for p in pieces) and n % tn == 0, name

    def body(*refs):
        b_ref, o_ref = refs[len(pieces):]
        for p_ref, first, count in zip(refs, firsts, counts):
            @pl.when((pl.program_id(0) >= first) & (pl.program_id(0) < first + count))
            def _(p_ref=p_ref):
                o_ref[...] = _dot_tn(p_ref[...], b_ref[...]).astype(out_dtype)

    def piece_spec(first, count):
        return pl.BlockSpec((k, tw), lambda i, j: (0, jnp.clip(i - first, 0, count - 1)))

    return pl.pallas_call(
        body, name=name, grid=(sum(counts), n // tn),
        in_specs=[piece_spec(first, count) for first, count in zip(firsts, counts)]
        + [pl.BlockSpec((k, tn), lambda i, j: (0, j))],
        out_specs=pl.BlockSpec((tw, tn), lambda i, j: (i, j)),
        out_shape=jax.ShapeDtypeStruct((sum(counts) * tw, n), out_dtype),
        compiler_params=_cparams(("parallel", "parallel")),
    )(*pieces, b)


ROWS = 512


def _rms_fwd(x, g):
    s, d = x.shape

    def body(x_ref, g_ref, h_ref):
        xv = x_ref[...]
        r = lax.rsqrt(jnp.mean(xv * xv, axis=-1, keepdims=True) + EPS)
        h_ref[...] = (xv * r * g_ref[...]).astype(BF16)

    return pl.pallas_call(
        body, name="rms_fwd", grid=(s // ROWS,),
        in_specs=[pl.BlockSpec((ROWS, d), lambda i: (i, 0)), pl.BlockSpec((1, d), lambda i: (0, 0))],
        out_specs=pl.BlockSpec((ROWS, d), lambda i: (i, 0)),
        out_shape=jax.ShapeDtypeStruct((s, d), BF16),
        compiler_params=_cparams(("parallel",)),
    )(x, g)


def _post_fwd(x, y, g, g_next=None):
    s, d = x.shape

    def body(x_ref, y_ref, g_ref, *rest):
        yv = y_ref[...]
        r = lax.rsqrt(jnp.mean(yv * yv, axis=-1, keepdims=True) + EPS)
        out = x_ref[...] + yv * r * g_ref[...]
        if g_next is None:
            rest[0][...] = out
            return
        gn_ref, o_ref, h_ref = rest
        o_ref[...] = out
        rn = lax.rsqrt(jnp.mean(out * out, axis=-1, keepdims=True) + EPS)
        h_ref[...] = (out * rn * gn_ref[...]).astype(BF16)

    row = pl.BlockSpec((ROWS, d), lambda i: (i, 0))
    vec = pl.BlockSpec((1, d), lambda i: (0, 0))
    both = g_next is not None
    res = pl.pallas_call(
        body, name="post_fwd", grid=(s // ROWS,),
        in_specs=[row, row, vec] + [vec] * both,
        out_specs=[row] + [row] * both,
        out_shape=[jax.ShapeDtypeStruct((s, d), F32)] + [jax.ShapeDtypeStruct((s, d), BF16)] * both,
        compiler_params=_cparams(("parallel",)),
    )(x, y, g, *([g_next] * both))
    return (res[0], res[1]) if both else (res[0], None)


def _loss_head(out, tgt):
    s, d = out.shape

    def body(o_ref, t_ref, dout_ref, sum_ref):
        @pl.when(pl.program_id(0) == 0)
        def _():
            sum_ref[...] = jnp.zeros_like(sum_ref)

        e = o_ref[...] - t_ref[...]
        dout_ref[...] = e * (1.0 / d)
        sum_ref[...] += jnp.sum(jnp.sum(e * e, axis=1, keepdims=True), axis=0, keepdims=True)

    row = pl.BlockSpec((ROWS, d), lambda i: (i, 0))
    return pl.pallas_call(
        body, name="loss_head", grid=(s // ROWS,),
        in_specs=[row, row],
        out_specs=[row, pl.BlockSpec((1, 1), lambda i: (0, 0))],
        out_shape=[jax.ShapeDtypeStruct((s, d), F32), jax.ShapeDtypeStruct((1, 1), F32)],
        compiler_params=_cparams(("arbitrary",)),
    )(out, tgt)


def _post_bwd(dout, y, g):
    s, d = y.shape

    def body(do_ref, y_ref, g_ref, dy_ref, dg_ref):
        @pl.when(pl.program_id(0) == 0)
        def _():
            dg_ref[...] = jnp.zeros_like(dg_ref)

        yv = y_ref[...]
        dv = do_ref[...]
        r = lax.rsqrt(jnp.mean(yv * yv, axis=-1, keepdims=True) + EPS)
        dg_ref[...] += jnp.sum(dv * yv * r, axis=0, keepdims=True)
        w = dv * g_ref[...]
        dy = r * (w - yv * (r * r) * jnp.mean(w * yv, axis=-1, keepdims=True))
        dy_ref[...] = dy.astype(BF16)

    row = pl.BlockSpec((ROWS, d), lambda i: (i, 0))
    vec = pl.BlockSpec((1, d), lambda i: (0, 0))
    return pl.pallas_call(
        body, name="post_bwd", grid=(s // ROWS,),
        in_specs=[row, row, vec],
        out_specs=[row, vec],
        out_shape=[jax.ShapeDtypeStruct((s, d), BF16), jax.ShapeDtypeStruct((1, d), F32)],
        compiler_params=_cparams(("arbitrary",)),
    )(dout, y, g)


def _pre_bwd(dh, x, g, dout):
    s, d = x.shape

    def body(dh_ref, x_ref, g_ref, do_ref, dx_ref, dg_ref):
        @pl.when(pl.program_id(0) == 0)
        def _():
            dg_ref[...] = jnp.zeros_like(dg_ref)

        xv = x_ref[...]
        dv = dh_ref[...]
        r = lax.rsqrt(jnp.mean(xv * xv, axis=-1, keepdims=True) + EPS)
        dg_ref[...] += jnp.sum(dv * xv * r, axis=0, keepdims=True)
        w = dv * g_ref[...]
        dx_ref[...] = do_ref[...] + r * (w - xv * (r * r) * jnp.mean(w * xv, axis=-1, keepdims=True))

    row = pl.BlockSpec((ROWS, d), lambda i: (i, 0))
    vec = pl.BlockSpec((1, d), lambda i: (0, 0))
    return pl.pallas_call(
        body, name="pre_bwd", grid=(s // ROWS,),
        in_specs=[row, row, vec, row],
        out_specs=[row, vec],
        out_shape=[jax.ShapeDtypeStruct((s, d), F32), jax.ShapeDtypeStruct((1, d), F32)],
        compiler_params=_cparams(("arbitrary",)),
    )(dh, x, g, dout)


GLA_STEP = 4
GLA_ROWS = GLA_STEP * CHUNK
GLA_CHUNKS = [slice(c * CHUNK, (c + 1) * CHUNK) for c in range(GLA_STEP)]


def _chunk_triangles():
    ri = lax.broadcasted_iota(jnp.int32, (GLA_ROWS, GLA_ROWS), 0)
    ci = lax.broadcasted_iota(jnp.int32, (GLA_ROWS, GLA_ROWS), 1)
    same = (ri // CHUNK) == (ci // CHUNK)
    return (jnp.where(same & (ri >= ci), 1.0, 0.0).astype(BF16), jnp.where(same & (ci >= ri), 1.0, 0.0).astype(BF16))


def _per_chunk(fn, like):
    row = lax.broadcasted_iota(jnp.int32, like.shape, 0)
    return [fn((row >= c * CHUNK) & (row < (c + 1) * CHUNK)) for c in range(GLA_STEP)]


def _spread(per_chunk, like):
    row = lax.broadcasted_iota(jnp.int32, like.shape, 0)
    out = per_chunk[-1]
    for c in reversed(range(GLA_STEP - 1)):
        out = jnp.where(row < (c + 1) * CHUNK, per_chunk[c], out)
    return out


def _gla_gate(ga_b, wa_b, b_ref, tri):
    pre = _dot(ga_b, wa_b) + b_ref[...]
    la = _log_sigmoid(pre) * (1.0 / GLA_TAU)
    return pre, _dot01(tri, la)


def _chunk_ends(cum):
    row = lax.broadcasted_iota(jnp.int32, cum.shape, 0)
    return [jnp.sum(jnp.where(row == (c + 1) * CHUNK - 1, cum, 0.0), axis=0, keepdims=True)
            for c in range(GLA_STEP)]


def _heads(width):
    return [slice(h * width, (h + 1) * width) for h in range(GLA_HEADS)]


def _z_specs_gla(rev=None):
    idx = (lambda n: n) if rev is None else rev
    return [
        pl.BlockSpec((GLA_ROWS, GLA_KW), lambda n: (idx(n), 0)),
        pl.BlockSpec((GLA_ROWS, GLA_KW), lambda n: (idx(n), 1)),
        pl.BlockSpec((GLA_ROWS, D_GLA), lambda n: (idx(n), 1)),
        pl.BlockSpec((GLA_ROWS, D_GLA), lambda n: (idx(n), 2)),
        pl.BlockSpec((GLA_ROWS, LANE), lambda n: (idx(n), OFF_GA // LANE)),
    ]


def _gla_fwd(z, wa_pad, b_alpha, g_gla):
    s = z.shape[0]
    nchunk = s // CHUNK

    def body(q_ref, k_ref, v_ref, gg_ref, ga_ref, wa_ref, b_ref, g_ref, y_ref, o_ref, st_ref, pre_ref, cum_ref,
             state):
        @pl.when(pl.program_id(0) == 0)
        def _():
            state[...] = jnp.zeros_like(state)

        ga_b = ga_ref[...].astype(BF16)
        tri, _ = _chunk_triangles()
        nh = range(GLA_HEADS)
        keys, vals = _heads(GLA_DK), _heads(GLA_DV)
        pre, cum = _gla_gate(ga_b, wa_ref[...].astype(BF16), b_ref, tri)
        pre_ref[...] = pre
        cum_ref[...] = cum
        cends = _chunk_ends(cum)
        kd_b = (k_ref[...] * jnp.exp(_spread(cends, cum) - cum)).astype(BF16)
        qs = (q_ref[...] * GLA_SCALE).astype(BF16)
        v_b = v_ref[...].astype(BF16)
        uts = [[_dot_tn(v_b[rs, vals[h]], kd_b[rs, keys[h]]) for h in nh] for rs in GLA_CHUNKS]
        sts, prev = [], [state[h] for h in nh]
        for c in range(GLA_STEP):
            a = jnp.exp(cends[c])
            prev = [prev[h] * a[:, keys[h]] + uts[c][h] for h in nh]
            sts.append(prev)
        for h in nh:
            state[h] = prev[h]
            for c in range(GLA_STEP):
                st_ref[c, h] = sts[c][h]
        outs = [[_dot_nt(qs[rs, keys[h]], sts[c][h].astype(BF16)) for h in nh] for c, rs in enumerate(GLA_CHUNKS)]
        for h in nh:
            o, vs = jnp.concatenate([outs[c][h] for c in range(GLA_STEP)], axis=0), vals[h]
            o_ref[:, vs] = o
            r = lax.rsqrt(jnp.mean(o * o, axis=-1, keepdims=True) + EPS)
            gg = gg_ref[:, vs]
            y_ref[:, vs] = (o * r * g_ref[:, vs] * (gg * _sigmoid(gg))).astype(BF16)

    full = lambda shape: pl.BlockSpec(shape, lambda n: tuple(0 for _ in shape))
    wide = pl.BlockSpec((GLA_ROWS, D_GLA), lambda n: (n, 0))
    return pl.pallas_call(
        body, name="gla_fwd", grid=(nchunk // GLA_STEP,),
        in_specs=_z_specs_gla() + [full((LANE, GLA_KW)), full((1, GLA_KW)), full((1, D_GLA))],
        out_specs=[wide, wide, pl.BlockSpec((GLA_STEP, GLA_HEADS, GLA_DV, GLA_DK), lambda n: (n, 0, 0, 0)),
                   pl.BlockSpec((GLA_ROWS, GLA_KW), lambda n: (n, 0)), pl.BlockSpec((GLA_ROWS, GLA_KW), lambda n: (n, 0))],
        out_shape=[jax.ShapeDtypeStruct((s, D_GLA), BF16), jax.ShapeDtypeStruct((s, D_GLA), F32),
                   jax.ShapeDtypeStruct((nchunk, GLA_HEADS, GLA_DV, GLA_DK), F32),
                   jax.ShapeDtypeStruct((s, GLA_KW), F32), jax.ShapeDtypeStruct((s, GLA_KW), F32)],
        scratch_shapes=[pltpu.VMEM((GLA_HEADS, GLA_DV, GLA_DK), F32)],
        compiler_params=_cparams(("arbitrary",)),
    )(z, z, z, z, z, wa_pad, b_alpha, g_gla)


def _gla_bwd(dyc, o_gla, z, wa_pad, g_gla, states, gate_pre, gate_cum):
    s = z.shape[0]
    nsteps = s // GLA_ROWS
    rev = lambda n: nsteps - 1 - n

    def body(dy_ref, o_ref, q_ref, k_ref, v_ref, gg_ref, ga_ref, wa_ref, g_ref, st_ref, stp_ref, pre_ref, cum_ref,
             dq_ref, dk_ref, dv_ref, dgg_ref, dga_ref, dwa_ref, db_ref, dg_ref, carry):
        step = pl.program_id(0)

        @pl.when(step == 0)
        def _():
            carry[...] = jnp.zeros_like(carry)
            dwa_ref[...] = jnp.zeros_like(dwa_ref)
            db_ref[...] = jnp.zeros_like(db_ref)
            dg_ref[...] = jnp.zeros_like(dg_ref)

        has_prev = (step < nsteps - 1).astype(F32)
        ga_b = ga_ref[...].astype(BF16)
        _, tri_up = _chunk_triangles()
        nh, nc = range(GLA_HEADS), range(GLA_STEP)
        keys, vals = _heads(GLA_DK), _heads(GLA_DV)
        wa_b = wa_ref[...].astype(BF16)
        pre, cum = pre_ref[...], cum_ref[...]
        cends = _chunk_ends(cum)
        e = jnp.exp(_spread(cends, cum) - cum)
        a = [jnp.exp(cends[c]) for c in nc]
        kf = k_ref[...]
        kd_b = (kf * e).astype(BF16)
        v_b = v_ref[...].astype(BF16)
        qs = (q_ref[...] * GLA_SCALE).astype(BF16)
        do_b = []
        for h in nh:
            vs = vals[h]
            o = o_ref[:, vs]
            gg = gg_ref[:, vs]
            g = g_ref[:, vs]
            dy = dy_ref[:, vs]
            r = lax.rsqrt(jnp.mean(o * o, axis=-1, keepdims=True) + EPS)
            sg = _sigmoid(gg)
            dogn = dy * (gg * sg)
            dgg_ref[:, vs] = (dy * (o * r * g) * (sg * (1.0 + gg * (1.0 - sg)))).astype(BF16)
            dg_ref[:, vs] += jnp.sum(dogn * o * r, axis=0, keepdims=True)
            w = dogn * g
            do_b.append((r * (w - o * (r * r) * jnp.mean(w * o, axis=-1, keepdims=True))).astype(BF16))
        dqs = [jnp.concatenate([_dot(do_b[h][rs], st_ref[c, h].astype(BF16)) for c, rs in enumerate(GLA_CHUNKS)],
                               axis=0) for h in nh]
        dq_ref[...] = (jnp.concatenate(dqs, axis=1) * GLA_SCALE).astype(BF16)
        own = [[_dot_tn(do_b[h][rs], qs[rs, keys[h]]) for h in nh] for rs in GLA_CHUNKS]
        gts, later = [None] * GLA_STEP, [carry[h] for h in nh]
        for c in reversed(nc):
            gts[c] = [own[c][h] + later[h] for h in nh]
            later = [gts[c][h] * a[c][:, keys[h]] for h in nh]
        for h in nh:
            carry[h] = later[h]
        gt_b = [[gts[c][h].astype(BF16) for h in nh] for c in nc]
        dkd = jnp.concatenate([jnp.concatenate([_dot(v_b[rs, vals[h]], gt_b[c][h]) for h in nh], axis=1)
                               for c, rs in enumerate(GLA_CHUNKS)], axis=0)
        dvs = [[_dot_nt(kd_b[rs, keys[h]], gt_b[c][h]) for h in nh] for c, rs in enumerate(GLA_CHUNKS)]
        before = lambda c, h: st_ref[c - 1, h] if c > 0 else stp_ref[0, h] * has_prev
        da = [jnp.concatenate([jnp.sum(gts[c][h] * before(c, h), axis=0, keepdims=True) for h in nh], axis=1)
              for c in nc]
        for h in nh:
            dv_ref[:, vals[h]] = jnp.concatenate([dvs[c][h] for c in nc], axis=0).astype(BF16)
        dk_ref[...] = (dkd * e).astype(BF16)
        dd = dkd * kf * e
        dsum = _per_chunk(lambda mine: jnp.sum(jnp.where(mine, dd, 0.0), axis=0, keepdims=True), dd)
        dcend = _spread([dsum[c] + da[c] * a[c] for c in nc], dd)
        dla = dcend - _dot01(tri_up, dd)
        dpre = dla * (1.0 / GLA_TAU) * (1.0 - _sigmoid(pre))
        dpre_b = dpre.astype(BF16)
        dga_ref[...] = _dot_nt(dpre_b, wa_b).astype(BF16)
        dwa_ref[...] += _dot_tn(ga_b, dpre_b)
        db_ref[...] += jnp.sum(dpre, axis=0, keepdims=True)

    full = lambda shape: pl.BlockSpec(shape, lambda n: tuple(0 for _ in shape))
    wide = pl.BlockSpec((GLA_ROWS, D_GLA), lambda n: (rev(n), 0))
    keyw = pl.BlockSpec((GLA_ROWS, GLA_KW), lambda n: (rev(n), 0))
    st_spec = pl.BlockSpec((GLA_STEP, GLA_HEADS, GLA_DV, GLA_DK), lambda n: (rev(n), 0, 0, 0))
    stp_spec = pl.BlockSpec((1, GLA_HEADS, GLA_DV, GLA_DK),
                            lambda n: (jnp.maximum(GLA_STEP * rev(n) - 1, 0), 0, 0, 0))
    return pl.pallas_call(
        body, name="gla_bwd", grid=(nsteps,),
        in_specs=[wide, wide] + _z_specs_gla(rev)
        + [full((LANE, GLA_KW)), full((1, D_GLA)), st_spec, stp_spec, keyw, keyw],
        out_specs=[keyw, keyw, wide, wide, pl.BlockSpec((GLA_ROWS, LANE), lambda n: (rev(n), 0)),
                   full((LANE, GLA_KW)), full((1, GLA_KW)), full((1, D_GLA))],
        out_shape=[jax.ShapeDtypeStruct((s, GLA_KW), BF16), jax.ShapeDtypeStruct((s, GLA_KW), BF16),
                   jax.ShapeDtypeStruct((s, D_GLA), BF16), jax.ShapeDtypeStruct((s, D_GLA), BF16),
                   jax.ShapeDtypeStruct((s, LANE), BF16),
                   jax.ShapeDtypeStruct((LANE, GLA_KW), F32), jax.ShapeDtypeStruct((1, GLA_KW), F32),
                   jax.ShapeDtypeStruct((1, D_GLA), F32)],
        scratch_shapes=[pltpu.VMEM((GLA_HEADS, GLA_DV, GLA_DK), F32)],
        compiler_params=_cparams(("arbitrary",)),
    )(dyc, o_gla, z, z, z, z, z, wa_pad, g_gla, states, states, gate_pre, gate_cum)


def _build_bias_table(rb_row, et_ref):
    far = jnp.broadcast_to(rb_row[:, 2 * REL_CLIP:2 * REL_CLIP + 1], (1, LANE))
    near_hi = rb_row[:, REL_CLIP:2 * REL_CLIP]
    near_lo = rb_row[:, 0:REL_CLIP]
    past = jnp.broadcast_to(rb_row[:, 0:1], (1, LANE))
    seg = [far, far, far, far, near_hi, near_lo] + [past] * (ET_ROWS // LANE - 5)
    ri = lax.broadcasted_iota(jnp.int32, (LANE, LANE), 0)
    ci = lax.broadcasted_iota(jnp.int32, (LANE, LANE), 1)
    for kb in range(ET_ROWS // LANE):
        wmat = jnp.where(ri + ci < LANE, seg[kb], seg[kb + 1])
        blk = pltpu.roll(wmat, 0, 1, stride=1, stride_axis=0)
        lag = LEFT_CHUNKS + ci // CHUNK - (2 * kb + ri // CHUNK)
        et_ref[kb * LANE:(kb + 1) * LANE, :] = jnp.where((lag >= 0) & (lag <= LEFT_CHUNKS), blk, NEG)


def _reduce_bias_table(det_ref):
    lane = lax.broadcasted_iota(jnp.int32, (1, LANE), 1)
    ri = lax.broadcasted_iota(jnp.int32, (LANE, LANE), 0)
    ci = lax.broadcasted_iota(jnp.int32, (LANE, LANE), 1)
    flip = jnp.where(ri + ci == LANE - 1, 1.0, 0.0).astype(BF16)
    segs = jnp.zeros((8, LANE), F32)
    seg_row = lax.broadcasted_iota(jnp.int32, (8, LANE), 0)
    prev_minus = jnp.zeros((1, LANE), F32)
    for kb in range(6):
        rolled = pltpu.roll(_dot01(det_ref[kb * LANE:(kb + 1) * LANE, :], flip, left=False), 0, 1,
                            stride=1, stride_axis=0)
        plus = jnp.sum(jnp.where(ci >= ri, rolled, 0.0), axis=0, keepdims=True)
        minus = jnp.sum(jnp.where(ci < ri, rolled, 0.0), axis=0, keepdims=True)
        segs = segs + jnp.where(seg_row == kb, plus + prev_minus, 0.0)
        prev_minus = minus
    segs = _dot01(segs, flip, left=False)
    pick = lambda kb: jnp.sum(jnp.where(seg_row == kb, segs, 0.0), axis=0, keepdims=True)
    far = jnp.sum(pick(0) + pick(1) + pick(2) + pick(3), axis=1, keepdims=True)
    last = jnp.where(lane == 0, far, 0.0)
    return jnp.concatenate([pick(5), pick(4), last], axis=1)


def _att_window(b):
    c0 = 2 * b
    kstart = pl.multiple_of(jnp.maximum(c0 - LEFT_CHUNKS, 0) * CHUNK, CHUNK)
    eoff = pl.multiple_of(jnp.maximum(LEFT_CHUNKS - c0, 0) * CHUNK, CHUNK)
    return kstart, eoff


def _att_probs(q_b, kw_b, et):
    st = _dot_nt(kw_b, q_b) * ATT_SCALE + et
    m = jnp.max(st, axis=0, keepdims=True)
    ex = jnp.exp(st - m)
    return ex * (1.0 / jnp.sum(ex, axis=0, keepdims=True))


def _att_fwd(z, rb_pad, g_att):
    s = z.shape[0]
    nblk = s // QB
    c_aq, c_ak, c_av, c_ag = [(OFF_AQ + i * D_ATT) // ATT_HD for i in range(4)]

    def body(q_ref, k_ref, v_ref, ag_ref, rb_ref, g_ref, y_ref, o_ref, p_ref, et_ref, kb_ref, vb_ref):
        h = pl.program_id(0)
        b = pl.program_id(1)

        @pl.when(b == 0)
        def _():
            _build_bias_table(rb_ref[pl.ds(h, 1), :], et_ref)
            kb_ref[...] = k_ref[...].astype(BF16)
            vb_ref[...] = v_ref[...].astype(BF16)

        for j in range(ATT_UNROLL):
            rs = slice(j * QB, (j + 1) * QB)
            kstart, eoff = _att_window(b * ATT_UNROLL + j)
            q_b = q_ref[rs, :].astype(BF16)
            kw_b = kb_ref[pl.ds(kstart, WIN), :]
            vw_b = vb_ref[pl.ds(kstart, WIN), :]
            pt = _att_probs(q_b, kw_b, et_ref[pl.ds(eoff, WIN), :])
            p_ref[0, j] = pt
            o = _dot_tn(pt.astype(BF16), vw_b)
            o_ref[rs, :] = o
            r = lax.rsqrt(jnp.mean(o * o, axis=-1, keepdims=True) + EPS)
            ag = ag_ref[rs, :]
            y_ref[rs, :] = (o * r * g_ref[...] * (ag * _sigmoid(ag))).astype(BF16)

    blk = lambda col: pl.BlockSpec((ATT_UNROLL * QB, ATT_HD), lambda h, b: (b, col + h))
    seq = lambda col: pl.BlockSpec((s, ATT_HD), lambda h, b: (0, col + h))
    out_blk = pl.BlockSpec((ATT_UNROLL * QB, ATT_HD), lambda h, b: (b, h))
    return pl.pallas_call(
        body, name="att_fwd", grid=(ATT_HEADS, nblk // ATT_UNROLL),
        in_specs=[blk(c_aq), seq(c_ak), seq(c_av), blk(c_ag),
                  pl.BlockSpec((ATT_HEADS, 3 * LANE), lambda h, b: (0, 0)),
                  pl.BlockSpec((1, ATT_HD), lambda h, b: (0, h))],
        out_specs=[out_blk, out_blk, pl.BlockSpec((1, ATT_UNROLL, WIN, QB), lambda h, b: (h, b, 0, 0))],
        out_shape=[jax.ShapeDtypeStruct((s, D_ATT), BF16), jax.ShapeDtypeStruct((s, D_ATT), F32),
                   jax.ShapeDtypeStruct((ATT_HEADS, nblk, WIN, QB), F32)],
        scratch_shapes=[pltpu.VMEM((ET_ROWS, LANE), F32), pltpu.VMEM((s, ATT_HD), BF16),
                        pltpu.VMEM((s, ATT_HD), BF16)],
        compiler_params=_cparams(("arbitrary", "arbitrary")),
    )(z, z, z, z, rb_pad, g_att)


def _att_bwd(dyc, o_att, probs, z, g_att):
    s = z.shape[0]
    nblk = s // QB
    c_aq, c_ak, c_av, c_ag = [(OFF_AQ + i * D_ATT) // ATT_HD for i in range(4)]
    c_dy = D_GLA // ATT_HD

    def body(dy_ref, o_ref, p_ref, q_ref, k_ref, v_ref, ag_ref, g_ref,
             dq_ref, dk_ref, dv_ref, dag_ref, drb_ref, dg_ref, det_ref, kb_ref, vb_ref, dk_acc, dv_acc):
        b = pl.program_id(1)

        @pl.when(b == 0)
        def _():
            kb_ref[...] = k_ref[...].astype(BF16)
            vb_ref[...] = v_ref[...].astype(BF16)
            det_ref[...] = jnp.zeros_like(det_ref)
            dk_acc[...] = jnp.zeros_like(dk_acc)
            dv_acc[...] = jnp.zeros_like(dv_acc)
            dg_ref[...] = jnp.zeros_like(dg_ref)

        g = g_ref[...]
        dg = jnp.zeros((1, ATT_HD), F32)
        for j in range(ATT_UNROLL):
            rs = slice(j * QB, (j + 1) * QB)
            kstart, eoff = _att_window(b * ATT_UNROLL + j)
            q_b = q_ref[rs, :].astype(BF16)
            kw_b = kb_ref[pl.ds(kstart, WIN), :]
            vw_b = vb_ref[pl.ds(kstart, WIN), :]
            pt = p_ref[0, j]
            o = o_ref[rs, :]
            ag = ag_ref[rs, :]
            dy = dy_ref[rs, :]
            r = lax.rsqrt(jnp.mean(o * o, axis=-1, keepdims=True) + EPS)
            sg = _sigmoid(ag)
            don = dy * (ag * sg)
            dag_ref[rs, :] = (dy * (o * r * g) * (sg * (1.0 + ag * (1.0 - sg)))).astype(BF16)
            dg = dg + jnp.sum(don * o * r, axis=0, keepdims=True)
            w = don * g
            do_b = (r * (w - o * (r * r) * jnp.mean(w * o, axis=-1, keepdims=True))).astype(BF16)
            pt_b = pt.astype(BF16)
            dpt = _dot_nt(vw_b, do_b)
            dst = pt * (dpt - jnp.sum(dpt * pt, axis=0, keepdims=True))
            det_ref[pl.ds(eoff, WIN), :] += dst
            ds_b = (dst * ATT_SCALE).astype(BF16)
            dq_ref[rs, :] = _dot_tn(ds_b, kw_b).astype(BF16)
            dk_acc[pl.ds(kstart, WIN), :] += _dot(ds_b, q_b)
            dv_acc[pl.ds(kstart, WIN), :] += _dot(pt_b, do_b)
        dg_ref[...] += dg

        @pl.when(b == nblk // ATT_UNROLL - 1)
        def _():
            drb_ref[0] = jnp.broadcast_to(_reduce_bias_table(det_ref), (8, 3 * LANE))
            dk_ref[...] = dk_acc[...].astype(BF16)
            dv_ref[...] = dv_acc[...].astype(BF16)

    blk = lambda col: pl.BlockSpec((ATT_UNROLL * QB, ATT_HD), lambda h, b: (b, col + h))
    seq = lambda col: pl.BlockSpec((s, ATT_HD), lambda h, b: (0, col + h))
    out_blk = pl.BlockSpec((ATT_UNROLL * QB, ATT_HD), lambda h, b: (b, h))
    out_seq = pl.BlockSpec((s, ATT_HD), lambda h, b: (0, h))
    return pl.pallas_call(
        body, name="att_bwd", grid=(ATT_HEADS, nblk // ATT_UNROLL),
        in_specs=[blk(c_dy), blk(0), pl.BlockSpec((1, ATT_UNROLL, WIN, QB), lambda h, b: (h, b, 0, 0)),
                  blk(c_aq), seq(c_ak), seq(c_av), blk(c_ag),
                  pl.BlockSpec((1, ATT_HD), lambda h, b: (0, h))],
        out_specs=[out_blk, out_seq, out_seq, out_blk,
                   pl.BlockSpec((1, 8, 3 * LANE), lambda h, b: (h, 0, 0)),
                   pl.BlockSpec((1, ATT_HD), lambda h, b: (0, h))],
        out_shape=[jax.ShapeDtypeStruct((s, D_ATT), BF16), jax.ShapeDtypeStruct((s, D_ATT), BF16),
                   jax.ShapeDtypeStruct((s, D_ATT), BF16), jax.ShapeDtypeStruct((s, D_ATT), BF16),
                   jax.ShapeDtypeStruct((ATT_HEADS, 8, 3 * LANE), F32),
                   jax.ShapeDtypeStruct((1, D_ATT), F32)],
        scratch_shapes=[pltpu.VMEM((ET_ROWS, LANE), F32),
                        pltpu.VMEM((s, ATT_HD), BF16), pltpu.VMEM((s, ATT_HD), BF16),
                        pltpu.VMEM((s, ATT_HD), F32), pltpu.VMEM((s, ATT_HD), F32)],
        compiler_params=_cparams(("arbitrary", "arbitrary")),
    )(dyc, o_att, probs, z, z, z, z, g_att)


ADAM_ROWS = 64
ADAM_COL_ROWS = 32


def _adam_math(w, g, m, v):
    m2 = ADAM_B1 * m + (1.0 - ADAM_B1) * g
    v2 = ADAM_B2 * v + (1.0 - ADAM_B2) * (g * g)
    m_hat = m2 / (1.0 - ADAM_B1 ** ADAM_STEP)
    v_hat = v2 / (1.0 - ADAM_B2 ** ADAM_STEP)
    delta = -ADAM_LR * (m_hat / (jnp.sqrt(v_hat) + ADAM_EPS) + ADAM_WD * w)
    return delta, m2, v2


def _adam_sharded(parts, first, w, m, v, name):
    nl, nr, nc = w.shape

    def body(*refs):
        p_refs = refs[:nl]
        w_ref, m_ref, v_ref, g_ref, d_ref, m2_ref, v2_ref = refs[nl:]
        for k in range(nl):
            @pl.when(pl.program_id(0) == k)
            def _(p_ref=p_refs[k]):
                g = p_ref[0].astype(F32)
                for dev in range(1, N_DEV):
                    g = g + p_ref[dev].astype(F32)
                delta, m2, v2 = _adam_math(w_ref[0], g, m_ref[0], v_ref[0])
                g_ref[0] = g
                d_ref[0] = delta
                m2_ref[0] = m2
                v2_ref[0] = v2

    def part_spec(k):
        return pl.BlockSpec((N_DEV, ADAM_ROWS, nc), lambda l, i: (0, first + jnp.where(l == k, i, 0), 0))

    blk = pl.BlockSpec((1, ADAM_ROWS, nc), lambda l, i: (l, i, 0))
    shp = jax.ShapeDtypeStruct(w.shape, F32)
    return pl.pallas_call(
        body, name=name, grid=(nl, pl.cdiv(nr, ADAM_ROWS)),
        in_specs=[part_spec(k) for k in range(nl)] + [blk, blk, blk],
        out_specs=[blk, blk, blk, blk],
        out_shape=[shp, shp, shp, shp],
        compiler_params=_cparams(("arbitrary", "arbitrary")),
    )(*parts, w, m, v)


def _adam_columns(parts, first, w, m, v):
    nc, nl, d = w.shape

    def body(*refs):
        p_refs = refs[:nl]
        w_ref, m_ref, v_ref, g_ref, d_ref, m2_ref, v2_ref = refs[nl:]
        for l in range(nl):
            g = p_refs[l][0].astype(F32)
            for slot in range(1, parts[l].shape[0]):
                g = g + p_refs[l][slot].astype(F32)
            delta, m2, v2 = _adam_math(w_ref[:, l, :], g, m_ref[:, l, :], v_ref[:, l, :])
            g_ref[:, l, :] = g
            d_ref[:, l, :] = delta
            m2_ref[:, l, :] = m2
            v2_ref[:, l, :] = v2

    blk = pl.BlockSpec((ADAM_COL_ROWS, nl, d), lambda i: (i, 0, 0))
    shp = jax.ShapeDtypeStruct(w.shape, F32)
    return pl.pallas_call(
        body, name="adam_w_in", grid=(pl.cdiv(nc, ADAM_COL_ROWS),),
        in_specs=[pl.BlockSpec((p.shape[0], ADAM_COL_ROWS, d), lambda i: (0, first + i, 0)) for p in parts]
        + [blk, blk, blk],
        out_specs=[blk, blk, blk, blk],
        out_shape=[shp, shp, shp, shp],
        compiler_params=_cparams(("parallel",)),
    )(*parts, w, m, v)


def _adam_small(ws, gs, ms, vs):
    n = len(ws)

    def body(*refs):
        w_refs, g_refs, m_refs, v_refs, d_refs, m2_refs, v2_refs = [refs[i * n:(i + 1) * n] for i in range(7)]
        for i in range(n):
            delta, m2, v2 = _adam_math(w_refs[i][...], g_refs[i][...], m_refs[i][...], v_refs[i][...])
            d_refs[i][...] = delta
            m2_refs[i][...] = m2
            v2_refs[i][...] = v2

    shapes = [jax.ShapeDtypeStruct(w.shape, F32) for w in ws]
    out = pl.pallas_call(body, name="adam_small", out_shape=shapes * 3)(*ws, *gs, *ms, *vs)
    return out[:n], out[n:2 * n], out[2 * n:]


def _position():
    return lax.axis_index("x"), lax.axis_index("y"), lax.axis_index("c")


def _slot(p):
    return 4 * p[0] + 2 * p[1] + p[2]


BF16_TILE_ROWS = 16


def _slab_rows(rows, cols):
    return -(-(rows + cols) // BF16_TILE_ROWS) * BF16_TILE_ROWS


RELAYOUT_COLS = 1024
RELAYOUT_CHUNK = 64


def _shard_pieces(dev, rows, cols):
    moved = ((0, GA_ORIG, 0), (GA_ORIG, GA_ORIG + GLA_RANK, OFF_GA - GA_ORIG), (GA_ORIG + GLA_RANK, D_IN, -GLA_RANK))
    c0, c1 = dev * cols, (dev + 1) * cols
    return [(rows + max(c0, lo) - c0, max(c0, lo) + off, min(c1, hi) - max(c0, lo))
            for lo, hi, off in moved if max(c0, lo) < min(c1, hi)]


def _move_rows(src, src_row, dst, dst_row, n):
    assert src_row % 2 == 0 and dst_row % 2 == 0 and n % 2 == 0
    for r in range(0, n // 2, RELAYOUT_CHUNK):
        m = min(RELAYOUT_CHUNK, n // 2 - r)
        dst[dst_row // 2 + r:dst_row // 2 + r + m, :] = src[src_row // 2 + r:src_row // 2 + r + m, :]


def _aligned_weight(land, rows, cols):
    _, slab, d = land.shape
    ct = min(RELAYOUT_COLS, d)

    def body(land_ref, wt_ref, wo_ref):
        dev = pl.program_id(1)
        src = land_ref.bitcast(jnp.uint32)
        dst = wt_ref.bitcast(jnp.uint32)
        wo_ref[...] = land_ref[0:rows, :]

        @pl.when(dev == 0)
        def _():
            dst[D_IN // 2:D_ZP // 2, :] = jnp.zeros(((D_ZP - D_IN) // 2, ct), jnp.uint32)

        for k in range(N_DEV):
            @pl.when(dev == k)
            def _(k=k):
                for at, to, n in _shard_pieces(k, rows, cols):
                    _move_rows(src, at, dst, to, n)

    return pl.pallas_call(
        body, name="aligned_weight", grid=(d // ct, N_DEV),
        in_specs=[pl.BlockSpec((slab, ct), lambda c, dev: (dev, c))],
        out_specs=[pl.BlockSpec((D_ZP, ct), lambda c, dev: (0, c)),
                   pl.BlockSpec((rows, ct), lambda c, dev: (dev, c))],
        out_shape=[jax.ShapeDtypeStruct((D_ZP, d), land.dtype),
                   jax.ShapeDtypeStruct((N_DEV * rows, d), land.dtype)],
        compiler_params=_cparams(("parallel", "arbitrary")),
    )(land.reshape(N_DEV * slab, d))


def _partial_slabs(dwt, cols, by_core=False):
    d = dwt[0].shape[1]
    bounds = (0, GA_ORIG, OFF_GA, D_ZP)
    assert tuple(a.shape[0] for a in dwt) == tuple(hi - lo for lo, hi in zip(bounds, bounds[1:]))
    slab = _slab_rows(0, cols)
    ct = min(RELAYOUT_COLS, d)

    def body(*refs):
        out_ref = refs[-1]
        dev = pl.program_id(1)
        srcs = [ref.bitcast(jnp.uint32) for ref in refs[:-1]]
        dst = out_ref.bitcast(jnp.uint32)
        dst[cols // 2:slab // 2, :] = jnp.zeros(((slab - cols) // 2, ct), jnp.uint32)
        for k in range(N_DEV):
            @pl.when(dev == k)
            def _(k=k):
                for to, at, n in _shard_pieces(k, 0, cols):
                    which = max(i for i, lo in enumerate(bounds[:-1]) if lo <= at)
                    assert at + n <= bounds[which + 1]
                    _move_rows(srcs[which], at - bounds[which], dst, to, n)

    place = (lambda dev: (dev % 2) * (N_DEV // 2) + dev // 2) if by_core else (lambda dev: dev)
    out = pl.pallas_call(
        body, name="partial_slabs", grid=(d // ct, N_DEV),
        in_specs=[pl.BlockSpec((a.shape[0], ct), lambda c, dev: (0, c)) for a in dwt],
        out_specs=pl.BlockSpec((slab, ct), lambda c, dev: (place(dev), c)),
        out_shape=jax.ShapeDtypeStruct((N_DEV * slab, d), dwt[0].dtype),
        compiler_params=_cparams(("parallel", "arbitrary")),
    )(*dwt)
    return out.reshape((2, N_DEV // 2, slab, d) if by_core else (N_DEV, slab, d))


def _pair_sum(mine, theirs):
    _, nchip, slab, d = mine.shape
    rows = next(r for r in range(512, 0, -BF16_TILE_ROWS) if slab % r == 0)

    def body(m_ref, t_ref, o_ref):
        south = lax.axis_index("c") == 0
        own = jnp.where(south, m_ref[0, 0], m_ref[1, 0]).astype(F32)
        got = jnp.where(south, t_ref[1, 0], t_ref[0, 0]).astype(F32)
        o_ref[0] = (own + got).astype(o_ref.dtype)

    both = pl.BlockSpec((2, 1, rows, d), lambda j, i: (0, j, i, 0))
    return pl.pallas_call(
        body, name="pair_sum", grid=(nchip, slab // rows),
        in_specs=[both, both],
        out_specs=pl.BlockSpec((1, rows, d), lambda j, i: (j, i, 0)),
        out_shape=jax.ShapeDtypeStruct((nchip, slab, d), mine.dtype),
        compiler_params=_cparams(("parallel", "parallel")),
    )(mine, theirs)


def _peer(pos, k):
    x, y, c = pos
    return (1 - x if k & 4 else x, 1 - y if k & 2 else y, 1 - c if k & 1 else c)


HBM_SPEC = pl.BlockSpec(memory_space=pltpu.HBM)
SEM_SPEC = pl.BlockSpec(memory_space=pltpu.SEMAPHORE)
GATHER_PEERS = (1, 4, 2, 6)
ALL_PEERS = (1, 2, 3, 4, 5, 6, 7)


def _hbm(a):
    return pltpu.with_memory_space_constraint(a, pltpu.HBM)


BY_DEVICE = (_slot, N_DEV)
BY_CORE = (lambda p: p[2], 2)
BY_CHIP = (lambda p: 2 * p[0] + p[1], 4)


def _split_copies(src_ref, land_ref, send_sems, recv_sems, ks, per_peer, landed, slots):
    slot_of = slots[0]
    me = _position()
    out = []
    for i, k in enumerate(ks):
        peer = _peer(me, k)
        src = src_ref.at[slot_of(peer)] if per_peer else src_ref
        dst = land_ref.at[slot_of(peer) if landed else slot_of(me)]
        out.append(pltpu.make_async_remote_copy(
            src_ref=src, dst_ref=dst, send_sem=send_sems.at[i], recv_sem=recv_sems.at[i],
            device_id=peer, device_id_type=MESH))
    return out


def _exchange_start(src, after, ks, per_peer, name, slots=BY_DEVICE):
    slab = src.shape[1:] if per_peer else src.shape
    land_shape = (slots[1],) + tuple(slab)
    n = len(ks)

    def body(src_ref, land_ref, after_ref, send_sems, recv_sems, src_thru, land_thru, token):
        for cp in _split_copies(src_ref, land_ref, send_sems, recv_sems, ks, per_peer, False, slots):
            cp.start()
        token[...] = jnp.zeros_like(token)

    return pl.pallas_call(
        body, name=name,
        out_shape=(pltpu.SemaphoreType.DMA((n,)), pltpu.SemaphoreType.DMA((n,)),
                   pltpu.HBM(src.shape, src.dtype), pltpu.HBM(land_shape, src.dtype),
                   jax.ShapeDtypeStruct((8, LANE), F32)),
        in_specs=(HBM_SPEC, HBM_SPEC, ANY),
        out_specs=(SEM_SPEC, SEM_SPEC, HBM_SPEC, HBM_SPEC, pl.BlockSpec(memory_space=pltpu.VMEM)),
        input_output_aliases={0: 2, 1: 3},
        compiler_params=pltpu.CompilerParams(has_side_effects=pltpu.SideEffectType.DATAFLOW_SIDE_EFFECTING),
    )(_hbm(src), _hbm(lax.empty(land_shape, src.dtype)), after)


def _exchange_wait(started, after, ks, per_peer, name, slots=BY_DEVICE):
    send_sems, recv_sems, src_thru, land_thru = started

    def body(src_ref, land_ref, send_sems, recv_sems, after_ref, src_dead, land_out):
        for cp in _split_copies(src_ref, land_ref, send_sems, recv_sems, ks, per_peer, True, slots):
            cp.wait_send()
            cp.wait_recv()

    return pl.pallas_call(
        body, name=name,
        out_shape=(pltpu.HBM(src_thru.shape, src_thru.dtype), pltpu.HBM(land_thru.shape, land_thru.dtype)),
        in_specs=(HBM_SPEC, HBM_SPEC, SEM_SPEC, SEM_SPEC, ANY), out_specs=(HBM_SPEC, HBM_SPEC),
        input_output_aliases={0: 0, 1: 1},
        compiler_params=pltpu.CompilerParams(has_side_effects=pltpu.SideEffectType.DATAFLOW_SIDE_EFFECTING),
    )(src_thru, land_thru, send_sems, recv_sems, after)


def _relay_copies(land_ref, send_sems, recv_sems, landed):
    me = _position()
    sibling = _peer(me, 1)
    out = []
    for i, k in enumerate(GATHER_PEERS[1:]):
        blk = land_ref.at[_slot(_peer(sibling if landed else me, k))]
        out.append(pltpu.make_async_remote_copy(
            src_ref=blk, dst_ref=blk, send_sem=send_sems.at[i], recv_sem=recv_sems.at[i],
            device_id=sibling, device_id_type=MESH))
    return out


def _relay_start(land, name):
    n = len(GATHER_PEERS) - 1

    def body(land_ref, send_sems, recv_sems, land_thru, token):
        for cp in _relay_copies(land_ref, send_sems, recv_sems, landed=False):
            cp.start()
        token[...] = jnp.zeros_like(token)

    return pl.pallas_call(
        body, name=name,
        out_shape=(pltpu.SemaphoreType.DMA((n,)), pltpu.SemaphoreType.DMA((n,)),
                   pltpu.HBM(land.shape, land.dtype), jax.ShapeDtypeStruct((8, LANE), F32)),
        in_specs=(HBM_SPEC,),
        out_specs=(SEM_SPEC, SEM_SPEC, HBM_SPEC, pl.BlockSpec(memory_space=pltpu.VMEM)),
        input_output_aliases={0: 2},
        compiler_params=pltpu.CompilerParams(has_side_effects=pltpu.SideEffectType.DATAFLOW_SIDE_EFFECTING),
    )(_hbm(land))


def _relay_wait(started, after, name):
    send_sems, recv_sems, land_thru = started

    def body(land_ref, send_sems, recv_sems, after_ref, land_out):
        for cp in _relay_copies(land_ref, send_sems, recv_sems, landed=True):
            cp.wait_send()
            cp.wait_recv()

    return pl.pallas_call(
        body, name=name,
        out_shape=pltpu.HBM(land_thru.shape, land_thru.dtype),
        in_specs=(HBM_SPEC, SEM_SPEC, SEM_SPEC, ANY), out_specs=HBM_SPEC,
        input_output_aliases={0: 0},
        compiler_params=pltpu.CompilerParams(has_side_effects=pltpu.SideEffectType.DATAFLOW_SIDE_EFFECTING),
    )(land_thru, send_sems, recv_sems, after)


def _share(vec, name, after=None):
    follows = [] if after is None else [after]

    def body(vec_ref, *rest):
        out_ref, send_sems, recv_sems, local_sem = rest[len(follows):]
        me = _position()

        def copy(k, landed):
            peer = _peer(me, k)
            return pltpu.make_async_remote_copy(
                src_ref=vec_ref, dst_ref=out_ref.at[_slot(peer) if landed else _slot(me)],
                send_sem=send_sems.at[k - 1], recv_sem=recv_sems.at[k - 1], device_id=peer, device_id_type=MESH)

        mine = pltpu.make_async_copy(vec_ref, out_ref.at[_slot(me)], local_sem)
        mine.start()
        sent = [copy(k, False) for k in ALL_PEERS]
        for cp in sent:
            cp.start()
        for k in ALL_PEERS:
            copy(k, True).wait_recv()
        for cp in sent:
            cp.wait_send()
        mine.wait()

    return pl.pallas_call(
        body, name=name,
        in_specs=[ANY] * (1 + len(follows)), out_specs=ANY,
        out_shape=jax.ShapeDtypeStruct((N_DEV,) + vec.shape, vec.dtype),
        scratch_shapes=[pltpu.SemaphoreType.DMA((N_DEV - 1,)), pltpu.SemaphoreType.DMA((N_DEV - 1,)),
                        pltpu.SemaphoreType.DMA],
    )(vec, *follows)


def _sum_slots(parts):
    def body(p_ref, o_ref):
        acc = p_ref[0]
        for dev in range(1, N_DEV):
            acc = acc + p_ref[dev]
        o_ref[...] = acc

    return pl.pallas_call(body, name="sum_slots",
                          out_shape=jax.ShapeDtypeStruct(parts.shape[1:], F32))(parts)


PACK_ROWS = 8


def _packed_rows(size):
    return -(-size // (PACK_ROWS * LANE)) * PACK_ROWS


def _pack(arrs):
    def rows(a):
        flat = a.reshape(-1)
        return jnp.pad(flat, (0, _packed_rows(flat.shape[0]) * LANE - flat.shape[0])).reshape(-1, LANE)

    return jnp.concatenate([rows(a) for a in arrs], axis=0)


def _unpack(packed, shapes):
    out, at = [], 0
    for shp in shapes:
        size = 1
        for dim in shp:
            size *= dim
        nrows = _packed_rows(size)
        out.append(packed[at:at + nrows].reshape(-1)[:size].reshape(shp))
        at += nrows
    return out


def _layer_fwd(x, wt, wo, g_pre, g_post, wa_pad, b_alpha, g_gla, g_att, rb_pad, midway=None, h=None,
               g_pre_next=None):
    if h is None:
        h = _rms_fwd(x, g_pre)
    z = _matmul(h, wt, "nt", F32, *TILES["in_proj"], "in_proj", n_outer=True)
    y_gla, o_gla, *gla_kept = _gla_fwd(z, wa_pad, b_alpha, g_gla)
    y_att, o_att, probs = _att_fwd(z, rb_pad, g_att)
    token = None if midway is None else midway(y_att)
    y = _matmul_cols([y_gla, y_att], wo, F32, *TILES["out_proj"][:2], "out_proj", after=token)
    out, h_next = _post_fwd(x, y, g_post, g_pre_next)
    return out, h_next, (x, h, z, o_gla, gla_kept, o_att, probs, y_gla, y_att, y)


def _layer_bwd(dout, saved, wt, wo, g_pre, g_post, wa_pad, b_alpha, g_gla, g_att, rb_pad, on_dwo, on_dwt):
    x, h, z, o_gla, gla_kept, o_att, probs, y_gla, y_att, y = saved
    dy, dg_post = _post_bwd(dout, y, g_post)
    dwo = _matmul_rows([y_gla, y_att], dy, BF16, *TILES["out_proj_dw"][:2], "out_proj_dw")
    token = on_dwo(dwo)
    dycat = _matmul(dy, wo, "nt", F32, *TILES["out_proj_dx"], "out_proj_dx", n_outer=True, after=token)
    dq, dk, dv, dgg, dga, dwa, db, dg_gla = _gla_bwd(dycat, o_gla, z, wa_pad, g_gla, *gla_kept)
    daq, dak, dav, dag, drb, dg_att = _att_bwd(dycat, o_att, probs, z, g_att)
    tw, tn = TILES["in_proj_dw"][:2]
    dwt = (_matmul_rows([dq, dk, dv, dgg], h, BF16, tw, tn, "in_proj_dw_gla"),
           _matmul_rows([daq, dak, dav, dag], h, BF16, tw, tn, "in_proj_dw_att"),
           _matmul_rows([dga], h, BF16, LANE, tn, "in_proj_dw_gate"))
    token = on_dwt(dwt)
    dh = _matmul_cols([dq, dk, dv, dgg, daq, dak, dav, dag, dga], wt, F32, *TILES["in_proj_dx"][:2],
                      "in_proj_dx", after=token)
    dx, dg_pre = _pre_bwd(dh, x, g_pre, dout)
    small = (dg_pre[0], dg_post[0], dwa[:GLA_RANK], db[0], dg_gla[0], dg_att[0], drb[:, 0, :N_REL])
    return dx, small


def kernel(x, w_in, w_out, g_pre, g_post, w_alpha, b_alpha, g_gla, g_att, rel_bias, loss_target, m_w_in, m_w_out, m_g_pre, m_g_post, m_w_alpha, m_b_alpha, m_g_gla, m_g_att, m_rel_bias, v_w_in, v_w_out, v_g_pre, v_g_post, v_w_alpha, v_b_alpha, v_g_gla, v_g_att, v_rel_bias):
    nl, d, cols = w_in.shape
    rows = w_out.shape[1]
    s = x.shape[1]
    x0 = x.reshape(s, d)
    tgt = loss_target.reshape(s, d)

    cols_first = lambda a: jnp.transpose(a, (2, 0, 1))
    w_c = cols_first(w_in)
    slab = _slab_rows(rows, cols)
    is_out = lax.broadcasted_iota(jnp.int32, (slab, d), 0) < rows

    def shard(l, zero=0.0):
        top = jnp.pad((w_out[l] + zero).astype(BF16), ((0, slab - rows), (0, 0)))
        rest = jnp.pad((w_c[:, l] + zero).astype(BF16), ((rows, slab - rows - cols), (0, 0)))
        return jnp.where(is_out, top, rest)

    first_fetch = _exchange_start(shard(0), x, GATHER_PEERS, False, "gather_start_0")
    began = first_fetch[4][0, 0]
    shards = [None] + [shard(l, began) for l in range(1, nl)]
    alpha = _pack([w_alpha]) + began
    wa_g = _share(alpha, "gather_alpha")
    wa_cols = w_alpha.shape[2]
    wa_full = wa_g.reshape(N_DEV, -1)[:, :nl * GLA_RANK * wa_cols].reshape(N_DEV, nl, GLA_RANK, wa_cols)
    wa_full = jnp.transpose(wa_full, (1, 2, 0, 3)).reshape(nl, GLA_RANK, GLA_KW)
    wa_pad = jnp.pad(wa_full, ((0, 0), (0, LANE - GLA_RANK), (0, 0)))
    rb_pad = jnp.pad(rel_bias, ((0, 0), (0, 0), (0, 3 * LANE - N_REL)))

    def layer_args(l, follows=None):
        gp = g_pre[l:l + 1] if follows is None else g_pre[l:l + 1] + follows[:1, :1]
        return (wts[l], wos[l], gp, g_post[l:l + 1], wa_pad[l], b_alpha[l:l + 1], g_gla[l:l + 1],
                g_att[l:l + 1], rb_pad[l])

    my = _slot(_position())

    def fetch(l, after):
        return _exchange_start(shards[l], after, GATHER_PEERS, False, f"gather_start_{l}")

    def relay(l, first_hop, after):
        own[l], land = _exchange_wait(first_hop[:4], after, GATHER_PEERS, False, f"gather_wait_{l}")
        return _relay_start(land, f"relay_start_{l}")

    def midway(l, y):
        flight["relay"] = relay(l + 1, flight["fetch"], y)
        if l + 2 >= nl:
            return flight["relay"][3]
        flight["fetch"] = fetch(l + 2, flight["relay"][2])
        return flight["fetch"][4]

    act, h_next, saved, wts, wos, flight, own = x0, None, [], [], [], {}, [None] * nl
    prepared = (wa_pad[0, :1, :1] + sum(sh[:1, :1].astype(F32) for sh in shards[1:]))
    flight["relay"] = relay(0, first_fetch, prepared)
    if nl > 1:
        flight["fetch"] = fetch(1, flight["relay"][2])
    for l in range(nl):
        land = _relay_wait(flight["relay"][:3], act, f"relay_wait_{l}")
        land = lax.dynamic_update_slice_in_dim(land, own[l][None], my, 0)
        wt_l, wo_l = _aligned_weight(land, rows, cols)
        wts.append(wt_l)
        wos.append(wo_l)
        act, h_next, sv = _layer_fwd(act, *layer_args(l, follows=first_fetch[4] if l == 0 else None),
                                     midway=functools.partial(midway, l) if l + 1 < nl else None, h=h_next,
                                     g_pre_next=g_pre[l + 1:l + 2] if l + 1 < nl else None)
        saved.append(sv)
    dout, sq = _loss_head(act, tgt)
    loss = lax.psum(sq[0, 0] * (0.5 / d), ("x", "y", "c"))

    smalls, pending_out, pending_in = [None] * nl, [None] * nl, [None] * nl

    def send_out(l, dwo):
        pending_out[l] = _exchange_start(dwo.reshape(N_DEV, rows, d), dwo[:1, :1], ALL_PEERS, True,
                                         f"scatter_out_start_{l}")
        return pending_out[l][4]

    def send_in(l, dwt):
        if l > 0:
            pending_in[l] = _exchange_start(_partial_slabs(dwt, cols), dwt[-1], ALL_PEERS, True,
                                            f"scatter_in_start_{l}")
            return pending_in[l][4]
        pair = _exchange_start(_partial_slabs(dwt, cols, by_core=True), dwt[-1], (1,), True,
                               "pair_start_0", slots=BY_CORE)
        by_core, from_sibling = _exchange_wait(pair[:4], pair[4], (1,), True, "pair_wait_0", slots=BY_CORE)
        pending_in[l] = _exchange_start(_pair_sum(by_core, from_sibling), dwt[-1], GATHER_PEERS[1:], True,
                                        "scatter_in_start_0", slots=BY_CHIP)
        return pending_in[l][4]

    for l in reversed(range(nl)):
        dout, smalls[l] = _layer_bwd(dout, saved[l], *layer_args(l), on_dwo=functools.partial(send_out, l),
                                     on_dwt=functools.partial(send_in, l))
    grad_x = dout.reshape(x.shape)

    def landed(started, after, name, ks=ALL_PEERS, slots=BY_DEVICE):
        partial, land = _exchange_wait(started[:4], after, ks, True, name, slots=slots)
        mine = slots[0](_position())
        return lax.dynamic_update_slice_in_dim(land, lax.dynamic_slice_in_dim(partial, mine, 1, 0), mine, 0)

    parts_out = [landed(pending_out[l], dout, f"scatter_out_wait_{l}") for l in range(nl)]
    g_w_out, d_w_out, m2_w_out, v2_w_out = _adam_sharded(parts_out, 0, w_out, m_w_out, v_w_out, "adam_w_out")
    names = 7
    small_stacked = [jnp.stack([smalls[l][i] for l in range(nl)]) for i in range(names)]
    shapes = [a.shape for a in small_stacked]
    gathered = _share(_pack(small_stacked), "gather_small_grads", after=d_w_out)
    g_pre_g, g_post_g, wa_g_full, b_g, gla_g, att_g, rb_g = _unpack(_sum_slots(gathered), shapes)
    wa_g_mine = lax.dynamic_slice_in_dim(wa_g_full, my * wa_cols, wa_cols, axis=2)
    grads = [g_pre_g, g_post_g, wa_g_mine, b_g, gla_g, att_g, rb_g]
    ws = [g_pre, g_post, w_alpha, b_alpha, g_gla, g_att, rel_bias]
    ms = [m_g_pre, m_g_post, m_w_alpha, m_b_alpha, m_g_gla, m_g_att, m_rel_bias]
    vs = [v_g_pre, v_g_post, v_w_alpha, v_b_alpha, v_g_gla, v_g_att, v_rel_bias]
    d_s, m2_s, v2_s = _adam_small(ws, grads, ms, vs)

    parts_in = [landed(pending_in[0], d_s[0], "scatter_in_wait_0", GATHER_PEERS[1:], BY_CHIP)]
    parts_in += [landed(pending_in[l], d_s[0], f"scatter_in_wait_{l}") for l in range(1, nl)]
    g_w_in, d_w_in, m2_w_in, v2_w_in = [
        jnp.transpose(a, (1, 2, 0))
        for a in _adam_columns(parts_in, 0, w_c, cols_first(m_w_in), cols_first(v_w_in))]

    def ordered(big_in, big_out, small):
        return [big_in, big_out] + list(small)

    return (loss, grad_x,
            *ordered(g_w_in, g_w_out, grads),
            *ordered(d_w_in, d_w_out, d_s),
            *ordered(m2_w_in, m2_w_out, m2_s),
            *ordered(v2_w_in, v2_w_out, v2_s))
```

```python
import functools

import jax
import jax.numpy as jnp
from jax import lax
from jax.experimental import pallas as pl
from jax.experimental.pallas import tpu as pltpu

F32 = jnp.float32
BF16 = jnp.bfloat16
MESH = pl.DeviceIdType.MESH
ANY = pl.BlockSpec(memory_space=pl.ANY)

CHUNK = 64
GLA_HEADS = 4
GLA_DK = 128
GLA_DV = 256
GLA_KW = GLA_HEADS * GLA_DK
D_GLA = GLA_HEADS * GLA_DV
GLA_RANK = 16
GLA_TAU = 16.0
ATT_HEADS = 8
ATT_HD = 128
D_ATT = ATT_HEADS * ATT_HD
LEFT_CHUNKS = 8
REL_CLIP = 128
N_REL = 2 * REL_CLIP + 1
EPS = 1e-6
D_IN = 2 * GLA_KW + 2 * D_GLA + GLA_RANK + 4 * D_ATT
GLA_SCALE = GLA_DK ** -0.5
ATT_SCALE = ATT_HD ** -0.5

ADAM_LR = 0.001
ADAM_B1 = 0.9
ADAM_B2 = 0.999
ADAM_EPS = 1e-08
ADAM_WD = 0.01
ADAM_STEP = 10

N_DEV = 8
LANE = 128
GA_ORIG = 2 * GLA_KW + 2 * D_GLA
OFF_AQ = GA_ORIG
OFF_GA = GA_ORIG + 4 * D_ATT
D_ZP = OFF_GA + LANE
QB = 2 * CHUNK
ATT_UNROLL = 16
WIN = (LEFT_CHUNKS + 2) * CHUNK
ET_ROWS = WIN + LEFT_CHUNKS * CHUNK
NEG = -1e30
VMEM_LIMIT = 48 * 1024 * 1024


def _cparams(sem):
    return pltpu.CompilerParams(dimension_semantics=sem, vmem_limit_bytes=VMEM_LIMIT)


def _dot(a, b):
    return jnp.dot(a, b, preferred_element_type=F32)


def _dot_nt(a, b):
    return lax.dot_general(a, b, (((1,), (1,)), ((), ())), preferred_element_type=F32)


def _dot_tn(a, b):
    return lax.dot_general(a, b, (((0,), (0,)), ((), ())), preferred_element_type=F32)


def _dot01(t, x, left=True):
    if not left:
        t, x = x, t
    hi = x.astype(BF16)
    r = x - hi.astype(F32)
    mid = r.astype(BF16)
    lo = (r - mid.astype(F32)).astype(BF16)
    if left:
        return _dot(t, hi) + _dot(t, mid) + _dot(t, lo)
    return _dot(hi, t) + _dot(mid, t) + _dot(lo, t)


def _sigmoid(x):
    return 1.0 / (1.0 + jnp.exp(-x))


def _log_sigmoid(x):
    return jnp.minimum(x, 0.0) - jnp.log(1.0 + jnp.exp(-jnp.abs(x)))


TILES = {
    "in_proj": (512, D_ZP // 3, None),
    "in_proj_dx": (512, 512, None),
    "in_proj_dw": (512, 2048, None),
    "out_proj": (512, 1024, None),
    "out_proj_dx": (512, 1024, None),
    "out_proj_dw": (1024, 1024, None),
}


def _matmul(a, b, mode, out_dtype, tm, tn, tk, name, n_outer=False, after=None):
    if mode == "nn":
        (m, k), n = a.shape, b.shape[1]
    elif mode == "nt":
        (m, k), n = a.shape, b.shape[0]
    else:
        (k, m), n = a.shape, b.shape[1]
    tm, tn, tk = min(tm, m), min(tn, n), k if tk is None else min(tk, k)
    assert m % tm == 0 and n % tn == 0 and k % tk == 0, (name, m, n, k)
    nk = k // tk
    dot = {"nn": _dot, "nt": _dot_nt, "tn": _dot_tn}[mode]

    follows = [] if after is None else [after]

    def body_whole_k(a_ref, b_ref, *rest):
        o_ref = rest[-1]
        o_ref[...] = dot(a_ref[...], b_ref[...]).astype(out_dtype)

    def body(a_ref, b_ref, *rest):
        o_ref, acc_ref = rest[-2:]
        kk = pl.program_id(2)

        @pl.when(kk == 0)
        def _():
            acc_ref[...] = jnp.zeros_like(acc_ref)

        acc_ref[...] += dot(a_ref[...], b_ref[...])

        @pl.when(kk == nk - 1)
        def _():
            o_ref[...] = acc_ref[...].astype(out_dtype)

    def at(index):
        return (lambda j, i, kk: index(i, j, kk)) if n_outer else index

    if mode == "tn":
        a_spec = pl.BlockSpec((tk, tm), at(lambda i, j, kk: (kk, i)))
    else:
        a_spec = pl.BlockSpec((tm, tk), at(lambda i, j, kk: (i, kk)))
    if mode == "nt":
        b_spec = pl.BlockSpec((tn, tk), at(lambda i, j, kk: (j, kk)))
    else:
        b_spec = pl.BlockSpec((tk, tn), at(lambda i, j, kk: (kk, j)))
    return pl.pallas_call(
        body_whole_k if nk == 1 else body, name=name,
        grid=(n // tn, m // tm, nk) if n_outer else (m // tm, n // tn, nk),
        in_specs=[a_spec, b_spec] + [ANY] * len(follows),
        out_specs=pl.BlockSpec((tm, tn), at(lambda i, j, kk: (i, j))),
        out_shape=jax.ShapeDtypeStruct((m, n), out_dtype),
        scratch_shapes=[] if nk == 1 else [pltpu.VMEM((tm, tn), F32)],
        compiler_params=_cparams(("parallel", "parallel", "arbitrary")),
    )(a, b, *follows)


def _matmul_cols(pieces, b, out_dtype, tm, tn, name, after=None):
    m, n = pieces[0].shape[0], b.shape[1]
    widths = [p.shape[1] for p in pieces]
    starts = [sum(widths[:i]) for i in range(len(pieces))]
    follows = [] if after is None else [after]
    tm, tn = min(tm, m), min(tn, n)
    assert sum(widths) == b.shape[0] and m % tm == 0 and n % tn == 0, name

    def body(*refs):
        b_ref, o_ref = refs[len(pieces)], refs[-1]
        acc = None
        for p_ref, at, width in zip(refs, starts, widths):
            part = _dot(p_ref[...], b_ref[at:at + width, :])
            acc = part if acc is None else acc + part
        o_ref[...] = acc.astype(out_dtype)

    return pl.pallas_call(
        body, name=name, grid=(n // tn, m // tm),
        in_specs=[pl.BlockSpec((tm, width), lambda j, i: (i, 0)) for width in widths]
        + [pl.BlockSpec((b.shape[0], tn), lambda j, i: (0, j))] + [ANY] * len(follows),
        out_specs=pl.BlockSpec((tm, tn), lambda j, i: (i, j)),
        out_shape=jax.ShapeDtypeStruct((m, n), out_dtype),
        compiler_params=_cparams(("parallel", "parallel")),
    )(*pieces, b, *follows)


def _matmul_rows(pieces, b, out_dtype, tw, tn, name):
    k, n = b.shape
    tn = min(tn, n)
    counts = [p.shape[1] // tw for p in pieces]
    firsts = [sum(counts[:i]) for i in range(len(pieces))]
    assert all(p.shape[1] % tw == 0 for p in pieces) and n % tn == 0, name

    def body(*refs):
        b_ref, o_ref = refs[len(pieces):]
        for p_ref, first, count in zip(refs, firsts, counts):
            @pl.when((pl.program_id(0) >= first) & (pl.program_id(0) < first + count))
            def _(p_ref=p_ref):
                o_ref[...] = _dot_tn(p_ref[...], b_ref[...]).astype(out_dtype)

    def piece_spec(first, count):
        return pl.BlockSpec((k, tw), lambda i, j: (0, jnp.clip(i - first, 0, count - 1)))

    return pl.pallas_call(
        body, name=name, grid=(sum(counts), n // tn),
        in_specs=[piece_spec(first, count) for first, count in zip(firsts, counts)]
        + [pl.BlockSpec((k, tn), lambda i, j: (0, j))],
        out_specs=pl.BlockSpec((tw, tn), lambda i, j: (i, j)),
        out_shape=jax.ShapeDtypeStruct((sum(counts) * tw, n), out_dtype),
        compiler_params=_cparams(("parallel", "parallel")),
    )(*pieces, b)


ROWS = 512


def _rms_fwd(x, g):
    s, d = x.shape

    def body(x_ref, g_ref, h_ref):
        xv = x_ref[...]
        r = lax.rsqrt(jnp.mean(xv * xv, axis=-1, keepdims=True) + EPS)
        h_ref[...] = (xv * r * g_ref[...]).astype(BF16)

    return pl.pallas_call(
        body, name="rms_fwd", grid=(s // ROWS,),
        in_specs=[pl.BlockSpec((ROWS, d), lambda i: (i, 0)), pl.BlockSpec((1, d), lambda i: (0, 0))],
        out_specs=pl.BlockSpec((ROWS, d), lambda i: (i, 0)),
        out_shape=jax.ShapeDtypeStruct((s, d), BF16),
        compiler_params=_cparams(("parallel",)),
    )(x, g)


def _post_fwd(x, y, g, g_next=None):
    s, d = x.shape

    def body(x_ref, y_ref, g_ref, *rest):
        yv = y_ref[...]
        r = lax.rsqrt(jnp.mean(yv * yv, axis=-1, keepdims=True) + EPS)
        out = x_ref[...] + yv * r * g_ref[...]
        if g_next is None:
            rest[0][...] = out
            return
        gn_ref, o_ref, h_ref = rest
        o_ref[...] = out
        rn = lax.rsqrt(jnp.mean(out * out, axis=-1, keepdims=True) + EPS)
        h_ref[...] = (out * rn * gn_ref[...]).astype(BF16)

    row = pl.BlockSpec((ROWS, d), lambda i: (i, 0))
    vec = pl.BlockSpec((1, d), lambda i: (0, 0))
    both = g_next is not None
    res = pl.pallas_call(
        body, name="post_fwd", grid=(s // ROWS,),
        in_specs=[row, row, vec] + [vec] * both,
        out_specs=[row] + [row] * both,
        out_shape=[jax.ShapeDtypeStruct((s, d), F32)] + [jax.ShapeDtypeStruct((s, d), BF16)] * both,
        compiler_params=_cparams(("parallel",)),
    )(x, y, g, *([g_next] * both))
    return (res[0], res[1]) if both else (res[0], None)


def _loss_head(out, tgt):
    s, d = out.shape

    def body(o_ref, t_ref, dout_ref, sum_ref):
        @pl.when(pl.program_id(0) == 0)
        def _():
            sum_ref[...] = jnp.zeros_like(sum_ref)

        e = o_ref[...] - t_ref[...]
        dout_ref[...] = e * (1.0 / d)
        sum_ref[...] += jnp.sum(jnp.sum(e * e, axis=1, keepdims=True), axis=0, keepdims=True)

    row = pl.BlockSpec((ROWS, d), lambda i: (i, 0))
    return pl.pallas_call(
        body, name="loss_head", grid=(s // ROWS,),
        in_specs=[row, row],
        out_specs=[row, pl.BlockSpec((1, 1), lambda i: (0, 0))],
        out_shape=[jax.ShapeDtypeStruct((s, d), F32), jax.ShapeDtypeStruct((1, 1), F32)],
        compiler_params=_cparams(("arbitrary",)),
    )(out, tgt)


def _post_bwd(dout, y, g):
    s, d = y.shape

    def body(do_ref, y_ref, g_ref, dy_ref, dg_ref):
        @pl.when(pl.program_id(0) == 0)
        def _():
            dg_ref[...] = jnp.zeros_like(dg_ref)

        yv = y_ref[...]
        dv = do_ref[...]
        r = lax.rsqrt(jnp.mean(yv * yv, axis=-1, keepdims=True) + EPS)
        dg_ref[...] += jnp.sum(dv * yv * r, axis=0, keepdims=True)
        w = dv * g_ref[...]
        dy = r * (w - yv * (r * r) * jnp.mean(w * yv, axis=-1, keepdims=True))
        dy_ref[...] = dy.astype(BF16)

    row = pl.BlockSpec((ROWS, d), lambda i: (i, 0))
    vec = pl.BlockSpec((1, d), lambda i: (0, 0))
    return pl.pallas_call(
        body, name="post_bwd", grid=(s // ROWS,),
        in_specs=[row, row, vec],
        out_specs=[row, vec],
        out_shape=[jax.ShapeDtypeStruct((s, d), BF16), jax.ShapeDtypeStruct((1, d), F32)],
        compiler_params=_cparams(("arbitrary",)),
    )(dout, y, g)


def _pre_bwd(dh, x, g, dout, below=None):
    s, d = x.shape
    rows = ROWS if below is None else ROWS // 2

    def body(dh_ref, x_ref, g_ref, do_ref, *rest):
        if below is None:
            dx_ref, dg_ref = rest
            sums = (dg_ref,)
        else:
            y_ref, gq_ref, dx_ref, dg_ref, dy_ref, dgq_ref = rest
            sums = (dg_ref, dgq_ref)

        @pl.when(pl.program_id(0) == 0)
        def _():
            for ref in sums:
                ref[...] = jnp.zeros_like(ref)

        xv = x_ref[...]
        dv = dh_ref[...]
        r = lax.rsqrt(jnp.mean(xv * xv, axis=-1, keepdims=True) + EPS)
        w = dv * g_ref[...]
        dx = do_ref[...] + r * (w - xv * (r * r) * jnp.mean(w * xv, axis=-1, keepdims=True))
        if below is not None:
            yv = y_ref[...]
            ry = lax.rsqrt(jnp.mean(yv * yv, axis=-1, keepdims=True) + EPS)
            dgq_ref[...] += jnp.sum(dx * yv * ry, axis=0, keepdims=True)
            wy = dx * gq_ref[...]
            dy_ref[...] = (ry * (wy - yv * (ry * ry) * jnp.mean(wy * yv, axis=-1, keepdims=True))).astype(BF16)
        dg_ref[...] += jnp.sum(dv * xv * r, axis=0, keepdims=True)
        dx_ref[...] = dx

    row = pl.BlockSpec((rows, d), lambda i: (i, 0))
    vec = pl.BlockSpec((1, d), lambda i: (0, 0))
    fused = below is not None
    return pl.pallas_call(
        body, name="pre_bwd", grid=(s // rows,),
        in_specs=[row, row, vec, row] + [row, vec] * fused,
        out_specs=[row, vec] + [row, vec] * fused,
        out_shape=[jax.ShapeDtypeStruct((s, d), F32), jax.ShapeDtypeStruct((1, d), F32)]
        + [jax.ShapeDtypeStruct((s, d), BF16), jax.ShapeDtypeStruct((1, d), F32)] * fused,
        compiler_params=_cparams(("arbitrary",)),
    )(dh, x, g, dout, *(below if fused else ()))


GLA_STEP = 4
GLA_ROWS = GLA_STEP * CHUNK
GLA_CHUNKS = [slice(c * CHUNK, (c + 1) * CHUNK) for c in range(GLA_STEP)]


def _chunk_triangles():
    ri = lax.broadcasted_iota(jnp.int32, (GLA_ROWS, GLA_ROWS), 0)
    ci = lax.broadcasted_iota(jnp.int32, (GLA_ROWS, GLA_ROWS), 1)
    same = (ri // CHUNK) == (ci // CHUNK)
    return (jnp.where(same & (ri >= ci), 1.0, 0.0).astype(BF16), jnp.where(same & (ci >= ri), 1.0, 0.0).astype(BF16))


def _per_chunk(fn, like):
    row = lax.broadcasted_iota(jnp.int32, like.shape, 0)
    return [fn((row >= c * CHUNK) & (row < (c + 1) * CHUNK)) for c in range(GLA_STEP)]


def _spread(per_chunk, like):
    row = lax.broadcasted_iota(jnp.int32, like.shape, 0)
    out = per_chunk[-1]
    for c in reversed(range(GLA_STEP - 1)):
        out = jnp.where(row < (c + 1) * CHUNK, per_chunk[c], out)
    return out


def _gla_gate(ga_b, wa_b, b_ref, tri):
    pre = _dot(ga_b, wa_b) + b_ref[...]
    la = _log_sigmoid(pre) * (1.0 / GLA_TAU)
    return pre, _dot01(tri, la)


def _chunk_ends(cum):
    row = lax.broadcasted_iota(jnp.int32, cum.shape, 0)
    return [jnp.sum(jnp.where(row == (c + 1) * CHUNK - 1, cum, 0.0), axis=0, keepdims=True)
            for c in range(GLA_STEP)]


def _heads(width):
    return [slice(h * width, (h + 1) * width) for h in range(GLA_HEADS)]


def _z_specs_gla(rev=None):
    idx = (lambda n: n) if rev is None else rev
    return [
        pl.BlockSpec((GLA_ROWS, GLA_KW), lambda n: (idx(n), 0)),
        pl.BlockSpec((GLA_ROWS, GLA_KW), lambda n: (idx(n), 1)),
        pl.BlockSpec((GLA_ROWS, D_GLA), lambda n: (idx(n), 1)),
        pl.BlockSpec((GLA_ROWS, D_GLA), lambda n: (idx(n), 2)),
        pl.BlockSpec((GLA_ROWS, LANE), lambda n: (idx(n), OFF_GA // LANE)),
    ]


def _gla_fwd(z, wa_pad, b_alpha, g_gla):
    s = z.shape[0]
    nchunk = s // CHUNK

    def body(q_ref, k_ref, v_ref, gg_ref, ga_ref, wa_ref, b_ref, g_ref, y_ref, o_ref, st_ref, pre_ref, cum_ref,
             state):
        @pl.when(pl.program_id(0) == 0)
        def _():
            state[...] = jnp.zeros_like(state)

        ga_b = ga_ref[...].astype(BF16)
        tri, _ = _chunk_triangles()
        nh = range(GLA_HEADS)
        keys, vals = _heads(GLA_DK), _heads(GLA_DV)
        pre, cum = _gla_gate(ga_b, wa_ref[...].astype(BF16), b_ref, tri)
        pre_ref[...] = pre
        cum_ref[...] = cum
        cends = _chunk_ends(cum)
        kd_b = (k_ref[...] * jnp.exp(_spread(cends, cum) - cum)).astype(BF16)
        qs = (q_ref[...] * GLA_SCALE).astype(BF16)
        v_b = v_ref[...].astype(BF16)
        uts = [[_dot_tn(v_b[rs, vals[h]], kd_b[rs, keys[h]]) for h in nh] for rs in GLA_CHUNKS]
        sts, prev = [], [state[h] for h in nh]
        for c in range(GLA_STEP):
            a = jnp.exp(cends[c])
            prev = [prev[h] * a[:, keys[h]] + uts[c][h] for h in nh]
            sts.append(prev)
        for h in nh:
            state[h] = prev[h]
            for c in range(GLA_STEP):
                st_ref[c, h] = sts[c][h]
        outs = [[_dot_nt(qs[rs, keys[h]], sts[c][h].astype(BF16)) for h in nh] for c, rs in enumerate(GLA_CHUNKS)]
        for h in nh:
            o, vs = jnp.concatenate([outs[c][h] for c in range(GLA_STEP)], axis=0), vals[h]
            o_ref[:, vs] = o
            r = lax.rsqrt(jnp.mean(o * o, axis=-1, keepdims=True) + EPS)
            gg = gg_ref[:, vs]
            y_ref[:, vs] = (o * r * g_ref[:, vs] * (gg * _sigmoid(gg))).astype(BF16)

    full = lambda shape: pl.BlockSpec(shape, lambda n: tuple(0 for _ in shape))
    wide = pl.BlockSpec((GLA_ROWS, D_GLA), lambda n: (n, 0))
    return pl.pallas_call(
        body, name="gla_fwd", grid=(nchunk // GLA_STEP,),
        in_specs=_z_specs_gla() + [full((LANE, GLA_KW)), full((1, GLA_KW)), full((1, D_GLA))],
        out_specs=[wide, wide, pl.BlockSpec((GLA_STEP, GLA_HEADS, GLA_DV, GLA_DK), lambda n: (n, 0, 0, 0)),
                   pl.BlockSpec((GLA_ROWS, GLA_KW), lambda n: (n, 0)), pl.BlockSpec((GLA_ROWS, GLA_KW), lambda n: (n, 0))],
        out_shape=[jax.ShapeDtypeStruct((s, D_GLA), BF16), jax.ShapeDtypeStruct((s, D_GLA), F32),
                   jax.ShapeDtypeStruct((nchunk, GLA_HEADS, GLA_DV, GLA_DK), F32),
                   jax.ShapeDtypeStruct((s, GLA_KW), F32), jax.ShapeDtypeStruct((s, GLA_KW), F32)],
        scratch_shapes=[pltpu.VMEM((GLA_HEADS, GLA_DV, GLA_DK), F32)],
        compiler_params=_cparams(("arbitrary",)),
    )(z, z, z, z, z, wa_pad, b_alpha, g_gla)


def _gla_bwd(dyc, o_gla, z, wa_pad, g_gla, states, gate_pre, gate_cum):
    s = z.shape[0]
    nsteps = s // GLA_ROWS
    rev = lambda n: nsteps - 1 - n

    def body(dy_ref, o_ref, q_ref, k_ref, v_ref, gg_ref, ga_ref, wa_ref, g_ref, st_ref, stp_ref, pre_ref, cum_ref,
             dq_ref, dk_ref, dv_ref, dgg_ref, dga_ref, dwa_ref, db_ref, dg_ref, carry):
        step = pl.program_id(0)

        @pl.when(step == 0)
        def _():
            carry[...] = jnp.zeros_like(carry)
            dwa_ref[...] = jnp.zeros_like(dwa_ref)
            db_ref[...] = jnp.zeros_like(db_ref)
            dg_ref[...] = jnp.zeros_like(dg_ref)

        has_prev = (step < nsteps - 1).astype(F32)
        ga_b = ga_ref[...].astype(BF16)
        _, tri_up = _chunk_triangles()
        nh, nc = range(GLA_HEADS), range(GLA_STEP)
        keys, vals = _heads(GLA_DK), _heads(GLA_DV)
        wa_b = wa_ref[...].astype(BF16)
        pre, cum = pre_ref[...], cum_ref[...]
        cends = _chunk_ends(cum)
        e = jnp.exp(_spread(cends, cum) - cum)
        a = [jnp.exp(cends[c]) for c in nc]
        kf = k_ref[...]
        kd_b = (kf * e).astype(BF16)
        v_b = v_ref[...].astype(BF16)
        qs = (q_ref[...] * GLA_SCALE).astype(BF16)
        do_b = []
        for h in nh:
            vs = vals[h]
            o = o_ref[:, vs]
            gg = gg_ref[:, vs]
            g = g_ref[:, vs]
            dy = dy_ref[:, vs]
            r = lax.rsqrt(jnp.mean(o * o, axis=-1, keepdims=True) + EPS)
            sg = _sigmoid(gg)
            dogn = dy * (gg * sg)
            dgg_ref[:, vs] = (dy * (o * r * g) * (sg * (1.0 + gg * (1.0 - sg)))).astype(BF16)
            dg_ref[:, vs] += jnp.sum(dogn * o * r, axis=0, keepdims=True)
            w = dogn * g
            do_b.append((r * (w - o * (r * r) * jnp.mean(w * o, axis=-1, keepdims=True))).astype(BF16))
        dqs = [jnp.concatenate([_dot(do_b[h][rs], st_ref[c, h].astype(BF16)) for c, rs in enumerate(GLA_CHUNKS)],
                               axis=0) for h in nh]
        dq_ref[...] = (jnp.concatenate(dqs, axis=1) * GLA_SCALE).astype(BF16)
        own = [[_dot_tn(do_b[h][rs], qs[rs, keys[h]]) for h in nh] for rs in GLA_CHUNKS]
        gts, later = [None] * GLA_STEP, [carry[h] for h in nh]
        for c in reversed(nc):
            gts[c] = [own[c][h] + later[h] for h in nh]
            later = [gts[c][h] * a[c][:, keys[h]] for h in nh]
        for h in nh:
            carry[h] = later[h]
        gt_b = [[gts[c][h].astype(BF16) for h in nh] for c in nc]
        dkd = jnp.concatenate([jnp.concatenate([_dot(v_b[rs, vals[h]], gt_b[c][h]) for h in nh], axis=1)
                               for c, rs in enumerate(GLA_CHUNKS)], axis=0)
        dvs = [[_dot_nt(kd_b[rs, keys[h]], gt_b[c][h]) for h in nh] for c, rs in enumerate(GLA_CHUNKS)]
        before = lambda c, h: st_ref[c - 1, h] if c > 0 else stp_ref[0, h] * has_prev
        da = [jnp.concatenate([jnp.sum(gts[c][h] * before(c, h), axis=0, keepdims=True) for h in nh], axis=1)
              for c in nc]
        for h in nh:
            dv_ref[:, vals[h]] = jnp.concatenate([dvs[c][h] for c in nc], axis=0).astype(BF16)
        dk_ref[...] = (dkd * e).astype(BF16)
        dd = dkd * kf * e
        dsum = _per_chunk(lambda mine: jnp.sum(jnp.where(mine, dd, 0.0), axis=0, keepdims=True), dd)
        dcend = _spread([dsum[c] + da[c] * a[c] for c in nc], dd)
        dla = dcend - _dot01(tri_up, dd)
        dpre = dla * (1.0 / GLA_TAU) * (1.0 - _sigmoid(pre))
        dpre_b = dpre.astype(BF16)
        dga_ref[...] = _dot_nt(dpre_b, wa_b).astype(BF16)
        dwa_ref[...] += _dot_tn(ga_b, dpre_b)
        db_ref[...] += jnp.sum(dpre, axis=0, keepdims=True)

    full = lambda shape: pl.BlockSpec(shape, lambda n: tuple(0 for _ in shape))
    wide = pl.BlockSpec((GLA_ROWS, D_GLA), lambda n: (rev(n), 0))
    keyw = pl.BlockSpec((GLA_ROWS, GLA_KW), lambda n: (rev(n), 0))
    st_spec = pl.BlockSpec((GLA_STEP, GLA_HEADS, GLA_DV, GLA_DK), lambda n: (rev(n), 0, 0, 0))
    stp_spec = pl.BlockSpec((1, GLA_HEADS, GLA_DV, GLA_DK),
                            lambda n: (jnp.maximum(GLA_STEP * rev(n) - 1, 0), 0, 0, 0))
    return pl.pallas_call(
        body, name="gla_bwd", grid=(nsteps,),
        in_specs=[wide, wide] + _z_specs_gla(rev)
        + [full((LANE, GLA_KW)), full((1, D_GLA)), st_spec, stp_spec, keyw, keyw],
        out_specs=[keyw, keyw, wide, wide, pl.BlockSpec((GLA_ROWS, LANE), lambda n: (rev(n), 0)),
                   full((LANE, GLA_KW)), full((1, GLA_KW)), full((1, D_GLA))],
        out_shape=[jax.ShapeDtypeStruct((s, GLA_KW), BF16), jax.ShapeDtypeStruct((s, GLA_KW), BF16),
                   jax.ShapeDtypeStruct((s, D_GLA), BF16), jax.ShapeDtypeStruct((s, D_GLA), BF16),
                   jax.ShapeDtypeStruct((s, LANE), BF16),
                   jax.ShapeDtypeStruct((LANE, GLA_KW), F32), jax.ShapeDtypeStruct((1, GLA_KW), F32),
                   jax.ShapeDtypeStruct((1, D_GLA), F32)],
        scratch_shapes=[pltpu.VMEM((GLA_HEADS, GLA_DV, GLA_DK), F32)],
        compiler_params=_cparams(("arbitrary",)),
    )(dyc, o_gla, z, z, z, z, z, wa_pad, g_gla, states, states, gate_pre, gate_cum)


def _build_bias_table(rb_row, et_ref):
    far = jnp.broadcast_to(rb_row[:, 2 * REL_CLIP:2 * REL_CLIP + 1], (1, LANE))
    near_hi = rb_row[:, REL_CLIP:2 * REL_CLIP]
    near_lo = rb_row[:, 0:REL_CLIP]
    past = jnp.broadcast_to(rb_row[:, 0:1], (1, LANE))
    seg = [far, far, far, far, near_hi, near_lo] + [past] * (ET_ROWS // LANE - 5)
    ri = lax.broadcasted_iota(jnp.int32, (LANE, LANE), 0)
    ci = lax.broadcasted_iota(jnp.int32, (LANE, LANE), 1)
    for kb in range(ET_ROWS // LANE):
        wmat = jnp.where(ri + ci < LANE, seg[kb], seg[kb + 1])
        blk = pltpu.roll(wmat, 0, 1, stride=1, stride_axis=0)
        lag = LEFT_CHUNKS + ci // CHUNK - (2 * kb + ri // CHUNK)
        et_ref[kb * LANE:(kb + 1) * LANE, :] = jnp.where((lag >= 0) & (lag <= LEFT_CHUNKS), blk, NEG)


def _reduce_bias_table(det_ref):
    lane = lax.broadcasted_iota(jnp.int32, (1, LANE), 1)
    ri = lax.broadcasted_iota(jnp.int32, (LANE, LANE), 0)
    ci = lax.broadcasted_iota(jnp.int32, (LANE, LANE), 1)
    flip = jnp.where(ri + ci == LANE - 1, 1.0, 0.0).astype(BF16)
    segs = jnp.zeros((8, LANE), F32)
    seg_row = lax.broadcasted_iota(jnp.int32, (8, LANE), 0)
    prev_minus = jnp.zeros((1, LANE), F32)
    for kb in range(6):
        rolled = pltpu.roll(_dot01(det_ref[kb * LANE:(kb + 1) * LANE, :], flip, left=False), 0, 1,
                            stride=1, stride_axis=0)
        plus = jnp.sum(jnp.where(ci >= ri, rolled, 0.0), axis=0, keepdims=True)
        minus = jnp.sum(jnp.where(ci < ri, rolled, 0.0), axis=0, keepdims=True)
        segs = segs + jnp.where(seg_row == kb, plus + prev_minus, 0.0)
        prev_minus = minus
    segs = _dot01(segs, flip, left=False)
    pick = lambda kb: jnp.sum(jnp.where(seg_row == kb, segs, 0.0), axis=0, keepdims=True)
    far = jnp.sum(pick(0) + pick(1) + pick(2) + pick(3), axis=1, keepdims=True)
    last = jnp.where(lane == 0, far, 0.0)
    return jnp.concatenate([pick(5), pick(4), last], axis=1)


def _att_window(b):
    c0 = 2 * b
    kstart = pl.multiple_of(jnp.maximum(c0 - LEFT_CHUNKS, 0) * CHUNK, CHUNK)
    eoff = pl.multiple_of(jnp.maximum(LEFT_CHUNKS - c0, 0) * CHUNK, CHUNK)
    return kstart, eoff


def _att_probs(q_b, kw_b, et):
    st = _dot_nt(kw_b, q_b) * ATT_SCALE + et
    m = jnp.max(st, axis=0, keepdims=True)
    ex = jnp.exp(st - m)
    return ex * (1.0 / jnp.sum(ex, axis=0, keepdims=True))


def _att_fwd(z, rb_pad, g_att):
    s = z.shape[0]
    nblk = s // QB
    c_aq, c_ak, c_av, c_ag = [(OFF_AQ + i * D_ATT) // ATT_HD for i in range(4)]

    def body(q_ref, k_ref, v_ref, ag_ref, rb_ref, g_ref, y_ref, o_ref, p_ref, et_ref, kb_ref, vb_ref):
        h = pl.program_id(0)
        b = pl.program_id(1)

        @pl.when(b == 0)
        def _():
            _build_bias_table(rb_ref[pl.ds(h, 1), :], et_ref)
            kb_ref[...] = k_ref[...].astype(BF16)
            vb_ref[...] = v_ref[...].astype(BF16)

        for j in range(ATT_UNROLL):
            rs = slice(j * QB, (j + 1) * QB)
            kstart, eoff = _att_window(b * ATT_UNROLL + j)
            q_b = q_ref[rs, :].astype(BF16)
            kw_b = kb_ref[pl.ds(kstart, WIN), :]
            vw_b = vb_ref[pl.ds(kstart, WIN), :]
            pt = _att_probs(q_b, kw_b, et_ref[pl.ds(eoff, WIN), :])
            p_ref[0, j] = pt
            o = _dot_tn(pt.astype(BF16), vw_b)
            o_ref[rs, :] = o
            r = lax.rsqrt(jnp.mean(o * o, axis=-1, keepdims=True) + EPS)
            ag = ag_ref[rs, :]
            y_ref[rs, :] = (o * r * g_ref[...] * (ag * _sigmoid(ag))).astype(BF16)

    blk = lambda col: pl.BlockSpec((ATT_UNROLL * QB, ATT_HD), lambda h, b: (b, col + h))
    seq = lambda col: pl.BlockSpec((s, ATT_HD), lambda h, b: (0, col + h))
    out_blk = pl.BlockSpec((ATT_UNROLL * QB, ATT_HD), lambda h, b: (b, h))
    return pl.pallas_call(
        body, name="att_fwd", grid=(ATT_HEADS, nblk // ATT_UNROLL),
        in_specs=[blk(c_aq), seq(c_ak), seq(c_av), blk(c_ag),
                  pl.BlockSpec((ATT_HEADS, 3 * LANE), lambda h, b: (0, 0)),
                  pl.BlockSpec((1, ATT_HD), lambda h, b: (0, h))],
        out_specs=[out_blk, out_blk, pl.BlockSpec((1, ATT_UNROLL, WIN, QB), lambda h, b: (h, b, 0, 0))],
        out_shape=[jax.ShapeDtypeStruct((s, D_ATT), BF16), jax.ShapeDtypeStruct((s, D_ATT), F32),
                   jax.ShapeDtypeStruct((ATT_HEADS, nblk, WIN, QB), F32)],
        scratch_shapes=[pltpu.VMEM((ET_ROWS, LANE), F32), pltpu.VMEM((s, ATT_HD), BF16),
                        pltpu.VMEM((s, ATT_HD), BF16)],
        compiler_params=_cparams(("arbitrary", "arbitrary")),
    )(z, z, z, z, rb_pad, g_att)


def _att_bwd(dyc, o_att, probs, z, g_att):
    s = z.shape[0]
    nblk = s // QB
    c_aq, c_ak, c_av, c_ag = [(OFF_AQ + i * D_ATT) // ATT_HD for i in range(4)]
    c_dy = D_GLA // ATT_HD

    def body(dy_ref, o_ref, p_ref, q_ref, k_ref, v_ref, ag_ref, g_ref,
             dq_ref, dk_ref, dv_ref, dag_ref, drb_ref, dg_ref, det_ref, kb_ref, vb_ref, dk_acc, dv_acc):
        b = pl.program_id(1)

        @pl.when(b == 0)
        def _():
            kb_ref[...] = k_ref[...].astype(BF16)
            vb_ref[...] = v_ref[...].astype(BF16)
            det_ref[...] = jnp.zeros_like(det_ref)
            dk_acc[...] = jnp.zeros_like(dk_acc)
            dv_acc[...] = jnp.zeros_like(dv_acc)
            dg_ref[...] = jnp.zeros_like(dg_ref)

        g = g_ref[...]
        dg = jnp.zeros((1, ATT_HD), F32)
        for j in range(ATT_UNROLL):
            rs = slice(j * QB, (j + 1) * QB)
            kstart, eoff = _att_window(b * ATT_UNROLL + j)
            q_b = q_ref[rs, :].astype(BF16)
            kw_b = kb_ref[pl.ds(kstart, WIN), :]
            vw_b = vb_ref[pl.ds(kstart, WIN), :]
            pt = p_ref[0, j]
            o = o_ref[rs, :]
            ag = ag_ref[rs, :]
            dy = dy_ref[rs, :]
            r = lax.rsqrt(jnp.mean(o * o, axis=-1, keepdims=True) + EPS)
            sg = _sigmoid(ag)
            don = dy * (ag * sg)
            dag_ref[rs, :] = (dy * (o * r * g) * (sg * (1.0 + ag * (1.0 - sg)))).astype(BF16)
            dg = dg + jnp.sum(don * o * r, axis=0, keepdims=True)
            w = don * g
            do_b = (r * (w - o * (r * r) * jnp.mean(w * o, axis=-1, keepdims=True))).astype(BF16)
            pt_b = pt.astype(BF16)
            dpt = _dot_nt(vw_b, do_b)
            dst = pt * (dpt - jnp.sum(dpt * pt, axis=0, keepdims=True))
            det_ref[pl.ds(eoff, WIN), :] += dst
            ds_b = (dst * ATT_SCALE).astype(BF16)
            dq_ref[rs, :] = _dot_tn(ds_b, kw_b).astype(BF16)
            dk_acc[pl.ds(kstart, WIN), :] += _dot(ds_b, q_b)
            dv_acc[pl.ds(kstart, WIN), :] += _dot(pt_b, do_b)
        dg_ref[...] += dg

        @pl.when(b == nblk // ATT_UNROLL - 1)
        def _():
            drb_ref[0] = jnp.broadcast_to(_reduce_bias_table(det_ref), (8, 3 * LANE))
            dk_ref[...] = dk_acc[...].astype(BF16)
            dv_ref[...] = dv_acc[...].astype(BF16)

    blk = lambda col: pl.BlockSpec((ATT_UNROLL * QB, ATT_HD), lambda h, b: (b, col + h))
    seq = lambda col: pl.BlockSpec((s, ATT_HD), lambda h, b: (0, col + h))
    out_blk = pl.BlockSpec((ATT_UNROLL * QB, ATT_HD), lambda h, b: (b, h))
    out_seq = pl.BlockSpec((s, ATT_HD), lambda h, b: (0, h))
    return pl.pallas_call(
        body, name="att_bwd", grid=(ATT_HEADS, nblk // ATT_UNROLL),
        in_specs=[blk(c_dy), blk(0), pl.BlockSpec((1, ATT_UNROLL, WIN, QB), lambda h, b: (h, b, 0, 0)),
                  blk(c_aq), seq(c_ak), seq(c_av), blk(c_ag),
                  pl.BlockSpec((1, ATT_HD), lambda h, b: (0, h))],
        out_specs=[out_blk, out_seq, out_seq, out_blk,
                   pl.BlockSpec((1, 8, 3 * LANE), lambda h, b: (h, 0, 0)),
                   pl.BlockSpec((1, ATT_HD), lambda h, b: (0, h))],
        out_shape=[jax.ShapeDtypeStruct((s, D_ATT), BF16), jax.ShapeDtypeStruct((s, D_ATT), BF16),
                   jax.ShapeDtypeStruct((s, D_ATT), BF16), jax.ShapeDtypeStruct((s, D_ATT), BF16),
                   jax.ShapeDtypeStruct((ATT_HEADS, 8, 3 * LANE), F32),
                   jax.ShapeDtypeStruct((1, D_ATT), F32)],
        scratch_shapes=[pltpu.VMEM((ET_ROWS, LANE), F32),
                        pltpu.VMEM((s, ATT_HD), BF16), pltpu.VMEM((s, ATT_HD), BF16),
                        pltpu.VMEM((s, ATT_HD), F32), pltpu.VMEM((s, ATT_HD), F32)],
        compiler_params=_cparams(("arbitrary", "arbitrary")),
    )(dyc, o_att, probs, z, z, z, z, g_att)


ADAM_ROWS = 64
ADAM_COL_ROWS = 32


def _adam_math(w, g, m, v):
    m2 = ADAM_B1 * m + (1.0 - ADAM_B1) * g
    v2 = ADAM_B2 * v + (1.0 - ADAM_B2) * (g * g)
    m_hat = m2 / (1.0 - ADAM_B1 ** ADAM_STEP)
    v_hat = v2 / (1.0 - ADAM_B2 ** ADAM_STEP)
    delta = -ADAM_LR * (m_hat / (jnp.sqrt(v_hat) + ADAM_EPS) + ADAM_WD * w)
    return delta, m2, v2


def _adam_sharded(parts, first, w, m, v, name):
    nl, nr, nc = w.shape

    def body(*refs):
        p_refs = refs[:nl]
        w_ref, m_ref, v_ref, g_ref, d_ref, m2_ref, v2_ref = refs[nl:]
        for k in range(nl):
            @pl.when(pl.program_id(0) == k)
            def _(p_ref=p_refs[k]):
                g = p_ref[0].astype(F32)
                for dev in range(1, N_DEV):
                    g = g + p_ref[dev].astype(F32)
                delta, m2, v2 = _adam_math(w_ref[0], g, m_ref[0], v_ref[0])
                g_ref[0] = g
                d_ref[0] = delta
                m2_ref[0] = m2
                v2_ref[0] = v2

    def part_spec(k):
        return pl.BlockSpec((N_DEV, ADAM_ROWS, nc), lambda l, i: (0, first + jnp.where(l == k, i, 0), 0))

    blk = pl.BlockSpec((1, ADAM_ROWS, nc), lambda l, i: (l, i, 0))
    shp = jax.ShapeDtypeStruct(w.shape, F32)
    return pl.pallas_call(
        body, name=name, grid=(nl, pl.cdiv(nr, ADAM_ROWS)),
        in_specs=[part_spec(k) for k in range(nl)] + [blk, blk, blk],
        out_specs=[blk, blk, blk, blk],
        out_shape=[shp, shp, shp, shp],
        compiler_params=_cparams(("arbitrary", "arbitrary")),
    )(*parts, w, m, v)


def _adam_columns(parts, first, w, m, v):
    nc, nl, d = w.shape

    def body(*refs):
        p_refs = refs[:nl]
        w_ref, m_ref, v_ref, g_ref, d_ref, m2_ref, v2_ref = refs[nl:]
        for l in range(nl):
            g = p_refs[l][0].astype(F32)
            for slot in range(1, parts[l].shape[0]):
                g = g + p_refs[l][slot].astype(F32)
            delta, m2, v2 = _adam_math(w_ref[:, l, :], g, m_ref[:, l, :], v_ref[:, l, :])
            g_ref[:, l, :] = g
            d_ref[:, l, :] = delta
            m2_ref[:, l, :] = m2
            v2_ref[:, l, :] = v2

    blk = pl.BlockSpec((ADAM_COL_ROWS, nl, d), lambda i: (i, 0, 0))
    shp = jax.ShapeDtypeStruct(w.shape, F32)
    return pl.pallas_call(
        body, name="adam_w_in", grid=(pl.cdiv(nc, ADAM_COL_ROWS),),
        in_specs=[pl.BlockSpec((p.shape[0], ADAM_COL_ROWS, d), lambda i: (0, first + i, 0)) for p in parts]
        + [blk, blk, blk],
        out_specs=[blk, blk, blk, blk],
        out_shape=[shp, shp, shp, shp],
        compiler_params=_cparams(("parallel",)),
    )(*parts, w, m, v)


def _adam_small(ws, gs, ms, vs):
    n = len(ws)

    def body(*refs):
        w_refs, g_refs, m_refs, v_refs, d_refs, m2_refs, v2_refs = [refs[i * n:(i + 1) * n] for i in range(7)]
        for i in range(n):
            delta, m2, v2 = _adam_math(w_refs[i][...], g_refs[i][...], m_refs[i][...], v_refs[i][...])
            d_refs[i][...] = delta
            m2_refs[i][...] = m2
            v2_refs[i][...] = v2

    shapes = [jax.ShapeDtypeStruct(w.shape, F32) for w in ws]
    out = pl.pallas_call(body, name="adam_small", out_shape=shapes * 3)(*ws, *gs, *ms, *vs)
    return out[:n], out[n:2 * n], out[2 * n:]


def _position():
    return lax.axis_index("x"), lax.axis_index("y"), lax.axis_index("c")


def _slot(p):
    return 4 * p[0] + 2 * p[1] + p[2]


BF16_TILE_ROWS = 16


def _slab_rows(rows, cols):
    return -(-(rows + cols) // BF16_TILE_ROWS) * BF16_TILE_ROWS


RELAYOUT_COLS = 1024
RELAYOUT_CHUNK = 64


def _shard_pieces(dev, rows, cols):
    moved = ((0, GA_ORIG, 0), (GA_ORIG, GA_ORIG + GLA_RANK, OFF_GA - GA_ORIG), (GA_ORIG + GLA_RANK, D_IN, -GLA_RANK))
    c0, c1 = dev * cols, (dev + 1) * cols
    return [(rows + max(c0, lo) - c0, max(c0, lo) + off, min(c1, hi) - max(c0, lo))
            for lo, hi, off in moved if max(c0, lo) < min(c1, hi)]


def _move_rows(src, src_row, dst, dst_row, n):
    assert src_row % 2 == 0 and dst_row % 2 == 0 and n % 2 == 0
    for r in range(0, n // 2, RELAYOUT_CHUNK):
        m = min(RELAYOUT_CHUNK, n // 2 - r)
        dst[dst_row // 2 + r:dst_row // 2 + r + m, :] = src[src_row // 2 + r:src_row // 2 + r + m, :]


def _aligned_weight(land, rows, cols):
    _, slab, d = land.shape
    ct = min(RELAYOUT_COLS, d)

    def body(land_ref, wt_ref, wo_ref):
        dev = pl.program_id(1)
        src = land_ref.bitcast(jnp.uint32)
        dst = wt_ref.bitcast(jnp.uint32)
        wo_ref[...] = land_ref[0:rows, :]

        @pl.when(dev == 0)
        def _():
            dst[D_IN // 2:D_ZP // 2, :] = jnp.zeros(((D_ZP - D_IN) // 2, ct), jnp.uint32)

        for k in range(N_DEV):
            @pl.when(dev == k)
            def _(k=k):
                for at, to, n in _shard_pieces(k, rows, cols):
                    _move_rows(src, at, dst, to, n)

    return pl.pallas_call(
        body, name="aligned_weight", grid=(d // ct, N_DEV),
        in_specs=[pl.BlockSpec((slab, ct), lambda c, dev: (dev, c))],
        out_specs=[pl.BlockSpec((D_ZP, ct), lambda c, dev: (0, c)),
                   pl.BlockSpec((rows, ct), lambda c, dev: (dev, c))],
        out_shape=[jax.ShapeDtypeStruct((D_ZP, d), land.dtype),
                   jax.ShapeDtypeStruct((N_DEV * rows, d), land.dtype)],
        compiler_params=_cparams(("parallel", "arbitrary")),
    )(land.reshape(N_DEV * slab, d))


def _partial_slabs(dwt, cols, by_core=False):
    d = dwt[0].shape[1]
    bounds = (0, GA_ORIG, OFF_GA, D_ZP)
    assert tuple(a.shape[0] for a in dwt) == tuple(hi - lo for lo, hi in zip(bounds, bounds[1:]))
    slab = _slab_rows(0, cols)
    ct = min(RELAYOUT_COLS, d)

    def body(*refs):
        out_ref = refs[-1]
        dev = pl.program_id(1)
        srcs = [ref.bitcast(jnp.uint32) for ref in refs[:-1]]
        dst = out_ref.bitcast(jnp.uint32)
        dst[cols // 2:slab // 2, :] = jnp.zeros(((slab - cols) // 2, ct), jnp.uint32)
        for k in range(N_DEV):
            @pl.when(dev == k)
            def _(k=k):
                for to, at, n in _shard_pieces(k, 0, cols):
                    which = max(i for i, lo in enumerate(bounds[:-1]) if lo <= at)
                    assert at + n <= bounds[which + 1]
                    _move_rows(srcs[which], at - bounds[which], dst, to, n)

    place = (lambda dev: (dev % 2) * (N_DEV // 2) + dev // 2) if by_core else (lambda dev: dev)
    out = pl.pallas_call(
        body, name="partial_slabs", grid=(d // ct, N_DEV),
        in_specs=[pl.BlockSpec((a.shape[0], ct), lambda c, dev: (0, c)) for a in dwt],
        out_specs=pl.BlockSpec((slab, ct), lambda c, dev: (place(dev), c)),
        out_shape=jax.ShapeDtypeStruct((N_DEV * slab, d), dwt[0].dtype),
        compiler_params=_cparams(("parallel", "arbitrary")),
    )(*dwt)
    return out.reshape((2, N_DEV // 2, slab, d) if by_core else (N_DEV, slab, d))


def _pair_sum(mine, theirs):
    _, nchip, slab, d = mine.shape
    rows = next(r for r in range(512, 0, -BF16_TILE_ROWS) if slab % r == 0)

    def body(m_ref, t_ref, o_ref):
        south = lax.axis_index("c") == 0
        own = jnp.where(south, m_ref[0, 0], m_ref[1, 0]).astype(F32)
        got = jnp.where(south, t_ref[1, 0], t_ref[0, 0]).astype(F32)
        o_ref[0] = (own + got).astype(o_ref.dtype)

    both = pl.BlockSpec((2, 1, rows, d), lambda j, i: (0, j, i, 0))
    return pl.pallas_call(
        body, name="pair_sum", grid=(nchip, slab // rows),
        in_specs=[both, both],
        out_specs=pl.BlockSpec((1, rows, d), lambda j, i: (j, i, 0)),
        out_shape=jax.ShapeDtypeStruct((nchip, slab, d), mine.dtype),
        compiler_params=_cparams(("parallel", "parallel")),
    )(mine, theirs)


def _peer(pos, k):
    x, y, c = pos
    return (1 - x if k & 4 else x, 1 - y if k & 2 else y, 1 - c if k & 1 else c)


HBM_SPEC = pl.BlockSpec(memory_space=pltpu.HBM)
SEM_SPEC = pl.BlockSpec(memory_space=pltpu.SEMAPHORE)
GATHER_PEERS = (1, 4, 2, 6)
ALL_PEERS = (1, 2, 3, 4, 5, 6, 7)


def _hbm(a):
    return pltpu.with_memory_space_constraint(a, pltpu.HBM)


BY_DEVICE = (_slot, N_DEV)
BY_CORE = (lambda p: p[2], 2)
BY_CHIP = (lambda p: 2 * p[0] + p[1], 4)


def _split_copies(src_ref, land_ref, send_sems, recv_sems, ks, per_peer, landed, slots):
    slot_of = slots[0]
    me = _position()
    out = []
    for i, k in enumerate(ks):
        peer = _peer(me, k)
        src = src_ref.at[slot_of(peer)] if per_peer else src_ref
        dst = land_ref.at[slot_of(peer) if landed else slot_of(me)]
        out.append(pltpu.make_async_remote_copy(
            src_ref=src, dst_ref=dst, send_sem=send_sems.at[i], recv_sem=recv_sems.at[i],
            device_id=peer, device_id_type=MESH))
    return out


def _exchange_start(src, after, ks, per_peer, name, slots=BY_DEVICE):
    slab = src.shape[1:] if per_peer else src.shape
    land_shape = (slots[1],) + tuple(slab)
    n = len(ks)

    def body(src_ref, land_ref, after_ref, send_sems, recv_sems, src_thru, land_thru, token):
        for cp in _split_copies(src_ref, land_ref, send_sems, recv_sems, ks, per_peer, False, slots):
            cp.start()
        token[...] = jnp.zeros_like(token)

    return pl.pallas_call(
        body, name=name,
        out_shape=(pltpu.SemaphoreType.DMA((n,)), pltpu.SemaphoreType.DMA((n,)),
                   pltpu.HBM(src.shape, src.dtype), pltpu.HBM(land_shape, src.dtype),
                   jax.ShapeDtypeStruct((8, LANE), F32)),
        in_specs=(HBM_SPEC, HBM_SPEC, ANY),
        out_specs=(SEM_SPEC, SEM_SPEC, HBM_SPEC, HBM_SPEC, pl.BlockSpec(memory_space=pltpu.VMEM)),
        input_output_aliases={0: 2, 1: 3},
        compiler_params=pltpu.CompilerParams(has_side_effects=pltpu.SideEffectType.DATAFLOW_SIDE_EFFECTING),
    )(_hbm(src), _hbm(lax.empty(land_shape, src.dtype)), after)


def _exchange_wait(started, after, ks, per_peer, name, slots=BY_DEVICE):
    send_sems, recv_sems, src_thru, land_thru = started

    def body(src_ref, land_ref, send_sems, recv_sems, after_ref, src_dead, land_out):
        for cp in _split_copies(src_ref, land_ref, send_sems, recv_sems, ks, per_peer, True, slots):
            cp.wait_send()
            cp.wait_recv()

    return pl.pallas_call(
        body, name=name,
        out_shape=(pltpu.HBM(src_thru.shape, src_thru.dtype), pltpu.HBM(land_thru.shape, land_thru.dtype)),
        in_specs=(HBM_SPEC, HBM_SPEC, SEM_SPEC, SEM_SPEC, ANY), out_specs=(HBM_SPEC, HBM_SPEC),
        input_output_aliases={0: 0, 1: 1},
        compiler_params=pltpu.CompilerParams(has_side_effects=pltpu.SideEffectType.DATAFLOW_SIDE_EFFECTING),
    )(src_thru, land_thru, send_sems, recv_sems, after)


def _relay_copies(land_ref, send_sems, recv_sems, landed):
    me = _position()
    sibling = _peer(me, 1)
    out = []
    for i, k in enumerate(GATHER_PEERS[1:]):
        blk = land_ref.at[_slot(_peer(sibling if landed else me, k))]
        out.append(pltpu.make_async_remote_copy(
            src_ref=blk, dst_ref=blk, send_sem=send_sems.at[i], recv_sem=recv_sems.at[i],
            device_id=sibling, device_id_type=MESH))
    return out


def _relay_start(land, name):
    n = len(GATHER_PEERS) - 1

    def body(land_ref, send_sems, recv_sems, land_thru, token):
        for cp in _relay_copies(land_ref, send_sems, recv_sems, landed=False):
            cp.start()
        token[...] = jnp.zeros_like(token)

    return pl.pallas_call(
        body, name=name,
        out_shape=(pltpu.SemaphoreType.DMA((n,)), pltpu.SemaphoreType.DMA((n,)),
                   pltpu.HBM(land.shape, land.dtype), jax.ShapeDtypeStruct((8, LANE), F32)),
        in_specs=(HBM_SPEC,),
        out_specs=(SEM_SPEC, SEM_SPEC, HBM_SPEC, pl.BlockSpec(memory_space=pltpu.VMEM)),
        input_output_aliases={0: 2},
        compiler_params=pltpu.CompilerParams(has_side_effects=pltpu.SideEffectType.DATAFLOW_SIDE_EFFECTING),
    )(_hbm(land))


def _relay_wait(started, after, name):
    send_sems, recv_sems, land_thru = started

    def body(land_ref, send_sems, recv_sems, after_ref, land_out):
        for cp in _relay_copies(land_ref, send_sems, recv_sems, landed=True):
            cp.wait_send()
            cp.wait_recv()

    return pl.pallas_call(
        body, name=name,
        out_shape=pltpu.HBM(land_thru.shape, land_thru.dtype),
        in_specs=(HBM_SPEC, SEM_SPEC, SEM_SPEC, ANY), out_specs=HBM_SPEC,
        input_output_aliases={0: 0},
        compiler_params=pltpu.CompilerParams(has_side_effects=pltpu.SideEffectType.DATAFLOW_SIDE_EFFECTING),
    )(land_thru, send_sems, recv_sems, after)


def _share(vec, name, after=None):
    follows = [] if after is None else [after]

    def body(vec_ref, *rest):
        out_ref, send_sems, recv_sems, local_sem = rest[len(follows):]
        me = _position()

        def copy(k, landed):
            peer = _peer(me, k)
            return pltpu.make_async_remote_copy(
                src_ref=vec_ref, dst_ref=out_ref.at[_slot(peer) if landed else _slot(me)],
                send_sem=send_sems.at[k - 1], recv_sem=recv_sems.at[k - 1], device_id=peer, device_id_type=MESH)

        mine = pltpu.make_async_copy(vec_ref, out_ref.at[_slot(me)], local_sem)
        mine.start()
        sent = [copy(k, False) for k in ALL_PEERS]
        for cp in sent:
            cp.start()
        for k in ALL_PEERS:
            copy(k, True).wait_recv()
        for cp in sent:
            cp.wait_send()
        mine.wait()

    return pl.pallas_call(
        body, name=name,
        in_specs=[ANY] * (1 + len(follows)), out_specs=ANY,
        out_shape=jax.ShapeDtypeStruct((N_DEV,) + vec.shape, vec.dtype),
        scratch_shapes=[pltpu.SemaphoreType.DMA((N_DEV - 1,)), pltpu.SemaphoreType.DMA((N_DEV - 1,)),
                        pltpu.SemaphoreType.DMA],
    )(vec, *follows)


def _sum_slots(parts):
    def body(p_ref, o_ref):
        acc = p_ref[0]
        for dev in range(1, N_DEV):
            acc = acc + p_ref[dev]
        o_ref[...] = acc

    return pl.pallas_call(body, name="sum_slots",
                          out_shape=jax.ShapeDtypeStruct(parts.shape[1:], F32))(parts)


PACK_ROWS = 8


def _packed_rows(size):
    return -(-size // (PACK_ROWS * LANE)) * PACK_ROWS


def _pack(arrs):
    def rows(a):
        flat = a.reshape(-1)
        return jnp.pad(flat, (0, _packed_rows(flat.shape[0]) * LANE - flat.shape[0])).reshape(-1, LANE)

    return jnp.concatenate([rows(a) for a in arrs], axis=0)


def _unpack(packed, shapes):
    out, at = [], 0
    for shp in shapes:
        size = 1
        for dim in shp:
            size *= dim
        nrows = _packed_rows(size)
        out.append(packed[at:at + nrows].reshape(-1)[:size].reshape(shp))
        at += nrows
    return out


def _layer_fwd(x, wt, wo, g_pre, g_post, wa_pad, b_alpha, g_gla, g_att, rb_pad, midway=None, h=None,
               g_pre_next=None):
    if h is None:
        h = _rms_fwd(x, g_pre)
    z = _matmul(h, wt, "nt", F32, *TILES["in_proj"], "in_proj", n_outer=True)
    y_gla, o_gla, *gla_kept = _gla_fwd(z, wa_pad, b_alpha, g_gla)
    y_att, o_att, probs = _att_fwd(z, rb_pad, g_att)
    token = None if midway is None else midway(y_att)
    y = _matmul_cols([y_gla, y_att], wo, F32, *TILES["out_proj"][:2], "out_proj", after=token)
    out, h_next = _post_fwd(x, y, g_post, g_pre_next)
    return out, h_next, (x, h, z, o_gla, gla_kept, o_att, probs, y_gla, y_att, y)


def _layer_bwd(dout, saved, wt, wo, g_pre, g_post, wa_pad, b_alpha, g_gla, g_att, rb_pad, on_dwo, on_dwt,
               post=None, below=None):
    x, h, z, o_gla, gla_kept, o_att, probs, y_gla, y_att, y = saved
    dy, dg_post = _post_bwd(dout, y, g_post) if post is None else post
    dwo = _matmul_rows([y_gla, y_att], dy, BF16, *TILES["out_proj_dw"][:2], "out_proj_dw")
    token = on_dwo(dwo)
    dycat = _matmul(dy, wo, "nt", F32, *TILES["out_proj_dx"], "out_proj_dx", n_outer=True, after=token)
    dq, dk, dv, dgg, dga, dwa, db, dg_gla = _gla_bwd(dycat, o_gla, z, wa_pad, g_gla, *gla_kept)
    daq, dak, dav, dag, drb, dg_att = _att_bwd(dycat, o_att, probs, z, g_att)
    tw, tn = TILES["in_proj_dw"][:2]
    dwt = (_matmul_rows([dq, dk, dv, dgg], h, BF16, tw, tn, "in_proj_dw_gla"),
           _matmul_rows([daq, dak, dav, dag], h, BF16, tw, tn, "in_proj_dw_att"),
           _matmul_rows([dga], h, BF16, LANE, tn, "in_proj_dw_gate"))
    token = on_dwt(dwt)
    dh = _matmul_cols([dq, dk, dv, dgg, daq, dak, dav, dag, dga], wt, F32, *TILES["in_proj_dx"][:2],
                      "in_proj_dx", after=token)
    dx, dg_pre, *post_below = _pre_bwd(dh, x, g_pre, dout, below)
    small = (dg_pre[0], dg_post[0], dwa[:GLA_RANK], db[0], dg_gla[0], dg_att[0], drb[:, 0, :N_REL])
    return dx, small, post_below or None


def kernel(x, w_in, w_out, g_pre, g_post, w_alpha, b_alpha, g_gla, g_att, rel_bias, loss_target, m_w_in, m_w_out, m_g_pre, m_g_post, m_w_alpha, m_b_alpha, m_g_gla, m_g_att, m_rel_bias, v_w_in, v_w_out, v_g_pre, v_g_post, v_w_alpha, v_b_alpha, v_g_gla, v_g_att, v_rel_bias):
    nl, d, cols = w_in.shape
    rows = w_out.shape[1]
    s = x.shape[1]
    x0 = x.reshape(s, d)
    tgt = loss_target.reshape(s, d)

    cols_first = lambda a: jnp.transpose(a, (2, 0, 1))
    w_c = cols_first(w_in)
    slab = _slab_rows(rows, cols)
    is_out = lax.broadcasted_iota(jnp.int32, (slab, d), 0) < rows

    def shard(l, zero=0.0):
        top = jnp.pad((w_out[l] + zero).astype(BF16), ((0, slab - rows), (0, 0)))
        rest = jnp.pad((w_c[:, l] + zero).astype(BF16), ((rows, slab - rows - cols), (0, 0)))
        return jnp.where(is_out, top, rest)

    first_fetch = _exchange_start(shard(0), x, GATHER_PEERS, False, "gather_start_0")
    began = first_fetch[4][0, 0]
    shards = [None] + [shard(l, began) for l in range(1, nl)]
    alpha = _pack([w_alpha]) + began
    wa_g = _share(alpha, "gather_alpha")
    wa_cols = w_alpha.shape[2]
    wa_full = wa_g.reshape(N_DEV, -1)[:, :nl * GLA_RANK * wa_cols].reshape(N_DEV, nl, GLA_RANK, wa_cols)
    wa_full = jnp.transpose(wa_full, (1, 2, 0, 3)).reshape(nl, GLA_RANK, GLA_KW)
    wa_pad = jnp.pad(wa_full, ((0, 0), (0, LANE - GLA_RANK), (0, 0)))
    rb_pad = jnp.pad(rel_bias, ((0, 0), (0, 0), (0, 3 * LANE - N_REL)))

    def layer_args(l, follows=None):
        gp = g_pre[l:l + 1] if follows is None else g_pre[l:l + 1] + follows[:1, :1]
        return (wts[l], wos[l], gp, g_post[l:l + 1], wa_pad[l], b_alpha[l:l + 1], g_gla[l:l + 1],
                g_att[l:l + 1], rb_pad[l])

    my = _slot(_position())

    def fetch(l, after):
        return _exchange_start(shards[l], after, GATHER_PEERS, False, f"gather_start_{l}")

    def relay(l, first_hop, after):
        own[l], land = _exchange_wait(first_hop[:4], after, GATHER_PEERS, False, f"gather_wait_{l}")
        return _relay_start(land, f"relay_start_{l}")

    def midway(l, y):
        flight["relay"] = relay(l + 1, flight["fetch"], y)
        if l + 2 >= nl:
            return flight["relay"][3]
        flight["fetch"] = fetch(l + 2, flight["relay"][2])
        return flight["fetch"][4]

    act, h_next, saved, wts, wos, flight, own = x0, None, [], [], [], {}, [None] * nl
    prepared = (wa_pad[0, :1, :1] + sum(sh[:1, :1].astype(F32) for sh in shards[1:]))
    flight["relay"] = relay(0, first_fetch, prepared)
    if nl > 1:
        flight["fetch"] = fetch(1, flight["relay"][2])
    for l in range(nl):
        land = _relay_wait(flight["relay"][:3], act, f"relay_wait_{l}")
        land = lax.dynamic_update_slice_in_dim(land, own[l][None], my, 0)
        wt_l, wo_l = _aligned_weight(land, rows, cols)
        wts.append(wt_l)
        wos.append(wo_l)
        act, h_next, sv = _layer_fwd(act, *layer_args(l, follows=first_fetch[4] if l == 0 else None),
                                     midway=functools.partial(midway, l) if l + 1 < nl else None, h=h_next,
                                     g_pre_next=g_pre[l + 1:l + 2] if l + 1 < nl else None)
        saved.append(sv)
    dout, sq = _loss_head(act, tgt)
    loss = lax.psum(sq[0, 0] * (0.5 / d), ("x", "y", "c"))

    smalls, pending_out, pending_in = [None] * nl, [None] * nl, [None] * nl

    def send_out(l, dwo):
        pending_out[l] = _exchange_start(dwo.reshape(N_DEV, rows, d), dwo[:1, :1], ALL_PEERS, True,
                                         f"scatter_out_start_{l}")
        return pending_out[l][4]

    def send_in(l, dwt):
        if l > 0:
            pending_in[l] = _exchange_start(_partial_slabs(dwt, cols), dwt[-1], ALL_PEERS, True,
                                            f"scatter_in_start_{l}")
            return pending_in[l][4]
        pair = _exchange_start(_partial_slabs(dwt, cols, by_core=True), dwt[-1], (1,), True,
                               "pair_start_0", slots=BY_CORE)
        by_core, from_sibling = _exchange_wait(pair[:4], pair[4], (1,), True, "pair_wait_0", slots=BY_CORE)
        pending_in[l] = _exchange_start(_pair_sum(by_core, from_sibling), dwt[-1], GATHER_PEERS[1:], True,
                                        "scatter_in_start_0", slots=BY_CHIP)
        return pending_in[l][4]

    post = None
    for l in reversed(range(nl)):
        below = (saved[l - 1][-1], g_post[l - 1:l]) if l > 0 else None
        dout, smalls[l], post = _layer_bwd(dout, saved[l], *layer_args(l), on_dwo=functools.partial(send_out, l),
                                           on_dwt=functools.partial(send_in, l), post=post, below=below)
    grad_x = dout.reshape(x.shape)

    def landed(started, after, name, ks=ALL_PEERS, slots=BY_DEVICE):
        partial, land = _exchange_wait(started[:4], after, ks, True, name, slots=slots)
        mine = slots[0](_position())
        return lax.dynamic_update_slice_in_dim(land, lax.dynamic_slice_in_dim(partial, mine, 1, 0), mine, 0)

    parts_out = [landed(pending_out[l], dout, f"scatter_out_wait_{l}") for l in range(nl)]
    g_w_out, d_w_out, m2_w_out, v2_w_out = _adam_sharded(parts_out, 0, w_out, m_w_out, v_w_out, "adam_w_out")
    names = 7
    small_stacked = [jnp.stack([smalls[l][i] for l in range(nl)]) for i in range(names)]
    shapes = [a.shape for a in small_stacked]
    gathered = _share(_pack(small_stacked), "gather_small_grads", after=d_w_out)
    g_pre_g, g_post_g, wa_g_full, b_g, gla_g, att_g, rb_g = _unpack(_sum_slots(gathered), shapes)
    wa_g_mine = lax.dynamic_slice_in_dim(wa_g_full, my * wa_cols, wa_cols, axis=2)
    grads = [g_pre_g, g_post_g, wa_g_mine, b_g, gla_g, att_g, rb_g]
    ws = [g_pre, g_post, w_alpha, b_alpha, g_gla, g_att, rel_bias]
    ms = [m_g_pre, m_g_post, m_w_alpha, m_b_alpha, m_g_gla, m_g_att, m_rel_bias]
    vs = [v_g_pre, v_g_post, v_w_alpha, v_b_alpha, v_g_gla, v_g_att, v_rel_bias]
    d_s, m2_s, v2_s = _adam_small(ws, grads, ms, vs)

    parts_in = [landed(pending_in[0], d_s[0], "scatter_in_wait_0", GATHER_PEERS[1:], BY_CHIP)]
    parts_in += [landed(pending_in[l], d_s[0], f"scatter_in_wait_{l}") for l in range(1, nl)]
    g_w_in, d_w_in, m2_w_in, v2_w_in = [
        jnp.transpose(a, (1, 2, 0))
        for a in _adam_columns(parts_in, 0, w_c, cols_first(m_w_in), cols_first(v_w_in))]

    def ordered(big_in, big_out, small):
        return [big_in, big_out] + list(small)

    return (loss, grad_x,
            *ordered(g_w_in, g_w_out, grads),
            *ordered(d_w_in, d_w_out, d_s),
            *ordered(m2_w_in, m2_w_out, m2_s),
            *ordered(v2_w_in, v2_w_out, v2_s))
```

```python
import functools

import jax
import jax.numpy as jnp
from jax import lax
from jax.experimental import pallas as pl
from jax.experimental.pallas import tpu as pltpu

F32 = jnp.float32
BF16 = jnp.bfloat16
MESH = pl.DeviceIdType.MESH
ANY = pl.BlockSpec(memory_space=pl.ANY)

CHUNK = 64
GLA_HEADS = 4
GLA_DK = 128
GLA_DV = 256
GLA_KW = GLA_HEADS * GLA_DK
D_GLA = GLA_HEADS * GLA_DV
GLA_RANK = 16
GLA_TAU = 16.0
ATT_HEADS = 8
ATT_HD = 128
D_ATT = ATT_HEADS * ATT_HD
LEFT_CHUNKS = 8
REL_CLIP = 128
N_REL = 2 * REL_CLIP + 1
EPS = 1e-6
D_IN = 2 * GLA_KW + 2 * D_GLA + GLA_RANK + 4 * D_ATT
GLA_SCALE = GLA_DK ** -0.5
ATT_SCALE = ATT_HD ** -0.5

ADAM_LR = 0.001
ADAM_B1 = 0.9
ADAM_B2 = 0.999
ADAM_EPS = 1e-08
ADAM_WD = 0.01
ADAM_STEP = 10

N_DEV = 8
LANE = 128
GA_ORIG = 2 * GLA_KW + 2 * D_GLA
OFF_AQ = GA_ORIG
OFF_GA = GA_ORIG + 4 * D_ATT
D_ZP = OFF_GA + LANE
QB = 2 * CHUNK
ATT_UNROLL = 16
WIN = (LEFT_CHUNKS + 2) * CHUNK
ET_ROWS = WIN + LEFT_CHUNKS * CHUNK
NEG = float("-inf")
VMEM_LIMIT = 48 * 1024 * 1024


def _cparams(sem):
    return pltpu.CompilerParams(dimension_semantics=sem, vmem_limit_bytes=VMEM_LIMIT)


def _dot(a, b):
    return jnp.dot(a, b, preferred_element_type=F32)


def _dot_nt(a, b):
    return lax.dot_general(a, b, (((1,), (1,)), ((), ())), preferred_element_type=F32)


def _dot_tn(a, b):
    return lax.dot_general(a, b, (((0,), (0,)), ((), ())), preferred_element_type=F32)


def _dot01(t, x, left=True):
    if not left:
        t, x = x, t
    hi = x.astype(BF16)
    r = x - hi.astype(F32)
    mid = r.astype(BF16)
    lo = (r - mid.astype(F32)).astype(BF16)
    if left:
        return _dot(t, hi) + _dot(t, mid) + _dot(t, lo)
    return _dot(hi, t) + _dot(mid, t) + _dot(lo, t)


def _sigmoid(x):
    return 1.0 / (1.0 + jnp.exp(-x))


def _log_sigmoid(x):
    return jnp.minimum(x, 0.0) - jnp.log(1.0 + jnp.exp(-jnp.abs(x)))


TILES = {
    "in_proj": (512, D_ZP // 3, None),
    "in_proj_dx": (512, 512, None),
    "in_proj_dw": (512, 2048, None),
    "out_proj": (512, 1024, None),
    "out_proj_dx": (512, 1024, None),
    "out_proj_dw": (1024, 1024, None),
}


def _matmul(a, b, mode, out_dtype, tm, tn, tk, name, n_outer=False, after=None):
    if mode == "nn":
        (m, k), n = a.shape, b.shape[1]
    elif mode == "nt":
        (m, k), n = a.shape, b.shape[0]
    else:
        (k, m), n = a.shape, b.shape[1]
    tm, tn, tk = min(tm, m), min(tn, n), k if tk is None else min(tk, k)
    assert m % tm == 0 and n % tn == 0 and k % tk == 0, (name, m, n, k)
    nk = k // tk
    dot = {"nn": _dot, "nt": _dot_nt, "tn": _dot_tn}[mode]

    follows = [] if after is None else [after]

    def body_whole_k(a_ref, b_ref, *rest):
        o_ref = rest[-1]
        o_ref[...] = dot(a_ref[...], b_ref[...]).astype(out_dtype)

    def body(a_ref, b_ref, *rest):
        o_ref, acc_ref = rest[-2:]
        kk = pl.program_id(2)

        @pl.when(kk == 0)
        def _():
            acc_ref[...] = jnp.zeros_like(acc_ref)

        acc_ref[...] += dot(a_ref[...], b_ref[...])

        @pl.when(kk == nk - 1)
        def _():
            o_ref[...] = acc_ref[...].astype(out_dtype)

    def at(index):
        return (lambda j, i, kk: index(i, j, kk)) if n_outer else index

    if mode == "tn":
        a_spec = pl.BlockSpec((tk, tm), at(lambda i, j, kk: (kk, i)))
    else:
        a_spec = pl.BlockSpec((tm, tk), at(lambda i, j, kk: (i, kk)))
    if mode == "nt":
        b_spec = pl.BlockSpec((tn, tk), at(lambda i, j, kk: (j, kk)))
    else:
        b_spec = pl.BlockSpec((tk, tn), at(lambda i, j, kk: (kk, j)))
    return pl.pallas_call(
        body_whole_k if nk == 1 else body, name=name,
        grid=(n // tn, m // tm, nk) if n_outer else (m // tm, n // tn, nk),
        in_specs=[a_spec, b_spec] + [ANY] * len(follows),
        out_specs=pl.BlockSpec((tm, tn), at(lambda i, j, kk: (i, j))),
        out_shape=jax.ShapeDtypeStruct((m, n), out_dtype),
        scratch_shapes=[] if nk == 1 else [pltpu.VMEM((tm, tn), F32)],
        compiler_params=_cparams(("parallel", "parallel", "arbitrary")),
    )(a, b, *follows)


def _matmul_cols(pieces, b, out_dtype, tm, tn, name, after=None):
    m, n = pieces[0].shape[0], b.shape[1]
    widths = [p.shape[1] for p in pieces]
    starts = [sum(widths[:i]) for i in range(len(pieces))]
    follows = [] if after is None else [after]
    tm, tn = min(tm, m), min(tn, n)
    assert sum(widths) == b.shape[0] and m % tm == 0 and n % tn == 0, name

    def body(*refs):
        b_ref, o_ref = refs[len(pieces)], refs[-1]
        acc = None
        for p_ref, at, width in zip(refs, starts, widths):
            part = _dot(p_ref[...], b_ref[at:at + width, :])
            acc = part if acc is None else acc + part
        o_ref[...] = acc.astype(out_dtype)

    return pl.pallas_call(
        body, name=name, grid=(n // tn, m // tm),
        in_specs=[pl.BlockSpec((tm, width), lambda j, i: (i, 0)) for width in widths]
        + [pl.BlockSpec((b.shape[0], tn), lambda j, i: (0, j))] + [ANY] * len(follows),
        out_specs=pl.BlockSpec((tm, tn), lambda j, i: (i, j)),
        out_shape=jax.ShapeDtypeStruct((m, n), out_dtype),
        compiler_params=_cparams(("parallel", "parallel")),
    )(*pieces, b, *follows)


def _matmul_rows(pieces, b, out_dtype, tw, tn, name):
    k, n = b.shape
    tn = min(tn, n)
    counts = [p.shape[1] // tw for p in pieces]
    firsts = [sum(counts[:i]) for i in range(len(pieces))]
    assert all(p.shape[1] % tw == 0 for p in pieces) and n % tn == 0, name

    def body(*refs):
        b_ref, o_ref = refs[len(pieces):]
        for p_ref, first, count in zip(refs, firsts, counts):
            @pl.when((pl.program_id(0) >= first) & (pl.program_id(0) < first + count))
            def _(p_ref=p_ref):
                o_ref[...] = _dot_tn(p_ref[...], b_ref[...]).astype(out_dtype)

    def piece_spec(first, count):
        return pl.BlockSpec((k, tw), lambda i, j: (0, jnp.clip(i - first, 0, count - 1)))

    return pl.pallas_call(
        body, name=name, grid=(sum(counts), n // tn),
        in_specs=[piece_spec(first, count) for first, count in zip(firsts, counts)]
        + [pl.BlockSpec((k, tn), lambda i, j: (0, j))],
        out_specs=pl.BlockSpec((tw, tn), lambda i, j: (i, j)),
        out_shape=jax.ShapeDtypeStruct((sum(counts) * tw, n), out_dtype),
        compiler_params=_cparams(("parallel", "parallel")),
    )(*pieces, b)


ROWS = 512


def _rms_fwd(x, g):
    s, d = x.shape

    def body(x_ref, g_ref, h_ref):
        xv = x_ref[...]
        r = lax.rsqrt(jnp.mean(xv * xv, axis=-1, keepdims=True) + EPS)
        h_ref[...] = (xv * r * g_ref[...]).astype(BF16)

    return pl.pallas_call(
        body, name="rms_fwd", grid=(s // ROWS,),
        in_specs=[pl.BlockSpec((ROWS, d), lambda i: (i, 0)), pl.BlockSpec((1, d), lambda i: (0, 0))],
        out_specs=pl.BlockSpec((ROWS, d), lambda i: (i, 0)),
        out_shape=jax.ShapeDtypeStruct((s, d), BF16),
        compiler_params=_cparams(("parallel",)),
    )(x, g)


def _post_fwd(x, y, g, g_next=None):
    s, d = x.shape

    def body(x_ref, y_ref, g_ref, *rest):
        yv = y_ref[...]
        r = lax.rsqrt(jnp.mean(yv * yv, axis=-1, keepdims=True) + EPS)
        out = x_ref[...] + yv * r * g_ref[...]
        if g_next is None:
            rest[0][...] = out
            return
        gn_ref, o_ref, h_ref = rest
        o_ref[...] = out
        rn = lax.rsqrt(jnp.mean(out * out, axis=-1, keepdims=True) + EPS)
        h_ref[...] = (out * rn * gn_ref[...]).astype(BF16)

    row = pl.BlockSpec((ROWS, d), lambda i: (i, 0))
    vec = pl.BlockSpec((1, d), lambda i: (0, 0))
    both = g_next is not None
    res = pl.pallas_call(
        body, name="post_fwd", grid=(s // ROWS,),
        in_specs=[row, row, vec] + [vec] * both,
        out_specs=[row] + [row] * both,
        out_shape=[jax.ShapeDtypeStruct((s, d), F32)] + [jax.ShapeDtypeStruct((s, d), BF16)] * both,
        compiler_params=_cparams(("parallel",)),
    )(x, y, g, *([g_next] * both))
    return (res[0], res[1]) if both else (res[0], None)


def _loss_head(out, tgt):
    s, d = out.shape

    def body(o_ref, t_ref, dout_ref, sum_ref):
        @pl.when(pl.program_id(0) == 0)
        def _():
            sum_ref[...] = jnp.zeros_like(sum_ref)

        e = o_ref[...] - t_ref[...]
        dout_ref[...] = e * (1.0 / d)
        sum_ref[...] += jnp.sum(jnp.sum(e * e, axis=1, keepdims=True), axis=0, keepdims=True)

    row = pl.BlockSpec((ROWS, d), lambda i: (i, 0))
    return pl.pallas_call(
        body, name="loss_head", grid=(s // ROWS,),
        in_specs=[row, row],
        out_specs=[row, pl.BlockSpec((1, 1), lambda i: (0, 0))],
        out_shape=[jax.ShapeDtypeStruct((s, d), F32), jax.ShapeDtypeStruct((1, 1), F32)],
        compiler_params=_cparams(("arbitrary",)),
    )(out, tgt)


def _post_bwd(dout, y, g):
    s, d = y.shape

    def body(do_ref, y_ref, g_ref, dy_ref, dg_ref):
        @pl.when(pl.program_id(0) == 0)
        def _():
            dg_ref[...] = jnp.zeros_like(dg_ref)

        yv = y_ref[...]
        dv = do_ref[...]
        r = lax.rsqrt(jnp.mean(yv * yv, axis=-1, keepdims=True) + EPS)
        dg_ref[...] += jnp.sum(dv * yv * r, axis=0, keepdims=True)
        w = dv * g_ref[...]
        dy = r * (w - yv * (r * r) * jnp.mean(w * yv, axis=-1, keepdims=True))
        dy_ref[...] = dy.astype(BF16)

    row = pl.BlockSpec((ROWS, d), lambda i: (i, 0))
    vec = pl.BlockSpec((1, d), lambda i: (0, 0))
    return pl.pallas_call(
        body, name="post_bwd", grid=(s // ROWS,),
        in_specs=[row, row, vec],
        out_specs=[row, vec],
        out_shape=[jax.ShapeDtypeStruct((s, d), BF16), jax.ShapeDtypeStruct((1, d), F32)],
        compiler_params=_cparams(("arbitrary",)),
    )(dout, y, g)


def _pre_bwd(dh, x, g, dout, below=None):
    s, d = x.shape
    rows = ROWS if below is None else ROWS // 2

    def body(dh_ref, x_ref, g_ref, do_ref, *rest):
        if below is None:
            dx_ref, dg_ref = rest
            sums = (dg_ref,)
        else:
            y_ref, gq_ref, dx_ref, dg_ref, dy_ref, dgq_ref = rest
            sums = (dg_ref, dgq_ref)

        @pl.when(pl.program_id(0) == 0)
        def _():
            for ref in sums:
                ref[...] = jnp.zeros_like(ref)

        xv = x_ref[...]
        dv = dh_ref[...]
        r = lax.rsqrt(jnp.mean(xv * xv, axis=-1, keepdims=True) + EPS)
        w = dv * g_ref[...]
        dx = do_ref[...] + r * (w - xv * (r * r) * jnp.mean(w * xv, axis=-1, keepdims=True))
        if below is not None:
            yv = y_ref[...]
            ry = lax.rsqrt(jnp.mean(yv * yv, axis=-1, keepdims=True) + EPS)
            dgq_ref[...] += jnp.sum(dx * yv * ry, axis=0, keepdims=True)
            wy = dx * gq_ref[...]
            dy_ref[...] = (ry * (wy - yv * (ry * ry) * jnp.mean(wy * yv, axis=-1, keepdims=True))).astype(BF16)
        dg_ref[...] += jnp.sum(dv * xv * r, axis=0, keepdims=True)
        dx_ref[...] = dx

    row = pl.BlockSpec((rows, d), lambda i: (i, 0))
    vec = pl.BlockSpec((1, d), lambda i: (0, 0))
    fused = below is not None
    return pl.pallas_call(
        body, name="pre_bwd", grid=(s // rows,),
        in_specs=[row, row, vec, row] + [row, vec] * fused,
        out_specs=[row, vec] + [row, vec] * fused,
        out_shape=[jax.ShapeDtypeStruct((s, d), F32), jax.ShapeDtypeStruct((1, d), F32)]
        + [jax.ShapeDtypeStruct((s, d), BF16), jax.ShapeDtypeStruct((1, d), F32)] * fused,
        compiler_params=_cparams(("arbitrary",)),
    )(dh, x, g, dout, *(below if fused else ()))


GLA_STEP = 4
GLA_ROWS = GLA_STEP * CHUNK
GLA_CHUNKS = [slice(c * CHUNK, (c + 1) * CHUNK) for c in range(GLA_STEP)]


def _chunk_triangles():
    ri = lax.broadcasted_iota(jnp.int32, (GLA_ROWS, GLA_ROWS), 0)
    ci = lax.broadcasted_iota(jnp.int32, (GLA_ROWS, GLA_ROWS), 1)
    same = (ri // CHUNK) == (ci // CHUNK)
    return (jnp.where(same & (ri >= ci), 1.0, 0.0).astype(BF16), jnp.where(same & (ci >= ri), 1.0, 0.0).astype(BF16))


def _per_chunk(fn, like):
    row = lax.broadcasted_iota(jnp.int32, like.shape, 0)
    return [fn((row >= c * CHUNK) & (row < (c + 1) * CHUNK)) for c in range(GLA_STEP)]


def _spread(per_chunk, like):
    row = lax.broadcasted_iota(jnp.int32, like.shape, 0)
    out = per_chunk[-1]
    for c in reversed(range(GLA_STEP - 1)):
        out = jnp.where(row < (c + 1) * CHUNK, per_chunk[c], out)
    return out


def _gla_gate(ga_b, wa_b, b_ref, tri):
    pre = _dot(ga_b, wa_b) + b_ref[...]
    la = _log_sigmoid(pre) * (1.0 / GLA_TAU)
    return pre, _dot01(tri, la)


def _chunk_ends(cum):
    row = lax.broadcasted_iota(jnp.int32, cum.shape, 0)
    return [jnp.sum(jnp.where(row == (c + 1) * CHUNK - 1, cum, 0.0), axis=0, keepdims=True)
            for c in range(GLA_STEP)]


def _heads(width):
    return [slice(h * width, (h + 1) * width) for h in range(GLA_HEADS)]


def _z_specs_gla(rev=None):
    idx = (lambda n: n) if rev is None else rev
    return [
        pl.BlockSpec((GLA_ROWS, GLA_KW), lambda n: (idx(n), 0)),
        pl.BlockSpec((GLA_ROWS, GLA_KW), lambda n: (idx(n), 1)),
        pl.BlockSpec((GLA_ROWS, D_GLA), lambda n: (idx(n), 1)),
        pl.BlockSpec((GLA_ROWS, D_GLA), lambda n: (idx(n), 2)),
        pl.BlockSpec((GLA_ROWS, LANE), lambda n: (idx(n), OFF_GA // LANE)),
    ]


def _gla_fwd(z, wa_pad, b_alpha, g_gla):
    s = z.shape[0]
    nchunk = s // CHUNK

    def body(q_ref, k_ref, v_ref, gg_ref, ga_ref, wa_ref, b_ref, g_ref, y_ref, o_ref, st_ref, pre_ref, cum_ref,
             state):
        @pl.when(pl.program_id(0) == 0)
        def _():
            state[...] = jnp.zeros_like(state)

        ga_b = ga_ref[...].astype(BF16)
        tri, _ = _chunk_triangles()
        nh = range(GLA_HEADS)
        keys, vals = _heads(GLA_DK), _heads(GLA_DV)
        pre, cum = _gla_gate(ga_b, wa_ref[...].astype(BF16), b_ref, tri)
        pre_ref[...] = pre
        cum_ref[...] = cum
        cends = _chunk_ends(cum)
        kd_b = (k_ref[...] * jnp.exp(_spread(cends, cum) - cum)).astype(BF16)
        qs = (q_ref[...] * GLA_SCALE).astype(BF16)
        v_b = v_ref[...].astype(BF16)
        uts = [[_dot_tn(v_b[rs, vals[h]], kd_b[rs, keys[h]]) for h in nh] for rs in GLA_CHUNKS]
        sts, prev = [], [state[h] for h in nh]
        for c in range(GLA_STEP):
            a = jnp.exp(cends[c])
            prev = [prev[h] * a[:, keys[h]] + uts[c][h] for h in nh]
            sts.append(prev)
        for h in nh:
            state[h] = prev[h]
            for c in range(GLA_STEP):
                st_ref[c, h] = sts[c][h]
        outs = [[_dot_nt(qs[rs, keys[h]], sts[c][h].astype(BF16)) for h in nh] for c, rs in enumerate(GLA_CHUNKS)]
        for h in nh:
            o, vs = jnp.concatenate([outs[c][h] for c in range(GLA_STEP)], axis=0), vals[h]
            o_ref[:, vs] = o
            r = lax.rsqrt(jnp.mean(o * o, axis=-1, keepdims=True) + EPS)
            gg = gg_ref[:, vs]
            y_ref[:, vs] = (o * r * g_ref[:, vs] * (gg * _sigmoid(gg))).astype(BF16)

    full = lambda shape: pl.BlockSpec(shape, lambda n: tuple(0 for _ in shape))
    wide = pl.BlockSpec((GLA_ROWS, D_GLA), lambda n: (n, 0))
    return pl.pallas_call(
        body, name="gla_fwd", grid=(nchunk // GLA_STEP,),
        in_specs=_z_specs_gla() + [full((LANE, GLA_KW)), full((1, GLA_KW)), full((1, D_GLA))],
        out_specs=[wide, wide, pl.BlockSpec((GLA_STEP, GLA_HEADS, GLA_DV, GLA_DK), lambda n: (n, 0, 0, 0)),
                   pl.BlockSpec((GLA_ROWS, GLA_KW), lambda n: (n, 0)), pl.BlockSpec((GLA_ROWS, GLA_KW), lambda n: (n, 0))],
        out_shape=[jax.ShapeDtypeStruct((s, D_GLA), BF16), jax.ShapeDtypeStruct((s, D_GLA), F32),
                   jax.ShapeDtypeStruct((nchunk, GLA_HEADS, GLA_DV, GLA_DK), F32),
                   jax.ShapeDtypeStruct((s, GLA_KW), F32), jax.ShapeDtypeStruct((s, GLA_KW), F32)],
        scratch_shapes=[pltpu.VMEM((GLA_HEADS, GLA_DV, GLA_DK), F32)],
        compiler_params=_cparams(("arbitrary",)),
    )(z, z, z, z, z, wa_pad, b_alpha, g_gla)


def _gla_bwd(dyc, o_gla, z, wa_pad, g_gla, states, gate_pre, gate_cum):
    s = z.shape[0]
    nsteps = s // GLA_ROWS
    rev = lambda n: nsteps - 1 - n

    def body(dy_ref, o_ref, q_ref, k_ref, v_ref, gg_ref, ga_ref, wa_ref, g_ref, st_ref, stp_ref, pre_ref, cum_ref,
             dq_ref, dk_ref, dv_ref, dgg_ref, dga_ref, dwa_ref, db_ref, dg_ref, carry):
        step = pl.program_id(0)

        @pl.when(step == 0)
        def _():
            carry[...] = jnp.zeros_like(carry)
            dwa_ref[...] = jnp.zeros_like(dwa_ref)
            db_ref[...] = jnp.zeros_like(db_ref)
            dg_ref[...] = jnp.zeros_like(dg_ref)

        has_prev = (step < nsteps - 1).astype(F32)
        ga_b = ga_ref[...].astype(BF16)
        _, tri_up = _chunk_triangles()
        nh, nc = range(GLA_HEADS), range(GLA_STEP)
        keys, vals = _heads(GLA_DK), _heads(GLA_DV)
        wa_b = wa_ref[...].astype(BF16)
        pre, cum = pre_ref[...], cum_ref[...]
        cends = _chunk_ends(cum)
        e = jnp.exp(_spread(cends, cum) - cum)
        a = [jnp.exp(cends[c]) for c in nc]
        kf = k_ref[...]
        kd_b = (kf * e).astype(BF16)
        v_b = v_ref[...].astype(BF16)
        qs = (q_ref[...] * GLA_SCALE).astype(BF16)
        do_b = []
        for h in nh:
            vs = vals[h]
            o = o_ref[:, vs]
            gg = gg_ref[:, vs]
            g = g_ref[:, vs]
            dy = dy_ref[:, vs]
            r = lax.rsqrt(jnp.mean(o * o, axis=-1, keepdims=True) + EPS)
            sg = _sigmoid(gg)
            dogn = dy * (gg * sg)
            dgg_ref[:, vs] = (dy * (o * r * g) * (sg * (1.0 + gg * (1.0 - sg)))).astype(BF16)
            dg_ref[:, vs] += jnp.sum(dogn * o * r, axis=0, keepdims=True)
            w = dogn * g
            do_b.append((r * (w - o * (r * r) * jnp.mean(w * o, axis=-1, keepdims=True))).astype(BF16))
        dqs = [jnp.concatenate([_dot(do_b[h][rs], st_ref[c, h].astype(BF16)) for c, rs in enumerate(GLA_CHUNKS)],
                               axis=0) for h in nh]
        dq_ref[...] = (jnp.concatenate(dqs, axis=1) * GLA_SCALE).astype(BF16)
        own = [[_dot_tn(do_b[h][rs], qs[rs, keys[h]]) for h in nh] for rs in GLA_CHUNKS]
        gts, later = [None] * GLA_STEP, [carry[h] for h in nh]
        for c in reversed(nc):
            gts[c] = [own[c][h] + later[h] for h in nh]
            later = [gts[c][h] * a[c][:, keys[h]] for h in nh]
        for h in nh:
            carry[h] = later[h]
        gt_b = [[gts[c][h].astype(BF16) for h in nh] for c in nc]
        dkd = jnp.concatenate([jnp.concatenate([_dot(v_b[rs, vals[h]], gt_b[c][h]) for h in nh], axis=1)
                               for c, rs in enumerate(GLA_CHUNKS)], axis=0)
        dvs = [[_dot_nt(kd_b[rs, keys[h]], gt_b[c][h]) for h in nh] for c, rs in enumerate(GLA_CHUNKS)]
        before = lambda c, h: st_ref[c - 1, h] if c > 0 else stp_ref[0, h] * has_prev
        da = [jnp.concatenate([jnp.sum(gts[c][h] * before(c, h), axis=0, keepdims=True) for h in nh], axis=1)
              for c in nc]
        for h in nh:
            dv_ref[:, vals[h]] = jnp.concatenate([dvs[c][h] for c in nc], axis=0).astype(BF16)
        dk_ref[...] = (dkd * e).astype(BF16)
        dd = dkd * kf * e
        dsum = _per_chunk(lambda mine: jnp.sum(jnp.where(mine, dd, 0.0), axis=0, keepdims=True), dd)
        dcend = _spread([dsum[c] + da[c] * a[c] for c in nc], dd)
        dla = dcend - _dot01(tri_up, dd)
        dpre = dla * (1.0 / GLA_TAU) * (1.0 - _sigmoid(pre))
        dpre_b = dpre.astype(BF16)
        dga_ref[...] = _dot_nt(dpre_b, wa_b).astype(BF16)
        dwa_ref[...] += _dot_tn(ga_b, dpre_b)
        db_ref[...] += jnp.sum(dpre, axis=0, keepdims=True)

    full = lambda shape: pl.BlockSpec(shape, lambda n: tuple(0 for _ in shape))
    wide = pl.BlockSpec((GLA_ROWS, D_GLA), lambda n: (rev(n), 0))
    keyw = pl.BlockSpec((GLA_ROWS, GLA_KW), lambda n: (rev(n), 0))
    st_spec = pl.BlockSpec((GLA_STEP, GLA_HEADS, GLA_DV, GLA_DK), lambda n: (rev(n), 0, 0, 0))
    stp_spec = pl.BlockSpec((1, GLA_HEADS, GLA_DV, GLA_DK),
                            lambda n: (jnp.maximum(GLA_STEP * rev(n) - 1, 0), 0, 0, 0))
    return pl.pallas_call(
        body, name="gla_bwd", grid=(nsteps,),
        in_specs=[wide, wide] + _z_specs_gla(rev)
        + [full((LANE, GLA_KW)), full((1, D_GLA)), st_spec, stp_spec, keyw, keyw],
        out_specs=[keyw, keyw, wide, wide, pl.BlockSpec((GLA_ROWS, LANE), lambda n: (rev(n), 0)),
                   full((LANE, GLA_KW)), full((1, GLA_KW)), full((1, D_GLA))],
        out_shape=[jax.ShapeDtypeStruct((s, GLA_KW), BF16), jax.ShapeDtypeStruct((s, GLA_KW), BF16),
                   jax.ShapeDtypeStruct((s, D_GLA), BF16), jax.ShapeDtypeStruct((s, D_GLA), BF16),
                   jax.ShapeDtypeStruct((s, LANE), BF16),
                   jax.ShapeDtypeStruct((LANE, GLA_KW), F32), jax.ShapeDtypeStruct((1, GLA_KW), F32),
                   jax.ShapeDtypeStruct((1, D_GLA), F32)],
        scratch_shapes=[pltpu.VMEM((GLA_HEADS, GLA_DV, GLA_DK), F32)],
        compiler_params=_cparams(("arbitrary",)),
    )(dyc, o_gla, z, z, z, z, z, wa_pad, g_gla, states, states, gate_pre, gate_cum)


def _build_bias_table(rb_row, et_ref):
    far = jnp.broadcast_to(rb_row[:, 2 * REL_CLIP:2 * REL_CLIP + 1], (1, LANE))
    near_hi = rb_row[:, REL_CLIP:2 * REL_CLIP]
    near_lo = rb_row[:, 0:REL_CLIP]
    past = jnp.broadcast_to(rb_row[:, 0:1], (1, LANE))
    seg = [far, far, far, far, near_hi, near_lo] + [past] * (ET_ROWS // LANE - 5)
    ri = lax.broadcasted_iota(jnp.int32, (LANE, LANE), 0)
    ci = lax.broadcasted_iota(jnp.int32, (LANE, LANE), 1)
    for kb in range(ET_ROWS // LANE):
        wmat = jnp.where(ri + ci < LANE, seg[kb], seg[kb + 1])
        blk = pltpu.roll(wmat, 0, 1, stride=1, stride_axis=0)
        lag = LEFT_CHUNKS + ci // CHUNK - (2 * kb + ri // CHUNK)
        et_ref[kb * LANE:(kb + 1) * LANE, :] = jnp.where((lag >= 0) & (lag <= LEFT_CHUNKS), blk, NEG)


def _reduce_bias_table(det_ref):
    lane = lax.broadcasted_iota(jnp.int32, (1, LANE), 1)
    ri = lax.broadcasted_iota(jnp.int32, (LANE, LANE), 0)
    ci = lax.broadcasted_iota(jnp.int32, (LANE, LANE), 1)
    flip = jnp.where(ri + ci == LANE - 1, 1.0, 0.0).astype(BF16)
    segs = jnp.zeros((8, LANE), F32)
    seg_row = lax.broadcasted_iota(jnp.int32, (8, LANE), 0)
    prev_minus = jnp.zeros((1, LANE), F32)
    for kb in range(6):
        rolled = pltpu.roll(_dot01(det_ref[kb * LANE:(kb + 1) * LANE, :], flip, left=False), 0, 1,
                            stride=1, stride_axis=0)
        plus = jnp.sum(jnp.where(ci >= ri, rolled, 0.0), axis=0, keepdims=True)
        minus = jnp.sum(jnp.where(ci < ri, rolled, 0.0), axis=0, keepdims=True)
        segs = segs + jnp.where(seg_row == kb, plus + prev_minus, 0.0)
        prev_minus = minus
    segs = _dot01(segs, flip, left=False)
    pick = lambda kb: jnp.sum(jnp.where(seg_row == kb, segs, 0.0), axis=0, keepdims=True)
    far = jnp.sum(pick(0) + pick(1) + pick(2) + pick(3), axis=1, keepdims=True)
    last = jnp.where(lane == 0, far, 0.0)
    return jnp.concatenate([pick(5), pick(4), last], axis=1)


def _att_window(b):
    c0 = 2 * b
    kstart = pl.multiple_of(jnp.maximum(c0 - LEFT_CHUNKS, 0) * CHUNK, CHUNK)
    eoff = pl.multiple_of(jnp.maximum(LEFT_CHUNKS - c0, 0) * CHUNK, CHUNK)
    return kstart, eoff


def _att_probs(q_b, kw_b, et):
    st = _dot_nt(kw_b, q_b) * ATT_SCALE + et
    m = jnp.max(st, axis=0, keepdims=True)
    ex = jnp.exp(st - m)
    return ex * (1.0 / jnp.sum(ex, axis=0, keepdims=True))


def _att_fwd(z, rb_pad, g_att):
    s = z.shape[0]
    nblk = s // QB
    c_aq, c_ak, c_av, c_ag = [(OFF_AQ + i * D_ATT) // ATT_HD for i in range(4)]

    def body(q_ref, k_ref, v_ref, ag_ref, rb_ref, g_ref, y_ref, o_ref, p_ref, et_ref, kb_ref, vb_ref):
        h = pl.program_id(0)
        b = pl.program_id(1)

        @pl.when(b == 0)
        def _():
            _build_bias_table(rb_ref[pl.ds(h, 1), :], et_ref)
            kb_ref[...] = k_ref[...].astype(BF16)
            vb_ref[...] = v_ref[...].astype(BF16)

        for j in range(ATT_UNROLL):
            rs = slice(j * QB, (j + 1) * QB)
            kstart, eoff = _att_window(b * ATT_UNROLL + j)
            q_b = q_ref[rs, :].astype(BF16)
            kw_b = kb_ref[pl.ds(kstart, WIN), :]
            vw_b = vb_ref[pl.ds(kstart, WIN), :]
            pt = _att_probs(q_b, kw_b, et_ref[pl.ds(eoff, WIN), :])
            p_ref[0, j] = pt
            o = _dot_tn(pt.astype(BF16), vw_b)
            o_ref[rs, :] = o
            r = lax.rsqrt(jnp.mean(o * o, axis=-1, keepdims=True) + EPS)
            ag = ag_ref[rs, :]
            y_ref[rs, :] = (o * r * g_ref[...] * (ag * _sigmoid(ag))).astype(BF16)

    blk = lambda col: pl.BlockSpec((ATT_UNROLL * QB, ATT_HD), lambda h, b: (b, col + h))
    seq = lambda col: pl.BlockSpec((s, ATT_HD), lambda h, b: (0, col + h))
    out_blk = pl.BlockSpec((ATT_UNROLL * QB, ATT_HD), lambda h, b: (b, h))
    return pl.pallas_call(
        body, name="att_fwd", grid=(ATT_HEADS, nblk // ATT_UNROLL),
        in_specs=[blk(c_aq), seq(c_ak), seq(c_av), blk(c_ag),
                  pl.BlockSpec((ATT_HEADS, 3 * LANE), lambda h, b: (0, 0)),
                  pl.BlockSpec((1, ATT_HD), lambda h, b: (0, h))],
        out_specs=[out_blk, out_blk, pl.BlockSpec((1, ATT_UNROLL, WIN, QB), lambda h, b: (h, b, 0, 0))],
        out_shape=[jax.ShapeDtypeStruct((s, D_ATT), BF16), jax.ShapeDtypeStruct((s, D_ATT), F32),
                   jax.ShapeDtypeStruct((ATT_HEADS, nblk, WIN, QB), F32)],
        scratch_shapes=[pltpu.VMEM((ET_ROWS, LANE), F32), pltpu.VMEM((s, ATT_HD), BF16),
                        pltpu.VMEM((s, ATT_HD), BF16)],
        compiler_params=_cparams(("arbitrary", "arbitrary")),
    )(z, z, z, z, rb_pad, g_att)


def _att_bwd(dyc, o_att, probs, z, g_att):
    s = z.shape[0]
    nblk = s // QB
    c_aq, c_ak, c_av, c_ag = [(OFF_AQ + i * D_ATT) // ATT_HD for i in range(4)]
    c_dy = D_GLA // ATT_HD

    def body(dy_ref, o_ref, p_ref, q_ref, k_ref, v_ref, ag_ref, g_ref,
             dq_ref, dk_ref, dv_ref, dag_ref, drb_ref, dg_ref, det_ref, kb_ref, vb_ref, dk_acc, dv_acc):
        b = pl.program_id(1)

        @pl.when(b == 0)
        def _():
            kb_ref[...] = k_ref[...].astype(BF16)
            vb_ref[...] = v_ref[...].astype(BF16)
            det_ref[...] = jnp.zeros_like(det_ref)
            dk_acc[...] = jnp.zeros_like(dk_acc)
            dv_acc[...] = jnp.zeros_like(dv_acc)
            dg_ref[...] = jnp.zeros_like(dg_ref)

        g = g_ref[...]
        dg = jnp.zeros((1, ATT_HD), F32)
        for j in range(ATT_UNROLL):
            rs = slice(j * QB, (j + 1) * QB)
            kstart, eoff = _att_window(b * ATT_UNROLL + j)
            q_b = q_ref[rs, :].astype(BF16)
            kw_b = kb_ref[pl.ds(kstart, WIN), :]
            vw_b = vb_ref[pl.ds(kstart, WIN), :]
            pt = p_ref[0, j]
            o = o_ref[rs, :]
            ag = ag_ref[rs, :]
            dy = dy_ref[rs, :]
            r = lax.rsqrt(jnp.mean(o * o, axis=-1, keepdims=True) + EPS)
            sg = _sigmoid(ag)
            don = dy * (ag * sg)
            dag_ref[rs, :] = (dy * (o * r * g) * (sg * (1.0 + ag * (1.0 - sg)))).astype(BF16)
            dg = dg + jnp.sum(don * o * r, axis=0, keepdims=True)
            w = don * g
            do_b = (r * (w - o * (r * r) * jnp.mean(w * o, axis=-1, keepdims=True))).astype(BF16)
            pt_b = pt.astype(BF16)
            dpt = _dot_nt(vw_b, do_b)
            dst = pt * (dpt - jnp.sum(dpt * pt, axis=0, keepdims=True))
            det_ref[pl.ds(eoff, WIN), :] += dst
            ds_b = (dst * ATT_SCALE).astype(BF16)
            dq_ref[rs, :] = _dot_tn(ds_b, kw_b).astype(BF16)
            dk_acc[pl.ds(kstart, WIN), :] += _dot(ds_b, q_b)
            dv_acc[pl.ds(kstart, WIN), :] += _dot(pt_b, do_b)
        dg_ref[...] += dg

        @pl.when(b == nblk // ATT_UNROLL - 1)
        def _():
            drb_ref[0] = jnp.broadcast_to(_reduce_bias_table(det_ref), (8, 3 * LANE))
            dk_ref[...] = dk_acc[...].astype(BF16)
            dv_ref[...] = dv_acc[...].astype(BF16)

    blk = lambda col: pl.BlockSpec((ATT_UNROLL * QB, ATT_HD), lambda h, b: (b, col + h))
    seq = lambda col: pl.BlockSpec((s, ATT_HD), lambda h, b: (0, col + h))
    out_blk = pl.BlockSpec((ATT_UNROLL * QB, ATT_HD), lambda h, b: (b, h))
    out_seq = pl.BlockSpec((s, ATT_HD), lambda h, b: (0, h))
    return pl.pallas_call(
        body, name="att_bwd", grid=(ATT_HEADS, nblk // ATT_UNROLL),
        in_specs=[blk(c_dy), blk(0), pl.BlockSpec((1, ATT_UNROLL, WIN, QB), lambda h, b: (h, b, 0, 0)),
                  blk(c_aq), seq(c_ak), seq(c_av), blk(c_ag),
                  pl.BlockSpec((1, ATT_HD), lambda h, b: (0, h))],
        out_specs=[out_blk, out_seq, out_seq, out_blk,
                   pl.BlockSpec((1, 8, 3 * LANE), lambda h, b: (h, 0, 0)),
                   pl.BlockSpec((1, ATT_HD), lambda h, b: (0, h))],
        out_shape=[jax.ShapeDtypeStruct((s, D_ATT), BF16), jax.ShapeDtypeStruct((s, D_ATT), BF16),
                   jax.ShapeDtypeStruct((s, D_ATT), BF16), jax.ShapeDtypeStruct((s, D_ATT), BF16),
                   jax.ShapeDtypeStruct((ATT_HEADS, 8, 3 * LANE), F32),
                   jax.ShapeDtypeStruct((1, D_ATT), F32)],
        scratch_shapes=[pltpu.VMEM((ET_ROWS, LANE), F32),
                        pltpu.VMEM((s, ATT_HD), BF16), pltpu.VMEM((s, ATT_HD), BF16),
                        pltpu.VMEM((s, ATT_HD), F32), pltpu.VMEM((s, ATT_HD), F32)],
        compiler_params=_cparams(("arbitrary", "arbitrary")),
    )(dyc, o_att, probs, z, z, z, z, g_att)


ADAM_ROWS = 64
ADAM_COL_ROWS = 32


def _adam_math(w, g, m, v):
    m2 = ADAM_B1 * m + (1.0 - ADAM_B1) * g
    v2 = ADAM_B2 * v + (1.0 - ADAM_B2) * (g * g)
    m_hat = m2 / (1.0 - ADAM_B1 ** ADAM_STEP)
    v_hat = v2 / (1.0 - ADAM_B2 ** ADAM_STEP)
    delta = -ADAM_LR * (m_hat / (jnp.sqrt(v_hat) + ADAM_EPS) + ADAM_WD * w)
    return delta, m2, v2


def _adam_sharded(parts, first, w, m, v, name):
    nl, nr, nc = w.shape

    def body(*refs):
        p_refs = refs[:nl]
        w_ref, m_ref, v_ref, g_ref, d_ref, m2_ref, v2_ref = refs[nl:]
        for k in range(nl):
            @pl.when(pl.program_id(0) == k)
            def _(p_ref=p_refs[k]):
                g = p_ref[0].astype(F32)
                for dev in range(1, N_DEV):
                    g = g + p_ref[dev].astype(F32)
                delta, m2, v2 = _adam_math(w_ref[0], g, m_ref[0], v_ref[0])
                g_ref[0] = g
                d_ref[0] = delta
                m2_ref[0] = m2
                v2_ref[0] = v2

    def part_spec(k):
        return pl.BlockSpec((N_DEV, ADAM_ROWS, nc), lambda l, i: (0, first + jnp.where(l == k, i, 0), 0))

    blk = pl.BlockSpec((1, ADAM_ROWS, nc), lambda l, i: (l, i, 0))
    shp = jax.ShapeDtypeStruct(w.shape, F32)
    return pl.pallas_call(
        body, name=name, grid=(nl, pl.cdiv(nr, ADAM_ROWS)),
        in_specs=[part_spec(k) for k in range(nl)] + [blk, blk, blk],
        out_specs=[blk, blk, blk, blk],
        out_shape=[shp, shp, shp, shp],
        compiler_params=_cparams(("arbitrary", "arbitrary")),
    )(*parts, w, m, v)


def _adam_columns(parts, first, w, m, v):
    nc, nl, d = w.shape

    def body(*refs):
        p_refs = refs[:nl]
        w_ref, m_ref, v_ref, g_ref, d_ref, m2_ref, v2_ref = refs[nl:]
        for l in range(nl):
            g = p_refs[l][0].astype(F32)
            for slot in range(1, parts[l].shape[0]):
                g = g + p_refs[l][slot].astype(F32)
            delta, m2, v2 = _adam_math(w_ref[:, l, :], g, m_ref[:, l, :], v_ref[:, l, :])
            g_ref[:, l, :] = g
            d_ref[:, l, :] = delta
            m2_ref[:, l, :] = m2
            v2_ref[:, l, :] = v2

    blk = pl.BlockSpec((ADAM_COL_ROWS, nl, d), lambda i: (i, 0, 0))
    shp = jax.ShapeDtypeStruct(w.shape, F32)
    return pl.pallas_call(
        body, name="adam_w_in", grid=(pl.cdiv(nc, ADAM_COL_ROWS),),
        in_specs=[pl.BlockSpec((p.shape[0], ADAM_COL_ROWS, d), lambda i: (0, first + i, 0)) for p in parts]
        + [blk, blk, blk],
        out_specs=[blk, blk, blk, blk],
        out_shape=[shp, shp, shp, shp],
        compiler_params=_cparams(("parallel",)),
    )(*parts, w, m, v)


def _adam_small(ws, gs, ms, vs):
    n = len(ws)

    def body(*refs):
        w_refs, g_refs, m_refs, v_refs, d_refs, m2_refs, v2_refs = [refs[i * n:(i + 1) * n] for i in range(7)]
        for i in range(n):
            delta, m2, v2 = _adam_math(w_refs[i][...], g_refs[i][...], m_refs[i][...], v_refs[i][...])
            d_refs[i][...] = delta
            m2_refs[i][...] = m2
            v2_refs[i][...] = v2

    shapes = [jax.ShapeDtypeStruct(w.shape, F32) for w in ws]
    out = pl.pallas_call(body, name="adam_small", out_shape=shapes * 3)(*ws, *gs, *ms, *vs)
    return out[:n], out[n:2 * n], out[2 * n:]


def _position():
    return lax.axis_index("x"), lax.axis_index("y"), lax.axis_index("c")


def _slot(p):
    return 4 * p[0] + 2 * p[1] + p[2]


BF16_TILE_ROWS = 16


def _slab_rows(rows, cols):
    return -(-(rows + cols) // BF16_TILE_ROWS) * BF16_TILE_ROWS


RELAYOUT_COLS = 1024
RELAYOUT_CHUNK = 64


def _shard_pieces(dev, rows, cols):
    moved = ((0, GA_ORIG, 0), (GA_ORIG, GA_ORIG + GLA_RANK, OFF_GA - GA_ORIG), (GA_ORIG + GLA_RANK, D_IN, -GLA_RANK))
    c0, c1 = dev * cols, (dev + 1) * cols
    return [(rows + max(c0, lo) - c0, max(c0, lo) + off, min(c1, hi) - max(c0, lo))
            for lo, hi, off in moved if max(c0, lo) < min(c1, hi)]


def _move_rows(src, src_row, dst, dst_row, n):
    assert src_row % 2 == 0 and dst_row % 2 == 0 and n % 2 == 0
    for r in range(0, n // 2, RELAYOUT_CHUNK):
        m = min(RELAYOUT_CHUNK, n // 2 - r)
        dst[dst_row // 2 + r:dst_row // 2 + r + m, :] = src[src_row // 2 + r:src_row // 2 + r + m, :]


def _aligned_weight(land, rows, cols):
    _, slab, d = land.shape
    ct = min(RELAYOUT_COLS, d)

    def body(land_ref, wt_ref, wo_ref):
        dev = pl.program_id(1)
        src = land_ref.bitcast(jnp.uint32)
        dst = wt_ref.bitcast(jnp.uint32)
        wo_ref[...] = land_ref[0:rows, :]

        @pl.when(dev == 0)
        def _():
            dst[D_IN // 2:D_ZP // 2, :] = jnp.zeros(((D_ZP - D_IN) // 2, ct), jnp.uint32)

        for k in range(N_DEV):
            @pl.when(dev == k)
            def _(k=k):
                for at, to, n in _shard_pieces(k, rows, cols):
                    _move_rows(src, at, dst, to, n)

    return pl.pallas_call(
        body, name="aligned_weight", grid=(d // ct, N_DEV),
        in_specs=[pl.BlockSpec((slab, ct), lambda c, dev: (dev, c))],
        out_specs=[pl.BlockSpec((D_ZP, ct), lambda c, dev: (0, c)),
                   pl.BlockSpec((rows, ct), lambda c, dev: (dev, c))],
        out_shape=[jax.ShapeDtypeStruct((D_ZP, d), land.dtype),
                   jax.ShapeDtypeStruct((N_DEV * rows, d), land.dtype)],
        compiler_params=_cparams(("parallel", "arbitrary")),
    )(land.reshape(N_DEV * slab, d))


def _partial_slabs(dwt, cols, by_core=False):
    d = dwt[0].shape[1]
    bounds = (0, GA_ORIG, OFF_GA, D_ZP)
    assert tuple(a.shape[0] for a in dwt) == tuple(hi - lo for lo, hi in zip(bounds, bounds[1:]))
    slab = _slab_rows(0, cols)
    ct = min(RELAYOUT_COLS, d)

    def body(*refs):
        out_ref = refs[-1]
        dev = pl.program_id(1)
        srcs = [ref.bitcast(jnp.uint32) for ref in refs[:-1]]
        dst = out_ref.bitcast(jnp.uint32)
        dst[cols // 2:slab // 2, :] = jnp.zeros(((slab - cols) // 2, ct), jnp.uint32)
        for k in range(N_DEV):
            @pl.when(dev == k)
            def _(k=k):
                for to, at, n in _shard_pieces(k, 0, cols):
                    which = max(i for i, lo in enumerate(bounds[:-1]) if lo <= at)
                    assert at + n <= bounds[which + 1]
                    _move_rows(srcs[which], at - bounds[which], dst, to, n)

    place = (lambda dev: (dev % 2) * (N_DEV // 2) + dev // 2) if by_core else (lambda dev: dev)
    out = pl.pallas_call(
        body, name="partial_slabs", grid=(d // ct, N_DEV),
        in_specs=[pl.BlockSpec((a.shape[0], ct), lambda c, dev: (0, c)) for a in dwt],
        out_specs=pl.BlockSpec((slab, ct), lambda c, dev: (place(dev), c)),
        out_shape=jax.ShapeDtypeStruct((N_DEV * slab, d), dwt[0].dtype),
        compiler_params=_cparams(("parallel", "arbitrary")),
    )(*dwt)
    return out.reshape((2, N_DEV // 2, slab, d) if by_core else (N_DEV, slab, d))


def _pair_sum(mine, theirs):
    _, nchip, slab, d = mine.shape
    rows = next(r for r in range(512, 0, -BF16_TILE_ROWS) if slab % r == 0)

    def body(m_ref, t_ref, o_ref):
        south = lax.axis_index("c") == 0
        own = jnp.where(south, m_ref[0, 0], m_ref[1, 0]).astype(F32)
        got = jnp.where(south, t_ref[1, 0], t_ref[0, 0]).astype(F32)
        o_ref[0] = (own + got).astype(o_ref.dtype)

    both = pl.BlockSpec((2, 1, rows, d), lambda j, i: (0, j, i, 0))
    return pl.pallas_call(
        body, name="pair_sum", grid=(nchip, slab // rows),
        in_specs=[both, both],
        out_specs=pl.BlockSpec((1, rows, d), lambda j, i: (j, i, 0)),
        out_shape=jax.ShapeDtypeStruct((nchip, slab, d), mine.dtype),
        compiler_params=_cparams(("parallel", "parallel")),
    )(mine, theirs)


def _peer(pos, k):
    x, y, c = pos
    return (1 - x if k & 4 else x, 1 - y if k & 2 else y, 1 - c if k & 1 else c)


HBM_SPEC = pl.BlockSpec(memory_space=pltpu.HBM)
SEM_SPEC = pl.BlockSpec(memory_space=pltpu.SEMAPHORE)
GATHER_PEERS = (1, 4, 2, 6)
ALL_PEERS = (1, 2, 3, 4, 5, 6, 7)


def _hbm(a):
    return pltpu.with_memory_space_constraint(a, pltpu.HBM)


BY_DEVICE = (_slot, N_DEV)
BY_CORE = (lambda p: p[2], 2)
BY_CHIP = (lambda p: 2 * p[0] + p[1], 4)


def _split_copies(src_ref, land_ref, send_sems, recv_sems, ks, per_peer, landed, slots):
    slot_of = slots[0]
    me = _position()
    out = []
    for i, k in enumerate(ks):
        peer = _peer(me, k)
        src = src_ref.at[slot_of(peer)] if per_peer else src_ref
        dst = land_ref.at[slot_of(peer) if landed else slot_of(me)]
        out.append(pltpu.make_async_remote_copy(
            src_ref=src, dst_ref=dst, send_sem=send_sems.at[i], recv_sem=recv_sems.at[i],
            device_id=peer, device_id_type=MESH))
    return out


def _exchange_start(src, after, ks, per_peer, name, slots=BY_DEVICE):
    slab = src.shape[1:] if per_peer else src.shape
    land_shape = (slots[1],) + tuple(slab)
    n = len(ks)

    def body(src_ref, land_ref, after_ref, send_sems, recv_sems, src_thru, land_thru, token):
        for cp in _split_copies(src_ref, land_ref, send_sems, recv_sems, ks, per_peer, False, slots):
            cp.start()
        token[...] = jnp.zeros_like(token)

    return pl.pallas_call(
        body, name=name,
        out_shape=(pltpu.SemaphoreType.DMA((n,)), pltpu.SemaphoreType.DMA((n,)),
                   pltpu.HBM(src.shape, src.dtype), pltpu.HBM(land_shape, src.dtype),
                   jax.ShapeDtypeStruct((8, LANE), F32)),
        in_specs=(HBM_SPEC, HBM_SPEC, ANY),
        out_specs=(SEM_SPEC, SEM_SPEC, HBM_SPEC, HBM_SPEC, pl.BlockSpec(memory_space=pltpu.VMEM)),
        input_output_aliases={0: 2, 1: 3},
        compiler_params=pltpu.CompilerParams(has_side_effects=pltpu.SideEffectType.DATAFLOW_SIDE_EFFECTING),
    )(_hbm(src), _hbm(lax.empty(land_shape, src.dtype)), after)


def _exchange_wait(started, after, ks, per_peer, name, slots=BY_DEVICE):
    send_sems, recv_sems, src_thru, land_thru = started

    def body(src_ref, land_ref, send_sems, recv_sems, after_ref, src_dead, land_out):
        for cp in _split_copies(src_ref, land_ref, send_sems, recv_sems, ks, per_peer, True, slots):
            cp.wait_send()
            cp.wait_recv()

    return pl.pallas_call(
        body, name=name,
        out_shape=(pltpu.HBM(src_thru.shape, src_thru.dtype), pltpu.HBM(land_thru.shape, land_thru.dtype)),
        in_specs=(HBM_SPEC, HBM_SPEC, SEM_SPEC, SEM_SPEC, ANY), out_specs=(HBM_SPEC, HBM_SPEC),
        input_output_aliases={0: 0, 1: 1},
        compiler_params=pltpu.CompilerParams(has_side_effects=pltpu.SideEffectType.DATAFLOW_SIDE_EFFECTING),
    )(src_thru, land_thru, send_sems, recv_sems, after)


def _relay_copies(land_ref, send_sems, recv_sems, landed):
    me = _position()
    sibling = _peer(me, 1)
    out = []
    for i, k in enumerate(GATHER_PEERS[1:]):
        blk = land_ref.at[_slot(_peer(sibling if landed else me, k))]
        out.append(pltpu.make_async_remote_copy(
            src_ref=blk, dst_ref=blk, send_sem=send_sems.at[i], recv_sem=recv_sems.at[i],
            device_id=sibling, device_id_type=MESH))
    return out


def _relay_start(land, name):
    n = len(GATHER_PEERS) - 1

    def body(land_ref, send_sems, recv_sems, land_thru, token):
        for cp in _relay_copies(land_ref, send_sems, recv_sems, landed=False):
            cp.start()
        token[...] = jnp.zeros_like(token)

    return pl.pallas_call(
        body, name=name,
        out_shape=(pltpu.SemaphoreType.DMA((n,)), pltpu.SemaphoreType.DMA((n,)),
                   pltpu.HBM(land.shape, land.dtype), jax.ShapeDtypeStruct((8, LANE), F32)),
        in_specs=(HBM_SPEC,),
        out_specs=(SEM_SPEC, SEM_SPEC, HBM_SPEC, pl.BlockSpec(memory_space=pltpu.VMEM)),
        input_output_aliases={0: 2},
        compiler_params=pltpu.CompilerParams(has_side_effects=pltpu.SideEffectType.DATAFLOW_SIDE_EFFECTING),
    )(_hbm(land))


def _relay_wait(started, after, name):
    send_sems, recv_sems, land_thru = started

    def body(land_ref, send_sems, recv_sems, after_ref, land_out):
        for cp in _relay_copies(land_ref, send_sems, recv_sems, landed=True):
            cp.wait_send()
            cp.wait_recv()

    return pl.pallas_call(
        body, name=name,
        out_shape=pltpu.HBM(land_thru.shape, land_thru.dtype),
        in_specs=(HBM_SPEC, SEM_SPEC, SEM_SPEC, ANY), out_specs=HBM_SPEC,
        input_output_aliases={0: 0},
        compiler_params=pltpu.CompilerParams(has_side_effects=pltpu.SideEffectType.DATAFLOW_SIDE_EFFECTING),
    )(land_thru, send_sems, recv_sems, after)


def _share(vec, name, after=None):
    follows = [] if after is None else [after]

    def body(vec_ref, *rest):
        out_ref, send_sems, recv_sems, local_sem = rest[len(follows):]
        me = _position()

        def copy(k, landed):
            peer = _peer(me, k)
            return pltpu.make_async_remote_copy(
                src_ref=vec_ref, dst_ref=out_ref.at[_slot(peer) if landed else _slot(me)],
                send_sem=send_sems.at[k - 1], recv_sem=recv_sems.at[k - 1], device_id=peer, device_id_type=MESH)

        mine = pltpu.make_async_copy(vec_ref, out_ref.at[_slot(me)], local_sem)
        mine.start()
        sent = [copy(k, False) for k in ALL_PEERS]
        for cp in sent:
            cp.start()
        for k in ALL_PEERS:
            copy(k, True).wait_recv()
        for cp in sent:
            cp.wait_send()
        mine.wait()

    return pl.pallas_call(
        body, name=name,
        in_specs=[ANY] * (1 + len(follows)), out_specs=ANY,
        out_shape=jax.ShapeDtypeStruct((N_DEV,) + vec.shape, vec.dtype),
        scratch_shapes=[pltpu.SemaphoreType.DMA((N_DEV - 1,)), pltpu.SemaphoreType.DMA((N_DEV - 1,)),
                        pltpu.SemaphoreType.DMA],
    )(vec, *follows)


def _sum_slots(parts):
    def body(p_ref, o_ref):
        acc = p_ref[0]
        for dev in range(1, N_DEV):
            acc = acc + p_ref[dev]
        o_ref[...] = acc

    return pl.pallas_call(body, name="sum_slots",
                          out_shape=jax.ShapeDtypeStruct(parts.shape[1:], F32))(parts)


PACK_ROWS = 8


def _packed_rows(size):
    return -(-size // (PACK_ROWS * LANE)) * PACK_ROWS


def _pack(arrs):
    def rows(a):
        flat = a.reshape(-1)
        return jnp.pad(flat, (0, _packed_rows(flat.shape[0]) * LANE - flat.shape[0])).reshape(-1, LANE)

    return jnp.concatenate([rows(a) for a in arrs], axis=0)


def _unpack(packed, shapes):
    out, at = [], 0
    for shp in shapes:
        size = 1
        for dim in shp:
            size *= dim
        nrows = _packed_rows(size)
        out.append(packed[at:at + nrows].reshape(-1)[:size].reshape(shp))
        at += nrows
    return out


def _layer_fwd(x, wt, wo, g_pre, g_post, wa_pad, b_alpha, g_gla, g_att, rb_pad, midway=None, h=None,
               g_pre_next=None):
    if h is None:
        h = _rms_fwd(x, g_pre)
    z = _matmul(h, wt, "nt", F32, *TILES["in_proj"], "in_proj", n_outer=True)
    y_gla, o_gla, *gla_kept = _gla_fwd(z, wa_pad, b_alpha, g_gla)
    y_att, o_att, probs = _att_fwd(z, rb_pad, g_att)
    token = None if midway is None else midway(y_att)
    y = _matmul_cols([y_gla, y_att], wo, F32, *TILES["out_proj"][:2], "out_proj", after=token)
    out, h_next = _post_fwd(x, y, g_post, g_pre_next)
    return out, h_next, (x, h, z, o_gla, gla_kept, o_att, probs, y_gla, y_att, y)


def _layer_bwd(dout, saved, wt, wo, g_pre, g_post, wa_pad, b_alpha, g_gla, g_att, rb_pad, on_dwo, on_dwt,
               post=None, below=None):
    x, h, z, o_gla, gla_kept, o_att, probs, y_gla, y_att, y = saved
    dy, dg_post = _post_bwd(dout, y, g_post) if post is None else post
    dwo = _matmul_rows([y_gla, y_att], dy, BF16, *TILES["out_proj_dw"][:2], "out_proj_dw")
    token = on_dwo(dwo)
    dycat = _matmul(dy, wo, "nt", F32, *TILES["out_proj_dx"], "out_proj_dx", n_outer=True, after=token)
    dq, dk, dv, dgg, dga, dwa, db, dg_gla = _gla_bwd(dycat, o_gla, z, wa_pad, g_gla, *gla_kept)
    daq, dak, dav, dag, drb, dg_att = _att_bwd(dycat, o_att, probs, z, g_att)
    tw, tn = TILES["in_proj_dw"][:2]
    dwt = (_matmul_rows([dq, dk, dv, dgg], h, BF16, tw, tn, "in_proj_dw_gla"),
           _matmul_rows([daq, dak, dav, dag], h, BF16, tw, tn, "in_proj_dw_att"),
           _matmul_rows([dga], h, BF16, LANE, tn, "in_proj_dw_gate"))
    token = on_dwt(dwt)
    dh = _matmul_cols([dq, dk, dv, dgg, daq, dak, dav, dag, dga], wt, F32, *TILES["in_proj_dx"][:2],
                      "in_proj_dx", after=token)
    dx, dg_pre, *post_below = _pre_bwd(dh, x, g_pre, dout, below)
    small = (dg_pre[0], dg_post[0], dwa[:GLA_RANK], db[0], dg_gla[0], dg_att[0], drb[:, 0, :N_REL])
    return dx, small, post_below or None


def kernel(x, w_in, w_out, g_pre, g_post, w_alpha, b_alpha, g_gla, g_att, rel_bias, loss_target, m_w_in, m_w_out, m_g_pre, m_g_post, m_w_alpha, m_b_alpha, m_g_gla, m_g_att, m_rel_bias, v_w_in, v_w_out, v_g_pre, v_g_post, v_w_alpha, v_b_alpha, v_g_gla, v_g_att, v_rel_bias):
    nl, d, cols = w_in.shape
    rows = w_out.shape[1]
    s = x.shape[1]
    x0 = x.reshape(s, d)
    tgt = loss_target.reshape(s, d)

    cols_first = lambda a: jnp.transpose(a, (2, 0, 1))
    w_c = cols_first(w_in)
    slab = _slab_rows(rows, cols)
    is_out = lax.broadcasted_iota(jnp.int32, (slab, d), 0) < rows

    def shard(l, zero=0.0):
        top = jnp.pad((w_out[l] + zero).astype(BF16), ((0, slab - rows), (0, 0)))
        rest = jnp.pad((w_c[:, l] + zero).astype(BF16), ((rows, slab - rows - cols), (0, 0)))
        return jnp.where(is_out, top, rest)

    first_fetch = _exchange_start(shard(0), x, GATHER_PEERS, False, "gather_start_0")
    began = first_fetch[4][0, 0]
    shards = [None] + [shard(l, began) for l in range(1, nl)]
    alpha = _pack([w_alpha]) + began
    wa_g = _share(alpha, "gather_alpha")
    wa_cols = w_alpha.shape[2]
    wa_full = wa_g.reshape(N_DEV, -1)[:, :nl * GLA_RANK * wa_cols].reshape(N_DEV, nl, GLA_RANK, wa_cols)
    wa_full = jnp.transpose(wa_full, (1, 2, 0, 3)).reshape(nl, GLA_RANK, GLA_KW)
    wa_pad = jnp.pad(wa_full, ((0, 0), (0, LANE - GLA_RANK), (0, 0)))
    rb_pad = jnp.pad(rel_bias, ((0, 0), (0, 0), (0, 3 * LANE - N_REL)))

    def layer_args(l, follows=None):
        gp = g_pre[l:l + 1] if follows is None else g_pre[l:l + 1] + follows[:1, :1]
        return (wts[l], wos[l], gp, g_post[l:l + 1], wa_pad[l], b_alpha[l:l + 1], g_gla[l:l + 1],
                g_att[l:l + 1], rb_pad[l])

    my = _slot(_position())

    def fetch(l, after):
        return _exchange_start(shards[l], after, GATHER_PEERS, False, f"gather_start_{l}")

    def relay(l, first_hop, after):
        own[l], land = _exchange_wait(first_hop[:4], after, GATHER_PEERS, False, f"gather_wait_{l}")
        return _relay_start(land, f"relay_start_{l}")

    def midway(l, y):
        flight["relay"] = relay(l + 1, flight["fetch"], y)
        if l + 2 >= nl:
            return flight["relay"][3]
        flight["fetch"] = fetch(l + 2, flight["relay"][2])
        return flight["fetch"][4]

    act, h_next, saved, wts, wos, flight, own = x0, None, [], [], [], {}, [None] * nl
    prepared = (wa_pad[0, :1, :1] + sum(sh[:1, :1].astype(F32) for sh in shards[1:]))
    flight["relay"] = relay(0, first_fetch, prepared)
    if nl > 1:
        flight["fetch"] = fetch(1, flight["relay"][2])
    for l in range(nl):
        land = _relay_wait(flight["relay"][:3], act, f"relay_wait_{l}")
        land = lax.dynamic_update_slice_in_dim(land, own[l][None], my, 0)
        wt_l, wo_l = _aligned_weight(land, rows, cols)
        wts.append(wt_l)
        wos.append(wo_l)
        act, h_next, sv = _layer_fwd(act, *layer_args(l, follows=first_fetch[4] if l == 0 else None),
                                     midway=functools.partial(midway, l) if l + 1 < nl else None, h=h_next,
                                     g_pre_next=g_pre[l + 1:l + 2] if l + 1 < nl else None)
        saved.append(sv)
    dout, sq = _loss_head(act, tgt)
    loss = lax.psum(sq[0, 0] * (0.5 / d), ("x", "y", "c"))

    smalls, pending_out, pending_in = [None] * nl, [None] * nl, [None] * nl

    def send_out(l, dwo):
        pending_out[l] = _exchange_start(dwo.reshape(N_DEV, rows, d), dwo[:1, :1], ALL_PEERS, True,
                                         f"scatter_out_start_{l}")
        return pending_out[l][4]

    def send_in(l, dwt):
        if l > 0:
            pending_in[l] = _exchange_start(_partial_slabs(dwt, cols), dwt[-1], ALL_PEERS, True,
                                            f"scatter_in_start_{l}")
            return pending_in[l][4]
        pair = _exchange_start(_partial_slabs(dwt, cols, by_core=True), dwt[-1], (1,), True,
                               "pair_start_0", slots=BY_CORE)
        by_core, from_sibling = _exchange_wait(pair[:4], pair[4], (1,), True, "pair_wait_0", slots=BY_CORE)
        pending_in[l] = _exchange_start(_pair_sum(by_core, from_sibling), dwt[-1], GATHER_PEERS[1:], True,
                                        "scatter_in_start_0", slots=BY_CHIP)
        return pending_in[l][4]

    post = None
    for l in reversed(range(nl)):
        below = (saved[l - 1][-1], g_post[l - 1:l]) if l > 0 else None
        dout, smalls[l], post = _layer_bwd(dout, saved[l], *layer_args(l), on_dwo=functools.partial(send_out, l),
                                           on_dwt=functools.partial(send_in, l), post=post, below=below)
    grad_x = dout.reshape(x.shape)

    def landed(started, after, name, ks=ALL_PEERS, slots=BY_DEVICE):
        partial, land = _exchange_wait(started[:4], after, ks, True, name, slots=slots)
        mine = slots[0](_position())
        return lax.dynamic_update_slice_in_dim(land, lax.dynamic_slice_in_dim(partial, mine, 1, 0), mine, 0)

    parts_out = [landed(pending_out[l], dout, f"scatter_out_wait_{l}") for l in range(nl)]
    g_w_out, d_w_out, m2_w_out, v2_w_out = _adam_sharded(parts_out, 0, w_out, m_w_out, v_w_out, "adam_w_out")
    names = 7
    small_stacked = [jnp.stack([smalls[l][i] for l in range(nl)]) for i in range(names)]
    shapes = [a.shape for a in small_stacked]
    gathered = _share(_pack(small_stacked), "gather_small_grads", after=d_w_out)
    g_pre_g, g_post_g, wa_g_full, b_g, gla_g, att_g, rb_g = _unpack(_sum_slots(gathered), shapes)
    wa_g_mine = lax.dynamic_slice_in_dim(wa_g_full, my * wa_cols, wa_cols, axis=2)
    grads = [g_pre_g, g_post_g, wa_g_mine, b_g, gla_g, att_g, rb_g]
    ws = [g_pre, g_post, w_alpha, b_alpha, g_gla, g_att, rel_bias]
    ms = [m_g_pre, m_g_post, m_w_alpha, m_b_alpha, m_g_gla, m_g_att, m_rel_bias]
    vs = [v_g_pre, v_g_post, v_w_alpha, v_b_alpha, v_g_gla, v_g_att, v_rel_bias]
    d_s, m2_s, v2_s = _adam_small(ws, grads, ms, vs)

    parts_in = [landed(pending_in[0], d_s[0], "scatter_in_wait_0", GATHER_PEERS[1:], BY_CHIP)]
    parts_in += [landed(pending_in[l], d_s[0], f"scatter_in_wait_{l}") for l in range(1, nl)]
    g_w_in, d_w_in, m2_w_in, v2_w_in = [
        jnp.transpose(a, (1, 2, 0))
        for a in _adam_columns(parts_in, 0, w_c, cols_first(m_w_in), cols_first(v_w_in))]

    def ordered(big_in, big_out, small):
        return [big_in, big_out] + list(small)

    return (loss, grad_x,
            *ordered(g_w_in, g_w_out, grads),
            *ordered(d_w_in, d_w_out, d_s),
            *ordered(m2_w_in, m2_w_out, m2_s),
            *ordered(v2_w_in, v2_w_out, v2_s))
```

```python
import functools

import jax
import jax.numpy as jnp
from jax import lax
from jax.experimental import pallas as pl
from jax.experimental.pallas import tpu as pltpu

F32 = jnp.float32
BF16 = jnp.bfloat16
MESH = pl.DeviceIdType.MESH
ANY = pl.BlockSpec(memory_space=pl.ANY)

CHUNK = 64
GLA_HEADS = 4
GLA_DK = 128
GLA_DV = 256
GLA_KW = GLA_HEADS * GLA_DK
D_GLA = GLA_HEADS * GLA_DV
GLA_RANK = 16
GLA_TAU = 16.0
ATT_HEADS = 8
ATT_HD = 128
D_ATT = ATT_HEADS * ATT_HD
LEFT_CHUNKS = 8
REL_CLIP = 128
N_REL = 2 * REL_CLIP + 1
EPS = 1e-6
D_IN = 2 * GLA_KW + 2 * D_GLA + GLA_RANK + 4 * D_ATT
GLA_SCALE = GLA_DK ** -0.5
ATT_SCALE = ATT_HD ** -0.5

ADAM_LR = 0.001
ADAM_B1 = 0.9
ADAM_B2 = 0.999
ADAM_EPS = 1e-08
ADAM_WD = 0.01
ADAM_STEP = 10

N_DEV = 8
LANE = 128
GA_ORIG = 2 * GLA_KW + 2 * D_GLA
OFF_AQ = GA_ORIG
OFF_GA = GA_ORIG + 4 * D_ATT
D_ZP = OFF_GA + LANE
QB = 2 * CHUNK
ATT_UNROLL = 16
WIN = (LEFT_CHUNKS + 2) * CHUNK
ET_ROWS = WIN + LEFT_CHUNKS * CHUNK
NEG = float("-inf")
VMEM_LIMIT = 48 * 1024 * 1024


def _cparams(sem):
    return pltpu.CompilerParams(dimension_semantics=sem, vmem_limit_bytes=VMEM_LIMIT)


def _dot(a, b):
    return jnp.dot(a, b, preferred_element_type=F32)


def _dot_nt(a, b):
    return lax.dot_general(a, b, (((1,), (1,)), ((), ())), preferred_element_type=F32)


def _dot_tn(a, b):
    return lax.dot_general(a, b, (((0,), (0,)), ((), ())), preferred_element_type=F32)


def _dot01(t, x, left=True):
    if not left:
        t, x = x, t
    hi = x.astype(BF16)
    r = x - hi.astype(F32)
    mid = r.astype(BF16)
    lo = (r - mid.astype(F32)).astype(BF16)
    if left:
        return _dot(t, hi) + _dot(t, mid) + _dot(t, lo)
    return _dot(hi, t) + _dot(mid, t) + _dot(lo, t)


def _sigmoid(x):
    return 1.0 / (1.0 + jnp.exp(-x))


def _log_sigmoid(x):
    return jnp.minimum(x, 0.0) - jnp.log(1.0 + jnp.exp(-jnp.abs(x)))


TILES = {
    "in_proj": (512, D_ZP // 3, None),
    "in_proj_dx": (512, 512, None),
    "in_proj_dw": (512, 2048, None),
    "out_proj": (512, 1024, None),
    "out_proj_dx": (512, 1024, None),
    "out_proj_dw": (1024, 1024, None),
}


def _matmul(a, b, mode, out_dtype, tm, tn, tk, name, n_outer=False, after=None):
    if mode == "nn":
        (m, k), n = a.shape, b.shape[1]
    elif mode == "nt":
        (m, k), n = a.shape, b.shape[0]
    else:
        (k, m), n = a.shape, b.shape[1]
    tm, tn, tk = min(tm, m), min(tn, n), k if tk is None else min(tk, k)
    assert m % tm == 0 and n % tn == 0 and k % tk == 0, (name, m, n, k)
    nk = k // tk
    dot = {"nn": _dot, "nt": _dot_nt, "tn": _dot_tn}[mode]

    follows = [] if after is None else [after]

    def body_whole_k(a_ref, b_ref, *rest):
        o_ref = rest[-1]
        o_ref[...] = dot(a_ref[...], b_ref[...]).astype(out_dtype)

    def body(a_ref, b_ref, *rest):
        o_ref, acc_ref = rest[-2:]
        kk = pl.program_id(2)

        @pl.when(kk == 0)
        def _():
            acc_ref[...] = jnp.zeros_like(acc_ref)

        acc_ref[...] += dot(a_ref[...], b_ref[...])

        @pl.when(kk == nk - 1)
        def _():
            o_ref[...] = acc_ref[...].astype(out_dtype)

    def at(index):
        return (lambda j, i, kk: index(i, j, kk)) if n_outer else index

    if mode == "tn":
        a_spec = pl.BlockSpec((tk, tm), at(lambda i, j, kk: (kk, i)))
    else:
        a_spec = pl.BlockSpec((tm, tk), at(lambda i, j, kk: (i, kk)))
    if mode == "nt":
        b_spec = pl.BlockSpec((tn, tk), at(lambda i, j, kk: (j, kk)))
    else:
        b_spec = pl.BlockSpec((tk, tn), at(lambda i, j, kk: (kk, j)))
    return pl.pallas_call(
        body_whole_k if nk == 1 else body, name=name,
        grid=(n // tn, m // tm, nk) if n_outer else (m // tm, n // tn, nk),
        in_specs=[a_spec, b_spec] + [ANY] * len(follows),
        out_specs=pl.BlockSpec((tm, tn), at(lambda i, j, kk: (i, j))),
        out_shape=jax.ShapeDtypeStruct((m, n), out_dtype),
        scratch_shapes=[] if nk == 1 else [pltpu.VMEM((tm, tn), F32)],
        compiler_params=_cparams(("parallel", "parallel", "arbitrary")),
    )(a, b, *follows)


def _matmul_cols(pieces, b, out_dtype, tm, tn, name, after=None):
    m, n = pieces[0].shape[0], b.shape[1]
    widths = [p.shape[1] for p in pieces]
    starts = [sum(widths[:i]) for i in range(len(pieces))]
    follows = [] if after is None else [after]
    tm, tn = min(tm, m), min(tn, n)
    assert sum(widths) == b.shape[0] and m % tm == 0 and n % tn == 0, name

    def body(*refs):
        b_ref, o_ref = refs[len(pieces)], refs[-1]
        acc = None
        for p_ref, at, width in zip(refs, starts, widths):
            part = _dot(p_ref[...], b_ref[at:at + width, :])
            acc = part if acc is None else acc + part
        o_ref[...] = acc.astype(out_dtype)

    return pl.pallas_call(
        body, name=name, grid=(n // tn, m // tm),
        in_specs=[pl.BlockSpec((tm, width), lambda j, i: (i, 0)) for width in widths]
        + [pl.BlockSpec((b.shape[0], tn), lambda j, i: (0, j))] + [ANY] * len(follows),
        out_specs=pl.BlockSpec((tm, tn), lambda j, i: (i, j)),
        out_shape=jax.ShapeDtypeStruct((m, n), out_dtype),
        compiler_params=_cparams(("parallel", "parallel")),
    )(*pieces, b, *follows)


def _matmul_rows(pieces, b, out_dtype, tw, tn, name):
    k, n = b.shape
    tn = min(tn, n)
    counts = [p.shape[1] // tw for p in pieces]
    firsts = [sum(counts[:i]) for i in range(len(pieces))]
    assert all(p.shape[1] % tw == 0 for p in pieces) and n % tn == 0, name

    def body(*refs):
        b_ref, o_ref = refs[len(pieces):]
        for p_ref, first, count in zip(refs, firsts, counts):
            @pl.when((pl.program_id(0) >= first) & (pl.program_id(0) < first + count))
            def _(p_ref=p_ref):
                o_ref[...] = _dot_tn(p_ref[...], b_ref[...]).astype(out_dtype)

    def piece_spec(first, count):
        return pl.BlockSpec((k, tw), lambda i, j: (0, jnp.clip(i - first, 0, count - 1)))

    return pl.pallas_call(
        body, name=name, grid=(sum(counts), n // tn),
        in_specs=[piece_spec(first, count) for first, count in zip(firsts, counts)]
        + [pl.BlockSpec((k, tn), lambda i, j: (0, j))],
        out_specs=pl.BlockSpec((tw, tn), lambda i, j: (i, j)),
        out_shape=jax.ShapeDtypeStruct((sum(counts) * tw, n), out_dtype),
        compiler_params=_cparams(("parallel", "parallel")),
    )(*pieces, b)


ROWS = 512


def _rms_fwd(x, g):
    s, d = x.shape

    def body(x_ref, g_ref, h_ref):
        xv = x_ref[...]
        r = lax.rsqrt(jnp.mean(xv * xv, axis=-1, keepdims=True) + EPS)
        h_ref[...] = (xv * r * g_ref[...]).astype(BF16)

    return pl.pallas_call(
        body, name="rms_fwd", grid=(s // ROWS,),
        in_specs=[pl.BlockSpec((ROWS, d), lambda i: (i, 0)), pl.BlockSpec((1, d), lambda i: (0, 0))],
        out_specs=pl.BlockSpec((ROWS, d), lambda i: (i, 0)),
        out_shape=jax.ShapeDtypeStruct((s, d), BF16),
        compiler_params=_cparams(("parallel",)),
    )(x, g)


def _post_fwd(x, y, g, g_next=None):
    s, d = x.shape

    def body(x_ref, y_ref, g_ref, *rest):
        yv = y_ref[...]
        r = lax.rsqrt(jnp.mean(yv * yv, axis=-1, keepdims=True) + EPS)
        out = x_ref[...] + yv * r * g_ref[...]
        if g_next is None:
            rest[0][...] = out
            return
        gn_ref, o_ref, h_ref = rest
        o_ref[...] = out
        rn = lax.rsqrt(jnp.mean(out * out, axis=-1, keepdims=True) + EPS)
        h_ref[...] = (out * rn * gn_ref[...]).astype(BF16)

    row = pl.BlockSpec((ROWS, d), lambda i: (i, 0))
    vec = pl.BlockSpec((1, d), lambda i: (0, 0))
    both = g_next is not None
    res = pl.pallas_call(
        body, name="post_fwd", grid=(s // ROWS,),
        in_specs=[row, row, vec] + [vec] * both,
        out_specs=[row] + [row] * both,
        out_shape=[jax.ShapeDtypeStruct((s, d), F32)] + [jax.ShapeDtypeStruct((s, d), BF16)] * both,
        compiler_params=_cparams(("parallel",)),
    )(x, y, g, *([g_next] * both))
    return (res[0], res[1]) if both else (res[0], None)


def _post_loss(x, y, g, tgt):
    s, d = x.shape

    def body(x_ref, y_ref, g_ref, t_ref, dout_ref, sum_ref):
        @pl.when(pl.program_id(0) == 0)
        def _():
            sum_ref[...] = jnp.zeros_like(sum_ref)

        yv = y_ref[...]
        r = lax.rsqrt(jnp.mean(yv * yv, axis=-1, keepdims=True) + EPS)
        e = (x_ref[...] + yv * r * g_ref[...]) - t_ref[...]
        dout_ref[...] = e * (1.0 / d)
        sum_ref[...] += jnp.sum(jnp.sum(e * e, axis=1, keepdims=True), axis=0, keepdims=True)

    row = pl.BlockSpec((ROWS, d), lambda i: (i, 0))
    return pl.pallas_call(
        body, name="post_loss", grid=(s // ROWS,),
        in_specs=[row, row, pl.BlockSpec((1, d), lambda i: (0, 0)), row],
        out_specs=[row, pl.BlockSpec((1, 1), lambda i: (0, 0))],
        out_shape=[jax.ShapeDtypeStruct((s, d), F32), jax.ShapeDtypeStruct((1, 1), F32)],
        compiler_params=_cparams(("arbitrary",)),
    )(x, y, g, tgt)


def _post_bwd(dout, y, g):
    s, d = y.shape

    def body(do_ref, y_ref, g_ref, dy_ref, dg_ref):
        @pl.when(pl.program_id(0) == 0)
        def _():
            dg_ref[...] = jnp.zeros_like(dg_ref)

        yv = y_ref[...]
        dv = do_ref[...]
        r = lax.rsqrt(jnp.mean(yv * yv, axis=-1, keepdims=True) + EPS)
        dg_ref[...] += jnp.sum(dv * yv * r, axis=0, keepdims=True)
        w = dv * g_ref[...]
        dy = r * (w - yv * (r * r) * jnp.mean(w * yv, axis=-1, keepdims=True))
        dy_ref[...] = dy.astype(BF16)

    row = pl.BlockSpec((ROWS, d), lambda i: (i, 0))
    vec = pl.BlockSpec((1, d), lambda i: (0, 0))
    return pl.pallas_call(
        body, name="post_bwd", grid=(s // ROWS,),
        in_specs=[row, row, vec],
        out_specs=[row, vec],
        out_shape=[jax.ShapeDtypeStruct((s, d), BF16), jax.ShapeDtypeStruct((1, d), F32)],
        compiler_params=_cparams(("arbitrary",)),
    )(dout, y, g)


def _pre_bwd(dh, x, g, dout, below=None):
    s, d = x.shape
    rows = ROWS if below is None else ROWS // 2

    def body(dh_ref, x_ref, g_ref, do_ref, *rest):
        if below is None:
            dx_ref, dg_ref = rest
            sums = (dg_ref,)
        else:
            y_ref, gq_ref, dx_ref, dg_ref, dy_ref, dgq_ref = rest
            sums = (dg_ref, dgq_ref)

        @pl.when(pl.program_id(0) == 0)
        def _():
            for ref in sums:
                ref[...] = jnp.zeros_like(ref)

        xv = x_ref[...]
        dv = dh_ref[...]
        r = lax.rsqrt(jnp.mean(xv * xv, axis=-1, keepdims=True) + EPS)
        w = dv * g_ref[...]
        dx = do_ref[...] + r * (w - xv * (r * r) * jnp.mean(w * xv, axis=-1, keepdims=True))
        if below is not None:
            yv = y_ref[...]
            ry = lax.rsqrt(jnp.mean(yv * yv, axis=-1, keepdims=True) + EPS)
            dgq_ref[...] += jnp.sum(dx * yv * ry, axis=0, keepdims=True)
            wy = dx * gq_ref[...]
            dy_ref[...] = (ry * (wy - yv * (ry * ry) * jnp.mean(wy * yv, axis=-1, keepdims=True))).astype(BF16)
        dg_ref[...] += jnp.sum(dv * xv * r, axis=0, keepdims=True)
        dx_ref[...] = dx

    row = pl.BlockSpec((rows, d), lambda i: (i, 0))
    vec = pl.BlockSpec((1, d), lambda i: (0, 0))
    fused = below is not None
    return pl.pallas_call(
        body, name="pre_bwd", grid=(s // rows,),
        in_specs=[row, row, vec, row] + [row, vec] * fused,
        out_specs=[row, vec] + [row, vec] * fused,
        out_shape=[jax.ShapeDtypeStruct((s, d), F32), jax.ShapeDtypeStruct((1, d), F32)]
        + [jax.ShapeDtypeStruct((s, d), BF16), jax.ShapeDtypeStruct((1, d), F32)] * fused,
        compiler_params=_cparams(("arbitrary",)),
    )(dh, x, g, dout, *(below if fused else ()))


GLA_STEP = 4
GLA_ROWS = GLA_STEP * CHUNK
GLA_CHUNKS = [slice(c * CHUNK, (c + 1) * CHUNK) for c in range(GLA_STEP)]


def _chunk_triangles():
    ri = lax.broadcasted_iota(jnp.int32, (GLA_ROWS, GLA_ROWS), 0)
    ci = lax.broadcasted_iota(jnp.int32, (GLA_ROWS, GLA_ROWS), 1)
    same = (ri // CHUNK) == (ci // CHUNK)
    return (jnp.where(same & (ri >= ci), 1.0, 0.0).astype(BF16), jnp.where(same & (ci >= ri), 1.0, 0.0).astype(BF16))


def _per_chunk(fn, like):
    row = lax.broadcasted_iota(jnp.int32, like.shape, 0)
    return [fn((row >= c * CHUNK) & (row < (c + 1) * CHUNK)) for c in range(GLA_STEP)]


def _spread(per_chunk, like):
    row = lax.broadcasted_iota(jnp.int32, like.shape, 0)
    out = per_chunk[-1]
    for c in reversed(range(GLA_STEP - 1)):
        out = jnp.where(row < (c + 1) * CHUNK, per_chunk[c], out)
    return out


def _gla_gate(ga_b, wa_b, b_ref, tri):
    pre = _dot(ga_b, wa_b) + b_ref[...]
    la = _log_sigmoid(pre) * (1.0 / GLA_TAU)
    return pre, _dot01(tri, la)


def _chunk_ends(cum):
    row = lax.broadcasted_iota(jnp.int32, cum.shape, 0)
    return [jnp.sum(jnp.where(row == (c + 1) * CHUNK - 1, cum, 0.0), axis=0, keepdims=True)
            for c in range(GLA_STEP)]


def _heads(width):
    return [slice(h * width, (h + 1) * width) for h in range(GLA_HEADS)]


def _z_specs_gla(rev=None):
    idx = (lambda n: n) if rev is None else rev
    return [
        pl.BlockSpec((GLA_ROWS, GLA_KW), lambda n: (idx(n), 0)),
        pl.BlockSpec((GLA_ROWS, GLA_KW), lambda n: (idx(n), 1)),
        pl.BlockSpec((GLA_ROWS, D_GLA), lambda n: (idx(n), 1)),
        pl.BlockSpec((GLA_ROWS, D_GLA), lambda n: (idx(n), 2)),
        pl.BlockSpec((GLA_ROWS, LANE), lambda n: (idx(n), OFF_GA // LANE)),
    ]


def _gla_fwd(z, wa_pad, b_alpha, g_gla):
    s = z.shape[0]
    nchunk = s // CHUNK

    def body(q_ref, k_ref, v_ref, gg_ref, ga_ref, wa_ref, b_ref, g_ref, y_ref, o_ref, st_ref, pre_ref, cum_ref,
             state):
        @pl.when(pl.program_id(0) == 0)
        def _():
            state[...] = jnp.zeros_like(state)

        ga_b = ga_ref[...].astype(BF16)
        tri, _ = _chunk_triangles()
        nh = range(GLA_HEADS)
        keys, vals = _heads(GLA_DK), _heads(GLA_DV)
        pre, cum = _gla_gate(ga_b, wa_ref[...].astype(BF16), b_ref, tri)
        pre_ref[...] = pre
        cum_ref[...] = cum
        cends = _chunk_ends(cum)
        kd_b = (k_ref[...] * jnp.exp(_spread(cends, cum) - cum)).astype(BF16)
        qs = (q_ref[...] * GLA_SCALE).astype(BF16)
        v_b = v_ref[...].astype(BF16)
        uts = [[_dot_tn(v_b[rs, vals[h]], kd_b[rs, keys[h]]) for h in nh] for rs in GLA_CHUNKS]
        sts, prev = [], [state[h] for h in nh]
        for c in range(GLA_STEP):
            a = jnp.exp(cends[c])
            prev = [prev[h] * a[:, keys[h]] + uts[c][h] for h in nh]
            sts.append(prev)
        for h in nh:
            state[h] = prev[h]
            for c in range(GLA_STEP):
                st_ref[c, h] = sts[c][h]
        outs = [[_dot_nt(qs[rs, keys[h]], sts[c][h].astype(BF16)) for h in nh] for c, rs in enumerate(GLA_CHUNKS)]
        for h in nh:
            o, vs = jnp.concatenate([outs[c][h] for c in range(GLA_STEP)], axis=0), vals[h]
            o_ref[:, vs] = o
            r = lax.rsqrt(jnp.mean(o * o, axis=-1, keepdims=True) + EPS)
            gg = gg_ref[:, vs]
            y_ref[:, vs] = (o * r * g_ref[:, vs] * (gg * _sigmoid(gg))).astype(BF16)

    full = lambda shape: pl.BlockSpec(shape, lambda n: tuple(0 for _ in shape))
    wide = pl.BlockSpec((GLA_ROWS, D_GLA), lambda n: (n, 0))
    return pl.pallas_call(
        body, name="gla_fwd", grid=(nchunk // GLA_STEP,),
        in_specs=_z_specs_gla() + [full((LANE, GLA_KW)), full((1, GLA_KW)), full((1, D_GLA))],
        out_specs=[wide, wide, pl.BlockSpec((GLA_STEP, GLA_HEADS, GLA_DV, GLA_DK), lambda n: (n, 0, 0, 0)),
                   pl.BlockSpec((GLA_ROWS, GLA_KW), lambda n: (n, 0)), pl.BlockSpec((GLA_ROWS, GLA_KW), lambda n: (n, 0))],
        out_shape=[jax.ShapeDtypeStruct((s, D_GLA), BF16), jax.ShapeDtypeStruct((s, D_GLA), F32),
                   jax.ShapeDtypeStruct((nchunk, GLA_HEADS, GLA_DV, GLA_DK), F32),
                   jax.ShapeDtypeStruct((s, GLA_KW), F32), jax.ShapeDtypeStruct((s, GLA_KW), F32)],
        scratch_shapes=[pltpu.VMEM((GLA_HEADS, GLA_DV, GLA_DK), F32)],
        compiler_params=_cparams(("arbitrary",)),
    )(z, z, z, z, z, wa_pad, b_alpha, g_gla)


def _gla_bwd(dyc, o_gla, z, wa_pad, g_gla, states, gate_pre, gate_cum):
    s = z.shape[0]
    nsteps = s // GLA_ROWS
    rev = lambda n: nsteps - 1 - n

    def body(dy_ref, o_ref, q_ref, k_ref, v_ref, gg_ref, ga_ref, wa_ref, g_ref, st_ref, stp_ref, pre_ref, cum_ref,
             dq_ref, dk_ref, dv_ref, dgg_ref, dga_ref, dwa_ref, db_ref, dg_ref, carry):
        step = pl.program_id(0)

        @pl.when(step == 0)
        def _():
            carry[...] = jnp.zeros_like(carry)
            dwa_ref[...] = jnp.zeros_like(dwa_ref)
            db_ref[...] = jnp.zeros_like(db_ref)
            dg_ref[...] = jnp.zeros_like(dg_ref)

        has_prev = (step < nsteps - 1).astype(F32)
        ga_b = ga_ref[...].astype(BF16)
        _, tri_up = _chunk_triangles()
        nh, nc = range(GLA_HEADS), range(GLA_STEP)
        keys, vals = _heads(GLA_DK), _heads(GLA_DV)
        wa_b = wa_ref[...].astype(BF16)
        pre, cum = pre_ref[...], cum_ref[...]
        cends = _chunk_ends(cum)
        e = jnp.exp(_spread(cends, cum) - cum)
        a = [jnp.exp(cends[c]) for c in nc]
        kf = k_ref[...]
        kd_b = (kf * e).astype(BF16)
        v_b = v_ref[...].astype(BF16)
        qs = (q_ref[...] * GLA_SCALE).astype(BF16)
        do_b = []
        for h in nh:
            vs = vals[h]
            o = o_ref[:, vs]
            gg = gg_ref[:, vs]
            g = g_ref[:, vs]
            dy = dy_ref[:, vs]
            r = lax.rsqrt(jnp.mean(o * o, axis=-1, keepdims=True) + EPS)
            sg = _sigmoid(gg)
            dogn = dy * (gg * sg)
            dgg_ref[:, vs] = (dy * (o * r * g) * (sg * (1.0 + gg * (1.0 - sg)))).astype(BF16)
            dg_ref[:, vs] += jnp.sum(dogn * o * r, axis=0, keepdims=True)
            w = dogn * g
            do_b.append((r * (w - o * (r * r) * jnp.mean(w * o, axis=-1, keepdims=True))).astype(BF16))
        dqs = [jnp.concatenate([_dot(do_b[h][rs], st_ref[c, h].astype(BF16)) for c, rs in enumerate(GLA_CHUNKS)],
                               axis=0) for h in nh]
        dq_ref[...] = (jnp.concatenate(dqs, axis=1) * GLA_SCALE).astype(BF16)
        own = [[_dot_tn(do_b[h][rs], qs[rs, keys[h]]) for h in nh] for rs in GLA_CHUNKS]
        gts, later = [None] * GLA_STEP, [carry[h] for h in nh]
        for c in reversed(nc):
            gts[c] = [own[c][h] + later[h] for h in nh]
            later = [gts[c][h] * a[c][:, keys[h]] for h in nh]
        for h in nh:
            carry[h] = later[h]
        gt_b = [[gts[c][h].astype(BF16) for h in nh] for c in nc]
        dkd = jnp.concatenate([jnp.concatenate([_dot(v_b[rs, vals[h]], gt_b[c][h]) for h in nh], axis=1)
                               for c, rs in enumerate(GLA_CHUNKS)], axis=0)
        dvs = [[_dot_nt(kd_b[rs, keys[h]], gt_b[c][h]) for h in nh] for c, rs in enumerate(GLA_CHUNKS)]
        before = lambda c, h: st_ref[c - 1, h] if c > 0 else stp_ref[0, h] * has_prev
        da = [jnp.concatenate([jnp.sum(gts[c][h] * before(c, h), axis=0, keepdims=True) for h in nh], axis=1)
              for c in nc]
        for h in nh:
            dv_ref[:, vals[h]] = jnp.concatenate([dvs[c][h] for c in nc], axis=0).astype(BF16)
        dk_ref[...] = (dkd * e).astype(BF16)
        dd = dkd * kf * e
        dsum = _per_chunk(lambda mine: jnp.sum(jnp.where(mine, dd, 0.0), axis=0, keepdims=True), dd)
        dcend = _spread([dsum[c] + da[c] * a[c] for c in nc], dd)
        dla = dcend - _dot01(tri_up, dd)
        dpre = dla * (1.0 / GLA_TAU) * (1.0 - _sigmoid(pre))
        dpre_b = dpre.astype(BF16)
        dga_ref[...] = _dot_nt(dpre_b, wa_b).astype(BF16)
        dwa_ref[...] += _dot_tn(ga_b, dpre_b)
        db_ref[...] += jnp.sum(dpre, axis=0, keepdims=True)

    full = lambda shape: pl.BlockSpec(shape, lambda n: tuple(0 for _ in shape))
    wide = pl.BlockSpec((GLA_ROWS, D_GLA), lambda n: (rev(n), 0))
    keyw = pl.BlockSpec((GLA_ROWS, GLA_KW), lambda n: (rev(n), 0))
    st_spec = pl.BlockSpec((GLA_STEP, GLA_HEADS, GLA_DV, GLA_DK), lambda n: (rev(n), 0, 0, 0))
    stp_spec = pl.BlockSpec((1, GLA_HEADS, GLA_DV, GLA_DK),
                            lambda n: (jnp.maximum(GLA_STEP * rev(n) - 1, 0), 0, 0, 0))
    return pl.pallas_call(
        body, name="gla_bwd", grid=(nsteps,),
        in_specs=[wide, wide] + _z_specs_gla(rev)
        + [full((LANE, GLA_KW)), full((1, D_GLA)), st_spec, stp_spec, keyw, keyw],
        out_specs=[keyw, keyw, wide, wide, pl.BlockSpec((GLA_ROWS, LANE), lambda n: (rev(n), 0)),
                   full((LANE, GLA_KW)), full((1, GLA_KW)), full((1, D_GLA))],
        out_shape=[jax.ShapeDtypeStruct((s, GLA_KW), BF16), jax.ShapeDtypeStruct((s, GLA_KW), BF16),
                   jax.ShapeDtypeStruct((s, D_GLA), BF16), jax.ShapeDtypeStruct((s, D_GLA), BF16),
                   jax.ShapeDtypeStruct((s, LANE), BF16),
                   jax.ShapeDtypeStruct((LANE, GLA_KW), F32), jax.ShapeDtypeStruct((1, GLA_KW), F32),
                   jax.ShapeDtypeStruct((1, D_GLA), F32)],
        scratch_shapes=[pltpu.VMEM((GLA_HEADS, GLA_DV, GLA_DK), F32)],
        compiler_params=_cparams(("arbitrary",)),
    )(dyc, o_gla, z, z, z, z, z, wa_pad, g_gla, states, states, gate_pre, gate_cum)


def _build_bias_table(rb_row, et_ref):
    far = jnp.broadcast_to(rb_row[:, 2 * REL_CLIP:2 * REL_CLIP + 1], (1, LANE))
    near_hi = rb_row[:, REL_CLIP:2 * REL_CLIP]
    near_lo = rb_row[:, 0:REL_CLIP]
    past = jnp.broadcast_to(rb_row[:, 0:1], (1, LANE))
    seg = [far, far, far, far, near_hi, near_lo] + [past] * (ET_ROWS // LANE - 5)
    ri = lax.broadcasted_iota(jnp.int32, (LANE, LANE), 0)
    ci = lax.broadcasted_iota(jnp.int32, (LANE, LANE), 1)
    for kb in range(ET_ROWS // LANE):
        wmat = jnp.where(ri + ci < LANE, seg[kb], seg[kb + 1])
        blk = pltpu.roll(wmat, 0, 1, stride=1, stride_axis=0)
        lag = LEFT_CHUNKS + ci // CHUNK - (2 * kb + ri // CHUNK)
        et_ref[kb * LANE:(kb + 1) * LANE, :] = jnp.where((lag >= 0) & (lag <= LEFT_CHUNKS), blk, NEG)


def _reduce_bias_table(det_ref):
    lane = lax.broadcasted_iota(jnp.int32, (1, LANE), 1)
    ri = lax.broadcasted_iota(jnp.int32, (LANE, LANE), 0)
    ci = lax.broadcasted_iota(jnp.int32, (LANE, LANE), 1)
    flip = jnp.where(ri + ci == LANE - 1, 1.0, 0.0).astype(BF16)
    segs = jnp.zeros((8, LANE), F32)
    seg_row = lax.broadcasted_iota(jnp.int32, (8, LANE), 0)
    prev_minus = jnp.zeros((1, LANE), F32)
    for kb in range(6):
        rolled = pltpu.roll(_dot01(det_ref[kb * LANE:(kb + 1) * LANE, :], flip, left=False), 0, 1,
                            stride=1, stride_axis=0)
        plus = jnp.sum(jnp.where(ci >= ri, rolled, 0.0), axis=0, keepdims=True)
        minus = jnp.sum(jnp.where(ci < ri, rolled, 0.0), axis=0, keepdims=True)
        segs = segs + jnp.where(seg_row == kb, plus + prev_minus, 0.0)
        prev_minus = minus
    segs = _dot01(segs, flip, left=False)
    pick = lambda kb: jnp.sum(jnp.where(seg_row == kb, segs, 0.0), axis=0, keepdims=True)
    far = jnp.sum(pick(0) + pick(1) + pick(2) + pick(3), axis=1, keepdims=True)
    last = jnp.where(lane == 0, far, 0.0)
    return jnp.concatenate([pick(5), pick(4), last], axis=1)


def _att_window(b):
    c0 = 2 * b
    kstart = pl.multiple_of(jnp.maximum(c0 - LEFT_CHUNKS, 0) * CHUNK, CHUNK)
    eoff = pl.multiple_of(jnp.maximum(LEFT_CHUNKS - c0, 0) * CHUNK, CHUNK)
    return kstart, eoff


def _att_probs(q_b, kw_b, et):
    st = _dot_nt(kw_b, q_b) * ATT_SCALE + et
    m = jnp.max(st, axis=0, keepdims=True)
    ex = jnp.exp(st - m)
    return ex * (1.0 / jnp.sum(ex, axis=0, keepdims=True))


def _att_fwd(z, rb_pad, g_att):
    s = z.shape[0]
    nblk = s // QB
    c_aq, c_ak, c_av, c_ag = [(OFF_AQ + i * D_ATT) // ATT_HD for i in range(4)]

    def body(q_ref, k_ref, v_ref, ag_ref, rb_ref, g_ref, y_ref, o_ref, p_ref, et_ref, kb_ref, vb_ref):
        h = pl.program_id(0)
        b = pl.program_id(1)

        @pl.when(b == 0)
        def _():
            _build_bias_table(rb_ref[pl.ds(h, 1), :], et_ref)
            kb_ref[...] = k_ref[...].astype(BF16)
            vb_ref[...] = v_ref[...].astype(BF16)

        for j in range(ATT_UNROLL):
            rs = slice(j * QB, (j + 1) * QB)
            kstart, eoff = _att_window(b * ATT_UNROLL + j)
            q_b = q_ref[rs, :].astype(BF16)
            kw_b = kb_ref[pl.ds(kstart, WIN), :]
            vw_b = vb_ref[pl.ds(kstart, WIN), :]
            pt = _att_probs(q_b, kw_b, et_ref[pl.ds(eoff, WIN), :])
            p_ref[0, j] = pt
            o = _dot_tn(pt.astype(BF16), vw_b)
            o_ref[rs, :] = o
            r = lax.rsqrt(jnp.mean(o * o, axis=-1, keepdims=True) + EPS)
            ag = ag_ref[rs, :]
            y_ref[rs, :] = (o * r * g_ref[...] * (ag * _sigmoid(ag))).astype(BF16)

    blk = lambda col: pl.BlockSpec((ATT_UNROLL * QB, ATT_HD), lambda h, b: (b, col + h))
    seq = lambda col: pl.BlockSpec((s, ATT_HD), lambda h, b: (0, col + h))
    out_blk = pl.BlockSpec((ATT_UNROLL * QB, ATT_HD), lambda h, b: (b, h))
    return pl.pallas_call(
        body, name="att_fwd", grid=(ATT_HEADS, nblk // ATT_UNROLL),
        in_specs=[blk(c_aq), seq(c_ak), seq(c_av), blk(c_ag),
                  pl.BlockSpec((ATT_HEADS, 3 * LANE), lambda h, b: (0, 0)),
                  pl.BlockSpec((1, ATT_HD), lambda h, b: (0, h))],
        out_specs=[out_blk, out_blk, pl.BlockSpec((1, ATT_UNROLL, WIN, QB), lambda h, b: (h, b, 0, 0))],
        out_shape=[jax.ShapeDtypeStruct((s, D_ATT), BF16), jax.ShapeDtypeStruct((s, D_ATT), F32),
                   jax.ShapeDtypeStruct((ATT_HEADS, nblk, WIN, QB), F32)],
        scratch_shapes=[pltpu.VMEM((ET_ROWS, LANE), F32), pltpu.VMEM((s, ATT_HD), BF16),
                        pltpu.VMEM((s, ATT_HD), BF16)],
        compiler_params=_cparams(("arbitrary", "arbitrary")),
    )(z, z, z, z, rb_pad, g_att)


def _att_bwd(dyc, o_att, probs, z, g_att):
    s = z.shape[0]
    nblk = s // QB
    c_aq, c_ak, c_av, c_ag = [(OFF_AQ + i * D_ATT) // ATT_HD for i in range(4)]
    c_dy = D_GLA // ATT_HD

    def body(dy_ref, o_ref, p_ref, q_ref, k_ref, v_ref, ag_ref, g_ref,
             dq_ref, dk_ref, dv_ref, dag_ref, drb_ref, dg_ref, det_ref, kb_ref, vb_ref, dk_acc, dv_acc):
        b = pl.program_id(1)

        @pl.when(b == 0)
        def _():
            kb_ref[...] = k_ref[...].astype(BF16)
            vb_ref[...] = v_ref[...].astype(BF16)
            det_ref[...] = jnp.zeros_like(det_ref)
            dk_acc[...] = jnp.zeros_like(dk_acc)
            dv_acc[...] = jnp.zeros_like(dv_acc)
            dg_ref[...] = jnp.zeros_like(dg_ref)

        g = g_ref[...]
        dg = jnp.zeros((1, ATT_HD), F32)
        for j in range(ATT_UNROLL):
            rs = slice(j * QB, (j + 1) * QB)
            kstart, eoff = _att_window(b * ATT_UNROLL + j)
            q_b = q_ref[rs, :].astype(BF16)
            kw_b = kb_ref[pl.ds(kstart, WIN), :]
            vw_b = vb_ref[pl.ds(kstart, WIN), :]
            pt = p_ref[0, j]
            o = o_ref[rs, :]
            ag = ag_ref[rs, :]
            dy = dy_ref[rs, :]
            r = lax.rsqrt(jnp.mean(o * o, axis=-1, keepdims=True) + EPS)
            sg = _sigmoid(ag)
            don = dy * (ag * sg)
            dag_ref[rs, :] = (dy * (o * r * g) * (sg * (1.0 + ag * (1.0 - sg)))).astype(BF16)
            dg = dg + jnp.sum(don * o * r, axis=0, keepdims=True)
            w = don * g
            do_b = (r * (w - o * (r * r) * jnp.mean(w * o, axis=-1, keepdims=True))).astype(BF16)
            pt_b = pt.astype(BF16)
            dpt = _dot_nt(vw_b, do_b)
            dst = pt * (dpt - jnp.sum(dpt * pt, axis=0, keepdims=True))
            det_ref[pl.ds(eoff, WIN), :] += dst
            ds_b = (dst * ATT_SCALE).astype(BF16)
            dq_ref[rs, :] = _dot_tn(ds_b, kw_b).astype(BF16)
            dk_acc[pl.ds(kstart, WIN), :] += _dot(ds_b, q_b)
            dv_acc[pl.ds(kstart, WIN), :] += _dot(pt_b, do_b)
        dg_ref[...] += dg

        @pl.when(b == nblk // ATT_UNROLL - 1)
        def _():
            drb_ref[0] = jnp.broadcast_to(_reduce_bias_table(det_ref), (8, 3 * LANE))
            dk_ref[...] = dk_acc[...].astype(BF16)
            dv_ref[...] = dv_acc[...].astype(BF16)

    blk = lambda col: pl.BlockSpec((ATT_UNROLL * QB, ATT_HD), lambda h, b: (b, col + h))
    seq = lambda col: pl.BlockSpec((s, ATT_HD), lambda h, b: (0, col + h))
    out_blk = pl.BlockSpec((ATT_UNROLL * QB, ATT_HD), lambda h, b: (b, h))
    out_seq = pl.BlockSpec((s, ATT_HD), lambda h, b: (0, h))
    return pl.pallas_call(
        body, name="att_bwd", grid=(ATT_HEADS, nblk // ATT_UNROLL),
        in_specs=[blk(c_dy), blk(0), pl.BlockSpec((1, ATT_UNROLL, WIN, QB), lambda h, b: (h, b, 0, 0)),
                  blk(c_aq), seq(c_ak), seq(c_av), blk(c_ag),
                  pl.BlockSpec((1, ATT_HD), lambda h, b: (0, h))],
        out_specs=[out_blk, out_seq, out_seq, out_blk,
                   pl.BlockSpec((1, 8, 3 * LANE), lambda h, b: (h, 0, 0)),
                   pl.BlockSpec((1, ATT_HD), lambda h, b: (0, h))],
        out_shape=[jax.ShapeDtypeStruct((s, D_ATT), BF16), jax.ShapeDtypeStruct((s, D_ATT), BF16),
                   jax.ShapeDtypeStruct((s, D_ATT), BF16), jax.ShapeDtypeStruct((s, D_ATT), BF16),
                   jax.ShapeDtypeStruct((ATT_HEADS, 8, 3 * LANE), F32),
                   jax.ShapeDtypeStruct((1, D_ATT), F32)],
        scratch_shapes=[pltpu.VMEM((ET_ROWS, LANE), F32),
                        pltpu.VMEM((s, ATT_HD), BF16), pltpu.VMEM((s, ATT_HD), BF16),
                        pltpu.VMEM((s, ATT_HD), F32), pltpu.VMEM((s, ATT_HD), F32)],
        compiler_params=_cparams(("arbitrary", "arbitrary")),
    )(dyc, o_att, probs, z, z, z, z, g_att)


ADAM_ROWS = 64
ADAM_COL_ROWS = 32


def _adam_math(w, g, m, v):
    m2 = ADAM_B1 * m + (1.0 - ADAM_B1) * g
    v2 = ADAM_B2 * v + (1.0 - ADAM_B2) * (g * g)
    m_hat = m2 / (1.0 - ADAM_B1 ** ADAM_STEP)
    v_hat = v2 / (1.0 - ADAM_B2 ** ADAM_STEP)
    delta = -ADAM_LR * (m_hat / (jnp.sqrt(v_hat) + ADAM_EPS) + ADAM_WD * w)
    return delta, m2, v2


def _adam_sharded(parts, first, w, m, v, name):
    nl, nr, nc = w.shape

    def body(*refs):
        p_refs = refs[:nl]
        w_ref, m_ref, v_ref, g_ref, d_ref, m2_ref, v2_ref = refs[nl:]
        for k in range(nl):
            @pl.when(pl.program_id(0) == k)
            def _(p_ref=p_refs[k]):
                g = p_ref[0].astype(F32)
                for dev in range(1, N_DEV):
                    g = g + p_ref[dev].astype(F32)
                delta, m2, v2 = _adam_math(w_ref[0], g, m_ref[0], v_ref[0])
                g_ref[0] = g
                d_ref[0] = delta
                m2_ref[0] = m2
                v2_ref[0] = v2

    def part_spec(k):
        return pl.BlockSpec((N_DEV, ADAM_ROWS, nc), lambda l, i: (0, first + jnp.where(l == k, i, 0), 0))

    blk = pl.BlockSpec((1, ADAM_ROWS, nc), lambda l, i: (l, i, 0))
    shp = jax.ShapeDtypeStruct(w.shape, F32)
    return pl.pallas_call(
        body, name=name, grid=(nl, pl.cdiv(nr, ADAM_ROWS)),
        in_specs=[part_spec(k) for k in range(nl)] + [blk, blk, blk],
        out_specs=[blk, blk, blk, blk],
        out_shape=[shp, shp, shp, shp],
        compiler_params=_cparams(("arbitrary", "arbitrary")),
    )(*parts, w, m, v)


def _adam_columns(parts, first, w, m, v):
    nc, nl, d = w.shape

    def body(*refs):
        p_refs = refs[:nl]
        w_ref, m_ref, v_ref, g_ref, d_ref, m2_ref, v2_ref = refs[nl:]
        for l in range(nl):
            g = p_refs[l][0].astype(F32)
            for slot in range(1, parts[l].shape[0]):
                g = g + p_refs[l][slot].astype(F32)
            delta, m2, v2 = _adam_math(w_ref[:, l, :], g, m_ref[:, l, :], v_ref[:, l, :])
            g_ref[:, l, :] = g
            d_ref[:, l, :] = delta
            m2_ref[:, l, :] = m2
            v2_ref[:, l, :] = v2

    blk = pl.BlockSpec((ADAM_COL_ROWS, nl, d), lambda i: (i, 0, 0))
    shp = jax.ShapeDtypeStruct(w.shape, F32)
    return pl.pallas_call(
        body, name="adam_w_in", grid=(pl.cdiv(nc, ADAM_COL_ROWS),),
        in_specs=[pl.BlockSpec((p.shape[0], ADAM_COL_ROWS, d), lambda i: (0, first + i, 0)) for p in parts]
        + [blk, blk, blk],
        out_specs=[blk, blk, blk, blk],
        out_shape=[shp, shp, shp, shp],
        compiler_params=_cparams(("parallel",)),
    )(*parts, w, m, v)


def _adam_small(ws, gs, ms, vs):
    n = len(ws)

    def body(*refs):
        w_refs, g_refs, m_refs, v_refs, d_refs, m2_refs, v2_refs = [refs[i * n:(i + 1) * n] for i in range(7)]
        for i in range(n):
            delta, m2, v2 = _adam_math(w_refs[i][...], g_refs[i][...], m_refs[i][...], v_refs[i][...])
            d_refs[i][...] = delta
            m2_refs[i][...] = m2
            v2_refs[i][...] = v2

    shapes = [jax.ShapeDtypeStruct(w.shape, F32) for w in ws]
    out = pl.pallas_call(body, name="adam_small", out_shape=shapes * 3)(*ws, *gs, *ms, *vs)
    return out[:n], out[n:2 * n], out[2 * n:]


def _position():
    return lax.axis_index("x"), lax.axis_index("y"), lax.axis_index("c")


def _slot(p):
    return 4 * p[0] + 2 * p[1] + p[2]


BF16_TILE_ROWS = 16


def _slab_rows(rows, cols):
    return -(-(rows + cols) // BF16_TILE_ROWS) * BF16_TILE_ROWS


RELAYOUT_COLS = 1024
RELAYOUT_CHUNK = 64


def _shard_pieces(dev, rows, cols):
    moved = ((0, GA_ORIG, 0), (GA_ORIG, GA_ORIG + GLA_RANK, OFF_GA - GA_ORIG), (GA_ORIG + GLA_RANK, D_IN, -GLA_RANK))
    c0, c1 = dev * cols, (dev + 1) * cols
    return [(rows + max(c0, lo) - c0, max(c0, lo) + off, min(c1, hi) - max(c0, lo))
            for lo, hi, off in moved if max(c0, lo) < min(c1, hi)]


def _move_rows(src, src_row, dst, dst_row, n):
    assert src_row % 2 == 0 and dst_row % 2 == 0 and n % 2 == 0
    for r in range(0, n // 2, RELAYOUT_CHUNK):
        m = min(RELAYOUT_CHUNK, n // 2 - r)
        dst[dst_row // 2 + r:dst_row // 2 + r + m, :] = src[src_row // 2 + r:src_row // 2 + r + m, :]


def _aligned_weight(land, rows, cols):
    _, slab, d = land.shape
    ct = min(RELAYOUT_COLS, d)

    def body(land_ref, wt_ref, wo_ref):
        dev = pl.program_id(1)
        src = land_ref.bitcast(jnp.uint32)
        dst = wt_ref.bitcast(jnp.uint32)
        wo_ref[...] = land_ref[0:rows, :]

        @pl.when(dev == 0)
        def _():
            dst[D_IN // 2:D_ZP // 2, :] = jnp.zeros(((D_ZP - D_IN) // 2, ct), jnp.uint32)

        for k in range(N_DEV):
            @pl.when(dev == k)
            def _(k=k):
                for at, to, n in _shard_pieces(k, rows, cols):
                    _move_rows(src, at, dst, to, n)

    return pl.pallas_call(
        body, name="aligned_weight", grid=(d // ct, N_DEV),
        in_specs=[pl.BlockSpec((slab, ct), lambda c, dev: (dev, c))],
        out_specs=[pl.BlockSpec((D_ZP, ct), lambda c, dev: (0, c)),
                   pl.BlockSpec((rows, ct), lambda c, dev: (dev, c))],
        out_shape=[jax.ShapeDtypeStruct((D_ZP, d), land.dtype),
                   jax.ShapeDtypeStruct((N_DEV * rows, d), land.dtype)],
        compiler_params=_cparams(("parallel", "arbitrary")),
    )(land.reshape(N_DEV * slab, d))


def _partial_slabs(dwt, cols, by_core=False):
    d = dwt[0].shape[1]
    bounds = (0, GA_ORIG, OFF_GA, D_ZP)
    assert tuple(a.shape[0] for a in dwt) == tuple(hi - lo for lo, hi in zip(bounds, bounds[1:]))
    slab = _slab_rows(0, cols)
    ct = min(RELAYOUT_COLS, d)

    def body(*refs):
        out_ref = refs[-1]
        dev = pl.program_id(1)
        srcs = [ref.bitcast(jnp.uint32) for ref in refs[:-1]]
        dst = out_ref.bitcast(jnp.uint32)
        dst[cols // 2:slab // 2, :] = jnp.zeros(((slab - cols) // 2, ct), jnp.uint32)
        for k in range(N_DEV):
            @pl.when(dev == k)
            def _(k=k):
                for to, at, n in _shard_pieces(k, 0, cols):
                    which = max(i for i, lo in enumerate(bounds[:-1]) if lo <= at)
                    assert at + n <= bounds[which + 1]
                    _move_rows(srcs[which], at - bounds[which], dst, to, n)

    place = (lambda dev: (dev % 2) * (N_DEV // 2) + dev // 2) if by_core else (lambda dev: dev)
    out = pl.pallas_call(
        body, name="partial_slabs", grid=(d // ct, N_DEV),
        in_specs=[pl.BlockSpec((a.shape[0], ct), lambda c, dev: (0, c)) for a in dwt],
        out_specs=pl.BlockSpec((slab, ct), lambda c, dev: (place(dev), c)),
        out_shape=jax.ShapeDtypeStruct((N_DEV * slab, d), dwt[0].dtype),
        compiler_params=_cparams(("parallel", "arbitrary")),
    )(*dwt)
    return out.reshape((2, N_DEV // 2, slab, d) if by_core else (N_DEV, slab, d))


def _pair_sum(mine, theirs):
    _, nchip, slab, d = mine.shape
    rows = next(r for r in range(512, 0, -BF16_TILE_ROWS) if slab % r == 0)

    def body(m_ref, t_ref, o_ref):
        south = lax.axis_index("c") == 0
        own = jnp.where(south, m_ref[0, 0], m_ref[1, 0]).astype(F32)
        got = jnp.where(south, t_ref[1, 0], t_ref[0, 0]).astype(F32)
        o_ref[0] = (own + got).astype(o_ref.dtype)

    both = pl.BlockSpec((2, 1, rows, d), lambda j, i: (0, j, i, 0))
    return pl.pallas_call(
        body, name="pair_sum", grid=(nchip, slab // rows),
        in_specs=[both, both],
        out_specs=pl.BlockSpec((1, rows, d), lambda j, i: (j, i, 0)),
        out_shape=jax.ShapeDtypeStruct((nchip, slab, d), mine.dtype),
        compiler_params=_cparams(("parallel", "parallel")),
    )(mine, theirs)


def _peer(pos, k):
    x, y, c = pos
    return (1 - x if k & 4 else x, 1 - y if k & 2 else y, 1 - c if k & 1 else c)


HBM_SPEC = pl.BlockSpec(memory_space=pltpu.HBM)
SEM_SPEC = pl.BlockSpec(memory_space=pltpu.SEMAPHORE)
GATHER_PEERS = (1, 4, 2, 6)
ALL_PEERS = (1, 2, 3, 4, 5, 6, 7)


def _hbm(a):
    return pltpu.with_memory_space_constraint(a, pltpu.HBM)


BY_DEVICE = (_slot, N_DEV)
BY_CORE = (lambda p: p[2], 2)
BY_CHIP = (lambda p: 2 * p[0] + p[1], 4)


def _split_copies(src_ref, land_ref, send_sems, recv_sems, ks, per_peer, landed, slots):
    slot_of = slots[0]
    me = _position()
    out = []
    for i, k in enumerate(ks):
        peer = _peer(me, k)
        src = src_ref.at[slot_of(peer)] if per_peer else src_ref
        dst = land_ref.at[slot_of(peer) if landed else slot_of(me)]
        out.append(pltpu.make_async_remote_copy(
            src_ref=src, dst_ref=dst, send_sem=send_sems.at[i], recv_sem=recv_sems.at[i],
            device_id=peer, device_id_type=MESH))
    return out


def _exchange_start(src, after, ks, per_peer, name, slots=BY_DEVICE):
    slab = src.shape[1:] if per_peer else src.shape
    land_shape = (slots[1],) + tuple(slab)
    n = len(ks)

    def body(src_ref, land_ref, after_ref, send_sems, recv_sems, src_thru, land_thru, token):
        for cp in _split_copies(src_ref, land_ref, send_sems, recv_sems, ks, per_peer, False, slots):
            cp.start()
        token[...] = jnp.zeros_like(token)

    return pl.pallas_call(
        body, name=name,
        out_shape=(pltpu.SemaphoreType.DMA((n,)), pltpu.SemaphoreType.DMA((n,)),
                   pltpu.HBM(src.shape, src.dtype), pltpu.HBM(land_shape, src.dtype),
                   jax.ShapeDtypeStruct((8, LANE), F32)),
        in_specs=(HBM_SPEC, HBM_SPEC, ANY),
        out_specs=(SEM_SPEC, SEM_SPEC, HBM_SPEC, HBM_SPEC, pl.BlockSpec(memory_space=pltpu.VMEM)),
        input_output_aliases={0: 2, 1: 3},
        compiler_params=pltpu.CompilerParams(has_side_effects=pltpu.SideEffectType.DATAFLOW_SIDE_EFFECTING),
    )(_hbm(src), _hbm(lax.empty(land_shape, src.dtype)), after)


def _exchange_wait(started, after, ks, per_peer, name, slots=BY_DEVICE):
    send_sems, recv_sems, src_thru, land_thru = started

    def body(src_ref, land_ref, send_sems, recv_sems, after_ref, src_dead, land_out):
        for cp in _split_copies(src_ref, land_ref, send_sems, recv_sems, ks, per_peer, True, slots):
            cp.wait_send()
            cp.wait_recv()

    return pl.pallas_call(
        body, name=name,
        out_shape=(pltpu.HBM(src_thru.shape, src_thru.dtype), pltpu.HBM(land_thru.shape, land_thru.dtype)),
        in_specs=(HBM_SPEC, HBM_SPEC, SEM_SPEC, SEM_SPEC, ANY), out_specs=(HBM_SPEC, HBM_SPEC),
        input_output_aliases={0: 0, 1: 1},
        compiler_params=pltpu.CompilerParams(has_side_effects=pltpu.SideEffectType.DATAFLOW_SIDE_EFFECTING),
    )(src_thru, land_thru, send_sems, recv_sems, after)


def _relay_copies(land_ref, send_sems, recv_sems, landed):
    me = _position()
    sibling = _peer(me, 1)
    out = []
    for i, k in enumerate(GATHER_PEERS[1:]):
        blk = land_ref.at[_slot(_peer(sibling if landed else me, k))]
        out.append(pltpu.make_async_remote_copy(
            src_ref=blk, dst_ref=blk, send_sem=send_sems.at[i], recv_sem=recv_sems.at[i],
            device_id=sibling, device_id_type=MESH))
    return out


def _relay_start(land, name):
    n = len(GATHER_PEERS) - 1

    def body(land_ref, send_sems, recv_sems, land_thru, token):
        for cp in _relay_copies(land_ref, send_sems, recv_sems, landed=False):
            cp.start()
        token[...] = jnp.zeros_like(token)

    return pl.pallas_call(
        body, name=name,
        out_shape=(pltpu.SemaphoreType.DMA((n,)), pltpu.SemaphoreType.DMA((n,)),
                   pltpu.HBM(land.shape, land.dtype), jax.ShapeDtypeStruct((8, LANE), F32)),
        in_specs=(HBM_SPEC,),
        out_specs=(SEM_SPEC, SEM_SPEC, HBM_SPEC, pl.BlockSpec(memory_space=pltpu.VMEM)),
        input_output_aliases={0: 2},
        compiler_params=pltpu.CompilerParams(has_side_effects=pltpu.SideEffectType.DATAFLOW_SIDE_EFFECTING),
    )(_hbm(land))


def _relay_wait(started, after, name):
    send_sems, recv_sems, land_thru = started

    def body(land_ref, send_sems, recv_sems, after_ref, land_out):
        for cp in _relay_copies(land_ref, send_sems, recv_sems, landed=True):
            cp.wait_send()
            cp.wait_recv()

    return pl.pallas_call(
        body, name=name,
        out_shape=pltpu.HBM(land_thru.shape, land_thru.dtype),
        in_specs=(HBM_SPEC, SEM_SPEC, SEM_SPEC, ANY), out_specs=HBM_SPEC,
        input_output_aliases={0: 0},
        compiler_params=pltpu.CompilerParams(has_side_effects=pltpu.SideEffectType.DATAFLOW_SIDE_EFFECTING),
    )(land_thru, send_sems, recv_sems, after)


def _share(vec, name, after=None):
    follows = [] if after is None else [after]

    def body(vec_ref, *rest):
        out_ref, send_sems, recv_sems, local_sem = rest[len(follows):]
        me = _position()

        def copy(k, landed):
            peer = _peer(me, k)
            return pltpu.make_async_remote_copy(
                src_ref=vec_ref, dst_ref=out_ref.at[_slot(peer) if landed else _slot(me)],
                send_sem=send_sems.at[k - 1], recv_sem=recv_sems.at[k - 1], device_id=peer, device_id_type=MESH)

        mine = pltpu.make_async_copy(vec_ref, out_ref.at[_slot(me)], local_sem)
        mine.start()
        sent = [copy(k, False) for k in ALL_PEERS]
        for cp in sent:
            cp.start()
        for k in ALL_PEERS:
            copy(k, True).wait_recv()
        for cp in sent:
            cp.wait_send()
        mine.wait()

    return pl.pallas_call(
        body, name=name,
        in_specs=[ANY] * (1 + len(follows)), out_specs=ANY,
        out_shape=jax.ShapeDtypeStruct((N_DEV,) + vec.shape, vec.dtype),
        scratch_shapes=[pltpu.SemaphoreType.DMA((N_DEV - 1,)), pltpu.SemaphoreType.DMA((N_DEV - 1,)),
                        pltpu.SemaphoreType.DMA],
    )(vec, *follows)


def _sum_slots(parts):
    def body(p_ref, o_ref):
        acc = p_ref[0]
        for dev in range(1, N_DEV):
            acc = acc + p_ref[dev]
        o_ref[...] = acc

    return pl.pallas_call(body, name="sum_slots",
                          out_shape=jax.ShapeDtypeStruct(parts.shape[1:], F32))(parts)


PACK_ROWS = 8


def _packed_rows(size):
    return -(-size // (PACK_ROWS * LANE)) * PACK_ROWS


def _pack(arrs):
    def rows(a):
        flat = a.reshape(-1)
        return jnp.pad(flat, (0, _packed_rows(flat.shape[0]) * LANE - flat.shape[0])).reshape(-1, LANE)

    return jnp.concatenate([rows(a) for a in arrs], axis=0)


def _unpack(packed, shapes):
    out, at = [], 0
    for shp in shapes:
        size = 1
        for dim in shp:
            size *= dim
        nrows = _packed_rows(size)
        out.append(packed[at:at + nrows].reshape(-1)[:size].reshape(shp))
        at += nrows
    return out


def _layer_fwd(x, wt, wo, g_pre, g_post, wa_pad, b_alpha, g_gla, g_att, rb_pad, midway=None, h=None,
               g_pre_next=None, target=None):
    if h is None:
        h = _rms_fwd(x, g_pre)
    z = _matmul(h, wt, "nt", F32, *TILES["in_proj"], "in_proj", n_outer=True)
    y_gla, o_gla, *gla_kept = _gla_fwd(z, wa_pad, b_alpha, g_gla)
    y_att, o_att, probs = _att_fwd(z, rb_pad, g_att)
    token = None if midway is None else midway(y_att)
    y = _matmul_cols([y_gla, y_att], wo, F32, *TILES["out_proj"][:2], "out_proj", after=token)
    kept = (x, h, z, o_gla, gla_kept, o_att, probs, y_gla, y_att, y)
    if target is not None:
        return _post_loss(x, y, g_post, target), None, kept
    out, h_next = _post_fwd(x, y, g_post, g_pre_next)
    return out, h_next, kept


def _layer_bwd(dout, saved, wt, wo, g_pre, g_post, wa_pad, b_alpha, g_gla, g_att, rb_pad, on_dwo, on_dwt,
               post=None, below=None):
    x, h, z, o_gla, gla_kept, o_att, probs, y_gla, y_att, y = saved
    dy, dg_post = _post_bwd(dout, y, g_post) if post is None else post
    dwo = _matmul_rows([y_gla, y_att], dy, BF16, *TILES["out_proj_dw"][:2], "out_proj_dw")
    token = on_dwo(dwo)
    dycat = _matmul(dy, wo, "nt", F32, *TILES["out_proj_dx"], "out_proj_dx", n_outer=True, after=token)
    dq, dk, dv, dgg, dga, dwa, db, dg_gla = _gla_bwd(dycat, o_gla, z, wa_pad, g_gla, *gla_kept)
    daq, dak, dav, dag, drb, dg_att = _att_bwd(dycat, o_att, probs, z, g_att)
    tw, tn = TILES["in_proj_dw"][:2]
    dwt = (_matmul_rows([dq, dk, dv, dgg], h, BF16, tw, tn, "in_proj_dw_gla"),
           _matmul_rows([daq, dak, dav, dag], h, BF16, tw, tn, "in_proj_dw_att"),
           _matmul_rows([dga], h, BF16, LANE, tn, "in_proj_dw_gate"))
    token = on_dwt(dwt)
    dh = _matmul_cols([dq, dk, dv, dgg, daq, dak, dav, dag, dga], wt, F32, *TILES["in_proj_dx"][:2],
                      "in_proj_dx", after=token)
    dx, dg_pre, *post_below = _pre_bwd(dh, x, g_pre, dout, below)
    small = (dg_pre[0], dg_post[0], dwa[:GLA_RANK], db[0], dg_gla[0], dg_att[0], drb[:, 0, :N_REL])
    return dx, small, post_below or None


def kernel(x, w_in, w_out, g_pre, g_post, w_alpha, b_alpha, g_gla, g_att, rel_bias, loss_target, m_w_in, m_w_out, m_g_pre, m_g_post, m_w_alpha, m_b_alpha, m_g_gla, m_g_att, m_rel_bias, v_w_in, v_w_out, v_g_pre, v_g_post, v_w_alpha, v_b_alpha, v_g_gla, v_g_att, v_rel_bias):
    nl, d, cols = w_in.shape
    rows = w_out.shape[1]
    s = x.shape[1]
    x0 = x.reshape(s, d)
    tgt = loss_target.reshape(s, d)

    cols_first = lambda a: jnp.transpose(a, (2, 0, 1))
    w_c = cols_first(w_in)
    slab = _slab_rows(rows, cols)
    is_out = lax.broadcasted_iota(jnp.int32, (slab, d), 0) < rows

    def shard(l, zero=0.0):
        top = jnp.pad((w_out[l] + zero).astype(BF16), ((0, slab - rows), (0, 0)))
        rest = jnp.pad((w_c[:, l] + zero).astype(BF16), ((rows, slab - rows - cols), (0, 0)))
        return jnp.where(is_out, top, rest)

    first_fetch = _exchange_start(shard(0), x, GATHER_PEERS, False, "gather_start_0")
    began = first_fetch[4][0, 0]
    shards = [None] + [shard(l, began) for l in range(1, nl)]
    alpha = _pack([w_alpha]) + began
    wa_g = _share(alpha, "gather_alpha")
    wa_cols = w_alpha.shape[2]
    wa_full = wa_g.reshape(N_DEV, -1)[:, :nl * GLA_RANK * wa_cols].reshape(N_DEV, nl, GLA_RANK, wa_cols)
    wa_full = jnp.transpose(wa_full, (1, 2, 0, 3)).reshape(nl, GLA_RANK, GLA_KW)
    wa_pad = jnp.pad(wa_full, ((0, 0), (0, LANE - GLA_RANK), (0, 0)))
    rb_pad = jnp.pad(rel_bias, ((0, 0), (0, 0), (0, 3 * LANE - N_REL)))

    def layer_args(l, follows=None):
        gp = g_pre[l:l + 1] if follows is None else g_pre[l:l + 1] + follows[:1, :1]
        return (wts[l], wos[l], gp, g_post[l:l + 1], wa_pad[l], b_alpha[l:l + 1], g_gla[l:l + 1],
                g_att[l:l + 1], rb_pad[l])

    my = _slot(_position())

    def fetch(l, after):
        return _exchange_start(shards[l], after, GATHER_PEERS, False, f"gather_start_{l}")

    def relay(l, first_hop, after):
        own[l], land = _exchange_wait(first_hop[:4], after, GATHER_PEERS, False, f"gather_wait_{l}")
        return _relay_start(land, f"relay_start_{l}")

    def midway(l, y):
        flight["relay"] = relay(l + 1, flight["fetch"], y)
        if l + 2 >= nl:
            return flight["relay"][3]
        flight["fetch"] = fetch(l + 2, flight["relay"][2])
        return flight["fetch"][4]

    act, h_next, saved, wts, wos, flight, own = x0, None, [], [], [], {}, [None] * nl
    prepared = (wa_pad[0, :1, :1] + sum(sh[:1, :1].astype(F32) for sh in shards[1:]))
    flight["relay"] = relay(0, first_fetch, prepared)
    if nl > 1:
        flight["fetch"] = fetch(1, flight["relay"][2])
    for l in range(nl):
        land = _relay_wait(flight["relay"][:3], act, f"relay_wait_{l}")
        land = lax.dynamic_update_slice_in_dim(land, own[l][None], my, 0)
        wt_l, wo_l = _aligned_weight(land, rows, cols)
        wts.append(wt_l)
        wos.append(wo_l)
        act, h_next, sv = _layer_fwd(act, *layer_args(l, follows=first_fetch[4] if l == 0 else None),
                                     midway=functools.partial(midway, l) if l + 1 < nl else None, h=h_next,
                                     g_pre_next=g_pre[l + 1:l + 2] if l + 1 < nl else None,
                                     target=tgt if l + 1 == nl else None)
        saved.append(sv)
    dout, sq = act
    loss = lax.psum(sq[0, 0] * (0.5 / d), ("x", "y", "c"))

    smalls, pending_out, pending_in = [None] * nl, [None] * nl, [None] * nl

    def send_out(l, dwo):
        pending_out[l] = _exchange_start(dwo.reshape(N_DEV, rows, d), dwo[:1, :1], ALL_PEERS, True,
                                         f"scatter_out_start_{l}")
        return pending_out[l][4]

    def send_in(l, dwt):
        if l > 0:
            pending_in[l] = _exchange_start(_partial_slabs(dwt, cols), dwt[-1], ALL_PEERS, True,
                                            f"scatter_in_start_{l}")
            return pending_in[l][4]
        pair = _exchange_start(_partial_slabs(dwt, cols, by_core=True), dwt[-1], (1,), True,
                               "pair_start_0", slots=BY_CORE)
        by_core, from_sibling = _exchange_wait(pair[:4], pair[4], (1,), True, "pair_wait_0", slots=BY_CORE)
        pending_in[l] = _exchange_start(_pair_sum(by_core, from_sibling), dwt[-1], GATHER_PEERS[1:], True,
                                        "scatter_in_start_0", slots=BY_CHIP)
        return pending_in[l][4]

    post = None
    for l in reversed(range(nl)):
        below = (saved[l - 1][-1], g_post[l - 1:l]) if l > 0 else None
        dout, smalls[l], post = _layer_bwd(dout, saved[l], *layer_args(l), on_dwo=functools.partial(send_out, l),
                                           on_dwt=functools.partial(send_in, l), post=post, below=below)
    grad_x = dout.reshape(x.shape)

    def landed(started, after, name, ks=ALL_PEERS, slots=BY_DEVICE):
        partial, land = _exchange_wait(started[:4], after, ks, True, name, slots=slots)
        mine = slots[0](_position())
        return lax.dynamic_update_slice_in_dim(land, lax.dynamic_slice_in_dim(partial, mine, 1, 0), mine, 0)

    parts_out = [landed(pending_out[l], dout, f"scatter_out_wait_{l}") for l in range(nl)]
    g_w_out, d_w_out, m2_w_out, v2_w_out = _adam_sharded(parts_out, 0, w_out, m_w_out, v_w_out, "adam_w_out")
    names = 7
    small_stacked = [jnp.stack([smalls[l][i] for l in range(nl)]) for i in range(names)]
    shapes = [a.shape for a in small_stacked]
    gathered = _share(_pack(small_stacked), "gather_small_grads", after=d_w_out)
    g_pre_g, g_post_g, wa_g_full, b_g, gla_g, att_g, rb_g = _unpack(_sum_slots(gathered), shapes)
    wa_g_mine = lax.dynamic_slice_in_dim(wa_g_full, my * wa_cols, wa_cols, axis=2)
    grads = [g_pre_g, g_post_g, wa_g_mine, b_g, gla_g, att_g, rb_g]
    ws = [g_pre, g_post, w_alpha, b_alpha, g_gla, g_att, rel_bias]
    ms = [m_g_pre, m_g_post, m_w_alpha, m_b_alpha, m_g_gla, m_g_att, m_rel_bias]
    vs = [v_g_pre, v_g_post, v_w_alpha, v_b_alpha, v_g_gla, v_g_att, v_rel_bias]
    d_s, m2_s, v2_s = _adam_small(ws, grads, ms, vs)

    parts_in = [landed(pending_in[0], d_s[0], "scatter_in_wait_0", GATHER_PEERS[1:], BY_CHIP)]
    parts_in += [landed(pending_in[l], d_s[0], f"scatter_in_wait_{l}") for l in range(1, nl)]
    g_w_in, d_w_in, m2_w_in, v2_w_in = [
        jnp.transpose(a, (1, 2, 0))
        for a in _adam_columns(parts_in, 0, w_c, cols_first(m_w_in), cols_first(v_w_in))]

    def ordered(big_in, big_out, small):
        return [big_in, big_out] + list(small)

    return (loss, grad_x,
            *ordered(g_w_in, g_w_out, grads),
            *ordered(d_w_in, d_w_out, d_s),
            *ordered(m2_w_in, m2_w_out, m2_s),
            *ordered(v2_w_in, v2_w_out, v2_s))
```
